```python
import math
import jax, jax.numpy as jnp
from jax import lax
import numpy as np

D_MODEL = 1024
BATCH = 4
SEQ = 4096
DEPTH = 4
DEC_BATCH = 32
DEC_SEQ = 4
PAST_LEN = 8192
PAGE_SIZE = 128

N_MIXERS = 4
N_HEADS = 16
HEAD_DIM = D_MODEL // N_HEADS
N_KV_HEADS = 4
IDX_HEADS = 8
IDX_DIM = 64
TOPK_MAX = 256
Q_BLOCK = 128
N_BUCKETS = 32
MAX_DISTANCE = 128
Q_WIDTH = N_HEADS * HEAD_DIM
KV_WIDTH = N_KV_HEADS * HEAD_DIM
QI_WIDTH = IDX_HEADS * IDX_DIM
ATTN_SPLITS = (Q_WIDTH, Q_WIDTH + KV_WIDTH, Q_WIDTH + 2 * KV_WIDTH, Q_WIDTH + 2 * KV_WIDTH + QI_WIDTH, Q_WIDTH + 2 * KV_WIDTH + QI_WIDTH + IDX_DIM)
ATTN_IN = ATTN_SPLITS[-1] + IDX_HEADS
S5_GROUP = 16
S5_GROUPS = D_MODEL // S5_GROUP
S5_STATE = 64
SC_WIDTH = 3
SSD_INNER = 2 * D_MODEL
SSD_HEADDIM = 64
SSD_HEADS = SSD_INNER // SSD_HEADDIM
SSD_GROUPS = 4
SSD_STATE = 128
SSD_CONV = 4
SSD_CHUNK = 128
SSD_CONV_DIM = SSD_INNER + 2 * SSD_GROUPS * SSD_STATE
SSD_IN = SSD_INNER + SSD_CONV_DIM + SSD_HEADS
D_FF = 4 * D_MODEL
EPS = 1e-6

kernel_name = 'hybrid_dsa_s5_shortconv_ssd_step'


def _n_layers_of(m):
    return len(range(m, DEPTH, N_MIXERS))


def _rmsnorm(x, g):
    xf = x.astype(jnp.float32)
    y = xf * lax.rsqrt(jnp.mean(xf * xf, axis=-1, keepdims=True) + EPS)
    return (y * g.astype(jnp.float32)).astype(x.dtype)


def _adaln(c, w, b):
    return jnp.split(jax.nn.silu(c) @ w + b, 6, axis=-1)


def _modulate(h, shift, scale):
    return h * (1 + scale[:, None, :]) + shift[:, None, :]


def _mlp(h, w1, w2):
    return jnp.square(jax.nn.relu(h @ w1)) @ w2


def _causal_dwconv(u, buf, w):
    width, L = w.shape[0], u.shape[1]
    up = jnp.concatenate([buf.astype(u.dtype), u], axis=1)
    y = up[:, 0:L] * w[0]
    for k in range(1, width):
        y = y + up[:, k:k + L] * w[k]
    return y, up[:, L:]


def _t5_bucket(dist):
    n = jnp.maximum(dist, 0)
    max_exact = N_BUCKETS // 2
    nf = jnp.maximum(n, max_exact).astype(jnp.float32)
    large = max_exact + (jnp.log(nf / max_exact) / math.log(MAX_DISTANCE / max_exact) * (N_BUCKETS - max_exact)).astype(jnp.int32)
    large = jnp.minimum(large, N_BUCKETS - 1)
    return jnp.where(n < max_exact, n, large)


def _dsa_project(h, w_in):
    Bn, L, _ = h.shape
    q, k, v, qi, ki, wi = jnp.split(h @ w_in, list(ATTN_SPLITS), axis=-1)
    q = q.reshape(Bn, L, N_HEADS, HEAD_DIM)
    k = k.reshape(Bn, L, N_KV_HEADS, HEAD_DIM)
    v = v.reshape(Bn, L, N_KV_HEADS, HEAD_DIM)
    qi = qi.reshape(Bn, L, IDX_HEADS, IDX_DIM)
    return q, k, v, qi, ki, wi


def _dsa_attend(q, qi, wi, q_pos, kidx, gather_kv, topk, rel_bias):
    f32 = jnp.float32
    Bn, Lq = q.shape[:2]
    L = kidx.shape[1]
    dots = jnp.einsum('bqhd,bld->bqhl', qi.astype(f32), kidx.astype(f32)) * (IDX_DIM ** -0.5)
    score = jnp.einsum('bqh,bqhl->bql', wi.astype(f32) * (IDX_HEADS ** -0.5), jax.nn.relu(dots))
    admissible = jnp.arange(L, dtype=jnp.int32)[None, None, :] <= q_pos[None, :, None]
    score = jnp.where(admissible, score, -jnp.inf)
    _, sel = lax.top_k(score, topk)
    k_sel, v_sel = gather_kv(sel)
    dist = q_pos[None, :, None] - sel
    valid = dist >= 0
    r = N_HEADS // N_KV_HEADS
    bias = rel_bias[_t5_bucket(dist)].astype(f32).reshape(Bn, Lq, topk, N_KV_HEADS, r)
    bias = jnp.moveaxis(bias, 2, -1)
    qg = q.reshape(Bn, Lq, N_KV_HEADS, r, HEAD_DIM)
    logits = jnp.einsum('bqgrd,bqkgd->bqgrk', qg, k_sel).astype(f32) * (HEAD_DIM ** -0.5) + bias
    logits = jnp.where(valid[:, :, None, None, :], logits, -jnp.inf)
    p = jax.nn.softmax(logits, axis=-1).astype(v_sel.dtype)
    o = jnp.einsum('bqgrk,bqkgd->bqgrd', p, v_sel)
    return o.reshape(Bn, Lq, N_HEADS * HEAD_DIM)


def _dsa_prompt(h, w_in, w_out, rel_bias):
    Bn, L, _ = h.shape
    q, k, v, qi, ki, wi = _dsa_project(h, w_in)
    topk = min(TOPK_MAX, L // 4)
    nb = L // Q_BLOCK
    take = jax.vmap(lambda t, ix: t[ix])

    def gather_kv(sel):
        return take(k, sel), take(v, sel)

    def to_blocks(t):
        return jnp.moveaxis(t.reshape((Bn, nb, Q_BLOCK) + t.shape[2:]), 1, 0)

    pos = jnp.arange(L, dtype=jnp.int32).reshape(nb, Q_BLOCK)

    def block(args):
        qb, qib, wib, pb = args
        return _dsa_attend(qb, qib, wib, pb, ki, gather_kv, topk, rel_bias)

    o = lax.map(block, (to_blocks(q), to_blocks(qi), to_blocks(wi), pos))
    o = jnp.moveaxis(o, 0, 1).reshape(Bn, L, N_HEADS * HEAD_DIM)
    n_pg = L // PAGE_SIZE
    return o @ w_out, (k.reshape(Bn, n_pg, PAGE_SIZE, N_KV_HEADS, HEAD_DIM),
                       v.reshape(Bn, n_pg, PAGE_SIZE, N_KV_HEADS, HEAD_DIM),
                       ki.reshape(Bn, n_pg, PAGE_SIZE, IDX_DIM))


def _dsa_sample(h, cache_k, cache_v, cache_kidx, j, page_table, w_in, w_out, rel_bias):
    Bn, Lq, _ = h.shape
    past = page_table.shape[1] * PAGE_SIZE
    q, k, v, qi, ki, wi = _dsa_project(h, w_in)
    ki_past = cache_kidx[j, page_table].reshape(Bn, past, IDX_DIM)
    ki_all = jnp.concatenate([ki_past.astype(ki.dtype), ki], axis=1)
    topk = min(TOPK_MAX, (past + Lq) // 4)
    q_pos = past + jnp.arange(Lq, dtype=jnp.int32)
    take = jax.vmap(lambda t, ix: t[ix])

    def gather_kv(sel):
        p_old = jnp.minimum(sel, past - 1)
        phys = take(page_table, p_old // PAGE_SIZE)
        off = p_old % PAGE_SIZE
        p_new = jnp.clip(sel - past, 0, Lq - 1)
        is_new = (sel >= past)[..., None, None]
        k_sel = jnp.where(is_new, take(k, p_new), cache_k[j, phys, off])
        v_sel = jnp.where(is_new, take(v, p_new), cache_v[j, phys, off])
        return k_sel, v_sel

    o = _dsa_attend(q, qi, wi, q_pos, ki_all, gather_kv, topk, rel_bias)
    return o @ w_out, (k, v, ki)


def _cplx_combine(e1, e2):
    a1r, a1i, b1r, b1i = e1
    a2r, a2i, b2r, b2i = e2
    return (a1r * a2r - a1i * a2i, a1r * a2i + a1i * a2r,
            a2r * b1r - a2i * b1i + b2r, a2r * b1i + a2i * b1r + b2i)


def _s5(h, st_re, st_im, lam_re, lam_im, log_dt, b_re, b_im, c_re, c_im, d_skip, w_glu):
    f32 = jnp.float32
    Bn, L, _ = h.shape
    u = h.astype(f32).reshape(Bn, L, S5_GROUPS, S5_GROUP)
    lr, li = lam_re.astype(f32), lam_im.astype(f32)
    dt = jnp.exp(log_dt.astype(f32))[:, None]
    mag = jnp.exp(lr * dt)
    ab_re, ab_im = mag * jnp.cos(li * dt), mag * jnp.sin(li * dt)
    den = lr * lr + li * li
    nr = ab_re - 1.0
    f_re = (nr * lr + ab_im * li) / den
    f_im = (ab_im * lr - nr * li) / den
    br, bi = b_re.astype(f32), b_im.astype(f32)
    bb_re = f_re[..., None] * br - f_im[..., None] * bi
    bb_im = f_re[..., None] * bi + f_im[..., None] * br
    bu_re = jnp.einsum('blgc,gpc->blgp', u, bb_re)
    bu_im = jnp.einsum('blgc,gpc->blgp', u, bb_im)
    h0r, h0i = st_re.astype(f32), st_im.astype(f32)
    bu_re = bu_re.at[:, 0].add(ab_re * h0r - ab_im * h0i)
    bu_im = bu_im.at[:, 0].add(ab_re * h0i + ab_im * h0r)
    a_re = jnp.broadcast_to(ab_re, (1, L) + ab_re.shape)
    a_im = jnp.broadcast_to(ab_im, (1, L) + ab_im.shape)
    _, _, xr, xi = lax.associative_scan(_cplx_combine, (a_re, a_im, bu_re, bu_im), axis=1)
    y = jnp.einsum('blgp,gcp->blgc', xr, c_re.astype(f32)) - jnp.einsum('blgp,gcp->blgc', xi, c_im.astype(f32))
    y = y.reshape(Bn, L, D_MODEL) + d_skip.astype(f32) * h.astype(f32)
    z = jax.nn.gelu(y).astype(h.dtype)
    ga, gb = jnp.split(z @ w_glu, 2, axis=-1)
    return ga * jax.nn.sigmoid(gb), (xr[:, -1], xi[:, -1])


def _shortconv(h, buf, w_in, w_conv, w_out):
    gb, gc, xh = jnp.split(h @ w_in, 3, axis=-1)
    conv, new_buf = _causal_dwconv(gc * xh, buf, w_conv)
    return (gb * conv) @ w_out, new_buf


def _ssd_scan(x, dt, a, bm, cm, h0):
    Bn, L, G, R, P = x.shape
    N = bm.shape[-1]
    q = math.gcd(L, SSD_CHUNK)
    nc = L // q
    xc = x.reshape(Bn, nc, q, G, R, P)
    dtc = dt.reshape(Bn, nc, q, G, R)
    bc = bm.reshape(Bn, nc, q, G, N)
    cc = cm.reshape(Bn, nc, q, G, N)
    cum = jnp.cumsum(dtc * a, axis=2)
    seg = cum[:, :, :, None] - cum[:, :, None, :]
    causal = jnp.tril(jnp.ones((q, q), dtype=bool))
    decay = jnp.exp(jnp.where(causal[:, :, None, None], seg, -jnp.inf))
    dtx = xc * dtc[..., None]
    cb = jnp.einsum('bcqgn,bckgn->bcqkg', cc, bc)
    y_diag = jnp.einsum('bcqkg,bcqkgr,bckgrp->bcqgrp', cb, decay, dtx)
    decay_end = jnp.exp(cum[:, :, -1:] - cum)
    states = jnp.einsum('bckgn,bckgr,bckgrp->bcgrpn', bc, decay_end, dtx)
    chunk_decay = jnp.exp(cum[:, :, -1])

    def step(hprev, inp):
        s, dc = inp
        return hprev * dc[..., None, None] + s, hprev

    h_final, h_in = lax.scan(step, h0, (jnp.moveaxis(states, 1, 0), jnp.moveaxis(chunk_decay, 1, 0)))
    h_in = jnp.moveaxis(h_in, 0, 1)
    y_off = jnp.einsum('bcqgn,bcgrpn,bcqgr->bcqgrp', cc, h_in, jnp.exp(cum))
    return (y_diag + y_off).reshape(Bn, L, G, R, P), h_final


def _ssd(h, ssm_state, conv_buf, w_in, conv_w, conv_b, dt_bias, a_log, d_skip, norm_g, w_out):
    f32 = jnp.float32
    Bn, L, _ = h.shape
    G, R, P, N = SSD_GROUPS, SSD_HEADS // SSD_GROUPS, SSD_HEADDIM, SSD_STATE
    z, xbc, dt_raw = jnp.split(h @ w_in, [SSD_INNER, SSD_INNER + SSD_CONV_DIM], axis=-1)
    xbc, new_buf = _causal_dwconv(xbc, conv_buf, conv_w)
    xbc = jax.nn.silu(xbc + conv_b)
    xs, bm, cm = jnp.split(xbc.astype(f32), [SSD_INNER, SSD_INNER + G * N], axis=-1)
    xs = xs.reshape(Bn, L, G, R, P)
    bm = bm.reshape(Bn, L, G, N)
    cm = cm.reshape(Bn, L, G, N)
    dt = jax.nn.softplus(dt_raw.astype(f32) + dt_bias.astype(f32)).reshape(Bn, L, G, R)
    a = -jnp.exp(a_log.astype(f32)).reshape(G, R)
    y, h_final = _ssd_scan(xs, dt, a, bm, cm, ssm_state.astype(f32).reshape(Bn, G, R, P, N))
    y = y + d_skip.astype(f32).reshape(G, R)[..., None] * xs
    y = y.reshape(Bn, L, SSD_INNER) * jax.nn.silu(z.astype(f32))
    yg = y.reshape(Bn, L, G, SSD_INNER // G)
    yg = yg * lax.rsqrt(jnp.mean(yg * yg, axis=-1, keepdims=True) + EPS)
    y = (yg.reshape(Bn, L, SSD_INNER) * norm_g.astype(f32)).astype(h.dtype)
    return y @ w_out, (h_final.reshape(Bn, SSD_HEADS, P, N), new_buf)


def setup_inputs(seed: int = 0) -> dict:
    key = jax.random.key(seed)
    ks = iter(jax.random.split(key, 64))

    def nrm(shape, s=1.0):
        return s * jax.random.normal(next(ks), shape, jnp.float32)

    na, nb, nc, nd = (_n_layers_of(m) for m in range(N_MIXERS))
    n_pages = PAST_LEN // PAGE_SIZE
    n_used = DEC_BATCH * n_pages
    n_pool = n_used + (n_used + 3) // 4
    page_table = jax.random.permutation(next(ks), n_pool)[:n_used].reshape(DEC_BATCH, n_pages).astype(jnp.int32)
    dt_ssd = jnp.exp(jax.random.uniform(next(ks), (nd, SSD_HEADS), jnp.float32, math.log(1e-3), math.log(1e-1)))
    s5_n = jnp.arange(S5_STATE, dtype=jnp.float32)
    return {
        'x_prompt': nrm((BATCH, SEQ, D_MODEL)),
        'x_sample': nrm((DEC_BATCH, DEC_SEQ, D_MODEL)),
        'cache_k': nrm((na, n_pool, PAGE_SIZE, N_KV_HEADS, HEAD_DIM)),
        'cache_v': nrm((na, n_pool, PAGE_SIZE, N_KV_HEADS, HEAD_DIM)),
        'cache_kidx': nrm((na, n_pool, PAGE_SIZE, IDX_DIM)),
        'state_s5_re': nrm((nb, DEC_BATCH, S5_GROUPS, S5_STATE), 0.1),
        'state_s5_im': nrm((nb, DEC_BATCH, S5_GROUPS, S5_STATE), 0.1),
        'state_sconv': nrm((nc, DEC_BATCH, SC_WIDTH - 1, D_MODEL)),
        'state_ssd': nrm((nd, DEC_BATCH, SSD_HEADS, SSD_HEADDIM, SSD_STATE), 0.1),
        'state_ssd_conv': nrm((nd, DEC_BATCH, SSD_CONV - 1, SSD_CONV_DIM)),
        'page_table': page_table,
        'c_prompt': nrm((BATCH, D_MODEL)),
        'c_sample': nrm((DEC_BATCH, D_MODEL)),
        'rel_bias': nrm((N_BUCKETS, N_HEADS), 0.5),
        'ada_w': nrm((DEPTH, D_MODEL, 6 * D_MODEL), D_MODEL ** -0.5),
        'ada_b': nrm((DEPTH, 6 * D_MODEL), 0.01),
        'norm_mix': 1.0 + nrm((DEPTH, D_MODEL), 0.02),
        'norm_mlp': 1.0 + nrm((DEPTH, D_MODEL), 0.02),
        'norm_final': 1.0 + nrm((D_MODEL,), 0.02),
        'attn_w_in': nrm((na, D_MODEL, ATTN_IN), D_MODEL ** -0.5),
        'attn_w_out': nrm((na, Q_WIDTH, D_MODEL), Q_WIDTH ** -0.5),
        's5_lam_re': -0.5 + nrm((nb, S5_GROUPS, S5_STATE), 0.01),
        's5_lam_im': math.pi * s5_n + nrm((nb, S5_GROUPS, S5_STATE), 0.01),
        's5_log_dt': jax.random.uniform(next(ks), (nb, S5_GROUPS), jnp.float32, math.log(1e-3), math.log(1e-1)),
        's5_b_re': nrm((nb, S5_GROUPS, S5_STATE, S5_GROUP), (2 * S5_GROUP) ** -0.5),
        's5_b_im': nrm((nb, S5_GROUPS, S5_STATE, S5_GROUP), (2 * S5_GROUP) ** -0.5),
        's5_c_re': nrm((nb, S5_GROUPS, S5_GROUP, S5_STATE), 0.5),
        's5_c_im': nrm((nb, S5_GROUPS, S5_GROUP, S5_STATE), 0.5),
        's5_d': nrm((nb, D_MODEL)),
        's5_w_glu': nrm((nb, D_MODEL, 2 * D_MODEL), D_MODEL ** -0.5),
        'sc_w_in': nrm((nc, D_MODEL, 3 * D_MODEL), D_MODEL ** -0.5),
        'sc_w_conv': nrm((nc, SC_WIDTH, D_MODEL), SC_WIDTH ** -0.5),
        'sc_w_out': nrm((nc, D_MODEL, D_MODEL), D_MODEL ** -0.5),
        'ssd_w_in': nrm((nd, D_MODEL, SSD_IN), D_MODEL ** -0.5),
        'ssd_conv_w': nrm((nd, SSD_CONV, SSD_CONV_DIM), SSD_CONV ** -0.5),
        'ssd_conv_b': nrm((nd, SSD_CONV_DIM), 0.01),
        'ssd_dt_bias': dt_ssd + jnp.log(-jnp.expm1(-dt_ssd)),
        'ssd_a_log': jnp.log(jax.random.uniform(next(ks), (nd, SSD_HEADS), jnp.float32, 1.0, 16.0)),
        'ssd_d': 1.0 + nrm((nd, SSD_HEADS), 0.1),
        'ssd_norm': 1.0 + nrm((nd, SSD_INNER), 0.02),
        'ssd_w_out': nrm((nd, SSD_INNER, D_MODEL), SSD_INNER ** -0.5),
        'mlp_w1': nrm((DEPTH, D_MODEL, D_FF), D_MODEL ** -0.5),
        'mlp_w2': nrm((DEPTH, D_FF, D_MODEL), D_FF ** -0.5),
    }


def reference(x_prompt, x_sample, cache_k, cache_v, cache_kidx, state_s5_re, state_s5_im, state_sconv, state_ssd, state_ssd_conv, page_table, c_prompt, c_sample, rel_bias, ada_w, ada_b, norm_mix, norm_mlp, norm_final, attn_w_in, attn_w_out, s5_lam_re, s5_lam_im, s5_log_dt, s5_b_re, s5_b_im, s5_c_re, s5_c_im, s5_d, s5_w_glu, sc_w_in, sc_w_conv, sc_w_out, ssd_w_in, ssd_conv_w, ssd_conv_b, ssd_dt_bias, ssd_a_log, ssd_d, ssd_norm, ssd_w_out, mlp_w1, mlp_w2):
    f32 = jnp.float32
    xp, xs = x_prompt, x_sample
    bp = xp.shape[0]
    kp_l, vp_l, kip_l, ks_l, vs_l, kis_l = [], [], [], [], [], []
    s5p_re, s5p_im, s5s_re, s5s_im = [], [], [], []
    scp_l, scs_l = [], []
    ssdp_l, ssdcp_l, ssds_l, ssdcs_l = [], [], [], []
    for i in range(DEPTH):
        m, j = i % N_MIXERS, i // N_MIXERS
        mp = _adaln(c_prompt, ada_w[i], ada_b[i])
        ms = _adaln(c_sample, ada_w[i], ada_b[i])
        hp = _modulate(_rmsnorm(xp, norm_mix[i]), mp[0], mp[1])
        hs = _modulate(_rmsnorm(xs, norm_mix[i]), ms[0], ms[1])
        if m == 0:
            op, (a0, a1, a2) = _dsa_prompt(hp, attn_w_in[j], attn_w_out[j], rel_bias)
            osm, (b0, b1, b2) = _dsa_sample(hs, cache_k, cache_v, cache_kidx, j, page_table, attn_w_in[j], attn_w_out[j], rel_bias)
            kp_l.append(a0); vp_l.append(a1); kip_l.append(a2)
            ks_l.append(b0); vs_l.append(b1); kis_l.append(b2)
        elif m == 1:
            s5w = (s5_lam_re[j], s5_lam_im[j], s5_log_dt[j], s5_b_re[j], s5_b_im[j], s5_c_re[j], s5_c_im[j], s5_d[j], s5_w_glu[j])
            z0 = jnp.zeros((bp, S5_GROUPS, S5_STATE), f32)
            op, (a0, a1) = _s5(hp, z0, z0, *s5w)
            osm, (b0, b1) = _s5(hs, state_s5_re[j], state_s5_im[j], *s5w)
            s5p_re.append(a0); s5p_im.append(a1); s5s_re.append(b0); s5s_im.append(b1)
        elif m == 2:
            op, a0 = _shortconv(hp, jnp.zeros((bp, SC_WIDTH - 1, D_MODEL), hp.dtype), sc_w_in[j], sc_w_conv[j], sc_w_out[j])
            osm, b0 = _shortconv(hs, state_sconv[j], sc_w_in[j], sc_w_conv[j], sc_w_out[j])
            scp_l.append(a0); scs_l.append(b0)
        else:
            ssdw = (ssd_w_in[j], ssd_conv_w[j], ssd_conv_b[j], ssd_dt_bias[j], ssd_a_log[j], ssd_d[j], ssd_norm[j], ssd_w_out[j])
            h0 = jnp.zeros((bp, SSD_HEADS, SSD_HEADDIM, SSD_STATE), f32)
            cb0 = jnp.zeros((bp, SSD_CONV - 1, SSD_CONV_DIM), hp.dtype)
            op, (a0, a1) = _ssd(hp, h0, cb0, *ssdw)
            osm, (b0, b1) = _ssd(hs, state_ssd[j], state_ssd_conv[j], *ssdw)
            ssdp_l.append(a0); ssdcp_l.append(a1); ssds_l.append(b0); ssdcs_l.append(b1)
        xp = xp + (mp[2][:, None, :] * op).astype(xp.dtype)
        xs = xs + (ms[2][:, None, :] * osm).astype(xs.dtype)
        fp = _mlp(_modulate(_rmsnorm(xp, norm_mlp[i]), mp[3], mp[4]), mlp_w1[i], mlp_w2[i])
        fs = _mlp(_modulate(_rmsnorm(xs, norm_mlp[i]), ms[3], ms[4]), mlp_w1[i], mlp_w2[i])
        xp = xp + (mp[5][:, None, :] * fp).astype(xp.dtype)
        xs = xs + (ms[5][:, None, :] * fs).astype(xs.dtype)
    y_prompt = _rmsnorm(xp, norm_final)
    y_sample = _rmsnorm(xs, norm_final)
    st = jnp.stack
    return (y_prompt, y_sample, st(kp_l), st(vp_l), st(kip_l), st(ks_l), st(vs_l), st(kis_l), st(s5p_re), st(s5p_im), st(s5s_re), st(s5s_im), st(scp_l), st(scs_l), st(ssdp_l), st(ssdcp_l), st(ssds_l), st(ssdcs_l))
```

```python
import functools
import math

import jax
import jax.numpy as jnp
import numpy as np
from jax import lax
from jax.experimental import pallas as pl
from jax.experimental.pallas import tpu as pltpu

F32 = jnp.float32
BF16 = jnp.bfloat16
I32 = jnp.int32

EPS = 1e-6
N_HEADS = 16
HEAD_DIM = 64
N_KV_HEADS = 4
IDX_HEADS = 8
IDX_DIM = 64
TOPK_MAX = 256
N_BUCKETS = 32
MAX_DISTANCE = 128
S5_GROUP = 16
S5_STATE = 64
SSD_HEADDIM = 64
SSD_GROUPS = 4
SSD_STATE = 128
SSD_CHUNK = 128

LANE = 128
SUBLANE = 8
VMEM_LIMIT = 56 * 1024 * 1024
NEG_BIG = -1e30
INT_MIN = -2147483648


def _cparams(n_axes):
    return pltpu.CompilerParams(dimension_semantics=("arbitrary",) * n_axes,
                                vmem_limit_bytes=VMEM_LIMIT)


def _dot(a, b):
    return jnp.dot(a, b, preferred_element_type=F32)


def _dot_nt(a, b):
    return lax.dot_general(a, b, (((1,), (1,)), ((), ())), preferred_element_type=F32)


def _split(a):
    hi = a.astype(BF16)
    lo = (a - hi.astype(F32)).astype(BF16)
    return hi, lo


def _dot3(a, b):
    ah, al = _split(a)
    bh, bl = _split(b)
    return _dot(ah, bh) + (_dot(al, bh) + _dot(ah, bl))


def _rms_mod(x, g, shift, scale):
    y = x * lax.rsqrt(jnp.mean(x * x, axis=-1, keepdims=True) + EPS)
    return (y * g) * (1.0 + scale) + shift


def _sigmoid(x):
    return 1.0 / (1.0 + jnp.exp(-x))


def _silu(x):
    return x * _sigmoid(x)


def _row_block(arr, tm):
    n = arr.shape[-1]
    if arr.shape[1] == 1:
        return pl.BlockSpec((1, 1, n), lambda b, i: (b, 0, 0))
    return pl.BlockSpec((1, tm, n), lambda b, i: (b, i, 0))


def _const2(shape):
    return pl.BlockSpec(shape, lambda b, i: (0,) * len(shape))


def _ada_kernel(c_ref, w_ref, b_ref, o_ref):
    o_ref[0] = _dot3(_silu(c_ref[...]), w_ref[0]) + b_ref[0]


def _ada(c_all, ada_w, ada_b):
    depth, d, n = ada_w.shape
    rows = c_all.shape[0]
    tn = 1536
    return pl.pallas_call(
        _ada_kernel,
        grid=(depth, n // tn),
        in_specs=[pl.BlockSpec((rows, d), lambda l, j: (0, 0)),
                  pl.BlockSpec((1, d, tn), lambda l, j: (l, 0, j)),
                  pl.BlockSpec((1, 1, tn), lambda l, j: (l, 0, j))],
        out_specs=pl.BlockSpec((1, rows, tn), lambda l, j: (l, 0, j)),
        out_shape=jax.ShapeDtypeStruct((depth, rows, n), F32),
        compiler_params=_cparams(2),
        name="ada",
    )(c_all, ada_w, ada_b.reshape(depth, 1, n))


def _proj_kernel(x_ref, sh_ref, sc_ref, g_ref, w_ref, *o_refs, splits):
    h = _rms_mod(x_ref[0], g_ref[...], sh_ref[0], sc_ref[0]).astype(BF16)
    off = 0
    for o_ref, n in zip(o_refs, splits):
        o_ref[0] = _dot(h, w_ref[:, off:off + n])
        off += n


def _proj(x, shift, scale, g, w_bf16, splits, tm):
    bsz, seq, d = x.shape
    n = w_bf16.shape[1]
    assert sum(splits) == n and seq % tm == 0
    return pl.pallas_call(
        functools.partial(_proj_kernel, splits=tuple(splits)),
        grid=(bsz, seq // tm),
        in_specs=[pl.BlockSpec((1, tm, d), lambda b, i: (b, i, 0)),
                  _row_block(shift, tm), _row_block(scale, tm),
                  _const2((1, d)), _const2((d, n))],
        out_specs=[pl.BlockSpec((1, tm, s), lambda b, i: (b, i, 0)) for s in splits],
        out_shape=[jax.ShapeDtypeStruct((bsz, seq, s), F32) for s in splits],
        compiler_params=_cparams(2),
        name="proj",
    )(x, shift, scale, g.reshape(1, d), w_bf16)


def _attn_proj_kernel(x_ref, sh_ref, sc_ref, g_ref, w_ref, wi_ref,
                      q_ref, k_ref, v_ref, qi_ref, kw_ref, *, nq, nkv, nqi):
    h = _rms_mod(x_ref[0], g_ref[...], sh_ref[0], sc_ref[0])
    hb = h.astype(BF16)
    q_ref[0] = _dot(hb, w_ref[:, 0:nq])
    k_ref[0] = _dot(hb, w_ref[:, nq:nq + nkv])
    v_ref[0] = _dot(hb, w_ref[:, nq + nkv:nq + 2 * nkv])
    r = _dot3(h, wi_ref[...])
    qi_ref[0] = r[:, 0:nqi]
    kw_ref[0] = r[:, nqi:nqi + LANE]


def _attn_proj(x, shift, scale, g, w_qkv_bf16, w_idx, tm):
    bsz, seq, d = x.shape
    nq = N_HEADS * HEAD_DIM
    nkv = N_KV_HEADS * HEAD_DIM
    nqi = IDX_HEADS * IDX_DIM
    widths = (nq, nkv, nkv, nqi, LANE)
    return pl.pallas_call(
        functools.partial(_attn_proj_kernel, nq=nq, nkv=nkv, nqi=nqi),
        grid=(bsz, seq // tm),
        in_specs=[pl.BlockSpec((1, tm, d), lambda b, i: (b, i, 0)),
                  _row_block(shift, tm), _row_block(scale, tm),
                  _const2((1, d)), _const2(w_qkv_bf16.shape), _const2(w_idx.shape)],
        out_specs=[pl.BlockSpec((1, tm, s), lambda b, i: (b, i, 0)) for s in widths],
        out_shape=[jax.ShapeDtypeStruct((bsz, seq, s), F32) for s in widths],
        compiler_params=_cparams(2),
        name="attn_proj",
    )(x, shift, scale, g.reshape(1, d), w_qkv_bf16, w_idx)


def _outproj_kernel(o_ref, x_ref, gate_ref, w_ref, y_ref):
    y_ref[0] = x_ref[0] + gate_ref[0] * _dot(o_ref[0].astype(BF16), w_ref[...])


def _outproj(o, x, gate, w_bf16, tm):
    bsz, seq, d = x.shape
    k = o.shape[-1]
    return pl.pallas_call(
        _outproj_kernel,
        grid=(bsz, seq // tm),
        in_specs=[pl.BlockSpec((1, tm, k), lambda b, i: (b, i, 0)),
                  pl.BlockSpec((1, tm, d), lambda b, i: (b, i, 0)),
                  _row_block(gate, tm), _const2((k, d))],
        out_specs=pl.BlockSpec((1, tm, d), lambda b, i: (b, i, 0)),
        out_shape=jax.ShapeDtypeStruct((bsz, seq, d), F32),
        compiler_params=_cparams(2),
        name="outproj",
    )(o, x, gate, w_bf16)


def _mlp_kernel(x_ref, sh_ref, sc_ref, gate_ref, g_ref, w1_ref, w2_ref, gf_ref, y_ref,
                *, ff_chunk, final_norm):
    x = x_ref[0]
    h = _rms_mod(x, g_ref[...], sh_ref[0], sc_ref[0]).astype(BF16)
    dff = w1_ref.shape[1]
    acc = jnp.zeros(x.shape, F32)
    for c in range(dff // ff_chunk):
        a = jnp.maximum(_dot(h, w1_ref[:, c * ff_chunk:(c + 1) * ff_chunk]), 0.0)
        acc = acc + _dot((a * a).astype(BF16), w2_ref[c * ff_chunk:(c + 1) * ff_chunk, :])
    y = x + gate_ref[0] * acc
    if final_norm:
        y = (y * lax.rsqrt(jnp.mean(y * y, axis=-1, keepdims=True) + EPS)) * gf_ref[...]
    y_ref[0] = y


def _mlp(x, shift, scale, gate, g, w1_bf16, w2_bf16, g_final, final_norm, tm):
    bsz, seq, d = x.shape
    dff = w1_bf16.shape[1]
    return pl.pallas_call(
        functools.partial(_mlp_kernel, ff_chunk=1024, final_norm=final_norm),
        grid=(bsz, seq // tm),
        in_specs=[pl.BlockSpec((1, tm, d), lambda b, i: (b, i, 0)),
                  _row_block(shift, tm), _row_block(scale, tm), _row_block(gate, tm),
                  _const2((1, d)), _const2((d, dff)), _const2((dff, d)), _const2((1, d))],
        out_specs=pl.BlockSpec((1, tm, d), lambda b, i: (b, i, 0)),
        out_shape=jax.ShapeDtypeStruct((bsz, seq, d), F32),
        compiler_params=_cparams(2),
        name="mlp",
    )(x, shift, scale, gate, g.reshape(1, d), w1_bf16, w2_bf16, g_final.reshape(1, d))


S5_CH = 128
S5_BLK = (S5_CH // S5_GROUP) * S5_STATE


def _s5_disc_kernel(lr_ref, li_ref, ldt_ref, br_ref, bi_ref, ar_ref, ai_ref, bbr_ref, bbi_ref):
    lr, li = lr_ref[...], li_ref[...]
    dt = jnp.exp(ldt_ref[...])
    mag = jnp.exp(lr * dt)
    ab_re, ab_im = mag * jnp.cos(li * dt), mag * jnp.sin(li * dt)
    den = lr * lr + li * li
    nr = ab_re - 1.0
    f_re = (nr * lr + ab_im * li) / den
    f_im = (ab_im * lr - nr * li) / den
    ar_ref[...] = ab_re
    ai_ref[...] = ab_im
    for c in range(br_ref.shape[0]):
        br, bi = br_ref[c], bi_ref[c]
        bbr_ref[c] = f_re * br - f_im * bi
        bbi_ref[c] = f_re * bi + f_im * br


def _s5_weights(lam_re, lam_im, log_dt, b_re, b_im, c_re, c_im):
    g, p = lam_re.shape
    gc = b_re.shape[-1]
    brt = jnp.moveaxis(b_re, 2, 0)
    bit = jnp.moveaxis(b_im, 2, 0)
    ar, ai, bbr, bbi = pl.pallas_call(
        _s5_disc_kernel,
        out_shape=[jax.ShapeDtypeStruct((g, p), F32), jax.ShapeDtypeStruct((g, p), F32),
                   jax.ShapeDtypeStruct((gc, g, p), F32), jax.ShapeDtypeStruct((gc, g, p), F32)],
        name="s5_disc",
    )(lam_re, lam_im, log_dt.reshape(g, 1), brt, bit)
    nblk = (g * gc) // S5_CH
    gpb = g // nblk
    eye = jnp.eye(gpb, dtype=F32)

    def bd_in(bb):
        t = jnp.moveaxis(bb, 0, 1).reshape(nblk, gpb, gc, p)
        return jnp.einsum('ngcp,gh->ngchp', t, eye).reshape(nblk, gpb * gc, gpb * p)

    def bd_out(cc):
        t = cc.reshape(nblk, gpb, gc, p)
        return jnp.einsum('ngcp,gh->ngphc', t, eye).reshape(nblk, gpb * p, gpb * gc)

    wb = jnp.concatenate([bd_in(bbr), bd_in(bbi)], axis=-1).astype(BF16)
    wc = jnp.concatenate([bd_out(c_re), bd_out(c_im)], axis=1).astype(BF16)
    return ar.reshape(nblk, 1, gpb * p), ai.reshape(nblk, 1, gpb * p), wb, wc


def _gelu_tanh(y):
    return 0.5 * y * (1.0 + jnp.tanh(math.sqrt(2.0 / math.pi) * (y + 0.044715 * (y * y * y))))


def _cmul(ar, ai, xr, xi):
    return ar * xr - ai * xi, ar * xi + ai * xr


def _s5_prompt_kernel(x_ref, sh_ref, sc_ref, g_ref, d_ref, ar_ref, ai_ref, wb_ref, wc_ref,
                      perm_ref, unperm_ref, z_ref, fr_ref, fi_ref, u_s, br_s, bi_s, st_r, st_i,
                      *, nseg, seg):
    i = pl.program_id(1)
    nblk = wb_ref.shape[0]
    blk = ar_ref.shape[-1]

    @pl.when(i == 0)
    def _():
        st_r[...] = jnp.zeros(st_r.shape, F32)
        st_i[...] = jnp.zeros(st_i.shape, F32)

    u_s[...] = _dot_sel_lhs(perm_ref[...], _rms_mod(x_ref[0], g_ref[...], sh_ref[0], sc_ref[0]))

    for c in range(nblk):
        lo, hi = c * S5_CH, (c + 1) * S5_CH
        uc = u_s[:, lo:hi]
        bu = _dot(uc.astype(BF16), wb_ref[c])
        br_s[...] = bu[:, :blk]
        bi_s[...] = bu[:, blk:]
        ar = jnp.broadcast_to(ar_ref[c], (nseg, blk))
        ai = jnp.broadcast_to(ai_ref[c], (nseg, blk))

        def local(j, carry):
            xr, xi = carry
            r0 = pl.multiple_of(j * nseg, nseg)
            pr, pi = _cmul(ar, ai, xr, xi)
            nr = pr + br_s[pl.ds(r0, nseg), :]
            ni = pi + bi_s[pl.ds(r0, nseg), :]
            br_s[pl.ds(r0, nseg), :] = nr
            bi_s[pl.ds(r0, nseg), :] = ni
            return nr, ni

        zero = jnp.zeros((nseg, blk), F32)
        fr, fi = lax.fori_loop(0, seg, local, (zero, zero))

        pr, pi = ar_ref[c], ai_ref[c]
        for _ in range(int(math.log2(seg))):
            pr, pi = _cmul(pr, pi, pr, pi)
        cr, ci = st_r[c], st_i[c]
        rows_r, rows_i = [], []
        for s in range(nseg):
            rows_r.append(cr)
            rows_i.append(ci)
            tr, ti = _cmul(pr, pi, cr, ci)
            cr, ci = tr + fr[s:s + 1], ti + fi[s:s + 1]
        st_r[c] = cr
        st_i[c] = ci
        fr_ref[0, c] = cr
        fi_ref[0, c] = ci
        dr, di = _cmul(ar, ai, jnp.concatenate(rows_r, axis=0), jnp.concatenate(rows_i, axis=0))

        def fix(j, carry):
            dr, di = carry
            r0 = pl.multiple_of(j * nseg, nseg)
            br_s[pl.ds(r0, nseg), :] = br_s[pl.ds(r0, nseg), :] + dr
            bi_s[pl.ds(r0, nseg), :] = bi_s[pl.ds(r0, nseg), :] + di
            return _cmul(ar, ai, dr, di)

        lax.fori_loop(0, seg, fix, (dr, di))

        y = (_dot(br_s[...].astype(BF16), wc_ref[c, :blk, :])
             - _dot(bi_s[...].astype(BF16), wc_ref[c, blk:, :]))
        y = y + d_ref[:, lo:hi] * uc
        u_s[:, lo:hi] = _gelu_tanh(y)

    z_ref[0] = _dot_sel_lhs(unperm_ref[...], u_s[...])


def _s5_prompt(x, shift, scale, g, d_skip, ar, ai, wb, wc):
    bsz, seq, d = x.shape
    nblk, _, blk = ar.shape
    nseg, seg = SUBLANE, 32
    tm = nseg * seg
    assert seq % tm == 0
    perm = np.zeros((tm, tm), np.float32)
    for s in range(nseg):
        for j in range(seg):
            perm[j * nseg + s, s * seg + j] = 1.0
    unperm = jnp.asarray(perm.T, BF16)
    perm = jnp.asarray(perm, BF16)
    z, fr, fi = pl.pallas_call(
        functools.partial(_s5_prompt_kernel, nseg=nseg, seg=seg),
        grid=(bsz, seq // tm),
        in_specs=[pl.BlockSpec((1, tm, d), lambda b, i: (b, i, 0)),
                  _row_block(shift, tm), _row_block(scale, tm),
                  _const2((1, d)), _const2((1, d)),
                  _const2(ar.shape), _const2(ai.shape), _const2(wb.shape), _const2(wc.shape),
                  _const2((tm, tm)), _const2((tm, tm))],
        out_specs=[pl.BlockSpec((1, tm, d), lambda b, i: (b, i, 0)),
                   pl.BlockSpec((1, nblk, 1, blk), lambda b, i: (b, 0, 0, 0)),
                   pl.BlockSpec((1, nblk, 1, blk), lambda b, i: (b, 0, 0, 0))],
        out_shape=[jax.ShapeDtypeStruct((bsz, seq, d), F32),
                   jax.ShapeDtypeStruct((bsz, nblk, 1, blk), F32),
                   jax.ShapeDtypeStruct((bsz, nblk, 1, blk), F32)],
        scratch_shapes=[pltpu.VMEM((tm, d), F32), pltpu.VMEM((tm, blk), F32), pltpu.VMEM((tm, blk), F32),
                        pltpu.VMEM((nblk, 1, blk), F32), pltpu.VMEM((nblk, 1, blk), F32)],
        compiler_params=_cparams(2),
        name="s5_prompt",
    )(x, shift, scale, g.reshape(1, d), d_skip.reshape(1, d), ar, ai, wb, wc, perm, unperm)
    return z, fr, fi


def _s5_sample_kernel(x_ref, sh_ref, sc_ref, g_ref, d_ref, ar_ref, ai_ref, wb_ref, wc_ref,
                      s0r_ref, s0i_ref, z_ref, fr_ref, fi_ref, *, nb, steps):
    nblk = wb_ref.shape[0]
    blk = ar_ref.shape[-1]
    u = _rms_mod(x_ref[...], g_ref[...], sh_ref[...], sc_ref[...])
    for c in range(nblk):
        lo, hi = c * S5_CH, (c + 1) * S5_CH
        uc = u[:, lo:hi]
        bu = _dot(uc.astype(BF16), wb_ref[c])
        ar = jnp.broadcast_to(ar_ref[c], (nb, blk))
        ai = jnp.broadcast_to(ai_ref[c], (nb, blk))
        xr, xi = s0r_ref[:, c * blk:(c + 1) * blk], s0i_ref[:, c * blk:(c + 1) * blk]
        xrs, xis = [], []
        for t in range(steps):
            pr, pi = _cmul(ar, ai, xr, xi)
            xr = pr + bu[t * nb:(t + 1) * nb, :blk]
            xi = pi + bu[t * nb:(t + 1) * nb, blk:]
            xrs.append(xr)
            xis.append(xi)
        fr_ref[:, c * blk:(c + 1) * blk] = xr
        fi_ref[:, c * blk:(c + 1) * blk] = xi
        y = (_dot(jnp.concatenate(xrs, axis=0).astype(BF16), wc_ref[c, :blk, :])
             - _dot(jnp.concatenate(xis, axis=0).astype(BF16), wc_ref[c, blk:, :]))
        y = y + d_ref[:, lo:hi] * uc
        z_ref[:, lo:hi] = _gelu_tanh(y)


def _s5_sample(x_tb, shift_tb, scale_tb, g, d_skip, ar, ai, wb, wc, s0r, s0i, nb, steps):
    rows, d = x_tb.shape
    nblk, _, blk = ar.shape
    return pl.pallas_call(
        functools.partial(_s5_sample_kernel, nb=nb, steps=steps),
        out_shape=[jax.ShapeDtypeStruct((rows, d), F32),
                   jax.ShapeDtypeStruct((nb, nblk * blk), F32),
                   jax.ShapeDtypeStruct((nb, nblk * blk), F32)],
        compiler_params=pltpu.CompilerParams(vmem_limit_bytes=VMEM_LIMIT),
        name="s5_sample",
    )(x_tb, shift_tb, scale_tb, g.reshape(1, d), d_skip.reshape(1, d), ar, ai, wb, wc, s0r, s0i)


def _glu_kernel(z_ref, x_ref, gate_ref, w_ref, y_ref):
    d = x_ref.shape[-1]
    zb = z_ref[0].astype(BF16)
    ga = _dot(zb, w_ref[:, :d])
    gb = _dot(zb, w_ref[:, d:])
    y_ref[0] = x_ref[0] + gate_ref[0] * (ga * _sigmoid(gb))


def _glu(z, x, gate, w_bf16, tm):
    bsz, seq, d = x.shape
    return pl.pallas_call(
        _glu_kernel,
        grid=(bsz, seq // tm),
        in_specs=[pl.BlockSpec((1, tm, d), lambda b, i: (b, i, 0)),
                  pl.BlockSpec((1, tm, d), lambda b, i: (b, i, 0)),
                  _row_block(gate, tm), _const2(w_bf16.shape)],
        out_specs=pl.BlockSpec((1, tm, d), lambda b, i: (b, i, 0)),
        out_shape=jax.ShapeDtypeStruct((bsz, seq, d), F32),
        compiler_params=_cparams(2),
        name="glu",
    )(z, x, gate, w_bf16)


def _shift_rows(u, prev, k):
    if k == 0:
        return u
    rolled = pltpu.roll(u, k, 0)
    head = pltpu.roll(prev, k, 0)
    row = lax.broadcasted_iota(I32, (SUBLANE, u.shape[1]), 0)
    first = jnp.where(row < k, head, rolled[:SUBLANE])
    return jnp.concatenate([first, rolled[SUBLANE:]], axis=0)


def _sconv_prompt_kernel(p_ref, x_ref, gate_ref, buf_ref, wc_ref, wo_ref, y_ref, nb_ref, prev_s,
                         *, width):
    i = pl.program_id(1)
    d = x_ref.shape[-1]

    @pl.when(i == 0)
    def _():
        prev_s[...] = jnp.zeros(prev_s.shape, F32)
        prev_s[SUBLANE - (width - 1):, :] = buf_ref[0]

    p = p_ref[0]
    gb, gc, xh = p[:, :d], p[:, d:2 * d], p[:, 2 * d:]
    u = gc * xh
    prev = prev_s[...]
    conv = u * wc_ref[width - 1:width, :]
    for k in range(1, width):
        conv = conv + _shift_rows(u, prev, k) * wc_ref[width - 1 - k:width - k, :]
    prev_s[...] = u[u.shape[0] - SUBLANE:, :]
    nb_ref[0] = u[u.shape[0] - (width - 1):, :]
    y_ref[0] = x_ref[0] + gate_ref[0] * _dot((gb * conv).astype(BF16), wo_ref[...])


def _sconv_prompt(p, x, gate, buf, w_conv, w_out_bf16, tm):
    bsz, seq, d = x.shape
    width = w_conv.shape[0]
    return pl.pallas_call(
        functools.partial(_sconv_prompt_kernel, width=width),
        grid=(bsz, seq // tm),
        in_specs=[pl.BlockSpec((1, tm, 3 * d), lambda b, i: (b, i, 0)),
                  pl.BlockSpec((1, tm, d), lambda b, i: (b, i, 0)),
                  _row_block(gate, tm),
                  pl.BlockSpec((1, width - 1, d), lambda b, i: (b, 0, 0)),
                  _const2((width, d)), _const2((d, d))],
        out_specs=[pl.BlockSpec((1, tm, d), lambda b, i: (b, i, 0)),
                   pl.BlockSpec((1, width - 1, d), lambda b, i: (b, 0, 0))],
        out_shape=[jax.ShapeDtypeStruct((bsz, seq, d), F32),
                   jax.ShapeDtypeStruct((bsz, width - 1, d), F32)],
        scratch_shapes=[pltpu.VMEM((SUBLANE, d), F32)],
        compiler_params=_cparams(2),
        name="sconv_prompt",
    )(p, x, gate, buf, w_conv, w_out_bf16)


def _sconv_sample_kernel(p_ref, x_ref, gate_ref, buf_ref, wc_ref, wo_ref, y_ref, nb_ref,
                         *, width, nb, steps):
    d = x_ref.shape[-1]
    p = p_ref[...]
    gb, gc, xh = p[:, :d], p[:, d:2 * d], p[:, 2 * d:]
    u = gc * xh
    ext = jnp.concatenate([buf_ref[...], u], axis=0)
    conv = ext[0:steps * nb] * wc_ref[0:1, :]
    for k in range(1, width):
        conv = conv + ext[k * nb:(k + steps) * nb] * wc_ref[k:k + 1, :]
    nb_ref[...] = ext[steps * nb:]
    y_ref[...] = x_ref[...] + gate_ref[...] * _dot((gb * conv).astype(BF16), wo_ref[...])


def _sconv_sample(p_tb, x_tb, gate_tb, buf_kb, w_conv, w_out_bf16, nb, steps):
    rows, d = x_tb.shape
    width = w_conv.shape[0]
    return pl.pallas_call(
        functools.partial(_sconv_sample_kernel, width=width, nb=nb, steps=steps),
        out_shape=[jax.ShapeDtypeStruct((rows, d), F32),
                   jax.ShapeDtypeStruct(((width - 1) * nb, d), F32)],
        compiler_params=pltpu.CompilerParams(vmem_limit_bytes=VMEM_LIMIT),
        name="sconv_sample",
    )(p_tb, x_tb, gate_tb, buf_kb, w_conv, w_out_bf16)


def _sortable(s):
    bits = lax.bitcast_convert_type(s + 0.0, I32)
    return bits ^ ((bits >> 31) & 0x7FFFFFFF)


def _idx_lhs(qi, nrows):
    parts = []
    zero = jnp.zeros((nrows, IDX_DIM), BF16)
    for h in range(IDX_HEADS):
        ah, al = _split(qi[:, h * IDX_DIM:(h + 1) * IDX_DIM])
        parts.append(jnp.concatenate([ah, al, ah, zero], axis=1))
    return jnp.concatenate(parts, axis=0)


def _idx_rhs(ki):
    kh, kl = _split(ki)
    return jnp.concatenate([kh, kh, kl, jnp.zeros(kh.shape, BF16)], axis=1)


def _idx_score(dts, wcols, nrows):
    s = wcols[0] * jnp.maximum(dts[0:nrows], 0.0)
    for h in range(1, IDX_HEADS):
        s = s + wcols[h] * jnp.maximum(dts[h * nrows:(h + 1) * nrows], 0.0)
    return s


def _idx_wcols(kw):
    sc = (IDX_HEADS ** -0.5) * (IDX_DIM ** -0.5)
    return [kw[:, IDX_DIM + h:IDX_DIM + h + 1] * sc for h in range(IDX_HEADS)]


def _topk_select(count, topk, idx_bits):
    def bit_step(it, tu):
        mask = jnp.left_shift(jnp.int32(1), 31 - it)
        cand_u = tu | mask
        cand = cand_u ^ INT_MIN
        cnt = count(lambda c, key, idx: jnp.where(key >= cand, 1.0, 0.0))
        return jnp.where(cnt >= topk, cand_u, tu)

    rows = count(lambda c, key, idx: jnp.zeros(key.shape, F32)).shape[0]
    tu = lax.fori_loop(0, 32, bit_step, jnp.zeros((rows, 1), I32))
    t = tu ^ INT_MIN
    n_gt = count(lambda c, key, idx: jnp.where(key > t, 1.0, 0.0))
    n_ge = count(lambda c, key, idx: jnp.where(key >= t, 1.0, 0.0))
    need = topk - n_gt
    tied = jnp.max(jnp.where(t != INT_MIN, n_ge - topk, 0.0)) > 0.0

    def tie_search(_):
        def idx_step(it, j):
            cand = j | jnp.left_shift(jnp.int32(1), idx_bits - 1 - it)
            cnt = count(lambda c, key, idx: jnp.where((key == t) & (idx < cand), 1.0, 0.0))
            return jnp.where(cnt < need, cand, j)
        return lax.fori_loop(0, idx_bits, idx_step, jnp.zeros((rows, 1), I32))

    j = lax.cond(tied, tie_search, lambda _: jnp.full((rows, 1), 2 ** idx_bits, I32), 0)
    return t, j


def _dsa_prompt_kernel(q_ref, qi_ref, kwq_ref, k_ref, v_ref, kw_ref, bias_ref, o_ref,
                       kcat_s, kb_s, vb_s, key_s, madd_s, lg_s, mx_s, sm_s, acc_s,
                       *, tq, topk, idx_bits):
    i = pl.program_id(1)
    nk = i + 1
    seq = k_ref.shape[1]
    r = N_HEADS // N_KV_HEADS
    blk = 512 if seq % 512 == 0 else tq

    @pl.when(i == 0)
    def _():
        for s in range(seq // blk):
            rows = slice(s * blk, (s + 1) * blk)
            kcat_s[rows, :] = _idx_rhs(kw_ref[0, rows, 0:IDX_DIM])
            kb_s[rows, :] = k_ref[0, rows, :].astype(BF16)
            vb_s[rows, :] = v_ref[0, rows, :].astype(BF16)

    lhs = _idx_lhs(qi_ref[0], tq)
    wcols = _idx_wcols(kwq_ref[0])
    qrow = lax.broadcasted_iota(I32, (tq, tq), 0)
    kcol = lax.broadcasted_iota(I32, (tq, tq), 1)
    qpos = i * tq + qrow

    def score_chunk(c, _):
        r0 = pl.multiple_of(c * tq, tq)
        dts = _dot_nt(lhs, kcat_s[pl.ds(r0, tq), :])
        key = _sortable(_idx_score(dts, wcols, tq))
        key_s[c] = jnp.where(c * tq + kcol <= qpos, key, INT_MIN)
        return 0

    lax.fori_loop(0, nk, score_chunk, 0)

    def count(fn):
        def body(c, acc):
            return acc + fn(c, key_s[c], c * tq + kcol)
        acc = lax.fori_loop(0, nk, body, jnp.zeros((tq, tq), F32))
        return jnp.sum(acc, axis=1, keepdims=True)

    t, j = _topk_select(count, float(topk), idx_bits)

    def mask_chunk(c, _):
        key = key_s[c]
        sel = (key > t) | ((key == t) & (c * tq + kcol <= j))
        madd_s[c] = jnp.where(sel & (key != INT_MIN), 0.0, NEG_BIG)
        return 0

    lax.fori_loop(0, nk, mask_chunk, 0)

    q = q_ref[0] * (HEAD_DIM ** -0.5)
    for g in range(N_KV_HEADS):
        lanes = slice(g * HEAD_DIM, (g + 1) * HEAD_DIM)
        qg = jnp.concatenate(
            [q[:, (g * r + u) * HEAD_DIM:(g * r + u + 1) * HEAD_DIM] for u in range(r)], axis=0).astype(BF16)
        mx_s[...] = jnp.full(mx_s.shape, NEG_BIG, F32)

        def logits_chunk(c, mode):
            r0 = pl.multiple_of(c * tq, tq)
            lg = _dot_nt(qg, kb_s[pl.ds(r0, tq), lanes])
            m = madd_s[c]
            if mode == 1:
                m = [m + jnp.where(kcol > qrow, bias_ref[g * r + u], 0.0) for u in range(r)]
            elif mode == 2:
                m = [m + bias_ref[g * r + u] for u in range(r)]
            else:
                m = [m] * r
            lg = lg + jnp.concatenate(m, axis=0)
            lg_s[c] = lg
            mx_s[...] = jnp.maximum(mx_s[...], lg)

        def far(c, _):
            logits_chunk(c, 0)
            return 0

        lax.fori_loop(0, jnp.maximum(nk - 2, 0), far, 0)

        @pl.when(i >= 1)
        def _():
            logits_chunk(nk - 2, 1)

        logits_chunk(nk - 1, 2)

        mrow = jnp.max(mx_s[...], axis=1, keepdims=True)
        sm_s[...] = jnp.zeros(sm_s.shape, F32)
        acc_s[...] = jnp.zeros(acc_s.shape, F32)

        def pv_chunk(c, _):
            r0 = pl.multiple_of(c * tq, tq)
            p = jnp.exp(lg_s[c] - mrow)
            sm_s[...] = sm_s[...] + p
            acc_s[...] = acc_s[...] + _dot(p.astype(BF16), vb_s[pl.ds(r0, tq), lanes])
            return 0

        lax.fori_loop(0, nk, pv_chunk, 0)
        o = acc_s[...] / jnp.sum(sm_s[...], axis=1, keepdims=True)
        for u in range(r):
            h = g * r + u
            o_ref[0, :, h * HEAD_DIM:(h + 1) * HEAD_DIM] = o[u * tq:(u + 1) * tq, :]


def _t5_bucket(dist):
    n = jnp.maximum(dist, 0)
    max_exact = N_BUCKETS // 2
    nf = jnp.maximum(n, max_exact).astype(F32)
    large = max_exact + (jnp.log(nf / max_exact) / math.log(MAX_DISTANCE / max_exact)
                         * (N_BUCKETS - max_exact)).astype(I32)
    large = jnp.minimum(large, N_BUCKETS - 1)
    return jnp.where(n < max_exact, n, large)


def _bias_table(rel_bias, n):
    tab = rel_bias[_t5_bucket(jnp.arange(n, dtype=I32))]
    return (tab - rel_bias[N_BUCKETS - 1][None, :]).T


def _dsa_prompt(q, qi, kw, k, v, rel_bias, tq):
    bsz, seq, _ = q.shape
    topk = min(TOPK_MAX, seq // 4)
    nkc = seq // tq
    r = N_HEADS // N_KV_HEADS
    assert tq == MAX_DISTANCE and seq % tq == 0
    tab = _bias_table(rel_bias, tq)
    dmod = (np.arange(tq)[:, None] - np.arange(tq)[None, :]) % tq
    bias = tab[:, dmod]
    nq, nkv = N_HEADS * HEAD_DIM, N_KV_HEADS * HEAD_DIM
    return pl.pallas_call(
        functools.partial(_dsa_prompt_kernel, tq=tq, topk=topk, idx_bits=int(math.log2(seq))),
        grid=(bsz, nkc),
        in_specs=[pl.BlockSpec((1, tq, nq), lambda b, i: (b, i, 0)),
                  pl.BlockSpec((1, tq, IDX_HEADS * IDX_DIM), lambda b, i: (b, i, 0)),
                  pl.BlockSpec((1, tq, LANE), lambda b, i: (b, i, 0)),
                  pl.BlockSpec((1, seq, nkv), lambda b, i: (b, 0, 0)),
                  pl.BlockSpec((1, seq, nkv), lambda b, i: (b, 0, 0)),
                  pl.BlockSpec((1, seq, LANE), lambda b, i: (b, 0, 0)),
                  _const2((N_HEADS, tq, tq))],
        out_specs=pl.BlockSpec((1, tq, nq), lambda b, i: (b, i, 0)),
        out_shape=jax.ShapeDtypeStruct((bsz, seq, nq), F32),
        scratch_shapes=[pltpu.VMEM((seq, 4 * IDX_DIM), BF16),
                        pltpu.VMEM((seq, nkv), BF16), pltpu.VMEM((seq, nkv), BF16),
                        pltpu.VMEM((nkc, tq, tq), I32), pltpu.VMEM((nkc, tq, tq), F32),
                        pltpu.VMEM((nkc, r * tq, tq), F32),
                        pltpu.VMEM((r * tq, tq), F32), pltpu.VMEM((r * tq, tq), F32),
                        pltpu.VMEM((r * tq, HEAD_DIM), F32)],
        compiler_params=_cparams(2),
        name="dsa_prompt",
    )(q, qi, kw, k, v, kw, bias)


QP = SUBLANE
PAGES_PER_STEP = 8


def _page_specs(shape4, layer, pps):
    def spec(u):
        return pl.BlockSpec((1, 1) + tuple(shape4[2:]),
                            lambda b, g, pt: (layer, pt[b, g * pps + u], 0, 0))
    return [spec(u) for u in range(pps)]


def _dsa_sample_score_kernel(pt_ref, qi_ref, kw_ref, *rest, pps):
    page_refs, o_ref = rest[:pps], rest[pps]
    lhs = _idx_lhs(qi_ref[0], QP)
    wcols = _idx_wcols(kw_ref[0])
    for u in range(pps):
        dts = _dot_nt(lhs, _idx_rhs(page_refs[u][0, 0]))
        o_ref[0, u] = _idx_score(dts, wcols, QP)


def _dsa_sample_scores(page_table, qi, kw, cache_kidx, layer, pps):
    nb, n_pages = page_table.shape
    page = cache_kidx.shape[2]
    grid_spec = pltpu.PrefetchScalarGridSpec(
        num_scalar_prefetch=1,
        grid=(nb, n_pages // pps),
        in_specs=[pl.BlockSpec((1, QP, qi.shape[-1]), lambda b, g, pt: (b, 0, 0)),
                  pl.BlockSpec((1, QP, LANE), lambda b, g, pt: (b, 0, 0))]
                 + _page_specs(cache_kidx.shape, layer, pps),
        out_specs=pl.BlockSpec((1, pps, QP, page), lambda b, g, pt: (b, g, 0, 0)))
    return pl.pallas_call(
        functools.partial(_dsa_sample_score_kernel, pps=pps),
        grid_spec=grid_spec,
        out_shape=jax.ShapeDtypeStruct((nb, n_pages, QP, page), F32),
        compiler_params=_cparams(2),
        name="dsa_sample_scores",
    )(page_table, qi, kw, *([cache_kidx] * pps))


def _dsa_sample_attend_kernel(pt_ref, q_ref, qi_ref, kw_ref, kn_ref, vn_ref, sc_ref,
                              blast_ref, bnew_ref, *rest, pps, n_pages, steps, topk, idx_bits):
    k_refs, v_refs, o_ref = rest[:pps], rest[pps:2 * pps], rest[2 * pps]
    key_s, t_s, j_s, m_s, l_s, acc_s = rest[2 * pps + 1:]
    g = pl.program_id(1)
    last = pl.num_programs(1) - 1
    page = sc_ref.shape[-1]
    r = N_HEADS // N_KV_HEADS
    nkv = N_KV_HEADS * HEAD_DIM
    lane = lax.broadcasted_iota(I32, (QP, page), 1)
    row = lax.broadcasted_iota(I32, (QP, page), 0)

    @pl.when(g == 0)
    def _():
        def conv(c, _):
            key_s[c] = _sortable(sc_ref[0, c])
            return 0

        lax.fori_loop(0, n_pages, conv, 0)
        kw = kw_ref[0]
        knew = jnp.concatenate([kw[:, 0:IDX_DIM], jnp.zeros((page - QP, IDX_DIM), F32)], axis=0)
        dts = _dot_nt(_idx_lhs(qi_ref[0], QP), _idx_rhs(knew))
        s_new = _idx_score(dts, _idx_wcols(kw), QP)
        key_s[n_pages] = jnp.where((lane <= row) & (lane < steps), _sortable(s_new), INT_MIN)

        def count(fn):
            def body(c, acc):
                return acc + fn(c, key_s[c], c * page + lane)
            acc = lax.fori_loop(0, n_pages + 1, body, jnp.zeros((QP, page), F32))
            return jnp.sum(acc, axis=1, keepdims=True)

        t, j = _topk_select(count, float(topk), idx_bits)
        t_s[...] = t
        j_s[...] = j
        m_s[...] = jnp.full(m_s.shape, NEG_BIG, F32)
        l_s[...] = jnp.zeros(l_s.shape, F32)
        acc_s[...] = jnp.zeros(acc_s.shape, F32)

    t, j = t_s[...], j_s[...]
    q = q_ref[0] * (HEAD_DIM ** -0.5)
    zero = jnp.zeros((QP, HEAD_DIM), F32)
    parts = []
    for h in range(N_HEADS):
        gh = h // r
        parts.append(jnp.concatenate(
            [zero] * gh + [q[:, h * HEAD_DIM:(h + 1) * HEAD_DIM]] + [zero] * (N_KV_HEADS - 1 - gh), axis=1))
    lhsq = jnp.concatenate(parts, axis=0).astype(BF16)

    def mask_tile(c):
        key = key_s[c]
        sel = (key > t) | ((key == t) & (c * page + lane <= j))
        return jnp.where(sel & (key != INT_MIN), 0.0, NEG_BIG)

    def online(lg, vmat):
        m_old = m_s[...]
        m_new = jnp.maximum(m_old, jnp.max(lg, axis=1, keepdims=True))
        p = jnp.exp(lg - m_new)
        alpha = jnp.exp(m_old - m_new)
        l_s[...] = alpha * l_s[...] + jnp.sum(p, axis=1, keepdims=True)
        acc_s[...] = alpha * acc_s[...] + _dot(p.astype(BF16), vmat)
        m_s[...] = m_new

    kp = jnp.concatenate([k_refs[u][0, 0].astype(BF16) for u in range(pps)], axis=0)
    vp = jnp.concatenate([v_refs[u][0, 0].astype(BF16) for u in range(pps)], axis=0)
    madd = jnp.concatenate([mask_tile(g * pps + u) for u in range(pps)], axis=1)
    is_last = jnp.where(g == last, 1.0, 0.0)
    lg = _dot_nt(lhsq, kp) + jnp.concatenate([madd] * N_HEADS, axis=0) + blast_ref[...] * is_last
    online(lg, vp)

    @pl.when(g == last)
    def _():
        pad = jnp.zeros((page - QP, nkv), F32)
        knp = jnp.concatenate([kn_ref[0], pad], axis=0).astype(BF16)
        vnp = jnp.concatenate([vn_ref[0], pad], axis=0).astype(BF16)
        lgn = _dot_nt(lhsq, knp) + jnp.concatenate([mask_tile(n_pages)] * N_HEADS, axis=0) + bnew_ref[...]
        online(lgn, vnp)
        o = acc_s[...] / l_s[...]
        for h in range(N_HEADS):
            gh = h // r
            o_ref[0, :, h * HEAD_DIM:(h + 1) * HEAD_DIM] = o[h * QP:(h + 1) * QP, gh * HEAD_DIM:(gh + 1) * HEAD_DIM]


def _dsa_sample(page_table, q, qi, kw, kn, vn, scores, cache_k, cache_v, rel_bias, layer, steps, pps):
    nb, n_pages = page_table.shape
    page = cache_k.shape[2]
    past = n_pages * page
    nq, nkv = N_HEADS * HEAD_DIM, N_KV_HEADS * HEAD_DIM
    topk = min(TOPK_MAX, (past + steps) // 4)
    n_steps = n_pages // pps
    tab = _bias_table(rel_bias, 2 * page)
    qq = np.arange(QP)[:, None]
    off = np.arange(page)[None, :]
    blast = tab[:, page + qq - off].reshape(N_HEADS * QP, page)
    blast = jnp.concatenate([jnp.zeros((N_HEADS * QP, (pps - 1) * page), F32), blast], axis=1)
    bnew = tab[:, np.maximum(qq - off, 0)].reshape(N_HEADS * QP, page)
    ck = cache_k.reshape(cache_k.shape[:3] + (nkv,))
    cv = cache_v.reshape(cache_v.shape[:3] + (nkv,))
    bmap = lambda b, g, pt: (b, 0, 0)
    grid_spec = pltpu.PrefetchScalarGridSpec(
        num_scalar_prefetch=1,
        grid=(nb, n_steps),
        in_specs=[pl.BlockSpec((1, QP, nq), bmap),
                  pl.BlockSpec((1, QP, qi.shape[-1]), bmap),
                  pl.BlockSpec((1, QP, LANE), bmap),
                  pl.BlockSpec((1, QP, nkv), bmap),
                  pl.BlockSpec((1, QP, nkv), bmap),
                  pl.BlockSpec((1, n_pages, QP, page), lambda b, g, pt: (b, 0, 0, 0)),
                  pl.BlockSpec(blast.shape, lambda b, g, pt: (0, 0)),
                  pl.BlockSpec(bnew.shape, lambda b, g, pt: (0, 0))]
                 + _page_specs(ck.shape, layer, pps) + _page_specs(cv.shape, layer, pps),
        out_specs=pl.BlockSpec((1, QP, nq), bmap),
        scratch_shapes=[pltpu.VMEM((n_pages + 1, QP, page), I32),
                        pltpu.VMEM((QP, 1), I32), pltpu.VMEM((QP, 1), I32),
                        pltpu.VMEM((N_HEADS * QP, 1), F32), pltpu.VMEM((N_HEADS * QP, 1), F32),
                        pltpu.VMEM((N_HEADS * QP, nkv), F32)])
    return pl.pallas_call(
        functools.partial(_dsa_sample_attend_kernel, pps=pps, n_pages=n_pages, steps=steps, topk=topk,
                          idx_bits=int(math.ceil(math.log2(past + page)))),
        grid_spec=grid_spec,
        out_shape=jax.ShapeDtypeStruct((nb, QP, nq), F32),
        compiler_params=_cparams(2),
        name="dsa_sample_attend",
    )(page_table, q, qi, kw, kn, vn, scores, blast, bnew, *([ck] * pps), *([cv] * pps))


SSD_COLS = 512


def _softplus(x):
    return jnp.maximum(x, 0.0) + jnp.log1p(jnp.exp(-jnp.abs(x)))


def _split3(a):
    p1 = a.astype(BF16)
    r1 = a - p1.astype(F32)
    p2 = r1.astype(BF16)
    return p1, p2, (r1 - p2.astype(F32)).astype(BF16)


def _dot_sel_rhs(a, e):
    p1, p2, p3 = _split3(a)
    return _dot(p1, e) + (_dot(p2, e) + _dot(p3, e))


def _dot_sel_lhs(e, a):
    p1, p2, p3 = _split3(a)
    return _dot(e, p1) + (_dot(e, p2) + _dot(e, p3))


def _ssd_prompt_kernel(z_ref, xbc_ref, dtr_ref, cw_ref, cb_ref, dtb_ref, alog_ref, dexp_ref, ng_ref,
                       e_ref, y_ref, nbuf_ref, hf_ref, prev_s, xc_s, y_s, h_s, *, width, nh, inner):
    i = pl.program_id(1)
    last = pl.num_programs(1) - 1
    qn, cdim = xbc_ref.shape[1], xbc_ref.shape[2]
    ngrp, ns, hd = SSD_GROUPS, SSD_STATE, SSD_HEADDIM
    hpg = nh // ngrp

    @pl.when(i == 0)
    def _():
        prev_s[...] = jnp.zeros(prev_s.shape, F32)
        h_s[...] = jnp.zeros(h_s.shape, F32)

    for cb in range(cdim // SSD_COLS):
        cols = slice(cb * SSD_COLS, (cb + 1) * SSD_COLS)
        xr = xbc_ref[0, :, cols]
        prev = prev_s[:, cols]
        conv = xr * cw_ref[width - 1:width, cols]
        for k in range(1, width):
            conv = conv + _shift_rows(xr, prev, k) * cw_ref[width - 1 - k:width - k, cols]
        xc_s[:, cols] = _silu(conv + cb_ref[:, cols])
        prev_s[:, cols] = xr[qn - SUBLANE:, :]
        nbuf_ref[0, :, cols] = xr[qn - (width - 1):, :]

    hl = lax.broadcasted_iota(I32, (qn, LANE), 1)
    dt = jnp.where(hl < nh, _softplus(dtr_ref[0] + dtb_ref[...]), 0.0)
    dta = dt * (-jnp.exp(alog_ref[...]))
    qrow = lax.broadcasted_iota(I32, (qn, qn), 0)
    kcol = lax.broadcasted_iota(I32, (qn, qn), 1)
    causal = kcol <= qrow
    cum = _dot_sel_lhs(jnp.where(causal, 1.0, 0.0).astype(BF16), dta)
    cum_t = cum.T
    cum_last = cum[qn - 1:qn, :]
    ecum = jnp.exp(cum)
    dtw = dt * jnp.exp(cum_last - cum)
    e_last = jnp.exp(cum_last)
    lane_p = lax.broadcasted_iota(I32, (qn, LANE), 1)
    row_p = lax.broadcasted_iota(I32, (LANE, ns), 0)

    for g in range(ngrp):
        gl = slice(g * SSD_COLS, (g + 1) * SSD_COLS)
        e_g = e_ref[:, gl]
        xs_g = xc_s[:, gl]
        dtx_g = xs_g * _dot_sel_rhs(dt, e_g)
        dtxw_g = xs_g * _dot_sel_rhs(dtw, e_g)
        ecx_g = _dot_sel_rhs(ecum, e_g)
        bm = xc_s[:, inner + g * ns:inner + (g + 1) * ns].astype(BF16)
        cm = xc_s[:, inner + (ngrp + g) * ns:inner + (ngrp + g + 1) * ns].astype(BF16)
        cbm = _dot_nt(cm, bm)
        for pi in range(hpg // 2):
            ha = g * hpg + 2 * pi
            pls = slice(pi * LANE, (pi + 1) * LANE)
            lanes = slice(ha * hd, ha * hd + LANE)
            dtx_p = dtx_g[:, pls].astype(BF16)
            yds = []
            for h in (ha, ha + 1):
                seg = cum[:, h:h + 1] - cum_t[h:h + 1, :]
                dec = jnp.exp(jnp.where(causal, seg, -jnp.inf))
                yds.append(_dot((cbm * dec).astype(BF16), dtx_p))
            yd = jnp.where(lane_p < hd, yds[0], yds[1])
            hp = h_s[ha // 2]
            yo = _dot_nt(cm, hp.astype(BF16)) * ecx_g[:, pls]
            y_s[:, lanes] = yd + yo + dexp_ref[:, lanes] * xs_g[:, pls]
            s_new = _dot(dtxw_g[:, pls].T.astype(BF16), bm)
            cd = jnp.where(row_p < hd, e_last[:, ha:ha + 1], e_last[:, ha + 1:ha + 2])
            h_s[ha // 2] = hp * cd + s_new

    for g in range(ngrp):
        gl = slice(g * SSD_COLS, (g + 1) * SSD_COLS)
        yg = y_s[:, gl] * _silu(z_ref[0, :, gl])
        ms = jnp.mean(yg * yg, axis=-1, keepdims=True)
        y_ref[0, :, gl] = (yg * lax.rsqrt(ms + EPS)) * ng_ref[:, gl]

    @pl.when(i == last)
    def _():
        hf_ref[0] = h_s[...]


def _head_expand(nh, hd):
    e = np.zeros((LANE, nh * hd), np.float32)
    for h in range(nh):
        e[h, h * hd:(h + 1) * hd] = 1.0
    return jnp.asarray(e, BF16)


def _pad_lanes(v, n=LANE):
    v = v.reshape(1, -1)
    return jnp.pad(v, ((0, 0), (0, n - v.shape[1])))


def _ssd_prompt(z, xbc, dtr, conv_w, conv_b, dt_bias, a_log, d_skip, norm_g):
    bsz, seq, inner = z.shape
    cdim = xbc.shape[-1]
    nh = dt_bias.shape[0]
    width = conv_w.shape[0]
    qn = math.gcd(seq, SSD_CHUNK)
    assert qn == SSD_CHUNK and inner == SSD_GROUPS * SSD_COLS and cdim % SSD_COLS == 0
    dexp = jnp.repeat(d_skip, SSD_HEADDIM).reshape(1, inner)
    blk = lambda n: pl.BlockSpec((1, qn, n), lambda b, i: (b, i, 0))
    y, nbuf, hf = pl.pallas_call(
        functools.partial(_ssd_prompt_kernel, width=width, nh=nh, inner=inner),
        grid=(bsz, seq // qn),
        in_specs=[blk(inner), blk(cdim), blk(LANE),
                  _const2((width, cdim)), _const2((1, cdim)), _const2((1, LANE)), _const2((1, LANE)),
                  _const2((1, inner)), _const2((1, inner)), _const2((LANE, inner))],
        out_specs=[blk(inner),
                   pl.BlockSpec((1, width - 1, cdim), lambda b, i: (b, 0, 0)),
                   pl.BlockSpec((1, nh // 2, LANE, SSD_STATE), lambda b, i: (b, 0, 0, 0))],
        out_shape=[jax.ShapeDtypeStruct((bsz, seq, inner), F32),
                   jax.ShapeDtypeStruct((bsz, width - 1, cdim), F32),
                   jax.ShapeDtypeStruct((bsz, nh // 2, LANE, SSD_STATE), F32)],
        scratch_shapes=[pltpu.VMEM((SUBLANE, cdim), F32), pltpu.VMEM((qn, cdim), F32),
                        pltpu.VMEM((qn, inner), F32), pltpu.VMEM((nh // 2, LANE, SSD_STATE), F32)],
        compiler_params=_cparams(2),
        name="ssd_prompt",
    )(z, xbc, dtr, conv_w, conv_b.reshape(1, cdim), _pad_lanes(dt_bias), _pad_lanes(a_log),
      dexp, norm_g.reshape(1, inner), _head_expand(nh, SSD_HEADDIM))
    return y, nbuf, hf.reshape(bsz, nh, SSD_HEADDIM, SSD_STATE)


def _ssd_prep_sample_kernel(xbc_ref, buf_ref, dtr_ref, cw_ref, cb_ref, dtb_ref, xc_ref, dt_ref, nbuf_ref,
                            *, width, nb, steps, nh):
    cdim = xbc_ref.shape[1]
    for cb in range(cdim // SSD_COLS):
        cols = slice(cb * SSD_COLS, (cb + 1) * SSD_COLS)
        ext = jnp.concatenate([buf_ref[:, cols], xbc_ref[:, cols]], axis=0)
        conv = ext[0:steps * nb] * cw_ref[0:1, cols]
        for k in range(1, width):
            conv = conv + ext[k * nb:(k + steps) * nb] * cw_ref[k:k + 1, cols]
        xc_ref[:, cols] = _silu(conv + cb_ref[:, cols])
        nbuf_ref[:, cols] = ext[steps * nb:]
    hl = lax.broadcasted_iota(I32, dtr_ref.shape, 1)
    dt_ref[...] = jnp.where(hl < nh, _softplus(dtr_ref[...] + dtb_ref[...]), 0.0)


def _ssd_prep_sample(xbc_tb, buf_kb, dtr_tb, conv_w, conv_b, dt_bias, nb, steps):
    rows, cdim = xbc_tb.shape
    width = conv_w.shape[0]
    return pl.pallas_call(
        functools.partial(_ssd_prep_sample_kernel, width=width, nb=nb, steps=steps, nh=dt_bias.shape[0]),
        out_shape=[jax.ShapeDtypeStruct((rows, cdim), F32),
                   jax.ShapeDtypeStruct((rows, LANE), F32),
                   jax.ShapeDtypeStruct(((width - 1) * nb, cdim), F32)],
        compiler_params=pltpu.CompilerParams(vmem_limit_bytes=VMEM_LIMIT),
        name="ssd_prep_sample",
    )(xbc_tb, buf_kb, dtr_tb, conv_w, conv_b.reshape(1, cdim), _pad_lanes(dt_bias))


def _ssd_scan_sample_kernel(xc_ref, dt_ref, z_ref, h0_ref, alog_ref, dexp_ref, ng_ref, e_ref,
                            y_ref, hf_ref, *, steps, inner):
    ngrp, ns = SSD_GROUPS, SSD_STATE
    xc, dt = xc_ref[0], dt_ref[0]
    row = lax.broadcasted_iota(I32, (QP, LANE), 0)
    cum = dt * (-jnp.exp(alog_ref[...]))
    for sft in (1, 2, 4):
        cum = cum + jnp.where(row >= sft, pltpu.roll(cum, sft, 0), 0.0)
    cum_last = cum[QP - 1:QP, :]
    parts = [dt, jnp.exp(cum), dt * jnp.exp(cum_last - cum), jnp.broadcast_to(jnp.exp(cum_last), (QP, LANE))]
    for s in range(steps):
        parts.append(jnp.where(row >= s, jnp.exp(cum - cum[s:s + 1, :]), 0.0))
    stack = jnp.concatenate(parts, axis=0)

    for g in range(ngrp):
        gl = slice(g * SSD_COLS, (g + 1) * SSD_COLS)
        ex = _dot_sel_rhs(stack, e_ref[:, gl])
        dt_x, ecum_x, dtw_x, el_x = (ex[k * QP:(k + 1) * QP] for k in range(4))
        xs_g = xc[:, gl]
        dtx = xs_g * dt_x
        bm = xc[:, inner + g * ns:inner + (g + 1) * ns]
        cm = xc[:, inner + (ngrp + g) * ns:inner + (ngrp + g + 1) * ns].astype(BF16)
        bmp = jnp.concatenate([bm, jnp.zeros((LANE - QP, ns), F32)], axis=0).astype(BF16)
        cbm = _dot_nt(cm, bmp)
        yd = jnp.zeros((QP, SSD_COLS), F32)
        for s in range(steps):
            yd = yd + (ex[(4 + s) * QP:(5 + s) * QP] * cbm[:, s:s + 1]) * dtx[s:s + 1, :]
        h0g = h0_ref[0, gl, :]
        yo = _dot_nt(cm, h0g.astype(BF16)) * ecum_x
        y = (yd + yo + dexp_ref[:, gl] * xs_g) * _silu(z_ref[0, :, gl])
        ms = jnp.mean(y * y, axis=-1, keepdims=True)
        y_ref[0, :, gl] = (y * lax.rsqrt(ms + EPS)) * ng_ref[:, gl]
        tm = jnp.concatenate([xs_g * dtw_x, el_x[0:1], jnp.zeros((LANE - QP - 1, SSD_COLS), F32)], axis=0)
        tt = tm.T
        hf_ref[0, gl, :] = h0g * tt[:, QP:QP + 1] + _dot(tt.astype(BF16), bmp)


def _ssd_scan_sample(xc, dt, z, h0, a_log, d_skip, norm_g, steps):
    nb, _, cdim = xc.shape
    inner = z.shape[-1]
    nh = a_log.shape[0]
    rows_h = nh * SSD_HEADDIM
    dexp = jnp.repeat(d_skip, SSD_HEADDIM).reshape(1, inner)
    one = lambda shape: pl.BlockSpec(shape, lambda b: (0,) * len(shape))
    return pl.pallas_call(
        functools.partial(_ssd_scan_sample_kernel, steps=steps, inner=inner),
        grid=(nb,),
        in_specs=[pl.BlockSpec((1, QP, cdim), lambda b: (b, 0, 0)),
                  pl.BlockSpec((1, QP, LANE), lambda b: (b, 0, 0)),
                  pl.BlockSpec((1, QP, inner), lambda b: (b, 0, 0)),
                  pl.BlockSpec((1, rows_h, SSD_STATE), lambda b: (b, 0, 0)),
                  one((1, LANE)), one((1, inner)), one((1, inner)), one((LANE, inner))],
        out_specs=[pl.BlockSpec((1, QP, inner), lambda b: (b, 0, 0)),
                   pl.BlockSpec((1, rows_h, SSD_STATE), lambda b: (b, 0, 0))],
        out_shape=[jax.ShapeDtypeStruct((nb, QP, inner), F32),
                   jax.ShapeDtypeStruct((nb, rows_h, SSD_STATE), F32)],
        compiler_params=_cparams(1),
        name="ssd_scan_sample",
    )(xc, dt, z, h0, _pad_lanes(a_log), dexp, norm_g.reshape(1, inner), _head_expand(nh, SSD_HEADDIM))


TM = 256


def _to_steps(a, nb, steps):
    n = a.shape[-1]
    return a.reshape(nb, steps, n).transpose(1, 0, 2).reshape(steps * nb, n)


def _to_batch(a, nb, steps):
    n = a.shape[-1]
    return a.reshape(steps, nb, n).transpose(1, 0, 2).reshape(1, nb * steps, n)


def _pad_q(a, nb, steps):
    n = a.shape[-1]
    return jnp.pad(a.reshape(nb, steps, n), ((0, 0), (0, QP - steps), (0, 0)))


def kernel(x_prompt, x_sample, cache_k, cache_v, cache_kidx, state_s5_re, state_s5_im, state_sconv, state_ssd, state_ssd_conv, page_table, c_prompt, c_sample, rel_bias, ada_w, ada_b, norm_mix, norm_mlp, norm_final, attn_w_in, attn_w_out, s5_lam_re, s5_lam_im, s5_log_dt, s5_b_re, s5_b_im, s5_c_re, s5_c_im, s5_d, s5_w_glu, sc_w_in, sc_w_conv, sc_w_out, ssd_w_in, ssd_conv_w, ssd_conv_b, ssd_dt_bias, ssd_a_log, ssd_d, ssd_norm, ssd_w_out, mlp_w1, mlp_w2):
    bp, seq, d = x_prompt.shape
    nb, steps, _ = x_sample.shape
    depth = ada_w.shape[0]
    n_mixers = 4
    rs = nb * steps
    tm = min(TM, seq)

    rows = bp + nb
    c_all = jnp.pad(jnp.concatenate([c_prompt, c_sample], axis=0), ((0, (-rows) % SUBLANE), (0, 0)))
    ada = _ada(c_all, ada_w, ada_b)

    xp = x_prompt
    xs = x_sample.reshape(1, rs, d)
    outs = {name: [] for name in ("kp", "vp", "kip", "ks", "vs", "kis", "s5pr", "s5pi", "s5sr", "s5si",
                                  "scp", "scs", "ssdp", "ssdcp", "ssds", "ssdcs")}
    nq, nkv = N_HEADS * HEAD_DIM, N_KV_HEADS * HEAD_DIM
    for i in range(depth):
        m, j = i % n_mixers, i // n_mixers
        mp = [ada[i, :bp, k * d:(k + 1) * d].reshape(bp, 1, d) for k in range(6)]
        ms = [jnp.repeat(ada[i, bp:bp + nb, k * d:(k + 1) * d], steps, axis=0).reshape(1, rs, d)
              for k in range(6)]
        g_mix = norm_mix[i]
        if m == 0:
            w_in = attn_w_in[j]
            w_qkv = w_in[:, :nq + 2 * nkv].astype(BF16)
            n_idx = IDX_HEADS * IDX_DIM + LANE
            w_idx = jnp.pad(w_in[:, nq + 2 * nkv:], ((0, 0), (0, n_idx - (w_in.shape[1] - nq - 2 * nkv))))
            w_out = attn_w_out[j].astype(BF16)
            q, k, v, qi, kw = _attn_proj(xp, mp[0], mp[1], g_mix, w_qkv, w_idx, tm)
            o = _dsa_prompt(q, qi, kw, k, v, rel_bias, MAX_DISTANCE)
            xp = _outproj(o, xp, mp[2], w_out, tm)
            page = cache_k.shape[2]
            outs["kp"].append(k.reshape(bp, seq // page, page, N_KV_HEADS, HEAD_DIM))
            outs["vp"].append(v.reshape(bp, seq // page, page, N_KV_HEADS, HEAD_DIM))
            outs["kip"].append(kw[..., :IDX_DIM].reshape(bp, seq // page, page, IDX_DIM))
            q, k, v, qi, kw = _attn_proj(xs, ms[0], ms[1], g_mix, w_qkv, w_idx, rs)
            pps = math.gcd(PAGES_PER_STEP, page_table.shape[1])
            qi_p, kw_p = _pad_q(qi, nb, steps), _pad_q(kw, nb, steps)
            scores = _dsa_sample_scores(page_table, qi_p, kw_p, cache_kidx, j, pps)
            o = _dsa_sample(page_table, _pad_q(q, nb, steps), qi_p, kw_p, _pad_q(k, nb, steps),
                            _pad_q(v, nb, steps), scores, cache_k, cache_v, rel_bias, j, steps, pps)
            xs = _outproj(o[:, :steps].reshape(1, rs, nq), xs, ms[2], w_out, rs)
            outs["ks"].append(k.reshape(nb, steps, N_KV_HEADS, HEAD_DIM))
            outs["vs"].append(v.reshape(nb, steps, N_KV_HEADS, HEAD_DIM))
            outs["kis"].append(kw[..., :IDX_DIM].reshape(nb, steps, IDX_DIM))
        elif m == 1:
            ar, ai, wb, wc = _s5_weights(s5_lam_re[j], s5_lam_im[j], s5_log_dt[j], s5_b_re[j], s5_b_im[j],
                                         s5_c_re[j], s5_c_im[j])
            w_glu = s5_w_glu[j].astype(BF16)
            grp, nst = s5_lam_re.shape[1:]
            z, fr, fi = _s5_prompt(xp, mp[0], mp[1], g_mix, s5_d[j], ar, ai, wb, wc)
            xp = _glu(z, xp, mp[2], w_glu, tm)
            outs["s5pr"].append(fr.reshape(bp, grp, nst))
            outs["s5pi"].append(fi.reshape(bp, grp, nst))
            z, fr, fi = _s5_sample(_to_steps(xs, nb, steps), _to_steps(ms[0], nb, steps),
                                   _to_steps(ms[1], nb, steps), g_mix, s5_d[j], ar, ai, wb, wc,
                                   state_s5_re[j].reshape(nb, grp * nst), state_s5_im[j].reshape(nb, grp * nst),
                                   nb, steps)
            xs = _glu(_to_batch(z, nb, steps), xs, ms[2], w_glu, rs)
            outs["s5sr"].append(fr.reshape(nb, grp, nst))
            outs["s5si"].append(fi.reshape(nb, grp, nst))
        elif m == 2:
            w_in = sc_w_in[j].astype(BF16)
            w_out = sc_w_out[j].astype(BF16)
            width = sc_w_conv.shape[1]
            (p,) = _proj(xp, mp[0], mp[1], g_mix, w_in, (3 * d,), tm)
            xp, nbuf = _sconv_prompt(p, xp, mp[2], jnp.zeros((bp, width - 1, d), F32), sc_w_conv[j], w_out, tm)
            outs["scp"].append(nbuf)
            (p,) = _proj(xs, ms[0], ms[1], g_mix, w_in, (3 * d,), rs)
            buf = state_sconv[j].transpose(1, 0, 2).reshape((width - 1) * nb, d)
            y, nbuf = _sconv_sample(_to_steps(p, nb, steps), _to_steps(xs, nb, steps), _to_steps(ms[2], nb, steps),
                                    buf, sc_w_conv[j], w_out, nb, steps)
            xs = _to_batch(y, nb, steps)
            outs["scs"].append(nbuf.reshape(width - 1, nb, d).transpose(1, 0, 2))
        else:
            inner = ssd_norm.shape[1]
            cdim = ssd_conv_w.shape[2]
            nh = ssd_dt_bias.shape[1]
            width = ssd_conv_w.shape[1]
            w_in = jnp.pad(ssd_w_in[j], ((0, 0), (0, LANE - nh))).astype(BF16)
            w_out = ssd_w_out[j].astype(BF16)
            z, xbc, dtr = _proj(xp, mp[0], mp[1], g_mix, w_in, (inner, cdim, LANE), tm)
            y, nbuf, hf = _ssd_prompt(z, xbc, dtr, ssd_conv_w[j], ssd_conv_b[j], ssd_dt_bias[j], ssd_a_log[j],
                                      ssd_d[j], ssd_norm[j])
            xp = _outproj(y, xp, mp[2], w_out, tm)
            outs["ssdp"].append(hf)
            outs["ssdcp"].append(nbuf)
            z, xbc, dtr = _proj(xs, ms[0], ms[1], g_mix, w_in, (inner, cdim, LANE), rs)
            buf = state_ssd_conv[j].transpose(1, 0, 2).reshape((width - 1) * nb, cdim)
            xc, dt, nbuf = _ssd_prep_sample(_to_steps(xbc, nb, steps), buf, _to_steps(dtr, nb, steps),
                                            ssd_conv_w[j], ssd_conv_b[j], ssd_dt_bias[j], nb, steps)
            y, hf = _ssd_scan_sample(_pad_q(_to_batch(xc, nb, steps), nb, steps),
                                     _pad_q(_to_batch(dt, nb, steps), nb, steps), _pad_q(z, nb, steps),
                                     state_ssd[j].reshape(nb, nh * SSD_HEADDIM, SSD_STATE),
                                     ssd_a_log[j], ssd_d[j], ssd_norm[j], steps)
            xs = _outproj(y[:, :steps].reshape(1, rs, inner), xs, ms[2], w_out, rs)
            outs["ssds"].append(hf.reshape(nb, nh, SSD_HEADDIM, SSD_STATE))
            outs["ssdcs"].append(nbuf.reshape(width - 1, nb, cdim).transpose(1, 0, 2))
        w1 = mlp_w1[i].astype(BF16)
        w2 = mlp_w2[i].astype(BF16)
        fin = i == depth - 1
        xp = _mlp(xp, mp[3], mp[4], mp[5], norm_mlp[i], w1, w2, norm_final, fin, tm)
        xs = _mlp(xs, ms[3], ms[4], ms[5], norm_mlp[i], w1, w2, norm_final, fin, rs)
    st = jnp.stack
    return (xp, xs.reshape(nb, steps, d), st(outs["kp"]), st(outs["vp"]), st(outs["kip"]),
            st(outs["ks"]), st(outs["vs"]), st(outs["kis"]), st(outs["s5pr"]), st(outs["s5pi"]),
            st(outs["s5sr"]), st(outs["s5si"]), st(outs["scp"]), st(outs["scs"]),
            st(outs["ssdp"]), st(outs["ssdcp"]), st(outs["ssds"]), st(outs["ssdcs"]))
```

```python
import functools
import math

import jax
import jax.numpy as jnp
import numpy as np
from jax import lax
from jax.experimental import pallas as pl
from jax.experimental.pallas import tpu as pltpu

F32 = jnp.float32
BF16 = jnp.bfloat16
I32 = jnp.int32

EPS = 1e-6
N_HEADS = 16
HEAD_DIM = 64
N_KV_HEADS = 4
IDX_HEADS = 8
IDX_DIM = 64
TOPK_MAX = 256
N_BUCKETS = 32
MAX_DISTANCE = 128
S5_GROUP = 16
S5_STATE = 64
SSD_HEADDIM = 64
SSD_GROUPS = 4
SSD_STATE = 128
SSD_CHUNK = 128

LANE = 128
SUBLANE = 8
VMEM_LIMIT = 56 * 1024 * 1024
NEG_BIG = -1e30
INT_MIN = -2147483648


def _cparams(n_axes):
    return pltpu.CompilerParams(dimension_semantics=("arbitrary",) * n_axes,
                                vmem_limit_bytes=VMEM_LIMIT)


def _dot(a, b):
    return jnp.dot(a, b, preferred_element_type=F32)


def _dot_nt(a, b):
    return lax.dot_general(a, b, (((1,), (1,)), ((), ())), preferred_element_type=F32)


def _split(a):
    hi = a.astype(BF16)
    lo = (a - hi.astype(F32)).astype(BF16)
    return hi, lo


def _dot3(a, b):
    ah, al = _split(a)
    bh, bl = _split(b)
    return _dot(ah, bh) + (_dot(al, bh) + _dot(ah, bl))


def _rms_mod(x, g, shift, scale):
    y = x * lax.rsqrt(jnp.mean(x * x, axis=-1, keepdims=True) + EPS)
    return (y * g) * (1.0 + scale) + shift


def _sigmoid(x):
    return 1.0 / (1.0 + jnp.exp(-x))


def _silu(x):
    return x * _sigmoid(x)


def _row_block(arr, tm):
    n = arr.shape[-1]
    if arr.shape[1] == 1:
        return pl.BlockSpec((1, 1, n), lambda b, i: (b, 0, 0))
    return pl.BlockSpec((1, tm, n), lambda b, i: (b, i, 0))


def _const2(shape):
    return pl.BlockSpec(shape, lambda b, i: (0,) * len(shape))


def _ada_kernel(c_ref, w_ref, b_ref, o_ref):
    o_ref[0] = _dot3(_silu(c_ref[...]), w_ref[0]) + b_ref[0]


def _ada(c_all, ada_w, ada_b):
    depth, d, n = ada_w.shape
    rows = c_all.shape[0]
    tn = 1536
    return pl.pallas_call(
        _ada_kernel,
        grid=(depth, n // tn),
        in_specs=[pl.BlockSpec((rows, d), lambda l, j: (0, 0)),
                  pl.BlockSpec((1, d, tn), lambda l, j: (l, 0, j)),
                  pl.BlockSpec((1, 1, tn), lambda l, j: (l, 0, j))],
        out_specs=pl.BlockSpec((1, rows, tn), lambda l, j: (l, 0, j)),
        out_shape=jax.ShapeDtypeStruct((depth, rows, n), F32),
        compiler_params=_cparams(2),
        name="ada",
    )(c_all, ada_w, ada_b.reshape(depth, 1, n))


def _proj_kernel(x_ref, sh_ref, sc_ref, g_ref, w_ref, *o_refs, splits):
    h = _rms_mod(x_ref[0], g_ref[...], sh_ref[0], sc_ref[0]).astype(BF16)
    off = 0
    for o_ref, n in zip(o_refs, splits):
        o_ref[0] = _dot(h, w_ref[:, off:off + n])
        off += n


def _proj(x, shift, scale, g, w_bf16, splits, tm):
    bsz, seq, d = x.shape
    n = w_bf16.shape[1]
    assert sum(splits) == n and seq % tm == 0
    return pl.pallas_call(
        functools.partial(_proj_kernel, splits=tuple(splits)),
        grid=(bsz, seq // tm),
        in_specs=[pl.BlockSpec((1, tm, d), lambda b, i: (b, i, 0)),
                  _row_block(shift, tm), _row_block(scale, tm),
                  _const2((1, d)), _const2((d, n))],
        out_specs=[pl.BlockSpec((1, tm, s), lambda b, i: (b, i, 0)) for s in splits],
        out_shape=[jax.ShapeDtypeStruct((bsz, seq, s), F32) for s in splits],
        compiler_params=_cparams(2),
        name="proj",
    )(x, shift, scale, g.reshape(1, d), w_bf16)


def _aug_heads(t, col):
    rows = t.shape[0]
    lane = lax.broadcasted_iota(I32, (rows, LANE - HEAD_DIM), 1)
    extra = jnp.where(lane == 0, col, 0.0).astype(F32)
    parts = []
    for g in range(N_KV_HEADS):
        parts += [t[:, g * HEAD_DIM:(g + 1) * HEAD_DIM], extra]
    return jnp.concatenate(parts, axis=1).astype(BF16)


def _attn_proj_kernel(x_ref, sh_ref, sc_ref, g_ref, w_ref, wi_ref,
                      q_ref, k_ref, v_ref, kw_ref, qcat_ref, kcat_ref, kaug_ref, vaug_ref,
                      *, nq, nkv, nqi):
    h = _rms_mod(x_ref[0], g_ref[...], sh_ref[0], sc_ref[0])
    hb = h.astype(BF16)
    q_ref[0] = _dot(hb, w_ref[:, 0:nq]) * (HEAD_DIM ** -0.5)
    k = _dot(hb, w_ref[:, nq:nq + nkv])
    v = _dot(hb, w_ref[:, nq + nkv:nq + 2 * nkv])
    k_ref[0] = k
    v_ref[0] = v
    r = _dot3(h, wi_ref[...])
    kw = r[:, nqi:nqi + LANE]
    kw_ref[0] = kw
    zero = jnp.zeros((h.shape[0], IDX_DIM), BF16)
    parts = []
    for hh in range(IDX_HEADS):
        ah, al = _split(r[:, hh * IDX_DIM:(hh + 1) * IDX_DIM])
        parts += [ah, al, ah, zero]
    qcat_ref[0] = jnp.concatenate(parts, axis=1)
    kcat_ref[0] = _idx_rhs(kw[:, 0:IDX_DIM])
    kaug_ref[0] = _aug_heads(k, -1.0)
    vaug_ref[0] = _aug_heads(v, 1.0)


def _attn_proj(x, shift, scale, g, w_qkv_bf16, w_idx, tm):
    bsz, seq, d = x.shape
    nq = N_HEADS * HEAD_DIM
    nkv = N_KV_HEADS * HEAD_DIM
    nqi = IDX_HEADS * IDX_DIM
    widths = ((nq, F32), (nkv, F32), (nkv, F32), (LANE, F32), (4 * nqi, BF16), (4 * IDX_DIM, BF16),
              (N_KV_HEADS * LANE, BF16), (N_KV_HEADS * LANE, BF16))
    return pl.pallas_call(
        functools.partial(_attn_proj_kernel, nq=nq, nkv=nkv, nqi=nqi),
        grid=(bsz, seq // tm),
        in_specs=[pl.BlockSpec((1, tm, d), lambda b, i: (b, i, 0)),
                  _row_block(shift, tm), _row_block(scale, tm),
                  _const2((1, d)), _const2(w_qkv_bf16.shape), _const2(w_idx.shape)],
        out_specs=[pl.BlockSpec((1, tm, s), lambda b, i: (b, i, 0)) for s, _ in widths],
        out_shape=[jax.ShapeDtypeStruct((bsz, seq, s), dt) for s, dt in widths],
        compiler_params=_cparams(2),
        name="attn_proj",
    )(x, shift, scale, g.reshape(1, d), w_qkv_bf16, w_idx)


def _outproj_kernel(o_ref, x_ref, gate_ref, w_ref, y_ref):
    y_ref[0] = x_ref[0] + gate_ref[0] * _dot(o_ref[0].astype(BF16), w_ref[...])


def _outproj(o, x, gate, w_bf16, tm):
    bsz, seq, d = x.shape
    k = o.shape[-1]
    return pl.pallas_call(
        _outproj_kernel,
        grid=(bsz, seq // tm),
        in_specs=[pl.BlockSpec((1, tm, k), lambda b, i: (b, i, 0)),
                  pl.BlockSpec((1, tm, d), lambda b, i: (b, i, 0)),
                  _row_block(gate, tm), _const2((k, d))],
        out_specs=pl.BlockSpec((1, tm, d), lambda b, i: (b, i, 0)),
        out_shape=jax.ShapeDtypeStruct((bsz, seq, d), F32),
        compiler_params=_cparams(2),
        name="outproj",
    )(o, x, gate, w_bf16)


def _mlp_kernel(x_ref, sh_ref, sc_ref, gate_ref, g_ref, w1_ref, w2_ref, gf_ref, y_ref,
                *, ff_chunk, final_norm):
    x = x_ref[0]
    h = _rms_mod(x, g_ref[...], sh_ref[0], sc_ref[0]).astype(BF16)
    dff = w1_ref.shape[1]
    acc = jnp.zeros(x.shape, F32)
    for c in range(dff // ff_chunk):
        a = jnp.maximum(_dot(h, w1_ref[:, c * ff_chunk:(c + 1) * ff_chunk]), 0.0)
        acc = acc + _dot((a * a).astype(BF16), w2_ref[c * ff_chunk:(c + 1) * ff_chunk, :])
    y = x + gate_ref[0] * acc
    if final_norm:
        y = (y * lax.rsqrt(jnp.mean(y * y, axis=-1, keepdims=True) + EPS)) * gf_ref[...]
    y_ref[0] = y


def _mlp(x, shift, scale, gate, g, w1_bf16, w2_bf16, g_final, final_norm, tm):
    bsz, seq, d = x.shape
    dff = w1_bf16.shape[1]
    return pl.pallas_call(
        functools.partial(_mlp_kernel, ff_chunk=1024, final_norm=final_norm),
        grid=(bsz, seq // tm),
        in_specs=[pl.BlockSpec((1, tm, d), lambda b, i: (b, i, 0)),
                  _row_block(shift, tm), _row_block(scale, tm), _row_block(gate, tm),
                  _const2((1, d)), _const2((d, dff)), _const2((dff, d)), _const2((1, d))],
        out_specs=pl.BlockSpec((1, tm, d), lambda b, i: (b, i, 0)),
        out_shape=jax.ShapeDtypeStruct((bsz, seq, d), F32),
        compiler_params=_cparams(2),
        name="mlp",
    )(x, shift, scale, gate, g.reshape(1, d), w1_bf16, w2_bf16, g_final.reshape(1, d))


S5_CH = 128
S5_BLK = (S5_CH // S5_GROUP) * S5_STATE


def _s5_disc_kernel(lr_ref, li_ref, ldt_ref, br_ref, bi_ref, ar_ref, ai_ref, bbr_ref, bbi_ref):
    lr, li = lr_ref[...], li_ref[...]
    dt = jnp.exp(ldt_ref[...])
    mag = jnp.exp(lr * dt)
    ab_re, ab_im = mag * jnp.cos(li * dt), mag * jnp.sin(li * dt)
    den = lr * lr + li * li
    nr = ab_re - 1.0
    f_re = (nr * lr + ab_im * li) / den
    f_im = (ab_im * lr - nr * li) / den
    ar_ref[...] = ab_re
    ai_ref[...] = ab_im
    for c in range(br_ref.shape[0]):
        br, bi = br_ref[c], bi_ref[c]
        bbr_ref[c] = f_re * br - f_im * bi
        bbi_ref[c] = f_re * bi + f_im * br


def _s5_weights(lam_re, lam_im, log_dt, b_re, b_im, c_re, c_im):
    g, p = lam_re.shape
    gc = b_re.shape[-1]
    brt = jnp.moveaxis(b_re, 2, 0)
    bit = jnp.moveaxis(b_im, 2, 0)
    ar, ai, bbr, bbi = pl.pallas_call(
        _s5_disc_kernel,
        out_shape=[jax.ShapeDtypeStruct((g, p), F32), jax.ShapeDtypeStruct((g, p), F32),
                   jax.ShapeDtypeStruct((gc, g, p), F32), jax.ShapeDtypeStruct((gc, g, p), F32)],
        name="s5_disc",
    )(lam_re, lam_im, log_dt.reshape(g, 1), brt, bit)
    nblk = (g * gc) // S5_CH
    gpb = g // nblk
    eye = jnp.eye(gpb, dtype=F32)

    def bd_in(bb):
        t = jnp.moveaxis(bb, 0, 1).reshape(nblk, gpb, gc, p)
        return jnp.einsum('ngcp,gh->ngchp', t, eye).reshape(nblk, gpb * gc, gpb * p)

    def bd_out(cc):
        t = cc.reshape(nblk, gpb, gc, p)
        return jnp.einsum('ngcp,gh->ngphc', t, eye).reshape(nblk, gpb * p, gpb * gc)

    wb = jnp.concatenate([bd_in(bbr), bd_in(bbi)], axis=-1).astype(BF16)
    wc = jnp.concatenate([bd_out(c_re), bd_out(c_im)], axis=1).astype(BF16)
    return ar.reshape(nblk, 1, gpb * p), ai.reshape(nblk, 1, gpb * p), wb, wc


def _gelu_tanh(y):
    return 0.5 * y * (1.0 + jnp.tanh(math.sqrt(2.0 / math.pi) * (y + 0.044715 * (y * y * y))))


def _cmul(ar, ai, xr, xi):
    return ar * xr - ai * xi, ar * xi + ai * xr


def _s5_prompt_kernel(x_ref, sh_ref, sc_ref, g_ref, d_ref, ar_ref, ai_ref, wb_ref, wc_ref,
                      perm_ref, unperm_ref, z_ref, fr_ref, fi_ref, u_s, br_s, bi_s, st_r, st_i,
                      *, nseg, seg):
    i = pl.program_id(1)
    nblk = wb_ref.shape[0]
    blk = ar_ref.shape[-1]

    @pl.when(i == 0)
    def _():
        st_r[...] = jnp.zeros(st_r.shape, F32)
        st_i[...] = jnp.zeros(st_i.shape, F32)

    u_s[...] = _dot_sel_lhs(perm_ref[...], _rms_mod(x_ref[0], g_ref[...], sh_ref[0], sc_ref[0]))

    for c in range(nblk):
        lo, hi = c * S5_CH, (c + 1) * S5_CH
        uc = u_s[:, lo:hi]
        bu = _dot(uc.astype(BF16), wb_ref[c])
        br_s[...] = bu[:, :blk]
        bi_s[...] = bu[:, blk:]
        ar = jnp.broadcast_to(ar_ref[c], (nseg, blk))
        ai = jnp.broadcast_to(ai_ref[c], (nseg, blk))

        def local(j, carry):
            xr, xi = carry
            r0 = pl.multiple_of(j * nseg, nseg)
            pr, pi = _cmul(ar, ai, xr, xi)
            nr = pr + br_s[pl.ds(r0, nseg), :]
            ni = pi + bi_s[pl.ds(r0, nseg), :]
            br_s[pl.ds(r0, nseg), :] = nr
            bi_s[pl.ds(r0, nseg), :] = ni
            return nr, ni

        zero = jnp.zeros((nseg, blk), F32)
        fr, fi = lax.fori_loop(0, seg, local, (zero, zero))

        pr, pi = ar_ref[c], ai_ref[c]
        for _ in range(int(math.log2(seg))):
            pr, pi = _cmul(pr, pi, pr, pi)
        cr, ci = st_r[c], st_i[c]
        rows_r, rows_i = [], []
        for s in range(nseg):
            rows_r.append(cr)
            rows_i.append(ci)
            tr, ti = _cmul(pr, pi, cr, ci)
            cr, ci = tr + fr[s:s + 1], ti + fi[s:s + 1]
        st_r[c] = cr
        st_i[c] = ci
        fr_ref[0, c] = cr
        fi_ref[0, c] = ci
        dr, di = _cmul(ar, ai, jnp.concatenate(rows_r, axis=0), jnp.concatenate(rows_i, axis=0))

        def fix(j, carry):
            dr, di = carry
            r0 = pl.multiple_of(j * nseg, nseg)
            br_s[pl.ds(r0, nseg), :] = br_s[pl.ds(r0, nseg), :] + dr
            bi_s[pl.ds(r0, nseg), :] = bi_s[pl.ds(r0, nseg), :] + di
            return _cmul(ar, ai, dr, di)

        lax.fori_loop(0, seg, fix, (dr, di))

        y = (_dot(br_s[...].astype(BF16), wc_ref[c, :blk, :])
             - _dot(bi_s[...].astype(BF16), wc_ref[c, blk:, :]))
        y = y + d_ref[:, lo:hi] * uc
        u_s[:, lo:hi] = _gelu_tanh(y)

    z_ref[0] = _dot_sel_lhs(unperm_ref[...], u_s[...])


def _s5_prompt(x, shift, scale, g, d_skip, ar, ai, wb, wc):
    bsz, seq, d = x.shape
    nblk, _, blk = ar.shape
    nseg, seg = SUBLANE, 32
    tm = nseg * seg
    assert seq % tm == 0
    perm = np.zeros((tm, tm), np.float32)
    for s in range(nseg):
        for j in range(seg):
            perm[j * nseg + s, s * seg + j] = 1.0
    unperm = jnp.asarray(perm.T, BF16)
    perm = jnp.asarray(perm, BF16)
    z, fr, fi = pl.pallas_call(
        functools.partial(_s5_prompt_kernel, nseg=nseg, seg=seg),
        grid=(bsz, seq // tm),
        in_specs=[pl.BlockSpec((1, tm, d), lambda b, i: (b, i, 0)),
                  _row_block(shift, tm), _row_block(scale, tm),
                  _const2((1, d)), _const2((1, d)),
                  _const2(ar.shape), _const2(ai.shape), _const2(wb.shape), _const2(wc.shape),
                  _const2((tm, tm)), _const2((tm, tm))],
        out_specs=[pl.BlockSpec((1, tm, d), lambda b, i: (b, i, 0)),
                   pl.BlockSpec((1, nblk, 1, blk), lambda b, i: (b, 0, 0, 0)),
                   pl.BlockSpec((1, nblk, 1, blk), lambda b, i: (b, 0, 0, 0))],
        out_shape=[jax.ShapeDtypeStruct((bsz, seq, d), F32),
                   jax.ShapeDtypeStruct((bsz, nblk, 1, blk), F32),
                   jax.ShapeDtypeStruct((bsz, nblk, 1, blk), F32)],
        scratch_shapes=[pltpu.VMEM((tm, d), F32), pltpu.VMEM((tm, blk), F32), pltpu.VMEM((tm, blk), F32),
                        pltpu.VMEM((nblk, 1, blk), F32), pltpu.VMEM((nblk, 1, blk), F32)],
        compiler_params=_cparams(2),
        name="s5_prompt",
    )(x, shift, scale, g.reshape(1, d), d_skip.reshape(1, d), ar, ai, wb, wc, perm, unperm)
    return z, fr, fi


def _s5_sample_kernel(x_ref, sh_ref, sc_ref, g_ref, d_ref, ar_ref, ai_ref, wb_ref, wc_ref,
                      s0r_ref, s0i_ref, z_ref, fr_ref, fi_ref, *, nb, steps):
    nblk = wb_ref.shape[0]
    blk = ar_ref.shape[-1]
    u = _rms_mod(x_ref[...], g_ref[...], sh_ref[...], sc_ref[...])
    for c in range(nblk):
        lo, hi = c * S5_CH, (c + 1) * S5_CH
        uc = u[:, lo:hi]
        bu = _dot(uc.astype(BF16), wb_ref[c])
        ar = jnp.broadcast_to(ar_ref[c], (nb, blk))
        ai = jnp.broadcast_to(ai_ref[c], (nb, blk))
        xr, xi = s0r_ref[:, c * blk:(c + 1) * blk], s0i_ref[:, c * blk:(c + 1) * blk]
        xrs, xis = [], []
        for t in range(steps):
            pr, pi = _cmul(ar, ai, xr, xi)
            xr = pr + bu[t * nb:(t + 1) * nb, :blk]
            xi = pi + bu[t * nb:(t + 1) * nb, blk:]
            xrs.append(xr)
            xis.append(xi)
        fr_ref[:, c * blk:(c + 1) * blk] = xr
        fi_ref[:, c * blk:(c + 1) * blk] = xi
        y = (_dot(jnp.concatenate(xrs, axis=0).astype(BF16), wc_ref[c, :blk, :])
             - _dot(jnp.concatenate(xis, axis=0).astype(BF16), wc_ref[c, blk:, :]))
        y = y + d_ref[:, lo:hi] * uc
        z_ref[:, lo:hi] = _gelu_tanh(y)


def _s5_sample(x_tb, shift_tb, scale_tb, g, d_skip, ar, ai, wb, wc, s0r, s0i, nb, steps):
    rows, d = x_tb.shape
    nblk, _, blk = ar.shape
    return pl.pallas_call(
        functools.partial(_s5_sample_kernel, nb=nb, steps=steps),
        out_shape=[jax.ShapeDtypeStruct((rows, d), F32),
                   jax.ShapeDtypeStruct((nb, nblk * blk), F32),
                   jax.ShapeDtypeStruct((nb, nblk * blk), F32)],
        compiler_params=pltpu.CompilerParams(vmem_limit_bytes=VMEM_LIMIT),
        name="s5_sample",
    )(x_tb, shift_tb, scale_tb, g.reshape(1, d), d_skip.reshape(1, d), ar, ai, wb, wc, s0r, s0i)


def _glu_kernel(z_ref, x_ref, gate_ref, w_ref, y_ref):
    d = x_ref.shape[-1]
    zb = z_ref[0].astype(BF16)
    ga = _dot(zb, w_ref[:, :d])
    gb = _dot(zb, w_ref[:, d:])
    y_ref[0] = x_ref[0] + gate_ref[0] * (ga * _sigmoid(gb))


def _glu(z, x, gate, w_bf16, tm):
    bsz, seq, d = x.shape
    return pl.pallas_call(
        _glu_kernel,
        grid=(bsz, seq // tm),
        in_specs=[pl.BlockSpec((1, tm, d), lambda b, i: (b, i, 0)),
                  pl.BlockSpec((1, tm, d), lambda b, i: (b, i, 0)),
                  _row_block(gate, tm), _const2(w_bf16.shape)],
        out_specs=pl.BlockSpec((1, tm, d), lambda b, i: (b, i, 0)),
        out_shape=jax.ShapeDtypeStruct((bsz, seq, d), F32),
        compiler_params=_cparams(2),
        name="glu",
    )(z, x, gate, w_bf16)


def _shift_rows(u, prev, k):
    if k == 0:
        return u
    rolled = pltpu.roll(u, k, 0)
    head = pltpu.roll(prev, k, 0)
    row = lax.broadcasted_iota(I32, (SUBLANE, u.shape[1]), 0)
    first = jnp.where(row < k, head, rolled[:SUBLANE])
    return jnp.concatenate([first, rolled[SUBLANE:]], axis=0)


def _sconv_prompt_kernel(p_ref, x_ref, gate_ref, buf_ref, wc_ref, wo_ref, y_ref, nb_ref, prev_s,
                         *, width):
    i = pl.program_id(1)
    d = x_ref.shape[-1]

    @pl.when(i == 0)
    def _():
        prev_s[...] = jnp.zeros(prev_s.shape, F32)
        prev_s[SUBLANE - (width - 1):, :] = buf_ref[0]

    p = p_ref[0]
    gb, gc, xh = p[:, :d], p[:, d:2 * d], p[:, 2 * d:]
    u = gc * xh
    prev = prev_s[...]
    conv = u * wc_ref[width - 1:width, :]
    for k in range(1, width):
        conv = conv + _shift_rows(u, prev, k) * wc_ref[width - 1 - k:width - k, :]
    prev_s[...] = u[u.shape[0] - SUBLANE:, :]
    nb_ref[0] = u[u.shape[0] - (width - 1):, :]
    y_ref[0] = x_ref[0] + gate_ref[0] * _dot((gb * conv).astype(BF16), wo_ref[...])


def _sconv_prompt(p, x, gate, buf, w_conv, w_out_bf16, tm):
    bsz, seq, d = x.shape
    width = w_conv.shape[0]
    return pl.pallas_call(
        functools.partial(_sconv_prompt_kernel, width=width),
        grid=(bsz, seq // tm),
        in_specs=[pl.BlockSpec((1, tm, 3 * d), lambda b, i: (b, i, 0)),
                  pl.BlockSpec((1, tm, d), lambda b, i: (b, i, 0)),
                  _row_block(gate, tm),
                  pl.BlockSpec((1, width - 1, d), lambda b, i: (b, 0, 0)),
                  _const2((width, d)), _const2((d, d))],
        out_specs=[pl.BlockSpec((1, tm, d), lambda b, i: (b, i, 0)),
                   pl.BlockSpec((1, width - 1, d), lambda b, i: (b, 0, 0))],
        out_shape=[jax.ShapeDtypeStruct((bsz, seq, d), F32),
                   jax.ShapeDtypeStruct((bsz, width - 1, d), F32)],
        scratch_shapes=[pltpu.VMEM((SUBLANE, d), F32)],
        compiler_params=_cparams(2),
        name="sconv_prompt",
    )(p, x, gate, buf, w_conv, w_out_bf16)


def _sconv_sample_kernel(p_ref, x_ref, gate_ref, buf_ref, wc_ref, wo_ref, y_ref, nb_ref,
                         *, width, nb, steps):
    d = x_ref.shape[-1]
    p = p_ref[...]
    gb, gc, xh = p[:, :d], p[:, d:2 * d], p[:, 2 * d:]
    u = gc * xh
    ext = jnp.concatenate([buf_ref[...], u], axis=0)
    conv = ext[0:steps * nb] * wc_ref[0:1, :]
    for k in range(1, width):
        conv = conv + ext[k * nb:(k + steps) * nb] * wc_ref[k:k + 1, :]
    nb_ref[...] = ext[steps * nb:]
    y_ref[...] = x_ref[...] + gate_ref[...] * _dot((gb * conv).astype(BF16), wo_ref[...])


def _sconv_sample(p_tb, x_tb, gate_tb, buf_kb, w_conv, w_out_bf16, nb, steps):
    rows, d = x_tb.shape
    width = w_conv.shape[0]
    return pl.pallas_call(
        functools.partial(_sconv_sample_kernel, width=width, nb=nb, steps=steps),
        out_shape=[jax.ShapeDtypeStruct((rows, d), F32),
                   jax.ShapeDtypeStruct(((width - 1) * nb, d), F32)],
        compiler_params=pltpu.CompilerParams(vmem_limit_bytes=VMEM_LIMIT),
        name="sconv_sample",
    )(p_tb, x_tb, gate_tb, buf_kb, w_conv, w_out_bf16)


def _sortable(s):
    bits = lax.bitcast_convert_type(s + 0.0, I32)
    return bits ^ ((bits >> 31) & 0x7FFFFFFF)


def _stack_heads(qcat):
    w = 4 * IDX_DIM
    return jnp.concatenate([qcat[:, h * w:(h + 1) * w] for h in range(IDX_HEADS)], axis=0)


def _idx_rhs(ki):
    kh, kl = _split(ki)
    return jnp.concatenate([kh, kh, kl, jnp.zeros(kh.shape, BF16)], axis=1)


def _idx_score(dts, wcols, nrows):
    s = wcols[0] * jnp.maximum(dts[0:nrows], 0.0)
    for h in range(1, IDX_HEADS):
        s = s + wcols[h] * jnp.maximum(dts[h * nrows:(h + 1) * nrows], 0.0)
    return s


def _idx_wcols(kw):
    sc = (IDX_HEADS ** -0.5) * (IDX_DIM ** -0.5)
    return [kw[:, IDX_DIM + h:IDX_DIM + h + 1] * sc for h in range(IDX_HEADS)]


def _topk_select(count, rows, lanes, topk, idx_bits):
    wide = lambda v: jnp.broadcast_to(v, (rows, lanes))

    def bit_step(it, tu):
        mask = jnp.left_shift(jnp.int32(1), 31 - it)
        cand_u = tu | mask
        cand = wide(cand_u ^ INT_MIN)
        cnt = count(lambda c, key, idx: jnp.where(key >= cand, 1.0, 0.0))
        return jnp.where(cnt >= topk, cand_u, tu)

    tu = lax.fori_loop(0, 32, bit_step, jnp.zeros((rows, 1), I32))
    t = tu ^ INT_MIN
    tw = wide(t)
    n_gt = count(lambda c, key, idx: jnp.where(key > tw, 1.0, 0.0))
    n_ge = count(lambda c, key, idx: jnp.where(key >= tw, 1.0, 0.0))
    need = topk - n_gt
    tied = jnp.max(jnp.where(t != INT_MIN, n_ge - topk, 0.0)) > 0.0

    def tie_search(_):
        def idx_step(it, j):
            cand = j | jnp.left_shift(jnp.int32(1), idx_bits - 1 - it)
            cw = wide(cand)
            cnt = count(lambda c, key, idx: jnp.where((key == tw) & (idx < cw), 1.0, 0.0))
            return jnp.where(cnt < need, cand, j)
        return lax.fori_loop(0, idx_bits, idx_step, jnp.zeros((rows, 1), I32))

    j = lax.cond(tied, tie_search, lambda _: jnp.full((rows, 1), 2 ** idx_bits, I32), 0)
    return t, j


def _dsa_prompt_kernel(q_ref, qcat_ref, kwq_ref, kcat_ref, kaug_ref, vaug_ref, bias_ref, o_ref,
                       key_s, madd_s, qa_s, qm_s, mx_s, acc_s, *, tq, topk, idx_bits):
    i = pl.program_id(1)
    nk = i + 1
    nk2 = (nk + 1) // 2
    r = N_HEADS // N_KV_HEADS

    wcols = _idx_wcols(kwq_ref[0])
    qrow = lax.broadcasted_iota(I32, (tq, tq), 0)
    kcol = lax.broadcasted_iota(I32, (tq, tq), 1)
    qpos = i * tq + qrow
    wq = 4 * IDX_DIM

    def score_pair(c2, _):
        r0 = pl.multiple_of(c2 * 2 * tq, 2 * tq)
        kc = kcat_ref[0, pl.ds(r0, 2 * tq), :]
        s = None
        for h in range(IDX_HEADS):
            d = wcols[h] * jnp.maximum(_dot_nt(qcat_ref[0, :, h * wq:(h + 1) * wq], kc), 0.0)
            s = d if s is None else s + d
        for u in range(2):
            c = 2 * c2 + u
            key_s[c] = jnp.where(c * tq + kcol <= qpos, _sortable(s[:, u * tq:(u + 1) * tq]), INT_MIN)
        return 0

    lax.fori_loop(0, nk2, score_pair, 0)

    def count(fn):
        def body(c2, acc):
            c = 2 * c2
            acc = acc + fn(c, key_s[c], c * tq + kcol)
            return acc + fn(c + 1, key_s[c + 1], (c + 1) * tq + kcol)
        acc = lax.fori_loop(0, nk2, body, jnp.zeros((tq, tq), F32))
        return jnp.sum(acc, axis=1, keepdims=True)

    t, j = _topk_select(count, tq, tq, float(topk), idx_bits)
    tw = jnp.broadcast_to(t, (tq, tq))
    jw = jnp.broadcast_to(j, (tq, tq))

    def mask_pair(c2, _):
        for u in range(2):
            c = 2 * c2 + u
            key = key_s[c]
            sel = (key > tw) | ((key == tw) & (c * tq + kcol <= jw))
            madd_s[c] = jnp.where(sel & (key != INT_MIN), 0.0, NEG_BIG)
        return 0

    lax.fori_loop(0, nk2, mask_pair, 0)

    q = q_ref[0]
    zpad = jnp.zeros((tq, LANE - HEAD_DIM), F32)
    qa = []
    for g in range(N_KV_HEADS):
        qa.append(jnp.concatenate(
            [jnp.concatenate([q[:, (g * r + u) * HEAD_DIM:(g * r + u + 1) * HEAD_DIM], zpad], axis=1)
             for u in range(r)], axis=0))
        qa_s[g] = qa[g].astype(BF16)
    mx_s[...] = jnp.full(mx_s.shape, NEG_BIG, F32)

    def pair_mask(c2, near, g):
        halves = []
        for u in range(2):
            c = 2 * c2 + u
            m = madd_s[c]
            if near:
                back = nk - 1 - c
                carries = (back == 0) | ((back == 1) & (kcol > qrow))
                halves.append([m + jnp.where(carries, bias_ref[g * r + v], 0.0) for v in range(r)])
            else:
                halves.append([m] * r)
        return jnp.concatenate([jnp.concatenate([halves[0][v], halves[1][v]], axis=1) for v in range(r)], axis=0)

    def max_pair(c2, near):
        r0 = pl.multiple_of(c2 * 2 * tq, 2 * tq)
        ka = kaug_ref[0, pl.ds(r0, 2 * tq), :]
        for g in range(N_KV_HEADS):
            lg = _dot_nt(qa_s[g], ka[:, g * LANE:(g + 1) * LANE]) + pair_mask(c2, near, g)
            mx_s[g] = jnp.maximum(mx_s[g], jnp.maximum(lg[:, :tq], lg[:, tq:]))

    def pv_pair(c2, near):
        r0 = pl.multiple_of(c2 * 2 * tq, 2 * tq)
        ka = kaug_ref[0, pl.ds(r0, 2 * tq), :]
        va = vaug_ref[0, pl.ds(r0, 2 * tq), :]
        for g in range(N_KV_HEADS):
            lg = _dot_nt(qm_s[g], ka[:, g * LANE:(g + 1) * LANE]) + pair_mask(c2, near, g)
            acc_s[g] = acc_s[g] + _dot(jnp.exp(lg).astype(BF16), va[:, g * LANE:(g + 1) * LANE])

    def all_blocks(pair_fn):
        def far(c2, _):
            pair_fn(c2, False)
            return 0

        lax.fori_loop(0, jnp.maximum(nk2 - 2, 0), far, 0)

        @pl.when(nk2 >= 2)
        def _():
            pair_fn(nk2 - 2, True)

        pair_fn(nk2 - 1, True)

    all_blocks(max_pair)
    lane_a = lax.broadcasted_iota(I32, (r * tq, LANE), 1)
    for g in range(N_KV_HEADS):
        shift = jnp.max(mx_s[g], axis=1, keepdims=True).astype(BF16).astype(F32)
        qm_s[g] = jnp.where(lane_a == HEAD_DIM, shift, qa[g]).astype(BF16)
    acc_s[...] = jnp.zeros(acc_s.shape, F32)
    all_blocks(pv_pair)
    for g in range(N_KV_HEADS):
        acc = acc_s[g]
        o = acc[:, 0:HEAD_DIM] / acc[:, HEAD_DIM:HEAD_DIM + 1]
        for u in range(r):
            h = g * r + u
            o_ref[0, :, h * HEAD_DIM:(h + 1) * HEAD_DIM] = o[u * tq:(u + 1) * tq, :]


def _t5_bucket(dist):
    n = jnp.maximum(dist, 0)
    max_exact = N_BUCKETS // 2
    nf = jnp.maximum(n, max_exact).astype(F32)
    large = max_exact + (jnp.log(nf / max_exact) / math.log(MAX_DISTANCE / max_exact)
                         * (N_BUCKETS - max_exact)).astype(I32)
    large = jnp.minimum(large, N_BUCKETS - 1)
    return jnp.where(n < max_exact, n, large)


def _bias_table(rel_bias, n):
    tab = rel_bias[_t5_bucket(jnp.arange(n, dtype=I32))]
    return (tab - rel_bias[N_BUCKETS - 1][None, :]).T


def _dsa_prompt(q, qcat, kw, kcat, kaug, vaug, rel_bias, tq):
    bsz, seq, nq = q.shape
    topk = min(TOPK_MAX, seq // 4)
    nkc = seq // tq
    r = N_HEADS // N_KV_HEADS
    assert tq == MAX_DISTANCE and seq % (2 * tq) == 0
    tab = _bias_table(rel_bias, tq)
    dmod = (np.arange(tq)[:, None] - np.arange(tq)[None, :]) % tq
    bias = tab[:, dmod]
    row = lambda n: pl.BlockSpec((1, tq, n), lambda b, i: (b, i, 0))
    full = lambda n: pl.BlockSpec((1, seq, n), lambda b, i: (b, 0, 0))
    return pl.pallas_call(
        functools.partial(_dsa_prompt_kernel, tq=tq, topk=topk, idx_bits=int(math.log2(seq))),
        grid=(bsz, nkc),
        in_specs=[row(nq), row(qcat.shape[-1]), row(LANE),
                  full(kcat.shape[-1]), full(kaug.shape[-1]), full(vaug.shape[-1]),
                  _const2((N_HEADS, tq, tq))],
        out_specs=row(nq),
        out_shape=jax.ShapeDtypeStruct((bsz, seq, nq), F32),
        scratch_shapes=[pltpu.VMEM((nkc, tq, tq), I32), pltpu.VMEM((nkc, tq, tq), F32),
                        pltpu.VMEM((N_KV_HEADS, r * tq, LANE), BF16),
                        pltpu.VMEM((N_KV_HEADS, r * tq, LANE), BF16),
                        pltpu.VMEM((N_KV_HEADS, r * tq, tq), F32),
                        pltpu.VMEM((N_KV_HEADS, r * tq, LANE), F32)],
        compiler_params=_cparams(2),
        name="dsa_prompt",
    )(q, qcat, kw, kcat, kaug, vaug, bias)


QP = SUBLANE
PAGES_PER_STEP = 8


def _page_specs(shape4, layer, pps):
    def spec(u):
        return pl.BlockSpec((1, 1) + tuple(shape4[2:]),
                            lambda b, g, pt: (layer, pt[b, g * pps + u], 0, 0))
    return [spec(u) for u in range(pps)]


def _dsa_sample_score_kernel(pt_ref, qcat_ref, kw_ref, *rest, pps):
    page_refs, o_ref = rest[:pps], rest[pps]
    lhs = _stack_heads(qcat_ref[0]).astype(BF16)
    wcols = _idx_wcols(kw_ref[0])
    for u in range(pps):
        dts = _dot_nt(lhs, _idx_rhs(page_refs[u][0, 0]))
        o_ref[0, u] = _idx_score(dts, wcols, QP)


def _dsa_sample_scores(page_table, qcat, kw, cache_kidx, layer, pps):
    nb, n_pages = page_table.shape
    page = cache_kidx.shape[2]
    grid_spec = pltpu.PrefetchScalarGridSpec(
        num_scalar_prefetch=1,
        grid=(nb, n_pages // pps),
        in_specs=[pl.BlockSpec((1, QP, qcat.shape[-1]), lambda b, g, pt: (b, 0, 0)),
                  pl.BlockSpec((1, QP, LANE), lambda b, g, pt: (b, 0, 0))]
                 + _page_specs(cache_kidx.shape, layer, pps),
        out_specs=pl.BlockSpec((1, pps, QP, page), lambda b, g, pt: (b, g, 0, 0)))
    return pl.pallas_call(
        functools.partial(_dsa_sample_score_kernel, pps=pps),
        grid_spec=grid_spec,
        out_shape=jax.ShapeDtypeStruct((nb, n_pages, QP, page), F32),
        compiler_params=_cparams(2),
        name="dsa_sample_scores",
    )(page_table, qcat, kw, *([cache_kidx] * pps))


def _dsa_sample_attend_kernel(pt_ref, q_ref, qcat_ref, kw_ref, kcn_ref, kn_ref, vn_ref, sc_ref,
                              blast_ref, bnew_ref, *rest, pps, n_pages, steps, topk, idx_bits):
    k_refs, v_refs, o_ref = rest[:pps], rest[pps:2 * pps], rest[2 * pps]
    key_s, t_s, j_s, m_s, l_s, acc_s = rest[2 * pps + 1:]
    g = pl.program_id(1)
    last = pl.num_programs(1) - 1
    page = sc_ref.shape[-1]
    r = N_HEADS // N_KV_HEADS
    nkv = N_KV_HEADS * HEAD_DIM
    lane = lax.broadcasted_iota(I32, (QP, page), 1)
    row = lax.broadcasted_iota(I32, (QP, page), 0)

    @pl.when(g == 0)
    def _():
        def conv(c, _):
            key_s[c] = _sortable(sc_ref[0, c])
            return 0

        lax.fori_loop(0, n_pages, conv, 0)
        kcn = kcn_ref[0]
        knew = jnp.concatenate([kcn, jnp.zeros((page - QP, kcn.shape[1]), F32)], axis=0).astype(BF16)
        dts = _dot_nt(_stack_heads(qcat_ref[0]).astype(BF16), knew)
        s_new = _idx_score(dts, _idx_wcols(kw_ref[0]), QP)
        key_s[n_pages] = jnp.where((lane <= row) & (lane < steps), _sortable(s_new), INT_MIN)

        def count(fn):
            def body(c, acc):
                return acc + fn(c, key_s[c], c * page + lane)
            acc = lax.fori_loop(0, n_pages + 1, body, jnp.zeros((QP, page), F32))
            return jnp.sum(acc, axis=1, keepdims=True)

        t, j = _topk_select(count, QP, page, float(topk), idx_bits)
        t_s[...] = t
        j_s[...] = j
        m_s[...] = jnp.full(m_s.shape, NEG_BIG, F32)
        l_s[...] = jnp.zeros(l_s.shape, F32)
        acc_s[...] = jnp.zeros(acc_s.shape, F32)

    t, j = t_s[...], j_s[...]
    q = q_ref[0]
    zero = jnp.zeros((QP, HEAD_DIM), F32)
    parts = []
    for h in range(N_HEADS):
        gh = h // r
        parts.append(jnp.concatenate(
            [zero] * gh + [q[:, h * HEAD_DIM:(h + 1) * HEAD_DIM]] + [zero] * (N_KV_HEADS - 1 - gh), axis=1))
    lhsq = jnp.concatenate(parts, axis=0).astype(BF16)

    def mask_tile(c):
        key = key_s[c]
        sel = (key > t) | ((key == t) & (c * page + lane <= j))
        return jnp.where(sel & (key != INT_MIN), 0.0, NEG_BIG)

    def online(lg, vmat):
        m_old = m_s[...]
        m_new = jnp.maximum(m_old, jnp.max(lg, axis=1, keepdims=True))
        p = jnp.exp(lg - m_new)
        alpha = jnp.exp(m_old - m_new)
        l_s[...] = alpha * l_s[...] + jnp.sum(p, axis=1, keepdims=True)
        acc_s[...] = alpha * acc_s[...] + _dot(p.astype(BF16), vmat)
        m_s[...] = m_new

    kp = jnp.concatenate([k_refs[u][0, 0].astype(BF16) for u in range(pps)], axis=0)
    vp = jnp.concatenate([v_refs[u][0, 0].astype(BF16) for u in range(pps)], axis=0)
    madd = jnp.concatenate([mask_tile(g * pps + u) for u in range(pps)], axis=1)
    is_last = jnp.where(g == last, 1.0, 0.0)
    lg = _dot_nt(lhsq, kp) + jnp.concatenate([madd] * N_HEADS, axis=0) + blast_ref[...] * is_last
    online(lg, vp)

    @pl.when(g == last)
    def _():
        pad = jnp.zeros((page - QP, nkv), F32)
        knp = jnp.concatenate([kn_ref[0], pad], axis=0).astype(BF16)
        vnp = jnp.concatenate([vn_ref[0], pad], axis=0).astype(BF16)
        lgn = _dot_nt(lhsq, knp) + jnp.concatenate([mask_tile(n_pages)] * N_HEADS, axis=0) + bnew_ref[...]
        online(lgn, vnp)
        o = acc_s[...] / l_s[...]
        for h in range(N_HEADS):
            gh = h // r
            o_ref[0, :, h * HEAD_DIM:(h + 1) * HEAD_DIM] = o[h * QP:(h + 1) * QP, gh * HEAD_DIM:(gh + 1) * HEAD_DIM]


def _dsa_sample(page_table, q, qcat, kw, kcn, kn, vn, scores, cache_k, cache_v, rel_bias, layer, steps, pps):
    nb, n_pages = page_table.shape
    page = cache_k.shape[2]
    past = n_pages * page
    nq, nkv = N_HEADS * HEAD_DIM, N_KV_HEADS * HEAD_DIM
    topk = min(TOPK_MAX, (past + steps) // 4)
    n_steps = n_pages // pps
    tab = _bias_table(rel_bias, 2 * page)
    qq = np.arange(QP)[:, None]
    off = np.arange(page)[None, :]
    blast = tab[:, page + qq - off].reshape(N_HEADS * QP, page)
    blast = jnp.concatenate([jnp.zeros((N_HEADS * QP, (pps - 1) * page), F32), blast], axis=1)
    bnew = tab[:, np.maximum(qq - off, 0)].reshape(N_HEADS * QP, page)
    ck = cache_k.reshape(cache_k.shape[:3] + (nkv,))
    cv = cache_v.reshape(cache_v.shape[:3] + (nkv,))
    bmap = lambda b, g, pt: (b, 0, 0)
    grid_spec = pltpu.PrefetchScalarGridSpec(
        num_scalar_prefetch=1,
        grid=(nb, n_steps),
        in_specs=[pl.BlockSpec((1, QP, nq), bmap),
                  pl.BlockSpec((1, QP, qcat.shape[-1]), bmap),
                  pl.BlockSpec((1, QP, LANE), bmap),
                  pl.BlockSpec((1, QP, kcn.shape[-1]), bmap),
                  pl.BlockSpec((1, QP, nkv), bmap),
                  pl.BlockSpec((1, QP, nkv), bmap),
                  pl.BlockSpec((1, n_pages, QP, page), lambda b, g, pt: (b, 0, 0, 0)),
                  pl.BlockSpec(blast.shape, lambda b, g, pt: (0, 0)),
                  pl.BlockSpec(bnew.shape, lambda b, g, pt: (0, 0))]
                 + _page_specs(ck.shape, layer, pps) + _page_specs(cv.shape, layer, pps),
        out_specs=pl.BlockSpec((1, QP, nq), bmap),
        scratch_shapes=[pltpu.VMEM((n_pages + 1, QP, page), I32),
                        pltpu.VMEM((QP, 1), I32), pltpu.VMEM((QP, 1), I32),
                        pltpu.VMEM((N_HEADS * QP, 1), F32), pltpu.VMEM((N_HEADS * QP, 1), F32),
                        pltpu.VMEM((N_HEADS * QP, nkv), F32)])
    return pl.pallas_call(
        functools.partial(_dsa_sample_attend_kernel, pps=pps, n_pages=n_pages, steps=steps, topk=topk,
                          idx_bits=int(math.ceil(math.log2(past + page)))),
        grid_spec=grid_spec,
        out_shape=jax.ShapeDtypeStruct((nb, QP, nq), F32),
        compiler_params=_cparams(2),
        name="dsa_sample_attend",
    )(page_table, q, qcat, kw, kcn, kn, vn, scores, blast, bnew, *([ck] * pps), *([cv] * pps))


SSD_COLS = 512


def _softplus(x):
    return jnp.maximum(x, 0.0) + jnp.log1p(jnp.exp(-jnp.abs(x)))


def _split3(a):
    p1 = a.astype(BF16)
    r1 = a - p1.astype(F32)
    p2 = r1.astype(BF16)
    return p1, p2, (r1 - p2.astype(F32)).astype(BF16)


def _dot_sel_rhs(a, e):
    p1, p2, p3 = _split3(a)
    return _dot(p1, e) + (_dot(p2, e) + _dot(p3, e))


def _dot_sel_lhs(e, a):
    p1, p2, p3 = _split3(a)
    return _dot(e, p1) + (_dot(e, p2) + _dot(e, p3))


def _ssd_prompt_kernel(z_ref, xbc_ref, dtr_ref, cw_ref, cb_ref, dtb_ref, alog_ref, dexp_ref, ng_ref,
                       e_ref, y_ref, nbuf_ref, hf_ref, prev_s, xc_s, y_s, h_s, *, width, nh, inner):
    i = pl.program_id(1)
    last = pl.num_programs(1) - 1
    qn, cdim = xbc_ref.shape[1], xbc_ref.shape[2]
    ngrp, ns, hd = SSD_GROUPS, SSD_STATE, SSD_HEADDIM
    hpg = nh // ngrp

    @pl.when(i == 0)
    def _():
        prev_s[...] = jnp.zeros(prev_s.shape, F32)
        h_s[...] = jnp.zeros(h_s.shape, F32)

    for cb in range(cdim // SSD_COLS):
        cols = slice(cb * SSD_COLS, (cb + 1) * SSD_COLS)
        xr = xbc_ref[0, :, cols]
        prev = prev_s[:, cols]
        conv = xr * cw_ref[width - 1:width, cols]
        for k in range(1, width):
            conv = conv + _shift_rows(xr, prev, k) * cw_ref[width - 1 - k:width - k, cols]
        xc_s[:, cols] = _silu(conv + cb_ref[:, cols])
        prev_s[:, cols] = xr[qn - SUBLANE:, :]
        nbuf_ref[0, :, cols] = xr[qn - (width - 1):, :]

    hl = lax.broadcasted_iota(I32, (qn, LANE), 1)
    dt = jnp.where(hl < nh, _softplus(dtr_ref[0] + dtb_ref[...]), 0.0)
    dta = dt * (-jnp.exp(alog_ref[...]))
    qrow = lax.broadcasted_iota(I32, (qn, qn), 0)
    kcol = lax.broadcasted_iota(I32, (qn, qn), 1)
    causal = kcol <= qrow
    cum = _dot_sel_lhs(jnp.where(causal, 1.0, 0.0).astype(BF16), dta)
    cum_t = cum.T
    cum_last = cum[qn - 1:qn, :]
    ecum = jnp.exp(cum)
    dtw = dt * jnp.exp(cum_last - cum)
    e_last = jnp.exp(cum_last)
    lane_p = lax.broadcasted_iota(I32, (qn, LANE), 1)
    row_p = lax.broadcasted_iota(I32, (LANE, ns), 0)

    for g in range(ngrp):
        gl = slice(g * SSD_COLS, (g + 1) * SSD_COLS)
        e_g = e_ref[:, gl]
        xs_g = xc_s[:, gl]
        dtx_g = xs_g * _dot_sel_rhs(dt, e_g)
        dtxw_g = xs_g * _dot_sel_rhs(dtw, e_g)
        ecx_g = _dot_sel_rhs(ecum, e_g)
        bm = xc_s[:, inner + g * ns:inner + (g + 1) * ns].astype(BF16)
        cm = xc_s[:, inner + (ngrp + g) * ns:inner + (ngrp + g + 1) * ns].astype(BF16)
        cbm = _dot_nt(cm, bm)
        for pi in range(hpg // 2):
            ha = g * hpg + 2 * pi
            pls = slice(pi * LANE, (pi + 1) * LANE)
            lanes = slice(ha * hd, ha * hd + LANE)
            dtx_p = dtx_g[:, pls].astype(BF16)
            yds = []
            for h in (ha, ha + 1):
                seg = cum[:, h:h + 1] - cum_t[h:h + 1, :]
                dec = jnp.exp(jnp.where(causal, seg, -jnp.inf))
                yds.append(_dot((cbm * dec).astype(BF16), dtx_p))
            yd = jnp.where(lane_p < hd, yds[0], yds[1])
            hp = h_s[ha // 2]
            yo = _dot_nt(cm, hp.astype(BF16)) * ecx_g[:, pls]
            y_s[:, lanes] = yd + yo + dexp_ref[:, lanes] * xs_g[:, pls]
            s_new = _dot(dtxw_g[:, pls].T.astype(BF16), bm)
            cd = jnp.where(row_p < hd, e_last[:, ha:ha + 1], e_last[:, ha + 1:ha + 2])
            h_s[ha // 2] = hp * cd + s_new

    for g in range(ngrp):
        gl = slice(g * SSD_COLS, (g + 1) * SSD_COLS)
        yg = y_s[:, gl] * _silu(z_ref[0, :, gl])
        ms = jnp.mean(yg * yg, axis=-1, keepdims=True)
        y_ref[0, :, gl] = (yg * lax.rsqrt(ms + EPS)) * ng_ref[:, gl]

    @pl.when(i == last)
    def _():
        hf_ref[0] = h_s[...]


def _head_expand(nh, hd):
    e = np.zeros((LANE, nh * hd), np.float32)
    for h in range(nh):
        e[h, h * hd:(h + 1) * hd] = 1.0
    return jnp.asarray(e, BF16)


def _pad_lanes(v, n=LANE):
    v = v.reshape(1, -1)
    return jnp.pad(v, ((0, 0), (0, n - v.shape[1])))


def _ssd_prompt(z, xbc, dtr, conv_w, conv_b, dt_bias, a_log, d_skip, norm_g):
    bsz, seq, inner = z.shape
    cdim = xbc.shape[-1]
    nh = dt_bias.shape[0]
    width = conv_w.shape[0]
    qn = math.gcd(seq, SSD_CHUNK)
    assert qn == SSD_CHUNK and inner == SSD_GROUPS * SSD_COLS and cdim % SSD_COLS == 0
    dexp = jnp.repeat(d_skip, SSD_HEADDIM).reshape(1, inner)
    blk = lambda n: pl.BlockSpec((1, qn, n), lambda b, i: (b, i, 0))
    y, nbuf, hf = pl.pallas_call(
        functools.partial(_ssd_prompt_kernel, width=width, nh=nh, inner=inner),
        grid=(bsz, seq // qn),
        in_specs=[blk(inner), blk(cdim), blk(LANE),
                  _const2((width, cdim)), _const2((1, cdim)), _const2((1, LANE)), _const2((1, LANE)),
                  _const2((1, inner)), _const2((1, inner)), _const2((LANE, inner))],
        out_specs=[blk(inner),
                   pl.BlockSpec((1, width - 1, cdim), lambda b, i: (b, 0, 0)),
                   pl.BlockSpec((1, nh // 2, LANE, SSD_STATE), lambda b, i: (b, 0, 0, 0))],
        out_shape=[jax.ShapeDtypeStruct((bsz, seq, inner), F32),
                   jax.ShapeDtypeStruct((bsz, width - 1, cdim), F32),
                   jax.ShapeDtypeStruct((bsz, nh // 2, LANE, SSD_STATE), F32)],
        scratch_shapes=[pltpu.VMEM((SUBLANE, cdim), F32), pltpu.VMEM((qn, cdim), F32),
                        pltpu.VMEM((qn, inner), F32), pltpu.VMEM((nh // 2, LANE, SSD_STATE), F32)],
        compiler_params=_cparams(2),
        name="ssd_prompt",
    )(z, xbc, dtr, conv_w, conv_b.reshape(1, cdim), _pad_lanes(dt_bias), _pad_lanes(a_log),
      dexp, norm_g.reshape(1, inner), _head_expand(nh, SSD_HEADDIM))
    return y, nbuf, hf.reshape(bsz, nh, SSD_HEADDIM, SSD_STATE)


def _ssd_prep_sample_kernel(xbc_ref, buf_ref, dtr_ref, cw_ref, cb_ref, dtb_ref, xc_ref, dt_ref, nbuf_ref,
                            *, width, nb, steps, nh):
    cdim = xbc_ref.shape[1]
    for cb in range(cdim // SSD_COLS):
        cols = slice(cb * SSD_COLS, (cb + 1) * SSD_COLS)
        ext = jnp.concatenate([buf_ref[:, cols], xbc_ref[:, cols]], axis=0)
        conv = ext[0:steps * nb] * cw_ref[0:1, cols]
        for k in range(1, width):
            conv = conv + ext[k * nb:(k + steps) * nb] * cw_ref[k:k + 1, cols]
        xc_ref[:, cols] = _silu(conv + cb_ref[:, cols])
        nbuf_ref[:, cols] = ext[steps * nb:]
    hl = lax.broadcasted_iota(I32, dtr_ref.shape, 1)
    dt_ref[...] = jnp.where(hl < nh, _softplus(dtr_ref[...] + dtb_ref[...]), 0.0)


def _ssd_prep_sample(xbc_tb, buf_kb, dtr_tb, conv_w, conv_b, dt_bias, nb, steps):
    rows, cdim = xbc_tb.shape
    width = conv_w.shape[0]
    return pl.pallas_call(
        functools.partial(_ssd_prep_sample_kernel, width=width, nb=nb, steps=steps, nh=dt_bias.shape[0]),
        out_shape=[jax.ShapeDtypeStruct((rows, cdim), F32),
                   jax.ShapeDtypeStruct((rows, LANE), F32),
                   jax.ShapeDtypeStruct(((width - 1) * nb, cdim), F32)],
        compiler_params=pltpu.CompilerParams(vmem_limit_bytes=VMEM_LIMIT),
        name="ssd_prep_sample",
    )(xbc_tb, buf_kb, dtr_tb, conv_w, conv_b.reshape(1, cdim), _pad_lanes(dt_bias))


def _ssd_scan_sample_kernel(xc_ref, dt_ref, z_ref, h0_ref, alog_ref, dexp_ref, ng_ref, e_ref,
                            y_ref, hf_ref, *, steps, inner):
    ngrp, ns = SSD_GROUPS, SSD_STATE
    xc, dt = xc_ref[0], dt_ref[0]
    row = lax.broadcasted_iota(I32, (QP, LANE), 0)
    cum = dt * (-jnp.exp(alog_ref[...]))
    for sft in (1, 2, 4):
        cum = cum + jnp.where(row >= sft, pltpu.roll(cum, sft, 0), 0.0)
    cum_last = cum[QP - 1:QP, :]
    parts = [dt, jnp.exp(cum), dt * jnp.exp(cum_last - cum), jnp.broadcast_to(jnp.exp(cum_last), (QP, LANE))]
    for s in range(steps):
        parts.append(jnp.where(row >= s, jnp.exp(cum - cum[s:s + 1, :]), 0.0))
    stack = jnp.concatenate(parts, axis=0)

    for g in range(ngrp):
        gl = slice(g * SSD_COLS, (g + 1) * SSD_COLS)
        ex = _dot_sel_rhs(stack, e_ref[:, gl])
        dt_x, ecum_x, dtw_x, el_x = (ex[k * QP:(k + 1) * QP] for k in range(4))
        xs_g = xc[:, gl]
        dtx = xs_g * dt_x
        bm = xc[:, inner + g * ns:inner + (g + 1) * ns]
        cm = xc[:, inner + (ngrp + g) * ns:inner + (ngrp + g + 1) * ns].astype(BF16)
        bmp = jnp.concatenate([bm, jnp.zeros((LANE - QP, ns), F32)], axis=0).astype(BF16)
        cbm = _dot_nt(cm, bmp)
        yd = jnp.zeros((QP, SSD_COLS), F32)
        for s in range(steps):
            yd = yd + (ex[(4 + s) * QP:(5 + s) * QP] * cbm[:, s:s + 1]) * dtx[s:s + 1, :]
        h0g = h0_ref[0, gl, :]
        yo = _dot_nt(cm, h0g.astype(BF16)) * ecum_x
        y = (yd + yo + dexp_ref[:, gl] * xs_g) * _silu(z_ref[0, :, gl])
        ms = jnp.mean(y * y, axis=-1, keepdims=True)
        y_ref[0, :, gl] = (y * lax.rsqrt(ms + EPS)) * ng_ref[:, gl]
        tm = jnp.concatenate([xs_g * dtw_x, el_x[0:1], jnp.zeros((LANE - QP - 1, SSD_COLS), F32)], axis=0)
        tt = tm.T
        hf_ref[0, gl, :] = h0g * tt[:, QP:QP + 1] + _dot(tt.astype(BF16), bmp)


def _ssd_scan_sample(xc, dt, z, h0, a_log, d_skip, norm_g, steps):
    nb, _, cdim = xc.shape
    inner = z.shape[-1]
    nh = a_log.shape[0]
    rows_h = nh * SSD_HEADDIM
    dexp = jnp.repeat(d_skip, SSD_HEADDIM).reshape(1, inner)
    one = lambda shape: pl.BlockSpec(shape, lambda b: (0,) * len(shape))
    return pl.pallas_call(
        functools.partial(_ssd_scan_sample_kernel, steps=steps, inner=inner),
        grid=(nb,),
        in_specs=[pl.BlockSpec((1, QP, cdim), lambda b: (b, 0, 0)),
                  pl.BlockSpec((1, QP, LANE), lambda b: (b, 0, 0)),
                  pl.BlockSpec((1, QP, inner), lambda b: (b, 0, 0)),
                  pl.BlockSpec((1, rows_h, SSD_STATE), lambda b: (b, 0, 0)),
                  one((1, LANE)), one((1, inner)), one((1, inner)), one((LANE, inner))],
        out_specs=[pl.BlockSpec((1, QP, inner), lambda b: (b, 0, 0)),
                   pl.BlockSpec((1, rows_h, SSD_STATE), lambda b: (b, 0, 0))],
        out_shape=[jax.ShapeDtypeStruct((nb, QP, inner), F32),
                   jax.ShapeDtypeStruct((nb, rows_h, SSD_STATE), F32)],
        compiler_params=_cparams(1),
        name="ssd_scan_sample",
    )(xc, dt, z, h0, _pad_lanes(a_log), dexp, norm_g.reshape(1, inner), _head_expand(nh, SSD_HEADDIM))


TM = 256


def _to_steps(a, nb, steps):
    n = a.shape[-1]
    return a.reshape(nb, steps, n).transpose(1, 0, 2).reshape(steps * nb, n)


def _to_batch(a, nb, steps):
    n = a.shape[-1]
    return a.reshape(steps, nb, n).transpose(1, 0, 2).reshape(1, nb * steps, n)


def _pad_q(a, nb, steps):
    n = a.shape[-1]
    return jnp.pad(a.reshape(nb, steps, n), ((0, 0), (0, QP - steps), (0, 0)))


def kernel(x_prompt, x_sample, cache_k, cache_v, cache_kidx, state_s5_re, state_s5_im, state_sconv, state_ssd, state_ssd_conv, page_table, c_prompt, c_sample, rel_bias, ada_w, ada_b, norm_mix, norm_mlp, norm_final, attn_w_in, attn_w_out, s5_lam_re, s5_lam_im, s5_log_dt, s5_b_re, s5_b_im, s5_c_re, s5_c_im, s5_d, s5_w_glu, sc_w_in, sc_w_conv, sc_w_out, ssd_w_in, ssd_conv_w, ssd_conv_b, ssd_dt_bias, ssd_a_log, ssd_d, ssd_norm, ssd_w_out, mlp_w1, mlp_w2):
    bp, seq, d = x_prompt.shape
    nb, steps, _ = x_sample.shape
    depth = ada_w.shape[0]
    n_mixers = 4
    rs = nb * steps
    tm = min(TM, seq)

    rows = bp + nb
    c_all = jnp.pad(jnp.concatenate([c_prompt, c_sample], axis=0), ((0, (-rows) % SUBLANE), (0, 0)))
    ada = _ada(c_all, ada_w, ada_b)

    xp = x_prompt
    xs = x_sample.reshape(1, rs, d)
    outs = {name: [] for name in ("kp", "vp", "kip", "ks", "vs", "kis", "s5pr", "s5pi", "s5sr", "s5si",
                                  "scp", "scs", "ssdp", "ssdcp", "ssds", "ssdcs")}
    nq, nkv = N_HEADS * HEAD_DIM, N_KV_HEADS * HEAD_DIM
    for i in range(depth):
        m, j = i % n_mixers, i // n_mixers
        mp = [ada[i, :bp, k * d:(k + 1) * d].reshape(bp, 1, d) for k in range(6)]
        ms = [jnp.repeat(ada[i, bp:bp + nb, k * d:(k + 1) * d], steps, axis=0).reshape(1, rs, d)
              for k in range(6)]
        g_mix = norm_mix[i]
        if m == 0:
            w_in = attn_w_in[j]
            w_qkv = w_in[:, :nq + 2 * nkv].astype(BF16)
            n_idx = IDX_HEADS * IDX_DIM + LANE
            w_idx = jnp.pad(w_in[:, nq + 2 * nkv:], ((0, 0), (0, n_idx - (w_in.shape[1] - nq - 2 * nkv))))
            w_out = attn_w_out[j].astype(BF16)
            q, k, v, kw, qcat, kcat, kaug, vaug = _attn_proj(xp, mp[0], mp[1], g_mix, w_qkv, w_idx, tm)
            o = _dsa_prompt(q, qcat, kw, kcat, kaug, vaug, rel_bias, MAX_DISTANCE)
            xp = _outproj(o, xp, mp[2], w_out, tm)
            page = cache_k.shape[2]
            outs["kp"].append(k.reshape(bp, seq // page, page, N_KV_HEADS, HEAD_DIM))
            outs["vp"].append(v.reshape(bp, seq // page, page, N_KV_HEADS, HEAD_DIM))
            outs["kip"].append(kw[..., :IDX_DIM].reshape(bp, seq // page, page, IDX_DIM))
            q, k, v, kw, qcat, kcat, _, _ = _attn_proj(xs, ms[0], ms[1], g_mix, w_qkv, w_idx, rs)
            pps = math.gcd(PAGES_PER_STEP, page_table.shape[1])
            qc_p, kw_p = _pad_q(qcat.astype(F32), nb, steps), _pad_q(kw, nb, steps)
            scores = _dsa_sample_scores(page_table, qc_p, kw_p, cache_kidx, j, pps)
            o = _dsa_sample(page_table, _pad_q(q, nb, steps), qc_p, kw_p, _pad_q(kcat.astype(F32), nb, steps),
                            _pad_q(k, nb, steps), _pad_q(v, nb, steps), scores, cache_k, cache_v, rel_bias,
                            j, steps, pps)
            xs = _outproj(o[:, :steps].reshape(1, rs, nq), xs, ms[2], w_out, rs)
            outs["ks"].append(k.reshape(nb, steps, N_KV_HEADS, HEAD_DIM))
            outs["vs"].append(v.reshape(nb, steps, N_KV_HEADS, HEAD_DIM))
            outs["kis"].append(kw[..., :IDX_DIM].reshape(nb, steps, IDX_DIM))
        elif m == 1:
            ar, ai, wb, wc = _s5_weights(s5_lam_re[j], s5_lam_im[j], s5_log_dt[j], s5_b_re[j], s5_b_im[j],
                                         s5_c_re[j], s5_c_im[j])
            w_glu = s5_w_glu[j].astype(BF16)
            grp, nst = s5_lam_re.shape[1:]
            z, fr, fi = _s5_prompt(xp, mp[0], mp[1], g_mix, s5_d[j], ar, ai, wb, wc)
            xp = _glu(z, xp, mp[2], w_glu, tm)
            outs["s5pr"].append(fr.reshape(bp, grp, nst))
            outs["s5pi"].append(fi.reshape(bp, grp, nst))
            z, fr, fi = _s5_sample(_to_steps(xs, nb, steps), _to_steps(ms[0], nb, steps),
                                   _to_steps(ms[1], nb, steps), g_mix, s5_d[j], ar, ai, wb, wc,
                                   state_s5_re[j].reshape(nb, grp * nst), state_s5_im[j].reshape(nb, grp * nst),
                                   nb, steps)
            xs = _glu(_to_batch(z, nb, steps), xs, ms[2], w_glu, rs)
            outs["s5sr"].append(fr.reshape(nb, grp, nst))
            outs["s5si"].append(fi.reshape(nb, grp, nst))
        elif m == 2:
            w_in = sc_w_in[j].astype(BF16)
            w_out = sc_w_out[j].astype(BF16)
            width = sc_w_conv.shape[1]
            (p,) = _proj(xp, mp[0], mp[1], g_mix, w_in, (3 * d,), tm)
            xp, nbuf = _sconv_prompt(p, xp, mp[2], jnp.zeros((bp, width - 1, d), F32), sc_w_conv[j], w_out, tm)
            outs["scp"].append(nbuf)
            (p,) = _proj(xs, ms[0], ms[1], g_mix, w_in, (3 * d,), rs)
            buf = state_sconv[j].transpose(1, 0, 2).reshape((width - 1) * nb, d)
            y, nbuf = _sconv_sample(_to_steps(p, nb, steps), _to_steps(xs, nb, steps), _to_steps(ms[2], nb, steps),
                                    buf, sc_w_conv[j], w_out, nb, steps)
            xs = _to_batch(y, nb, steps)
            outs["scs"].append(nbuf.reshape(width - 1, nb, d).transpose(1, 0, 2))
        else:
            inner = ssd_norm.shape[1]
            cdim = ssd_conv_w.shape[2]
            nh = ssd_dt_bias.shape[1]
            width = ssd_conv_w.shape[1]
            w_in = jnp.pad(ssd_w_in[j], ((0, 0), (0, LANE - nh))).astype(BF16)
            w_out = ssd_w_out[j].astype(BF16)
            z, xbc, dtr = _proj(xp, mp[0], mp[1], g_mix, w_in, (inner, cdim, LANE), tm)
            y, nbuf, hf = _ssd_prompt(z, xbc, dtr, ssd_conv_w[j], ssd_conv_b[j], ssd_dt_bias[j], ssd_a_log[j],
                                      ssd_d[j], ssd_norm[j])
            xp = _outproj(y, xp, mp[2], w_out, tm)
            outs["ssdp"].append(hf)
            outs["ssdcp"].append(nbuf)
            z, xbc, dtr = _proj(xs, ms[0], ms[1], g_mix, w_in, (inner, cdim, LANE), rs)
            buf = state_ssd_conv[j].transpose(1, 0, 2).reshape((width - 1) * nb, cdim)
            xc, dt, nbuf = _ssd_prep_sample(_to_steps(xbc, nb, steps), buf, _to_steps(dtr, nb, steps),
                                            ssd_conv_w[j], ssd_conv_b[j], ssd_dt_bias[j], nb, steps)
            y, hf = _ssd_scan_sample(_pad_q(_to_batch(xc, nb, steps), nb, steps),
                                     _pad_q(_to_batch(dt, nb, steps), nb, steps), _pad_q(z, nb, steps),
                                     state_ssd[j].reshape(nb, nh * SSD_HEADDIM, SSD_STATE),
                                     ssd_a_log[j], ssd_d[j], ssd_norm[j], steps)
            xs = _outproj(y[:, :steps].reshape(1, rs, inner), xs, ms[2], w_out, rs)
            outs["ssds"].append(hf.reshape(nb, nh, SSD_HEADDIM, SSD_STATE))
            outs["ssdcs"].append(nbuf.reshape(width - 1, nb, cdim).transpose(1, 0, 2))
        w1 = mlp_w1[i].astype(BF16)
        w2 = mlp_w2[i].astype(BF16)
        fin = i == depth - 1
        xp = _mlp(xp, mp[3], mp[4], mp[5], norm_mlp[i], w1, w2, norm_final, fin, tm)
        xs = _mlp(xs, ms[3], ms[4], ms[5], norm_mlp[i], w1, w2, norm_final, fin, rs)
    st = jnp.stack
    return (xp, xs.reshape(nb, steps, d), st(outs["kp"]), st(outs["vp"]), st(outs["kip"]),
            st(outs["ks"]), st(outs["vs"]), st(outs["kis"]), st(outs["s5pr"]), st(outs["s5pi"]),
            st(outs["s5sr"]), st(outs["s5si"]), st(outs["scp"]), st(outs["scs"]),
            st(outs["ssdp"]), st(outs["ssdcp"]), st(outs["ssds"]), st(outs["ssdcs"]))
```

```python
import functools
import math

import jax
import jax.numpy as jnp
import numpy as np
from jax import lax
from jax.experimental import pallas as pl
from jax.experimental.pallas import tpu as pltpu

F32 = jnp.float32
BF16 = jnp.bfloat16
I32 = jnp.int32

EPS = 1e-6
N_HEADS = 16
HEAD_DIM = 64
N_KV_HEADS = 4
IDX_HEADS = 8
IDX_DIM = 64
TOPK_MAX = 256
N_BUCKETS = 32
MAX_DISTANCE = 128
S5_GROUP = 16
S5_STATE = 64
SSD_HEADDIM = 64
SSD_GROUPS = 4
SSD_STATE = 128
SSD_CHUNK = 128

LANE = 128
SUBLANE = 8
VMEM_LIMIT = 56 * 1024 * 1024
NEG_BIG = -1e30
INT_MIN = -2147483648


def _cparams(n_axes):
    return pltpu.CompilerParams(dimension_semantics=("arbitrary",) * n_axes,
                                vmem_limit_bytes=VMEM_LIMIT)


def _dot(a, b):
    return jnp.dot(a, b, preferred_element_type=F32)


def _dot_nt(a, b):
    return lax.dot_general(a, b, (((1,), (1,)), ((), ())), preferred_element_type=F32)


def _split(a):
    hi = a.astype(BF16)
    lo = (a - hi.astype(F32)).astype(BF16)
    return hi, lo


def _dot3(a, b):
    ah, al = _split(a)
    bh, bl = _split(b)
    return _dot(ah, bh) + (_dot(al, bh) + _dot(ah, bl))


def _rms_mod(x, g, shift, scale):
    y = x * lax.rsqrt(jnp.mean(x * x, axis=-1, keepdims=True) + EPS)
    return (y * g) * (1.0 + scale) + shift


def _sigmoid(x):
    return 1.0 / (1.0 + jnp.exp(-x))


def _silu(x):
    return x * _sigmoid(x)


def _row_block(arr, tm):
    n = arr.shape[-1]
    if arr.shape[1] == 1:
        return pl.BlockSpec((1, 1, n), lambda b, i: (b, 0, 0))
    return pl.BlockSpec((1, tm, n), lambda b, i: (b, i, 0))


def _const2(shape):
    return pl.BlockSpec(shape, lambda b, i: (0,) * len(shape))


def _ada_kernel(c_ref, w_ref, b_ref, o_ref):
    o_ref[0] = _dot3(_silu(c_ref[...]), w_ref[0]) + b_ref[0]


def _ada(c_all, ada_w, ada_b):
    depth, d, n = ada_w.shape
    rows = c_all.shape[0]
    tn = 1536
    return pl.pallas_call(
        _ada_kernel,
        grid=(depth, n // tn),
        in_specs=[pl.BlockSpec((rows, d), lambda l, j: (0, 0)),
                  pl.BlockSpec((1, d, tn), lambda l, j: (l, 0, j)),
                  pl.BlockSpec((1, 1, tn), lambda l, j: (l, 0, j))],
        out_specs=pl.BlockSpec((1, rows, tn), lambda l, j: (l, 0, j)),
        out_shape=jax.ShapeDtypeStruct((depth, rows, n), F32),
        compiler_params=_cparams(2),
        name="ada",
    )(c_all, ada_w, ada_b.reshape(depth, 1, n))


def _proj_kernel(x_ref, sh_ref, sc_ref, g_ref, w_ref, *o_refs, splits):
    h = _rms_mod(x_ref[0], g_ref[...], sh_ref[0], sc_ref[0]).astype(BF16)
    off = 0
    for o_ref, n in zip(o_refs, splits):
        o_ref[0] = _dot(h, w_ref[:, off:off + n])
        off += n


def _proj(x, shift, scale, g, w_bf16, splits, tm):
    bsz, seq, d = x.shape
    n = w_bf16.shape[1]
    assert sum(splits) == n and seq % tm == 0
    return pl.pallas_call(
        functools.partial(_proj_kernel, splits=tuple(splits)),
        grid=(bsz, seq // tm),
        in_specs=[pl.BlockSpec((1, tm, d), lambda b, i: (b, i, 0)),
                  _row_block(shift, tm), _row_block(scale, tm),
                  _const2((1, d)), _const2((d, n))],
        out_specs=[pl.BlockSpec((1, tm, s), lambda b, i: (b, i, 0)) for s in splits],
        out_shape=[jax.ShapeDtypeStruct((bsz, seq, s), F32) for s in splits],
        compiler_params=_cparams(2),
        name="proj",
    )(x, shift, scale, g.reshape(1, d), w_bf16)


def _aug_heads(t, col):
    rows = t.shape[0]
    lane = lax.broadcasted_iota(I32, (rows, LANE - HEAD_DIM), 1)
    extra = jnp.where(lane == 0, col, 0.0).astype(F32)
    parts = []
    for g in range(N_KV_HEADS):
        parts += [t[:, g * HEAD_DIM:(g + 1) * HEAD_DIM], extra]
    return jnp.concatenate(parts, axis=1).astype(BF16)


def _attn_proj_kernel(x_ref, sh_ref, sc_ref, g_ref, w_ref, wi_ref,
                      q_ref, k_ref, v_ref, kw_ref, qcat_ref, kcat_ref, kaug_ref, vaug_ref,
                      *, nq, nkv, nqi, key_major):
    h = _rms_mod(x_ref[0], g_ref[...], sh_ref[0], sc_ref[0])
    hb = h.astype(BF16)
    q_ref[0] = _dot(hb, w_ref[:, 0:nq]) * (HEAD_DIM ** -0.5)
    k = _dot(hb, w_ref[:, nq:nq + nkv])
    v = _dot(hb, w_ref[:, nq + nkv:nq + 2 * nkv])
    k_ref[0] = k
    v_ref[0] = v
    r = _dot3(h, wi_ref[...])
    kw = r[:, nqi:nqi + LANE]
    kw_ref[0] = kw
    zero = jnp.zeros((h.shape[0], IDX_DIM), BF16)
    parts = []
    for hh in range(IDX_HEADS):
        ah, al = _split(r[:, hh * IDX_DIM:(hh + 1) * IDX_DIM])
        parts += [ah, al, ah, zero]
    qcat_ref[0] = jnp.concatenate(parts, axis=1)
    kcat = _idx_rhs(kw[:, 0:IDX_DIM])
    kaug = _aug_heads(k, -1.0)
    vaug_ref[0] = _aug_heads(v, 1.0)
    if key_major:
        kcat_ref[0, 0] = kcat.astype(F32).T.astype(BF16)
        kaug_ref[0, 0] = kaug.astype(F32).T.astype(BF16)
    else:
        kcat_ref[0] = kcat
        kaug_ref[0] = kaug


def _attn_proj(x, shift, scale, g, w_qkv_bf16, w_idx, tm, key_major):
    bsz, seq, d = x.shape
    nq = N_HEADS * HEAD_DIM
    nkv = N_KV_HEADS * HEAD_DIM
    nqi = IDX_HEADS * IDX_DIM
    rows = lambda n, dt: (pl.BlockSpec((1, tm, n), lambda b, i: (b, i, 0)), jax.ShapeDtypeStruct((bsz, seq, n), dt))
    cols = lambda n, dt: (pl.BlockSpec((1, 1, n, tm), lambda b, i: (b, i, 0, 0)),
                          jax.ShapeDtypeStruct((bsz, seq // tm, n, tm), dt))
    keys = cols if key_major else rows
    outs = [rows(nq, F32), rows(nkv, F32), rows(nkv, F32), rows(LANE, F32), rows(4 * nqi, BF16),
            keys(4 * IDX_DIM, BF16), keys(N_KV_HEADS * LANE, BF16), rows(N_KV_HEADS * LANE, BF16)]
    return pl.pallas_call(
        functools.partial(_attn_proj_kernel, nq=nq, nkv=nkv, nqi=nqi, key_major=key_major),
        grid=(bsz, seq // tm),
        in_specs=[pl.BlockSpec((1, tm, d), lambda b, i: (b, i, 0)),
                  _row_block(shift, tm), _row_block(scale, tm),
                  _const2((1, d)), _const2(w_qkv_bf16.shape), _const2(w_idx.shape)],
        out_specs=[o[0] for o in outs],
        out_shape=[o[1] for o in outs],
        compiler_params=_cparams(2),
        name="attn_proj",
    )(x, shift, scale, g.reshape(1, d), w_qkv_bf16, w_idx)


def _outproj_kernel(o_ref, x_ref, gate_ref, w_ref, y_ref):
    y_ref[0] = x_ref[0] + gate_ref[0] * _dot(o_ref[0].astype(BF16), w_ref[...])


def _outproj(o, x, gate, w_bf16, tm):
    bsz, seq, d = x.shape
    k = o.shape[-1]
    return pl.pallas_call(
        _outproj_kernel,
        grid=(bsz, seq // tm),
        in_specs=[pl.BlockSpec((1, tm, k), lambda b, i: (b, i, 0)),
                  pl.BlockSpec((1, tm, d), lambda b, i: (b, i, 0)),
                  _row_block(gate, tm), _const2((k, d))],
        out_specs=pl.BlockSpec((1, tm, d), lambda b, i: (b, i, 0)),
        out_shape=jax.ShapeDtypeStruct((bsz, seq, d), F32),
        compiler_params=_cparams(2),
        name="outproj",
    )(o, x, gate, w_bf16)


def _mlp_kernel(x_ref, sh_ref, sc_ref, gate_ref, g_ref, w1_ref, w2_ref, gf_ref, y_ref,
                *, ff_chunk, final_norm):
    x = x_ref[0]
    h = _rms_mod(x, g_ref[...], sh_ref[0], sc_ref[0]).astype(BF16)
    dff = w1_ref.shape[1]
    acc = jnp.zeros(x.shape, F32)
    for c in range(dff // ff_chunk):
        a = jnp.maximum(_dot(h, w1_ref[:, c * ff_chunk:(c + 1) * ff_chunk]), 0.0)
        acc = acc + _dot((a * a).astype(BF16), w2_ref[c * ff_chunk:(c + 1) * ff_chunk, :])
    y = x + gate_ref[0] * acc
    if final_norm:
        y = (y * lax.rsqrt(jnp.mean(y * y, axis=-1, keepdims=True) + EPS)) * gf_ref[...]
    y_ref[0] = y


def _mlp(x, shift, scale, gate, g, w1_bf16, w2_bf16, g_final, final_norm, tm):
    bsz, seq, d = x.shape
    dff = w1_bf16.shape[1]
    return pl.pallas_call(
        functools.partial(_mlp_kernel, ff_chunk=1024, final_norm=final_norm),
        grid=(bsz, seq // tm),
        in_specs=[pl.BlockSpec((1, tm, d), lambda b, i: (b, i, 0)),
                  _row_block(shift, tm), _row_block(scale, tm), _row_block(gate, tm),
                  _const2((1, d)), _const2((d, dff)), _const2((dff, d)), _const2((1, d))],
        out_specs=pl.BlockSpec((1, tm, d), lambda b, i: (b, i, 0)),
        out_shape=jax.ShapeDtypeStruct((bsz, seq, d), F32),
        compiler_params=_cparams(2),
        name="mlp",
    )(x, shift, scale, gate, g.reshape(1, d), w1_bf16, w2_bf16, g_final.reshape(1, d))


S5_CH = 128
S5_BLK = (S5_CH // S5_GROUP) * S5_STATE


def _s5_disc_kernel(lr_ref, li_ref, ldt_ref, br_ref, bi_ref, ar_ref, ai_ref, bbr_ref, bbi_ref):
    lr, li = lr_ref[...], li_ref[...]
    dt = jnp.exp(ldt_ref[...])
    mag = jnp.exp(lr * dt)
    ab_re, ab_im = mag * jnp.cos(li * dt), mag * jnp.sin(li * dt)
    den = lr * lr + li * li
    nr = ab_re - 1.0
    f_re = (nr * lr + ab_im * li) / den
    f_im = (ab_im * lr - nr * li) / den
    ar_ref[...] = ab_re
    ai_ref[...] = ab_im
    for c in range(br_ref.shape[0]):
        br, bi = br_ref[c], bi_ref[c]
        bbr_ref[c] = f_re * br - f_im * bi
        bbi_ref[c] = f_re * bi + f_im * br


def _s5_weights(lam_re, lam_im, log_dt, b_re, b_im, c_re, c_im):
    g, p = lam_re.shape
    gc = b_re.shape[-1]
    brt = jnp.moveaxis(b_re, 2, 0)
    bit = jnp.moveaxis(b_im, 2, 0)
    ar, ai, bbr, bbi = pl.pallas_call(
        _s5_disc_kernel,
        out_shape=[jax.ShapeDtypeStruct((g, p), F32), jax.ShapeDtypeStruct((g, p), F32),
                   jax.ShapeDtypeStruct((gc, g, p), F32), jax.ShapeDtypeStruct((gc, g, p), F32)],
        name="s5_disc",
    )(lam_re, lam_im, log_dt.reshape(g, 1), brt, bit)
    nblk = (g * gc) // S5_CH
    gpb = g // nblk
    eye = jnp.eye(gpb, dtype=F32)

    def bd_in(bb):
        t = jnp.moveaxis(bb, 0, 1).reshape(nblk, gpb, gc, p)
        return jnp.einsum('ngcp,gh->ngchp', t, eye).reshape(nblk, gpb * gc, gpb * p)

    def bd_out(cc):
        t = cc.reshape(nblk, gpb, gc, p)
        return jnp.einsum('ngcp,gh->ngphc', t, eye).reshape(nblk, gpb * p, gpb * gc)

    wb = jnp.concatenate([bd_in(bbr), bd_in(bbi)], axis=-1).astype(BF16)
    wc = jnp.concatenate([bd_out(c_re), bd_out(c_im)], axis=1).astype(BF16)
    return ar.reshape(nblk, 1, gpb * p), ai.reshape(nblk, 1, gpb * p), wb, wc


def _gelu_tanh(y):
    return 0.5 * y * (1.0 + jnp.tanh(math.sqrt(2.0 / math.pi) * (y + 0.044715 * (y * y * y))))


def _cmul(ar, ai, xr, xi):
    return ar * xr - ai * xi, ar * xi + ai * xr


def _s5_prompt_kernel(x_ref, sh_ref, sc_ref, g_ref, d_ref, ar_ref, ai_ref, wb_ref, wc_ref,
                      perm_ref, unperm_ref, z_ref, fr_ref, fi_ref, u_s, br_s, bi_s, st_r, st_i,
                      *, nseg, seg):
    i = pl.program_id(1)
    nblk = wb_ref.shape[0]
    blk = ar_ref.shape[-1]

    @pl.when(i == 0)
    def _():
        st_r[...] = jnp.zeros(st_r.shape, F32)
        st_i[...] = jnp.zeros(st_i.shape, F32)

    u_s[...] = _dot_sel_lhs(perm_ref[...], _rms_mod(x_ref[0], g_ref[...], sh_ref[0], sc_ref[0]))

    for c in range(nblk):
        lo, hi = c * S5_CH, (c + 1) * S5_CH
        uc = u_s[:, lo:hi]
        bu = _dot(uc.astype(BF16), wb_ref[c])
        br_s[...] = bu[:, :blk]
        bi_s[...] = bu[:, blk:]
        ar = jnp.broadcast_to(ar_ref[c], (nseg, blk))
        ai = jnp.broadcast_to(ai_ref[c], (nseg, blk))

        def local(j, carry):
            xr, xi = carry
            r0 = pl.multiple_of(j * nseg, nseg)
            pr, pi = _cmul(ar, ai, xr, xi)
            nr = pr + br_s[pl.ds(r0, nseg), :]
            ni = pi + bi_s[pl.ds(r0, nseg), :]
            br_s[pl.ds(r0, nseg), :] = nr
            bi_s[pl.ds(r0, nseg), :] = ni
            return nr, ni

        zero = jnp.zeros((nseg, blk), F32)
        fr, fi = lax.fori_loop(0, seg, local, (zero, zero))

        pr, pi = ar_ref[c], ai_ref[c]
        for _ in range(int(math.log2(seg))):
            pr, pi = _cmul(pr, pi, pr, pi)
        cr, ci = st_r[c], st_i[c]
        rows_r, rows_i = [], []
        for s in range(nseg):
            rows_r.append(cr)
            rows_i.append(ci)
            tr, ti = _cmul(pr, pi, cr, ci)
            cr, ci = tr + fr[s:s + 1], ti + fi[s:s + 1]
        st_r[c] = cr
        st_i[c] = ci
        fr_ref[0, c] = cr
        fi_ref[0, c] = ci
        dr, di = _cmul(ar, ai, jnp.concatenate(rows_r, axis=0), jnp.concatenate(rows_i, axis=0))

        def fix(j, carry):
            dr, di = carry
            r0 = pl.multiple_of(j * nseg, nseg)
            br_s[pl.ds(r0, nseg), :] = br_s[pl.ds(r0, nseg), :] + dr
            bi_s[pl.ds(r0, nseg), :] = bi_s[pl.ds(r0, nseg), :] + di
            return _cmul(ar, ai, dr, di)

        lax.fori_loop(0, seg, fix, (dr, di))

        y = (_dot(br_s[...].astype(BF16), wc_ref[c, :blk, :])
             - _dot(bi_s[...].astype(BF16), wc_ref[c, blk:, :]))
        y = y + d_ref[:, lo:hi] * uc
        u_s[:, lo:hi] = _gelu_tanh(y)

    z_ref[0] = _dot_sel_lhs(unperm_ref[...], u_s[...])


def _s5_prompt(x, shift, scale, g, d_skip, ar, ai, wb, wc):
    bsz, seq, d = x.shape
    nblk, _, blk = ar.shape
    nseg, seg = SUBLANE, 32
    tm = nseg * seg
    assert seq % tm == 0
    perm = np.zeros((tm, tm), np.float32)
    for s in range(nseg):
        for j in range(seg):
            perm[j * nseg + s, s * seg + j] = 1.0
    unperm = jnp.asarray(perm.T, BF16)
    perm = jnp.asarray(perm, BF16)
    z, fr, fi = pl.pallas_call(
        functools.partial(_s5_prompt_kernel, nseg=nseg, seg=seg),
        grid=(bsz, seq // tm),
        in_specs=[pl.BlockSpec((1, tm, d), lambda b, i: (b, i, 0)),
                  _row_block(shift, tm), _row_block(scale, tm),
                  _const2((1, d)), _const2((1, d)),
                  _const2(ar.shape), _const2(ai.shape), _const2(wb.shape), _const2(wc.shape),
                  _const2((tm, tm)), _const2((tm, tm))],
        out_specs=[pl.BlockSpec((1, tm, d), lambda b, i: (b, i, 0)),
                   pl.BlockSpec((1, nblk, 1, blk), lambda b, i: (b, 0, 0, 0)),
                   pl.BlockSpec((1, nblk, 1, blk), lambda b, i: (b, 0, 0, 0))],
        out_shape=[jax.ShapeDtypeStruct((bsz, seq, d), F32),
                   jax.ShapeDtypeStruct((bsz, nblk, 1, blk), F32),
                   jax.ShapeDtypeStruct((bsz, nblk, 1, blk), F32)],
        scratch_shapes=[pltpu.VMEM((tm, d), F32), pltpu.VMEM((tm, blk), F32), pltpu.VMEM((tm, blk), F32),
                        pltpu.VMEM((nblk, 1, blk), F32), pltpu.VMEM((nblk, 1, blk), F32)],
        compiler_params=_cparams(2),
        name="s5_prompt",
    )(x, shift, scale, g.reshape(1, d), d_skip.reshape(1, d), ar, ai, wb, wc, perm, unperm)
    return z, fr, fi


def _s5_sample_kernel(x_ref, sh_ref, sc_ref, g_ref, d_ref, ar_ref, ai_ref, wb_ref, wc_ref,
                      s0r_ref, s0i_ref, z_ref, fr_ref, fi_ref, *, nb, steps):
    nblk = wb_ref.shape[0]
    blk = ar_ref.shape[-1]
    u = _rms_mod(x_ref[...], g_ref[...], sh_ref[...], sc_ref[...])
    for c in range(nblk):
        lo, hi = c * S5_CH, (c + 1) * S5_CH
        uc = u[:, lo:hi]
        bu = _dot(uc.astype(BF16), wb_ref[c])
        ar = jnp.broadcast_to(ar_ref[c], (nb, blk))
        ai = jnp.broadcast_to(ai_ref[c], (nb, blk))
        xr, xi = s0r_ref[:, c * blk:(c + 1) * blk], s0i_ref[:, c * blk:(c + 1) * blk]
        xrs, xis = [], []
        for t in range(steps):
            pr, pi = _cmul(ar, ai, xr, xi)
            xr = pr + bu[t * nb:(t + 1) * nb, :blk]
            xi = pi + bu[t * nb:(t + 1) * nb, blk:]
            xrs.append(xr)
            xis.append(xi)
        fr_ref[:, c * blk:(c + 1) * blk] = xr
        fi_ref[:, c * blk:(c + 1) * blk] = xi
        y = (_dot(jnp.concatenate(xrs, axis=0).astype(BF16), wc_ref[c, :blk, :])
             - _dot(jnp.concatenate(xis, axis=0).astype(BF16), wc_ref[c, blk:, :]))
        y = y + d_ref[:, lo:hi] * uc
        z_ref[:, lo:hi] = _gelu_tanh(y)


def _s5_sample(x_tb, shift_tb, scale_tb, g, d_skip, ar, ai, wb, wc, s0r, s0i, nb, steps):
    rows, d = x_tb.shape
    nblk, _, blk = ar.shape
    return pl.pallas_call(
        functools.partial(_s5_sample_kernel, nb=nb, steps=steps),
        out_shape=[jax.ShapeDtypeStruct((rows, d), F32),
                   jax.ShapeDtypeStruct((nb, nblk * blk), F32),
                   jax.ShapeDtypeStruct((nb, nblk * blk), F32)],
        compiler_params=pltpu.CompilerParams(vmem_limit_bytes=VMEM_LIMIT),
        name="s5_sample",
    )(x_tb, shift_tb, scale_tb, g.reshape(1, d), d_skip.reshape(1, d), ar, ai, wb, wc, s0r, s0i)


def _glu_kernel(z_ref, x_ref, gate_ref, w_ref, y_ref):
    d = x_ref.shape[-1]
    zb = z_ref[0].astype(BF16)
    ga = _dot(zb, w_ref[:, :d])
    gb = _dot(zb, w_ref[:, d:])
    y_ref[0] = x_ref[0] + gate_ref[0] * (ga * _sigmoid(gb))


def _glu(z, x, gate, w_bf16, tm):
    bsz, seq, d = x.shape
    return pl.pallas_call(
        _glu_kernel,
        grid=(bsz, seq // tm),
        in_specs=[pl.BlockSpec((1, tm, d), lambda b, i: (b, i, 0)),
                  pl.BlockSpec((1, tm, d), lambda b, i: (b, i, 0)),
                  _row_block(gate, tm), _const2(w_bf16.shape)],
        out_specs=pl.BlockSpec((1, tm, d), lambda b, i: (b, i, 0)),
        out_shape=jax.ShapeDtypeStruct((bsz, seq, d), F32),
        compiler_params=_cparams(2),
        name="glu",
    )(z, x, gate, w_bf16)


def _shift_rows(u, prev, k):
    if k == 0:
        return u
    rolled = pltpu.roll(u, k, 0)
    head = pltpu.roll(prev, k, 0)
    row = lax.broadcasted_iota(I32, (SUBLANE, u.shape[1]), 0)
    first = jnp.where(row < k, head, rolled[:SUBLANE])
    return jnp.concatenate([first, rolled[SUBLANE:]], axis=0)


def _sconv_prompt_kernel(p_ref, x_ref, gate_ref, buf_ref, wc_ref, wo_ref, y_ref, nb_ref, prev_s,
                         *, width):
    i = pl.program_id(1)
    d = x_ref.shape[-1]

    @pl.when(i == 0)
    def _():
        prev_s[...] = jnp.zeros(prev_s.shape, F32)
        prev_s[SUBLANE - (width - 1):, :] = buf_ref[0]

    p = p_ref[0]
    gb, gc, xh = p[:, :d], p[:, d:2 * d], p[:, 2 * d:]
    u = gc * xh
    prev = prev_s[...]
    conv = u * wc_ref[width - 1:width, :]
    for k in range(1, width):
        conv = conv + _shift_rows(u, prev, k) * wc_ref[width - 1 - k:width - k, :]
    prev_s[...] = u[u.shape[0] - SUBLANE:, :]
    nb_ref[0] = u[u.shape[0] - (width - 1):, :]
    y_ref[0] = x_ref[0] + gate_ref[0] * _dot((gb * conv).astype(BF16), wo_ref[...])


def _sconv_prompt(p, x, gate, buf, w_conv, w_out_bf16, tm):
    bsz, seq, d = x.shape
    width = w_conv.shape[0]
    return pl.pallas_call(
        functools.partial(_sconv_prompt_kernel, width=width),
        grid=(bsz, seq // tm),
        in_specs=[pl.BlockSpec((1, tm, 3 * d), lambda b, i: (b, i, 0)),
                  pl.BlockSpec((1, tm, d), lambda b, i: (b, i, 0)),
                  _row_block(gate, tm),
                  pl.BlockSpec((1, width - 1, d), lambda b, i: (b, 0, 0)),
                  _const2((width, d)), _const2((d, d))],
        out_specs=[pl.BlockSpec((1, tm, d), lambda b, i: (b, i, 0)),
                   pl.BlockSpec((1, width - 1, d), lambda b, i: (b, 0, 0))],
        out_shape=[jax.ShapeDtypeStruct((bsz, seq, d), F32),
                   jax.ShapeDtypeStruct((bsz, width - 1, d), F32)],
        scratch_shapes=[pltpu.VMEM((SUBLANE, d), F32)],
        compiler_params=_cparams(2),
        name="sconv_prompt",
    )(p, x, gate, buf, w_conv, w_out_bf16)


def _sconv_sample_kernel(p_ref, x_ref, gate_ref, buf_ref, wc_ref, wo_ref, y_ref, nb_ref,
                         *, width, nb, steps):
    d = x_ref.shape[-1]
    p = p_ref[...]
    gb, gc, xh = p[:, :d], p[:, d:2 * d], p[:, 2 * d:]
    u = gc * xh
    ext = jnp.concatenate([buf_ref[...], u], axis=0)
    conv = ext[0:steps * nb] * wc_ref[0:1, :]
    for k in range(1, width):
        conv = conv + ext[k * nb:(k + steps) * nb] * wc_ref[k:k + 1, :]
    nb_ref[...] = ext[steps * nb:]
    y_ref[...] = x_ref[...] + gate_ref[...] * _dot((gb * conv).astype(BF16), wo_ref[...])


def _sconv_sample(p_tb, x_tb, gate_tb, buf_kb, w_conv, w_out_bf16, nb, steps):
    rows, d = x_tb.shape
    width = w_conv.shape[0]
    return pl.pallas_call(
        functools.partial(_sconv_sample_kernel, width=width, nb=nb, steps=steps),
        out_shape=[jax.ShapeDtypeStruct((rows, d), F32),
                   jax.ShapeDtypeStruct(((width - 1) * nb, d), F32)],
        compiler_params=pltpu.CompilerParams(vmem_limit_bytes=VMEM_LIMIT),
        name="sconv_sample",
    )(p_tb, x_tb, gate_tb, buf_kb, w_conv, w_out_bf16)


def _sortable(s):
    bits = lax.bitcast_convert_type(s + 0.0, I32)
    return bits ^ ((bits >> 31) & 0x7FFFFFFF)


def _stack_heads(qcat):
    w = 4 * IDX_DIM
    return jnp.concatenate([qcat[:, h * w:(h + 1) * w] for h in range(IDX_HEADS)], axis=0)


def _idx_rhs(ki):
    kh, kl = _split(ki)
    return jnp.concatenate([kh, kh, kl, jnp.zeros(kh.shape, BF16)], axis=1)


def _idx_score(dts, wcols, nrows):
    s = wcols[0] * jnp.maximum(dts[0:nrows], 0.0)
    for h in range(1, IDX_HEADS):
        s = s + wcols[h] * jnp.maximum(dts[h * nrows:(h + 1) * nrows], 0.0)
    return s


def _idx_wcols(kw):
    sc = (IDX_HEADS ** -0.5) * (IDX_DIM ** -0.5)
    return [kw[:, IDX_DIM + h:IDX_DIM + h + 1] * sc for h in range(IDX_HEADS)]


def _topk_select(count, rows, lanes, topk, idx_bits, stash=lambda v: (lambda: v)):
    wide = lambda v: jnp.broadcast_to(v, (rows, lanes))

    def bit_step(it, tu):
        mask = jnp.left_shift(jnp.int32(1), 31 - it)
        cand_u = tu | mask
        cand = stash(wide(cand_u ^ INT_MIN))
        cnt = count(lambda c, key, idx: jnp.where(key >= cand(), 1.0, 0.0))
        return jnp.where(cnt >= topk, cand_u, tu)

    tu = lax.fori_loop(0, 32, bit_step, jnp.zeros((rows, 1), I32))
    t = tu ^ INT_MIN
    tw = wide(t)
    n_gt = count(lambda c, key, idx: jnp.where(key > tw, 1.0, 0.0))
    n_ge = count(lambda c, key, idx: jnp.where(key >= tw, 1.0, 0.0))
    need = topk - n_gt
    tied = jnp.max(jnp.where(t != INT_MIN, n_ge - topk, 0.0)) > 0.0

    def tie_search(_):
        def idx_step(it, j):
            cand = j | jnp.left_shift(jnp.int32(1), idx_bits - 1 - it)
            cw = wide(cand)
            cnt = count(lambda c, key, idx: jnp.where((key == tw) & (idx < cw), 1.0, 0.0))
            return jnp.where(cnt < need, cand, j)
        return lax.fori_loop(0, idx_bits, idx_step, jnp.zeros((rows, 1), I32))

    j = lax.cond(tied, tie_search, lambda _: jnp.full((rows, 1), 2 ** idx_bits, I32), 0)
    return t, j


def _dsa_prompt_kernel(q_ref, qcat_ref, kwq_ref, kcat_ref, kaug_ref, vaug_ref, bias_ref, o_ref,
                       key_s, madd_s, wbc_s, cand_s, qa_s, qm_s, mx_s, acc_s, *, tq, topk, idx_bits):
    i = pl.program_id(1)
    nk = i + 1
    nk2 = (nk + 1) // 2
    r = N_HEADS // N_KV_HEADS

    for h, w in enumerate(_idx_wcols(kwq_ref[0])):
        wbc_s[h] = jnp.broadcast_to(w, (tq, tq))
    qrow = lax.broadcasted_iota(I32, (tq, tq), 0)
    kcol = lax.broadcasted_iota(I32, (tq, tq), 1)
    qpos = i * tq + qrow
    wq = 4 * IDX_DIM

    def score_pair(c2, _):
        kc = kcat_ref[0, c2]
        s = None
        for h in range(IDX_HEADS):
            w = wbc_s[h]
            d = jnp.concatenate([w, w], axis=1) * jnp.maximum(_dot(qcat_ref[0, :, h * wq:(h + 1) * wq], kc), 0.0)
            s = d if s is None else s + d
        for u in range(2):
            c = 2 * c2 + u
            key_s[c] = jnp.where(c * tq + kcol <= qpos, _sortable(s[:, u * tq:(u + 1) * tq]), INT_MIN)
        return 0

    lax.fori_loop(0, nk2, score_pair, 0)

    def count(fn):
        def body(c2, acc):
            c = 2 * c2
            acc = acc + fn(c, key_s[c], c * tq + kcol)
            return acc + fn(c + 1, key_s[c + 1], (c + 1) * tq + kcol)
        acc = lax.fori_loop(0, nk2, body, jnp.zeros((tq, tq), F32))
        return jnp.sum(acc, axis=1, keepdims=True)

    def stash(v):
        cand_s[...] = v
        return lambda: cand_s[...]

    t, j = _topk_select(count, tq, tq, float(topk), idx_bits, stash)
    tw = jnp.broadcast_to(t, (tq, tq))
    jw = jnp.broadcast_to(j, (tq, tq))

    def mask_pair(c2, _):
        for u in range(2):
            c = 2 * c2 + u
            key = key_s[c]
            sel = (key > tw) | ((key == tw) & (c * tq + kcol <= jw))
            madd_s[c] = jnp.where(sel & (key != INT_MIN), 0.0, NEG_BIG)
        return 0

    lax.fori_loop(0, nk2, mask_pair, 0)

    q = q_ref[0]
    zpad = jnp.zeros((tq, LANE - HEAD_DIM), F32)
    qa = []
    for g in range(N_KV_HEADS):
        qa.append(jnp.concatenate(
            [jnp.concatenate([q[:, (g * r + u) * HEAD_DIM:(g * r + u + 1) * HEAD_DIM], zpad], axis=1)
             for u in range(r)], axis=0))
        qa_s[g] = qa[g].astype(BF16)
    mx_s[...] = jnp.full(mx_s.shape, NEG_BIG, F32)

    def pair_mask(c2, near, g):
        halves = []
        for u in range(2):
            c = 2 * c2 + u
            m = madd_s[c]
            if near:
                back = nk - 1 - c
                carries = (back == 0) | ((back == 1) & (kcol > qrow))
                halves.append([m + jnp.where(carries, bias_ref[g * r + v], 0.0) for v in range(r)])
            else:
                halves.append([m] * r)
        return jnp.concatenate([jnp.concatenate([halves[0][v], halves[1][v]], axis=1) for v in range(r)], axis=0)

    def max_pair(c2, near):
        for g in range(N_KV_HEADS):
            lg = _dot(qa_s[g], kaug_ref[0, c2, g * LANE:(g + 1) * LANE, :]) + pair_mask(c2, near, g)
            mx_s[g] = jnp.maximum(mx_s[g], jnp.maximum(lg[:, :tq], lg[:, tq:]))

    def pv_pair(c2, near):
        r0 = pl.multiple_of(c2 * 2 * tq, 2 * tq)
        va = vaug_ref[0, pl.ds(r0, 2 * tq), :]
        for g in range(N_KV_HEADS):
            lg = _dot(qm_s[g], kaug_ref[0, c2, g * LANE:(g + 1) * LANE, :]) + pair_mask(c2, near, g)
            acc_s[g] = acc_s[g] + _dot(jnp.exp(lg).astype(BF16), va[:, g * LANE:(g + 1) * LANE])

    def all_blocks(pair_fn):
        def far(c2, _):
            pair_fn(c2, False)
            return 0

        lax.fori_loop(0, jnp.maximum(nk2 - 2, 0), far, 0)

        @pl.when(nk2 >= 2)
        def _():
            pair_fn(nk2 - 2, True)

        pair_fn(nk2 - 1, True)

    all_blocks(max_pair)
    lane_a = lax.broadcasted_iota(I32, (r * tq, LANE), 1)
    for g in range(N_KV_HEADS):
        shift = jnp.max(mx_s[g], axis=1, keepdims=True).astype(BF16).astype(F32)
        qm_s[g] = jnp.where(lane_a == HEAD_DIM, shift, qa[g]).astype(BF16)
    acc_s[...] = jnp.zeros(acc_s.shape, F32)
    all_blocks(pv_pair)
    for g in range(N_KV_HEADS):
        acc = acc_s[g]
        o = acc[:, 0:HEAD_DIM] / acc[:, HEAD_DIM:HEAD_DIM + 1]
        for u in range(r):
            h = g * r + u
            o_ref[0, :, h * HEAD_DIM:(h + 1) * HEAD_DIM] = o[u * tq:(u + 1) * tq, :]


def _t5_bucket(dist):
    n = jnp.maximum(dist, 0)
    max_exact = N_BUCKETS // 2
    nf = jnp.maximum(n, max_exact).astype(F32)
    large = max_exact + (jnp.log(nf / max_exact) / math.log(MAX_DISTANCE / max_exact)
                         * (N_BUCKETS - max_exact)).astype(I32)
    large = jnp.minimum(large, N_BUCKETS - 1)
    return jnp.where(n < max_exact, n, large)


def _bias_table(rel_bias, n):
    tab = rel_bias[_t5_bucket(jnp.arange(n, dtype=I32))]
    return (tab - rel_bias[N_BUCKETS - 1][None, :]).T


def _dsa_prompt(q, qcat, kw, kcat, kaug, vaug, rel_bias, tq):
    bsz, seq, nq = q.shape
    topk = min(TOPK_MAX, seq // 4)
    nkc = seq // tq
    r = N_HEADS // N_KV_HEADS
    assert tq == MAX_DISTANCE and seq % (2 * tq) == 0
    tab = _bias_table(rel_bias, tq)
    dmod = (np.arange(tq)[:, None] - np.arange(tq)[None, :]) % tq
    bias = tab[:, dmod]
    assert kcat.shape[1:] == (seq // (2 * tq), 4 * IDX_DIM, 2 * tq)
    row = lambda n: pl.BlockSpec((1, tq, n), lambda b, i: (b, i, 0))
    tiles = lambda a: pl.BlockSpec((1,) + a.shape[1:], lambda b, i: (b, 0, 0, 0))
    return pl.pallas_call(
        functools.partial(_dsa_prompt_kernel, tq=tq, topk=topk, idx_bits=int(math.log2(seq))),
        grid=(bsz, nkc),
        in_specs=[row(nq), row(qcat.shape[-1]), row(LANE), tiles(kcat), tiles(kaug),
                  pl.BlockSpec((1, seq, vaug.shape[-1]), lambda b, i: (b, 0, 0)),
                  _const2((N_HEADS, tq, tq))],
        out_specs=row(nq),
        out_shape=jax.ShapeDtypeStruct((bsz, seq, nq), F32),
        scratch_shapes=[pltpu.VMEM((nkc, tq, tq), I32), pltpu.VMEM((nkc, tq, tq), F32),
                        pltpu.VMEM((IDX_HEADS, tq, tq), F32), pltpu.VMEM((tq, tq), I32),
                        pltpu.VMEM((N_KV_HEADS, r * tq, LANE), BF16),
                        pltpu.VMEM((N_KV_HEADS, r * tq, LANE), BF16),
                        pltpu.VMEM((N_KV_HEADS, r * tq, tq), F32),
                        pltpu.VMEM((N_KV_HEADS, r * tq, LANE), F32)],
        compiler_params=_cparams(2),
        name="dsa_prompt",
    )(q, qcat, kw, kcat, kaug, vaug, bias)


QP = SUBLANE
PAGES_PER_STEP = 16


def _page_specs(shape4, layer, pps):
    def spec(u):
        return pl.BlockSpec((1, 1) + tuple(shape4[2:]),
                            lambda b, g, pt: (layer, pt[b, g * pps + u], 0, 0))
    return [spec(u) for u in range(pps)]


def _pages_t(cache):
    nd = cache.ndim
    t = jnp.transpose(cache, (0, 1) + tuple(range(3, nd)) + (2,))
    return t.reshape(t.shape[0], t.shape[1], -1, t.shape[-1])


def _dsa_sample_score_kernel(pt_ref, qcat_ref, kw_ref, *rest, pps):
    page_refs, o_ref = rest[:pps], rest[pps]
    lhs = _stack_heads(qcat_ref[0]).astype(BF16)
    wcols = _idx_wcols(kw_ref[0])
    kt = jnp.concatenate([page_refs[u][0, 0] for u in range(pps)], axis=1)
    kh, kl = _split(kt)
    rhs = jnp.concatenate([kh, kh, kl, jnp.zeros(kh.shape, BF16)], axis=0)
    o_ref[0] = _idx_score(_dot(lhs, rhs), wcols, QP)


def _dsa_sample_scores(page_table, qcat, kw, kidx_t, layer, pps):
    nb, n_pages = page_table.shape
    page = kidx_t.shape[-1]
    grid_spec = pltpu.PrefetchScalarGridSpec(
        num_scalar_prefetch=1,
        grid=(nb, n_pages // pps),
        in_specs=[pl.BlockSpec((1, QP, qcat.shape[-1]), lambda b, g, pt: (b, 0, 0)),
                  pl.BlockSpec((1, QP, LANE), lambda b, g, pt: (b, 0, 0))]
                 + _page_specs(kidx_t.shape, layer, pps),
        out_specs=pl.BlockSpec((1, QP, pps * page), lambda b, g, pt: (b, 0, g)))
    return pl.pallas_call(
        functools.partial(_dsa_sample_score_kernel, pps=pps),
        grid_spec=grid_spec,
        out_shape=jax.ShapeDtypeStruct((nb, QP, n_pages * page), F32),
        compiler_params=_cparams(2),
        name="dsa_sample_scores",
    )(page_table, qcat, kw, *([kidx_t] * pps))


def _dsa_sample_attend_kernel(pt_ref, q_ref, qcat_ref, kw_ref, kcn_ref, kn_ref, vn_ref, sc_all_ref, sc_ref,
                              blast_ref, bnew_ref, *rest, pps, n_pages, steps, topk, idx_bits):
    k_refs, v_refs, o_ref = rest[:pps], rest[pps:2 * pps], rest[2 * pps]
    keyn_s, t_s, j_s, lhs_s, m_s, l_s, acc_s = rest[2 * pps + 1:]
    g = pl.program_id(1)
    last = pl.num_programs(1) - 1
    page = k_refs[0].shape[-1]
    past = n_pages * page
    n = pps * page
    r = N_HEADS // N_KV_HEADS
    grow = r * QP

    def pad_t(a):
        return jnp.concatenate([a, jnp.zeros((page - QP, a.shape[1]), F32)], axis=0).T

    @pl.when(g == 0)
    def _():
        dts = _dot(_stack_heads(qcat_ref[0]).astype(BF16), pad_t(kcn_ref[0]).astype(BF16))
        s_new = _idx_score(dts, _idx_wcols(kw_ref[0]), QP)
        lane = lax.broadcasted_iota(I32, (QP, page), 1)
        row = lax.broadcasted_iota(I32, (QP, page), 0)
        key_new = jnp.where((lane <= row) & (lane < steps), _sortable(s_new), INT_MIN)
        keyn_s[...] = key_new
        keys = jnp.concatenate([_sortable(sc_all_ref[0]), key_new], axis=1)
        idx = lax.broadcasted_iota(I32, keys.shape, 1)
        count = lambda fn: jnp.sum(fn(0, keys, idx), axis=1, keepdims=True)
        t, j = _topk_select(count, QP, past + page, float(topk), idx_bits)
        t_s[...] = t
        j_s[...] = j
        m_s[...] = jnp.full(m_s.shape, NEG_BIG, F32)
        l_s[...] = jnp.zeros(l_s.shape, F32)
        acc_s[...] = jnp.zeros(acc_s.shape, F32)
        q = q_ref[0]
        for gg in range(N_KV_HEADS):
            lhs_s[gg] = jnp.concatenate(
                [q[:, (gg * r + u) * HEAD_DIM:(gg * r + u + 1) * HEAD_DIM] for u in range(r)], axis=0).astype(BF16)

    t, j = t_s[...], j_s[...]

    def select(key, idx):
        sel = (key > t) | ((key == t) & (idx <= j))
        return jnp.where(sel & (key != INT_MIN), 0.0, NEG_BIG)

    def attend(kt, vt, madd, bias):
        mrows = jnp.concatenate([madd] * r, axis=0)
        for gg in range(N_KV_HEADS):
            rows = slice(gg * grow, (gg + 1) * grow)
            feat = slice(gg * HEAD_DIM, (gg + 1) * HEAD_DIM)
            lg = _dot(lhs_s[gg], kt[feat, :]) + mrows
            if bias is not None:
                lg = lg + bias[rows, :]
            m_old = m_s[rows, :]
            m_new = jnp.maximum(m_old, jnp.max(lg, axis=1, keepdims=True))
            p = jnp.exp(lg - m_new)
            alpha = jnp.exp(m_old - m_new)
            l_s[rows, :] = alpha * l_s[rows, :] + jnp.sum(p, axis=1, keepdims=True)
            acc_s[rows, :] = alpha * acc_s[rows, :] + _dot_nt(p.astype(BF16), vt[feat, :])
            m_s[rows, :] = m_new

    def pages(bias):
        kt = jnp.concatenate([k_refs[u][0, 0] for u in range(pps)], axis=1).astype(BF16)
        vt = jnp.concatenate([v_refs[u][0, 0] for u in range(pps)], axis=1).astype(BF16)
        idx = g * n + lax.broadcasted_iota(I32, (QP, n), 1)
        attend(kt, vt, select(_sortable(sc_ref[0]), idx), bias)

    @pl.when(g != last)
    def _():
        pages(None)

    @pl.when(g == last)
    def _():
        pages(jnp.concatenate([jnp.zeros((N_HEADS * QP, n - page), F32), blast_ref[...]], axis=1))
        idx = past + lax.broadcasted_iota(I32, (QP, page), 1)
        attend(pad_t(kn_ref[0]).astype(BF16), pad_t(vn_ref[0]).astype(BF16), select(keyn_s[...], idx),
               bnew_ref[...])
        o = acc_s[...] / l_s[...]
        for h in range(N_HEADS):
            o_ref[0, :, h * HEAD_DIM:(h + 1) * HEAD_DIM] = o[h * QP:(h + 1) * QP, :]


def _dsa_sample(page_table, q, qcat, kw, kcn, kn, vn, scores, k_t, v_t, rel_bias, layer, steps, pps):
    nb, n_pages = page_table.shape
    page = k_t.shape[-1]
    past = n_pages * page
    nq, nkv = N_HEADS * HEAD_DIM, N_KV_HEADS * HEAD_DIM
    topk = min(TOPK_MAX, (past + steps) // 4)
    n_steps = n_pages // pps
    r = N_HEADS // N_KV_HEADS
    tab = _bias_table(rel_bias, 2 * page)
    qq = np.arange(QP)[:, None]
    off = np.arange(page)[None, :]
    blast = tab[:, page + qq - off].reshape(N_HEADS * QP, page)
    bnew = tab[:, np.maximum(qq - off, 0)].reshape(N_HEADS * QP, page)
    bmap = lambda b, g, pt: (b, 0, 0)
    grid_spec = pltpu.PrefetchScalarGridSpec(
        num_scalar_prefetch=1,
        grid=(nb, n_steps),
        in_specs=[pl.BlockSpec((1, QP, nq), bmap),
                  pl.BlockSpec((1, QP, qcat.shape[-1]), bmap),
                  pl.BlockSpec((1, QP, LANE), bmap),
                  pl.BlockSpec((1, QP, kcn.shape[-1]), bmap),
                  pl.BlockSpec((1, QP, nkv), bmap),
                  pl.BlockSpec((1, QP, nkv), bmap),
                  pl.BlockSpec((1, QP, past), bmap),
                  pl.BlockSpec((1, QP, pps * page), lambda b, g, pt: (b, 0, g)),
                  pl.BlockSpec(blast.shape, lambda b, g, pt: (0, 0)),
                  pl.BlockSpec(bnew.shape, lambda b, g, pt: (0, 0))]
                 + _page_specs(k_t.shape, layer, pps) + _page_specs(v_t.shape, layer, pps),
        out_specs=pl.BlockSpec((1, QP, nq), bmap),
        scratch_shapes=[pltpu.VMEM((QP, page), I32),
                        pltpu.VMEM((QP, 1), I32), pltpu.VMEM((QP, 1), I32),
                        pltpu.VMEM((N_KV_HEADS, r * QP, HEAD_DIM), BF16),
                        pltpu.VMEM((N_HEADS * QP, 1), F32), pltpu.VMEM((N_HEADS * QP, 1), F32),
                        pltpu.VMEM((N_HEADS * QP, HEAD_DIM), F32)])
    return pl.pallas_call(
        functools.partial(_dsa_sample_attend_kernel, pps=pps, n_pages=n_pages, steps=steps, topk=topk,
                          idx_bits=int(math.ceil(math.log2(past + page)))),
        grid_spec=grid_spec,
        out_shape=jax.ShapeDtypeStruct((nb, QP, nq), F32),
        compiler_params=_cparams(2),
        name="dsa_sample_attend",
    )(page_table, q, qcat, kw, kcn, kn, vn, scores, scores, blast, bnew, *([k_t] * pps), *([v_t] * pps))


SSD_COLS = 512


def _softplus(x):
    return jnp.maximum(x, 0.0) + jnp.log1p(jnp.exp(-jnp.abs(x)))


def _split3(a):
    p1 = a.astype(BF16)
    r1 = a - p1.astype(F32)
    p2 = r1.astype(BF16)
    return p1, p2, (r1 - p2.astype(F32)).astype(BF16)


def _dot_sel_rhs(a, e):
    p1, p2, p3 = _split3(a)
    return _dot(p1, e) + (_dot(p2, e) + _dot(p3, e))


def _dot_sel_lhs(e, a):
    p1, p2, p3 = _split3(a)
    return _dot(e, p1) + (_dot(e, p2) + _dot(e, p3))


def _ssd_prompt_kernel(z_ref, xbc_ref, dtr_ref, cw_ref, cb_ref, dtb_ref, alog_ref, dexp_ref, ng_ref,
                       e_ref, y_ref, nbuf_ref, hf_ref, prev_s, xc_s, y_s, h_s, *, width, nh, inner):
    i = pl.program_id(1)
    last = pl.num_programs(1) - 1
    qn, cdim = xbc_ref.shape[1], xbc_ref.shape[2]
    ngrp, ns, hd = SSD_GROUPS, SSD_STATE, SSD_HEADDIM
    hpg = nh // ngrp

    @pl.when(i == 0)
    def _():
        prev_s[...] = jnp.zeros(prev_s.shape, F32)
        h_s[...] = jnp.zeros(h_s.shape, F32)

    for cb in range(cdim // SSD_COLS):
        cols = slice(cb * SSD_COLS, (cb + 1) * SSD_COLS)
        xr = xbc_ref[0, :, cols]
        prev = prev_s[:, cols]
        conv = xr * cw_ref[width - 1:width, cols]
        for k in range(1, width):
            conv = conv + _shift_rows(xr, prev, k) * cw_ref[width - 1 - k:width - k, cols]
        xc_s[:, cols] = _silu(conv + cb_ref[:, cols])
        prev_s[:, cols] = xr[qn - SUBLANE:, :]
        nbuf_ref[0, :, cols] = xr[qn - (width - 1):, :]

    hl = lax.broadcasted_iota(I32, (qn, LANE), 1)
    dt = jnp.where(hl < nh, _softplus(dtr_ref[0] + dtb_ref[...]), 0.0)
    dta = dt * (-jnp.exp(alog_ref[...]))
    qrow = lax.broadcasted_iota(I32, (qn, qn), 0)
    kcol = lax.broadcasted_iota(I32, (qn, qn), 1)
    causal = kcol <= qrow
    cum = _dot_sel_lhs(jnp.where(causal, 1.0, 0.0).astype(BF16), dta)
    cum_t = cum.T
    cum_last = cum[qn - 1:qn, :]
    ecum = jnp.exp(cum)
    dtw = dt * jnp.exp(cum_last - cum)
    e_last = jnp.exp(cum_last)
    lane_p = lax.broadcasted_iota(I32, (qn, LANE), 1)
    row_p = lax.broadcasted_iota(I32, (LANE, ns), 0)

    for g in range(ngrp):
        gl = slice(g * SSD_COLS, (g + 1) * SSD_COLS)
        e_g = e_ref[:, gl]
        xs_g = xc_s[:, gl]
        dtx_g = xs_g * _dot_sel_rhs(dt, e_g)
        dtxw_g = xs_g * _dot_sel_rhs(dtw, e_g)
        ecx_g = _dot_sel_rhs(ecum, e_g)
        bm = xc_s[:, inner + g * ns:inner + (g + 1) * ns].astype(BF16)
        cm = xc_s[:, inner + (ngrp + g) * ns:inner + (ngrp + g + 1) * ns].astype(BF16)
        cbm = _dot_nt(cm, bm)
        for pi in range(hpg // 2):
            ha = g * hpg + 2 * pi
            pls = slice(pi * LANE, (pi + 1) * LANE)
            lanes = slice(ha * hd, ha * hd + LANE)
            dtx_p = dtx_g[:, pls].astype(BF16)
            yds = []
            for h in (ha, ha + 1):
                seg = cum[:, h:h + 1] - cum_t[h:h + 1, :]
                dec = jnp.exp(jnp.where(causal, seg, -jnp.inf))
                yds.append(_dot((cbm * dec).astype(BF16), dtx_p))
            yd = jnp.where(lane_p < hd, yds[0], yds[1])
            hp = h_s[ha // 2]
            yo = _dot_nt(cm, hp.astype(BF16)) * ecx_g[:, pls]
            y_s[:, lanes] = yd + yo + dexp_ref[:, lanes] * xs_g[:, pls]
            s_new = _dot(dtxw_g[:, pls].T.astype(BF16), bm)
            cd = jnp.where(row_p < hd, e_last[:, ha:ha + 1], e_last[:, ha + 1:ha + 2])
            h_s[ha // 2] = hp * cd + s_new

    for g in range(ngrp):
        gl = slice(g * SSD_COLS, (g + 1) * SSD_COLS)
        yg = y_s[:, gl] * _silu(z_ref[0, :, gl])
        ms = jnp.mean(yg * yg, axis=-1, keepdims=True)
        y_ref[0, :, gl] = (yg * lax.rsqrt(ms + EPS)) * ng_ref[:, gl]

    @pl.when(i == last)
    def _():
        hf_ref[0] = h_s[...]


def _head_expand(nh, hd):
    e = np.zeros((LANE, nh * hd), np.float32)
    for h in range(nh):
        e[h, h * hd:(h + 1) * hd] = 1.0
    return jnp.asarray(e, BF16)


def _pad_lanes(v, n=LANE):
    v = v.reshape(1, -1)
    return jnp.pad(v, ((0, 0), (0, n - v.shape[1])))


def _ssd_prompt(z, xbc, dtr, conv_w, conv_b, dt_bias, a_log, d_skip, norm_g):
    bsz, seq, inner = z.shape
    cdim = xbc.shape[-1]
    nh = dt_bias.shape[0]
    width = conv_w.shape[0]
    qn = math.gcd(seq, SSD_CHUNK)
    assert qn == SSD_CHUNK and inner == SSD_GROUPS * SSD_COLS and cdim % SSD_COLS == 0
    dexp = jnp.repeat(d_skip, SSD_HEADDIM).reshape(1, inner)
    blk = lambda n: pl.BlockSpec((1, qn, n), lambda b, i: (b, i, 0))
    y, nbuf, hf = pl.pallas_call(
        functools.partial(_ssd_prompt_kernel, width=width, nh=nh, inner=inner),
        grid=(bsz, seq // qn),
        in_specs=[blk(inner), blk(cdim), blk(LANE),
                  _const2((width, cdim)), _const2((1, cdim)), _const2((1, LANE)), _const2((1, LANE)),
                  _const2((1, inner)), _const2((1, inner)), _const2((LANE, inner))],
        out_specs=[blk(inner),
                   pl.BlockSpec((1, width - 1, cdim), lambda b, i: (b, 0, 0)),
                   pl.BlockSpec((1, nh // 2, LANE, SSD_STATE), lambda b, i: (b, 0, 0, 0))],
        out_shape=[jax.ShapeDtypeStruct((bsz, seq, inner), F32),
                   jax.ShapeDtypeStruct((bsz, width - 1, cdim), F32),
                   jax.ShapeDtypeStruct((bsz, nh // 2, LANE, SSD_STATE), F32)],
        scratch_shapes=[pltpu.VMEM((SUBLANE, cdim), F32), pltpu.VMEM((qn, cdim), F32),
                        pltpu.VMEM((qn, inner), F32), pltpu.VMEM((nh // 2, LANE, SSD_STATE), F32)],
        compiler_params=_cparams(2),
        name="ssd_prompt",
    )(z, xbc, dtr, conv_w, conv_b.reshape(1, cdim), _pad_lanes(dt_bias), _pad_lanes(a_log),
      dexp, norm_g.reshape(1, inner), _head_expand(nh, SSD_HEADDIM))
    return y, nbuf, hf.reshape(bsz, nh, SSD_HEADDIM, SSD_STATE)


def _ssd_prep_sample_kernel(xbc_ref, buf_ref, dtr_ref, cw_ref, cb_ref, dtb_ref, xc_ref, dt_ref, nbuf_ref,
                            *, width, nb, steps, nh):
    cdim = xbc_ref.shape[1]
    for cb in range(cdim // SSD_COLS):
        cols = slice(cb * SSD_COLS, (cb + 1) * SSD_COLS)
        ext = jnp.concatenate([buf_ref[:, cols], xbc_ref[:, cols]], axis=0)
        conv = ext[0:steps * nb] * cw_ref[0:1, cols]
        for k in range(1, width):
            conv = conv + ext[k * nb:(k + steps) * nb] * cw_ref[k:k + 1, cols]
        xc_ref[:, cols] = _silu(conv + cb_ref[:, cols])
        nbuf_ref[:, cols] = ext[steps * nb:]
    hl = lax.broadcasted_iota(I32, dtr_ref.shape, 1)
    dt_ref[...] = jnp.where(hl < nh, _softplus(dtr_ref[...] + dtb_ref[...]), 0.0)


def _ssd_prep_sample(xbc_tb, buf_kb, dtr_tb, conv_w, conv_b, dt_bias, nb, steps):
    rows, cdim = xbc_tb.shape
    width = conv_w.shape[0]
    return pl.pallas_call(
        functools.partial(_ssd_prep_sample_kernel, width=width, nb=nb, steps=steps, nh=dt_bias.shape[0]),
        out_shape=[jax.ShapeDtypeStruct((rows, cdim), F32),
                   jax.ShapeDtypeStruct((rows, LANE), F32),
                   jax.ShapeDtypeStruct(((width - 1) * nb, cdim), F32)],
        compiler_params=pltpu.CompilerParams(vmem_limit_bytes=VMEM_LIMIT),
        name="ssd_prep_sample",
    )(xbc_tb, buf_kb, dtr_tb, conv_w, conv_b.reshape(1, cdim), _pad_lanes(dt_bias))


def _ssd_scan_sample_kernel(xc_ref, dt_ref, z_ref, h0_ref, alog_ref, dexp_ref, ng_ref, e_ref,
                            y_ref, hf_ref, *, steps, inner):
    ngrp, ns = SSD_GROUPS, SSD_STATE
    xc, dt = xc_ref[0], dt_ref[0]
    row = lax.broadcasted_iota(I32, (QP, LANE), 0)
    cum = dt * (-jnp.exp(alog_ref[...]))
    for sft in (1, 2, 4):
        cum = cum + jnp.where(row >= sft, pltpu.roll(cum, sft, 0), 0.0)
    cum_last = cum[QP - 1:QP, :]
    parts = [dt, jnp.exp(cum), dt * jnp.exp(cum_last - cum), jnp.broadcast_to(jnp.exp(cum_last), (QP, LANE))]
    for s in range(steps):
        parts.append(jnp.where(row >= s, jnp.exp(cum - cum[s:s + 1, :]), 0.0))
    stack = jnp.concatenate(parts, axis=0)

    for g in range(ngrp):
        gl = slice(g * SSD_COLS, (g + 1) * SSD_COLS)
        ex = _dot_sel_rhs(stack, e_ref[:, gl])
        dt_x, ecum_x, dtw_x, el_x = (ex[k * QP:(k + 1) * QP] for k in range(4))
        xs_g = xc[:, gl]
        dtx = xs_g * dt_x
        bm = xc[:, inner + g * ns:inner + (g + 1) * ns]
        cm = xc[:, inner + (ngrp + g) * ns:inner + (ngrp + g + 1) * ns].astype(BF16)
        bmp = jnp.concatenate([bm, jnp.zeros((LANE - QP, ns), F32)], axis=0).astype(BF16)
        cbm = _dot_nt(cm, bmp)
        yd = jnp.zeros((QP, SSD_COLS), F32)
        for s in range(steps):
            yd = yd + (ex[(4 + s) * QP:(5 + s) * QP] * cbm[:, s:s + 1]) * dtx[s:s + 1, :]
        h0g = h0_ref[0, gl, :]
        yo = _dot_nt(cm, h0g.astype(BF16)) * ecum_x
        y = (yd + yo + dexp_ref[:, gl] * xs_g) * _silu(z_ref[0, :, gl])
        ms = jnp.mean(y * y, axis=-1, keepdims=True)
        y_ref[0, :, gl] = (y * lax.rsqrt(ms + EPS)) * ng_ref[:, gl]
        tm = jnp.concatenate([xs_g * dtw_x, el_x[0:1], jnp.zeros((LANE - QP - 1, SSD_COLS), F32)], axis=0)
        tt = tm.T
        hf_ref[0, gl, :] = h0g * tt[:, QP:QP + 1] + _dot(tt.astype(BF16), bmp)


def _ssd_scan_sample(xc, dt, z, h0, a_log, d_skip, norm_g, steps):
    nb, _, cdim = xc.shape
    inner = z.shape[-1]
    nh = a_log.shape[0]
    rows_h = nh * SSD_HEADDIM
    dexp = jnp.repeat(d_skip, SSD_HEADDIM).reshape(1, inner)
    one = lambda shape: pl.BlockSpec(shape, lambda b: (0,) * len(shape))
    return pl.pallas_call(
        functools.partial(_ssd_scan_sample_kernel, steps=steps, inner=inner),
        grid=(nb,),
        in_specs=[pl.BlockSpec((1, QP, cdim), lambda b: (b, 0, 0)),
                  pl.BlockSpec((1, QP, LANE), lambda b: (b, 0, 0)),
                  pl.BlockSpec((1, QP, inner), lambda b: (b, 0, 0)),
                  pl.BlockSpec((1, rows_h, SSD_STATE), lambda b: (b, 0, 0)),
                  one((1, LANE)), one((1, inner)), one((1, inner)), one((LANE, inner))],
        out_specs=[pl.BlockSpec((1, QP, inner), lambda b: (b, 0, 0)),
                   pl.BlockSpec((1, rows_h, SSD_STATE), lambda b: (b, 0, 0))],
        out_shape=[jax.ShapeDtypeStruct((nb, QP, inner), F32),
                   jax.ShapeDtypeStruct((nb, rows_h, SSD_STATE), F32)],
        compiler_params=_cparams(1),
        name="ssd_scan_sample",
    )(xc, dt, z, h0, _pad_lanes(a_log), dexp, norm_g.reshape(1, inner), _head_expand(nh, SSD_HEADDIM))


TM = 256


def _to_steps(a, nb, steps):
    n = a.shape[-1]
    return a.reshape(nb, steps, n).transpose(1, 0, 2).reshape(steps * nb, n)


def _to_batch(a, nb, steps):
    n = a.shape[-1]
    return a.reshape(steps, nb, n).transpose(1, 0, 2).reshape(1, nb * steps, n)


def _pad_q(a, nb, steps):
    n = a.shape[-1]
    return jnp.pad(a.reshape(nb, steps, n), ((0, 0), (0, QP - steps), (0, 0)))


def kernel(x_prompt, x_sample, cache_k, cache_v, cache_kidx, state_s5_re, state_s5_im, state_sconv, state_ssd, state_ssd_conv, page_table, c_prompt, c_sample, rel_bias, ada_w, ada_b, norm_mix, norm_mlp, norm_final, attn_w_in, attn_w_out, s5_lam_re, s5_lam_im, s5_log_dt, s5_b_re, s5_b_im, s5_c_re, s5_c_im, s5_d, s5_w_glu, sc_w_in, sc_w_conv, sc_w_out, ssd_w_in, ssd_conv_w, ssd_conv_b, ssd_dt_bias, ssd_a_log, ssd_d, ssd_norm, ssd_w_out, mlp_w1, mlp_w2):
    bp, seq, d = x_prompt.shape
    nb, steps, _ = x_sample.shape
    depth = ada_w.shape[0]
    n_mixers = 4
    rs = nb * steps
    tm = min(TM, seq)

    rows = bp + nb
    c_all = jnp.pad(jnp.concatenate([c_prompt, c_sample], axis=0), ((0, (-rows) % SUBLANE), (0, 0)))
    ada = _ada(c_all, ada_w, ada_b)

    xp = x_prompt
    xs = x_sample.reshape(1, rs, d)
    outs = {name: [] for name in ("kp", "vp", "kip", "ks", "vs", "kis", "s5pr", "s5pi", "s5sr", "s5si",
                                  "scp", "scs", "ssdp", "ssdcp", "ssds", "ssdcs")}
    nq, nkv = N_HEADS * HEAD_DIM, N_KV_HEADS * HEAD_DIM
    for i in range(depth):
        m, j = i % n_mixers, i // n_mixers
        mp = [ada[i, :bp, k * d:(k + 1) * d].reshape(bp, 1, d) for k in range(6)]
        ms = [jnp.repeat(ada[i, bp:bp + nb, k * d:(k + 1) * d], steps, axis=0).reshape(1, rs, d)
              for k in range(6)]
        g_mix = norm_mix[i]
        if m == 0:
            w_in = attn_w_in[j]
            w_qkv = w_in[:, :nq + 2 * nkv].astype(BF16)
            n_idx = IDX_HEADS * IDX_DIM + LANE
            w_idx = jnp.pad(w_in[:, nq + 2 * nkv:], ((0, 0), (0, n_idx - (w_in.shape[1] - nq - 2 * nkv))))
            w_out = attn_w_out[j].astype(BF16)
            q, k, v, kw, qcat, kcat, kaug, vaug = _attn_proj(xp, mp[0], mp[1], g_mix, w_qkv, w_idx,
                                                             2 * MAX_DISTANCE, True)
            o = _dsa_prompt(q, qcat, kw, kcat, kaug, vaug, rel_bias, MAX_DISTANCE)
            xp = _outproj(o, xp, mp[2], w_out, tm)
            page = cache_k.shape[2]
            outs["kp"].append(k.reshape(bp, seq // page, page, N_KV_HEADS, HEAD_DIM))
            outs["vp"].append(v.reshape(bp, seq // page, page, N_KV_HEADS, HEAD_DIM))
            outs["kip"].append(kw[..., :IDX_DIM].reshape(bp, seq // page, page, IDX_DIM))
            q, k, v, kw, qcat, kcat, _, _ = _attn_proj(xs, ms[0], ms[1], g_mix, w_qkv, w_idx, rs, False)
            pps = math.gcd(PAGES_PER_STEP, page_table.shape[1])
            qc_p, kw_p = _pad_q(qcat.astype(F32), nb, steps), _pad_q(kw, nb, steps)
            scores = _dsa_sample_scores(page_table, qc_p, kw_p, _pages_t(cache_kidx), j, pps)
            o = _dsa_sample(page_table, _pad_q(q, nb, steps), qc_p, kw_p, _pad_q(kcat.astype(F32), nb, steps),
                            _pad_q(k, nb, steps), _pad_q(v, nb, steps), scores, _pages_t(cache_k),
                            _pages_t(cache_v), rel_bias, j, steps, pps)
            xs = _outproj(o[:, :steps].reshape(1, rs, nq), xs, ms[2], w_out, rs)
            outs["ks"].append(k.reshape(nb, steps, N_KV_HEADS, HEAD_DIM))
            outs["vs"].append(v.reshape(nb, steps, N_KV_HEADS, HEAD_DIM))
            outs["kis"].append(kw[..., :IDX_DIM].reshape(nb, steps, IDX_DIM))
        elif m == 1:
            ar, ai, wb, wc = _s5_weights(s5_lam_re[j], s5_lam_im[j], s5_log_dt[j], s5_b_re[j], s5_b_im[j],
                                         s5_c_re[j], s5_c_im[j])
            w_glu = s5_w_glu[j].astype(BF16)
            grp, nst = s5_lam_re.shape[1:]
            z, fr, fi = _s5_prompt(xp, mp[0], mp[1], g_mix, s5_d[j], ar, ai, wb, wc)
            xp = _glu(z, xp, mp[2], w_glu, tm)
            outs["s5pr"].append(fr.reshape(bp, grp, nst))
            outs["s5pi"].append(fi.reshape(bp, grp, nst))
            z, fr, fi = _s5_sample(_to_steps(xs, nb, steps), _to_steps(ms[0], nb, steps),
                                   _to_steps(ms[1], nb, steps), g_mix, s5_d[j], ar, ai, wb, wc,
                                   state_s5_re[j].reshape(nb, grp * nst), state_s5_im[j].reshape(nb, grp * nst),
                                   nb, steps)
            xs = _glu(_to_batch(z, nb, steps), xs, ms[2], w_glu, rs)
            outs["s5sr"].append(fr.reshape(nb, grp, nst))
            outs["s5si"].append(fi.reshape(nb, grp, nst))
        elif m == 2:
            w_in = sc_w_in[j].astype(BF16)
            w_out = sc_w_out[j].astype(BF16)
            width = sc_w_conv.shape[1]
            (p,) = _proj(xp, mp[0], mp[1], g_mix, w_in, (3 * d,), tm)
            xp, nbuf = _sconv_prompt(p, xp, mp[2], jnp.zeros((bp, width - 1, d), F32), sc_w_conv[j], w_out, tm)
            outs["scp"].append(nbuf)
            (p,) = _proj(xs, ms[0], ms[1], g_mix, w_in, (3 * d,), rs)
            buf = state_sconv[j].transpose(1, 0, 2).reshape((width - 1) * nb, d)
            y, nbuf = _sconv_sample(_to_steps(p, nb, steps), _to_steps(xs, nb, steps), _to_steps(ms[2], nb, steps),
                                    buf, sc_w_conv[j], w_out, nb, steps)
            xs = _to_batch(y, nb, steps)
            outs["scs"].append(nbuf.reshape(width - 1, nb, d).transpose(1, 0, 2))
        else:
            inner = ssd_norm.shape[1]
            cdim = ssd_conv_w.shape[2]
            nh = ssd_dt_bias.shape[1]
            width = ssd_conv_w.shape[1]
            w_in = jnp.pad(ssd_w_in[j], ((0, 0), (0, LANE - nh))).astype(BF16)
            w_out = ssd_w_out[j].astype(BF16)
            z, xbc, dtr = _proj(xp, mp[0], mp[1], g_mix, w_in, (inner, cdim, LANE), tm)
            y, nbuf, hf = _ssd_prompt(z, xbc, dtr, ssd_conv_w[j], ssd_conv_b[j], ssd_dt_bias[j], ssd_a_log[j],
                                      ssd_d[j], ssd_norm[j])
            xp = _outproj(y, xp, mp[2], w_out, tm)
            outs["ssdp"].append(hf)
            outs["ssdcp"].append(nbuf)
            z, xbc, dtr = _proj(xs, ms[0], ms[1], g_mix, w_in, (inner, cdim, LANE), rs)
            buf = state_ssd_conv[j].transpose(1, 0, 2).reshape((width - 1) * nb, cdim)
            xc, dt, nbuf = _ssd_prep_sample(_to_steps(xbc, nb, steps), buf, _to_steps(dtr, nb, steps),
                                            ssd_conv_w[j], ssd_conv_b[j], ssd_dt_bias[j], nb, steps)
            y, hf = _ssd_scan_sample(_pad_q(_to_batch(xc, nb, steps), nb, steps),
                                     _pad_q(_to_batch(dt, nb, steps), nb, steps), _pad_q(z, nb, steps),
                                     state_ssd[j].reshape(nb, nh * SSD_HEADDIM, SSD_STATE),
                                     ssd_a_log[j], ssd_d[j], ssd_norm[j], steps)
            xs = _outproj(y[:, :steps].reshape(1, rs, inner), xs, ms[2], w_out, rs)
            outs["ssds"].append(hf.reshape(nb, nh, SSD_HEADDIM, SSD_STATE))
            outs["ssdcs"].append(nbuf.reshape(width - 1, nb, cdim).transpose(1, 0, 2))
        w1 = mlp_w1[i].astype(BF16)
        w2 = mlp_w2[i].astype(BF16)
        fin = i == depth - 1
        xp = _mlp(xp, mp[3], mp[4], mp[5], norm_mlp[i], w1, w2, norm_final, fin, tm)
        xs = _mlp(xs, ms[3], ms[4], ms[5], norm_mlp[i], w1, w2, norm_final, fin, rs)
    st = jnp.stack
    return (xp, xs.reshape(nb, steps, d), st(outs["kp"]), st(outs["vp"]), st(outs["kip"]),
            st(outs["ks"]), st(outs["vs"]), st(outs["kis"]), st(outs["s5pr"]), st(outs["s5pi"]),
            st(outs["s5sr"]), st(outs["s5si"]), st(outs["scp"]), st(outs["scs"]),
            st(outs["ssdp"]), st(outs["ssdcp"]), st(outs["ssds"]), st(outs["ssdcs"]))
```

```python
import functools
import math

import jax
import jax.numpy as jnp
import numpy as np
from jax import lax
from jax.experimental import pallas as pl
from jax.experimental.pallas import tpu as pltpu

F32 = jnp.float32
BF16 = jnp.bfloat16
I32 = jnp.int32

EPS = 1e-6
N_HEADS = 16
HEAD_DIM = 64
N_KV_HEADS = 4
IDX_HEADS = 8
IDX_DIM = 64
TOPK_MAX = 256
N_BUCKETS = 32
MAX_DISTANCE = 128
S5_GROUP = 16
S5_STATE = 64
SSD_HEADDIM = 64
SSD_GROUPS = 4
SSD_STATE = 128
SSD_CHUNK = 128

LANE = 128
SUBLANE = 8
VMEM_LIMIT = 56 * 1024 * 1024
NEG_BIG = -1e30
INT_MIN = -2147483648


def _cparams(n_axes):
    return pltpu.CompilerParams(dimension_semantics=("arbitrary",) * n_axes,
                                vmem_limit_bytes=VMEM_LIMIT)


def _dot(a, b):
    return jnp.dot(a, b, preferred_element_type=F32)


def _dot_nt(a, b):
    return lax.dot_general(a, b, (((1,), (1,)), ((), ())), preferred_element_type=F32)


def _split(a):
    hi = a.astype(BF16)
    lo = (a - hi.astype(F32)).astype(BF16)
    return hi, lo


def _dot3(a, b):
    ah, al = _split(a)
    bh, bl = _split(b)
    return _dot(ah, bh) + (_dot(al, bh) + _dot(ah, bl))


def _rms_mod(x, g, shift, scale):
    y = x * lax.rsqrt(jnp.mean(x * x, axis=-1, keepdims=True) + EPS)
    return (y * g) * (1.0 + scale) + shift


def _sigmoid(x):
    return 1.0 / (1.0 + jnp.exp(-x))


def _silu(x):
    return x * _sigmoid(x)


def _row_block(arr, tm):
    n = arr.shape[-1]
    if arr.shape[1] == 1:
        return pl.BlockSpec((1, 1, n), lambda b, i: (b, 0, 0))
    return pl.BlockSpec((1, tm, n), lambda b, i: (b, i, 0))


def _const2(shape):
    return pl.BlockSpec(shape, lambda b, i: (0,) * len(shape))


def _ada_kernel(c_ref, w_ref, b_ref, o_ref):
    o_ref[0] = _dot3(_silu(c_ref[...]), w_ref[0]) + b_ref[0]


def _ada(c_all, ada_w, ada_b):
    depth, d, n = ada_w.shape
    rows = c_all.shape[0]
    tn = 1536
    return pl.pallas_call(
        _ada_kernel,
        grid=(depth, n // tn),
        in_specs=[pl.BlockSpec((rows, d), lambda l, j: (0, 0)),
                  pl.BlockSpec((1, d, tn), lambda l, j: (l, 0, j)),
                  pl.BlockSpec((1, 1, tn), lambda l, j: (l, 0, j))],
        out_specs=pl.BlockSpec((1, rows, tn), lambda l, j: (l, 0, j)),
        out_shape=jax.ShapeDtypeStruct((depth, rows, n), F32),
        compiler_params=_cparams(2),
        name="ada",
    )(c_all, ada_w, ada_b.reshape(depth, 1, n))


def _proj_kernel(x_ref, sh_ref, sc_ref, g_ref, w_ref, *o_refs, splits):
    h = _rms_mod(x_ref[0], g_ref[...], sh_ref[0], sc_ref[0]).astype(BF16)
    off = 0
    for o_ref, n in zip(o_refs, splits):
        o_ref[0] = _dot(h, w_ref[:, off:off + n])
        off += n


def _proj(x, shift, scale, g, w_bf16, splits, tm):
    bsz, seq, d = x.shape
    n = w_bf16.shape[1]
    assert sum(splits) == n and seq % tm == 0
    return pl.pallas_call(
        functools.partial(_proj_kernel, splits=tuple(splits)),
        grid=(bsz, seq // tm),
        in_specs=[pl.BlockSpec((1, tm, d), lambda b, i: (b, i, 0)),
                  _row_block(shift, tm), _row_block(scale, tm),
                  _const2((1, d)), _const2((d, n))],
        out_specs=[pl.BlockSpec((1, tm, s), lambda b, i: (b, i, 0)) for s in splits],
        out_shape=[jax.ShapeDtypeStruct((bsz, seq, s), F32) for s in splits],
        compiler_params=_cparams(2),
        name="proj",
    )(x, shift, scale, g.reshape(1, d), w_bf16)


def _aug_heads(t, col):
    rows = t.shape[0]
    lane = lax.broadcasted_iota(I32, (rows, LANE - HEAD_DIM), 1)
    extra = jnp.where(lane == 0, col, 0.0).astype(F32)
    parts = []
    for g in range(N_KV_HEADS):
        parts += [t[:, g * HEAD_DIM:(g + 1) * HEAD_DIM], extra]
    return jnp.concatenate(parts, axis=1).astype(BF16)


def _attn_proj_kernel(x_ref, sh_ref, sc_ref, g_ref, w_ref, wi_ref,
                      q_ref, k_ref, v_ref, kw_ref, qcat_ref, kcat_ref, kaug_ref, vaug_ref,
                      *, nq, nkv, nqi, key_major):
    h = _rms_mod(x_ref[0], g_ref[...], sh_ref[0], sc_ref[0])
    hb = h.astype(BF16)
    q_ref[0] = _dot(hb, w_ref[:, 0:nq]) * (HEAD_DIM ** -0.5)
    k = _dot(hb, w_ref[:, nq:nq + nkv])
    v = _dot(hb, w_ref[:, nq + nkv:nq + 2 * nkv])
    k_ref[0] = k
    v_ref[0] = v
    r = _dot3(h, wi_ref[...])
    kw = r[:, nqi:nqi + LANE]
    kw_ref[0] = kw
    zero = jnp.zeros((h.shape[0], IDX_DIM), BF16)
    parts = []
    for hh in range(IDX_HEADS):
        ah, al = _split(r[:, hh * IDX_DIM:(hh + 1) * IDX_DIM])
        parts += [ah, al, ah, zero]
    qcat_ref[0] = jnp.concatenate(parts, axis=1)
    kcat = _idx_rhs(kw[:, 0:IDX_DIM])
    kaug = _aug_heads(k, -1.0)
    vaug_ref[0] = _aug_heads(v, 1.0)
    if key_major:
        kcat_ref[0, 0] = kcat.astype(F32).T.astype(BF16)
        kaug_ref[0, 0] = kaug.astype(F32).T.astype(BF16)
    else:
        kcat_ref[0] = kcat
        kaug_ref[0] = kaug


def _attn_proj(x, shift, scale, g, w_qkv_bf16, w_idx, tm, key_major):
    bsz, seq, d = x.shape
    nq = N_HEADS * HEAD_DIM
    nkv = N_KV_HEADS * HEAD_DIM
    nqi = IDX_HEADS * IDX_DIM
    rows = lambda n, dt: (pl.BlockSpec((1, tm, n), lambda b, i: (b, i, 0)), jax.ShapeDtypeStruct((bsz, seq, n), dt))
    cols = lambda n, dt: (pl.BlockSpec((1, 1, n, tm), lambda b, i: (b, i, 0, 0)),
                          jax.ShapeDtypeStruct((bsz, seq // tm, n, tm), dt))
    keys = cols if key_major else rows
    outs = [rows(nq, F32), rows(nkv, F32), rows(nkv, F32), rows(LANE, F32), rows(4 * nqi, BF16),
            keys(4 * IDX_DIM, BF16), keys(N_KV_HEADS * LANE, BF16), rows(N_KV_HEADS * LANE, BF16)]
    return pl.pallas_call(
        functools.partial(_attn_proj_kernel, nq=nq, nkv=nkv, nqi=nqi, key_major=key_major),
        grid=(bsz, seq // tm),
        in_specs=[pl.BlockSpec((1, tm, d), lambda b, i: (b, i, 0)),
                  _row_block(shift, tm), _row_block(scale, tm),
                  _const2((1, d)), _const2(w_qkv_bf16.shape), _const2(w_idx.shape)],
        out_specs=[o[0] for o in outs],
        out_shape=[o[1] for o in outs],
        compiler_params=_cparams(2),
        name="attn_proj",
    )(x, shift, scale, g.reshape(1, d), w_qkv_bf16, w_idx)


def _mlp_kernel(*refs, ff_chunk, final_norm, mixer_out):
    if mixer_out is None:
        x_ref, sh_ref, sc_ref, gate_ref, g_ref, w1_ref, w2_ref, gf_ref, y_ref = refs
        x = x_ref[0]
    else:
        o_ref, gmix_ref, wo_ref, x_ref, sh_ref, sc_ref, gate_ref, g_ref, w1_ref, w2_ref, gf_ref, y_ref = refs
        d = x_ref.shape[-1]
        ob = o_ref[0].astype(BF16)
        if mixer_out == "glu":
            t = _dot(ob, wo_ref[:, :d]) * _sigmoid(_dot(ob, wo_ref[:, d:]))
        else:
            t = _dot(ob, wo_ref[...])
        x = x_ref[0] + gmix_ref[0] * t
    h = _rms_mod(x, g_ref[...], sh_ref[0], sc_ref[0]).astype(BF16)
    dff = w1_ref.shape[1]
    acc = jnp.zeros(x.shape, F32)
    for c in range(dff // ff_chunk):
        a = jnp.maximum(_dot(h, w1_ref[:, c * ff_chunk:(c + 1) * ff_chunk]), 0.0)
        acc = acc + _dot((a * a).astype(BF16), w2_ref[c * ff_chunk:(c + 1) * ff_chunk, :])
    y = x + gate_ref[0] * acc
    if final_norm:
        y = (y * lax.rsqrt(jnp.mean(y * y, axis=-1, keepdims=True) + EPS)) * gf_ref[...]
    y_ref[0] = y


def _resident(shape):
    return pl.BlockSpec(shape, lambda b, i: (0,) * len(shape), pipeline_mode=pl.Buffered(1))


def _mlp(x, shift, scale, gate, g, w1_bf16, w2_bf16, g_final, final_norm, tm, mixer=None):
    bsz, seq, d = x.shape
    dff = w1_bf16.shape[1]
    rows = lambda n: pl.BlockSpec((1, tm, n), lambda b, i: (b, i, 0))
    args, specs, kind = [], [], None
    if mixer is not None:
        kind, o, gmix, wo = mixer
        args += [o, gmix, wo]
        specs += [rows(o.shape[-1]), _row_block(gmix, tm), _resident(wo.shape)]
    args += [x, shift, scale, gate, g.reshape(1, d), w1_bf16, w2_bf16, g_final.reshape(1, d)]
    specs += [rows(d), _row_block(shift, tm), _row_block(scale, tm), _row_block(gate, tm),
              _const2((1, d)), _resident((d, dff)), _resident((dff, d)), _const2((1, d))]
    return pl.pallas_call(
        functools.partial(_mlp_kernel, ff_chunk=1024, final_norm=final_norm, mixer_out=kind),
        grid=(bsz, seq // tm),
        in_specs=specs,
        out_specs=rows(d),
        out_shape=jax.ShapeDtypeStruct((bsz, seq, d), F32),
        compiler_params=_cparams(2),
        name="mlp",
    )(*args)


S5_CH = 128
S5_BLK = (S5_CH // S5_GROUP) * S5_STATE


def _s5_disc_kernel(lr_ref, li_ref, ldt_ref, br_ref, bi_ref, ar_ref, ai_ref, bbr_ref, bbi_ref):
    lr, li = lr_ref[...], li_ref[...]
    dt = jnp.exp(ldt_ref[...])
    mag = jnp.exp(lr * dt)
    ab_re, ab_im = mag * jnp.cos(li * dt), mag * jnp.sin(li * dt)
    den = lr * lr + li * li
    nr = ab_re - 1.0
    f_re = (nr * lr + ab_im * li) / den
    f_im = (ab_im * lr - nr * li) / den
    ar_ref[...] = ab_re
    ai_ref[...] = ab_im
    for c in range(br_ref.shape[0]):
        br, bi = br_ref[c], bi_ref[c]
        bbr_ref[c] = f_re * br - f_im * bi
        bbi_ref[c] = f_re * bi + f_im * br


def _s5_weights(lam_re, lam_im, log_dt, b_re, b_im, c_re, c_im):
    g, p = lam_re.shape
    gc = b_re.shape[-1]
    brt = jnp.moveaxis(b_re, 2, 0)
    bit = jnp.moveaxis(b_im, 2, 0)
    ar, ai, bbr, bbi = pl.pallas_call(
        _s5_disc_kernel,
        out_shape=[jax.ShapeDtypeStruct((g, p), F32), jax.ShapeDtypeStruct((g, p), F32),
                   jax.ShapeDtypeStruct((gc, g, p), F32), jax.ShapeDtypeStruct((gc, g, p), F32)],
        name="s5_disc",
    )(lam_re, lam_im, log_dt.reshape(g, 1), brt, bit)
    nblk = (g * gc) // S5_CH
    gpb = g // nblk
    eye = jnp.eye(gpb, dtype=F32)

    def bd_in(bb):
        t = jnp.moveaxis(bb, 0, 1).reshape(nblk, gpb, gc, p)
        return jnp.einsum('ngcp,gh->ngchp', t, eye).reshape(nblk, gpb * gc, gpb * p)

    def bd_out(cc):
        t = cc.reshape(nblk, gpb, gc, p)
        return jnp.einsum('ngcp,gh->ngphc', t, eye).reshape(nblk, gpb * p, gpb * gc)

    wb = jnp.concatenate([bd_in(bbr), bd_in(bbi)], axis=-1).astype(BF16)
    wc = jnp.concatenate([bd_out(c_re), bd_out(c_im)], axis=1).astype(BF16)
    return ar.reshape(nblk, 1, gpb * p), ai.reshape(nblk, 1, gpb * p), wb, wc


def _gelu_tanh(y):
    return 0.5 * y * (1.0 + jnp.tanh(math.sqrt(2.0 / math.pi) * (y + 0.044715 * (y * y * y))))


def _cmul(ar, ai, xr, xi):
    return ar * xr - ai * xi, ar * xi + ai * xr


def _s5_prompt_kernel(x_ref, sh_ref, sc_ref, g_ref, d_ref, ar_ref, ai_ref, wb_ref, wc_ref,
                      perm_ref, unperm_ref, z_ref, fr_ref, fi_ref, u_s, br_s, bi_s, st_r, st_i,
                      *, nseg, seg):
    i = pl.program_id(1)
    nblk = wb_ref.shape[0]
    blk = ar_ref.shape[-1]

    @pl.when(i == 0)
    def _():
        st_r[...] = jnp.zeros(st_r.shape, F32)
        st_i[...] = jnp.zeros(st_i.shape, F32)

    u_s[...] = _dot_sel_lhs(perm_ref[...], _rms_mod(x_ref[0], g_ref[...], sh_ref[0], sc_ref[0]))

    for c in range(nblk):
        lo, hi = c * S5_CH, (c + 1) * S5_CH
        uc = u_s[:, lo:hi]
        bu = _dot(uc.astype(BF16), wb_ref[c])
        br_s[...] = bu[:, :blk]
        bi_s[...] = bu[:, blk:]
        ar = jnp.broadcast_to(ar_ref[c], (nseg, blk))
        ai = jnp.broadcast_to(ai_ref[c], (nseg, blk))

        def local(j, carry):
            xr, xi = carry
            r0 = pl.multiple_of(j * nseg, nseg)
            pr, pi = _cmul(ar, ai, xr, xi)
            nr = pr + br_s[pl.ds(r0, nseg), :]
            ni = pi + bi_s[pl.ds(r0, nseg), :]
            br_s[pl.ds(r0, nseg), :] = nr
            bi_s[pl.ds(r0, nseg), :] = ni
            return nr, ni

        zero = jnp.zeros((nseg, blk), F32)
        fr, fi = lax.fori_loop(0, seg, local, (zero, zero))

        pr, pi = ar_ref[c], ai_ref[c]
        for _ in range(int(math.log2(seg))):
            pr, pi = _cmul(pr, pi, pr, pi)
        cr, ci = st_r[c], st_i[c]
        rows_r, rows_i = [], []
        for s in range(nseg):
            rows_r.append(cr)
            rows_i.append(ci)
            tr, ti = _cmul(pr, pi, cr, ci)
            cr, ci = tr + fr[s:s + 1], ti + fi[s:s + 1]
        st_r[c] = cr
        st_i[c] = ci
        fr_ref[0, c] = cr
        fi_ref[0, c] = ci
        dr, di = _cmul(ar, ai, jnp.concatenate(rows_r, axis=0), jnp.concatenate(rows_i, axis=0))

        def fix(j, carry):
            dr, di = carry
            r0 = pl.multiple_of(j * nseg, nseg)
            br_s[pl.ds(r0, nseg), :] = br_s[pl.ds(r0, nseg), :] + dr
            bi_s[pl.ds(r0, nseg), :] = bi_s[pl.ds(r0, nseg), :] + di
            return _cmul(ar, ai, dr, di)

        lax.fori_loop(0, seg, fix, (dr, di))

        y = (_dot(br_s[...].astype(BF16), wc_ref[c, :blk, :])
             - _dot(bi_s[...].astype(BF16), wc_ref[c, blk:, :]))
        y = y + d_ref[:, lo:hi] * uc
        u_s[:, lo:hi] = _gelu_tanh(y)

    z_ref[0] = _dot_sel_lhs(unperm_ref[...], u_s[...])


def _s5_prompt(x, shift, scale, g, d_skip, ar, ai, wb, wc):
    bsz, seq, d = x.shape
    nblk, _, blk = ar.shape
    nseg, seg = SUBLANE, 32
    tm = nseg * seg
    assert seq % tm == 0
    perm = np.zeros((tm, tm), np.float32)
    for s in range(nseg):
        for j in range(seg):
            perm[j * nseg + s, s * seg + j] = 1.0
    unperm = jnp.asarray(perm.T, BF16)
    perm = jnp.asarray(perm, BF16)
    z, fr, fi = pl.pallas_call(
        functools.partial(_s5_prompt_kernel, nseg=nseg, seg=seg),
        grid=(bsz, seq // tm),
        in_specs=[pl.BlockSpec((1, tm, d), lambda b, i: (b, i, 0)),
                  _row_block(shift, tm), _row_block(scale, tm),
                  _const2((1, d)), _const2((1, d)),
                  _const2(ar.shape), _const2(ai.shape), _const2(wb.shape), _const2(wc.shape),
                  _const2((tm, tm)), _const2((tm, tm))],
        out_specs=[pl.BlockSpec((1, tm, d), lambda b, i: (b, i, 0)),
                   pl.BlockSpec((1, nblk, 1, blk), lambda b, i: (b, 0, 0, 0)),
                   pl.BlockSpec((1, nblk, 1, blk), lambda b, i: (b, 0, 0, 0))],
        out_shape=[jax.ShapeDtypeStruct((bsz, seq, d), F32),
                   jax.ShapeDtypeStruct((bsz, nblk, 1, blk), F32),
                   jax.ShapeDtypeStruct((bsz, nblk, 1, blk), F32)],
        scratch_shapes=[pltpu.VMEM((tm, d), F32), pltpu.VMEM((tm, blk), F32), pltpu.VMEM((tm, blk), F32),
                        pltpu.VMEM((nblk, 1, blk), F32), pltpu.VMEM((nblk, 1, blk), F32)],
        compiler_params=_cparams(2),
        name="s5_prompt",
    )(x, shift, scale, g.reshape(1, d), d_skip.reshape(1, d), ar, ai, wb, wc, perm, unperm)
    return z, fr, fi


def _s5_sample_kernel(x_ref, sh_ref, sc_ref, g_ref, d_ref, ar_ref, ai_ref, wb_ref, wc_ref,
                      s0r_ref, s0i_ref, z_ref, fr_ref, fi_ref, *, nb, steps):
    nblk = wb_ref.shape[0]
    blk = ar_ref.shape[-1]
    u = _rms_mod(x_ref[...], g_ref[...], sh_ref[...], sc_ref[...])
    for c in range(nblk):
        lo, hi = c * S5_CH, (c + 1) * S5_CH
        uc = u[:, lo:hi]
        bu = _dot(uc.astype(BF16), wb_ref[c])
        ar = jnp.broadcast_to(ar_ref[c], (nb, blk))
        ai = jnp.broadcast_to(ai_ref[c], (nb, blk))
        xr, xi = s0r_ref[:, c * blk:(c + 1) * blk], s0i_ref[:, c * blk:(c + 1) * blk]
        xrs, xis = [], []
        for t in range(steps):
            pr, pi = _cmul(ar, ai, xr, xi)
            xr = pr + bu[t * nb:(t + 1) * nb, :blk]
            xi = pi + bu[t * nb:(t + 1) * nb, blk:]
            xrs.append(xr)
            xis.append(xi)
        fr_ref[:, c * blk:(c + 1) * blk] = xr
        fi_ref[:, c * blk:(c + 1) * blk] = xi
        y = (_dot(jnp.concatenate(xrs, axis=0).astype(BF16), wc_ref[c, :blk, :])
             - _dot(jnp.concatenate(xis, axis=0).astype(BF16), wc_ref[c, blk:, :]))
        y = y + d_ref[:, lo:hi] * uc
        z_ref[:, lo:hi] = _gelu_tanh(y)


def _s5_sample(x_tb, shift_tb, scale_tb, g, d_skip, ar, ai, wb, wc, s0r, s0i, nb, steps):
    rows, d = x_tb.shape
    nblk, _, blk = ar.shape
    return pl.pallas_call(
        functools.partial(_s5_sample_kernel, nb=nb, steps=steps),
        out_shape=[jax.ShapeDtypeStruct((rows, d), F32),
                   jax.ShapeDtypeStruct((nb, nblk * blk), F32),
                   jax.ShapeDtypeStruct((nb, nblk * blk), F32)],
        compiler_params=pltpu.CompilerParams(vmem_limit_bytes=VMEM_LIMIT),
        name="s5_sample",
    )(x_tb, shift_tb, scale_tb, g.reshape(1, d), d_skip.reshape(1, d), ar, ai, wb, wc, s0r, s0i)


def _shift_rows(u, prev, k):
    if k == 0:
        return u
    rolled = pltpu.roll(u, k, 0)
    head = pltpu.roll(prev, k, 0)
    row = lax.broadcasted_iota(I32, (SUBLANE, u.shape[1]), 0)
    first = jnp.where(row < k, head, rolled[:SUBLANE])
    return jnp.concatenate([first, rolled[SUBLANE:]], axis=0)


def _sconv_prompt_kernel(p_ref, x_ref, gate_ref, buf_ref, wc_ref, wo_ref, y_ref, nb_ref, prev_s,
                         *, width):
    i = pl.program_id(1)
    d = x_ref.shape[-1]

    @pl.when(i == 0)
    def _():
        prev_s[...] = jnp.zeros(prev_s.shape, F32)
        prev_s[SUBLANE - (width - 1):, :] = buf_ref[0]

    p = p_ref[0]
    gb, gc, xh = p[:, :d], p[:, d:2 * d], p[:, 2 * d:]
    u = gc * xh
    prev = prev_s[...]
    conv = u * wc_ref[width - 1:width, :]
    for k in range(1, width):
        conv = conv + _shift_rows(u, prev, k) * wc_ref[width - 1 - k:width - k, :]
    prev_s[...] = u[u.shape[0] - SUBLANE:, :]
    nb_ref[0] = u[u.shape[0] - (width - 1):, :]
    y_ref[0] = x_ref[0] + gate_ref[0] * _dot((gb * conv).astype(BF16), wo_ref[...])


def _sconv_prompt(p, x, gate, buf, w_conv, w_out_bf16, tm):
    bsz, seq, d = x.shape
    width = w_conv.shape[0]
    return pl.pallas_call(
        functools.partial(_sconv_prompt_kernel, width=width),
        grid=(bsz, seq // tm),
        in_specs=[pl.BlockSpec((1, tm, 3 * d), lambda b, i: (b, i, 0)),
                  pl.BlockSpec((1, tm, d), lambda b, i: (b, i, 0)),
                  _row_block(gate, tm),
                  pl.BlockSpec((1, width - 1, d), lambda b, i: (b, 0, 0)),
                  _const2((width, d)), _const2((d, d))],
        out_specs=[pl.BlockSpec((1, tm, d), lambda b, i: (b, i, 0)),
                   pl.BlockSpec((1, width - 1, d), lambda b, i: (b, 0, 0))],
        out_shape=[jax.ShapeDtypeStruct((bsz, seq, d), F32),
                   jax.ShapeDtypeStruct((bsz, width - 1, d), F32)],
        scratch_shapes=[pltpu.VMEM((SUBLANE, d), F32)],
        compiler_params=_cparams(2),
        name="sconv_prompt",
    )(p, x, gate, buf, w_conv, w_out_bf16)


def _sconv_sample_kernel(p_ref, x_ref, gate_ref, buf_ref, wc_ref, wo_ref, y_ref, nb_ref,
                         *, width, nb, steps):
    d = x_ref.shape[-1]
    p = p_ref[...]
    gb, gc, xh = p[:, :d], p[:, d:2 * d], p[:, 2 * d:]
    u = gc * xh
    ext = jnp.concatenate([buf_ref[...], u], axis=0)
    conv = ext[0:steps * nb] * wc_ref[0:1, :]
    for k in range(1, width):
        conv = conv + ext[k * nb:(k + steps) * nb] * wc_ref[k:k + 1, :]
    nb_ref[...] = ext[steps * nb:]
    y_ref[...] = x_ref[...] + gate_ref[...] * _dot((gb * conv).astype(BF16), wo_ref[...])


def _sconv_sample(p_tb, x_tb, gate_tb, buf_kb, w_conv, w_out_bf16, nb, steps):
    rows, d = x_tb.shape
    width = w_conv.shape[0]
    return pl.pallas_call(
        functools.partial(_sconv_sample_kernel, width=width, nb=nb, steps=steps),
        out_shape=[jax.ShapeDtypeStruct((rows, d), F32),
                   jax.ShapeDtypeStruct(((width - 1) * nb, d), F32)],
        compiler_params=pltpu.CompilerParams(vmem_limit_bytes=VMEM_LIMIT),
        name="sconv_sample",
    )(p_tb, x_tb, gate_tb, buf_kb, w_conv, w_out_bf16)


def _sortable(s):
    bits = lax.bitcast_convert_type(s + 0.0, I32)
    return bits ^ ((bits >> 31) & 0x7FFFFFFF)


def _stack_heads(qcat):
    w = 4 * IDX_DIM
    return jnp.concatenate([qcat[:, h * w:(h + 1) * w] for h in range(IDX_HEADS)], axis=0)


def _idx_rhs(ki):
    kh, kl = _split(ki)
    return jnp.concatenate([kh, kh, kl, jnp.zeros(kh.shape, BF16)], axis=1)


def _idx_score(dts, wcols, nrows):
    s = wcols[0] * jnp.maximum(dts[0:nrows], 0.0)
    for h in range(1, IDX_HEADS):
        s = s + wcols[h] * jnp.maximum(dts[h * nrows:(h + 1) * nrows], 0.0)
    return s


def _idx_wcols(kw):
    sc = (IDX_HEADS ** -0.5) * (IDX_DIM ** -0.5)
    return [kw[:, IDX_DIM + h:IDX_DIM + h + 1] * sc for h in range(IDX_HEADS)]


def _topk_select(count, rows, lanes, topk, idx_bits, stash=lambda v: (lambda: v)):
    wide = lambda v: jnp.broadcast_to(v, (rows, lanes))

    def bit_step(it, tu):
        mask = jnp.left_shift(jnp.int32(1), 31 - it)
        cand_u = tu | mask
        cand = stash(wide(cand_u ^ INT_MIN))
        cnt = count(lambda c, key, idx: jnp.where(key >= cand(), 1.0, 0.0))
        return jnp.where(cnt >= topk, cand_u, tu)

    tu = lax.fori_loop(0, 32, bit_step, jnp.zeros((rows, 1), I32))
    t = tu ^ INT_MIN
    tw = wide(t)
    n_gt = count(lambda c, key, idx: jnp.where(key > tw, 1.0, 0.0))
    n_ge = count(lambda c, key, idx: jnp.where(key >= tw, 1.0, 0.0))
    need = topk - n_gt
    tied = jnp.max(jnp.where(t != INT_MIN, n_ge - topk, 0.0)) > 0.0

    def tie_search(_):
        def idx_step(it, j):
            cand = j | jnp.left_shift(jnp.int32(1), idx_bits - 1 - it)
            cw = wide(cand)
            cnt = count(lambda c, key, idx: jnp.where((key == tw) & (idx < cw), 1.0, 0.0))
            return jnp.where(cnt < need, cand, j)
        return lax.fori_loop(0, idx_bits, idx_step, jnp.zeros((rows, 1), I32))

    j = lax.cond(tied, tie_search, lambda _: jnp.full((rows, 1), 2 ** idx_bits, I32), 0)
    return t, j


def _dsa_prompt_kernel(q_ref, qcat_ref, kwq_ref, kcat_ref, kaug_ref, vaug_ref, bias_ref, o_ref,
                       key_s, madd_s, wbc_s, cand_s, qa_s, qm_s, mx_s, acc_s, *, tq, topk, idx_bits):
    i = pl.program_id(1)
    nk = i + 1
    nk2 = (nk + 1) // 2
    r = N_HEADS // N_KV_HEADS

    for h, w in enumerate(_idx_wcols(kwq_ref[0])):
        wbc_s[h] = jnp.broadcast_to(w, (tq, tq))
    qrow = lax.broadcasted_iota(I32, (tq, tq), 0)
    kcol = lax.broadcasted_iota(I32, (tq, tq), 1)
    qpos = i * tq + qrow
    wq = 4 * IDX_DIM

    def score_pair(c2, _):
        kc = kcat_ref[0, c2]
        s = None
        for h in range(IDX_HEADS):
            w = wbc_s[h]
            d = jnp.concatenate([w, w], axis=1) * jnp.maximum(_dot(qcat_ref[0, :, h * wq:(h + 1) * wq], kc), 0.0)
            s = d if s is None else s + d
        for u in range(2):
            c = 2 * c2 + u
            key_s[c] = jnp.where(c * tq + kcol <= qpos, _sortable(s[:, u * tq:(u + 1) * tq]), INT_MIN)
        return 0

    lax.fori_loop(0, nk2, score_pair, 0)

    def count(fn):
        def body(c2, acc):
            c = 2 * c2
            acc = acc + fn(c, key_s[c], c * tq + kcol)
            return acc + fn(c + 1, key_s[c + 1], (c + 1) * tq + kcol)
        acc = lax.fori_loop(0, nk2, body, jnp.zeros((tq, tq), F32))
        return jnp.sum(acc, axis=1, keepdims=True)

    def stash(v):
        cand_s[...] = v
        return lambda: cand_s[...]

    t, j = _topk_select(count, tq, tq, float(topk), idx_bits, stash)
    tw = jnp.broadcast_to(t, (tq, tq))
    jw = jnp.broadcast_to(j, (tq, tq))

    def mask_pair(c2, _):
        for u in range(2):
            c = 2 * c2 + u
            key = key_s[c]
            sel = (key > tw) | ((key == tw) & (c * tq + kcol <= jw))
            madd_s[c] = jnp.where(sel & (key != INT_MIN), 0.0, NEG_BIG)
        return 0

    lax.fori_loop(0, nk2, mask_pair, 0)

    q = q_ref[0]
    zpad = jnp.zeros((tq, LANE - HEAD_DIM), F32)
    qa = []
    for g in range(N_KV_HEADS):
        qa.append(jnp.concatenate(
            [jnp.concatenate([q[:, (g * r + u) * HEAD_DIM:(g * r + u + 1) * HEAD_DIM], zpad], axis=1)
             for u in range(r)], axis=0))
        qa_s[g] = qa[g].astype(BF16)
    mx_s[...] = jnp.full(mx_s.shape, NEG_BIG, F32)

    def pair_mask(c2, near, g):
        halves = []
        for u in range(2):
            c = 2 * c2 + u
            m = madd_s[c]
            if near:
                back = nk - 1 - c
                carries = (back == 0) | ((back == 1) & (kcol > qrow))
                halves.append([m + jnp.where(carries, bias_ref[g * r + v], 0.0) for v in range(r)])
            else:
                halves.append([m] * r)
        return jnp.concatenate([jnp.concatenate([halves[0][v], halves[1][v]], axis=1) for v in range(r)], axis=0)

    def max_pair(c2, near):
        for g in range(N_KV_HEADS):
            lg = _dot(qa_s[g], kaug_ref[0, c2, g * LANE:(g + 1) * LANE, :]) + pair_mask(c2, near, g)
            mx_s[g] = jnp.maximum(mx_s[g], jnp.maximum(lg[:, :tq], lg[:, tq:]))

    def pv_pair(c2, near):
        r0 = pl.multiple_of(c2 * 2 * tq, 2 * tq)
        va = vaug_ref[0, pl.ds(r0, 2 * tq), :]
        for g in range(N_KV_HEADS):
            lg = _dot(qm_s[g], kaug_ref[0, c2, g * LANE:(g + 1) * LANE, :]) + pair_mask(c2, near, g)
            acc_s[g] = acc_s[g] + _dot(jnp.exp(lg).astype(BF16), va[:, g * LANE:(g + 1) * LANE])

    def all_blocks(pair_fn):
        def far(c2, _):
            pair_fn(c2, False)
            return 0

        lax.fori_loop(0, jnp.maximum(nk2 - 2, 0), far, 0)

        @pl.when(nk2 >= 2)
        def _():
            pair_fn(nk2 - 2, True)

        pair_fn(nk2 - 1, True)

    all_blocks(max_pair)
    lane_a = lax.broadcasted_iota(I32, (r * tq, LANE), 1)
    for g in range(N_KV_HEADS):
        shift = jnp.max(mx_s[g], axis=1, keepdims=True).astype(BF16).astype(F32)
        qm_s[g] = jnp.where(lane_a == HEAD_DIM, shift, qa[g]).astype(BF16)
    acc_s[...] = jnp.zeros(acc_s.shape, F32)
    all_blocks(pv_pair)
    for g in range(N_KV_HEADS):
        acc = acc_s[g]
        o = acc[:, 0:HEAD_DIM] / acc[:, HEAD_DIM:HEAD_DIM + 1]
        for u in range(r):
            h = g * r + u
            o_ref[0, :, h * HEAD_DIM:(h + 1) * HEAD_DIM] = o[u * tq:(u + 1) * tq, :]


def _t5_bucket(dist):
    n = jnp.maximum(dist, 0)
    max_exact = N_BUCKETS // 2
    nf = jnp.maximum(n, max_exact).astype(F32)
    large = max_exact + (jnp.log(nf / max_exact) / math.log(MAX_DISTANCE / max_exact)
                         * (N_BUCKETS - max_exact)).astype(I32)
    large = jnp.minimum(large, N_BUCKETS - 1)
    return jnp.where(n < max_exact, n, large)


def _bias_table(rel_bias, n):
    tab = rel_bias[_t5_bucket(jnp.arange(n, dtype=I32))]
    return (tab - rel_bias[N_BUCKETS - 1][None, :]).T


def _dsa_prompt(q, qcat, kw, kcat, kaug, vaug, rel_bias, tq):
    bsz, seq, nq = q.shape
    topk = min(TOPK_MAX, seq // 4)
    nkc = seq // tq
    r = N_HEADS // N_KV_HEADS
    assert tq == MAX_DISTANCE and seq % (2 * tq) == 0
    tab = _bias_table(rel_bias, tq)
    dmod = (np.arange(tq)[:, None] - np.arange(tq)[None, :]) % tq
    onehot = (jnp.arange(tq, dtype=I32)[:, None, None] == jnp.asarray(dmod, I32)[None]).astype(F32)
    bias = jnp.einsum('hd,dqk->hqk', tab, onehot, precision=lax.Precision.HIGHEST)
    assert kcat.shape[1:] == (seq // (2 * tq), 4 * IDX_DIM, 2 * tq)
    row = lambda n: pl.BlockSpec((1, tq, n), lambda b, i: (b, i, 0))
    tiles = lambda a: pl.BlockSpec((1,) + a.shape[1:], lambda b, i: (b, 0, 0, 0))
    return pl.pallas_call(
        functools.partial(_dsa_prompt_kernel, tq=tq, topk=topk, idx_bits=int(math.log2(seq))),
        grid=(bsz, nkc),
        in_specs=[row(nq), row(qcat.shape[-1]), row(LANE), tiles(kcat), tiles(kaug),
                  pl.BlockSpec((1, seq, vaug.shape[-1]), lambda b, i: (b, 0, 0)),
                  _const2((N_HEADS, tq, tq))],
        out_specs=row(nq),
        out_shape=jax.ShapeDtypeStruct((bsz, seq, nq), F32),
        scratch_shapes=[pltpu.VMEM((nkc, tq, tq), I32), pltpu.VMEM((nkc, tq, tq), F32),
                        pltpu.VMEM((IDX_HEADS, tq, tq), F32), pltpu.VMEM((tq, tq), I32),
                        pltpu.VMEM((N_KV_HEADS, r * tq, LANE), BF16),
                        pltpu.VMEM((N_KV_HEADS, r * tq, LANE), BF16),
                        pltpu.VMEM((N_KV_HEADS, r * tq, tq), F32),
                        pltpu.VMEM((N_KV_HEADS, r * tq, LANE), F32)],
        compiler_params=_cparams(2),
        name="dsa_prompt",
    )(q, qcat, kw, kcat, kaug, vaug, bias)


QP = SUBLANE
PAGES_PER_STEP = 16


def _page_specs(shape4, layer, pps):
    def spec(u):
        return pl.BlockSpec((1, 1) + tuple(shape4[2:]),
                            lambda b, g, pt: (layer, pt[b, g * pps + u], 0, 0))
    return [spec(u) for u in range(pps)]


def _pages_t(cache):
    nd = cache.ndim
    t = jnp.transpose(cache, (0, 1) + tuple(range(3, nd)) + (2,))
    return t.reshape(t.shape[0], t.shape[1], -1, t.shape[-1])


def _pad_t(a, page):
    return jnp.concatenate([a, jnp.zeros((page - QP, a.shape[1]), F32)], axis=0).T


def _dsa_sample_score_kernel(pt_ref, qcat_ref, kw_ref, kcn_ref, *rest, pps):
    page_refs, o_ref, on_ref = rest[:pps], rest[pps], rest[pps + 1]
    lhs = _stack_heads(qcat_ref[0]).astype(BF16)
    wcols = _idx_wcols(kw_ref[0])
    kt = jnp.concatenate([page_refs[u][0, 0] for u in range(pps)], axis=1)
    kh, kl = _split(kt)
    rhs = jnp.concatenate([kh, kh, kl, jnp.zeros(kh.shape, BF16)], axis=0)
    o_ref[0] = _idx_score(_dot(lhs, rhs), wcols, QP)

    @pl.when(pl.program_id(1) == pl.num_programs(1) - 1)
    def _():
        on_ref[0] = _idx_score(_dot(lhs, _pad_t(kcn_ref[0], on_ref.shape[-1]).astype(BF16)), wcols, QP)


def _dsa_sample_scores(page_table, qcat, kw, kcn, kidx_t, layer, pps):
    nb, n_pages = page_table.shape
    page = kidx_t.shape[-1]
    bmap = lambda b, g, pt: (b, 0, 0)
    grid_spec = pltpu.PrefetchScalarGridSpec(
        num_scalar_prefetch=1,
        grid=(nb, n_pages // pps),
        in_specs=[pl.BlockSpec((1, QP, qcat.shape[-1]), bmap), pl.BlockSpec((1, QP, LANE), bmap),
                  pl.BlockSpec((1, QP, kcn.shape[-1]), bmap)] + _page_specs(kidx_t.shape, layer, pps),
        out_specs=[pl.BlockSpec((1, QP, pps * page), lambda b, g, pt: (b, 0, g)),
                   pl.BlockSpec((1, QP, page), bmap)])
    return pl.pallas_call(
        functools.partial(_dsa_sample_score_kernel, pps=pps),
        grid_spec=grid_spec,
        out_shape=[jax.ShapeDtypeStruct((nb, QP, n_pages * page), F32),
                   jax.ShapeDtypeStruct((nb, QP, page), F32)],
        compiler_params=_cparams(2),
        name="dsa_sample_scores",
    )(page_table, qcat, kw, kcn, *([kidx_t] * pps))


def _new_token_keys(s_new, steps):
    lane = lax.broadcasted_iota(I32, s_new.shape, 1)
    q = lax.broadcasted_iota(I32, s_new.shape, 0) % QP
    return jnp.where((lane <= q) & (lane < steps), _sortable(s_new), INT_MIN)


def _dsa_sample_select_kernel(sc_ref, sn_ref, t_ref, j_ref, *, steps, topk, idx_bits):
    keys = jnp.concatenate([_sortable(sc_ref[...]), _new_token_keys(sn_ref[...], steps)], axis=1)
    idx = lax.broadcasted_iota(I32, keys.shape, 1)
    count = lambda fn: jnp.sum(fn(0, keys, idx), axis=1, keepdims=True)
    t, j = _topk_select(count, keys.shape[0], keys.shape[1], float(topk), idx_bits)
    t_ref[...] = t
    j_ref[...] = j


def _dsa_sample_select(scores, snew, steps, topk):
    nb, _, past = scores.shape
    page = snew.shape[-1]
    rows = nb * QP
    rb = math.gcd(rows, 64)
    t, j = pl.pallas_call(
        functools.partial(_dsa_sample_select_kernel, steps=steps, topk=topk,
                          idx_bits=int(math.ceil(math.log2(past + page)))),
        grid=(rows // rb,),
        in_specs=[pl.BlockSpec((rb, past), lambda i: (i, 0)), pl.BlockSpec((rb, page), lambda i: (i, 0))],
        out_specs=[pl.BlockSpec((rb, 1), lambda i: (i, 0)), pl.BlockSpec((rb, 1), lambda i: (i, 0))],
        out_shape=[jax.ShapeDtypeStruct((rows, 1), I32), jax.ShapeDtypeStruct((rows, 1), I32)],
        compiler_params=_cparams(1),
        name="dsa_sample_select",
    )(scores.reshape(rows, past), snew.reshape(rows, page))
    return t.reshape(nb, QP, 1), j.reshape(nb, QP, 1)


def _dsa_sample_attend_kernel(pt_ref, q_ref, kn_ref, vn_ref, sc_ref, sn_ref, t_ref, j_ref,
                              blast_ref, bnew_ref, *rest, pps, n_pages, steps):
    k_refs, v_refs, o_ref = rest[:pps], rest[pps:2 * pps], rest[2 * pps]
    lhs_s, m_s, l_s, acc_s = rest[2 * pps + 1:]
    g = pl.program_id(1)
    last = pl.num_programs(1) - 1
    page = k_refs[0].shape[-1]
    past = n_pages * page
    n = pps * page
    r = N_HEADS // N_KV_HEADS
    grow = r * QP

    @pl.when(g == 0)
    def _():
        m_s[...] = jnp.full(m_s.shape, NEG_BIG, F32)
        l_s[...] = jnp.zeros(l_s.shape, F32)
        acc_s[...] = jnp.zeros(acc_s.shape, F32)
        q = q_ref[0]
        for gg in range(N_KV_HEADS):
            lhs_s[gg] = jnp.concatenate(
                [q[:, (gg * r + u) * HEAD_DIM:(gg * r + u + 1) * HEAD_DIM] for u in range(r)], axis=0).astype(BF16)

    t, j = t_ref[0], j_ref[0]

    def select(key, idx):
        sel = (key > t) | ((key == t) & (idx <= j))
        return jnp.where(sel & (key != INT_MIN), 0.0, NEG_BIG)

    def attend(kt, vt, madd, bias):
        mrows = jnp.concatenate([madd] * r, axis=0)
        for gg in range(N_KV_HEADS):
            rows = slice(gg * grow, (gg + 1) * grow)
            feat = slice(gg * HEAD_DIM, (gg + 1) * HEAD_DIM)
            lg = _dot(lhs_s[gg], kt[feat, :]) + mrows
            if bias is not None:
                lg = lg + bias[rows, :]
            m_old = m_s[rows, :]
            m_new = jnp.maximum(m_old, jnp.max(lg, axis=1, keepdims=True))
            p = jnp.exp(lg - m_new)
            alpha = jnp.exp(m_old - m_new)
            l_s[rows, :] = alpha * l_s[rows, :] + jnp.sum(p, axis=1, keepdims=True)
            acc_s[rows, :] = alpha * acc_s[rows, :] + _dot_nt(p.astype(BF16), vt[feat, :])
            m_s[rows, :] = m_new

    def pages(bias):
        kt = jnp.concatenate([k_refs[u][0, 0] for u in range(pps)], axis=1).astype(BF16)
        vt = jnp.concatenate([v_refs[u][0, 0] for u in range(pps)], axis=1).astype(BF16)
        idx = g * n + lax.broadcasted_iota(I32, (QP, n), 1)
        attend(kt, vt, select(_sortable(sc_ref[0]), idx), bias)

    @pl.when(g != last)
    def _():
        pages(None)

    @pl.when(g == last)
    def _():
        pages(jnp.concatenate([jnp.zeros((N_HEADS * QP, n - page), F32), blast_ref[...]], axis=1))
        idx = past + lax.broadcasted_iota(I32, (QP, page), 1)
        attend(_pad_t(kn_ref[0], page).astype(BF16), _pad_t(vn_ref[0], page).astype(BF16),
               select(_new_token_keys(sn_ref[0], steps), idx), bnew_ref[...])
        o = acc_s[...] / l_s[...]
        for h in range(N_HEADS):
            o_ref[0, :, h * HEAD_DIM:(h + 1) * HEAD_DIM] = o[h * QP:(h + 1) * QP, :]


def _dsa_sample(page_table, q, kn, vn, scores, snew, k_t, v_t, rel_bias, layer, steps, pps):
    nb, n_pages = page_table.shape
    page = k_t.shape[-1]
    past = n_pages * page
    nq, nkv = N_HEADS * HEAD_DIM, N_KV_HEADS * HEAD_DIM
    t, j = _dsa_sample_select(scores, snew, steps, min(TOPK_MAX, (past + steps) // 4))
    n_steps = n_pages // pps
    r = N_HEADS // N_KV_HEADS
    tab = _bias_table(rel_bias, 2 * page)
    qq = np.arange(QP)[:, None]
    off = np.arange(page)[None, :]
    blast = tab[:, page + qq - off].reshape(N_HEADS * QP, page)
    bnew = tab[:, np.maximum(qq - off, 0)].reshape(N_HEADS * QP, page)
    bmap = lambda b, g, pt: (b, 0, 0)
    grid_spec = pltpu.PrefetchScalarGridSpec(
        num_scalar_prefetch=1,
        grid=(nb, n_steps),
        in_specs=[pl.BlockSpec((1, QP, nq), bmap),
                  pl.BlockSpec((1, QP, nkv), bmap),
                  pl.BlockSpec((1, QP, nkv), bmap),
                  pl.BlockSpec((1, QP, pps * page), lambda b, g, pt: (b, 0, g)),
                  pl.BlockSpec((1, QP, page), bmap),
                  pl.BlockSpec((1, QP, 1), bmap), pl.BlockSpec((1, QP, 1), bmap),
                  pl.BlockSpec(blast.shape, lambda b, g, pt: (0, 0)),
                  pl.BlockSpec(bnew.shape, lambda b, g, pt: (0, 0))]
                 + _page_specs(k_t.shape, layer, pps) + _page_specs(v_t.shape, layer, pps),
        out_specs=pl.BlockSpec((1, QP, nq), bmap),
        scratch_shapes=[pltpu.VMEM((N_KV_HEADS, r * QP, HEAD_DIM), BF16),
                        pltpu.VMEM((N_HEADS * QP, 1), F32), pltpu.VMEM((N_HEADS * QP, 1), F32),
                        pltpu.VMEM((N_HEADS * QP, HEAD_DIM), F32)])
    return pl.pallas_call(
        functools.partial(_dsa_sample_attend_kernel, pps=pps, n_pages=n_pages, steps=steps),
        grid_spec=grid_spec,
        out_shape=jax.ShapeDtypeStruct((nb, QP, nq), F32),
        compiler_params=_cparams(2),
        name="dsa_sample_attend",
    )(page_table, q, kn, vn, scores, snew, t, j, blast, bnew, *([k_t] * pps), *([v_t] * pps))


SSD_COLS = 512


def _softplus(x):
    return jnp.maximum(x, 0.0) + jnp.log1p(jnp.exp(-jnp.abs(x)))


def _split3(a):
    p1 = a.astype(BF16)
    r1 = a - p1.astype(F32)
    p2 = r1.astype(BF16)
    return p1, p2, (r1 - p2.astype(F32)).astype(BF16)


def _dot_sel_rhs(a, e):
    p1, p2, p3 = _split3(a)
    return _dot(p1, e) + (_dot(p2, e) + _dot(p3, e))


def _dot_sel_lhs(e, a):
    p1, p2, p3 = _split3(a)
    return _dot(e, p1) + (_dot(e, p2) + _dot(e, p3))


def _ssd_prompt_kernel(z_ref, xbc_ref, dtr_ref, cw_ref, cb_ref, dtb_ref, alog_ref, dexp_ref, ng_ref,
                       e_ref, y_ref, nbuf_ref, hf_ref, prev_s, xc_s, y_s, h_s, *, width, nh, inner):
    i = pl.program_id(1)
    last = pl.num_programs(1) - 1
    qn, cdim = xbc_ref.shape[1], xbc_ref.shape[2]
    ngrp, ns, hd = SSD_GROUPS, SSD_STATE, SSD_HEADDIM
    hpg = nh // ngrp

    @pl.when(i == 0)
    def _():
        prev_s[...] = jnp.zeros(prev_s.shape, F32)
        h_s[...] = jnp.zeros(h_s.shape, F32)

    for cb in range(cdim // SSD_COLS):
        cols = slice(cb * SSD_COLS, (cb + 1) * SSD_COLS)
        xr = xbc_ref[0, :, cols]
        prev = prev_s[:, cols]
        conv = xr * cw_ref[width - 1:width, cols]
        for k in range(1, width):
            conv = conv + _shift_rows(xr, prev, k) * cw_ref[width - 1 - k:width - k, cols]
        xc_s[:, cols] = _silu(conv + cb_ref[:, cols])
        prev_s[:, cols] = xr[qn - SUBLANE:, :]
        nbuf_ref[0, :, cols] = xr[qn - (width - 1):, :]

    hl = lax.broadcasted_iota(I32, (qn, LANE), 1)
    dt = jnp.where(hl < nh, _softplus(dtr_ref[0] + dtb_ref[...]), 0.0)
    dta = dt * (-jnp.exp(alog_ref[...]))
    qrow = lax.broadcasted_iota(I32, (qn, qn), 0)
    kcol = lax.broadcasted_iota(I32, (qn, qn), 1)
    causal = kcol <= qrow
    cum = _dot_sel_lhs(jnp.where(causal, 1.0, 0.0).astype(BF16), dta)
    cum_t = cum.T
    cum_last = cum[qn - 1:qn, :]
    ecum = jnp.exp(cum)
    dtw = dt * jnp.exp(cum_last - cum)
    e_last = jnp.exp(cum_last)
    lane_p = lax.broadcasted_iota(I32, (qn, LANE), 1)
    row_p = lax.broadcasted_iota(I32, (LANE, ns), 0)

    for g in range(ngrp):
        gl = slice(g * SSD_COLS, (g + 1) * SSD_COLS)
        e_g = e_ref[:, gl]
        xs_g = xc_s[:, gl]
        dtx_g = xs_g * _dot_sel_rhs(dt, e_g)
        dtxw_g = xs_g * _dot_sel_rhs(dtw, e_g)
        ecx_g = _dot_sel_rhs(ecum, e_g)
        bm = xc_s[:, inner + g * ns:inner + (g + 1) * ns].astype(BF16)
        cm = xc_s[:, inner + (ngrp + g) * ns:inner + (ngrp + g + 1) * ns].astype(BF16)
        cbm = _dot_nt(cm, bm)
        for pi in range(hpg // 2):
            ha = g * hpg + 2 * pi
            pls = slice(pi * LANE, (pi + 1) * LANE)
            lanes = slice(ha * hd, ha * hd + LANE)
            dtx_p = dtx_g[:, pls].astype(BF16)
            yds = []
            for h in (ha, ha + 1):
                seg = cum[:, h:h + 1] - cum_t[h:h + 1, :]
                dec = jnp.exp(jnp.where(causal, seg, -jnp.inf))
                yds.append(_dot((cbm * dec).astype(BF16), dtx_p))
            yd = jnp.where(lane_p < hd, yds[0], yds[1])
            hp = h_s[ha // 2]
            yo = _dot_nt(cm, hp.astype(BF16)) * ecx_g[:, pls]
            y_s[:, lanes] = yd + yo + dexp_ref[:, lanes] * xs_g[:, pls]
            s_new = _dot(dtxw_g[:, pls].T.astype(BF16), bm)
            cd = jnp.where(row_p < hd, e_last[:, ha:ha + 1], e_last[:, ha + 1:ha + 2])
            h_s[ha // 2] = hp * cd + s_new

    for g in range(ngrp):
        gl = slice(g * SSD_COLS, (g + 1) * SSD_COLS)
        yg = y_s[:, gl] * _silu(z_ref[0, :, gl])
        ms = jnp.mean(yg * yg, axis=-1, keepdims=True)
        y_ref[0, :, gl] = (yg * lax.rsqrt(ms + EPS)) * ng_ref[:, gl]

    @pl.when(i == last)
    def _():
        hf_ref[0] = h_s[...]


def _head_expand(nh, hd):
    e = np.zeros((LANE, nh * hd), np.float32)
    for h in range(nh):
        e[h, h * hd:(h + 1) * hd] = 1.0
    return jnp.asarray(e, BF16)


def _pad_lanes(v, n=LANE):
    v = v.reshape(1, -1)
    return jnp.pad(v, ((0, 0), (0, n - v.shape[1])))


def _ssd_prompt(z, xbc, dtr, conv_w, conv_b, dt_bias, a_log, d_skip, norm_g):
    bsz, seq, inner = z.shape
    cdim = xbc.shape[-1]
    nh = dt_bias.shape[0]
    width = conv_w.shape[0]
    qn = math.gcd(seq, SSD_CHUNK)
    assert qn == SSD_CHUNK and inner == SSD_GROUPS * SSD_COLS and cdim % SSD_COLS == 0
    dexp = jnp.repeat(d_skip, SSD_HEADDIM).reshape(1, inner)
    blk = lambda n: pl.BlockSpec((1, qn, n), lambda b, i: (b, i, 0))
    y, nbuf, hf = pl.pallas_call(
        functools.partial(_ssd_prompt_kernel, width=width, nh=nh, inner=inner),
        grid=(bsz, seq // qn),
        in_specs=[blk(inner), blk(cdim), blk(LANE),
                  _const2((width, cdim)), _const2((1, cdim)), _const2((1, LANE)), _const2((1, LANE)),
                  _const2((1, inner)), _const2((1, inner)), _const2((LANE, inner))],
        out_specs=[blk(inner),
                   pl.BlockSpec((1, width - 1, cdim), lambda b, i: (b, 0, 0)),
                   pl.BlockSpec((1, nh // 2, LANE, SSD_STATE), lambda b, i: (b, 0, 0, 0))],
        out_shape=[jax.ShapeDtypeStruct((bsz, seq, inner), F32),
                   jax.ShapeDtypeStruct((bsz, width - 1, cdim), F32),
                   jax.ShapeDtypeStruct((bsz, nh // 2, LANE, SSD_STATE), F32)],
        scratch_shapes=[pltpu.VMEM((SUBLANE, cdim), F32), pltpu.VMEM((qn, cdim), F32),
                        pltpu.VMEM((qn, inner), F32), pltpu.VMEM((nh // 2, LANE, SSD_STATE), F32)],
        compiler_params=_cparams(2),
        name="ssd_prompt",
    )(z, xbc, dtr, conv_w, conv_b.reshape(1, cdim), _pad_lanes(dt_bias), _pad_lanes(a_log),
      dexp, norm_g.reshape(1, inner), _head_expand(nh, SSD_HEADDIM))
    return y, nbuf, hf.reshape(bsz, nh, SSD_HEADDIM, SSD_STATE)


def _ssd_prep_sample_kernel(xbc_ref, buf_ref, dtr_ref, cw_ref, cb_ref, dtb_ref, xc_ref, dt_ref, nbuf_ref,
                            *, width, nb, steps, nh):
    cdim = xbc_ref.shape[1]
    for cb in range(cdim // SSD_COLS):
        cols = slice(cb * SSD_COLS, (cb + 1) * SSD_COLS)
        ext = jnp.concatenate([buf_ref[:, cols], xbc_ref[:, cols]], axis=0)
        conv = ext[0:steps * nb] * cw_ref[0:1, cols]
        for k in range(1, width):
            conv = conv + ext[k * nb:(k + steps) * nb] * cw_ref[k:k + 1, cols]
        xc_ref[:, cols] = _silu(conv + cb_ref[:, cols])
        nbuf_ref[:, cols] = ext[steps * nb:]
    hl = lax.broadcasted_iota(I32, dtr_ref.shape, 1)
    dt_ref[...] = jnp.where(hl < nh, _softplus(dtr_ref[...] + dtb_ref[...]), 0.0)


def _ssd_prep_sample(xbc_tb, buf_kb, dtr_tb, conv_w, conv_b, dt_bias, nb, steps):
    rows, cdim = xbc_tb.shape
    width = conv_w.shape[0]
    return pl.pallas_call(
        functools.partial(_ssd_prep_sample_kernel, width=width, nb=nb, steps=steps, nh=dt_bias.shape[0]),
        out_shape=[jax.ShapeDtypeStruct((rows, cdim), F32),
                   jax.ShapeDtypeStruct((rows, LANE), F32),
                   jax.ShapeDtypeStruct(((width - 1) * nb, cdim), F32)],
        compiler_params=pltpu.CompilerParams(vmem_limit_bytes=VMEM_LIMIT),
        name="ssd_prep_sample",
    )(xbc_tb, buf_kb, dtr_tb, conv_w, conv_b.reshape(1, cdim), _pad_lanes(dt_bias))


def _ssd_scan_sample_kernel(xc_ref, dt_ref, z_ref, h0_ref, alog_ref, dexp_ref, ng_ref, e_ref,
                            y_ref, hf_ref, *, steps, inner):
    ngrp, ns = SSD_GROUPS, SSD_STATE
    xc, dt = xc_ref[0], dt_ref[0]
    row = lax.broadcasted_iota(I32, (QP, LANE), 0)
    cum = dt * (-jnp.exp(alog_ref[...]))
    for sft in (1, 2, 4):
        cum = cum + jnp.where(row >= sft, pltpu.roll(cum, sft, 0), 0.0)
    cum_last = cum[QP - 1:QP, :]
    parts = [dt, jnp.exp(cum), dt * jnp.exp(cum_last - cum), jnp.broadcast_to(jnp.exp(cum_last), (QP, LANE))]
    for s in range(steps):
        parts.append(jnp.where(row >= s, jnp.exp(cum - cum[s:s + 1, :]), 0.0))
    stack = jnp.concatenate(parts, axis=0)

    for g in range(ngrp):
        gl = slice(g * SSD_COLS, (g + 1) * SSD_COLS)
        ex = _dot_sel_rhs(stack, e_ref[:, gl])
        dt_x, ecum_x, dtw_x, el_x = (ex[k * QP:(k + 1) * QP] for k in range(4))
        xs_g = xc[:, gl]
        dtx = xs_g * dt_x
        bm = xc[:, inner + g * ns:inner + (g + 1) * ns]
        cm = xc[:, inner + (ngrp + g) * ns:inner + (ngrp + g + 1) * ns].astype(BF16)
        bmp = jnp.concatenate([bm, jnp.zeros((LANE - QP, ns), F32)], axis=0).astype(BF16)
        cbm = _dot_nt(cm, bmp)
        yd = jnp.zeros((QP, SSD_COLS), F32)
        for s in range(steps):
            yd = yd + (ex[(4 + s) * QP:(5 + s) * QP] * cbm[:, s:s + 1]) * dtx[s:s + 1, :]
        h0g = h0_ref[0, gl, :]
        yo = _dot_nt(cm, h0g.astype(BF16)) * ecum_x
        y = (yd + yo + dexp_ref[:, gl] * xs_g) * _silu(z_ref[0, :, gl])
        ms = jnp.mean(y * y, axis=-1, keepdims=True)
        y_ref[0, :, gl] = (y * lax.rsqrt(ms + EPS)) * ng_ref[:, gl]
        tm = jnp.concatenate([xs_g * dtw_x, el_x[0:1], jnp.zeros((LANE - QP - 1, SSD_COLS), F32)], axis=0)
        tt = tm.T
        hf_ref[0, gl, :] = h0g * tt[:, QP:QP + 1] + _dot(tt.astype(BF16), bmp)


def _ssd_scan_sample(xc, dt, z, h0, a_log, d_skip, norm_g, steps):
    nb, _, cdim = xc.shape
    inner = z.shape[-1]
    nh = a_log.shape[0]
    rows_h = nh * SSD_HEADDIM
    dexp = jnp.repeat(d_skip, SSD_HEADDIM).reshape(1, inner)
    one = lambda shape: pl.BlockSpec(shape, lambda b: (0,) * len(shape))
    return pl.pallas_call(
        functools.partial(_ssd_scan_sample_kernel, steps=steps, inner=inner),
        grid=(nb,),
        in_specs=[pl.BlockSpec((1, QP, cdim), lambda b: (b, 0, 0)),
                  pl.BlockSpec((1, QP, LANE), lambda b: (b, 0, 0)),
                  pl.BlockSpec((1, QP, inner), lambda b: (b, 0, 0)),
                  pl.BlockSpec((1, rows_h, SSD_STATE), lambda b: (b, 0, 0)),
                  one((1, LANE)), one((1, inner)), one((1, inner)), one((LANE, inner))],
        out_specs=[pl.BlockSpec((1, QP, inner), lambda b: (b, 0, 0)),
                   pl.BlockSpec((1, rows_h, SSD_STATE), lambda b: (b, 0, 0))],
        out_shape=[jax.ShapeDtypeStruct((nb, QP, inner), F32),
                   jax.ShapeDtypeStruct((nb, rows_h, SSD_STATE), F32)],
        compiler_params=_cparams(1),
        name="ssd_scan_sample",
    )(xc, dt, z, h0, _pad_lanes(a_log), dexp, norm_g.reshape(1, inner), _head_expand(nh, SSD_HEADDIM))


TM = 256


def _to_steps(a, nb, steps):
    n = a.shape[-1]
    return a.reshape(nb, steps, n).transpose(1, 0, 2).reshape(steps * nb, n)


def _to_batch(a, nb, steps):
    n = a.shape[-1]
    return a.reshape(steps, nb, n).transpose(1, 0, 2).reshape(1, nb * steps, n)


def _pad_q(a, nb, steps):
    n = a.shape[-1]
    return jnp.pad(a.reshape(nb, steps, n), ((0, 0), (0, QP - steps), (0, 0)))


def kernel(x_prompt, x_sample, cache_k, cache_v, cache_kidx, state_s5_re, state_s5_im, state_sconv, state_ssd, state_ssd_conv, page_table, c_prompt, c_sample, rel_bias, ada_w, ada_b, norm_mix, norm_mlp, norm_final, attn_w_in, attn_w_out, s5_lam_re, s5_lam_im, s5_log_dt, s5_b_re, s5_b_im, s5_c_re, s5_c_im, s5_d, s5_w_glu, sc_w_in, sc_w_conv, sc_w_out, ssd_w_in, ssd_conv_w, ssd_conv_b, ssd_dt_bias, ssd_a_log, ssd_d, ssd_norm, ssd_w_out, mlp_w1, mlp_w2):
    bp, seq, d = x_prompt.shape
    nb, steps, _ = x_sample.shape
    depth = ada_w.shape[0]
    n_mixers = 4
    rs = nb * steps
    tm = min(TM, seq)

    rows = bp + nb
    c_all = jnp.pad(jnp.concatenate([c_prompt, c_sample], axis=0), ((0, (-rows) % SUBLANE), (0, 0)))
    ada = _ada(c_all, ada_w, ada_b)

    xp = x_prompt
    xs = x_sample.reshape(1, rs, d)
    outs = {name: [] for name in ("kp", "vp", "kip", "ks", "vs", "kis", "s5pr", "s5pi", "s5sr", "s5si",
                                  "scp", "scs", "ssdp", "ssdcp", "ssds", "ssdcs")}
    nq, nkv = N_HEADS * HEAD_DIM, N_KV_HEADS * HEAD_DIM
    for i in range(depth):
        m, j = i % n_mixers, i // n_mixers
        mp = [ada[i, :bp, k * d:(k + 1) * d].reshape(bp, 1, d) for k in range(6)]
        ms = [jnp.repeat(ada[i, bp:bp + nb, k * d:(k + 1) * d], steps, axis=0).reshape(1, rs, d)
              for k in range(6)]
        g_mix = norm_mix[i]
        mix_p = mix_s = None
        if m == 0:
            w_in = attn_w_in[j]
            w_qkv = w_in[:, :nq + 2 * nkv].astype(BF16)
            n_idx = IDX_HEADS * IDX_DIM + LANE
            w_idx = jnp.pad(w_in[:, nq + 2 * nkv:], ((0, 0), (0, n_idx - (w_in.shape[1] - nq - 2 * nkv))))
            w_out = attn_w_out[j].astype(BF16)
            q, k, v, kw, qcat, kcat, kaug, vaug = _attn_proj(xp, mp[0], mp[1], g_mix, w_qkv, w_idx,
                                                             2 * MAX_DISTANCE, True)
            o = _dsa_prompt(q, qcat, kw, kcat, kaug, vaug, rel_bias, MAX_DISTANCE)
            mix_p = ("proj", o, mp[2], w_out)
            page = cache_k.shape[2]
            outs["kp"].append(k.reshape(bp, seq // page, page, N_KV_HEADS, HEAD_DIM))
            outs["vp"].append(v.reshape(bp, seq // page, page, N_KV_HEADS, HEAD_DIM))
            outs["kip"].append(kw[..., :IDX_DIM].reshape(bp, seq // page, page, IDX_DIM))
            q, k, v, kw, qcat, kcat, _, _ = _attn_proj(xs, ms[0], ms[1], g_mix, w_qkv, w_idx, rs, False)
            pps = math.gcd(PAGES_PER_STEP, page_table.shape[1])
            qc_p, kw_p = _pad_q(qcat.astype(F32), nb, steps), _pad_q(kw, nb, steps)
            scores, snew = _dsa_sample_scores(page_table, qc_p, kw_p, _pad_q(kcat.astype(F32), nb, steps),
                                              _pages_t(cache_kidx), j, pps)
            o = _dsa_sample(page_table, _pad_q(q, nb, steps), _pad_q(k, nb, steps), _pad_q(v, nb, steps),
                            scores, snew, _pages_t(cache_k), _pages_t(cache_v), rel_bias, j, steps, pps)
            mix_s = ("proj", o[:, :steps].reshape(1, rs, nq), ms[2], w_out)
            outs["ks"].append(k.reshape(nb, steps, N_KV_HEADS, HEAD_DIM))
            outs["vs"].append(v.reshape(nb, steps, N_KV_HEADS, HEAD_DIM))
            outs["kis"].append(kw[..., :IDX_DIM].reshape(nb, steps, IDX_DIM))
        elif m == 1:
            ar, ai, wb, wc = _s5_weights(s5_lam_re[j], s5_lam_im[j], s5_log_dt[j], s5_b_re[j], s5_b_im[j],
                                         s5_c_re[j], s5_c_im[j])
            w_glu = s5_w_glu[j].astype(BF16)
            grp, nst = s5_lam_re.shape[1:]
            z, fr, fi = _s5_prompt(xp, mp[0], mp[1], g_mix, s5_d[j], ar, ai, wb, wc)
            mix_p = ("glu", z, mp[2], w_glu)
            outs["s5pr"].append(fr.reshape(bp, grp, nst))
            outs["s5pi"].append(fi.reshape(bp, grp, nst))
            z, fr, fi = _s5_sample(_to_steps(xs, nb, steps), _to_steps(ms[0], nb, steps),
                                   _to_steps(ms[1], nb, steps), g_mix, s5_d[j], ar, ai, wb, wc,
                                   state_s5_re[j].reshape(nb, grp * nst), state_s5_im[j].reshape(nb, grp * nst),
                                   nb, steps)
            mix_s = ("glu", _to_batch(z, nb, steps), ms[2], w_glu)
            outs["s5sr"].append(fr.reshape(nb, grp, nst))
            outs["s5si"].append(fi.reshape(nb, grp, nst))
        elif m == 2:
            w_in = sc_w_in[j].astype(BF16)
            w_out = sc_w_out[j].astype(BF16)
            width = sc_w_conv.shape[1]
            (p,) = _proj(xp, mp[0], mp[1], g_mix, w_in, (3 * d,), tm)
            xp, nbuf = _sconv_prompt(p, xp, mp[2], jnp.zeros((bp, width - 1, d), F32), sc_w_conv[j], w_out, tm)
            outs["scp"].append(nbuf)
            (p,) = _proj(xs, ms[0], ms[1], g_mix, w_in, (3 * d,), rs)
            buf = state_sconv[j].transpose(1, 0, 2).reshape((width - 1) * nb, d)
            y, nbuf = _sconv_sample(_to_steps(p, nb, steps), _to_steps(xs, nb, steps), _to_steps(ms[2], nb, steps),
                                    buf, sc_w_conv[j], w_out, nb, steps)
            xs = _to_batch(y, nb, steps)
            outs["scs"].append(nbuf.reshape(width - 1, nb, d).transpose(1, 0, 2))
        else:
            inner = ssd_norm.shape[1]
            cdim = ssd_conv_w.shape[2]
            nh = ssd_dt_bias.shape[1]
            width = ssd_conv_w.shape[1]
            w_in = jnp.pad(ssd_w_in[j], ((0, 0), (0, LANE - nh))).astype(BF16)
            w_out = ssd_w_out[j].astype(BF16)
            z, xbc, dtr = _proj(xp, mp[0], mp[1], g_mix, w_in, (inner, cdim, LANE), tm)
            y, nbuf, hf = _ssd_prompt(z, xbc, dtr, ssd_conv_w[j], ssd_conv_b[j], ssd_dt_bias[j], ssd_a_log[j],
                                      ssd_d[j], ssd_norm[j])
            mix_p = ("proj", y, mp[2], w_out)
            outs["ssdp"].append(hf)
            outs["ssdcp"].append(nbuf)
            z, xbc, dtr = _proj(xs, ms[0], ms[1], g_mix, w_in, (inner, cdim, LANE), rs)
            buf = state_ssd_conv[j].transpose(1, 0, 2).reshape((width - 1) * nb, cdim)
            xc, dt, nbuf = _ssd_prep_sample(_to_steps(xbc, nb, steps), buf, _to_steps(dtr, nb, steps),
                                            ssd_conv_w[j], ssd_conv_b[j], ssd_dt_bias[j], nb, steps)
            y, hf = _ssd_scan_sample(_pad_q(_to_batch(xc, nb, steps), nb, steps),
                                     _pad_q(_to_batch(dt, nb, steps), nb, steps), _pad_q(z, nb, steps),
                                     state_ssd[j].reshape(nb, nh * SSD_HEADDIM, SSD_STATE),
                                     ssd_a_log[j], ssd_d[j], ssd_norm[j], steps)
            mix_s = ("proj", y[:, :steps].reshape(1, rs, inner), ms[2], w_out)
            outs["ssds"].append(hf.reshape(nb, nh, SSD_HEADDIM, SSD_STATE))
            outs["ssdcs"].append(nbuf.reshape(width - 1, nb, cdim).transpose(1, 0, 2))
        w1 = mlp_w1[i].astype(BF16)
        w2 = mlp_w2[i].astype(BF16)
        fin = i == depth - 1
        xp = _mlp(xp, mp[3], mp[4], mp[5], norm_mlp[i], w1, w2, norm_final, fin, tm, mix_p)
        xs = _mlp(xs, ms[3], ms[4], ms[5], norm_mlp[i], w1, w2, norm_final, fin, rs, mix_s)
    st = jnp.stack
    return (xp, xs.reshape(nb, steps, d), st(outs["kp"]), st(outs["vp"]), st(outs["kip"]),
            st(outs["ks"]), st(outs["vs"]), st(outs["kis"]), st(outs["s5pr"]), st(outs["s5pi"]),
            st(outs["s5sr"]), st(outs["s5si"]), st(outs["scp"]), st(outs["scs"]),
            st(outs["ssdp"]), st(outs["ssdcp"]), st(outs["ssds"]), st(outs["ssdcs"]))
```

```python
import functools
import math

import jax
import jax.numpy as jnp
import numpy as np
from jax import lax
from jax.experimental import pallas as pl
from jax.experimental.pallas import tpu as pltpu

F32 = jnp.float32
BF16 = jnp.bfloat16
I32 = jnp.int32

EPS = 1e-6
N_HEADS = 16
HEAD_DIM = 64
N_KV_HEADS = 4
IDX_HEADS = 8
IDX_DIM = 64
TOPK_MAX = 256
N_BUCKETS = 32
MAX_DISTANCE = 128
S5_GROUP = 16
S5_STATE = 64
SSD_HEADDIM = 64
SSD_GROUPS = 4
SSD_STATE = 128
SSD_CHUNK = 128

LANE = 128
SUBLANE = 8
VMEM_LIMIT = 56 * 1024 * 1024
NEG_BIG = -1e30
INT_MIN = -2147483648


def _cparams(n_axes):
    return pltpu.CompilerParams(dimension_semantics=("arbitrary",) * n_axes,
                                vmem_limit_bytes=VMEM_LIMIT)


def _dot(a, b):
    return jnp.dot(a, b, preferred_element_type=F32)


def _dot_nt(a, b):
    return lax.dot_general(a, b, (((1,), (1,)), ((), ())), preferred_element_type=F32)


def _split(a):
    hi = a.astype(BF16)
    lo = (a - hi.astype(F32)).astype(BF16)
    return hi, lo


def _dot3(a, b):
    ah, al = _split(a)
    bh, bl = _split(b)
    return _dot(ah, bh) + (_dot(al, bh) + _dot(ah, bl))


def _rms_mod(x, g, shift, scale):
    y = x * lax.rsqrt(jnp.mean(x * x, axis=-1, keepdims=True) + EPS)
    return (y * g) * (1.0 + scale) + shift


def _sigmoid(x):
    return 1.0 / (1.0 + jnp.exp(-x))


def _silu(x):
    return x * _sigmoid(x)


def _row_block(arr, tm):
    n = arr.shape[-1]
    if arr.shape[1] == 1:
        return pl.BlockSpec((1, 1, n), lambda b, i: (b, 0, 0))
    return pl.BlockSpec((1, tm, n), lambda b, i: (b, i, 0))


def _const2(shape):
    return pl.BlockSpec(shape, lambda b, i: (0,) * len(shape))


def _ada_kernel(c_ref, w_ref, b_ref, o_ref):
    o_ref[0] = _dot3(_silu(c_ref[...]), w_ref[0]) + b_ref[0]


def _ada(c_all, ada_w, ada_b):
    depth, d, n = ada_w.shape
    rows = c_all.shape[0]
    tn = 1536
    return pl.pallas_call(
        _ada_kernel,
        grid=(depth, n // tn),
        in_specs=[pl.BlockSpec((rows, d), lambda l, j: (0, 0)),
                  pl.BlockSpec((1, d, tn), lambda l, j: (l, 0, j)),
                  pl.BlockSpec((1, 1, tn), lambda l, j: (l, 0, j))],
        out_specs=pl.BlockSpec((1, rows, tn), lambda l, j: (l, 0, j)),
        out_shape=jax.ShapeDtypeStruct((depth, rows, n), F32),
        compiler_params=_cparams(2),
        name="ada",
    )(c_all, ada_w, ada_b.reshape(depth, 1, n))


def _proj_kernel(x_ref, sh_ref, sc_ref, g_ref, w_ref, *o_refs, splits):
    h = _rms_mod(x_ref[0], g_ref[...], sh_ref[0], sc_ref[0]).astype(BF16)
    off = 0
    for o_ref, n in zip(o_refs, splits):
        o_ref[0] = _dot(h, w_ref[:, off:off + n])
        off += n


def _proj(x, shift, scale, g, w_bf16, splits, tm):
    bsz, seq, d = x.shape
    n = w_bf16.shape[1]
    assert sum(splits) == n and seq % tm == 0
    return pl.pallas_call(
        functools.partial(_proj_kernel, splits=tuple(splits)),
        grid=(bsz, seq // tm),
        in_specs=[pl.BlockSpec((1, tm, d), lambda b, i: (b, i, 0)),
                  _row_block(shift, tm), _row_block(scale, tm),
                  _const2((1, d)), _const2((d, n))],
        out_specs=[pl.BlockSpec((1, tm, s), lambda b, i: (b, i, 0)) for s in splits],
        out_shape=[jax.ShapeDtypeStruct((bsz, seq, s), F32) for s in splits],
        compiler_params=_cparams(2),
        name="proj",
    )(x, shift, scale, g.reshape(1, d), w_bf16)


def _aug_heads(t, col):
    rows = t.shape[0]
    lane = lax.broadcasted_iota(I32, (rows, LANE - HEAD_DIM), 1)
    extra = jnp.where(lane == 0, col, 0.0).astype(F32)
    parts = []
    for g in range(N_KV_HEADS):
        parts += [t[:, g * HEAD_DIM:(g + 1) * HEAD_DIM], extra]
    return jnp.concatenate(parts, axis=1).astype(BF16)


def _attn_proj_kernel(x_ref, sh_ref, sc_ref, g_ref, w_ref, wi_ref,
                      q_ref, k_ref, v_ref, kw_ref, qcat_ref, kcat_ref, kaug_ref, vaug_ref,
                      *, nq, nkv, nqi, key_major):
    h = _rms_mod(x_ref[0], g_ref[...], sh_ref[0], sc_ref[0])
    hb = h.astype(BF16)
    q_ref[0] = _dot(hb, w_ref[:, 0:nq]) * (HEAD_DIM ** -0.5)
    k = _dot(hb, w_ref[:, nq:nq + nkv])
    v = _dot(hb, w_ref[:, nq + nkv:nq + 2 * nkv])
    k_ref[0] = k
    v_ref[0] = v
    r = _dot3(h, wi_ref[...])
    kw = r[:, nqi:nqi + LANE]
    kw_ref[0] = kw
    zero = jnp.zeros((h.shape[0], IDX_DIM), BF16)
    parts = []
    for hh in range(IDX_HEADS):
        ah, al = _split(r[:, hh * IDX_DIM:(hh + 1) * IDX_DIM])
        parts += [ah, al, ah, zero]
    qcat_ref[0] = jnp.concatenate(parts, axis=1)
    kcat = _idx_rhs(kw[:, 0:IDX_DIM])
    kaug = _aug_heads(k, 0.0)
    vaug_ref[0] = _aug_heads(v, 1.0)
    if key_major:
        kcat_ref[0, 0] = kcat.astype(F32).T.astype(BF16)
        kaug_ref[0, 0] = kaug.astype(F32).T.astype(BF16)
    else:
        kcat_ref[0] = kcat
        kaug_ref[0] = kaug


def _attn_proj(x, shift, scale, g, w_qkv_bf16, w_idx, tm, key_major):
    bsz, seq, d = x.shape
    nq = N_HEADS * HEAD_DIM
    nkv = N_KV_HEADS * HEAD_DIM
    nqi = IDX_HEADS * IDX_DIM
    rows = lambda n, dt: (pl.BlockSpec((1, tm, n), lambda b, i: (b, i, 0)), jax.ShapeDtypeStruct((bsz, seq, n), dt))
    cols = lambda n, dt: (pl.BlockSpec((1, 1, n, tm), lambda b, i: (b, i, 0, 0)),
                          jax.ShapeDtypeStruct((bsz, seq // tm, n, tm), dt))
    keys = cols if key_major else rows
    outs = [rows(nq, F32), rows(nkv, F32), rows(nkv, F32), rows(LANE, F32), rows(4 * nqi, BF16),
            keys(4 * IDX_DIM, BF16), keys(N_KV_HEADS * LANE, BF16), rows(N_KV_HEADS * LANE, BF16)]
    return pl.pallas_call(
        functools.partial(_attn_proj_kernel, nq=nq, nkv=nkv, nqi=nqi, key_major=key_major),
        grid=(bsz, seq // tm),
        in_specs=[pl.BlockSpec((1, tm, d), lambda b, i: (b, i, 0)),
                  _row_block(shift, tm), _row_block(scale, tm),
                  _const2((1, d)), _const2(w_qkv_bf16.shape), _const2(w_idx.shape)],
        out_specs=[o[0] for o in outs],
        out_shape=[o[1] for o in outs],
        compiler_params=_cparams(2),
        name="attn_proj",
    )(x, shift, scale, g.reshape(1, d), w_qkv_bf16, w_idx)


def _mlp_kernel(*refs, ff_chunk, final_norm, mixer_out):
    if mixer_out is None:
        x_ref, sh_ref, sc_ref, gate_ref, g_ref, w1_ref, w2_ref, gf_ref, y_ref = refs
        x = x_ref[0]
    else:
        o_ref, gmix_ref, wo_ref, x_ref, sh_ref, sc_ref, gate_ref, g_ref, w1_ref, w2_ref, gf_ref, y_ref = refs
        d = x_ref.shape[-1]
        ob = o_ref[0].astype(BF16)
        if mixer_out == "glu":
            t = _dot(ob, wo_ref[:, :d]) * _sigmoid(_dot(ob, wo_ref[:, d:]))
        else:
            t = _dot(ob, wo_ref[...])
        x = x_ref[0] + gmix_ref[0] * t
    h = _rms_mod(x, g_ref[...], sh_ref[0], sc_ref[0]).astype(BF16)
    dff = w1_ref.shape[1]
    acc = jnp.zeros(x.shape, F32)
    for c in range(dff // ff_chunk):
        a = jnp.maximum(_dot(h, w1_ref[:, c * ff_chunk:(c + 1) * ff_chunk]), 0.0)
        acc = acc + _dot((a * a).astype(BF16), w2_ref[c * ff_chunk:(c + 1) * ff_chunk, :])
    y = x + gate_ref[0] * acc
    if final_norm:
        y = (y * lax.rsqrt(jnp.mean(y * y, axis=-1, keepdims=True) + EPS)) * gf_ref[...]
    y_ref[0] = y


def _resident(shape):
    return pl.BlockSpec(shape, lambda b, i: (0,) * len(shape), pipeline_mode=pl.Buffered(1))


def _mlp(x, shift, scale, gate, g, w1_bf16, w2_bf16, g_final, final_norm, tm, mixer=None):
    bsz, seq, d = x.shape
    dff = w1_bf16.shape[1]
    rows = lambda n: pl.BlockSpec((1, tm, n), lambda b, i: (b, i, 0))
    args, specs, kind = [], [], None
    if mixer is not None:
        kind, o, gmix, wo = mixer
        args += [o, gmix, wo]
        specs += [rows(o.shape[-1]), _row_block(gmix, tm), _resident(wo.shape)]
    args += [x, shift, scale, gate, g.reshape(1, d), w1_bf16, w2_bf16, g_final.reshape(1, d)]
    specs += [rows(d), _row_block(shift, tm), _row_block(scale, tm), _row_block(gate, tm),
              _const2((1, d)), _resident((d, dff)), _resident((dff, d)), _const2((1, d))]
    return pl.pallas_call(
        functools.partial(_mlp_kernel, ff_chunk=1024, final_norm=final_norm, mixer_out=kind),
        grid=(bsz, seq // tm),
        in_specs=specs,
        out_specs=rows(d),
        out_shape=jax.ShapeDtypeStruct((bsz, seq, d), F32),
        compiler_params=_cparams(2),
        name="mlp",
    )(*args)


S5_CH = 128
S5_BLK = (S5_CH // S5_GROUP) * S5_STATE


def _s5_disc_kernel(lr_ref, li_ref, ldt_ref, br_ref, bi_ref, ar_ref, ai_ref, bbr_ref, bbi_ref):
    lr, li = lr_ref[...], li_ref[...]
    dt = jnp.exp(ldt_ref[...])
    mag = jnp.exp(lr * dt)
    ab_re, ab_im = mag * jnp.cos(li * dt), mag * jnp.sin(li * dt)
    den = lr * lr + li * li
    nr = ab_re - 1.0
    f_re = (nr * lr + ab_im * li) / den
    f_im = (ab_im * lr - nr * li) / den
    ar_ref[...] = ab_re
    ai_ref[...] = ab_im
    for c in range(br_ref.shape[0]):
        br, bi = br_ref[c], bi_ref[c]
        bbr_ref[c] = f_re * br - f_im * bi
        bbi_ref[c] = f_re * bi + f_im * br


def _s5_weights(lam_re, lam_im, log_dt, b_re, b_im, c_re, c_im):
    g, p = lam_re.shape
    gc = b_re.shape[-1]
    brt = jnp.moveaxis(b_re, 2, 0)
    bit = jnp.moveaxis(b_im, 2, 0)
    ar, ai, bbr, bbi = pl.pallas_call(
        _s5_disc_kernel,
        out_shape=[jax.ShapeDtypeStruct((g, p), F32), jax.ShapeDtypeStruct((g, p), F32),
                   jax.ShapeDtypeStruct((gc, g, p), F32), jax.ShapeDtypeStruct((gc, g, p), F32)],
        name="s5_disc",
    )(lam_re, lam_im, log_dt.reshape(g, 1), brt, bit)
    nblk = (g * gc) // S5_CH
    gpb = g // nblk
    eye = jnp.eye(gpb, dtype=F32)

    def bd_in(bb):
        t = jnp.moveaxis(bb, 0, 1).reshape(nblk, gpb, gc, p)
        return jnp.einsum('ngcp,gh->ngchp', t, eye).reshape(nblk, gpb * gc, gpb * p)

    def bd_out(cc):
        t = cc.reshape(nblk, gpb, gc, p)
        return jnp.einsum('ngcp,gh->ngphc', t, eye).reshape(nblk, gpb * p, gpb * gc)

    wb = jnp.concatenate([bd_in(bbr), bd_in(bbi)], axis=-1).astype(BF16)
    wc = jnp.concatenate([bd_out(c_re), bd_out(c_im)], axis=1).astype(BF16)
    return ar.reshape(nblk, 1, gpb * p), ai.reshape(nblk, 1, gpb * p), wb, wc


def _gelu_tanh(y):
    return 0.5 * y * (1.0 + jnp.tanh(math.sqrt(2.0 / math.pi) * (y + 0.044715 * (y * y * y))))


def _cmul(ar, ai, xr, xi):
    return ar * xr - ai * xi, ar * xi + ai * xr


def _s5_prompt_kernel(x_ref, sh_ref, sc_ref, g_ref, d_ref, ar_ref, ai_ref, wb_ref, wc_ref,
                      perm_ref, unperm_ref, z_ref, fr_ref, fi_ref, u_s, br_s, bi_s, st_r, st_i,
                      *, nseg, seg):
    i = pl.program_id(1)
    nblk = wb_ref.shape[0]
    blk = ar_ref.shape[-1]

    @pl.when(i == 0)
    def _():
        st_r[...] = jnp.zeros(st_r.shape, F32)
        st_i[...] = jnp.zeros(st_i.shape, F32)

    u_s[...] = _dot_sel_lhs(perm_ref[...], _rms_mod(x_ref[0], g_ref[...], sh_ref[0], sc_ref[0]))

    for c in range(nblk):
        lo, hi = c * S5_CH, (c + 1) * S5_CH
        uc = u_s[:, lo:hi]
        bu = _dot(uc.astype(BF16), wb_ref[c])
        br_s[...] = bu[:, :blk]
        bi_s[...] = bu[:, blk:]
        ar = jnp.broadcast_to(ar_ref[c], (nseg, blk))
        ai = jnp.broadcast_to(ai_ref[c], (nseg, blk))

        def local(j, carry):
            xr, xi = carry
            r0 = pl.multiple_of(j * nseg, nseg)
            pr, pi = _cmul(ar, ai, xr, xi)
            nr = pr + br_s[pl.ds(r0, nseg), :]
            ni = pi + bi_s[pl.ds(r0, nseg), :]
            br_s[pl.ds(r0, nseg), :] = nr
            bi_s[pl.ds(r0, nseg), :] = ni
            return nr, ni

        zero = jnp.zeros((nseg, blk), F32)
        fr, fi = lax.fori_loop(0, seg, local, (zero, zero))

        pr, pi = ar_ref[c], ai_ref[c]
        for _ in range(int(math.log2(seg))):
            pr, pi = _cmul(pr, pi, pr, pi)
        cr, ci = st_r[c], st_i[c]
        rows_r, rows_i = [], []
        for s in range(nseg):
            rows_r.append(cr)
            rows_i.append(ci)
            tr, ti = _cmul(pr, pi, cr, ci)
            cr, ci = tr + fr[s:s + 1], ti + fi[s:s + 1]
        st_r[c] = cr
        st_i[c] = ci
        fr_ref[0, c] = cr
        fi_ref[0, c] = ci
        dr, di = _cmul(ar, ai, jnp.concatenate(rows_r, axis=0), jnp.concatenate(rows_i, axis=0))

        def fix(j, carry):
            dr, di = carry
            r0 = pl.multiple_of(j * nseg, nseg)
            br_s[pl.ds(r0, nseg), :] = br_s[pl.ds(r0, nseg), :] + dr
            bi_s[pl.ds(r0, nseg), :] = bi_s[pl.ds(r0, nseg), :] + di
            return _cmul(ar, ai, dr, di)

        lax.fori_loop(0, seg, fix, (dr, di))

        y = (_dot(br_s[...].astype(BF16), wc_ref[c, :blk, :])
             - _dot(bi_s[...].astype(BF16), wc_ref[c, blk:, :]))
        y = y + d_ref[:, lo:hi] * uc
        u_s[:, lo:hi] = _gelu_tanh(y)

    z_ref[0] = _dot_sel_lhs(unperm_ref[...], u_s[...])


def _s5_prompt(x, shift, scale, g, d_skip, ar, ai, wb, wc):
    bsz, seq, d = x.shape
    nblk, _, blk = ar.shape
    nseg, seg = SUBLANE, 32
    tm = nseg * seg
    assert seq % tm == 0
    perm = np.zeros((tm, tm), np.float32)
    for s in range(nseg):
        for j in range(seg):
            perm[j * nseg + s, s * seg + j] = 1.0
    unperm = jnp.asarray(perm.T, BF16)
    perm = jnp.asarray(perm, BF16)
    z, fr, fi = pl.pallas_call(
        functools.partial(_s5_prompt_kernel, nseg=nseg, seg=seg),
        grid=(bsz, seq // tm),
        in_specs=[pl.BlockSpec((1, tm, d), lambda b, i: (b, i, 0)),
                  _row_block(shift, tm), _row_block(scale, tm),
                  _const2((1, d)), _const2((1, d)),
                  _const2(ar.shape), _const2(ai.shape), _const2(wb.shape), _const2(wc.shape),
                  _const2((tm, tm)), _const2((tm, tm))],
        out_specs=[pl.BlockSpec((1, tm, d), lambda b, i: (b, i, 0)),
                   pl.BlockSpec((1, nblk, 1, blk), lambda b, i: (b, 0, 0, 0)),
                   pl.BlockSpec((1, nblk, 1, blk), lambda b, i: (b, 0, 0, 0))],
        out_shape=[jax.ShapeDtypeStruct((bsz, seq, d), F32),
                   jax.ShapeDtypeStruct((bsz, nblk, 1, blk), F32),
                   jax.ShapeDtypeStruct((bsz, nblk, 1, blk), F32)],
        scratch_shapes=[pltpu.VMEM((tm, d), F32), pltpu.VMEM((tm, blk), F32), pltpu.VMEM((tm, blk), F32),
                        pltpu.VMEM((nblk, 1, blk), F32), pltpu.VMEM((nblk, 1, blk), F32)],
        compiler_params=_cparams(2),
        name="s5_prompt",
    )(x, shift, scale, g.reshape(1, d), d_skip.reshape(1, d), ar, ai, wb, wc, perm, unperm)
    return z, fr, fi


def _s5_sample_kernel(x_ref, sh_ref, sc_ref, g_ref, d_ref, ar_ref, ai_ref, wb_ref, wc_ref,
                      s0r_ref, s0i_ref, z_ref, fr_ref, fi_ref, *, nb, steps):
    nblk = wb_ref.shape[0]
    blk = ar_ref.shape[-1]
    u = _rms_mod(x_ref[...], g_ref[...], sh_ref[...], sc_ref[...])
    for c in range(nblk):
        lo, hi = c * S5_CH, (c + 1) * S5_CH
        uc = u[:, lo:hi]
        bu = _dot(uc.astype(BF16), wb_ref[c])
        ar = jnp.broadcast_to(ar_ref[c], (nb, blk))
        ai = jnp.broadcast_to(ai_ref[c], (nb, blk))
        xr, xi = s0r_ref[:, c * blk:(c + 1) * blk], s0i_ref[:, c * blk:(c + 1) * blk]
        xrs, xis = [], []
        for t in range(steps):
            pr, pi = _cmul(ar, ai, xr, xi)
            xr = pr + bu[t * nb:(t + 1) * nb, :blk]
            xi = pi + bu[t * nb:(t + 1) * nb, blk:]
            xrs.append(xr)
            xis.append(xi)
        fr_ref[:, c * blk:(c + 1) * blk] = xr
        fi_ref[:, c * blk:(c + 1) * blk] = xi
        y = (_dot(jnp.concatenate(xrs, axis=0).astype(BF16), wc_ref[c, :blk, :])
             - _dot(jnp.concatenate(xis, axis=0).astype(BF16), wc_ref[c, blk:, :]))
        y = y + d_ref[:, lo:hi] * uc
        z_ref[:, lo:hi] = _gelu_tanh(y)


def _s5_sample(x_tb, shift_tb, scale_tb, g, d_skip, ar, ai, wb, wc, s0r, s0i, nb, steps):
    rows, d = x_tb.shape
    nblk, _, blk = ar.shape
    return pl.pallas_call(
        functools.partial(_s5_sample_kernel, nb=nb, steps=steps),
        out_shape=[jax.ShapeDtypeStruct((rows, d), F32),
                   jax.ShapeDtypeStruct((nb, nblk * blk), F32),
                   jax.ShapeDtypeStruct((nb, nblk * blk), F32)],
        compiler_params=pltpu.CompilerParams(vmem_limit_bytes=VMEM_LIMIT),
        name="s5_sample",
    )(x_tb, shift_tb, scale_tb, g.reshape(1, d), d_skip.reshape(1, d), ar, ai, wb, wc, s0r, s0i)


def _shift_rows(u, prev, k):
    if k == 0:
        return u
    rolled = pltpu.roll(u, k, 0)
    head = pltpu.roll(prev, k, 0)
    row = lax.broadcasted_iota(I32, (SUBLANE, u.shape[1]), 0)
    first = jnp.where(row < k, head, rolled[:SUBLANE])
    return jnp.concatenate([first, rolled[SUBLANE:]], axis=0)


def _sconv_prompt_kernel(p_ref, x_ref, gate_ref, buf_ref, wc_ref, wo_ref, y_ref, nb_ref, prev_s,
                         *, width):
    i = pl.program_id(1)
    d = x_ref.shape[-1]

    @pl.when(i == 0)
    def _():
        prev_s[...] = jnp.zeros(prev_s.shape, F32)
        prev_s[SUBLANE - (width - 1):, :] = buf_ref[0]

    p = p_ref[0]
    gb, gc, xh = p[:, :d], p[:, d:2 * d], p[:, 2 * d:]
    u = gc * xh
    prev = prev_s[...]
    conv = u * wc_ref[width - 1:width, :]
    for k in range(1, width):
        conv = conv + _shift_rows(u, prev, k) * wc_ref[width - 1 - k:width - k, :]
    prev_s[...] = u[u.shape[0] - SUBLANE:, :]
    nb_ref[0] = u[u.shape[0] - (width - 1):, :]
    y_ref[0] = x_ref[0] + gate_ref[0] * _dot((gb * conv).astype(BF16), wo_ref[...])


def _sconv_prompt(p, x, gate, buf, w_conv, w_out_bf16, tm):
    bsz, seq, d = x.shape
    width = w_conv.shape[0]
    return pl.pallas_call(
        functools.partial(_sconv_prompt_kernel, width=width),
        grid=(bsz, seq // tm),
        in_specs=[pl.BlockSpec((1, tm, 3 * d), lambda b, i: (b, i, 0)),
                  pl.BlockSpec((1, tm, d), lambda b, i: (b, i, 0)),
                  _row_block(gate, tm),
                  pl.BlockSpec((1, width - 1, d), lambda b, i: (b, 0, 0)),
                  _const2((width, d)), _const2((d, d))],
        out_specs=[pl.BlockSpec((1, tm, d), lambda b, i: (b, i, 0)),
                   pl.BlockSpec((1, width - 1, d), lambda b, i: (b, 0, 0))],
        out_shape=[jax.ShapeDtypeStruct((bsz, seq, d), F32),
                   jax.ShapeDtypeStruct((bsz, width - 1, d), F32)],
        scratch_shapes=[pltpu.VMEM((SUBLANE, d), F32)],
        compiler_params=_cparams(2),
        name="sconv_prompt",
    )(p, x, gate, buf, w_conv, w_out_bf16)


def _sconv_sample_kernel(p_ref, x_ref, gate_ref, buf_ref, wc_ref, wo_ref, y_ref, nb_ref,
                         *, width, nb, steps):
    d = x_ref.shape[-1]
    p = p_ref[...]
    gb, gc, xh = p[:, :d], p[:, d:2 * d], p[:, 2 * d:]
    u = gc * xh
    ext = jnp.concatenate([buf_ref[...], u], axis=0)
    conv = ext[0:steps * nb] * wc_ref[0:1, :]
    for k in range(1, width):
        conv = conv + ext[k * nb:(k + steps) * nb] * wc_ref[k:k + 1, :]
    nb_ref[...] = ext[steps * nb:]
    y_ref[...] = x_ref[...] + gate_ref[...] * _dot((gb * conv).astype(BF16), wo_ref[...])


def _sconv_sample(p_tb, x_tb, gate_tb, buf_kb, w_conv, w_out_bf16, nb, steps):
    rows, d = x_tb.shape
    width = w_conv.shape[0]
    return pl.pallas_call(
        functools.partial(_sconv_sample_kernel, width=width, nb=nb, steps=steps),
        out_shape=[jax.ShapeDtypeStruct((rows, d), F32),
                   jax.ShapeDtypeStruct(((width - 1) * nb, d), F32)],
        compiler_params=pltpu.CompilerParams(vmem_limit_bytes=VMEM_LIMIT),
        name="sconv_sample",
    )(p_tb, x_tb, gate_tb, buf_kb, w_conv, w_out_bf16)


def _sortable(s):
    bits = lax.bitcast_convert_type(s + 0.0, I32)
    return bits ^ ((bits >> 31) & 0x7FFFFFFF)


def _stack_heads(qcat):
    w = 4 * IDX_DIM
    return jnp.concatenate([qcat[:, h * w:(h + 1) * w] for h in range(IDX_HEADS)], axis=0)


def _idx_rhs(ki):
    kh, kl = _split(ki)
    return jnp.concatenate([kh, kh, kl, jnp.zeros(kh.shape, BF16)], axis=1)


def _idx_score(dts, wcols, nrows):
    s = wcols[0] * jnp.maximum(dts[0:nrows], 0.0)
    for h in range(1, IDX_HEADS):
        s = s + wcols[h] * jnp.maximum(dts[h * nrows:(h + 1) * nrows], 0.0)
    return s


def _idx_wcols(kw):
    sc = (IDX_HEADS ** -0.5) * (IDX_DIM ** -0.5)
    return [kw[:, IDX_DIM + h:IDX_DIM + h + 1] * sc for h in range(IDX_HEADS)]


def _topk_select(count, rows, lanes, topk, idx_bits, stash=lambda v: (lambda: v)):
    wide = lambda v: jnp.broadcast_to(v, (rows, lanes))

    def bit_step(it, tu):
        mask = jnp.left_shift(jnp.int32(1), 31 - it)
        cand_u = tu | mask
        cand = stash(wide(cand_u ^ INT_MIN))
        cnt = count(lambda c, key, idx: jnp.where(key >= cand(), 1.0, 0.0))
        return jnp.where(cnt >= topk, cand_u, tu)

    tu = lax.fori_loop(0, 32, bit_step, jnp.zeros((rows, 1), I32))
    t = tu ^ INT_MIN
    tw = wide(t)
    n_gt = count(lambda c, key, idx: jnp.where(key > tw, 1.0, 0.0))
    n_ge = count(lambda c, key, idx: jnp.where(key >= tw, 1.0, 0.0))
    need = topk - n_gt
    tied = jnp.max(jnp.where(t != INT_MIN, n_ge - topk, 0.0)) > 0.0

    def tie_search(_):
        def idx_step(it, j):
            cand = j | jnp.left_shift(jnp.int32(1), idx_bits - 1 - it)
            cw = wide(cand)
            cnt = count(lambda c, key, idx: jnp.where((key == tw) & (idx < cw), 1.0, 0.0))
            return jnp.where(cnt < need, cand, j)
        return lax.fori_loop(0, idx_bits, idx_step, jnp.zeros((rows, 1), I32))

    j = lax.cond(tied, tie_search, lambda _: jnp.full((rows, 1), 2 ** idx_bits, I32), 0)
    return t, j


def _dsa_prompt_kernel(q_ref, qcat_ref, kwq_ref, kcat_ref, kaug_ref, vaug_ref, bias_ref, o_ref,
                       key_s, madd_s, wbc_s, cand_s, qa_s, lg_s, mx_s, sh_s, acc_s, *, tq, topk, idx_bits):
    i = pl.program_id(1)
    nk = i + 1
    nk2 = (nk + 1) // 2
    r = N_HEADS // N_KV_HEADS

    for h, w in enumerate(_idx_wcols(kwq_ref[0])):
        wbc_s[h] = jnp.broadcast_to(w, (tq, tq))
    qrow = lax.broadcasted_iota(I32, (tq, tq), 0)
    kcol = lax.broadcasted_iota(I32, (tq, tq), 1)
    qpos = i * tq + qrow
    wq = 4 * IDX_DIM

    def score_pair(c2, _):
        kc = kcat_ref[0, c2]
        s = None
        for h in range(IDX_HEADS):
            w = wbc_s[h]
            d = jnp.concatenate([w, w], axis=1) * jnp.maximum(_dot(qcat_ref[0, :, h * wq:(h + 1) * wq], kc), 0.0)
            s = d if s is None else s + d
        for u in range(2):
            c = 2 * c2 + u
            key_s[c] = jnp.where(c * tq + kcol <= qpos, _sortable(s[:, u * tq:(u + 1) * tq]), INT_MIN)
        return 0

    lax.fori_loop(0, nk2, score_pair, 0)

    def count(fn):
        def body(c2, acc):
            c = 2 * c2
            acc = acc + fn(c, key_s[c], c * tq + kcol)
            return acc + fn(c + 1, key_s[c + 1], (c + 1) * tq + kcol)
        acc = lax.fori_loop(0, nk2, body, jnp.zeros((tq, tq), F32))
        return jnp.sum(acc, axis=1, keepdims=True)

    def stash(v):
        cand_s[...] = v
        return lambda: cand_s[...]

    t, j = _topk_select(count, tq, tq, float(topk), idx_bits, stash)
    tw = jnp.broadcast_to(t, (tq, tq))
    jw = jnp.broadcast_to(j, (tq, tq))

    def mask_pair(c2, _):
        for u in range(2):
            c = 2 * c2 + u
            key = key_s[c]
            sel = (key > tw) | ((key == tw) & (c * tq + kcol <= jw))
            madd_s[c] = jnp.where(sel & (key != INT_MIN), 0.0, NEG_BIG)
        return 0

    lax.fori_loop(0, nk2, mask_pair, 0)

    q = q_ref[0]
    zpad = jnp.zeros((tq, LANE - HEAD_DIM), F32)
    for g in range(N_KV_HEADS):
        qa_s[g] = jnp.concatenate(
            [jnp.concatenate([q[:, (g * r + u) * HEAD_DIM:(g * r + u + 1) * HEAD_DIM], zpad], axis=1)
             for u in range(r)], axis=0).astype(BF16)

    def pair_mask(c2, near, g):
        halves = []
        for u in range(2):
            c = 2 * c2 + u
            m = madd_s[c]
            if near:
                back = nk - 1 - c
                carries = (back == 0) | ((back == 1) & (kcol > qrow))
                halves.append([m + jnp.where(carries, bias_ref[g * r + v], 0.0) for v in range(r)])
            else:
                halves.append([m] * r)
        return jnp.concatenate([jnp.concatenate([halves[0][v], halves[1][v]], axis=1) for v in range(r)], axis=0)

    ngh = lg_s.shape[0]
    for half in range(N_KV_HEADS // ngh):
        groups = [half * ngh + k for k in range(ngh)]
        mx_s[...] = jnp.full(mx_s.shape, NEG_BIG, F32)

        def logits_pair(c2, near):
            for k, g in enumerate(groups):
                lg = _dot(qa_s[g], kaug_ref[0, c2, g * LANE:(g + 1) * LANE, :]) + pair_mask(c2, near, g)
                lg_s[k, c2] = lg
                mx_s[k] = jnp.maximum(mx_s[k], jnp.maximum(lg[:, :tq], lg[:, tq:]))

        def far(c4, _):
            logits_pair(2 * c4, False)
            logits_pair(2 * c4 + 1, False)
            return 0

        nfar = jnp.maximum(nk2 - 2, 0)
        lax.fori_loop(0, nfar // 2, far, 0)

        @pl.when(nfar % 2 == 1)
        def _():
            logits_pair(nfar - 1, False)

        @pl.when(nk2 >= 2)
        def _():
            logits_pair(nk2 - 2, True)

        logits_pair(nk2 - 1, True)

        for k in range(ngh):
            sh_s[k] = jnp.broadcast_to(jnp.max(mx_s[k], axis=1, keepdims=True), (r * tq, LANE))
        acc_s[...] = jnp.zeros(acc_s.shape, F32)

        def pv_pair(c2):
            r0 = pl.multiple_of(c2 * 2 * tq, 2 * tq)
            va = vaug_ref[0, pl.ds(r0, 2 * tq), :]
            for k, g in enumerate(groups):
                s = sh_s[k]
                p = jnp.exp(lg_s[k, c2] - jnp.concatenate([s, s], axis=1))
                acc_s[k] = acc_s[k] + _dot(p.astype(BF16), va[:, g * LANE:(g + 1) * LANE])

        def pv_two(c4, _):
            pv_pair(2 * c4)
            pv_pair(2 * c4 + 1)
            return 0

        lax.fori_loop(0, nk2 // 2, pv_two, 0)

        @pl.when(nk2 % 2 == 1)
        def _():
            pv_pair(nk2 - 1)
        for k, g in enumerate(groups):
            acc = acc_s[k]
            o = acc[:, 0:HEAD_DIM] / acc[:, HEAD_DIM:HEAD_DIM + 1]
            for u in range(r):
                h = g * r + u
                o_ref[0, :, h * HEAD_DIM:(h + 1) * HEAD_DIM] = o[u * tq:(u + 1) * tq, :]


def _t5_bucket(dist):
    n = jnp.maximum(dist, 0)
    max_exact = N_BUCKETS // 2
    nf = jnp.maximum(n, max_exact).astype(F32)
    large = max_exact + (jnp.log(nf / max_exact) / math.log(MAX_DISTANCE / max_exact)
                         * (N_BUCKETS - max_exact)).astype(I32)
    large = jnp.minimum(large, N_BUCKETS - 1)
    return jnp.where(n < max_exact, n, large)


def _bias_table(rel_bias, n):
    tab = rel_bias[_t5_bucket(jnp.arange(n, dtype=I32))]
    return (tab - rel_bias[N_BUCKETS - 1][None, :]).T


DSA_GROUPS_PER_PASS = 2


def _dsa_prompt(q, qcat, kw, kcat, kaug, vaug, rel_bias, tq):
    bsz, seq, nq = q.shape
    topk = min(TOPK_MAX, seq // 4)
    nkc = seq // tq
    r = N_HEADS // N_KV_HEADS
    assert tq == MAX_DISTANCE and seq % (2 * tq) == 0
    tab = _bias_table(rel_bias, tq)
    dmod = (np.arange(tq)[:, None] - np.arange(tq)[None, :]) % tq
    onehot = (jnp.arange(tq, dtype=I32)[:, None, None] == jnp.asarray(dmod, I32)[None]).astype(F32)
    bias = jnp.einsum('hd,dqk->hqk', tab, onehot, precision=lax.Precision.HIGHEST)
    assert kcat.shape[1:] == (seq // (2 * tq), 4 * IDX_DIM, 2 * tq)
    row = lambda n: pl.BlockSpec((1, tq, n), lambda b, i: (b, i, 0))
    tiles = lambda a: pl.BlockSpec((1,) + a.shape[1:], lambda b, i: (b, 0, 0, 0), pipeline_mode=pl.Buffered(1))
    return pl.pallas_call(
        functools.partial(_dsa_prompt_kernel, tq=tq, topk=topk, idx_bits=int(math.log2(seq))),
        grid=(bsz, nkc),
        in_specs=[row(nq), row(qcat.shape[-1]), row(LANE), tiles(kcat), tiles(kaug),
                  pl.BlockSpec((1, seq, vaug.shape[-1]), lambda b, i: (b, 0, 0), pipeline_mode=pl.Buffered(1)),
                  _resident((N_HEADS, tq, tq))],
        out_specs=row(nq),
        out_shape=jax.ShapeDtypeStruct((bsz, seq, nq), F32),
        scratch_shapes=[pltpu.VMEM((nkc, tq, tq), I32), pltpu.VMEM((nkc, tq, tq), F32),
                        pltpu.VMEM((IDX_HEADS, tq, tq), F32), pltpu.VMEM((tq, tq), I32),
                        pltpu.VMEM((N_KV_HEADS, r * tq, LANE), BF16),
                        pltpu.VMEM((DSA_GROUPS_PER_PASS, nkc // 2, r * tq, 2 * tq), F32),
                        pltpu.VMEM((DSA_GROUPS_PER_PASS, r * tq, tq), F32),
                        pltpu.VMEM((DSA_GROUPS_PER_PASS, r * tq, LANE), F32),
                        pltpu.VMEM((DSA_GROUPS_PER_PASS, r * tq, LANE), F32)],
        compiler_params=_cparams(2),
        name="dsa_prompt",
    )(q, qcat, kw, kcat, kaug, vaug, bias)


QP = SUBLANE
PAGES_PER_STEP = 16


def _page_specs(shape4, layer, pps):
    def spec(u):
        return pl.BlockSpec((1, 1) + tuple(shape4[2:]),
                            lambda b, g, pt: (layer, pt[b, g * pps + u], 0, 0))
    return [spec(u) for u in range(pps)]


def _pages_t(cache):
    nd = cache.ndim
    t = jnp.transpose(cache, (0, 1) + tuple(range(3, nd)) + (2,))
    return t.reshape(t.shape[0], t.shape[1], -1, t.shape[-1])


def _pad_t(a, page):
    return jnp.concatenate([a, jnp.zeros((page - QP, a.shape[1]), F32)], axis=0).T


def _dsa_sample_score_kernel(pt_ref, qcat_ref, kw_ref, kcn_ref, *rest, pps):
    page_refs, o_ref, on_ref = rest[:pps], rest[pps], rest[pps + 1]
    lhs = _stack_heads(qcat_ref[0]).astype(BF16)
    wcols = _idx_wcols(kw_ref[0])
    kt = jnp.concatenate([page_refs[u][0, 0] for u in range(pps)], axis=1)
    kh, kl = _split(kt)
    rhs = jnp.concatenate([kh, kh, kl, jnp.zeros(kh.shape, BF16)], axis=0)
    o_ref[0] = _idx_score(_dot(lhs, rhs), wcols, QP)

    @pl.when(pl.program_id(1) == pl.num_programs(1) - 1)
    def _():
        on_ref[0] = _idx_score(_dot(lhs, _pad_t(kcn_ref[0], on_ref.shape[-1]).astype(BF16)), wcols, QP)


def _dsa_sample_scores(page_table, qcat, kw, kcn, kidx_t, layer, pps):
    nb, n_pages = page_table.shape
    page = kidx_t.shape[-1]
    bmap = lambda b, g, pt: (b, 0, 0)
    grid_spec = pltpu.PrefetchScalarGridSpec(
        num_scalar_prefetch=1,
        grid=(nb, n_pages // pps),
        in_specs=[pl.BlockSpec((1, QP, qcat.shape[-1]), bmap), pl.BlockSpec((1, QP, LANE), bmap),
                  pl.BlockSpec((1, QP, kcn.shape[-1]), bmap)] + _page_specs(kidx_t.shape, layer, pps),
        out_specs=[pl.BlockSpec((1, QP, pps * page), lambda b, g, pt: (b, 0, g)),
                   pl.BlockSpec((1, QP, page), bmap)])
    return pl.pallas_call(
        functools.partial(_dsa_sample_score_kernel, pps=pps),
        grid_spec=grid_spec,
        out_shape=[jax.ShapeDtypeStruct((nb, QP, n_pages * page), F32),
                   jax.ShapeDtypeStruct((nb, QP, page), F32)],
        compiler_params=_cparams(2),
        name="dsa_sample_scores",
    )(page_table, qcat, kw, kcn, *([kidx_t] * pps))


def _new_token_keys(s_new, steps):
    lane = lax.broadcasted_iota(I32, s_new.shape, 1)
    q = lax.broadcasted_iota(I32, s_new.shape, 0) % QP
    return jnp.where((lane <= q) & (lane < steps), _sortable(s_new), INT_MIN)


def _dsa_sample_select_kernel(sc_ref, sn_ref, t_ref, j_ref, *, steps, topk, idx_bits):
    keys = jnp.concatenate([_sortable(sc_ref[...]), _new_token_keys(sn_ref[...], steps)], axis=1)
    idx = lax.broadcasted_iota(I32, keys.shape, 1)
    count = lambda fn: jnp.sum(fn(0, keys, idx), axis=1, keepdims=True)
    t, j = _topk_select(count, keys.shape[0], keys.shape[1], float(topk), idx_bits)
    t_ref[...] = t
    j_ref[...] = j


def _dsa_sample_select(scores, snew, steps, topk):
    nb, _, past = scores.shape
    page = snew.shape[-1]
    rows = nb * QP
    rb = math.gcd(rows, 64)
    t, j = pl.pallas_call(
        functools.partial(_dsa_sample_select_kernel, steps=steps, topk=topk,
                          idx_bits=int(math.ceil(math.log2(past + page)))),
        grid=(rows // rb,),
        in_specs=[pl.BlockSpec((rb, past), lambda i: (i, 0)), pl.BlockSpec((rb, page), lambda i: (i, 0))],
        out_specs=[pl.BlockSpec((rb, 1), lambda i: (i, 0)), pl.BlockSpec((rb, 1), lambda i: (i, 0))],
        out_shape=[jax.ShapeDtypeStruct((rows, 1), I32), jax.ShapeDtypeStruct((rows, 1), I32)],
        compiler_params=_cparams(1),
        name="dsa_sample_select",
    )(scores.reshape(rows, past), snew.reshape(rows, page))
    return t.reshape(nb, QP, 1), j.reshape(nb, QP, 1)


def _dsa_sample_attend_kernel(pt_ref, q_ref, kn_ref, vn_ref, sc_ref, sn_ref, t_ref, j_ref,
                              blast_ref, bnew_ref, *rest, pps, n_pages, steps):
    k_refs, v_refs, o_ref = rest[:pps], rest[pps:2 * pps], rest[2 * pps]
    lhs_s, m_s, l_s, acc_s = rest[2 * pps + 1:]
    g = pl.program_id(1)
    last = pl.num_programs(1) - 1
    page = k_refs[0].shape[-1]
    past = n_pages * page
    n = pps * page
    r = N_HEADS // N_KV_HEADS
    grow = r * QP

    @pl.when(g == 0)
    def _():
        m_s[...] = jnp.full(m_s.shape, NEG_BIG, F32)
        l_s[...] = jnp.zeros(l_s.shape, F32)
        acc_s[...] = jnp.zeros(acc_s.shape, F32)
        q = q_ref[0]
        for gg in range(N_KV_HEADS):
            lhs_s[gg] = jnp.concatenate(
                [q[:, (gg * r + u) * HEAD_DIM:(gg * r + u + 1) * HEAD_DIM] for u in range(r)], axis=0).astype(BF16)

    t, j = t_ref[0], j_ref[0]

    def select(key, idx):
        sel = (key > t) | ((key == t) & (idx <= j))
        return jnp.where(sel & (key != INT_MIN), 0.0, NEG_BIG)

    def attend(kt, vt, madd, bias):
        mrows = jnp.concatenate([madd] * r, axis=0)
        for gg in range(N_KV_HEADS):
            rows = slice(gg * grow, (gg + 1) * grow)
            feat = slice(gg * HEAD_DIM, (gg + 1) * HEAD_DIM)
            lg = _dot(lhs_s[gg], kt[feat, :]) + mrows
            if bias is not None:
                lg = lg + bias[rows, :]
            m_old = m_s[rows, :]
            m_new = jnp.maximum(m_old, jnp.max(lg, axis=1, keepdims=True))
            p = jnp.exp(lg - m_new)
            alpha = jnp.exp(m_old - m_new)
            l_s[rows, :] = alpha * l_s[rows, :] + jnp.sum(p, axis=1, keepdims=True)
            acc_s[rows, :] = alpha * acc_s[rows, :] + _dot_nt(p.astype(BF16), vt[feat, :])
            m_s[rows, :] = m_new

    def pages(bias):
        kt = jnp.concatenate([k_refs[u][0, 0] for u in range(pps)], axis=1).astype(BF16)
        vt = jnp.concatenate([v_refs[u][0, 0] for u in range(pps)], axis=1).astype(BF16)
        idx = g * n + lax.broadcasted_iota(I32, (QP, n), 1)
        attend(kt, vt, select(_sortable(sc_ref[0]), idx), bias)

    @pl.when(g != last)
    def _():
        pages(None)

    @pl.when(g == last)
    def _():
        pages(jnp.concatenate([jnp.zeros((N_HEADS * QP, n - page), F32), blast_ref[...]], axis=1))
        idx = past + lax.broadcasted_iota(I32, (QP, page), 1)
        attend(_pad_t(kn_ref[0], page).astype(BF16), _pad_t(vn_ref[0], page).astype(BF16),
               select(_new_token_keys(sn_ref[0], steps), idx), bnew_ref[...])
        o = acc_s[...] / l_s[...]
        for h in range(N_HEADS):
            o_ref[0, :, h * HEAD_DIM:(h + 1) * HEAD_DIM] = o[h * QP:(h + 1) * QP, :]


def _dsa_sample(page_table, q, kn, vn, scores, snew, k_t, v_t, rel_bias, layer, steps, pps):
    nb, n_pages = page_table.shape
    page = k_t.shape[-1]
    past = n_pages * page
    nq, nkv = N_HEADS * HEAD_DIM, N_KV_HEADS * HEAD_DIM
    t, j = _dsa_sample_select(scores, snew, steps, min(TOPK_MAX, (past + steps) // 4))
    n_steps = n_pages // pps
    r = N_HEADS // N_KV_HEADS
    tab = _bias_table(rel_bias, 2 * page)
    qq = np.arange(QP)[:, None]
    off = np.arange(page)[None, :]
    blast = tab[:, page + qq - off].reshape(N_HEADS * QP, page)
    bnew = tab[:, np.maximum(qq - off, 0)].reshape(N_HEADS * QP, page)
    bmap = lambda b, g, pt: (b, 0, 0)
    grid_spec = pltpu.PrefetchScalarGridSpec(
        num_scalar_prefetch=1,
        grid=(nb, n_steps),
        in_specs=[pl.BlockSpec((1, QP, nq), bmap),
                  pl.BlockSpec((1, QP, nkv), bmap),
                  pl.BlockSpec((1, QP, nkv), bmap),
                  pl.BlockSpec((1, QP, pps * page), lambda b, g, pt: (b, 0, g)),
                  pl.BlockSpec((1, QP, page), bmap),
                  pl.BlockSpec((1, QP, 1), bmap), pl.BlockSpec((1, QP, 1), bmap),
                  pl.BlockSpec(blast.shape, lambda b, g, pt: (0, 0)),
                  pl.BlockSpec(bnew.shape, lambda b, g, pt: (0, 0))]
                 + _page_specs(k_t.shape, layer, pps) + _page_specs(v_t.shape, layer, pps),
        out_specs=pl.BlockSpec((1, QP, nq), bmap),
        scratch_shapes=[pltpu.VMEM((N_KV_HEADS, r * QP, HEAD_DIM), BF16),
                        pltpu.VMEM((N_HEADS * QP, 1), F32), pltpu.VMEM((N_HEADS * QP, 1), F32),
                        pltpu.VMEM((N_HEADS * QP, HEAD_DIM), F32)])
    return pl.pallas_call(
        functools.partial(_dsa_sample_attend_kernel, pps=pps, n_pages=n_pages, steps=steps),
        grid_spec=grid_spec,
        out_shape=jax.ShapeDtypeStruct((nb, QP, nq), F32),
        compiler_params=_cparams(2),
        name="dsa_sample_attend",
    )(page_table, q, kn, vn, scores, snew, t, j, blast, bnew, *([k_t] * pps), *([v_t] * pps))


SSD_COLS = 512


def _softplus(x):
    return jnp.maximum(x, 0.0) + jnp.log1p(jnp.exp(-jnp.abs(x)))


def _split3(a):
    p1 = a.astype(BF16)
    r1 = a - p1.astype(F32)
    p2 = r1.astype(BF16)
    return p1, p2, (r1 - p2.astype(F32)).astype(BF16)


def _dot_sel_rhs(a, e):
    p1, p2, p3 = _split3(a)
    return _dot(p1, e) + (_dot(p2, e) + _dot(p3, e))


def _dot_sel_lhs(e, a):
    p1, p2, p3 = _split3(a)
    return _dot(e, p1) + (_dot(e, p2) + _dot(e, p3))


def _ssd_prompt_kernel(z_ref, xbc_ref, dtr_ref, cw_ref, cb_ref, dtb_ref, alog_ref, dexp_ref, ng_ref,
                       e_ref, y_ref, nbuf_ref, hf_ref, prev_s, xc_s, y_s, h_s, *, width, nh, inner):
    i = pl.program_id(1)
    last = pl.num_programs(1) - 1
    qn, cdim = xbc_ref.shape[1], xbc_ref.shape[2]
    ngrp, ns, hd = SSD_GROUPS, SSD_STATE, SSD_HEADDIM
    hpg = nh // ngrp

    @pl.when(i == 0)
    def _():
        prev_s[...] = jnp.zeros(prev_s.shape, F32)
        h_s[...] = jnp.zeros(h_s.shape, F32)

    for cb in range(cdim // SSD_COLS):
        cols = slice(cb * SSD_COLS, (cb + 1) * SSD_COLS)
        xr = xbc_ref[0, :, cols]
        prev = prev_s[:, cols]
        conv = xr * cw_ref[width - 1:width, cols]
        for k in range(1, width):
            conv = conv + _shift_rows(xr, prev, k) * cw_ref[width - 1 - k:width - k, cols]
        xc_s[:, cols] = _silu(conv + cb_ref[:, cols])
        prev_s[:, cols] = xr[qn - SUBLANE:, :]
        nbuf_ref[0, :, cols] = xr[qn - (width - 1):, :]

    hl = lax.broadcasted_iota(I32, (qn, LANE), 1)
    dt = jnp.where(hl < nh, _softplus(dtr_ref[0] + dtb_ref[...]), 0.0)
    dta = dt * (-jnp.exp(alog_ref[...]))
    qrow = lax.broadcasted_iota(I32, (qn, qn), 0)
    kcol = lax.broadcasted_iota(I32, (qn, qn), 1)
    causal = kcol <= qrow
    cum = _dot_sel_lhs(jnp.where(causal, 1.0, 0.0).astype(BF16), dta)
    cum_t = cum.T
    cum_last = cum[qn - 1:qn, :]
    ecum = jnp.exp(cum)
    dtw = dt * jnp.exp(cum_last - cum)
    e_last = jnp.exp(cum_last)
    lane_p = lax.broadcasted_iota(I32, (qn, LANE), 1)
    row_p = lax.broadcasted_iota(I32, (LANE, ns), 0)

    for g in range(ngrp):
        gl = slice(g * SSD_COLS, (g + 1) * SSD_COLS)
        e_g = e_ref[:, gl]
        xs_g = xc_s[:, gl]
        dtx_g = xs_g * _dot_sel_rhs(dt, e_g)
        dtxw_g = xs_g * _dot_sel_rhs(dtw, e_g)
        ecx_g = _dot_sel_rhs(ecum, e_g)
        bm = xc_s[:, inner + g * ns:inner + (g + 1) * ns].astype(BF16)
        cm = xc_s[:, inner + (ngrp + g) * ns:inner + (ngrp + g + 1) * ns].astype(BF16)
        cbm = _dot_nt(cm, bm)
        for pi in range(hpg // 2):
            ha = g * hpg + 2 * pi
            pls = slice(pi * LANE, (pi + 1) * LANE)
            lanes = slice(ha * hd, ha * hd + LANE)
            dtx_p = dtx_g[:, pls].astype(BF16)
            yds = []
            for h in (ha, ha + 1):
                seg = cum[:, h:h + 1] - cum_t[h:h + 1, :]
                dec = jnp.exp(jnp.where(causal, seg, -jnp.inf))
                yds.append(_dot((cbm * dec).astype(BF16), dtx_p))
            yd = jnp.where(lane_p < hd, yds[0], yds[1])
            hp = h_s[ha // 2]
            yo = _dot_nt(cm, hp.astype(BF16)) * ecx_g[:, pls]
            y_s[:, lanes] = yd + yo + dexp_ref[:, lanes] * xs_g[:, pls]
            s_new = _dot(dtxw_g[:, pls].T.astype(BF16), bm)
            cd = jnp.where(row_p < hd, e_last[:, ha:ha + 1], e_last[:, ha + 1:ha + 2])
            h_s[ha // 2] = hp * cd + s_new

    for g in range(ngrp):
        gl = slice(g * SSD_COLS, (g + 1) * SSD_COLS)
        yg = y_s[:, gl] * _silu(z_ref[0, :, gl])
        ms = jnp.mean(yg * yg, axis=-1, keepdims=True)
        y_ref[0, :, gl] = (yg * lax.rsqrt(ms + EPS)) * ng_ref[:, gl]

    @pl.when(i == last)
    def _():
        hf_ref[0] = h_s[...]


def _head_expand(nh, hd):
    e = np.zeros((LANE, nh * hd), np.float32)
    for h in range(nh):
        e[h, h * hd:(h + 1) * hd] = 1.0
    return jnp.asarray(e, BF16)


def _pad_lanes(v, n=LANE):
    v = v.reshape(1, -1)
    return jnp.pad(v, ((0, 0), (0, n - v.shape[1])))


def _ssd_prompt(z, xbc, dtr, conv_w, conv_b, dt_bias, a_log, d_skip, norm_g):
    bsz, seq, inner = z.shape
    cdim = xbc.shape[-1]
    nh = dt_bias.shape[0]
    width = conv_w.shape[0]
    qn = math.gcd(seq, SSD_CHUNK)
    assert qn == SSD_CHUNK and inner == SSD_GROUPS * SSD_COLS and cdim % SSD_COLS == 0
    dexp = jnp.repeat(d_skip, SSD_HEADDIM).reshape(1, inner)
    blk = lambda n: pl.BlockSpec((1, qn, n), lambda b, i: (b, i, 0))
    y, nbuf, hf = pl.pallas_call(
        functools.partial(_ssd_prompt_kernel, width=width, nh=nh, inner=inner),
        grid=(bsz, seq // qn),
        in_specs=[blk(inner), blk(cdim), blk(LANE),
                  _const2((width, cdim)), _const2((1, cdim)), _const2((1, LANE)), _const2((1, LANE)),
                  _const2((1, inner)), _const2((1, inner)), _const2((LANE, inner))],
        out_specs=[blk(inner),
                   pl.BlockSpec((1, width - 1, cdim), lambda b, i: (b, 0, 0)),
                   pl.BlockSpec((1, nh // 2, LANE, SSD_STATE), lambda b, i: (b, 0, 0, 0))],
        out_shape=[jax.ShapeDtypeStruct((bsz, seq, inner), F32),
                   jax.ShapeDtypeStruct((bsz, width - 1, cdim), F32),
                   jax.ShapeDtypeStruct((bsz, nh // 2, LANE, SSD_STATE), F32)],
        scratch_shapes=[pltpu.VMEM((SUBLANE, cdim), F32), pltpu.VMEM((qn, cdim), F32),
                        pltpu.VMEM((qn, inner), F32), pltpu.VMEM((nh // 2, LANE, SSD_STATE), F32)],
        compiler_params=_cparams(2),
        name="ssd_prompt",
    )(z, xbc, dtr, conv_w, conv_b.reshape(1, cdim), _pad_lanes(dt_bias), _pad_lanes(a_log),
      dexp, norm_g.reshape(1, inner), _head_expand(nh, SSD_HEADDIM))
    return y, nbuf, hf.reshape(bsz, nh, SSD_HEADDIM, SSD_STATE)


def _ssd_prep_sample_kernel(xbc_ref, buf_ref, dtr_ref, cw_ref, cb_ref, dtb_ref, xc_ref, dt_ref, nbuf_ref,
                            *, width, nb, steps, nh):
    cdim = xbc_ref.shape[1]
    for cb in range(cdim // SSD_COLS):
        cols = slice(cb * SSD_COLS, (cb + 1) * SSD_COLS)
        ext = jnp.concatenate([buf_ref[:, cols], xbc_ref[:, cols]], axis=0)
        conv = ext[0:steps * nb] * cw_ref[0:1, cols]
        for k in range(1, width):
            conv = conv + ext[k * nb:(k + steps) * nb] * cw_ref[k:k + 1, cols]
        xc_ref[:, cols] = _silu(conv + cb_ref[:, cols])
        nbuf_ref[:, cols] = ext[steps * nb:]
    hl = lax.broadcasted_iota(I32, dtr_ref.shape, 1)
    dt_ref[...] = jnp.where(hl < nh, _softplus(dtr_ref[...] + dtb_ref[...]), 0.0)


def _ssd_prep_sample(xbc_tb, buf_kb, dtr_tb, conv_w, conv_b, dt_bias, nb, steps):
    rows, cdim = xbc_tb.shape
    width = conv_w.shape[0]
    return pl.pallas_call(
        functools.partial(_ssd_prep_sample_kernel, width=width, nb=nb, steps=steps, nh=dt_bias.shape[0]),
        out_shape=[jax.ShapeDtypeStruct((rows, cdim), F32),
                   jax.ShapeDtypeStruct((rows, LANE), F32),
                   jax.ShapeDtypeStruct(((width - 1) * nb, cdim), F32)],
        compiler_params=pltpu.CompilerParams(vmem_limit_bytes=VMEM_LIMIT),
        name="ssd_prep_sample",
    )(xbc_tb, buf_kb, dtr_tb, conv_w, conv_b.reshape(1, cdim), _pad_lanes(dt_bias))


def _ssd_scan_sample_kernel(xc_ref, dt_ref, z_ref, h0_ref, alog_ref, dexp_ref, ng_ref, e_ref,
                            y_ref, hf_ref, *, steps, inner):
    ngrp, ns = SSD_GROUPS, SSD_STATE
    xc, dt = xc_ref[0], dt_ref[0]
    row = lax.broadcasted_iota(I32, (QP, LANE), 0)
    cum = dt * (-jnp.exp(alog_ref[...]))
    for sft in (1, 2, 4):
        cum = cum + jnp.where(row >= sft, pltpu.roll(cum, sft, 0), 0.0)
    cum_last = cum[QP - 1:QP, :]
    parts = [dt, jnp.exp(cum), dt * jnp.exp(cum_last - cum), jnp.broadcast_to(jnp.exp(cum_last), (QP, LANE))]
    for s in range(steps):
        parts.append(jnp.where(row >= s, jnp.exp(cum - cum[s:s + 1, :]), 0.0))
    stack = jnp.concatenate(parts, axis=0)

    for g in range(ngrp):
        gl = slice(g * SSD_COLS, (g + 1) * SSD_COLS)
        ex = _dot_sel_rhs(stack, e_ref[:, gl])
        dt_x, ecum_x, dtw_x, el_x = (ex[k * QP:(k + 1) * QP] for k in range(4))
        xs_g = xc[:, gl]
        dtx = xs_g * dt_x
        bm = xc[:, inner + g * ns:inner + (g + 1) * ns]
        cm = xc[:, inner + (ngrp + g) * ns:inner + (ngrp + g + 1) * ns].astype(BF16)
        bmp = jnp.concatenate([bm, jnp.zeros((LANE - QP, ns), F32)], axis=0).astype(BF16)
        cbm = _dot_nt(cm, bmp)
        yd = jnp.zeros((QP, SSD_COLS), F32)
        for s in range(steps):
            yd = yd + (ex[(4 + s) * QP:(5 + s) * QP] * cbm[:, s:s + 1]) * dtx[s:s + 1, :]
        h0g = h0_ref[0, gl, :]
        yo = _dot_nt(cm, h0g.astype(BF16)) * ecum_x
        y = (yd + yo + dexp_ref[:, gl] * xs_g) * _silu(z_ref[0, :, gl])
        ms = jnp.mean(y * y, axis=-1, keepdims=True)
        y_ref[0, :, gl] = (y * lax.rsqrt(ms + EPS)) * ng_ref[:, gl]
        tm = jnp.concatenate([xs_g * dtw_x, el_x[0:1], jnp.zeros((LANE - QP - 1, SSD_COLS), F32)], axis=0)
        tt = tm.T
        hf_ref[0, gl, :] = h0g * tt[:, QP:QP + 1] + _dot(tt.astype(BF16), bmp)


def _ssd_scan_sample(xc, dt, z, h0, a_log, d_skip, norm_g, steps):
    nb, _, cdim = xc.shape
    inner = z.shape[-1]
    nh = a_log.shape[0]
    rows_h = nh * SSD_HEADDIM
    dexp = jnp.repeat(d_skip, SSD_HEADDIM).reshape(1, inner)
    one = lambda shape: pl.BlockSpec(shape, lambda b: (0,) * len(shape))
    return pl.pallas_call(
        functools.partial(_ssd_scan_sample_kernel, steps=steps, inner=inner),
        grid=(nb,),
        in_specs=[pl.BlockSpec((1, QP, cdim), lambda b: (b, 0, 0)),
                  pl.BlockSpec((1, QP, LANE), lambda b: (b, 0, 0)),
                  pl.BlockSpec((1, QP, inner), lambda b: (b, 0, 0)),
                  pl.BlockSpec((1, rows_h, SSD_STATE), lambda b: (b, 0, 0)),
                  one((1, LANE)), one((1, inner)), one((1, inner)), one((LANE, inner))],
        out_specs=[pl.BlockSpec((1, QP, inner), lambda b: (b, 0, 0)),
                   pl.BlockSpec((1, rows_h, SSD_STATE), lambda b: (b, 0, 0))],
        out_shape=[jax.ShapeDtypeStruct((nb, QP, inner), F32),
                   jax.ShapeDtypeStruct((nb, rows_h, SSD_STATE), F32)],
        compiler_params=_cparams(1),
        name="ssd_scan_sample",
    )(xc, dt, z, h0, _pad_lanes(a_log), dexp, norm_g.reshape(1, inner), _head_expand(nh, SSD_HEADDIM))


TM = 256


def _to_steps(a, nb, steps):
    n = a.shape[-1]
    return a.reshape(nb, steps, n).transpose(1, 0, 2).reshape(steps * nb, n)


def _to_batch(a, nb, steps):
    n = a.shape[-1]
    return a.reshape(steps, nb, n).transpose(1, 0, 2).reshape(1, nb * steps, n)


def _pad_q(a, nb, steps):
    n = a.shape[-1]
    return jnp.pad(a.reshape(nb, steps, n), ((0, 0), (0, QP - steps), (0, 0)))


def kernel(x_prompt, x_sample, cache_k, cache_v, cache_kidx, state_s5_re, state_s5_im, state_sconv, state_ssd, state_ssd_conv, page_table, c_prompt, c_sample, rel_bias, ada_w, ada_b, norm_mix, norm_mlp, norm_final, attn_w_in, attn_w_out, s5_lam_re, s5_lam_im, s5_log_dt, s5_b_re, s5_b_im, s5_c_re, s5_c_im, s5_d, s5_w_glu, sc_w_in, sc_w_conv, sc_w_out, ssd_w_in, ssd_conv_w, ssd_conv_b, ssd_dt_bias, ssd_a_log, ssd_d, ssd_norm, ssd_w_out, mlp_w1, mlp_w2):
    bp, seq, d = x_prompt.shape
    nb, steps, _ = x_sample.shape
    depth = ada_w.shape[0]
    n_mixers = 4
    rs = nb * steps
    tm = min(TM, seq)

    rows = bp + nb
    c_all = jnp.pad(jnp.concatenate([c_prompt, c_sample], axis=0), ((0, (-rows) % SUBLANE), (0, 0)))
    ada = _ada(c_all, ada_w, ada_b)

    xp = x_prompt
    xs = x_sample.reshape(1, rs, d)
    outs = {name: [] for name in ("kp", "vp", "kip", "ks", "vs", "kis", "s5pr", "s5pi", "s5sr", "s5si",
                                  "scp", "scs", "ssdp", "ssdcp", "ssds", "ssdcs")}
    nq, nkv = N_HEADS * HEAD_DIM, N_KV_HEADS * HEAD_DIM
    for i in range(depth):
        m, j = i % n_mixers, i // n_mixers
        mp = [ada[i, :bp, k * d:(k + 1) * d].reshape(bp, 1, d) for k in range(6)]
        ms = [jnp.repeat(ada[i, bp:bp + nb, k * d:(k + 1) * d], steps, axis=0).reshape(1, rs, d)
              for k in range(6)]
        g_mix = norm_mix[i]
        mix_p = mix_s = None
        if m == 0:
            w_in = attn_w_in[j]
            w_qkv = w_in[:, :nq + 2 * nkv].astype(BF16)
            n_idx = IDX_HEADS * IDX_DIM + LANE
            w_idx = jnp.pad(w_in[:, nq + 2 * nkv:], ((0, 0), (0, n_idx - (w_in.shape[1] - nq - 2 * nkv))))
            w_out = attn_w_out[j].astype(BF16)
            q, k, v, kw, qcat, kcat, kaug, vaug = _attn_proj(xp, mp[0], mp[1], g_mix, w_qkv, w_idx,
                                                             2 * MAX_DISTANCE, True)
            o = _dsa_prompt(q, qcat, kw, kcat, kaug, vaug, rel_bias, MAX_DISTANCE)
            mix_p = ("proj", o, mp[2], w_out)
            page = cache_k.shape[2]
            outs["kp"].append(k.reshape(bp, seq // page, page, N_KV_HEADS, HEAD_DIM))
            outs["vp"].append(v.reshape(bp, seq // page, page, N_KV_HEADS, HEAD_DIM))
            outs["kip"].append(kw[..., :IDX_DIM].reshape(bp, seq // page, page, IDX_DIM))
            q, k, v, kw, qcat, kcat, _, _ = _attn_proj(xs, ms[0], ms[1], g_mix, w_qkv, w_idx, rs, False)
            pps = math.gcd(PAGES_PER_STEP, page_table.shape[1])
            qc_p, kw_p = _pad_q(qcat.astype(F32), nb, steps), _pad_q(kw, nb, steps)
            scores, snew = _dsa_sample_scores(page_table, qc_p, kw_p, _pad_q(kcat.astype(F32), nb, steps),
                                              _pages_t(cache_kidx), j, pps)
            o = _dsa_sample(page_table, _pad_q(q, nb, steps), _pad_q(k, nb, steps), _pad_q(v, nb, steps),
                            scores, snew, _pages_t(cache_k), _pages_t(cache_v), rel_bias, j, steps, pps)
            mix_s = ("proj", o[:, :steps].reshape(1, rs, nq), ms[2], w_out)
            outs["ks"].append(k.reshape(nb, steps, N_KV_HEADS, HEAD_DIM))
            outs["vs"].append(v.reshape(nb, steps, N_KV_HEADS, HEAD_DIM))
            outs["kis"].append(kw[..., :IDX_DIM].reshape(nb, steps, IDX_DIM))
        elif m == 1:
            ar, ai, wb, wc = _s5_weights(s5_lam_re[j], s5_lam_im[j], s5_log_dt[j], s5_b_re[j], s5_b_im[j],
                                         s5_c_re[j], s5_c_im[j])
            w_glu = s5_w_glu[j].astype(BF16)
            grp, nst = s5_lam_re.shape[1:]
            z, fr, fi = _s5_prompt(xp, mp[0], mp[1], g_mix, s5_d[j], ar, ai, wb, wc)
            mix_p = ("glu", z, mp[2], w_glu)
            outs["s5pr"].append(fr.reshape(bp, grp, nst))
            outs["s5pi"].append(fi.reshape(bp, grp, nst))
            z, fr, fi = _s5_sample(_to_steps(xs, nb, steps), _to_steps(ms[0], nb, steps),
                                   _to_steps(ms[1], nb, steps), g_mix, s5_d[j], ar, ai, wb, wc,
                                   state_s5_re[j].reshape(nb, grp * nst), state_s5_im[j].reshape(nb, grp * nst),
                                   nb, steps)
            mix_s = ("glu", _to_batch(z, nb, steps), ms[2], w_glu)
            outs["s5sr"].append(fr.reshape(nb, grp, nst))
            outs["s5si"].append(fi.reshape(nb, grp, nst))
        elif m == 2:
            w_in = sc_w_in[j].astype(BF16)
            w_out = sc_w_out[j].astype(BF16)
            width = sc_w_conv.shape[1]
            (p,) = _proj(xp, mp[0], mp[1], g_mix, w_in, (3 * d,), tm)
            xp, nbuf = _sconv_prompt(p, xp, mp[2], jnp.zeros((bp, width - 1, d), F32), sc_w_conv[j], w_out, tm)
            outs["scp"].append(nbuf)
            (p,) = _proj(xs, ms[0], ms[1], g_mix, w_in, (3 * d,), rs)
            buf = state_sconv[j].transpose(1, 0, 2).reshape((width - 1) * nb, d)
            y, nbuf = _sconv_sample(_to_steps(p, nb, steps), _to_steps(xs, nb, steps), _to_steps(ms[2], nb, steps),
                                    buf, sc_w_conv[j], w_out, nb, steps)
            xs = _to_batch(y, nb, steps)
            outs["scs"].append(nbuf.reshape(width - 1, nb, d).transpose(1, 0, 2))
        else:
            inner = ssd_norm.shape[1]
            cdim = ssd_conv_w.shape[2]
            nh = ssd_dt_bias.shape[1]
            width = ssd_conv_w.shape[1]
            w_in = jnp.pad(ssd_w_in[j], ((0, 0), (0, LANE - nh))).astype(BF16)
            w_out = ssd_w_out[j].astype(BF16)
            z, xbc, dtr = _proj(xp, mp[0], mp[1], g_mix, w_in, (inner, cdim, LANE), tm)
            y, nbuf, hf = _ssd_prompt(z, xbc, dtr, ssd_conv_w[j], ssd_conv_b[j], ssd_dt_bias[j], ssd_a_log[j],
                                      ssd_d[j], ssd_norm[j])
            mix_p = ("proj", y, mp[2], w_out)
            outs["ssdp"].append(hf)
            outs["ssdcp"].append(nbuf)
            z, xbc, dtr = _proj(xs, ms[0], ms[1], g_mix, w_in, (inner, cdim, LANE), rs)
            buf = state_ssd_conv[j].transpose(1, 0, 2).reshape((width - 1) * nb, cdim)
            xc, dt, nbuf = _ssd_prep_sample(_to_steps(xbc, nb, steps), buf, _to_steps(dtr, nb, steps),
                                            ssd_conv_w[j], ssd_conv_b[j], ssd_dt_bias[j], nb, steps)
            y, hf = _ssd_scan_sample(_pad_q(_to_batch(xc, nb, steps), nb, steps),
                                     _pad_q(_to_batch(dt, nb, steps), nb, steps), _pad_q(z, nb, steps),
                                     state_ssd[j].reshape(nb, nh * SSD_HEADDIM, SSD_STATE),
                                     ssd_a_log[j], ssd_d[j], ssd_norm[j], steps)
            mix_s = ("proj", y[:, :steps].reshape(1, rs, inner), ms[2], w_out)
            outs["ssds"].append(hf.reshape(nb, nh, SSD_HEADDIM, SSD_STATE))
            outs["ssdcs"].append(nbuf.reshape(width - 1, nb, cdim).transpose(1, 0, 2))
        w1 = mlp_w1[i].astype(BF16)
        w2 = mlp_w2[i].astype(BF16)
        fin = i == depth - 1
        xp = _mlp(xp, mp[3], mp[4], mp[5], norm_mlp[i], w1, w2, norm_final, fin, tm, mix_p)
        xs = _mlp(xs, ms[3], ms[4], ms[5], norm_mlp[i], w1, w2, norm_final, fin, rs, mix_s)
    st = jnp.stack
    return (xp, xs.reshape(nb, steps, d), st(outs["kp"]), st(outs["vp"]), st(outs["kip"]),
            st(outs["ks"]), st(outs["vs"]), st(outs["kis"]), st(outs["s5pr"]), st(outs["s5pi"]),
            st(outs["s5sr"]), st(outs["s5si"]), st(outs["scp"]), st(outs["scs"]),
            st(outs["ssdp"]), st(outs["ssdcp"]), st(outs["ssds"]), st(outs["ssdcs"]))
```

```python
import functools
import math

import jax
import jax.numpy as jnp
import numpy as np
from jax import lax
from jax.experimental import pallas as pl
from jax.experimental.pallas import tpu as pltpu

F32 = jnp.float32
BF16 = jnp.bfloat16
I32 = jnp.int32

EPS = 1e-6
N_HEADS = 16
HEAD_DIM = 64
N_KV_HEADS = 4
IDX_HEADS = 8
IDX_DIM = 64
TOPK_MAX = 256
N_BUCKETS = 32
MAX_DISTANCE = 128
S5_GROUP = 16
S5_STATE = 64
SSD_HEADDIM = 64
SSD_GROUPS = 4
SSD_STATE = 128
SSD_CHUNK = 128

LANE = 128
SUBLANE = 8
VMEM_LIMIT = 56 * 1024 * 1024
NEG_BIG = -1e30
INT_MIN = -2147483648


def _cparams(n_axes):
    return pltpu.CompilerParams(dimension_semantics=("arbitrary",) * n_axes,
                                vmem_limit_bytes=VMEM_LIMIT)


def _dot(a, b):
    return jnp.dot(a, b, preferred_element_type=F32)


def _dot_nt(a, b):
    return lax.dot_general(a, b, (((1,), (1,)), ((), ())), preferred_element_type=F32)


def _split(a):
    hi = a.astype(BF16)
    lo = (a - hi.astype(F32)).astype(BF16)
    return hi, lo


def _dot3(a, b):
    ah, al = _split(a)
    bh, bl = _split(b)
    return _dot(ah, bh) + (_dot(al, bh) + _dot(ah, bl))


def _rms_mod(x, g, shift, scale):
    y = x * lax.rsqrt(jnp.mean(x * x, axis=-1, keepdims=True) + EPS)
    return (y * g) * (1.0 + scale) + shift


def _sigmoid(x):
    return 1.0 / (1.0 + jnp.exp(-x))


def _silu(x):
    return x * _sigmoid(x)


def _row_block(arr, tm):
    n = arr.shape[-1]
    if arr.shape[1] == 1:
        return pl.BlockSpec((1, 1, n), lambda b, i: (b, 0, 0))
    return pl.BlockSpec((1, tm, n), lambda b, i: (b, i, 0))


def _const2(shape):
    return pl.BlockSpec(shape, lambda b, i: (0,) * len(shape))


def _ada_kernel(c_ref, w_ref, b_ref, o_ref):
    o_ref[0] = _dot3(_silu(c_ref[...]), w_ref[0]) + b_ref[0]


def _ada(c_all, ada_w, ada_b):
    depth, d, n = ada_w.shape
    rows = c_all.shape[0]
    tn = 1536
    return pl.pallas_call(
        _ada_kernel,
        grid=(depth, n // tn),
        in_specs=[pl.BlockSpec((rows, d), lambda l, j: (0, 0)),
                  pl.BlockSpec((1, d, tn), lambda l, j: (l, 0, j)),
                  pl.BlockSpec((1, 1, tn), lambda l, j: (l, 0, j))],
        out_specs=pl.BlockSpec((1, rows, tn), lambda l, j: (l, 0, j)),
        out_shape=jax.ShapeDtypeStruct((depth, rows, n), F32),
        compiler_params=_cparams(2),
        name="ada",
    )(c_all, ada_w, ada_b.reshape(depth, 1, n))


def _proj_kernel(x_ref, sh_ref, sc_ref, g_ref, w_ref, *o_refs, splits):
    h = _rms_mod(x_ref[0], g_ref[...], sh_ref[0], sc_ref[0]).astype(BF16)
    off = 0
    for o_ref, n in zip(o_refs, splits):
        o_ref[0] = _dot(h, w_ref[:, off:off + n])
        off += n


def _proj(x, shift, scale, g, w_bf16, splits, tm):
    bsz, seq, d = x.shape
    n = w_bf16.shape[1]
    assert sum(splits) == n and seq % tm == 0
    return pl.pallas_call(
        functools.partial(_proj_kernel, splits=tuple(splits)),
        grid=(bsz, seq // tm),
        in_specs=[pl.BlockSpec((1, tm, d), lambda b, i: (b, i, 0)),
                  _row_block(shift, tm), _row_block(scale, tm),
                  _const2((1, d)), _const2((d, n))],
        out_specs=[pl.BlockSpec((1, tm, s), lambda b, i: (b, i, 0)) for s in splits],
        out_shape=[jax.ShapeDtypeStruct((bsz, seq, s), F32) for s in splits],
        compiler_params=_cparams(2),
        name="proj",
    )(x, shift, scale, g.reshape(1, d), w_bf16)


def _aug_heads(t, col):
    rows = t.shape[0]
    lane = lax.broadcasted_iota(I32, (rows, LANE - HEAD_DIM), 1)
    extra = jnp.where(lane == 0, col, 0.0).astype(F32)
    parts = []
    for g in range(N_KV_HEADS):
        parts += [t[:, g * HEAD_DIM:(g + 1) * HEAD_DIM], extra]
    return jnp.concatenate(parts, axis=1).astype(BF16)


def _attn_proj_kernel(x_ref, sh_ref, sc_ref, g_ref, w_ref, wi_ref,
                      q_ref, k_ref, v_ref, kw_ref, qcat_ref, kcat_ref, kaug_ref, vaug_ref, *ki_refs,
                      nq, nkv, nqi, key_major):
    (ki_ref,) = ki_refs if key_major else (None,)
    h = _rms_mod(x_ref[0], g_ref[...], sh_ref[0], sc_ref[0])
    hb = h.astype(BF16)
    q_ref[0] = _dot(hb, w_ref[:, 0:nq]) * (HEAD_DIM ** -0.5)
    k = _dot(hb, w_ref[:, nq:nq + nkv])
    v = _dot(hb, w_ref[:, nq + nkv:nq + 2 * nkv])
    r = _dot3(h, wi_ref[...])
    kw = r[:, nqi:nqi + LANE]
    kw_ref[0] = kw
    if key_major:
        page = k_ref.shape[-1]
        kt, vt, kwt = k.T, v.T, kw.T
        for p in range(k_ref.shape[1]):
            k_ref[0, p] = kt[:, p * page:(p + 1) * page]
            v_ref[0, p] = vt[:, p * page:(p + 1) * page]
            ki_ref[0, p] = kwt[0:IDX_DIM, p * page:(p + 1) * page]
    else:
        k_ref[0] = k
        v_ref[0] = v
    zero = jnp.zeros((h.shape[0], IDX_DIM), BF16)
    parts = []
    for hh in range(IDX_HEADS):
        ah, al = _split(r[:, hh * IDX_DIM:(hh + 1) * IDX_DIM])
        parts += [ah, al, ah, zero]
    qcat_ref[0] = jnp.concatenate(parts, axis=1)
    kcat = _idx_rhs(kw[:, 0:IDX_DIM])
    kaug = _aug_heads(k, 0.0)
    vaug_ref[0] = _aug_heads(v, 1.0)
    if key_major:
        kcat_ref[0, 0] = kcat.astype(F32).T.astype(BF16)
        kaug_ref[0, 0] = kaug.astype(F32).T.astype(BF16)
    else:
        kcat_ref[0] = kcat
        kaug_ref[0] = kaug


def _attn_proj(x, shift, scale, g, w_qkv_bf16, w_idx, tm, key_major, page=None):
    bsz, seq, d = x.shape
    nq = N_HEADS * HEAD_DIM
    nkv = N_KV_HEADS * HEAD_DIM
    nqi = IDX_HEADS * IDX_DIM
    rows = lambda n, dt: (pl.BlockSpec((1, tm, n), lambda b, i: (b, i, 0)), jax.ShapeDtypeStruct((bsz, seq, n), dt))
    cols = lambda n, dt: (pl.BlockSpec((1, 1, n, tm), lambda b, i: (b, i, 0, 0)),
                          jax.ShapeDtypeStruct((bsz, seq // tm, n, tm), dt))
    keys = cols if key_major else rows
    if key_major:
        ppt = tm // page
        paged = lambda n: (pl.BlockSpec((1, ppt, n, page), lambda b, i: (b, i, 0, 0)),
                           jax.ShapeDtypeStruct((bsz, seq // page, n, page), F32))
        kv = [paged(nkv), paged(nkv)]
    else:
        kv = [rows(nkv, F32), rows(nkv, F32)]
    outs = [rows(nq, F32)] + kv + [rows(LANE, F32), rows(4 * nqi, BF16),
                                   keys(4 * IDX_DIM, BF16), keys(N_KV_HEADS * LANE, BF16),
                                   rows(N_KV_HEADS * LANE, BF16)]
    if key_major:
        outs.append(paged(IDX_DIM))
    return pl.pallas_call(
        functools.partial(_attn_proj_kernel, nq=nq, nkv=nkv, nqi=nqi, key_major=key_major),
        grid=(bsz, seq // tm),
        in_specs=[pl.BlockSpec((1, tm, d), lambda b, i: (b, i, 0)),
                  _row_block(shift, tm), _row_block(scale, tm),
                  _const2((1, d)), _const2(w_qkv_bf16.shape), _const2(w_idx.shape)],
        out_specs=[o[0] for o in outs],
        out_shape=[o[1] for o in outs],
        compiler_params=_cparams(2),
        name="attn_proj",
    )(x, shift, scale, g.reshape(1, d), w_qkv_bf16, w_idx)


def _mlp_kernel(*refs, ff_chunk, final_norm, mixer_out):
    if mixer_out is None:
        x_ref, sh_ref, sc_ref, gate_ref, g_ref, w1_ref, w2_ref, gf_ref, y_ref = refs
        x = x_ref[0]
    else:
        o_ref, gmix_ref, wo_ref, x_ref, sh_ref, sc_ref, gate_ref, g_ref, w1_ref, w2_ref, gf_ref, y_ref = refs
        d = x_ref.shape[-1]
        ob = o_ref[0].astype(BF16)
        if mixer_out == "glu":
            t = _dot(ob, wo_ref[:, :d]) * _sigmoid(_dot(ob, wo_ref[:, d:]))
        else:
            t = _dot(ob, wo_ref[...])
        x = x_ref[0] + gmix_ref[0] * t
    h = _rms_mod(x, g_ref[...], sh_ref[0], sc_ref[0]).astype(BF16)
    dff = w1_ref.shape[-1]
    acc = jnp.zeros(x.shape, F32)
    for c in range(dff // ff_chunk):
        a = jnp.maximum(_dot(h, w1_ref[0, :, c * ff_chunk:(c + 1) * ff_chunk]), 0.0)
        acc = acc + _dot((a * a).astype(BF16), w2_ref[0, c * ff_chunk:(c + 1) * ff_chunk, :])
    y = x + gate_ref[0] * acc
    if final_norm:
        y = (y * lax.rsqrt(jnp.mean(y * y, axis=-1, keepdims=True) + EPS)) * gf_ref[...]
    y_ref[0] = y


def _resident(shape):
    return pl.BlockSpec(shape, lambda b, i: (0,) * len(shape), pipeline_mode=pl.Buffered(1))


def _mlp(x, shift, scale, gate, g, w1_bf16, w2_bf16, layer, g_final, final_norm, tm, mixer=None):
    bsz, seq, d = x.shape
    dff = w1_bf16.shape[-1]
    layer_w = lambda r, c: pl.BlockSpec((1, r, c), lambda b, i: (layer, 0, 0), pipeline_mode=pl.Buffered(1))
    rows = lambda n: pl.BlockSpec((1, tm, n), lambda b, i: (b, i, 0))
    args, specs, kind = [], [], None
    if mixer is not None:
        kind, o, gmix, wo = mixer
        args += [o, gmix, wo]
        specs += [rows(o.shape[-1]), _row_block(gmix, tm), _resident(wo.shape)]
    args += [x, shift, scale, gate, g.reshape(1, d), w1_bf16, w2_bf16, g_final.reshape(1, d)]
    specs += [rows(d), _row_block(shift, tm), _row_block(scale, tm), _row_block(gate, tm),
              _const2((1, d)), layer_w(d, dff), layer_w(dff, d), _const2((1, d))]
    return pl.pallas_call(
        functools.partial(_mlp_kernel, ff_chunk=1024, final_norm=final_norm, mixer_out=kind),
        grid=(bsz, seq // tm),
        in_specs=specs,
        out_specs=rows(d),
        out_shape=jax.ShapeDtypeStruct((bsz, seq, d), F32),
        compiler_params=_cparams(2),
        name="mlp",
    )(*args)


S5_CH = 128
S5_BLK = (S5_CH // S5_GROUP) * S5_STATE


def _s5_disc_kernel(lr_ref, li_ref, ldt_ref, br_ref, bi_ref, ar_ref, ai_ref, bbr_ref, bbi_ref):
    lr, li = lr_ref[...], li_ref[...]
    dt = jnp.exp(ldt_ref[...])
    mag = jnp.exp(lr * dt)
    ab_re, ab_im = mag * jnp.cos(li * dt), mag * jnp.sin(li * dt)
    den = lr * lr + li * li
    nr = ab_re - 1.0
    f_re = (nr * lr + ab_im * li) / den
    f_im = (ab_im * lr - nr * li) / den
    ar_ref[...] = ab_re
    ai_ref[...] = ab_im
    for c in range(br_ref.shape[0]):
        br, bi = br_ref[c], bi_ref[c]
        bbr_ref[c] = f_re * br - f_im * bi
        bbi_ref[c] = f_re * bi + f_im * br


def _s5_weights(lam_re, lam_im, log_dt, b_re, b_im, c_re, c_im):
    g, p = lam_re.shape
    gc = b_re.shape[-1]
    brt = jnp.moveaxis(b_re, 2, 0)
    bit = jnp.moveaxis(b_im, 2, 0)
    ar, ai, bbr, bbi = pl.pallas_call(
        _s5_disc_kernel,
        out_shape=[jax.ShapeDtypeStruct((g, p), F32), jax.ShapeDtypeStruct((g, p), F32),
                   jax.ShapeDtypeStruct((gc, g, p), F32), jax.ShapeDtypeStruct((gc, g, p), F32)],
        name="s5_disc",
    )(lam_re, lam_im, log_dt.reshape(g, 1), brt, bit)
    nblk = (g * gc) // S5_CH
    gpb = g // nblk
    eye = jnp.eye(gpb, dtype=F32)

    def bd_in(bb):
        t = jnp.moveaxis(bb, 0, 1).reshape(nblk, gpb, gc, p)
        return jnp.einsum('ngcp,gh->ngchp', t, eye).reshape(nblk, gpb * gc, gpb * p)

    def bd_out(cc):
        t = cc.reshape(nblk, gpb, gc, p)
        return jnp.einsum('ngcp,gh->ngphc', t, eye).reshape(nblk, gpb * p, gpb * gc)

    wb = jnp.concatenate([bd_in(bbr), bd_in(bbi)], axis=-1).astype(BF16)
    wc = jnp.concatenate([bd_out(c_re), bd_out(c_im)], axis=1).astype(BF16)
    return ar.reshape(nblk, 1, gpb * p), ai.reshape(nblk, 1, gpb * p), wb, wc


def _gelu_tanh(y):
    return 0.5 * y * (1.0 + jnp.tanh(math.sqrt(2.0 / math.pi) * (y + 0.044715 * (y * y * y))))


def _cmul(ar, ai, xr, xi):
    return ar * xr - ai * xi, ar * xi + ai * xr


def _s5_prompt_kernel(x_ref, sh_ref, sc_ref, g_ref, d_ref, ar_ref, ai_ref, wb_ref, wc_ref,
                      perm_ref, unperm_ref, z_ref, fr_ref, fi_ref, u_s, br_s, bi_s, st_r, st_i,
                      *, nseg, seg):
    i = pl.program_id(1)
    nblk = wb_ref.shape[0]
    blk = ar_ref.shape[-1]

    @pl.when(i == 0)
    def _():
        st_r[...] = jnp.zeros(st_r.shape, F32)
        st_i[...] = jnp.zeros(st_i.shape, F32)

    u_s[...] = _dot_sel_lhs(perm_ref[...], _rms_mod(x_ref[0], g_ref[...], sh_ref[0], sc_ref[0]))

    for c in range(nblk):
        lo, hi = c * S5_CH, (c + 1) * S5_CH
        uc = u_s[:, lo:hi]
        bu = _dot(uc.astype(BF16), wb_ref[c])
        br_s[...] = bu[:, :blk]
        bi_s[...] = bu[:, blk:]
        ar = jnp.broadcast_to(ar_ref[c], (nseg, blk))
        ai = jnp.broadcast_to(ai_ref[c], (nseg, blk))

        def local(j, carry):
            xr, xi = carry
            r0 = pl.multiple_of(j * nseg, nseg)
            pr, pi = _cmul(ar, ai, xr, xi)
            nr = pr + br_s[pl.ds(r0, nseg), :]
            ni = pi + bi_s[pl.ds(r0, nseg), :]
            br_s[pl.ds(r0, nseg), :] = nr
            bi_s[pl.ds(r0, nseg), :] = ni
            return nr, ni

        zero = jnp.zeros((nseg, blk), F32)
        fr, fi = lax.fori_loop(0, seg, local, (zero, zero))

        pr, pi = ar_ref[c], ai_ref[c]
        for _ in range(int(math.log2(seg))):
            pr, pi = _cmul(pr, pi, pr, pi)
        cr, ci = st_r[c], st_i[c]
        rows_r, rows_i = [], []
        for s in range(nseg):
            rows_r.append(cr)
            rows_i.append(ci)
            tr, ti = _cmul(pr, pi, cr, ci)
            cr, ci = tr + fr[s:s + 1], ti + fi[s:s + 1]
        st_r[c] = cr
        st_i[c] = ci
        fr_ref[0, c] = cr
        fi_ref[0, c] = ci
        dr, di = _cmul(ar, ai, jnp.concatenate(rows_r, axis=0), jnp.concatenate(rows_i, axis=0))

        def fix(j, carry):
            dr, di = carry
            r0 = pl.multiple_of(j * nseg, nseg)
            br_s[pl.ds(r0, nseg), :] = br_s[pl.ds(r0, nseg), :] + dr
            bi_s[pl.ds(r0, nseg), :] = bi_s[pl.ds(r0, nseg), :] + di
            return _cmul(ar, ai, dr, di)

        lax.fori_loop(0, seg, fix, (dr, di))

        y = (_dot(br_s[...].astype(BF16), wc_ref[c, :blk, :])
             - _dot(bi_s[...].astype(BF16), wc_ref[c, blk:, :]))
        y = y + d_ref[:, lo:hi] * uc
        u_s[:, lo:hi] = _gelu_tanh(y)

    z_ref[0] = _dot_sel_lhs(unperm_ref[...], u_s[...])


def _s5_prompt(x, shift, scale, g, d_skip, ar, ai, wb, wc):
    bsz, seq, d = x.shape
    nblk, _, blk = ar.shape
    nseg, seg = SUBLANE, 32
    tm = nseg * seg
    assert seq % tm == 0
    perm = np.zeros((tm, tm), np.float32)
    for s in range(nseg):
        for j in range(seg):
            perm[j * nseg + s, s * seg + j] = 1.0
    unperm = jnp.asarray(perm.T, BF16)
    perm = jnp.asarray(perm, BF16)
    z, fr, fi = pl.pallas_call(
        functools.partial(_s5_prompt_kernel, nseg=nseg, seg=seg),
        grid=(bsz, seq // tm),
        in_specs=[pl.BlockSpec((1, tm, d), lambda b, i: (b, i, 0)),
                  _row_block(shift, tm), _row_block(scale, tm),
                  _const2((1, d)), _const2((1, d)),
                  _const2(ar.shape), _const2(ai.shape), _const2(wb.shape), _const2(wc.shape),
                  _const2((tm, tm)), _const2((tm, tm))],
        out_specs=[pl.BlockSpec((1, tm, d), lambda b, i: (b, i, 0)),
                   pl.BlockSpec((1, nblk, 1, blk), lambda b, i: (b, 0, 0, 0)),
                   pl.BlockSpec((1, nblk, 1, blk), lambda b, i: (b, 0, 0, 0))],
        out_shape=[jax.ShapeDtypeStruct((bsz, seq, d), F32),
                   jax.ShapeDtypeStruct((bsz, nblk, 1, blk), F32),
                   jax.ShapeDtypeStruct((bsz, nblk, 1, blk), F32)],
        scratch_shapes=[pltpu.VMEM((tm, d), F32), pltpu.VMEM((tm, blk), F32), pltpu.VMEM((tm, blk), F32),
                        pltpu.VMEM((nblk, 1, blk), F32), pltpu.VMEM((nblk, 1, blk), F32)],
        compiler_params=_cparams(2),
        name="s5_prompt",
    )(x, shift, scale, g.reshape(1, d), d_skip.reshape(1, d), ar, ai, wb, wc, perm, unperm)
    return z, fr, fi


def _s5_sample_kernel(x_ref, sh_ref, sc_ref, g_ref, d_ref, ar_ref, ai_ref, wb_ref, wc_ref,
                      s0r_ref, s0i_ref, z_ref, fr_ref, fi_ref, *, nb, steps):
    nblk = wb_ref.shape[0]
    blk = ar_ref.shape[-1]
    u = _rms_mod(x_ref[...], g_ref[...], sh_ref[...], sc_ref[...])
    for c in range(nblk):
        lo, hi = c * S5_CH, (c + 1) * S5_CH
        uc = u[:, lo:hi]
        bu = _dot(uc.astype(BF16), wb_ref[c])
        ar = jnp.broadcast_to(ar_ref[c], (nb, blk))
        ai = jnp.broadcast_to(ai_ref[c], (nb, blk))
        xr, xi = s0r_ref[:, c * blk:(c + 1) * blk], s0i_ref[:, c * blk:(c + 1) * blk]
        xrs, xis = [], []
        for t in range(steps):
            pr, pi = _cmul(ar, ai, xr, xi)
            xr = pr + bu[t * nb:(t + 1) * nb, :blk]
            xi = pi + bu[t * nb:(t + 1) * nb, blk:]
            xrs.append(xr)
            xis.append(xi)
        fr_ref[:, c * blk:(c + 1) * blk] = xr
        fi_ref[:, c * blk:(c + 1) * blk] = xi
        y = (_dot(jnp.concatenate(xrs, axis=0).astype(BF16), wc_ref[c, :blk, :])
             - _dot(jnp.concatenate(xis, axis=0).astype(BF16), wc_ref[c, blk:, :]))
        y = y + d_ref[:, lo:hi] * uc
        z_ref[:, lo:hi] = _gelu_tanh(y)


def _s5_sample(x_tb, shift_tb, scale_tb, g, d_skip, ar, ai, wb, wc, s0r, s0i, nb, steps):
    rows, d = x_tb.shape
    nblk, _, blk = ar.shape
    return pl.pallas_call(
        functools.partial(_s5_sample_kernel, nb=nb, steps=steps),
        out_shape=[jax.ShapeDtypeStruct((rows, d), F32),
                   jax.ShapeDtypeStruct((nb, nblk * blk), F32),
                   jax.ShapeDtypeStruct((nb, nblk * blk), F32)],
        compiler_params=pltpu.CompilerParams(vmem_limit_bytes=VMEM_LIMIT),
        name="s5_sample",
    )(x_tb, shift_tb, scale_tb, g.reshape(1, d), d_skip.reshape(1, d), ar, ai, wb, wc, s0r, s0i)


def _shift_rows(u, prev, k):
    if k == 0:
        return u
    rolled = pltpu.roll(u, k, 0)
    head = pltpu.roll(prev, k, 0)
    row = lax.broadcasted_iota(I32, (SUBLANE, u.shape[1]), 0)
    first = jnp.where(row < k, head, rolled[:SUBLANE])
    return jnp.concatenate([first, rolled[SUBLANE:]], axis=0)


def _sconv_prompt_kernel(p_ref, x_ref, gate_ref, buf_ref, wc_ref, wo_ref, y_ref, nb_ref, prev_s,
                         *, width):
    i = pl.program_id(1)
    d = x_ref.shape[-1]

    @pl.when(i == 0)
    def _():
        prev_s[...] = jnp.zeros(prev_s.shape, F32)
        prev_s[SUBLANE - (width - 1):, :] = buf_ref[0]

    p = p_ref[0]
    gb, gc, xh = p[:, :d], p[:, d:2 * d], p[:, 2 * d:]
    u = gc * xh
    prev = prev_s[...]
    conv = u * wc_ref[width - 1:width, :]
    for k in range(1, width):
        conv = conv + _shift_rows(u, prev, k) * wc_ref[width - 1 - k:width - k, :]
    prev_s[...] = u[u.shape[0] - SUBLANE:, :]
    nb_ref[0] = u[u.shape[0] - (width - 1):, :]
    y_ref[0] = x_ref[0] + gate_ref[0] * _dot((gb * conv).astype(BF16), wo_ref[...])


def _sconv_prompt(p, x, gate, buf, w_conv, w_out_bf16, tm):
    bsz, seq, d = x.shape
    width = w_conv.shape[0]
    return pl.pallas_call(
        functools.partial(_sconv_prompt_kernel, width=width),
        grid=(bsz, seq // tm),
        in_specs=[pl.BlockSpec((1, tm, 3 * d), lambda b, i: (b, i, 0)),
                  pl.BlockSpec((1, tm, d), lambda b, i: (b, i, 0)),
                  _row_block(gate, tm),
                  pl.BlockSpec((1, width - 1, d), lambda b, i: (b, 0, 0)),
                  _const2((width, d)), _const2((d, d))],
        out_specs=[pl.BlockSpec((1, tm, d), lambda b, i: (b, i, 0)),
                   pl.BlockSpec((1, width - 1, d), lambda b, i: (b, 0, 0))],
        out_shape=[jax.ShapeDtypeStruct((bsz, seq, d), F32),
                   jax.ShapeDtypeStruct((bsz, width - 1, d), F32)],
        scratch_shapes=[pltpu.VMEM((SUBLANE, d), F32)],
        compiler_params=_cparams(2),
        name="sconv_prompt",
    )(p, x, gate, buf, w_conv, w_out_bf16)


def _sconv_sample_kernel(p_ref, x_ref, gate_ref, buf_ref, wc_ref, wo_ref, y_ref, nb_ref,
                         *, width, nb, steps):
    d = x_ref.shape[-1]
    p = p_ref[...]
    gb, gc, xh = p[:, :d], p[:, d:2 * d], p[:, 2 * d:]
    u = gc * xh
    ext = jnp.concatenate([buf_ref[...], u], axis=0)
    conv = ext[0:steps * nb] * wc_ref[0:1, :]
    for k in range(1, width):
        conv = conv + ext[k * nb:(k + steps) * nb] * wc_ref[k:k + 1, :]
    nb_ref[...] = ext[steps * nb:]
    y_ref[...] = x_ref[...] + gate_ref[...] * _dot((gb * conv).astype(BF16), wo_ref[...])


def _sconv_sample(p_tb, x_tb, gate_tb, buf_kb, w_conv, w_out_bf16, nb, steps):
    rows, d = x_tb.shape
    width = w_conv.shape[0]
    return pl.pallas_call(
        functools.partial(_sconv_sample_kernel, width=width, nb=nb, steps=steps),
        out_shape=[jax.ShapeDtypeStruct((rows, d), F32),
                   jax.ShapeDtypeStruct(((width - 1) * nb, d), F32)],
        compiler_params=pltpu.CompilerParams(vmem_limit_bytes=VMEM_LIMIT),
        name="sconv_sample",
    )(p_tb, x_tb, gate_tb, buf_kb, w_conv, w_out_bf16)


def _sortable(s):
    bits = lax.bitcast_convert_type(s + 0.0, I32)
    return bits ^ ((bits >> 31) & 0x7FFFFFFF)


def _stack_heads(qcat):
    w = 4 * IDX_DIM
    return jnp.concatenate([qcat[:, h * w:(h + 1) * w] for h in range(IDX_HEADS)], axis=0)


def _idx_rhs(ki):
    kh, kl = _split(ki)
    return jnp.concatenate([kh, kh, kl, jnp.zeros(kh.shape, BF16)], axis=1)


def _idx_score(dts, wcols, nrows):
    s = wcols[0] * jnp.maximum(dts[0:nrows], 0.0)
    for h in range(1, IDX_HEADS):
        s = s + wcols[h] * jnp.maximum(dts[h * nrows:(h + 1) * nrows], 0.0)
    return s


def _idx_wcols(kw):
    sc = (IDX_HEADS ** -0.5) * (IDX_DIM ** -0.5)
    return [kw[:, IDX_DIM + h:IDX_DIM + h + 1] * sc for h in range(IDX_HEADS)]


def _kth_largest_key(count, rows, lanes, topk):
    def bit_step(it, tu):
        mask = jnp.left_shift(jnp.int32(1), 31 - it)
        cand_u = tu | mask
        cand = jnp.broadcast_to(cand_u ^ INT_MIN, (rows, lanes))
        cnt = count(lambda c, key, idx: jnp.where(key >= cand, 1.0, 0.0))
        return jnp.where(cnt >= topk, cand_u, tu)

    return lax.fori_loop(0, 32, bit_step, jnp.zeros((rows, 1), I32)) ^ INT_MIN


def _topk_select(count, rows, lanes, topk, idx_bits, t=None):
    wide = lambda v: jnp.broadcast_to(v, (rows, lanes))
    if t is None:
        t = _kth_largest_key(count, rows, lanes, topk)
    tw = wide(t)
    n_gt = count(lambda c, key, idx: jnp.where(key > tw, 1.0, 0.0))
    n_ge = count(lambda c, key, idx: jnp.where(key >= tw, 1.0, 0.0))
    need = topk - n_gt
    tied = jnp.max(jnp.where(t != INT_MIN, n_ge - topk, 0.0)) > 0.0

    def tie_search(_):
        def idx_step(it, j):
            cand = j | jnp.left_shift(jnp.int32(1), idx_bits - 1 - it)
            cw = wide(cand)
            cnt = count(lambda c, key, idx: jnp.where((key == tw) & (idx < cw), 1.0, 0.0))
            return jnp.where(cnt < need, cand, j)
        return lax.fori_loop(0, idx_bits, idx_step, jnp.zeros((rows, 1), I32))

    j = lax.cond(tied, tie_search, lambda _: jnp.full((rows, 1), 2 ** idx_bits, I32), 0)
    return t, j


def _dsa_prompt_kernel(q_ref, qcat_ref, kwq_ref, kcat_ref, kaug_ref, vaug_ref, bias_ref, o_ref,
                       key_s, madd_s, hi_s, lo_s, wbc_s, cand_s, qa_s, lg_s, mx_s, sh_s, acc_s,
                       *, tq, topk, idx_bits):
    i = pl.program_id(1)
    nk = i + 1
    nk2 = (nk + 1) // 2
    r = N_HEADS // N_KV_HEADS

    for h, w in enumerate(_idx_wcols(kwq_ref[0])):
        wbc_s[h] = jnp.broadcast_to(w, (tq, tq))
    qrow = lax.broadcasted_iota(I32, (tq, tq), 0)
    kcol = lax.broadcasted_iota(I32, (tq, tq), 1)
    qpos = i * tq + qrow
    wq = 4 * IDX_DIM

    def score_pair(c2, _):
        kc = kcat_ref[0, c2]
        s = None
        for h in range(IDX_HEADS):
            w = wbc_s[h]
            d = jnp.concatenate([w, w], axis=1) * jnp.maximum(_dot(qcat_ref[0, :, h * wq:(h + 1) * wq], kc), 0.0)
            s = d if s is None else s + d
        for u in range(2):
            c = 2 * c2 + u
            su = s[:, u * tq:(u + 1) * tq] + 0.0
            adm = c * tq + kcol <= qpos
            key_s[c] = jnp.where(adm, _sortable(su), INT_MIN)
            top = lax.bitcast_convert_type(lax.bitcast_convert_type(su, I32) & jnp.int32(-65536), F32)
            hi_s[c] = jnp.where(adm, top, -jnp.inf).astype(BF16)
        return 0

    lax.fori_loop(0, nk2, score_pair, 0)

    def count(fn):
        def body(c2, acc):
            c = 2 * c2
            acc = acc + fn(c, key_s[c], c * tq + kcol)
            return acc + fn(c + 1, key_s[c + 1], (c + 1) * tq + kcol)
        acc = lax.fori_loop(0, nk2, body, jnp.zeros((tq, tq), F32))
        return jnp.sum(acc, axis=1, keepdims=True)

    one = jnp.ones((tq, tq), BF16)
    zero = jnp.zeros((tq, tq), BF16)
    wide16 = lambda v: jnp.broadcast_to(v, (tq, tq)).astype(BF16)

    def count16(src_s, cand):
        cand_s[...] = cand

        def body(c2, acc):
            cb = cand_s[...]
            acc = acc + jnp.where(src_s[2 * c2] >= cb, one, zero)
            return acc + jnp.where(src_s[2 * c2 + 1] >= cb, one, zero)

        acc = lax.fori_loop(0, nk2, body, zero)
        return jnp.sum(acc.astype(F32), axis=1, keepdims=True)

    def search16(src_s, bits, need, to_value):
        def step(it, v):
            cand_v = v | jnp.left_shift(jnp.int32(1), bits - 1 - it)
            return jnp.where(count16(src_s, wide16(to_value(cand_v))) >= need, cand_v, v)
        return lax.fori_loop(0, bits, step, jnp.zeros((tq, 1), I32))

    def code_to_score(u):
        s16 = u ^ 0x8000
        b = jnp.where((s16 & 0x8000) != 0, s16 ^ 0x7FFF, s16) & 0xFFFF
        return lax.bitcast_convert_type(b << 16, F32)

    u_hi = search16(hi_s, 16, float(topk), code_to_score)
    k_hi = ((u_hi ^ 0x8000) << 16) >> 16
    k_hi_w = jnp.broadcast_to(k_hi, (tq, tq))

    def split_pair(c2, acc):
        for u in range(2):
            c = 2 * c2 + u
            key = key_s[c]
            top = key >> 16
            same = top == k_hi_w
            hi_s[c] = jnp.where(same, (key >> 8) & 0xFF, -1).astype(F32).astype(BF16)
            lo_s[c] = jnp.where(same, key & 0xFF, -1).astype(F32).astype(BF16)
            acc = acc + jnp.where(top > k_hi_w, 1.0, 0.0)
        return acc

    above = lax.fori_loop(0, nk2, split_pair, jnp.zeros((tq, tq), F32))
    need1 = float(topk) - jnp.sum(above, axis=1, keepdims=True)
    as_f32 = lambda v: v.astype(F32)
    v1 = search16(hi_s, 8, need1, as_f32)
    v1_w = wide16(as_f32(v1))

    def narrow_pair(c2, acc):
        for u in range(2):
            c = 2 * c2 + u
            d1 = hi_s[c]
            lo_s[c] = jnp.where(d1 == v1_w, lo_s[c], -one)
            acc = acc + jnp.where(d1 > v1_w, one, zero)
        return acc

    above1 = lax.fori_loop(0, nk2, narrow_pair, zero)
    need0 = need1 - jnp.sum(above1.astype(F32), axis=1, keepdims=True)
    v0 = search16(lo_s, 8, need0, as_f32)
    t = (k_hi << 16) | (v1 << 8) | v0

    t, j = _topk_select(count, tq, tq, float(topk), idx_bits, t)
    tw = jnp.broadcast_to(t, (tq, tq))
    jw = jnp.broadcast_to(j, (tq, tq))

    def mask_pair(c2, _):
        for u in range(2):
            c = 2 * c2 + u
            key = key_s[c]
            sel = (key > tw) | ((key == tw) & (c * tq + kcol <= jw))
            madd_s[c] = jnp.where(sel & (key != INT_MIN), 0.0, NEG_BIG)
        return 0

    lax.fori_loop(0, nk2, mask_pair, 0)

    q = q_ref[0]
    zpad = jnp.zeros((tq, LANE - HEAD_DIM), F32)
    for g in range(N_KV_HEADS):
        qa_s[g] = jnp.concatenate(
            [jnp.concatenate([q[:, (g * r + u) * HEAD_DIM:(g * r + u + 1) * HEAD_DIM], zpad], axis=1)
             for u in range(r)], axis=0).astype(BF16)

    def pair_mask(c2, near, g):
        halves = []
        for u in range(2):
            c = 2 * c2 + u
            m = madd_s[c]
            if near:
                back = nk - 1 - c
                carries = (back == 0) | ((back == 1) & (kcol > qrow))
                halves.append([m + jnp.where(carries, bias_ref[g * r + v], 0.0) for v in range(r)])
            else:
                halves.append([m] * r)
        return jnp.concatenate([jnp.concatenate([halves[0][v], halves[1][v]], axis=1) for v in range(r)], axis=0)

    ngh = lg_s.shape[0]
    for half in range(N_KV_HEADS // ngh):
        groups = [half * ngh + k for k in range(ngh)]
        mx_s[...] = jnp.full(mx_s.shape, NEG_BIG, F32)

        def logits_pair(c2, near):
            for k, g in enumerate(groups):
                lg = _dot(qa_s[g], kaug_ref[0, c2, g * LANE:(g + 1) * LANE, :]) + pair_mask(c2, near, g)
                lg_s[k, c2] = lg
                mx_s[k] = jnp.maximum(mx_s[k], jnp.maximum(lg[:, :tq], lg[:, tq:]))

        def far(c4, _):
            logits_pair(2 * c4, False)
            logits_pair(2 * c4 + 1, False)
            return 0

        nfar = jnp.maximum(nk2 - 2, 0)
        lax.fori_loop(0, nfar // 2, far, 0)

        @pl.when(nfar % 2 == 1)
        def _():
            logits_pair(nfar - 1, False)

        @pl.when(nk2 >= 2)
        def _():
            logits_pair(nk2 - 2, True)

        logits_pair(nk2 - 1, True)

        for k in range(ngh):
            sh_s[k] = jnp.broadcast_to(jnp.max(mx_s[k], axis=1, keepdims=True), (r * tq, LANE))
        acc_s[...] = jnp.zeros(acc_s.shape, F32)

        def pv_pair(c2):
            r0 = pl.multiple_of(c2 * 2 * tq, 2 * tq)
            va = vaug_ref[0, pl.ds(r0, 2 * tq), :]
            for k, g in enumerate(groups):
                s = sh_s[k]
                p = jnp.exp(lg_s[k, c2] - jnp.concatenate([s, s], axis=1))
                acc_s[k] = acc_s[k] + _dot(p.astype(BF16), va[:, g * LANE:(g + 1) * LANE])

        def pv_two(c4, _):
            pv_pair(2 * c4)
            pv_pair(2 * c4 + 1)
            return 0

        lax.fori_loop(0, nk2 // 2, pv_two, 0)

        @pl.when(nk2 % 2 == 1)
        def _():
            pv_pair(nk2 - 1)
        for k, g in enumerate(groups):
            acc = acc_s[k]
            o = acc[:, 0:HEAD_DIM] / acc[:, HEAD_DIM:HEAD_DIM + 1]
            for u in range(r):
                h = g * r + u
                o_ref[0, :, h * HEAD_DIM:(h + 1) * HEAD_DIM] = o[u * tq:(u + 1) * tq, :]


def _t5_bucket(dist):
    n = jnp.maximum(dist, 0)
    max_exact = N_BUCKETS // 2
    nf = jnp.maximum(n, max_exact).astype(F32)
    large = max_exact + (jnp.log(nf / max_exact) / math.log(MAX_DISTANCE / max_exact)
                         * (N_BUCKETS - max_exact)).astype(I32)
    large = jnp.minimum(large, N_BUCKETS - 1)
    return jnp.where(n < max_exact, n, large)


def _bias_table(rel_bias, n):
    tab = rel_bias[_t5_bucket(jnp.arange(n, dtype=I32))]
    return (tab - rel_bias[N_BUCKETS - 1][None, :]).T


DSA_GROUPS_PER_PASS = 2


def _dsa_prompt(q, qcat, kw, kcat, kaug, vaug, rel_bias, tq):
    bsz, seq, nq = q.shape
    topk = min(TOPK_MAX, seq // 4)
    nkc = seq // tq
    r = N_HEADS // N_KV_HEADS
    assert tq == MAX_DISTANCE and seq % (2 * tq) == 0
    tab = _bias_table(rel_bias, tq)
    dmod = (np.arange(tq)[:, None] - np.arange(tq)[None, :]) % tq
    onehot = (jnp.arange(tq, dtype=I32)[:, None, None] == jnp.asarray(dmod, I32)[None]).astype(F32)
    bias = jnp.einsum('hd,dqk->hqk', tab, onehot, precision=lax.Precision.HIGHEST)
    assert kcat.shape[1:] == (seq // (2 * tq), 4 * IDX_DIM, 2 * tq)
    row = lambda n: pl.BlockSpec((1, tq, n), lambda b, i: (b, i, 0))
    tiles = lambda a: pl.BlockSpec((1,) + a.shape[1:], lambda b, i: (b, 0, 0, 0), pipeline_mode=pl.Buffered(1))
    return pl.pallas_call(
        functools.partial(_dsa_prompt_kernel, tq=tq, topk=topk, idx_bits=int(math.log2(seq))),
        grid=(bsz, nkc),
        in_specs=[row(nq), row(qcat.shape[-1]), row(LANE), tiles(kcat), tiles(kaug),
                  pl.BlockSpec((1, seq, vaug.shape[-1]), lambda b, i: (b, 0, 0), pipeline_mode=pl.Buffered(1)),
                  _resident((N_HEADS, tq, tq))],
        out_specs=row(nq),
        out_shape=jax.ShapeDtypeStruct((bsz, seq, nq), F32),
        scratch_shapes=[pltpu.VMEM((nkc, tq, tq), I32), pltpu.VMEM((nkc, tq, tq), F32),
                        pltpu.VMEM((nkc, tq, tq), BF16), pltpu.VMEM((nkc, tq, tq), BF16),
                        pltpu.VMEM((IDX_HEADS, tq, tq), F32), pltpu.VMEM((tq, tq), BF16),
                        pltpu.VMEM((N_KV_HEADS, r * tq, LANE), BF16),
                        pltpu.VMEM((DSA_GROUPS_PER_PASS, nkc // 2, r * tq, 2 * tq), F32),
                        pltpu.VMEM((DSA_GROUPS_PER_PASS, r * tq, tq), F32),
                        pltpu.VMEM((DSA_GROUPS_PER_PASS, r * tq, LANE), F32),
                        pltpu.VMEM((DSA_GROUPS_PER_PASS, r * tq, LANE), F32)],
        compiler_params=_cparams(2),
        name="dsa_prompt",
    )(q, qcat, kw, kcat, kaug, vaug, bias)


QP = SUBLANE
PAGES_PER_STEP = 16


def _page_specs(shape4, layer, pps):
    def spec(u):
        return pl.BlockSpec((1, 1) + tuple(shape4[2:]),
                            lambda b, g, pt: (layer, pt[b, g * pps + u], 0, 0))
    return [spec(u) for u in range(pps)]


def _pages_t(cache):
    nd = cache.ndim
    t = jnp.transpose(cache, (0, 1) + tuple(range(3, nd)) + (2,))
    return t.reshape(t.shape[0], t.shape[1], -1, t.shape[-1])


def _pad_t(a, page):
    return jnp.concatenate([a, jnp.zeros((page - QP, a.shape[1]), F32)], axis=0).T


def _dsa_sample_score_kernel(pt_ref, qcat_ref, kw_ref, kcn_ref, *rest, pps):
    page_refs, o_ref, on_ref = rest[:pps], rest[pps], rest[pps + 1]
    lhs = _stack_heads(qcat_ref[0]).astype(BF16)
    wcols = _idx_wcols(kw_ref[0])
    kt = jnp.concatenate([page_refs[u][0, 0] for u in range(pps)], axis=1)
    kh, kl = _split(kt)
    rhs = jnp.concatenate([kh, kh, kl, jnp.zeros(kh.shape, BF16)], axis=0)
    o_ref[0] = _idx_score(_dot(lhs, rhs), wcols, QP)

    @pl.when(pl.program_id(1) == pl.num_programs(1) - 1)
    def _():
        on_ref[0] = _idx_score(_dot(lhs, _pad_t(kcn_ref[0], on_ref.shape[-1]).astype(BF16)), wcols, QP)


def _dsa_sample_scores(page_table, qcat, kw, kcn, kidx_t, layer, pps):
    nb, n_pages = page_table.shape
    page = kidx_t.shape[-1]
    bmap = lambda b, g, pt: (b, 0, 0)
    grid_spec = pltpu.PrefetchScalarGridSpec(
        num_scalar_prefetch=1,
        grid=(nb, n_pages // pps),
        in_specs=[pl.BlockSpec((1, QP, qcat.shape[-1]), bmap), pl.BlockSpec((1, QP, LANE), bmap),
                  pl.BlockSpec((1, QP, kcn.shape[-1]), bmap)] + _page_specs(kidx_t.shape, layer, pps),
        out_specs=[pl.BlockSpec((1, QP, pps * page), lambda b, g, pt: (b, 0, g)),
                   pl.BlockSpec((1, QP, page), bmap)])
    return pl.pallas_call(
        functools.partial(_dsa_sample_score_kernel, pps=pps),
        grid_spec=grid_spec,
        out_shape=[jax.ShapeDtypeStruct((nb, QP, n_pages * page), F32),
                   jax.ShapeDtypeStruct((nb, QP, page), F32)],
        compiler_params=_cparams(2),
        name="dsa_sample_scores",
    )(page_table, qcat, kw, kcn, *([kidx_t] * pps))


def _new_token_keys(s_new, steps):
    lane = lax.broadcasted_iota(I32, s_new.shape, 1)
    q = lax.broadcasted_iota(I32, s_new.shape, 0) % QP
    return jnp.where((lane <= q) & (lane < steps), _sortable(s_new), INT_MIN)


def _dsa_sample_select_kernel(sc_ref, sn_ref, t_ref, j_ref, *, steps, topk, idx_bits):
    keys = jnp.concatenate([_sortable(sc_ref[...]), _new_token_keys(sn_ref[...], steps)], axis=1)
    idx = lax.broadcasted_iota(I32, keys.shape, 1)
    count = lambda fn: jnp.sum(fn(0, keys, idx), axis=1, keepdims=True)
    t, j = _topk_select(count, keys.shape[0], keys.shape[1], float(topk), idx_bits)
    t_ref[...] = t
    j_ref[...] = j


def _dsa_sample_select(scores, snew, steps, topk):
    nb, _, past = scores.shape
    page = snew.shape[-1]
    rows = nb * QP
    rb = math.gcd(rows, 64)
    t, j = pl.pallas_call(
        functools.partial(_dsa_sample_select_kernel, steps=steps, topk=topk,
                          idx_bits=int(math.ceil(math.log2(past + page)))),
        grid=(rows // rb,),
        in_specs=[pl.BlockSpec((rb, past), lambda i: (i, 0)), pl.BlockSpec((rb, page), lambda i: (i, 0))],
        out_specs=[pl.BlockSpec((rb, 1), lambda i: (i, 0)), pl.BlockSpec((rb, 1), lambda i: (i, 0))],
        out_shape=[jax.ShapeDtypeStruct((rows, 1), I32), jax.ShapeDtypeStruct((rows, 1), I32)],
        compiler_params=_cparams(1),
        name="dsa_sample_select",
    )(scores.reshape(rows, past), snew.reshape(rows, page))
    return t.reshape(nb, QP, 1), j.reshape(nb, QP, 1)


def _dsa_sample_attend_kernel(pt_ref, q_ref, kn_ref, vn_ref, sc_ref, sn_ref, t_ref, j_ref,
                              blast_ref, bnew_ref, *rest, pps, n_pages, steps):
    k_refs, v_refs, o_ref = rest[:pps], rest[pps:2 * pps], rest[2 * pps]
    lhs_s, m_s, l_s, acc_s = rest[2 * pps + 1:]
    g = pl.program_id(1)
    last = pl.num_programs(1) - 1
    page = k_refs[0].shape[-1]
    past = n_pages * page
    n = pps * page
    r = N_HEADS // N_KV_HEADS
    grow = r * QP

    @pl.when(g == 0)
    def _():
        m_s[...] = jnp.full(m_s.shape, NEG_BIG, F32)
        l_s[...] = jnp.zeros(l_s.shape, F32)
        acc_s[...] = jnp.zeros(acc_s.shape, F32)
        q = q_ref[0]
        for gg in range(N_KV_HEADS):
            lhs_s[gg] = jnp.concatenate(
                [q[:, (gg * r + u) * HEAD_DIM:(gg * r + u + 1) * HEAD_DIM] for u in range(r)], axis=0).astype(BF16)

    t, j = t_ref[0], j_ref[0]

    def select(key, idx):
        sel = (key > t) | ((key == t) & (idx <= j))
        return jnp.where(sel & (key != INT_MIN), 0.0, NEG_BIG)

    def attend(kt, vt, madd, bias):
        mrows = jnp.concatenate([madd] * r, axis=0)
        for gg in range(N_KV_HEADS):
            rows = slice(gg * grow, (gg + 1) * grow)
            feat = slice(gg * HEAD_DIM, (gg + 1) * HEAD_DIM)
            lg = _dot(lhs_s[gg], kt[feat, :]) + mrows
            if bias is not None:
                lg = lg + bias[rows, :]
            m_old = m_s[rows, :]
            m_new = jnp.maximum(m_old, jnp.max(lg, axis=1, keepdims=True))
            p = jnp.exp(lg - m_new)
            alpha = jnp.exp(m_old - m_new)
            l_s[rows, :] = alpha * l_s[rows, :] + jnp.sum(p, axis=1, keepdims=True)
            acc_s[rows, :] = alpha * acc_s[rows, :] + _dot_nt(p.astype(BF16), vt[feat, :])
            m_s[rows, :] = m_new

    def pages(bias):
        kt = jnp.concatenate([k_refs[u][0, 0] for u in range(pps)], axis=1).astype(BF16)
        vt = jnp.concatenate([v_refs[u][0, 0] for u in range(pps)], axis=1).astype(BF16)
        idx = g * n + lax.broadcasted_iota(I32, (QP, n), 1)
        attend(kt, vt, select(_sortable(sc_ref[0]), idx), bias)

    @pl.when(g != last)
    def _():
        pages(None)

    @pl.when(g == last)
    def _():
        pages(jnp.concatenate([jnp.zeros((N_HEADS * QP, n - page), F32), blast_ref[...]], axis=1))
        idx = past + lax.broadcasted_iota(I32, (QP, page), 1)
        attend(_pad_t(kn_ref[0], page).astype(BF16), _pad_t(vn_ref[0], page).astype(BF16),
               select(_new_token_keys(sn_ref[0], steps), idx), bnew_ref[...])
        o = acc_s[...] / l_s[...]
        for h in range(N_HEADS):
            o_ref[0, :, h * HEAD_DIM:(h + 1) * HEAD_DIM] = o[h * QP:(h + 1) * QP, :]


def _dsa_sample(page_table, q, kn, vn, scores, snew, k_t, v_t, rel_bias, layer, steps, pps):
    nb, n_pages = page_table.shape
    page = k_t.shape[-1]
    past = n_pages * page
    nq, nkv = N_HEADS * HEAD_DIM, N_KV_HEADS * HEAD_DIM
    t, j = _dsa_sample_select(scores, snew, steps, min(TOPK_MAX, (past + steps) // 4))
    n_steps = n_pages // pps
    r = N_HEADS // N_KV_HEADS
    tab = _bias_table(rel_bias, 2 * page)
    qq = np.arange(QP)[:, None]
    off = np.arange(page)[None, :]
    blast = tab[:, page + qq - off].reshape(N_HEADS * QP, page)
    bnew = tab[:, np.maximum(qq - off, 0)].reshape(N_HEADS * QP, page)
    bmap = lambda b, g, pt: (b, 0, 0)
    grid_spec = pltpu.PrefetchScalarGridSpec(
        num_scalar_prefetch=1,
        grid=(nb, n_steps),
        in_specs=[pl.BlockSpec((1, QP, nq), bmap),
                  pl.BlockSpec((1, QP, nkv), bmap),
                  pl.BlockSpec((1, QP, nkv), bmap),
                  pl.BlockSpec((1, QP, pps * page), lambda b, g, pt: (b, 0, g)),
                  pl.BlockSpec((1, QP, page), bmap),
                  pl.BlockSpec((1, QP, 1), bmap), pl.BlockSpec((1, QP, 1), bmap),
                  pl.BlockSpec(blast.shape, lambda b, g, pt: (0, 0)),
                  pl.BlockSpec(bnew.shape, lambda b, g, pt: (0, 0))]
                 + _page_specs(k_t.shape, layer, pps) + _page_specs(v_t.shape, layer, pps),
        out_specs=pl.BlockSpec((1, QP, nq), bmap),
        scratch_shapes=[pltpu.VMEM((N_KV_HEADS, r * QP, HEAD_DIM), BF16),
                        pltpu.VMEM((N_HEADS * QP, 1), F32), pltpu.VMEM((N_HEADS * QP, 1), F32),
                        pltpu.VMEM((N_HEADS * QP, HEAD_DIM), F32)])
    return pl.pallas_call(
        functools.partial(_dsa_sample_attend_kernel, pps=pps, n_pages=n_pages, steps=steps),
        grid_spec=grid_spec,
        out_shape=jax.ShapeDtypeStruct((nb, QP, nq), F32),
        compiler_params=_cparams(2),
        name="dsa_sample_attend",
    )(page_table, q, kn, vn, scores, snew, t, j, blast, bnew, *([k_t] * pps), *([v_t] * pps))


SSD_COLS = 512


def _softplus(x):
    return jnp.maximum(x, 0.0) + jnp.log1p(jnp.exp(-jnp.abs(x)))


def _split3(a):
    p1 = a.astype(BF16)
    r1 = a - p1.astype(F32)
    p2 = r1.astype(BF16)
    return p1, p2, (r1 - p2.astype(F32)).astype(BF16)


def _dot_sel_rhs(a, e):
    p1, p2, p3 = _split3(a)
    return _dot(p1, e) + (_dot(p2, e) + _dot(p3, e))


def _dot_sel_lhs(e, a):
    p1, p2, p3 = _split3(a)
    return _dot(e, p1) + (_dot(e, p2) + _dot(e, p3))


def _ssd_prompt_kernel(z_ref, xbc_ref, dtr_ref, cw_ref, cb_ref, dtb_ref, alog_ref, dexp_ref, ng_ref,
                       e_ref, y_ref, nbuf_ref, hf_ref, prev_s, xc_s, y_s, h_s, *, width, nh, inner):
    i = pl.program_id(1)
    last = pl.num_programs(1) - 1
    qn, cdim = xbc_ref.shape[1], xbc_ref.shape[2]
    ngrp, ns, hd = SSD_GROUPS, SSD_STATE, SSD_HEADDIM
    hpg = nh // ngrp

    @pl.when(i == 0)
    def _():
        prev_s[...] = jnp.zeros(prev_s.shape, F32)
        h_s[...] = jnp.zeros(h_s.shape, F32)

    for cb in range(cdim // SSD_COLS):
        cols = slice(cb * SSD_COLS, (cb + 1) * SSD_COLS)
        xr = xbc_ref[0, :, cols]
        prev = prev_s[:, cols]
        conv = xr * cw_ref[width - 1:width, cols]
        for k in range(1, width):
            conv = conv + _shift_rows(xr, prev, k) * cw_ref[width - 1 - k:width - k, cols]
        xc_s[:, cols] = _silu(conv + cb_ref[:, cols])
        prev_s[:, cols] = xr[qn - SUBLANE:, :]
        nbuf_ref[0, :, cols] = xr[qn - (width - 1):, :]

    hl = lax.broadcasted_iota(I32, (qn, LANE), 1)
    dt = jnp.where(hl < nh, _softplus(dtr_ref[0] + dtb_ref[...]), 0.0)
    dta = dt * (-jnp.exp(alog_ref[...]))
    qrow = lax.broadcasted_iota(I32, (qn, qn), 0)
    kcol = lax.broadcasted_iota(I32, (qn, qn), 1)
    causal = kcol <= qrow
    cum = _dot_sel_lhs(jnp.where(causal, 1.0, 0.0).astype(BF16), dta)
    cum_t = cum.T
    cum_last = cum[qn - 1:qn, :]
    ecum = jnp.exp(cum)
    dtw = dt * jnp.exp(cum_last - cum)
    e_last = jnp.exp(cum_last)
    lane_p = lax.broadcasted_iota(I32, (qn, LANE), 1)
    row_p = lax.broadcasted_iota(I32, (LANE, ns), 0)

    for g in range(ngrp):
        gl = slice(g * SSD_COLS, (g + 1) * SSD_COLS)
        e_g = e_ref[:, gl]
        xs_g = xc_s[:, gl]
        dtx_g = xs_g * _dot_sel_rhs(dt, e_g)
        dtxw_g = xs_g * _dot_sel_rhs(dtw, e_g)
        ecx_g = _dot_sel_rhs(ecum, e_g)
        bm = xc_s[:, inner + g * ns:inner + (g + 1) * ns].astype(BF16)
        cm = xc_s[:, inner + (ngrp + g) * ns:inner + (ngrp + g + 1) * ns].astype(BF16)
        cbm = _dot_nt(cm, bm)
        for pi in range(hpg // 2):
            ha = g * hpg + 2 * pi
            pls = slice(pi * LANE, (pi + 1) * LANE)
            lanes = slice(ha * hd, ha * hd + LANE)
            dtx_p = dtx_g[:, pls].astype(BF16)
            yds = []
            for h in (ha, ha + 1):
                seg = cum[:, h:h + 1] - cum_t[h:h + 1, :]
                dec = jnp.exp(jnp.where(causal, seg, -jnp.inf))
                yds.append(_dot((cbm * dec).astype(BF16), dtx_p))
            yd = jnp.where(lane_p < hd, yds[0], yds[1])
            hp = h_s[ha // 2]
            yo = _dot_nt(cm, hp.astype(BF16)) * ecx_g[:, pls]
            y_s[:, lanes] = yd + yo + dexp_ref[:, lanes] * xs_g[:, pls]
            s_new = _dot(dtxw_g[:, pls].T.astype(BF16), bm)
            cd = jnp.where(row_p < hd, e_last[:, ha:ha + 1], e_last[:, ha + 1:ha + 2])
            h_s[ha // 2] = hp * cd + s_new

    for g in range(ngrp):
        gl = slice(g * SSD_COLS, (g + 1) * SSD_COLS)
        yg = y_s[:, gl] * _silu(z_ref[0, :, gl])
        ms = jnp.mean(yg * yg, axis=-1, keepdims=True)
        y_ref[0, :, gl] = (yg * lax.rsqrt(ms + EPS)) * ng_ref[:, gl]

    @pl.when(i == last)
    def _():
        hf_ref[0] = h_s[...]


def _head_expand(nh, hd):
    e = np.zeros((LANE, nh * hd), np.float32)
    for h in range(nh):
        e[h, h * hd:(h + 1) * hd] = 1.0
    return jnp.asarray(e, BF16)


def _pad_lanes(v, n=LANE):
    v = v.reshape(1, -1)
    return jnp.pad(v, ((0, 0), (0, n - v.shape[1])))


def _ssd_prompt(z, xbc, dtr, conv_w, conv_b, dt_bias, a_log, d_skip, norm_g):
    bsz, seq, inner = z.shape
    cdim = xbc.shape[-1]
    nh = dt_bias.shape[0]
    width = conv_w.shape[0]
    qn = math.gcd(seq, SSD_CHUNK)
    assert qn == SSD_CHUNK and inner == SSD_GROUPS * SSD_COLS and cdim % SSD_COLS == 0
    dexp = jnp.repeat(d_skip, SSD_HEADDIM).reshape(1, inner)
    blk = lambda n: pl.BlockSpec((1, qn, n), lambda b, i: (b, i, 0))
    y, nbuf, hf = pl.pallas_call(
        functools.partial(_ssd_prompt_kernel, width=width, nh=nh, inner=inner),
        grid=(bsz, seq // qn),
        in_specs=[blk(inner), blk(cdim), blk(LANE),
                  _const2((width, cdim)), _const2((1, cdim)), _const2((1, LANE)), _const2((1, LANE)),
                  _const2((1, inner)), _const2((1, inner)), _const2((LANE, inner))],
        out_specs=[blk(inner),
                   pl.BlockSpec((1, width - 1, cdim), lambda b, i: (b, 0, 0)),
                   pl.BlockSpec((1, nh // 2, LANE, SSD_STATE), lambda b, i: (b, 0, 0, 0))],
        out_shape=[jax.ShapeDtypeStruct((bsz, seq, inner), F32),
                   jax.ShapeDtypeStruct((bsz, width - 1, cdim), F32),
                   jax.ShapeDtypeStruct((bsz, nh // 2, LANE, SSD_STATE), F32)],
        scratch_shapes=[pltpu.VMEM((SUBLANE, cdim), F32), pltpu.VMEM((qn, cdim), F32),
                        pltpu.VMEM((qn, inner), F32), pltpu.VMEM((nh // 2, LANE, SSD_STATE), F32)],
        compiler_params=_cparams(2),
        name="ssd_prompt",
    )(z, xbc, dtr, conv_w, conv_b.reshape(1, cdim), _pad_lanes(dt_bias), _pad_lanes(a_log),
      dexp, norm_g.reshape(1, inner), _head_expand(nh, SSD_HEADDIM))
    return y, nbuf, hf.reshape(bsz, nh, SSD_HEADDIM, SSD_STATE)


def _ssd_prep_sample_kernel(xbc_ref, buf_ref, dtr_ref, cw_ref, cb_ref, dtb_ref, xc_ref, dt_ref, nbuf_ref,
                            *, width, nb, steps, nh):
    cdim = xbc_ref.shape[1]
    for cb in range(cdim // SSD_COLS):
        cols = slice(cb * SSD_COLS, (cb + 1) * SSD_COLS)
        ext = jnp.concatenate([buf_ref[:, cols], xbc_ref[:, cols]], axis=0)
        conv = ext[0:steps * nb] * cw_ref[0:1, cols]
        for k in range(1, width):
            conv = conv + ext[k * nb:(k + steps) * nb] * cw_ref[k:k + 1, cols]
        xc_ref[:, cols] = _silu(conv + cb_ref[:, cols])
        nbuf_ref[:, cols] = ext[steps * nb:]
    hl = lax.broadcasted_iota(I32, dtr_ref.shape, 1)
    dt_ref[...] = jnp.where(hl < nh, _softplus(dtr_ref[...] + dtb_ref[...]), 0.0)


def _ssd_prep_sample(xbc_tb, buf_kb, dtr_tb, conv_w, conv_b, dt_bias, nb, steps):
    rows, cdim = xbc_tb.shape
    width = conv_w.shape[0]
    return pl.pallas_call(
        functools.partial(_ssd_prep_sample_kernel, width=width, nb=nb, steps=steps, nh=dt_bias.shape[0]),
        out_shape=[jax.ShapeDtypeStruct((rows, cdim), F32),
                   jax.ShapeDtypeStruct((rows, LANE), F32),
                   jax.ShapeDtypeStruct(((width - 1) * nb, cdim), F32)],
        compiler_params=pltpu.CompilerParams(vmem_limit_bytes=VMEM_LIMIT),
        name="ssd_prep_sample",
    )(xbc_tb, buf_kb, dtr_tb, conv_w, conv_b.reshape(1, cdim), _pad_lanes(dt_bias))


def _ssd_scan_sample_kernel(xc_ref, dt_ref, z_ref, h0_ref, alog_ref, dexp_ref, ng_ref, e_ref,
                            y_ref, hf_ref, *, steps, inner):
    ngrp, ns = SSD_GROUPS, SSD_STATE
    xc, dt = xc_ref[0], dt_ref[0]
    row = lax.broadcasted_iota(I32, (QP, LANE), 0)
    cum = dt * (-jnp.exp(alog_ref[...]))
    for sft in (1, 2, 4):
        cum = cum + jnp.where(row >= sft, pltpu.roll(cum, sft, 0), 0.0)
    cum_last = cum[QP - 1:QP, :]
    parts = [dt, jnp.exp(cum), dt * jnp.exp(cum_last - cum), jnp.broadcast_to(jnp.exp(cum_last), (QP, LANE))]
    for s in range(steps):
        parts.append(jnp.where(row >= s, jnp.exp(cum - cum[s:s + 1, :]), 0.0))
    stack = jnp.concatenate(parts, axis=0)

    for g in range(ngrp):
        gl = slice(g * SSD_COLS, (g + 1) * SSD_COLS)
        ex = _dot_sel_rhs(stack, e_ref[:, gl])
        dt_x, ecum_x, dtw_x, el_x = (ex[k * QP:(k + 1) * QP] for k in range(4))
        xs_g = xc[:, gl]
        dtx = xs_g * dt_x
        bm = xc[:, inner + g * ns:inner + (g + 1) * ns]
        cm = xc[:, inner + (ngrp + g) * ns:inner + (ngrp + g + 1) * ns].astype(BF16)
        bmp = jnp.concatenate([bm, jnp.zeros((LANE - QP, ns), F32)], axis=0).astype(BF16)
        cbm = _dot_nt(cm, bmp)
        yd = jnp.zeros((QP, SSD_COLS), F32)
        for s in range(steps):
            yd = yd + (ex[(4 + s) * QP:(5 + s) * QP] * cbm[:, s:s + 1]) * dtx[s:s + 1, :]
        h0g = h0_ref[0, gl, :]
        yo = _dot_nt(cm, h0g.astype(BF16)) * ecum_x
        y = (yd + yo + dexp_ref[:, gl] * xs_g) * _silu(z_ref[0, :, gl])
        ms = jnp.mean(y * y, axis=-1, keepdims=True)
        y_ref[0, :, gl] = (y * lax.rsqrt(ms + EPS)) * ng_ref[:, gl]
        tm = jnp.concatenate([xs_g * dtw_x, el_x[0:1], jnp.zeros((LANE - QP - 1, SSD_COLS), F32)], axis=0)
        tt = tm.T
        hf_ref[0, gl, :] = h0g * tt[:, QP:QP + 1] + _dot(tt.astype(BF16), bmp)


def _ssd_scan_sample(xc, dt, z, h0, a_log, d_skip, norm_g, steps):
    nb, _, cdim = xc.shape
    inner = z.shape[-1]
    nh = a_log.shape[0]
    rows_h = nh * SSD_HEADDIM
    dexp = jnp.repeat(d_skip, SSD_HEADDIM).reshape(1, inner)
    one = lambda shape: pl.BlockSpec(shape, lambda b: (0,) * len(shape))
    return pl.pallas_call(
        functools.partial(_ssd_scan_sample_kernel, steps=steps, inner=inner),
        grid=(nb,),
        in_specs=[pl.BlockSpec((1, QP, cdim), lambda b: (b, 0, 0)),
                  pl.BlockSpec((1, QP, LANE), lambda b: (b, 0, 0)),
                  pl.BlockSpec((1, QP, inner), lambda b: (b, 0, 0)),
                  pl.BlockSpec((1, rows_h, SSD_STATE), lambda b: (b, 0, 0)),
                  one((1, LANE)), one((1, inner)), one((1, inner)), one((LANE, inner))],
        out_specs=[pl.BlockSpec((1, QP, inner), lambda b: (b, 0, 0)),
                   pl.BlockSpec((1, rows_h, SSD_STATE), lambda b: (b, 0, 0))],
        out_shape=[jax.ShapeDtypeStruct((nb, QP, inner), F32),
                   jax.ShapeDtypeStruct((nb, rows_h, SSD_STATE), F32)],
        compiler_params=_cparams(1),
        name="ssd_scan_sample",
    )(xc, dt, z, h0, _pad_lanes(a_log), dexp, norm_g.reshape(1, inner), _head_expand(nh, SSD_HEADDIM))


TM = 256


def _to_steps(a, nb, steps):
    n = a.shape[-1]
    return a.reshape(nb, steps, n).transpose(1, 0, 2).reshape(steps * nb, n)


def _to_batch(a, nb, steps):
    n = a.shape[-1]
    return a.reshape(steps, nb, n).transpose(1, 0, 2).reshape(1, nb * steps, n)


def _pad_q(a, nb, steps):
    n = a.shape[-1]
    return jnp.pad(a.reshape(nb, steps, n), ((0, 0), (0, QP - steps), (0, 0)))


def kernel(x_prompt, x_sample, cache_k, cache_v, cache_kidx, state_s5_re, state_s5_im, state_sconv, state_ssd, state_ssd_conv, page_table, c_prompt, c_sample, rel_bias, ada_w, ada_b, norm_mix, norm_mlp, norm_final, attn_w_in, attn_w_out, s5_lam_re, s5_lam_im, s5_log_dt, s5_b_re, s5_b_im, s5_c_re, s5_c_im, s5_d, s5_w_glu, sc_w_in, sc_w_conv, sc_w_out, ssd_w_in, ssd_conv_w, ssd_conv_b, ssd_dt_bias, ssd_a_log, ssd_d, ssd_norm, ssd_w_out, mlp_w1, mlp_w2):
    bp, seq, d = x_prompt.shape
    nb, steps, _ = x_sample.shape
    depth = ada_w.shape[0]
    n_mixers = 4
    rs = nb * steps
    tm = min(TM, seq)

    rows = bp + nb
    c_all = jnp.pad(jnp.concatenate([c_prompt, c_sample], axis=0), ((0, (-rows) % SUBLANE), (0, 0)))
    ada = _ada(c_all, ada_w, ada_b)

    xp = x_prompt
    xs = x_sample.reshape(1, rs, d)
    outs = {name: [] for name in ("kp", "vp", "kip", "ks", "vs", "kis", "s5pr", "s5pi", "s5sr", "s5si",
                                  "scp", "scs", "ssdp", "ssdcp", "ssds", "ssdcs")}
    nq, nkv = N_HEADS * HEAD_DIM, N_KV_HEADS * HEAD_DIM
    w1_all, w2_all = mlp_w1.astype(BF16), mlp_w2.astype(BF16)
    for i in range(depth):
        m, j = i % n_mixers, i // n_mixers
        mp = [ada[i, :bp, k * d:(k + 1) * d].reshape(bp, 1, d) for k in range(6)]
        ms = [jnp.repeat(ada[i, bp:bp + nb, k * d:(k + 1) * d], steps, axis=0).reshape(1, rs, d)
              for k in range(6)]
        g_mix = norm_mix[i]
        mix_p = mix_s = None
        if m == 0:
            w_in = attn_w_in[j]
            w_qkv = w_in[:, :nq + 2 * nkv].astype(BF16)
            n_idx = IDX_HEADS * IDX_DIM + LANE
            w_idx = jnp.pad(w_in[:, nq + 2 * nkv:], ((0, 0), (0, n_idx - (w_in.shape[1] - nq - 2 * nkv))))
            w_out = attn_w_out[j].astype(BF16)
            page = cache_k.shape[2]
            q, k, v, kw, qcat, kcat, kaug, vaug, ki = _attn_proj(xp, mp[0], mp[1], g_mix, w_qkv, w_idx,
                                                                 2 * MAX_DISTANCE, True, page)
            o = _dsa_prompt(q, qcat, kw, kcat, kaug, vaug, rel_bias, MAX_DISTANCE)
            mix_p = ("proj", o, mp[2], w_out)
            to_pages = lambda t: jnp.transpose(t.reshape(bp, seq // page, N_KV_HEADS, HEAD_DIM, page), (0, 1, 4, 2, 3))
            outs["kp"].append(to_pages(k))
            outs["vp"].append(to_pages(v))
            outs["kip"].append(jnp.transpose(ki, (0, 1, 3, 2)))
            q, k, v, kw, qcat, kcat, _, _ = _attn_proj(xs, ms[0], ms[1], g_mix, w_qkv, w_idx, rs, False)
            pps = math.gcd(PAGES_PER_STEP, page_table.shape[1])
            qc_p, kw_p = _pad_q(qcat.astype(F32), nb, steps), _pad_q(kw, nb, steps)
            scores, snew = _dsa_sample_scores(page_table, qc_p, kw_p, _pad_q(kcat.astype(F32), nb, steps),
                                              _pages_t(cache_kidx), j, pps)
            o = _dsa_sample(page_table, _pad_q(q, nb, steps), _pad_q(k, nb, steps), _pad_q(v, nb, steps),
                            scores, snew, _pages_t(cache_k), _pages_t(cache_v), rel_bias, j, steps, pps)
            mix_s = ("proj", o[:, :steps].reshape(1, rs, nq), ms[2], w_out)
            outs["ks"].append(k.reshape(nb, steps, N_KV_HEADS, HEAD_DIM))
            outs["vs"].append(v.reshape(nb, steps, N_KV_HEADS, HEAD_DIM))
            outs["kis"].append(kw[..., :IDX_DIM].reshape(nb, steps, IDX_DIM))
        elif m == 1:
            ar, ai, wb, wc = _s5_weights(s5_lam_re[j], s5_lam_im[j], s5_log_dt[j], s5_b_re[j], s5_b_im[j],
                                         s5_c_re[j], s5_c_im[j])
            w_glu = s5_w_glu[j].astype(BF16)
            grp, nst = s5_lam_re.shape[1:]
            z, fr, fi = _s5_prompt(xp, mp[0], mp[1], g_mix, s5_d[j], ar, ai, wb, wc)
            mix_p = ("glu", z, mp[2], w_glu)
            outs["s5pr"].append(fr.reshape(bp, grp, nst))
            outs["s5pi"].append(fi.reshape(bp, grp, nst))
            z, fr, fi = _s5_sample(_to_steps(xs, nb, steps), _to_steps(ms[0], nb, steps),
                                   _to_steps(ms[1], nb, steps), g_mix, s5_d[j], ar, ai, wb, wc,
                                   state_s5_re[j].reshape(nb, grp * nst), state_s5_im[j].reshape(nb, grp * nst),
                                   nb, steps)
            mix_s = ("glu", _to_batch(z, nb, steps), ms[2], w_glu)
            outs["s5sr"].append(fr.reshape(nb, grp, nst))
            outs["s5si"].append(fi.reshape(nb, grp, nst))
        elif m == 2:
            w_in = sc_w_in[j].astype(BF16)
            w_out = sc_w_out[j].astype(BF16)
            width = sc_w_conv.shape[1]
            (p,) = _proj(xp, mp[0], mp[1], g_mix, w_in, (3 * d,), tm)
            xp, nbuf = _sconv_prompt(p, xp, mp[2], jnp.zeros((bp, width - 1, d), F32), sc_w_conv[j], w_out, tm)
            outs["scp"].append(nbuf)
            (p,) = _proj(xs, ms[0], ms[1], g_mix, w_in, (3 * d,), rs)
            buf = state_sconv[j].transpose(1, 0, 2).reshape((width - 1) * nb, d)
            y, nbuf = _sconv_sample(_to_steps(p, nb, steps), _to_steps(xs, nb, steps), _to_steps(ms[2], nb, steps),
                                    buf, sc_w_conv[j], w_out, nb, steps)
            xs = _to_batch(y, nb, steps)
            outs["scs"].append(nbuf.reshape(width - 1, nb, d).transpose(1, 0, 2))
        else:
            inner = ssd_norm.shape[1]
            cdim = ssd_conv_w.shape[2]
            nh = ssd_dt_bias.shape[1]
            width = ssd_conv_w.shape[1]
            w_in = jnp.pad(ssd_w_in[j], ((0, 0), (0, LANE - nh))).astype(BF16)
            w_out = ssd_w_out[j].astype(BF16)
            z, xbc, dtr = _proj(xp, mp[0], mp[1], g_mix, w_in, (inner, cdim, LANE), tm)
            y, nbuf, hf = _ssd_prompt(z, xbc, dtr, ssd_conv_w[j], ssd_conv_b[j], ssd_dt_bias[j], ssd_a_log[j],
                                      ssd_d[j], ssd_norm[j])
            mix_p = ("proj", y, mp[2], w_out)
            outs["ssdp"].append(hf)
            outs["ssdcp"].append(nbuf)
            z, xbc, dtr = _proj(xs, ms[0], ms[1], g_mix, w_in, (inner, cdim, LANE), rs)
            buf = state_ssd_conv[j].transpose(1, 0, 2).reshape((width - 1) * nb, cdim)
            xc, dt, nbuf = _ssd_prep_sample(_to_steps(xbc, nb, steps), buf, _to_steps(dtr, nb, steps),
                                            ssd_conv_w[j], ssd_conv_b[j], ssd_dt_bias[j], nb, steps)
            y, hf = _ssd_scan_sample(_pad_q(_to_batch(xc, nb, steps), nb, steps),
                                     _pad_q(_to_batch(dt, nb, steps), nb, steps), _pad_q(z, nb, steps),
                                     state_ssd[j].reshape(nb, nh * SSD_HEADDIM, SSD_STATE),
                                     ssd_a_log[j], ssd_d[j], ssd_norm[j], steps)
            mix_s = ("proj", y[:, :steps].reshape(1, rs, inner), ms[2], w_out)
            outs["ssds"].append(hf.reshape(nb, nh, SSD_HEADDIM, SSD_STATE))
            outs["ssdcs"].append(nbuf.reshape(width - 1, nb, cdim).transpose(1, 0, 2))
        fin = i == depth - 1
        xp = _mlp(xp, mp[3], mp[4], mp[5], norm_mlp[i], w1_all, w2_all, i, norm_final, fin, tm, mix_p)
        xs = _mlp(xs, ms[3], ms[4], ms[5], norm_mlp[i], w1_all, w2_all, i, norm_final, fin, rs, mix_s)
    st = jnp.stack
    return (xp, xs.reshape(nb, steps, d), st(outs["kp"]), st(outs["vp"]), st(outs["kip"]),
            st(outs["ks"]), st(outs["vs"]), st(outs["kis"]), st(outs["s5pr"]), st(outs["s5pi"]),
            st(outs["s5sr"]), st(outs["s5si"]), st(outs["scp"]), st(outs["scs"]),
            st(outs["ssdp"]), st(outs["ssdcp"]), st(outs["ssds"]), st(outs["ssdcs"]))
```

```python
import functools
import math

import jax
import jax.numpy as jnp
import numpy as np
from jax import lax
from jax.experimental import pallas as pl
from jax.experimental.pallas import tpu as pltpu

F32 = jnp.float32
BF16 = jnp.bfloat16
I32 = jnp.int32

EPS = 1e-6
N_HEADS = 16
HEAD_DIM = 64
N_KV_HEADS = 4
IDX_HEADS = 8
IDX_DIM = 64
TOPK_MAX = 256
N_BUCKETS = 32
MAX_DISTANCE = 128
S5_GROUP = 16
S5_STATE = 64
SSD_HEADDIM = 64
SSD_GROUPS = 4
SSD_STATE = 128
SSD_CHUNK = 128

LANE = 128
SUBLANE = 8
VMEM_LIMIT = 56 * 1024 * 1024
NEG_BIG = -1e30
INT_MIN = -2147483648


def _cparams(n_axes):
    return pltpu.CompilerParams(dimension_semantics=("arbitrary",) * n_axes,
                                vmem_limit_bytes=VMEM_LIMIT)


def _dot(a, b):
    return jnp.dot(a, b, preferred_element_type=F32)


def _dot_nt(a, b):
    return lax.dot_general(a, b, (((1,), (1,)), ((), ())), preferred_element_type=F32)


def _split(a):
    hi = a.astype(BF16)
    lo = (a - hi.astype(F32)).astype(BF16)
    return hi, lo


def _dot3(a, b):
    ah, al = _split(a)
    bh, bl = _split(b)
    return _dot(ah, bh) + (_dot(al, bh) + _dot(ah, bl))


def _rms_mod(x, g, shift, scale):
    y = x * lax.rsqrt(jnp.mean(x * x, axis=-1, keepdims=True) + EPS)
    return (y * g) * (1.0 + scale) + shift


def _sigmoid(x):
    return 1.0 / (1.0 + jnp.exp(-x))


def _silu(x):
    return x * _sigmoid(x)


def _row_block(arr, tm):
    n = arr.shape[-1]
    if arr.shape[1] == 1:
        return pl.BlockSpec((1, 1, n), lambda b, i: (b, 0, 0))
    return pl.BlockSpec((1, tm, n), lambda b, i: (b, i, 0))


def _const2(shape):
    return pl.BlockSpec(shape, lambda b, i: (0,) * len(shape))


def _ada_kernel(c_ref, w_ref, b_ref, o_ref):
    o_ref[0] = _dot3(_silu(c_ref[...]), w_ref[0]) + b_ref[0]


def _ada(c_all, ada_w, ada_b):
    depth, d, n = ada_w.shape
    rows = c_all.shape[0]
    tn = 1536
    return pl.pallas_call(
        _ada_kernel,
        grid=(depth, n // tn),
        in_specs=[pl.BlockSpec((rows, d), lambda l, j: (0, 0)),
                  pl.BlockSpec((1, d, tn), lambda l, j: (l, 0, j)),
                  pl.BlockSpec((1, 1, tn), lambda l, j: (l, 0, j))],
        out_specs=pl.BlockSpec((1, rows, tn), lambda l, j: (l, 0, j)),
        out_shape=jax.ShapeDtypeStruct((depth, rows, n), F32),
        compiler_params=_cparams(2),
        name="ada",
    )(c_all, ada_w, ada_b.reshape(depth, 1, n))


def _proj_kernel(x_ref, sh_ref, sc_ref, g_ref, w_ref, *o_refs, splits):
    h = _rms_mod(x_ref[0], g_ref[...], sh_ref[0], sc_ref[0]).astype(BF16)
    off = 0
    for o_ref, n in zip(o_refs, splits):
        o_ref[0] = _dot(h, w_ref[:, off:off + n])
        off += n


def _proj(x, shift, scale, g, w_bf16, splits, tm):
    bsz, seq, d = x.shape
    n = w_bf16.shape[1]
    assert sum(splits) == n and seq % tm == 0
    return pl.pallas_call(
        functools.partial(_proj_kernel, splits=tuple(splits)),
        grid=(bsz, seq // tm),
        in_specs=[pl.BlockSpec((1, tm, d), lambda b, i: (b, i, 0)),
                  _row_block(shift, tm), _row_block(scale, tm),
                  _const2((1, d)), _const2((d, n))],
        out_specs=[pl.BlockSpec((1, tm, s), lambda b, i: (b, i, 0)) for s in splits],
        out_shape=[jax.ShapeDtypeStruct((bsz, seq, s), F32) for s in splits],
        compiler_params=_cparams(2),
        name="proj",
    )(x, shift, scale, g.reshape(1, d), w_bf16)


def _aug_heads(t, col):
    rows = t.shape[0]
    lane = lax.broadcasted_iota(I32, (rows, LANE - HEAD_DIM), 1)
    extra = jnp.where(lane == 0, col, 0.0).astype(F32)
    parts = []
    for g in range(N_KV_HEADS):
        parts += [t[:, g * HEAD_DIM:(g + 1) * HEAD_DIM], extra]
    return jnp.concatenate(parts, axis=1).astype(BF16)


def _attn_proj_kernel(x_ref, sh_ref, sc_ref, g_ref, w_ref, wi_ref,
                      q_ref, k_ref, v_ref, kw_ref, qcat_ref, kcat_ref, kaug_ref, vaug_ref, *ki_refs,
                      nq, nkv, nqi, key_major):
    (ki_ref,) = ki_refs if key_major else (None,)
    h = _rms_mod(x_ref[0], g_ref[...], sh_ref[0], sc_ref[0])
    hb = h.astype(BF16)
    q_ref[0] = _dot(hb, w_ref[:, 0:nq]) * (HEAD_DIM ** -0.5)
    k = _dot(hb, w_ref[:, nq:nq + nkv])
    v = _dot(hb, w_ref[:, nq + nkv:nq + 2 * nkv])
    r = _dot3(h, wi_ref[...])
    kw = r[:, nqi:nqi + LANE]
    kw_ref[0] = kw
    if key_major:
        page = k_ref.shape[-1]
        kt, vt, kwt = k.T, v.T, kw.T
        for p in range(k_ref.shape[1]):
            k_ref[0, p] = kt[:, p * page:(p + 1) * page]
            v_ref[0, p] = vt[:, p * page:(p + 1) * page]
            ki_ref[0, p] = kwt[0:IDX_DIM, p * page:(p + 1) * page]
    else:
        k_ref[0] = k
        v_ref[0] = v
    zero = jnp.zeros((h.shape[0], IDX_DIM), BF16)
    parts = []
    for hh in range(IDX_HEADS):
        ah, al = _split(r[:, hh * IDX_DIM:(hh + 1) * IDX_DIM])
        parts += [ah, al, ah, zero]
    qcat_ref[0] = jnp.concatenate(parts, axis=1)
    kcat = _idx_rhs(kw[:, 0:IDX_DIM])
    kaug = _aug_heads(k, 0.0)
    vaug_ref[0] = _aug_heads(v, 1.0)
    if key_major:
        kcat_ref[0, 0] = kcat.astype(F32).T.astype(BF16)
        kaug_ref[0, 0] = kaug.astype(F32).T.astype(BF16)
    else:
        kcat_ref[0] = kcat
        kaug_ref[0] = kaug


def _attn_proj(x, shift, scale, g, w_qkv_bf16, w_idx, tm, key_major, page=None):
    bsz, seq, d = x.shape
    nq = N_HEADS * HEAD_DIM
    nkv = N_KV_HEADS * HEAD_DIM
    nqi = IDX_HEADS * IDX_DIM
    rows = lambda n, dt: (pl.BlockSpec((1, tm, n), lambda b, i: (b, i, 0)), jax.ShapeDtypeStruct((bsz, seq, n), dt))
    cols = lambda n, dt: (pl.BlockSpec((1, 1, n, tm), lambda b, i: (b, i, 0, 0)),
                          jax.ShapeDtypeStruct((bsz, seq // tm, n, tm), dt))
    keys = cols if key_major else rows
    if key_major:
        ppt = tm // page
        paged = lambda n: (pl.BlockSpec((1, ppt, n, page), lambda b, i: (b, i, 0, 0)),
                           jax.ShapeDtypeStruct((bsz, seq // page, n, page), F32))
        kv = [paged(nkv), paged(nkv)]
    else:
        kv = [rows(nkv, F32), rows(nkv, F32)]
    outs = [rows(nq, F32)] + kv + [rows(LANE, F32), rows(4 * nqi, BF16),
                                   keys(4 * IDX_DIM, BF16), keys(N_KV_HEADS * LANE, BF16),
                                   rows(N_KV_HEADS * LANE, BF16)]
    if key_major:
        outs.append(paged(IDX_DIM))
    return pl.pallas_call(
        functools.partial(_attn_proj_kernel, nq=nq, nkv=nkv, nqi=nqi, key_major=key_major),
        grid=(bsz, seq // tm),
        in_specs=[pl.BlockSpec((1, tm, d), lambda b, i: (b, i, 0)),
                  _row_block(shift, tm), _row_block(scale, tm),
                  _const2((1, d)), _const2(w_qkv_bf16.shape), _const2(w_idx.shape)],
        out_specs=[o[0] for o in outs],
        out_shape=[o[1] for o in outs],
        compiler_params=_cparams(2),
        name="attn_proj",
    )(x, shift, scale, g.reshape(1, d), w_qkv_bf16, w_idx)


def _mlp_kernel(*refs, ff_chunk, final_norm, mixer_out):
    if mixer_out is None:
        x_ref, sh_ref, sc_ref, gate_ref, g_ref, w1_ref, w2_ref, gf_ref, y_ref = refs
        x = x_ref[0]
    else:
        o_ref, gmix_ref, wo_ref, x_ref, sh_ref, sc_ref, gate_ref, g_ref, w1_ref, w2_ref, gf_ref, y_ref = refs
        d = x_ref.shape[-1]
        ob = o_ref[0].astype(BF16)
        if mixer_out == "glu":
            t = _dot(ob, wo_ref[:, :d]) * _sigmoid(_dot(ob, wo_ref[:, d:]))
        else:
            t = _dot(ob, wo_ref[...])
        x = x_ref[0] + gmix_ref[0] * t
    h = _rms_mod(x, g_ref[...], sh_ref[0], sc_ref[0]).astype(BF16)
    dff = w1_ref.shape[-1]
    acc = jnp.zeros(x.shape, F32)
    for c in range(dff // ff_chunk):
        a = jnp.maximum(_dot(h, w1_ref[0, :, c * ff_chunk:(c + 1) * ff_chunk]), 0.0)
        acc = acc + _dot((a * a).astype(BF16), w2_ref[0, c * ff_chunk:(c + 1) * ff_chunk, :])
    y = x + gate_ref[0] * acc
    if final_norm:
        y = (y * lax.rsqrt(jnp.mean(y * y, axis=-1, keepdims=True) + EPS)) * gf_ref[...]
    y_ref[0] = y


def _resident(shape):
    return pl.BlockSpec(shape, lambda b, i: (0,) * len(shape), pipeline_mode=pl.Buffered(1))


def _mlp(x, shift, scale, gate, g, w1_bf16, w2_bf16, layer, g_final, final_norm, tm, mixer=None):
    bsz, seq, d = x.shape
    dff = w1_bf16.shape[-1]
    layer_w = lambda r, c: pl.BlockSpec((1, r, c), lambda b, i: (layer, 0, 0), pipeline_mode=pl.Buffered(1))
    rows = lambda n: pl.BlockSpec((1, tm, n), lambda b, i: (b, i, 0))
    args, specs, kind = [], [], None
    if mixer is not None:
        kind, o, gmix, wo = mixer
        args += [o, gmix, wo]
        specs += [rows(o.shape[-1]), _row_block(gmix, tm), _resident(wo.shape)]
    args += [x, shift, scale, gate, g.reshape(1, d), w1_bf16, w2_bf16, g_final.reshape(1, d)]
    specs += [rows(d), _row_block(shift, tm), _row_block(scale, tm), _row_block(gate, tm),
              _const2((1, d)), layer_w(d, dff), layer_w(dff, d), _const2((1, d))]
    return pl.pallas_call(
        functools.partial(_mlp_kernel, ff_chunk=1024, final_norm=final_norm, mixer_out=kind),
        grid=(bsz, seq // tm),
        in_specs=specs,
        out_specs=rows(d),
        out_shape=jax.ShapeDtypeStruct((bsz, seq, d), F32),
        compiler_params=_cparams(2),
        name="mlp",
    )(*args)


S5_CH = 128
S5_BLK = (S5_CH // S5_GROUP) * S5_STATE


def _s5_disc_kernel(lr_ref, li_ref, ldt_ref, br_ref, bi_ref, ar_ref, ai_ref, bbr_ref, bbi_ref):
    lr, li = lr_ref[...], li_ref[...]
    dt = jnp.exp(ldt_ref[...])
    mag = jnp.exp(lr * dt)
    ab_re, ab_im = mag * jnp.cos(li * dt), mag * jnp.sin(li * dt)
    den = lr * lr + li * li
    nr = ab_re - 1.0
    f_re = (nr * lr + ab_im * li) / den
    f_im = (ab_im * lr - nr * li) / den
    ar_ref[...] = ab_re
    ai_ref[...] = ab_im
    for c in range(br_ref.shape[0]):
        br, bi = br_ref[c], bi_ref[c]
        bbr_ref[c] = f_re * br - f_im * bi
        bbi_ref[c] = f_re * bi + f_im * br


def _s5_weights(lam_re, lam_im, log_dt, b_re, b_im, c_re, c_im):
    g, p = lam_re.shape
    gc = b_re.shape[-1]
    brt = jnp.moveaxis(b_re, 2, 0)
    bit = jnp.moveaxis(b_im, 2, 0)
    ar, ai, bbr, bbi = pl.pallas_call(
        _s5_disc_kernel,
        out_shape=[jax.ShapeDtypeStruct((g, p), F32), jax.ShapeDtypeStruct((g, p), F32),
                   jax.ShapeDtypeStruct((gc, g, p), F32), jax.ShapeDtypeStruct((gc, g, p), F32)],
        name="s5_disc",
    )(lam_re, lam_im, log_dt.reshape(g, 1), brt, bit)
    nblk = (g * gc) // S5_CH
    gpb = g // nblk
    eye = jnp.eye(gpb, dtype=F32)

    def bd_in(bb):
        t = jnp.moveaxis(bb, 0, 1).reshape(nblk, gpb, gc, p)
        return jnp.einsum('ngcp,gh->ngchp', t, eye).reshape(nblk, gpb * gc, gpb * p)

    def bd_out(cc):
        t = cc.reshape(nblk, gpb, gc, p)
        return jnp.einsum('ngcp,gh->ngphc', t, eye).reshape(nblk, gpb * p, gpb * gc)

    wb = jnp.concatenate([bd_in(bbr), bd_in(bbi)], axis=-1).astype(BF16)
    wc = jnp.concatenate([bd_out(c_re), bd_out(c_im)], axis=1).astype(BF16)
    return ar.reshape(nblk, 1, gpb * p), ai.reshape(nblk, 1, gpb * p), wb, wc


def _gelu_tanh(y):
    return 0.5 * y * (1.0 + jnp.tanh(math.sqrt(2.0 / math.pi) * (y + 0.044715 * (y * y * y))))


def _cmul(ar, ai, xr, xi):
    return ar * xr - ai * xi, ar * xi + ai * xr


def _s5_prompt_kernel(x_ref, sh_ref, sc_ref, g_ref, d_ref, ar_ref, ai_ref, wb_ref, wc_ref,
                      perm_ref, unperm_ref, z_ref, fr_ref, fi_ref, u_s, br_s, bi_s, st_r, st_i,
                      *, nseg, seg):
    i = pl.program_id(1)
    nblk = wb_ref.shape[0]
    blk = ar_ref.shape[-1]

    @pl.when(i == 0)
    def _():
        st_r[...] = jnp.zeros(st_r.shape, F32)
        st_i[...] = jnp.zeros(st_i.shape, F32)

    u_s[...] = _dot_sel_lhs(perm_ref[...], _rms_mod(x_ref[0], g_ref[...], sh_ref[0], sc_ref[0]))

    for c in range(nblk):
        lo, hi = c * S5_CH, (c + 1) * S5_CH
        uc = u_s[:, lo:hi]
        bu = _dot(uc.astype(BF16), wb_ref[c])
        br_s[...] = bu[:, :blk]
        bi_s[...] = bu[:, blk:]
        ar = jnp.broadcast_to(ar_ref[c], (nseg, blk))
        ai = jnp.broadcast_to(ai_ref[c], (nseg, blk))

        def local(j, carry):
            xr, xi = carry
            r0 = pl.multiple_of(j * nseg, nseg)
            pr, pi = _cmul(ar, ai, xr, xi)
            nr = pr + br_s[pl.ds(r0, nseg), :]
            ni = pi + bi_s[pl.ds(r0, nseg), :]
            br_s[pl.ds(r0, nseg), :] = nr
            bi_s[pl.ds(r0, nseg), :] = ni
            return nr, ni

        zero = jnp.zeros((nseg, blk), F32)
        fr, fi = lax.fori_loop(0, seg, local, (zero, zero))

        pr, pi = ar_ref[c], ai_ref[c]
        for _ in range(int(math.log2(seg))):
            pr, pi = _cmul(pr, pi, pr, pi)
        cr, ci = st_r[c], st_i[c]
        rows_r, rows_i = [], []
        for s in range(nseg):
            rows_r.append(cr)
            rows_i.append(ci)
            tr, ti = _cmul(pr, pi, cr, ci)
            cr, ci = tr + fr[s:s + 1], ti + fi[s:s + 1]
        st_r[c] = cr
        st_i[c] = ci
        fr_ref[0, c] = cr
        fi_ref[0, c] = ci
        dr, di = _cmul(ar, ai, jnp.concatenate(rows_r, axis=0), jnp.concatenate(rows_i, axis=0))

        def fix(j, carry):
            dr, di = carry
            r0 = pl.multiple_of(j * nseg, nseg)
            br_s[pl.ds(r0, nseg), :] = br_s[pl.ds(r0, nseg), :] + dr
            bi_s[pl.ds(r0, nseg), :] = bi_s[pl.ds(r0, nseg), :] + di
            return _cmul(ar, ai, dr, di)

        lax.fori_loop(0, seg, fix, (dr, di))

        y = (_dot(br_s[...].astype(BF16), wc_ref[c, :blk, :])
             - _dot(bi_s[...].astype(BF16), wc_ref[c, blk:, :]))
        y = y + d_ref[:, lo:hi] * uc
        u_s[:, lo:hi] = _gelu_tanh(y)

    z_ref[0] = _dot_sel_lhs(unperm_ref[...], u_s[...])


def _s5_prompt(x, shift, scale, g, d_skip, ar, ai, wb, wc):
    bsz, seq, d = x.shape
    nblk, _, blk = ar.shape
    nseg, seg = SUBLANE, 32
    tm = nseg * seg
    assert seq % tm == 0
    perm = np.zeros((tm, tm), np.float32)
    for s in range(nseg):
        for j in range(seg):
            perm[j * nseg + s, s * seg + j] = 1.0
    unperm = jnp.asarray(perm.T, BF16)
    perm = jnp.asarray(perm, BF16)
    z, fr, fi = pl.pallas_call(
        functools.partial(_s5_prompt_kernel, nseg=nseg, seg=seg),
        grid=(bsz, seq // tm),
        in_specs=[pl.BlockSpec((1, tm, d), lambda b, i: (b, i, 0)),
                  _row_block(shift, tm), _row_block(scale, tm),
                  _const2((1, d)), _const2((1, d)),
                  _const2(ar.shape), _const2(ai.shape), _const2(wb.shape), _const2(wc.shape),
                  _const2((tm, tm)), _const2((tm, tm))],
        out_specs=[pl.BlockSpec((1, tm, d), lambda b, i: (b, i, 0)),
                   pl.BlockSpec((1, nblk, 1, blk), lambda b, i: (b, 0, 0, 0)),
                   pl.BlockSpec((1, nblk, 1, blk), lambda b, i: (b, 0, 0, 0))],
        out_shape=[jax.ShapeDtypeStruct((bsz, seq, d), F32),
                   jax.ShapeDtypeStruct((bsz, nblk, 1, blk), F32),
                   jax.ShapeDtypeStruct((bsz, nblk, 1, blk), F32)],
        scratch_shapes=[pltpu.VMEM((tm, d), F32), pltpu.VMEM((tm, blk), F32), pltpu.VMEM((tm, blk), F32),
                        pltpu.VMEM((nblk, 1, blk), F32), pltpu.VMEM((nblk, 1, blk), F32)],
        compiler_params=_cparams(2),
        name="s5_prompt",
    )(x, shift, scale, g.reshape(1, d), d_skip.reshape(1, d), ar, ai, wb, wc, perm, unperm)
    return z, fr, fi


def _s5_sample_kernel(x_ref, sh_ref, sc_ref, g_ref, d_ref, ar_ref, ai_ref, wb_ref, wc_ref,
                      s0r_ref, s0i_ref, z_ref, fr_ref, fi_ref, *, nb, steps):
    nblk = wb_ref.shape[0]
    blk = ar_ref.shape[-1]
    u = _rms_mod(x_ref[...], g_ref[...], sh_ref[...], sc_ref[...])
    for c in range(nblk):
        lo, hi = c * S5_CH, (c + 1) * S5_CH
        uc = u[:, lo:hi]
        bu = _dot(uc.astype(BF16), wb_ref[c])
        ar = jnp.broadcast_to(ar_ref[c], (nb, blk))
        ai = jnp.broadcast_to(ai_ref[c], (nb, blk))
        xr, xi = s0r_ref[:, c * blk:(c + 1) * blk], s0i_ref[:, c * blk:(c + 1) * blk]
        xrs, xis = [], []
        for t in range(steps):
            pr, pi = _cmul(ar, ai, xr, xi)
            xr = pr + bu[t * nb:(t + 1) * nb, :blk]
            xi = pi + bu[t * nb:(t + 1) * nb, blk:]
            xrs.append(xr)
            xis.append(xi)
        fr_ref[:, c * blk:(c + 1) * blk] = xr
        fi_ref[:, c * blk:(c + 1) * blk] = xi
        y = (_dot(jnp.concatenate(xrs, axis=0).astype(BF16), wc_ref[c, :blk, :])
             - _dot(jnp.concatenate(xis, axis=0).astype(BF16), wc_ref[c, blk:, :]))
        y = y + d_ref[:, lo:hi] * uc
        z_ref[:, lo:hi] = _gelu_tanh(y)


def _s5_sample(x_tb, shift_tb, scale_tb, g, d_skip, ar, ai, wb, wc, s0r, s0i, nb, steps):
    rows, d = x_tb.shape
    nblk, _, blk = ar.shape
    return pl.pallas_call(
        functools.partial(_s5_sample_kernel, nb=nb, steps=steps),
        out_shape=[jax.ShapeDtypeStruct((rows, d), F32),
                   jax.ShapeDtypeStruct((nb, nblk * blk), F32),
                   jax.ShapeDtypeStruct((nb, nblk * blk), F32)],
        compiler_params=pltpu.CompilerParams(vmem_limit_bytes=VMEM_LIMIT),
        name="s5_sample",
    )(x_tb, shift_tb, scale_tb, g.reshape(1, d), d_skip.reshape(1, d), ar, ai, wb, wc, s0r, s0i)


def _shift_rows(u, prev, k):
    if k == 0:
        return u
    rolled = pltpu.roll(u, k, 0)
    head = pltpu.roll(prev, k, 0)
    row = lax.broadcasted_iota(I32, (SUBLANE, u.shape[1]), 0)
    first = jnp.where(row < k, head, rolled[:SUBLANE])
    return jnp.concatenate([first, rolled[SUBLANE:]], axis=0)


def _sconv_prompt_kernel(p_ref, x_ref, gate_ref, buf_ref, wc_ref, wo_ref, y_ref, nb_ref, prev_s,
                         *, width):
    i = pl.program_id(1)
    d = x_ref.shape[-1]

    @pl.when(i == 0)
    def _():
        prev_s[...] = jnp.zeros(prev_s.shape, F32)
        prev_s[SUBLANE - (width - 1):, :] = buf_ref[0]

    p = p_ref[0]
    gb, gc, xh = p[:, :d], p[:, d:2 * d], p[:, 2 * d:]
    u = gc * xh
    prev = prev_s[...]
    conv = u * wc_ref[width - 1:width, :]
    for k in range(1, width):
        conv = conv + _shift_rows(u, prev, k) * wc_ref[width - 1 - k:width - k, :]
    prev_s[...] = u[u.shape[0] - SUBLANE:, :]
    nb_ref[0] = u[u.shape[0] - (width - 1):, :]
    y_ref[0] = x_ref[0] + gate_ref[0] * _dot((gb * conv).astype(BF16), wo_ref[...])


def _sconv_prompt(p, x, gate, buf, w_conv, w_out_bf16, tm):
    bsz, seq, d = x.shape
    width = w_conv.shape[0]
    return pl.pallas_call(
        functools.partial(_sconv_prompt_kernel, width=width),
        grid=(bsz, seq // tm),
        in_specs=[pl.BlockSpec((1, tm, 3 * d), lambda b, i: (b, i, 0)),
                  pl.BlockSpec((1, tm, d), lambda b, i: (b, i, 0)),
                  _row_block(gate, tm),
                  pl.BlockSpec((1, width - 1, d), lambda b, i: (b, 0, 0)),
                  _const2((width, d)), _const2((d, d))],
        out_specs=[pl.BlockSpec((1, tm, d), lambda b, i: (b, i, 0)),
                   pl.BlockSpec((1, width - 1, d), lambda b, i: (b, 0, 0))],
        out_shape=[jax.ShapeDtypeStruct((bsz, seq, d), F32),
                   jax.ShapeDtypeStruct((bsz, width - 1, d), F32)],
        scratch_shapes=[pltpu.VMEM((SUBLANE, d), F32)],
        compiler_params=_cparams(2),
        name="sconv_prompt",
    )(p, x, gate, buf, w_conv, w_out_bf16)


def _sconv_sample_kernel(p_ref, x_ref, gate_ref, buf_ref, wc_ref, wo_ref, y_ref, nb_ref,
                         *, width, nb, steps):
    d = x_ref.shape[-1]
    p = p_ref[...]
    gb, gc, xh = p[:, :d], p[:, d:2 * d], p[:, 2 * d:]
    u = gc * xh
    ext = jnp.concatenate([buf_ref[...], u], axis=0)
    conv = ext[0:steps * nb] * wc_ref[0:1, :]
    for k in range(1, width):
        conv = conv + ext[k * nb:(k + steps) * nb] * wc_ref[k:k + 1, :]
    nb_ref[...] = ext[steps * nb:]
    y_ref[...] = x_ref[...] + gate_ref[...] * _dot((gb * conv).astype(BF16), wo_ref[...])


def _sconv_sample(p_tb, x_tb, gate_tb, buf_kb, w_conv, w_out_bf16, nb, steps):
    rows, d = x_tb.shape
    width = w_conv.shape[0]
    return pl.pallas_call(
        functools.partial(_sconv_sample_kernel, width=width, nb=nb, steps=steps),
        out_shape=[jax.ShapeDtypeStruct((rows, d), F32),
                   jax.ShapeDtypeStruct(((width - 1) * nb, d), F32)],
        compiler_params=pltpu.CompilerParams(vmem_limit_bytes=VMEM_LIMIT),
        name="sconv_sample",
    )(p_tb, x_tb, gate_tb, buf_kb, w_conv, w_out_bf16)


def _sortable(s):
    bits = lax.bitcast_convert_type(s + 0.0, I32)
    return bits ^ ((bits >> 31) & 0x7FFFFFFF)


def _stack_heads(qcat):
    w = 4 * IDX_DIM
    return jnp.concatenate([qcat[:, h * w:(h + 1) * w] for h in range(IDX_HEADS)], axis=0)


def _idx_rhs(ki):
    kh, kl = _split(ki)
    return jnp.concatenate([kh, kh, kl, jnp.zeros(kh.shape, BF16)], axis=1)


def _idx_score(dts, wcols, nrows):
    s = wcols[0] * jnp.maximum(dts[0:nrows], 0.0)
    for h in range(1, IDX_HEADS):
        s = s + wcols[h] * jnp.maximum(dts[h * nrows:(h + 1) * nrows], 0.0)
    return s


def _idx_wcols(kw):
    sc = (IDX_HEADS ** -0.5) * (IDX_DIM ** -0.5)
    return [kw[:, IDX_DIM + h:IDX_DIM + h + 1] * sc for h in range(IDX_HEADS)]


def _topk_select(count, rows, lanes, topk, idx_bits, stash=lambda v: (lambda: v)):
    wide = lambda v: jnp.broadcast_to(v, (rows, lanes))

    def bit_step(it, tu):
        mask = jnp.left_shift(jnp.int32(1), 31 - it)
        cand_u = tu | mask
        cand = stash(wide(cand_u ^ INT_MIN))
        cnt = count(lambda c, key, idx: jnp.where(key >= cand(), 1.0, 0.0))
        return jnp.where(cnt >= topk, cand_u, tu)

    t = lax.fori_loop(0, 32, bit_step, jnp.zeros((rows, 1), I32)) ^ INT_MIN
    tw = wide(t)
    n_gt = count(lambda c, key, idx: jnp.where(key > tw, 1.0, 0.0))
    n_ge = count(lambda c, key, idx: jnp.where(key >= tw, 1.0, 0.0))
    need = topk - n_gt
    tied = jnp.max(jnp.where(t != INT_MIN, n_ge - topk, 0.0)) > 0.0

    def tie_search(_):
        def idx_step(it, j):
            cand = j | jnp.left_shift(jnp.int32(1), idx_bits - 1 - it)
            cw = wide(cand)
            cnt = count(lambda c, key, idx: jnp.where((key == tw) & (idx < cw), 1.0, 0.0))
            return jnp.where(cnt < need, cand, j)
        return lax.fori_loop(0, idx_bits, idx_step, jnp.zeros((rows, 1), I32))

    j = lax.cond(tied, tie_search, lambda _: jnp.full((rows, 1), 2 ** idx_bits, I32), 0)
    return t, j


def _dsa_prompt_kernel(q_ref, qcat_ref, kwq_ref, kcat_ref, kaug_ref, vaug_ref, bias_ref, o_ref,
                       key_s, madd_s, wbc_s, cand_s, qa_s, lg_s, mx_s, sh_s, acc_s, *, tq, topk, idx_bits):
    i = pl.program_id(1)
    nk = i + 1
    nk2 = (nk + 1) // 2
    r = N_HEADS // N_KV_HEADS

    for h, w in enumerate(_idx_wcols(kwq_ref[0])):
        wbc_s[h] = jnp.broadcast_to(w, (tq, tq))
    qrow = lax.broadcasted_iota(I32, (tq, tq), 0)
    kcol = lax.broadcasted_iota(I32, (tq, tq), 1)
    qpos = i * tq + qrow
    wq = 4 * IDX_DIM

    def score_pair(c2, _):
        kc = kcat_ref[0, c2]
        s = None
        for hp in range(IDX_HEADS // 2):
            lhs = jnp.concatenate([qcat_ref[0, :, (2 * hp + v) * wq:(2 * hp + v + 1) * wq] for v in range(2)], axis=0)
            dd = jnp.maximum(_dot(lhs, kc), 0.0)
            for v in range(2):
                w = wbc_s[2 * hp + v]
                d = jnp.concatenate([w, w], axis=1) * dd[v * tq:(v + 1) * tq]
                s = d if s is None else s + d
        for u in range(2):
            c = 2 * c2 + u
            key_s[c] = jnp.where(c * tq + kcol <= qpos, _sortable(s[:, u * tq:(u + 1) * tq]), INT_MIN)
        return 0

    lax.fori_loop(0, nk2, score_pair, 0)

    def count(fn):
        def body(c2, acc):
            c = 2 * c2
            acc = acc + fn(c, key_s[c], c * tq + kcol)
            return acc + fn(c + 1, key_s[c + 1], (c + 1) * tq + kcol)
        acc = lax.fori_loop(0, nk2, body, jnp.zeros((tq, tq), F32))
        return jnp.sum(acc, axis=1, keepdims=True)

    def stash(v):
        cand_s[...] = v
        return lambda: cand_s[...]

    t, j = _topk_select(count, tq, tq, float(topk), idx_bits, stash)
    tw = jnp.broadcast_to(t, (tq, tq))
    jw = jnp.broadcast_to(j, (tq, tq))

    def mask_pair(c2, _):
        for u in range(2):
            c = 2 * c2 + u
            key = key_s[c]
            sel = (key > tw) | ((key == tw) & (c * tq + kcol <= jw))
            madd_s[c] = jnp.where(sel & (key != INT_MIN), 0.0, NEG_BIG)
        return 0

    lax.fori_loop(0, nk2, mask_pair, 0)

    q = q_ref[0]
    zpad = jnp.zeros((tq, LANE - HEAD_DIM), F32)
    for g in range(N_KV_HEADS):
        qa_s[g] = jnp.concatenate(
            [jnp.concatenate([q[:, (g * r + u) * HEAD_DIM:(g * r + u + 1) * HEAD_DIM], zpad], axis=1)
             for u in range(r)], axis=0).astype(BF16)

    def pair_mask(c2, near, g):
        halves = []
        for u in range(2):
            c = 2 * c2 + u
            m = madd_s[c]
            if near:
                back = nk - 1 - c
                carries = (back == 0) | ((back == 1) & (kcol > qrow))
                halves.append([m + jnp.where(carries, bias_ref[g * r + v], 0.0) for v in range(r)])
            else:
                halves.append([m] * r)
        return jnp.concatenate([jnp.concatenate([halves[0][v], halves[1][v]], axis=1) for v in range(r)], axis=0)

    ngh = lg_s.shape[0]
    for half in range(N_KV_HEADS // ngh):
        groups = [half * ngh + k for k in range(ngh)]
        mx_s[...] = jnp.full(mx_s.shape, NEG_BIG, F32)

        def logits_pair(c2, near):
            for k, g in enumerate(groups):
                lg = _dot(qa_s[g], kaug_ref[0, c2, g * LANE:(g + 1) * LANE, :]) + pair_mask(c2, near, g)
                lg_s[k, c2] = lg
                mx_s[k] = jnp.maximum(mx_s[k], jnp.maximum(lg[:, :tq], lg[:, tq:]))

        def far(c4, _):
            logits_pair(2 * c4, False)
            logits_pair(2 * c4 + 1, False)
            return 0

        nfar = jnp.maximum(nk2 - 2, 0)
        lax.fori_loop(0, nfar // 2, far, 0)

        @pl.when(nfar % 2 == 1)
        def _():
            logits_pair(nfar - 1, False)

        @pl.when(nk2 >= 2)
        def _():
            logits_pair(nk2 - 2, True)

        logits_pair(nk2 - 1, True)

        for k in range(ngh):
            sh_s[k] = jnp.broadcast_to(jnp.max(mx_s[k], axis=1, keepdims=True), (r * tq, LANE))
        acc_s[...] = jnp.zeros(acc_s.shape, F32)

        def pv_pair(c2):
            r0 = pl.multiple_of(c2 * 2 * tq, 2 * tq)
            va = vaug_ref[0, pl.ds(r0, 2 * tq), :]
            for k, g in enumerate(groups):
                s = sh_s[k]
                p = jnp.exp(lg_s[k, c2] - jnp.concatenate([s, s], axis=1))
                acc_s[k] = acc_s[k] + _dot(p.astype(BF16), va[:, g * LANE:(g + 1) * LANE])

        def pv_two(c4, _):
            pv_pair(2 * c4)
            pv_pair(2 * c4 + 1)
            return 0

        lax.fori_loop(0, nk2 // 2, pv_two, 0)

        @pl.when(nk2 % 2 == 1)
        def _():
            pv_pair(nk2 - 1)
        for k, g in enumerate(groups):
            acc = acc_s[k]
            o = acc[:, 0:HEAD_DIM] / acc[:, HEAD_DIM:HEAD_DIM + 1]
            for u in range(r):
                h = g * r + u
                o_ref[0, :, h * HEAD_DIM:(h + 1) * HEAD_DIM] = o[u * tq:(u + 1) * tq, :]


def _t5_bucket(dist):
    n = jnp.maximum(dist, 0)
    max_exact = N_BUCKETS // 2
    nf = jnp.maximum(n, max_exact).astype(F32)
    large = max_exact + (jnp.log(nf / max_exact) / math.log(MAX_DISTANCE / max_exact)
                         * (N_BUCKETS - max_exact)).astype(I32)
    large = jnp.minimum(large, N_BUCKETS - 1)
    return jnp.where(n < max_exact, n, large)


def _bias_table(rel_bias, n):
    tab = rel_bias[_t5_bucket(jnp.arange(n, dtype=I32))]
    return (tab - rel_bias[N_BUCKETS - 1][None, :]).T


DSA_GROUPS_PER_PASS = 2


def _dsa_prompt(q, qcat, kw, kcat, kaug, vaug, rel_bias, tq):
    bsz, seq, nq = q.shape
    topk = min(TOPK_MAX, seq // 4)
    nkc = seq // tq
    r = N_HEADS // N_KV_HEADS
    assert tq == MAX_DISTANCE and seq % (2 * tq) == 0
    tab = _bias_table(rel_bias, tq)
    dmod = (np.arange(tq)[:, None] - np.arange(tq)[None, :]) % tq
    onehot = (jnp.arange(tq, dtype=I32)[:, None, None] == jnp.asarray(dmod, I32)[None]).astype(F32)
    bias = jnp.einsum('hd,dqk->hqk', tab, onehot, precision=lax.Precision.HIGHEST)
    assert kcat.shape[1:] == (seq // (2 * tq), 4 * IDX_DIM, 2 * tq)
    row = lambda n: pl.BlockSpec((1, tq, n), lambda b, i: (b, i, 0))
    tiles = lambda a: pl.BlockSpec((1,) + a.shape[1:], lambda b, i: (b, 0, 0, 0), pipeline_mode=pl.Buffered(1))
    return pl.pallas_call(
        functools.partial(_dsa_prompt_kernel, tq=tq, topk=topk, idx_bits=int(math.log2(seq))),
        grid=(bsz, nkc),
        in_specs=[row(nq), row(qcat.shape[-1]), row(LANE), tiles(kcat), tiles(kaug),
                  pl.BlockSpec((1, seq, vaug.shape[-1]), lambda b, i: (b, 0, 0), pipeline_mode=pl.Buffered(1)),
                  _resident((N_HEADS, tq, tq))],
        out_specs=row(nq),
        out_shape=jax.ShapeDtypeStruct((bsz, seq, nq), F32),
        scratch_shapes=[pltpu.VMEM((nkc, tq, tq), I32), pltpu.VMEM((nkc, tq, tq), F32),
                        pltpu.VMEM((IDX_HEADS, tq, tq), F32), pltpu.VMEM((tq, tq), I32),
                        pltpu.VMEM((N_KV_HEADS, r * tq, LANE), BF16),
                        pltpu.VMEM((DSA_GROUPS_PER_PASS, nkc // 2, r * tq, 2 * tq), F32),
                        pltpu.VMEM((DSA_GROUPS_PER_PASS, r * tq, tq), F32),
                        pltpu.VMEM((DSA_GROUPS_PER_PASS, r * tq, LANE), F32),
                        pltpu.VMEM((DSA_GROUPS_PER_PASS, r * tq, LANE), F32)],
        compiler_params=_cparams(2),
        name="dsa_prompt",
    )(q, qcat, kw, kcat, kaug, vaug, bias)


QP = SUBLANE
PAGES_PER_STEP = 16


def _page_specs(shape4, layer, pps):
    def spec(u):
        return pl.BlockSpec((1, 1) + tuple(shape4[2:]),
                            lambda b, g, pt: (layer, pt[b, g * pps + u], 0, 0))
    return [spec(u) for u in range(pps)]


def _pages_t(cache):
    nd = cache.ndim
    t = jnp.transpose(cache, (0, 1) + tuple(range(3, nd)) + (2,))
    return t.reshape(t.shape[0], t.shape[1], -1, t.shape[-1])


def _pad_t(a, page):
    return jnp.concatenate([a, jnp.zeros((page - QP, a.shape[1]), F32)], axis=0).T


def _dsa_sample_score_kernel(pt_ref, qcat_ref, kw_ref, kcn_ref, *rest, pps):
    page_refs, o_ref, on_ref = rest[:pps], rest[pps], rest[pps + 1]
    lhs = _stack_heads(qcat_ref[0]).astype(BF16)
    wcols = _idx_wcols(kw_ref[0])
    kt = jnp.concatenate([page_refs[u][0, 0] for u in range(pps)], axis=1)
    kh, kl = _split(kt)
    rhs = jnp.concatenate([kh, kh, kl, jnp.zeros(kh.shape, BF16)], axis=0)
    o_ref[0] = _idx_score(_dot(lhs, rhs), wcols, QP)

    @pl.when(pl.program_id(1) == pl.num_programs(1) - 1)
    def _():
        on_ref[0] = _idx_score(_dot(lhs, _pad_t(kcn_ref[0], on_ref.shape[-1]).astype(BF16)), wcols, QP)


def _dsa_sample_scores(page_table, qcat, kw, kcn, kidx_t, layer, pps):
    nb, n_pages = page_table.shape
    page = kidx_t.shape[-1]
    bmap = lambda b, g, pt: (b, 0, 0)
    grid_spec = pltpu.PrefetchScalarGridSpec(
        num_scalar_prefetch=1,
        grid=(nb, n_pages // pps),
        in_specs=[pl.BlockSpec((1, QP, qcat.shape[-1]), bmap), pl.BlockSpec((1, QP, LANE), bmap),
                  pl.BlockSpec((1, QP, kcn.shape[-1]), bmap)] + _page_specs(kidx_t.shape, layer, pps),
        out_specs=[pl.BlockSpec((1, QP, pps * page), lambda b, g, pt: (b, 0, g)),
                   pl.BlockSpec((1, QP, page), bmap)])
    return pl.pallas_call(
        functools.partial(_dsa_sample_score_kernel, pps=pps),
        grid_spec=grid_spec,
        out_shape=[jax.ShapeDtypeStruct((nb, QP, n_pages * page), F32),
                   jax.ShapeDtypeStruct((nb, QP, page), F32)],
        compiler_params=_cparams(2),
        name="dsa_sample_scores",
    )(page_table, qcat, kw, kcn, *([kidx_t] * pps))


def _new_token_keys(s_new, steps):
    lane = lax.broadcasted_iota(I32, s_new.shape, 1)
    q = lax.broadcasted_iota(I32, s_new.shape, 0) % QP
    return jnp.where((lane <= q) & (lane < steps), _sortable(s_new), INT_MIN)


def _dsa_sample_select_kernel(sc_ref, sn_ref, t_ref, j_ref, *, steps, topk, idx_bits):
    keys = jnp.concatenate([_sortable(sc_ref[...]), _new_token_keys(sn_ref[...], steps)], axis=1)
    idx = lax.broadcasted_iota(I32, keys.shape, 1)
    count = lambda fn: jnp.sum(fn(0, keys, idx), axis=1, keepdims=True)
    t, j = _topk_select(count, keys.shape[0], keys.shape[1], float(topk), idx_bits)
    t_ref[...] = t
    j_ref[...] = j


def _dsa_sample_select(scores, snew, steps, topk):
    nb, _, past = scores.shape
    page = snew.shape[-1]
    rows = nb * QP
    rb = math.gcd(rows, 64)
    t, j = pl.pallas_call(
        functools.partial(_dsa_sample_select_kernel, steps=steps, topk=topk,
                          idx_bits=int(math.ceil(math.log2(past + page)))),
        grid=(rows // rb,),
        in_specs=[pl.BlockSpec((rb, past), lambda i: (i, 0)), pl.BlockSpec((rb, page), lambda i: (i, 0))],
        out_specs=[pl.BlockSpec((rb, 1), lambda i: (i, 0)), pl.BlockSpec((rb, 1), lambda i: (i, 0))],
        out_shape=[jax.ShapeDtypeStruct((rows, 1), I32), jax.ShapeDtypeStruct((rows, 1), I32)],
        compiler_params=_cparams(1),
        name="dsa_sample_select",
    )(scores.reshape(rows, past), snew.reshape(rows, page))
    return t.reshape(nb, QP, 1), j.reshape(nb, QP, 1)


def _dsa_sample_attend_kernel(pt_ref, q_ref, kn_ref, vn_ref, sc_ref, sn_ref, t_ref, j_ref,
                              blast_ref, bnew_ref, *rest, pps, n_pages, steps):
    k_refs, v_refs, o_ref = rest[:pps], rest[pps:2 * pps], rest[2 * pps]
    lhs_s, m_s, l_s, acc_s = rest[2 * pps + 1:]
    g = pl.program_id(1)
    last = pl.num_programs(1) - 1
    page = k_refs[0].shape[-1]
    past = n_pages * page
    n = pps * page
    r = N_HEADS // N_KV_HEADS
    grow = r * QP

    @pl.when(g == 0)
    def _():
        m_s[...] = jnp.full(m_s.shape, NEG_BIG, F32)
        l_s[...] = jnp.zeros(l_s.shape, F32)
        acc_s[...] = jnp.zeros(acc_s.shape, F32)
        q = q_ref[0]
        for gg in range(N_KV_HEADS):
            lhs_s[gg] = jnp.concatenate(
                [q[:, (gg * r + u) * HEAD_DIM:(gg * r + u + 1) * HEAD_DIM] for u in range(r)], axis=0).astype(BF16)

    t, j = t_ref[0], j_ref[0]

    def select(key, idx):
        sel = (key > t) | ((key == t) & (idx <= j))
        return jnp.where(sel & (key != INT_MIN), 0.0, NEG_BIG)

    def attend(kt, vt, madd, bias):
        mrows = jnp.concatenate([madd] * r, axis=0)
        for gg in range(N_KV_HEADS):
            rows = slice(gg * grow, (gg + 1) * grow)
            feat = slice(gg * HEAD_DIM, (gg + 1) * HEAD_DIM)
            lg = _dot(lhs_s[gg], kt[feat, :]) + mrows
            if bias is not None:
                lg = lg + bias[rows, :]
            m_old = m_s[rows, :]
            m_new = jnp.maximum(m_old, jnp.max(lg, axis=1, keepdims=True))
            p = jnp.exp(lg - m_new)
            alpha = jnp.exp(m_old - m_new)
            l_s[rows, :] = alpha * l_s[rows, :] + jnp.sum(p, axis=1, keepdims=True)
            acc_s[rows, :] = alpha * acc_s[rows, :] + _dot_nt(p.astype(BF16), vt[feat, :])
            m_s[rows, :] = m_new

    def pages(bias):
        kt = jnp.concatenate([k_refs[u][0, 0] for u in range(pps)], axis=1).astype(BF16)
        vt = jnp.concatenate([v_refs[u][0, 0] for u in range(pps)], axis=1).astype(BF16)
        idx = g * n + lax.broadcasted_iota(I32, (QP, n), 1)
        attend(kt, vt, select(_sortable(sc_ref[0]), idx), bias)

    @pl.when(g != last)
    def _():
        pages(None)

    @pl.when(g == last)
    def _():
        pages(jnp.concatenate([jnp.zeros((N_HEADS * QP, n - page), F32), blast_ref[...]], axis=1))
        idx = past + lax.broadcasted_iota(I32, (QP, page), 1)
        attend(_pad_t(kn_ref[0], page).astype(BF16), _pad_t(vn_ref[0], page).astype(BF16),
               select(_new_token_keys(sn_ref[0], steps), idx), bnew_ref[...])
        o = acc_s[...] / l_s[...]
        for h in range(N_HEADS):
            o_ref[0, :, h * HEAD_DIM:(h + 1) * HEAD_DIM] = o[h * QP:(h + 1) * QP, :]


def _dsa_sample(page_table, q, kn, vn, scores, snew, k_t, v_t, rel_bias, layer, steps, pps):
    nb, n_pages = page_table.shape
    page = k_t.shape[-1]
    past = n_pages * page
    nq, nkv = N_HEADS * HEAD_DIM, N_KV_HEADS * HEAD_DIM
    t, j = _dsa_sample_select(scores, snew, steps, min(TOPK_MAX, (past + steps) // 4))
    n_steps = n_pages // pps
    r = N_HEADS // N_KV_HEADS
    tab = _bias_table(rel_bias, 2 * page)
    qq = np.arange(QP)[:, None]
    off = np.arange(page)[None, :]
    blast = tab[:, page + qq - off].reshape(N_HEADS * QP, page)
    bnew = tab[:, np.maximum(qq - off, 0)].reshape(N_HEADS * QP, page)
    bmap = lambda b, g, pt: (b, 0, 0)
    grid_spec = pltpu.PrefetchScalarGridSpec(
        num_scalar_prefetch=1,
        grid=(nb, n_steps),
        in_specs=[pl.BlockSpec((1, QP, nq), bmap),
                  pl.BlockSpec((1, QP, nkv), bmap),
                  pl.BlockSpec((1, QP, nkv), bmap),
                  pl.BlockSpec((1, QP, pps * page), lambda b, g, pt: (b, 0, g)),
                  pl.BlockSpec((1, QP, page), bmap),
                  pl.BlockSpec((1, QP, 1), bmap), pl.BlockSpec((1, QP, 1), bmap),
                  pl.BlockSpec(blast.shape, lambda b, g, pt: (0, 0)),
                  pl.BlockSpec(bnew.shape, lambda b, g, pt: (0, 0))]
                 + _page_specs(k_t.shape, layer, pps) + _page_specs(v_t.shape, layer, pps),
        out_specs=pl.BlockSpec((1, QP, nq), bmap),
        scratch_shapes=[pltpu.VMEM((N_KV_HEADS, r * QP, HEAD_DIM), BF16),
                        pltpu.VMEM((N_HEADS * QP, 1), F32), pltpu.VMEM((N_HEADS * QP, 1), F32),
                        pltpu.VMEM((N_HEADS * QP, HEAD_DIM), F32)])
    return pl.pallas_call(
        functools.partial(_dsa_sample_attend_kernel, pps=pps, n_pages=n_pages, steps=steps),
        grid_spec=grid_spec,
        out_shape=jax.ShapeDtypeStruct((nb, QP, nq), F32),
        compiler_params=_cparams(2),
        name="dsa_sample_attend",
    )(page_table, q, kn, vn, scores, snew, t, j, blast, bnew, *([k_t] * pps), *([v_t] * pps))


SSD_COLS = 512


def _softplus(x):
    return jnp.maximum(x, 0.0) + jnp.log1p(jnp.exp(-jnp.abs(x)))


def _split3(a):
    p1 = a.astype(BF16)
    r1 = a - p1.astype(F32)
    p2 = r1.astype(BF16)
    return p1, p2, (r1 - p2.astype(F32)).astype(BF16)


def _dot_sel_rhs(a, e):
    p1, p2, p3 = _split3(a)
    return _dot(p1, e) + (_dot(p2, e) + _dot(p3, e))


def _dot_sel_lhs(e, a):
    p1, p2, p3 = _split3(a)
    return _dot(e, p1) + (_dot(e, p2) + _dot(e, p3))


def _ssd_prompt_kernel(z_ref, xbc_ref, dtr_ref, cw_ref, cb_ref, dtb_ref, alog_ref, dexp_ref, ng_ref,
                       e_ref, y_ref, nbuf_ref, hf_ref, prev_s, xc_s, y_s, h_s, *, width, nh, inner):
    i = pl.program_id(1)
    last = pl.num_programs(1) - 1
    qn, cdim = xbc_ref.shape[1], xbc_ref.shape[2]
    ngrp, ns, hd = SSD_GROUPS, SSD_STATE, SSD_HEADDIM
    hpg = nh // ngrp

    @pl.when(i == 0)
    def _():
        prev_s[...] = jnp.zeros(prev_s.shape, F32)
        h_s[...] = jnp.zeros(h_s.shape, F32)

    for cb in range(cdim // SSD_COLS):
        cols = slice(cb * SSD_COLS, (cb + 1) * SSD_COLS)
        xr = xbc_ref[0, :, cols]
        prev = prev_s[:, cols]
        conv = xr * cw_ref[width - 1:width, cols]
        for k in range(1, width):
            conv = conv + _shift_rows(xr, prev, k) * cw_ref[width - 1 - k:width - k, cols]
        xc_s[:, cols] = _silu(conv + cb_ref[:, cols])
        prev_s[:, cols] = xr[qn - SUBLANE:, :]
        nbuf_ref[0, :, cols] = xr[qn - (width - 1):, :]

    hl = lax.broadcasted_iota(I32, (qn, LANE), 1)
    dt = jnp.where(hl < nh, _softplus(dtr_ref[0] + dtb_ref[...]), 0.0)
    dta = dt * (-jnp.exp(alog_ref[...]))
    qrow = lax.broadcasted_iota(I32, (qn, qn), 0)
    kcol = lax.broadcasted_iota(I32, (qn, qn), 1)
    causal = kcol <= qrow
    cum = _dot_sel_lhs(jnp.where(causal, 1.0, 0.0).astype(BF16), dta)
    cum_t = cum.T
    cum_last = cum[qn - 1:qn, :]
    ecum = jnp.exp(cum)
    dtw = dt * jnp.exp(cum_last - cum)
    e_last = jnp.exp(cum_last)
    lane_p = lax.broadcasted_iota(I32, (qn, LANE), 1)
    row_p = lax.broadcasted_iota(I32, (LANE, ns), 0)

    for g in range(ngrp):
        gl = slice(g * SSD_COLS, (g + 1) * SSD_COLS)
        e_g = e_ref[:, gl]
        xs_g = xc_s[:, gl]
        dtx_g = xs_g * _dot_sel_rhs(dt, e_g)
        dtxw_g = xs_g * _dot_sel_rhs(dtw, e_g)
        ecx_g = _dot_sel_rhs(ecum, e_g)
        bm = xc_s[:, inner + g * ns:inner + (g + 1) * ns].astype(BF16)
        cm = xc_s[:, inner + (ngrp + g) * ns:inner + (ngrp + g + 1) * ns].astype(BF16)
        cbm = _dot_nt(cm, bm)
        for pi in range(hpg // 2):
            ha = g * hpg + 2 * pi
            pls = slice(pi * LANE, (pi + 1) * LANE)
            lanes = slice(ha * hd, ha * hd + LANE)
            dtx_p = dtx_g[:, pls].astype(BF16)
            yds = []
            for h in (ha, ha + 1):
                seg = cum[:, h:h + 1] - cum_t[h:h + 1, :]
                dec = jnp.exp(jnp.where(causal, seg, -jnp.inf))
                yds.append(_dot((cbm * dec).astype(BF16), dtx_p))
            yd = jnp.where(lane_p < hd, yds[0], yds[1])
            hp = h_s[ha // 2]
            yo = _dot_nt(cm, hp.astype(BF16)) * ecx_g[:, pls]
            y_s[:, lanes] = yd + yo + dexp_ref[:, lanes] * xs_g[:, pls]
            s_new = _dot(dtxw_g[:, pls].T.astype(BF16), bm)
            cd = jnp.where(row_p < hd, e_last[:, ha:ha + 1], e_last[:, ha + 1:ha + 2])
            h_s[ha // 2] = hp * cd + s_new

    for g in range(ngrp):
        gl = slice(g * SSD_COLS, (g + 1) * SSD_COLS)
        yg = y_s[:, gl] * _silu(z_ref[0, :, gl])
        ms = jnp.mean(yg * yg, axis=-1, keepdims=True)
        y_ref[0, :, gl] = (yg * lax.rsqrt(ms + EPS)) * ng_ref[:, gl]

    @pl.when(i == last)
    def _():
        hf_ref[0] = h_s[...]


def _head_expand(nh, hd):
    e = np.zeros((LANE, nh * hd), np.float32)
    for h in range(nh):
        e[h, h * hd:(h + 1) * hd] = 1.0
    return jnp.asarray(e, BF16)


def _pad_lanes(v, n=LANE):
    v = v.reshape(1, -1)
    return jnp.pad(v, ((0, 0), (0, n - v.shape[1])))


def _ssd_prompt(z, xbc, dtr, conv_w, conv_b, dt_bias, a_log, d_skip, norm_g):
    bsz, seq, inner = z.shape
    cdim = xbc.shape[-1]
    nh = dt_bias.shape[0]
    width = conv_w.shape[0]
    qn = math.gcd(seq, SSD_CHUNK)
    assert qn == SSD_CHUNK and inner == SSD_GROUPS * SSD_COLS and cdim % SSD_COLS == 0
    dexp = jnp.repeat(d_skip, SSD_HEADDIM).reshape(1, inner)
    blk = lambda n: pl.BlockSpec((1, qn, n), lambda b, i: (b, i, 0))
    y, nbuf, hf = pl.pallas_call(
        functools.partial(_ssd_prompt_kernel, width=width, nh=nh, inner=inner),
        grid=(bsz, seq // qn),
        in_specs=[blk(inner), blk(cdim), blk(LANE),
                  _const2((width, cdim)), _const2((1, cdim)), _const2((1, LANE)), _const2((1, LANE)),
                  _const2((1, inner)), _const2((1, inner)), _const2((LANE, inner))],
        out_specs=[blk(inner),
                   pl.BlockSpec((1, width - 1, cdim), lambda b, i: (b, 0, 0)),
                   pl.BlockSpec((1, nh // 2, LANE, SSD_STATE), lambda b, i: (b, 0, 0, 0))],
        out_shape=[jax.ShapeDtypeStruct((bsz, seq, inner), F32),
                   jax.ShapeDtypeStruct((bsz, width - 1, cdim), F32),
                   jax.ShapeDtypeStruct((bsz, nh // 2, LANE, SSD_STATE), F32)],
        scratch_shapes=[pltpu.VMEM((SUBLANE, cdim), F32), pltpu.VMEM((qn, cdim), F32),
                        pltpu.VMEM((qn, inner), F32), pltpu.VMEM((nh // 2, LANE, SSD_STATE), F32)],
        compiler_params=_cparams(2),
        name="ssd_prompt",
    )(z, xbc, dtr, conv_w, conv_b.reshape(1, cdim), _pad_lanes(dt_bias), _pad_lanes(a_log),
      dexp, norm_g.reshape(1, inner), _head_expand(nh, SSD_HEADDIM))
    return y, nbuf, hf.reshape(bsz, nh, SSD_HEADDIM, SSD_STATE)


def _ssd_prep_sample_kernel(xbc_ref, buf_ref, dtr_ref, cw_ref, cb_ref, dtb_ref, xc_ref, dt_ref, nbuf_ref,
                            *, width, nb, steps, nh):
    cdim = xbc_ref.shape[1]
    for cb in range(cdim // SSD_COLS):
        cols = slice(cb * SSD_COLS, (cb + 1) * SSD_COLS)
        ext = jnp.concatenate([buf_ref[:, cols], xbc_ref[:, cols]], axis=0)
        conv = ext[0:steps * nb] * cw_ref[0:1, cols]
        for k in range(1, width):
            conv = conv + ext[k * nb:(k + steps) * nb] * cw_ref[k:k + 1, cols]
        xc_ref[:, cols] = _silu(conv + cb_ref[:, cols])
        nbuf_ref[:, cols] = ext[steps * nb:]
    hl = lax.broadcasted_iota(I32, dtr_ref.shape, 1)
    dt_ref[...] = jnp.where(hl < nh, _softplus(dtr_ref[...] + dtb_ref[...]), 0.0)


def _ssd_prep_sample(xbc_tb, buf_kb, dtr_tb, conv_w, conv_b, dt_bias, nb, steps):
    rows, cdim = xbc_tb.shape
    width = conv_w.shape[0]
    return pl.pallas_call(
        functools.partial(_ssd_prep_sample_kernel, width=width, nb=nb, steps=steps, nh=dt_bias.shape[0]),
        out_shape=[jax.ShapeDtypeStruct((rows, cdim), F32),
                   jax.ShapeDtypeStruct((rows, LANE), F32),
                   jax.ShapeDtypeStruct(((width - 1) * nb, cdim), F32)],
        compiler_params=pltpu.CompilerParams(vmem_limit_bytes=VMEM_LIMIT),
        name="ssd_prep_sample",
    )(xbc_tb, buf_kb, dtr_tb, conv_w, conv_b.reshape(1, cdim), _pad_lanes(dt_bias))


def _ssd_scan_sample_kernel(xc_ref, dt_ref, z_ref, h0_ref, alog_ref, dexp_ref, ng_ref, e_ref,
                            y_ref, hf_ref, *, steps, inner):
    ngrp, ns = SSD_GROUPS, SSD_STATE
    xc, dt = xc_ref[0], dt_ref[0]
    row = lax.broadcasted_iota(I32, (QP, LANE), 0)
    cum = dt * (-jnp.exp(alog_ref[...]))
    for sft in (1, 2, 4):
        cum = cum + jnp.where(row >= sft, pltpu.roll(cum, sft, 0), 0.0)
    cum_last = cum[QP - 1:QP, :]
    parts = [dt, jnp.exp(cum), dt * jnp.exp(cum_last - cum), jnp.broadcast_to(jnp.exp(cum_last), (QP, LANE))]
    for s in range(steps):
        parts.append(jnp.where(row >= s, jnp.exp(cum - cum[s:s + 1, :]), 0.0))
    stack = jnp.concatenate(parts, axis=0)

    for g in range(ngrp):
        gl = slice(g * SSD_COLS, (g + 1) * SSD_COLS)
        ex = _dot_sel_rhs(stack, e_ref[:, gl])
        dt_x, ecum_x, dtw_x, el_x = (ex[k * QP:(k + 1) * QP] for k in range(4))
        xs_g = xc[:, gl]
        dtx = xs_g * dt_x
        bm = xc[:, inner + g * ns:inner + (g + 1) * ns]
        cm = xc[:, inner + (ngrp + g) * ns:inner + (ngrp + g + 1) * ns].astype(BF16)
        bmp = jnp.concatenate([bm, jnp.zeros((LANE - QP, ns), F32)], axis=0).astype(BF16)
        cbm = _dot_nt(cm, bmp)
        yd = jnp.zeros((QP, SSD_COLS), F32)
        for s in range(steps):
            yd = yd + (ex[(4 + s) * QP:(5 + s) * QP] * cbm[:, s:s + 1]) * dtx[s:s + 1, :]
        h0g = h0_ref[0, gl, :]
        yo = _dot_nt(cm, h0g.astype(BF16)) * ecum_x
        y = (yd + yo + dexp_ref[:, gl] * xs_g) * _silu(z_ref[0, :, gl])
        ms = jnp.mean(y * y, axis=-1, keepdims=True)
        y_ref[0, :, gl] = (y * lax.rsqrt(ms + EPS)) * ng_ref[:, gl]
        tm = jnp.concatenate([xs_g * dtw_x, el_x[0:1], jnp.zeros((LANE - QP - 1, SSD_COLS), F32)], axis=0)
        tt = tm.T
        hf_ref[0, gl, :] = h0g * tt[:, QP:QP + 1] + _dot(tt.astype(BF16), bmp)


def _ssd_scan_sample(xc, dt, z, h0, a_log, d_skip, norm_g, steps):
    nb, _, cdim = xc.shape
    inner = z.shape[-1]
    nh = a_log.shape[0]
    rows_h = nh * SSD_HEADDIM
    dexp = jnp.repeat(d_skip, SSD_HEADDIM).reshape(1, inner)
    one = lambda shape: pl.BlockSpec(shape, lambda b: (0,) * len(shape))
    return pl.pallas_call(
        functools.partial(_ssd_scan_sample_kernel, steps=steps, inner=inner),
        grid=(nb,),
        in_specs=[pl.BlockSpec((1, QP, cdim), lambda b: (b, 0, 0)),
                  pl.BlockSpec((1, QP, LANE), lambda b: (b, 0, 0)),
                  pl.BlockSpec((1, QP, inner), lambda b: (b, 0, 0)),
                  pl.BlockSpec((1, rows_h, SSD_STATE), lambda b: (b, 0, 0)),
                  one((1, LANE)), one((1, inner)), one((1, inner)), one((LANE, inner))],
        out_specs=[pl.BlockSpec((1, QP, inner), lambda b: (b, 0, 0)),
                   pl.BlockSpec((1, rows_h, SSD_STATE), lambda b: (b, 0, 0))],
        out_shape=[jax.ShapeDtypeStruct((nb, QP, inner), F32),
                   jax.ShapeDtypeStruct((nb, rows_h, SSD_STATE), F32)],
        compiler_params=_cparams(1),
        name="ssd_scan_sample",
    )(xc, dt, z, h0, _pad_lanes(a_log), dexp, norm_g.reshape(1, inner), _head_expand(nh, SSD_HEADDIM))


TM = 256


def _to_steps(a, nb, steps):
    n = a.shape[-1]
    return a.reshape(nb, steps, n).transpose(1, 0, 2).reshape(steps * nb, n)


def _to_batch(a, nb, steps):
    n = a.shape[-1]
    return a.reshape(steps, nb, n).transpose(1, 0, 2).reshape(1, nb * steps, n)


def _pad_q(a, nb, steps):
    n = a.shape[-1]
    return jnp.pad(a.reshape(nb, steps, n), ((0, 0), (0, QP - steps), (0, 0)))


def kernel(x_prompt, x_sample, cache_k, cache_v, cache_kidx, state_s5_re, state_s5_im, state_sconv, state_ssd, state_ssd_conv, page_table, c_prompt, c_sample, rel_bias, ada_w, ada_b, norm_mix, norm_mlp, norm_final, attn_w_in, attn_w_out, s5_lam_re, s5_lam_im, s5_log_dt, s5_b_re, s5_b_im, s5_c_re, s5_c_im, s5_d, s5_w_glu, sc_w_in, sc_w_conv, sc_w_out, ssd_w_in, ssd_conv_w, ssd_conv_b, ssd_dt_bias, ssd_a_log, ssd_d, ssd_norm, ssd_w_out, mlp_w1, mlp_w2):
    bp, seq, d = x_prompt.shape
    nb, steps, _ = x_sample.shape
    depth = ada_w.shape[0]
    n_mixers = 4
    rs = nb * steps
    tm = min(TM, seq)

    rows = bp + nb
    c_all = jnp.pad(jnp.concatenate([c_prompt, c_sample], axis=0), ((0, (-rows) % SUBLANE), (0, 0)))
    ada = _ada(c_all, ada_w, ada_b)

    xp = x_prompt
    xs = x_sample.reshape(1, rs, d)
    outs = {name: [] for name in ("kp", "vp", "kip", "ks", "vs", "kis", "s5pr", "s5pi", "s5sr", "s5si",
                                  "scp", "scs", "ssdp", "ssdcp", "ssds", "ssdcs")}
    nq, nkv = N_HEADS * HEAD_DIM, N_KV_HEADS * HEAD_DIM
    w1_all, w2_all = mlp_w1.astype(BF16), mlp_w2.astype(BF16)
    for i in range(depth):
        m, j = i % n_mixers, i // n_mixers
        mp = [ada[i, :bp, k * d:(k + 1) * d].reshape(bp, 1, d) for k in range(6)]
        ms = [jnp.repeat(ada[i, bp:bp + nb, k * d:(k + 1) * d], steps, axis=0).reshape(1, rs, d)
              for k in range(6)]
        g_mix = norm_mix[i]
        mix_p = mix_s = None
        if m == 0:
            w_in = attn_w_in[j]
            w_qkv = w_in[:, :nq + 2 * nkv].astype(BF16)
            n_idx = IDX_HEADS * IDX_DIM + LANE
            w_idx = jnp.pad(w_in[:, nq + 2 * nkv:], ((0, 0), (0, n_idx - (w_in.shape[1] - nq - 2 * nkv))))
            w_out = attn_w_out[j].astype(BF16)
            page = cache_k.shape[2]
            q, k, v, kw, qcat, kcat, kaug, vaug, ki = _attn_proj(xp, mp[0], mp[1], g_mix, w_qkv, w_idx,
                                                                 2 * MAX_DISTANCE, True, page)
            o = _dsa_prompt(q, qcat, kw, kcat, kaug, vaug, rel_bias, MAX_DISTANCE)
            mix_p = ("proj", o, mp[2], w_out)
            to_pages = lambda t: jnp.transpose(t.reshape(bp, seq // page, N_KV_HEADS, HEAD_DIM, page), (0, 1, 4, 2, 3))
            outs["kp"].append(to_pages(k))
            outs["vp"].append(to_pages(v))
            outs["kip"].append(jnp.transpose(ki, (0, 1, 3, 2)))
            q, k, v, kw, qcat, kcat, _, _ = _attn_proj(xs, ms[0], ms[1], g_mix, w_qkv, w_idx, rs, False)
            pps = math.gcd(PAGES_PER_STEP, page_table.shape[1])
            qc_p, kw_p = _pad_q(qcat.astype(F32), nb, steps), _pad_q(kw, nb, steps)
            scores, snew = _dsa_sample_scores(page_table, qc_p, kw_p, _pad_q(kcat.astype(F32), nb, steps),
                                              _pages_t(cache_kidx), j, pps)
            o = _dsa_sample(page_table, _pad_q(q, nb, steps), _pad_q(k, nb, steps), _pad_q(v, nb, steps),
                            scores, snew, _pages_t(cache_k), _pages_t(cache_v), rel_bias, j, steps, pps)
            mix_s = ("proj", o[:, :steps].reshape(1, rs, nq), ms[2], w_out)
            outs["ks"].append(k.reshape(nb, steps, N_KV_HEADS, HEAD_DIM))
            outs["vs"].append(v.reshape(nb, steps, N_KV_HEADS, HEAD_DIM))
            outs["kis"].append(kw[..., :IDX_DIM].reshape(nb, steps, IDX_DIM))
        elif m == 1:
            ar, ai, wb, wc = _s5_weights(s5_lam_re[j], s5_lam_im[j], s5_log_dt[j], s5_b_re[j], s5_b_im[j],
                                         s5_c_re[j], s5_c_im[j])
            w_glu = s5_w_glu[j].astype(BF16)
            grp, nst = s5_lam_re.shape[1:]
            z, fr, fi = _s5_prompt(xp, mp[0], mp[1], g_mix, s5_d[j], ar, ai, wb, wc)
            mix_p = ("glu", z, mp[2], w_glu)
            outs["s5pr"].append(fr.reshape(bp, grp, nst))
            outs["s5pi"].append(fi.reshape(bp, grp, nst))
            z, fr, fi = _s5_sample(_to_steps(xs, nb, steps), _to_steps(ms[0], nb, steps),
                                   _to_steps(ms[1], nb, steps), g_mix, s5_d[j], ar, ai, wb, wc,
                                   state_s5_re[j].reshape(nb, grp * nst), state_s5_im[j].reshape(nb, grp * nst),
                                   nb, steps)
            mix_s = ("glu", _to_batch(z, nb, steps), ms[2], w_glu)
            outs["s5sr"].append(fr.reshape(nb, grp, nst))
            outs["s5si"].append(fi.reshape(nb, grp, nst))
        elif m == 2:
            w_in = sc_w_in[j].astype(BF16)
            w_out = sc_w_out[j].astype(BF16)
            width = sc_w_conv.shape[1]
            (p,) = _proj(xp, mp[0], mp[1], g_mix, w_in, (3 * d,), tm)
            xp, nbuf = _sconv_prompt(p, xp, mp[2], jnp.zeros((bp, width - 1, d), F32), sc_w_conv[j], w_out, tm)
            outs["scp"].append(nbuf)
            (p,) = _proj(xs, ms[0], ms[1], g_mix, w_in, (3 * d,), rs)
            buf = state_sconv[j].transpose(1, 0, 2).reshape((width - 1) * nb, d)
            y, nbuf = _sconv_sample(_to_steps(p, nb, steps), _to_steps(xs, nb, steps), _to_steps(ms[2], nb, steps),
                                    buf, sc_w_conv[j], w_out, nb, steps)
            xs = _to_batch(y, nb, steps)
            outs["scs"].append(nbuf.reshape(width - 1, nb, d).transpose(1, 0, 2))
        else:
            inner = ssd_norm.shape[1]
            cdim = ssd_conv_w.shape[2]
            nh = ssd_dt_bias.shape[1]
            width = ssd_conv_w.shape[1]
            w_in = jnp.pad(ssd_w_in[j], ((0, 0), (0, LANE - nh))).astype(BF16)
            w_out = ssd_w_out[j].astype(BF16)
            z, xbc, dtr = _proj(xp, mp[0], mp[1], g_mix, w_in, (inner, cdim, LANE), tm)
            y, nbuf, hf = _ssd_prompt(z, xbc, dtr, ssd_conv_w[j], ssd_conv_b[j], ssd_dt_bias[j], ssd_a_log[j],
                                      ssd_d[j], ssd_norm[j])
            mix_p = ("proj", y, mp[2], w_out)
            outs["ssdp"].append(hf)
            outs["ssdcp"].append(nbuf)
            z, xbc, dtr = _proj(xs, ms[0], ms[1], g_mix, w_in, (inner, cdim, LANE), rs)
            buf = state_ssd_conv[j].transpose(1, 0, 2).reshape((width - 1) * nb, cdim)
            xc, dt, nbuf = _ssd_prep_sample(_to_steps(xbc, nb, steps), buf, _to_steps(dtr, nb, steps),
                                            ssd_conv_w[j], ssd_conv_b[j], ssd_dt_bias[j], nb, steps)
            y, hf = _ssd_scan_sample(_pad_q(_to_batch(xc, nb, steps), nb, steps),
                                     _pad_q(_to_batch(dt, nb, steps), nb, steps), _pad_q(z, nb, steps),
                                     state_ssd[j].reshape(nb, nh * SSD_HEADDIM, SSD_STATE),
                                     ssd_a_log[j], ssd_d[j], ssd_norm[j], steps)
            mix_s = ("proj", y[:, :steps].reshape(1, rs, inner), ms[2], w_out)
            outs["ssds"].append(hf.reshape(nb, nh, SSD_HEADDIM, SSD_STATE))
            outs["ssdcs"].append(nbuf.reshape(width - 1, nb, cdim).transpose(1, 0, 2))
        fin = i == depth - 1
        xp = _mlp(xp, mp[3], mp[4], mp[5], norm_mlp[i], w1_all, w2_all, i, norm_final, fin, tm, mix_p)
        xs = _mlp(xs, ms[3], ms[4], ms[5], norm_mlp[i], w1_all, w2_all, i, norm_final, fin, rs, mix_s)
    st = jnp.stack
    return (xp, xs.reshape(nb, steps, d), st(outs["kp"]), st(outs["vp"]), st(outs["kip"]),
            st(outs["ks"]), st(outs["vs"]), st(outs["kis"]), st(outs["s5pr"]), st(outs["s5pi"]),
            st(outs["s5sr"]), st(outs["s5si"]), st(outs["scp"]), st(outs["scs"]),
            st(outs["ssdp"]), st(outs["ssdcp"]), st(outs["ssds"]), st(outs["ssdcs"]))
```

```python
import functools
import math

import jax
import jax.numpy as jnp
import numpy as np
from jax import lax
from jax.experimental import pallas as pl
from jax.experimental.pallas import tpu as pltpu

F32 = jnp.float32
BF16 = jnp.bfloat16
I32 = jnp.int32

EPS = 1e-6
N_HEADS = 16
HEAD_DIM = 64
N_KV_HEADS = 4
IDX_HEADS = 8
IDX_DIM = 64
TOPK_MAX = 256
N_BUCKETS = 32
MAX_DISTANCE = 128
S5_GROUP = 16
S5_STATE = 64
SSD_HEADDIM = 64
SSD_GROUPS = 4
SSD_STATE = 128
SSD_CHUNK = 128

LANE = 128
SUBLANE = 8
VMEM_LIMIT = 56 * 1024 * 1024
NEG_BIG = -1e30
INT_MIN = -2147483648


def _cparams(n_axes):
    return pltpu.CompilerParams(dimension_semantics=("arbitrary",) * n_axes,
                                vmem_limit_bytes=VMEM_LIMIT)


def _dot(a, b):
    return jnp.dot(a, b, preferred_element_type=F32)


def _dot_nt(a, b):
    return lax.dot_general(a, b, (((1,), (1,)), ((), ())), preferred_element_type=F32)


def _split(a):
    hi = a.astype(BF16)
    lo = (a - hi.astype(F32)).astype(BF16)
    return hi, lo


def _dot3(a, b):
    ah, al = _split(a)
    bh, bl = _split(b)
    return _dot(ah, bh) + (_dot(al, bh) + _dot(ah, bl))


def _rms_mod(x, g, shift, scale):
    y = x * lax.rsqrt(jnp.mean(x * x, axis=-1, keepdims=True) + EPS)
    return (y * g) * (1.0 + scale) + shift


def _sigmoid(x):
    return 1.0 / (1.0 + jnp.exp(-x))


def _silu(x):
    return x * _sigmoid(x)


def _row_block(arr, tm):
    n = arr.shape[-1]
    if arr.shape[1] == 1:
        return pl.BlockSpec((1, 1, n), lambda b, i: (b, 0, 0))
    return pl.BlockSpec((1, tm, n), lambda b, i: (b, i, 0))


def _const2(shape):
    return pl.BlockSpec(shape, lambda b, i: (0,) * len(shape))


def _ada_kernel(c_ref, w_ref, b_ref, o_ref):
    o_ref[0] = _dot3(_silu(c_ref[...]), w_ref[0]) + b_ref[0]


def _ada(c_all, ada_w, ada_b):
    depth, d, n = ada_w.shape
    rows = c_all.shape[0]
    tn = 1536
    return pl.pallas_call(
        _ada_kernel,
        grid=(depth, n // tn),
        in_specs=[pl.BlockSpec((rows, d), lambda l, j: (0, 0)),
                  pl.BlockSpec((1, d, tn), lambda l, j: (l, 0, j)),
                  pl.BlockSpec((1, 1, tn), lambda l, j: (l, 0, j))],
        out_specs=pl.BlockSpec((1, rows, tn), lambda l, j: (l, 0, j)),
        out_shape=jax.ShapeDtypeStruct((depth, rows, n), F32),
        compiler_params=_cparams(2),
        name="ada",
    )(c_all, ada_w, ada_b.reshape(depth, 1, n))


def _proj_kernel(x_ref, sh_ref, sc_ref, g_ref, w_ref, *o_refs, splits):
    h = _rms_mod(x_ref[0], g_ref[...], sh_ref[0], sc_ref[0]).astype(BF16)
    off = 0
    for o_ref, n in zip(o_refs, splits):
        o_ref[0] = _dot(h, w_ref[:, off:off + n])
        off += n


def _proj(x, shift, scale, g, w_bf16, splits, tm):
    bsz, seq, d = x.shape
    n = w_bf16.shape[1]
    assert sum(splits) == n and seq % tm == 0
    return pl.pallas_call(
        functools.partial(_proj_kernel, splits=tuple(splits)),
        grid=(bsz, seq // tm),
        in_specs=[pl.BlockSpec((1, tm, d), lambda b, i: (b, i, 0)),
                  _row_block(shift, tm), _row_block(scale, tm),
                  _const2((1, d)), _const2((d, n))],
        out_specs=[pl.BlockSpec((1, tm, s), lambda b, i: (b, i, 0)) for s in splits],
        out_shape=[jax.ShapeDtypeStruct((bsz, seq, s), F32) for s in splits],
        compiler_params=_cparams(2),
        name="proj",
    )(x, shift, scale, g.reshape(1, d), w_bf16)


def _aug_heads(t, col):
    rows = t.shape[0]
    lane = lax.broadcasted_iota(I32, (rows, LANE - HEAD_DIM), 1)
    extra = jnp.where(lane == 0, col, 0.0).astype(F32)
    parts = []
    for g in range(N_KV_HEADS):
        parts += [t[:, g * HEAD_DIM:(g + 1) * HEAD_DIM], extra]
    return jnp.concatenate(parts, axis=1).astype(BF16)


def _attn_proj_kernel(x_ref, sh_ref, sc_ref, g_ref, w_ref, wi_ref,
                      q_ref, k_ref, v_ref, kw_ref, qcat_ref, kcat_ref, kaug_ref, vaug_ref, *ki_refs,
                      nq, nkv, nqi, key_major):
    (ki_ref,) = ki_refs if key_major else (None,)
    h = _rms_mod(x_ref[0], g_ref[...], sh_ref[0], sc_ref[0])
    hb = h.astype(BF16)
    q_ref[0] = _dot(hb, w_ref[:, 0:nq]) * (HEAD_DIM ** -0.5)
    k = _dot(hb, w_ref[:, nq:nq + nkv])
    v = _dot(hb, w_ref[:, nq + nkv:nq + 2 * nkv])
    r = _dot3(h, wi_ref[...])
    kw = r[:, nqi:nqi + LANE]
    kw_ref[0] = kw
    if key_major:
        page = k_ref.shape[-1]
        kt, vt, kwt = k.T, v.T, kw.T
        for p in range(k_ref.shape[1]):
            k_ref[0, p] = kt[:, p * page:(p + 1) * page]
            v_ref[0, p] = vt[:, p * page:(p + 1) * page]
            ki_ref[0, p] = kwt[0:IDX_DIM, p * page:(p + 1) * page]
    else:
        k_ref[0] = k
        v_ref[0] = v
    zero = jnp.zeros((h.shape[0], IDX_DIM), BF16)
    parts = []
    for hh in range(IDX_HEADS):
        ah, al = _split(r[:, hh * IDX_DIM:(hh + 1) * IDX_DIM])
        parts += [ah, al, ah, zero]
    qcat_ref[0] = jnp.concatenate(parts, axis=1)
    kcat = _idx_rhs(kw[:, 0:IDX_DIM])
    kaug = _aug_heads(k, 0.0)
    vaug_ref[0] = _aug_heads(v, 1.0)
    if key_major:
        kcat_ref[0, 0] = kcat.astype(F32).T.astype(BF16)
        kaug_ref[0, 0] = kaug.astype(F32).T.astype(BF16)
    else:
        kcat_ref[0] = kcat
        kaug_ref[0] = kaug


def _attn_proj(x, shift, scale, g, w_qkv_bf16, w_idx, tm, key_major, page=None):
    bsz, seq, d = x.shape
    nq = N_HEADS * HEAD_DIM
    nkv = N_KV_HEADS * HEAD_DIM
    nqi = IDX_HEADS * IDX_DIM
    rows = lambda n, dt: (pl.BlockSpec((1, tm, n), lambda b, i: (b, i, 0)), jax.ShapeDtypeStruct((bsz, seq, n), dt))
    cols = lambda n, dt: (pl.BlockSpec((1, 1, n, tm), lambda b, i: (b, i, 0, 0)),
                          jax.ShapeDtypeStruct((bsz, seq // tm, n, tm), dt))
    keys = cols if key_major else rows
    if key_major:
        ppt = tm // page
        paged = lambda n: (pl.BlockSpec((1, ppt, n, page), lambda b, i: (b, i, 0, 0)),
                           jax.ShapeDtypeStruct((bsz, seq // page, n, page), F32))
        kv = [paged(nkv), paged(nkv)]
    else:
        kv = [rows(nkv, F32), rows(nkv, F32)]
    outs = [rows(nq, F32)] + kv + [rows(LANE, F32), rows(4 * nqi, BF16),
                                   keys(4 * IDX_DIM, BF16), keys(N_KV_HEADS * LANE, BF16),
                                   rows(N_KV_HEADS * LANE, BF16)]
    if key_major:
        outs.append(paged(IDX_DIM))
    return pl.pallas_call(
        functools.partial(_attn_proj_kernel, nq=nq, nkv=nkv, nqi=nqi, key_major=key_major),
        grid=(bsz, seq // tm),
        in_specs=[pl.BlockSpec((1, tm, d), lambda b, i: (b, i, 0)),
                  _row_block(shift, tm), _row_block(scale, tm),
                  _const2((1, d)), _const2(w_qkv_bf16.shape), _const2(w_idx.shape)],
        out_specs=[o[0] for o in outs],
        out_shape=[o[1] for o in outs],
        compiler_params=_cparams(2),
        name="attn_proj",
    )(x, shift, scale, g.reshape(1, d), w_qkv_bf16, w_idx)


def _mlp_kernel(*refs, ff_chunk, final_norm, mixer_out):
    if mixer_out is None:
        x_ref, sh_ref, sc_ref, gate_ref, g_ref, w1_ref, w2_ref, gf_ref, y_ref = refs
        x = x_ref[0]
    else:
        o_ref, gmix_ref, wo_ref, x_ref, sh_ref, sc_ref, gate_ref, g_ref, w1_ref, w2_ref, gf_ref, y_ref = refs
        d = x_ref.shape[-1]
        ob = o_ref[0].astype(BF16)
        if mixer_out == "glu":
            t = _dot(ob, wo_ref[:, :d]) * _sigmoid(_dot(ob, wo_ref[:, d:]))
        else:
            t = _dot(ob, wo_ref[...])
        x = x_ref[0] + gmix_ref[0] * t
    h = _rms_mod(x, g_ref[...], sh_ref[0], sc_ref[0]).astype(BF16)
    dff = w1_ref.shape[-1]
    acc = jnp.zeros(x.shape, F32)
    for c in range(dff // ff_chunk):
        a = jnp.maximum(_dot(h, w1_ref[0, :, c * ff_chunk:(c + 1) * ff_chunk]), 0.0)
        acc = acc + _dot((a * a).astype(BF16), w2_ref[0, c * ff_chunk:(c + 1) * ff_chunk, :])
    y = x + gate_ref[0] * acc
    if final_norm:
        y = (y * lax.rsqrt(jnp.mean(y * y, axis=-1, keepdims=True) + EPS)) * gf_ref[...]
    y_ref[0] = y


def _resident(shape):
    return pl.BlockSpec(shape, lambda b, i: (0,) * len(shape), pipeline_mode=pl.Buffered(1))


def _mlp(x, shift, scale, gate, g, w1_bf16, w2_bf16, layer, g_final, final_norm, tm, mixer=None):
    bsz, seq, d = x.shape
    dff = w1_bf16.shape[-1]
    layer_w = lambda r, c: pl.BlockSpec((1, r, c), lambda b, i: (layer, 0, 0), pipeline_mode=pl.Buffered(1))
    rows = lambda n: pl.BlockSpec((1, tm, n), lambda b, i: (b, i, 0))
    args, specs, kind = [], [], None
    if mixer is not None:
        kind, o, gmix, wo = mixer
        args += [o, gmix, wo]
        specs += [rows(o.shape[-1]), _row_block(gmix, tm), _resident(wo.shape)]
    args += [x, shift, scale, gate, g.reshape(1, d), w1_bf16, w2_bf16, g_final.reshape(1, d)]
    specs += [rows(d), _row_block(shift, tm), _row_block(scale, tm), _row_block(gate, tm),
              _const2((1, d)), layer_w(d, dff), layer_w(dff, d), _const2((1, d))]
    return pl.pallas_call(
        functools.partial(_mlp_kernel, ff_chunk=1024, final_norm=final_norm, mixer_out=kind),
        grid=(bsz, seq // tm),
        in_specs=specs,
        out_specs=rows(d),
        out_shape=jax.ShapeDtypeStruct((bsz, seq, d), F32),
        compiler_params=_cparams(2),
        name="mlp",
    )(*args)


S5_CH = 128
S5_BLK = (S5_CH // S5_GROUP) * S5_STATE


def _s5_disc_kernel(lr_ref, li_ref, ldt_ref, br_ref, bi_ref, ar_ref, ai_ref, bbr_ref, bbi_ref):
    lr, li = lr_ref[...], li_ref[...]
    dt = jnp.exp(ldt_ref[...])
    mag = jnp.exp(lr * dt)
    ab_re, ab_im = mag * jnp.cos(li * dt), mag * jnp.sin(li * dt)
    den = lr * lr + li * li
    nr = ab_re - 1.0
    f_re = (nr * lr + ab_im * li) / den
    f_im = (ab_im * lr - nr * li) / den
    ar_ref[...] = ab_re
    ai_ref[...] = ab_im
    for c in range(br_ref.shape[0]):
        br, bi = br_ref[c], bi_ref[c]
        bbr_ref[c] = f_re * br - f_im * bi
        bbi_ref[c] = f_re * bi + f_im * br


def _s5_weights(lam_re, lam_im, log_dt, b_re, b_im, c_re, c_im):
    g, p = lam_re.shape
    gc = b_re.shape[-1]
    brt = jnp.moveaxis(b_re, 2, 0)
    bit = jnp.moveaxis(b_im, 2, 0)
    ar, ai, bbr, bbi = pl.pallas_call(
        _s5_disc_kernel,
        out_shape=[jax.ShapeDtypeStruct((g, p), F32), jax.ShapeDtypeStruct((g, p), F32),
                   jax.ShapeDtypeStruct((gc, g, p), F32), jax.ShapeDtypeStruct((gc, g, p), F32)],
        name="s5_disc",
    )(lam_re, lam_im, log_dt.reshape(g, 1), brt, bit)
    nblk = (g * gc) // S5_CH
    gpb = g // nblk
    eye = jnp.eye(gpb, dtype=F32)

    def bd_in(bb):
        t = jnp.moveaxis(bb, 0, 1).reshape(nblk, gpb, gc, p)
        return jnp.einsum('ngcp,gh->ngchp', t, eye).reshape(nblk, gpb * gc, gpb * p)

    def bd_out(cc):
        t = cc.reshape(nblk, gpb, gc, p)
        return jnp.einsum('ngcp,gh->ngphc', t, eye).reshape(nblk, gpb * p, gpb * gc)

    wb = jnp.concatenate([bd_in(bbr), bd_in(bbi)], axis=-1).astype(BF16)
    wc = jnp.concatenate([bd_out(c_re), bd_out(c_im)], axis=1).astype(BF16)
    return ar.reshape(nblk, 1, gpb * p), ai.reshape(nblk, 1, gpb * p), wb, wc


def _gelu_tanh(y):
    return 0.5 * y * (1.0 + jnp.tanh(math.sqrt(2.0 / math.pi) * (y + 0.044715 * (y * y * y))))


def _cmul(ar, ai, xr, xi):
    return ar * xr - ai * xi, ar * xi + ai * xr


def _s5_prompt_kernel(x_ref, sh_ref, sc_ref, g_ref, d_ref, ar_ref, ai_ref, wb_ref, wc_ref,
                      perm_ref, unperm_ref, z_ref, fr_ref, fi_ref, u_s, br_s, bi_s, st_r, st_i,
                      *, nseg, seg):
    i = pl.program_id(1)
    nblk = wb_ref.shape[0]
    blk = ar_ref.shape[-1]

    @pl.when(i == 0)
    def _():
        st_r[...] = jnp.zeros(st_r.shape, F32)
        st_i[...] = jnp.zeros(st_i.shape, F32)

    u_s[...] = _dot_sel_lhs(perm_ref[...], _rms_mod(x_ref[0], g_ref[...], sh_ref[0], sc_ref[0]))

    for c in range(nblk):
        lo, hi = c * S5_CH, (c + 1) * S5_CH
        uc = u_s[:, lo:hi]
        bu = _dot(uc.astype(BF16), wb_ref[c])
        br_s[...] = bu[:, :blk]
        bi_s[...] = bu[:, blk:]
        ar = jnp.broadcast_to(ar_ref[c], (nseg, blk))
        ai = jnp.broadcast_to(ai_ref[c], (nseg, blk))

        def local(j, carry):
            xr, xi = carry
            r0 = pl.multiple_of(j * nseg, nseg)
            pr, pi = _cmul(ar, ai, xr, xi)
            nr = pr + br_s[pl.ds(r0, nseg), :]
            ni = pi + bi_s[pl.ds(r0, nseg), :]
            br_s[pl.ds(r0, nseg), :] = nr
            bi_s[pl.ds(r0, nseg), :] = ni
            return nr, ni

        zero = jnp.zeros((nseg, blk), F32)
        fr, fi = lax.fori_loop(0, seg, local, (zero, zero))

        pr, pi = ar_ref[c], ai_ref[c]
        for _ in range(int(math.log2(seg))):
            pr, pi = _cmul(pr, pi, pr, pi)
        cr, ci = st_r[c], st_i[c]
        rows_r, rows_i = [], []
        for s in range(nseg):
            rows_r.append(cr)
            rows_i.append(ci)
            tr, ti = _cmul(pr, pi, cr, ci)
            cr, ci = tr + fr[s:s + 1], ti + fi[s:s + 1]
        st_r[c] = cr
        st_i[c] = ci
        fr_ref[0, c] = cr
        fi_ref[0, c] = ci
        dr, di = _cmul(ar, ai, jnp.concatenate(rows_r, axis=0), jnp.concatenate(rows_i, axis=0))

        def fix(j, carry):
            dr, di = carry
            r0 = pl.multiple_of(j * nseg, nseg)
            br_s[pl.ds(r0, nseg), :] = br_s[pl.ds(r0, nseg), :] + dr
            bi_s[pl.ds(r0, nseg), :] = bi_s[pl.ds(r0, nseg), :] + di
            return _cmul(ar, ai, dr, di)

        lax.fori_loop(0, seg, fix, (dr, di))

        y = (_dot(br_s[...].astype(BF16), wc_ref[c, :blk, :])
             - _dot(bi_s[...].astype(BF16), wc_ref[c, blk:, :]))
        y = y + d_ref[:, lo:hi] * uc
        u_s[:, lo:hi] = _gelu_tanh(y)

    z_ref[0] = _dot_sel_lhs(unperm_ref[...], u_s[...]).astype(z_ref.dtype)


def _s5_prompt(x, shift, scale, g, d_skip, ar, ai, wb, wc):
    bsz, seq, d = x.shape
    nblk, _, blk = ar.shape
    nseg, seg = SUBLANE, 32
    tm = nseg * seg
    assert seq % tm == 0
    perm = np.zeros((tm, tm), np.float32)
    for s in range(nseg):
        for j in range(seg):
            perm[j * nseg + s, s * seg + j] = 1.0
    unperm = jnp.asarray(perm.T, BF16)
    perm = jnp.asarray(perm, BF16)
    z, fr, fi = pl.pallas_call(
        functools.partial(_s5_prompt_kernel, nseg=nseg, seg=seg),
        grid=(bsz, seq // tm),
        in_specs=[pl.BlockSpec((1, tm, d), lambda b, i: (b, i, 0)),
                  _row_block(shift, tm), _row_block(scale, tm),
                  _const2((1, d)), _const2((1, d)),
                  _const2(ar.shape), _const2(ai.shape), _const2(wb.shape), _const2(wc.shape),
                  _const2((tm, tm)), _const2((tm, tm))],
        out_specs=[pl.BlockSpec((1, tm, d), lambda b, i: (b, i, 0)),
                   pl.BlockSpec((1, nblk, 1, blk), lambda b, i: (b, 0, 0, 0)),
                   pl.BlockSpec((1, nblk, 1, blk), lambda b, i: (b, 0, 0, 0))],
        out_shape=[jax.ShapeDtypeStruct((bsz, seq, d), BF16),
                   jax.ShapeDtypeStruct((bsz, nblk, 1, blk), F32),
                   jax.ShapeDtypeStruct((bsz, nblk, 1, blk), F32)],
        scratch_shapes=[pltpu.VMEM((tm, d), F32), pltpu.VMEM((tm, blk), F32), pltpu.VMEM((tm, blk), F32),
                        pltpu.VMEM((nblk, 1, blk), F32), pltpu.VMEM((nblk, 1, blk), F32)],
        compiler_params=_cparams(2),
        name="s5_prompt",
    )(x, shift, scale, g.reshape(1, d), d_skip.reshape(1, d), ar, ai, wb, wc, perm, unperm)
    return z, fr, fi


def _s5_sample_kernel(x_ref, sh_ref, sc_ref, g_ref, d_ref, ar_ref, ai_ref, wb_ref, wc_ref,
                      s0r_ref, s0i_ref, z_ref, fr_ref, fi_ref, *, nb, steps):
    nblk = wb_ref.shape[0]
    blk = ar_ref.shape[-1]
    u = _rms_mod(x_ref[...], g_ref[...], sh_ref[...], sc_ref[...])
    for c in range(nblk):
        lo, hi = c * S5_CH, (c + 1) * S5_CH
        uc = u[:, lo:hi]
        bu = _dot(uc.astype(BF16), wb_ref[c])
        ar = jnp.broadcast_to(ar_ref[c], (nb, blk))
        ai = jnp.broadcast_to(ai_ref[c], (nb, blk))
        xr, xi = s0r_ref[:, c * blk:(c + 1) * blk], s0i_ref[:, c * blk:(c + 1) * blk]
        xrs, xis = [], []
        for t in range(steps):
            pr, pi = _cmul(ar, ai, xr, xi)
            xr = pr + bu[t * nb:(t + 1) * nb, :blk]
            xi = pi + bu[t * nb:(t + 1) * nb, blk:]
            xrs.append(xr)
            xis.append(xi)
        fr_ref[:, c * blk:(c + 1) * blk] = xr
        fi_ref[:, c * blk:(c + 1) * blk] = xi
        y = (_dot(jnp.concatenate(xrs, axis=0).astype(BF16), wc_ref[c, :blk, :])
             - _dot(jnp.concatenate(xis, axis=0).astype(BF16), wc_ref[c, blk:, :]))
        y = y + d_ref[:, lo:hi] * uc
        z_ref[:, lo:hi] = _gelu_tanh(y)


def _s5_sample(x_tb, shift_tb, scale_tb, g, d_skip, ar, ai, wb, wc, s0r, s0i, nb, steps):
    rows, d = x_tb.shape
    nblk, _, blk = ar.shape
    return pl.pallas_call(
        functools.partial(_s5_sample_kernel, nb=nb, steps=steps),
        out_shape=[jax.ShapeDtypeStruct((rows, d), F32),
                   jax.ShapeDtypeStruct((nb, nblk * blk), F32),
                   jax.ShapeDtypeStruct((nb, nblk * blk), F32)],
        compiler_params=pltpu.CompilerParams(vmem_limit_bytes=VMEM_LIMIT),
        name="s5_sample",
    )(x_tb, shift_tb, scale_tb, g.reshape(1, d), d_skip.reshape(1, d), ar, ai, wb, wc, s0r, s0i)


def _shift_rows(u, prev, k):
    if k == 0:
        return u
    rolled = pltpu.roll(u, k, 0)
    head = pltpu.roll(prev, k, 0)
    row = lax.broadcasted_iota(I32, (SUBLANE, u.shape[1]), 0)
    first = jnp.where(row < k, head, rolled[:SUBLANE])
    return jnp.concatenate([first, rolled[SUBLANE:]], axis=0)


def _sconv_prompt_kernel(x_ref, sh_ref, sc_ref, g_ref, wi_ref, gate_ref, buf_ref, wc_ref, wo_ref,
                         y_ref, nb_ref, prev_s, *, width):
    i = pl.program_id(1)
    d = x_ref.shape[-1]

    @pl.when(i == 0)
    def _():
        prev_s[...] = jnp.zeros(prev_s.shape, F32)
        prev_s[SUBLANE - (width - 1):, :] = buf_ref[0]

    h = _rms_mod(x_ref[0], g_ref[...], sh_ref[0], sc_ref[0]).astype(BF16)
    gb = _dot(h, wi_ref[:, :d])
    u = _dot(h, wi_ref[:, d:2 * d]) * _dot(h, wi_ref[:, 2 * d:])
    prev = prev_s[...]
    conv = u * wc_ref[width - 1:width, :]
    for k in range(1, width):
        conv = conv + _shift_rows(u, prev, k) * wc_ref[width - 1 - k:width - k, :]
    prev_s[...] = u[u.shape[0] - SUBLANE:, :]
    nb_ref[0] = u[u.shape[0] - (width - 1):, :]
    y_ref[0] = x_ref[0] + gate_ref[0] * _dot((gb * conv).astype(BF16), wo_ref[...])


def _sconv_prompt(x, shift, scale, g, w_in_bf16, gate, buf, w_conv, w_out_bf16, tm):
    bsz, seq, d = x.shape
    width = w_conv.shape[0]
    return pl.pallas_call(
        functools.partial(_sconv_prompt_kernel, width=width),
        grid=(bsz, seq // tm),
        in_specs=[pl.BlockSpec((1, tm, d), lambda b, i: (b, i, 0)),
                  _row_block(shift, tm), _row_block(scale, tm), _const2((1, d)),
                  _resident(w_in_bf16.shape), _row_block(gate, tm),
                  pl.BlockSpec((1, width - 1, d), lambda b, i: (b, 0, 0)),
                  _const2((width, d)), _resident((d, d))],
        out_specs=[pl.BlockSpec((1, tm, d), lambda b, i: (b, i, 0)),
                   pl.BlockSpec((1, width - 1, d), lambda b, i: (b, 0, 0))],
        out_shape=[jax.ShapeDtypeStruct((bsz, seq, d), F32),
                   jax.ShapeDtypeStruct((bsz, width - 1, d), F32)],
        scratch_shapes=[pltpu.VMEM((SUBLANE, d), F32)],
        compiler_params=_cparams(2),
        name="sconv_prompt",
    )(x, shift, scale, g.reshape(1, d), w_in_bf16, gate, buf, w_conv, w_out_bf16)


def _sconv_sample_kernel(p_ref, x_ref, gate_ref, buf_ref, wc_ref, wo_ref, y_ref, nb_ref,
                         *, width, nb, steps):
    d = x_ref.shape[-1]
    p = p_ref[...]
    gb, gc, xh = p[:, :d], p[:, d:2 * d], p[:, 2 * d:]
    u = gc * xh
    ext = jnp.concatenate([buf_ref[...], u], axis=0)
    conv = ext[0:steps * nb] * wc_ref[0:1, :]
    for k in range(1, width):
        conv = conv + ext[k * nb:(k + steps) * nb] * wc_ref[k:k + 1, :]
    nb_ref[...] = ext[steps * nb:]
    y_ref[...] = x_ref[...] + gate_ref[...] * _dot((gb * conv).astype(BF16), wo_ref[...])


def _sconv_sample(p_tb, x_tb, gate_tb, buf_kb, w_conv, w_out_bf16, nb, steps):
    rows, d = x_tb.shape
    width = w_conv.shape[0]
    return pl.pallas_call(
        functools.partial(_sconv_sample_kernel, width=width, nb=nb, steps=steps),
        out_shape=[jax.ShapeDtypeStruct((rows, d), F32),
                   jax.ShapeDtypeStruct(((width - 1) * nb, d), F32)],
        compiler_params=pltpu.CompilerParams(vmem_limit_bytes=VMEM_LIMIT),
        name="sconv_sample",
    )(p_tb, x_tb, gate_tb, buf_kb, w_conv, w_out_bf16)


KEY_NEG_INF = -2139095041
KEY_POS_INF = 2139095040


def _code_to_f32(key):
    key = jnp.clip(key, KEY_NEG_INF, KEY_POS_INF)
    return lax.bitcast_convert_type(key ^ ((key >> 31) & 0x7FFFFFFF), F32)


def _stack_heads(qcat):
    w = 4 * IDX_DIM
    return jnp.concatenate([qcat[:, h * w:(h + 1) * w] for h in range(IDX_HEADS)], axis=0)


def _idx_rhs(ki):
    kh, kl = _split(ki)
    return jnp.concatenate([kh, kh, kl, jnp.zeros(kh.shape, BF16)], axis=1)


def _idx_score(dts, wcols, nrows):
    s = wcols[0] * jnp.maximum(dts[0:nrows], 0.0)
    for h in range(1, IDX_HEADS):
        s = s + wcols[h] * jnp.maximum(dts[h * nrows:(h + 1) * nrows], 0.0)
    return s


def _idx_wcols(kw):
    sc = (IDX_HEADS ** -0.5) * (IDX_DIM ** -0.5)
    return [kw[:, IDX_DIM + h:IDX_DIM + h + 1] * sc for h in range(IDX_HEADS)]


def _topk_select(count, rows, lanes, topk, idx_bits, stash=lambda v: (lambda: v)):
    wide = lambda v: jnp.broadcast_to(v, (rows, lanes))

    def bit_step(it, tu):
        mask = jnp.left_shift(jnp.int32(1), 31 - it)
        cand_u = tu | mask
        cand = stash(wide(_code_to_f32(cand_u ^ INT_MIN)))
        cnt = count(lambda c, sc, idx: jnp.where(sc >= cand(), 1.0, 0.0))
        return jnp.where(cnt >= topk, cand_u, tu)

    t = _code_to_f32(lax.fori_loop(0, 32, bit_step, jnp.zeros((rows, 1), I32)) ^ INT_MIN)
    tw = wide(t)
    n_gt = count(lambda c, sc, idx: jnp.where(sc > tw, 1.0, 0.0))
    n_ge = count(lambda c, sc, idx: jnp.where(sc >= tw, 1.0, 0.0))
    need = topk - n_gt
    tied = jnp.max(jnp.where(t > -jnp.inf, n_ge - topk, 0.0)) > 0.0

    def tie_search(_):
        def idx_step(it, j):
            cand = j | jnp.left_shift(jnp.int32(1), idx_bits - 1 - it)
            cw = wide(cand)
            cnt = count(lambda c, sc, idx: jnp.where((sc == tw) & (idx < cw), 1.0, 0.0))
            return jnp.where(cnt < need, cand, j)
        return lax.fori_loop(0, idx_bits, idx_step, jnp.zeros((rows, 1), I32))

    j = lax.cond(tied, tie_search, lambda _: jnp.full((rows, 1), 2 ** idx_bits, I32), 0)
    return t, j


def _dsa_prompt_kernel(q_ref, qcat_ref, kwq_ref, kcat_ref, kaug_ref, vaug_ref, bias_ref, o_ref,
                       sc_s, madd_s, wbc_s, cand_s, qa_s, lg_s, mx_s, sh_s, acc_s, *, tq, topk, idx_bits):
    i = pl.program_id(1)
    nk = i + 1
    nk2 = (nk + 1) // 2
    r = N_HEADS // N_KV_HEADS

    for h, w in enumerate(_idx_wcols(kwq_ref[0])):
        wbc_s[h] = jnp.broadcast_to(w, (tq, tq))
    qrow = lax.broadcasted_iota(I32, (tq, tq), 0)
    kcol = lax.broadcasted_iota(I32, (tq, tq), 1)
    qpos = i * tq + qrow
    wq = 4 * IDX_DIM

    def score_pair(c2, _):
        kc = kcat_ref[0, c2]
        s = None
        for hp in range(IDX_HEADS // 2):
            lhs = jnp.concatenate([qcat_ref[0, :, (2 * hp + v) * wq:(2 * hp + v + 1) * wq] for v in range(2)], axis=0)
            dd = jnp.maximum(_dot(lhs, kc), 0.0)
            for v in range(2):
                w = wbc_s[2 * hp + v]
                d = jnp.concatenate([w, w], axis=1) * dd[v * tq:(v + 1) * tq]
                s = d if s is None else s + d
        for u in range(2):
            c = 2 * c2 + u
            sc_s[c] = jnp.where(c * tq + kcol <= qpos, s[:, u * tq:(u + 1) * tq], -jnp.inf)
        return 0

    lax.fori_loop(0, nk2, score_pair, 0)

    def count(fn):
        def body(c2, acc):
            c = 2 * c2
            acc = acc + fn(c, sc_s[c], c * tq + kcol)
            return acc + fn(c + 1, sc_s[c + 1], (c + 1) * tq + kcol)
        acc = lax.fori_loop(0, nk2, body, jnp.zeros((tq, tq), F32))
        return jnp.sum(acc, axis=1, keepdims=True)

    def stash(v):
        cand_s[...] = v
        return lambda: cand_s[...]

    t, j = _topk_select(count, tq, tq, float(topk), idx_bits, stash)
    tw = jnp.broadcast_to(t, (tq, tq))
    jw = jnp.broadcast_to(j, (tq, tq))

    def mask_pair(c2, _):
        for u in range(2):
            c = 2 * c2 + u
            sc = sc_s[c]
            sel = (sc > tw) | ((sc == tw) & (c * tq + kcol <= jw))
            madd_s[c] = jnp.where(sel & (sc > -jnp.inf), 0.0, NEG_BIG)
        return 0

    lax.fori_loop(0, nk2, mask_pair, 0)

    q = q_ref[0]
    zpad = jnp.zeros((tq, LANE - HEAD_DIM), F32)
    for g in range(N_KV_HEADS):
        qa_s[g] = jnp.concatenate(
            [jnp.concatenate([q[:, (g * r + u) * HEAD_DIM:(g * r + u + 1) * HEAD_DIM], zpad], axis=1)
             for u in range(r)], axis=0).astype(BF16)

    def pair_mask(c2, near, g):
        halves = []
        for u in range(2):
            c = 2 * c2 + u
            m = madd_s[c]
            if near:
                back = nk - 1 - c
                carries = (back == 0) | ((back == 1) & (kcol > qrow))
                halves.append([m + jnp.where(carries, bias_ref[g * r + v], 0.0) for v in range(r)])
            else:
                halves.append([m] * r)
        return jnp.concatenate([jnp.concatenate([halves[0][v], halves[1][v]], axis=1) for v in range(r)], axis=0)

    ngh = lg_s.shape[0]
    for half in range(N_KV_HEADS // ngh):
        groups = [half * ngh + k for k in range(ngh)]
        mx_s[...] = jnp.full(mx_s.shape, NEG_BIG, F32)

        def logits_pair(c2, near):
            for k, g in enumerate(groups):
                lg = _dot(qa_s[g], kaug_ref[0, c2, g * LANE:(g + 1) * LANE, :]) + pair_mask(c2, near, g)
                lg_s[k, c2] = lg
                mx_s[k] = jnp.maximum(mx_s[k], jnp.maximum(lg[:, :tq], lg[:, tq:]))

        def far(c4, _):
            logits_pair(2 * c4, False)
            logits_pair(2 * c4 + 1, False)
            return 0

        nfar = jnp.maximum(nk2 - 2, 0)
        lax.fori_loop(0, nfar // 2, far, 0)

        @pl.when(nfar % 2 == 1)
        def _():
            logits_pair(nfar - 1, False)

        @pl.when(nk2 >= 2)
        def _():
            logits_pair(nk2 - 2, True)

        logits_pair(nk2 - 1, True)

        for k in range(ngh):
            sh_s[k] = jnp.broadcast_to(jnp.max(mx_s[k], axis=1, keepdims=True), (r * tq, LANE))
        acc_s[...] = jnp.zeros(acc_s.shape, F32)

        def pv_pair(c2):
            r0 = pl.multiple_of(c2 * 2 * tq, 2 * tq)
            va = vaug_ref[0, pl.ds(r0, 2 * tq), :]
            for k, g in enumerate(groups):
                s = sh_s[k]
                p = jnp.exp(lg_s[k, c2] - jnp.concatenate([s, s], axis=1))
                acc_s[k] = acc_s[k] + _dot(p.astype(BF16), va[:, g * LANE:(g + 1) * LANE])

        def pv_two(c4, _):
            pv_pair(2 * c4)
            pv_pair(2 * c4 + 1)
            return 0

        lax.fori_loop(0, nk2 // 2, pv_two, 0)

        @pl.when(nk2 % 2 == 1)
        def _():
            pv_pair(nk2 - 1)
        for k, g in enumerate(groups):
            acc = acc_s[k]
            o = acc[:, 0:HEAD_DIM] / acc[:, HEAD_DIM:HEAD_DIM + 1]
            heads = jnp.concatenate([o[u * tq:(u + 1) * tq, :] for u in range(r)], axis=1)
            o_ref[0, :, g * r * HEAD_DIM:(g + 1) * r * HEAD_DIM] = heads.astype(o_ref.dtype)


def _t5_bucket(dist):
    n = jnp.maximum(dist, 0)
    max_exact = N_BUCKETS // 2
    nf = jnp.maximum(n, max_exact).astype(F32)
    large = max_exact + (jnp.log(nf / max_exact) / math.log(MAX_DISTANCE / max_exact)
                         * (N_BUCKETS - max_exact)).astype(I32)
    large = jnp.minimum(large, N_BUCKETS - 1)
    return jnp.where(n < max_exact, n, large)


def _bias_table(rel_bias, n):
    tab = rel_bias[_t5_bucket(jnp.arange(n, dtype=I32))]
    return (tab - rel_bias[N_BUCKETS - 1][None, :]).T


DSA_GROUPS_PER_PASS = 2


def _dsa_prompt(q, qcat, kw, kcat, kaug, vaug, rel_bias, tq):
    bsz, seq, nq = q.shape
    topk = min(TOPK_MAX, seq // 4)
    nkc = seq // tq
    r = N_HEADS // N_KV_HEADS
    assert tq == MAX_DISTANCE and seq % (2 * tq) == 0
    tab = _bias_table(rel_bias, tq)
    dmod = (np.arange(tq)[:, None] - np.arange(tq)[None, :]) % tq
    onehot = (jnp.arange(tq, dtype=I32)[:, None, None] == jnp.asarray(dmod, I32)[None]).astype(F32)
    bias = jnp.einsum('hd,dqk->hqk', tab, onehot, precision=lax.Precision.HIGHEST)
    assert kcat.shape[1:] == (seq // (2 * tq), 4 * IDX_DIM, 2 * tq)
    row = lambda n: pl.BlockSpec((1, tq, n), lambda b, i: (b, i, 0))
    tiles = lambda a: pl.BlockSpec((1,) + a.shape[1:], lambda b, i: (b, 0, 0, 0), pipeline_mode=pl.Buffered(1))
    return pl.pallas_call(
        functools.partial(_dsa_prompt_kernel, tq=tq, topk=topk, idx_bits=int(math.log2(seq))),
        grid=(bsz, nkc),
        in_specs=[row(nq), row(qcat.shape[-1]), row(LANE), tiles(kcat), tiles(kaug),
                  pl.BlockSpec((1, seq, vaug.shape[-1]), lambda b, i: (b, 0, 0), pipeline_mode=pl.Buffered(1)),
                  _resident((N_HEADS, tq, tq))],
        out_specs=row(nq),
        out_shape=jax.ShapeDtypeStruct((bsz, seq, nq), BF16),
        scratch_shapes=[pltpu.VMEM((nkc, tq, tq), F32), pltpu.VMEM((nkc, tq, tq), F32),
                        pltpu.VMEM((IDX_HEADS, tq, tq), F32), pltpu.VMEM((tq, tq), F32),
                        pltpu.VMEM((N_KV_HEADS, r * tq, LANE), BF16),
                        pltpu.VMEM((DSA_GROUPS_PER_PASS, nkc // 2, r * tq, 2 * tq), F32),
                        pltpu.VMEM((DSA_GROUPS_PER_PASS, r * tq, tq), F32),
                        pltpu.VMEM((DSA_GROUPS_PER_PASS, r * tq, LANE), F32),
                        pltpu.VMEM((DSA_GROUPS_PER_PASS, r * tq, LANE), F32)],
        compiler_params=_cparams(2),
        name="dsa_prompt",
    )(q, qcat, kw, kcat, kaug, vaug, bias)


QP = SUBLANE
PAGES_PER_STEP = 16


def _page_specs(shape4, layer, pps):
    def spec(u):
        return pl.BlockSpec((1, 1) + tuple(shape4[2:]),
                            lambda b, g, pt: (layer, pt[b, g * pps + u], 0, 0))
    return [spec(u) for u in range(pps)]


def _pages_t(cache):
    nd = cache.ndim
    t = jnp.transpose(cache, (0, 1) + tuple(range(3, nd)) + (2,))
    return t.reshape(t.shape[0], t.shape[1], -1, t.shape[-1])


def _pad_t(a, page):
    return jnp.concatenate([a, jnp.zeros((page - QP, a.shape[1]), F32)], axis=0).T


def _dsa_sample_score_kernel(pt_ref, qcat_ref, kw_ref, kcn_ref, *rest, pps):
    page_refs, o_ref, on_ref = rest[:pps], rest[pps], rest[pps + 1]
    lhs = _stack_heads(qcat_ref[0]).astype(BF16)
    wcols = _idx_wcols(kw_ref[0])
    kt = jnp.concatenate([page_refs[u][0, 0] for u in range(pps)], axis=1)
    kh, kl = _split(kt)
    rhs = jnp.concatenate([kh, kh, kl, jnp.zeros(kh.shape, BF16)], axis=0)
    o_ref[0] = _idx_score(_dot(lhs, rhs), wcols, QP)

    @pl.when(pl.program_id(1) == pl.num_programs(1) - 1)
    def _():
        on_ref[0] = _idx_score(_dot(lhs, _pad_t(kcn_ref[0], on_ref.shape[-1]).astype(BF16)), wcols, QP)


def _dsa_sample_scores(page_table, qcat, kw, kcn, kidx_t, layer, pps):
    nb, n_pages = page_table.shape
    page = kidx_t.shape[-1]
    bmap = lambda b, g, pt: (b, 0, 0)
    grid_spec = pltpu.PrefetchScalarGridSpec(
        num_scalar_prefetch=1,
        grid=(nb, n_pages // pps),
        in_specs=[pl.BlockSpec((1, QP, qcat.shape[-1]), bmap), pl.BlockSpec((1, QP, LANE), bmap),
                  pl.BlockSpec((1, QP, kcn.shape[-1]), bmap)] + _page_specs(kidx_t.shape, layer, pps),
        out_specs=[pl.BlockSpec((1, QP, pps * page), lambda b, g, pt: (b, 0, g)),
                   pl.BlockSpec((1, QP, page), bmap)])
    return pl.pallas_call(
        functools.partial(_dsa_sample_score_kernel, pps=pps),
        grid_spec=grid_spec,
        out_shape=[jax.ShapeDtypeStruct((nb, QP, n_pages * page), F32),
                   jax.ShapeDtypeStruct((nb, QP, page), F32)],
        compiler_params=_cparams(2),
        name="dsa_sample_scores",
    )(page_table, qcat, kw, kcn, *([kidx_t] * pps))


def _new_token_scores(s_new, steps):
    lane = lax.broadcasted_iota(I32, s_new.shape, 1)
    q = lax.broadcasted_iota(I32, s_new.shape, 0) % QP
    return jnp.where((lane <= q) & (lane < steps), s_new, -jnp.inf)


def _dsa_sample_select_kernel(sc_ref, sn_ref, t_ref, j_ref, *, steps, topk, idx_bits):
    scores = jnp.concatenate([sc_ref[...], _new_token_scores(sn_ref[...], steps)], axis=1)
    idx = lax.broadcasted_iota(I32, scores.shape, 1)
    count = lambda fn: jnp.sum(fn(0, scores, idx), axis=1, keepdims=True)
    t, j = _topk_select(count, scores.shape[0], scores.shape[1], float(topk), idx_bits)
    t_ref[...] = t
    j_ref[...] = j


def _dsa_sample_select(scores, snew, steps, topk):
    nb, _, past = scores.shape
    page = snew.shape[-1]
    rows = nb * QP
    rb = math.gcd(rows, 64)
    t, j = pl.pallas_call(
        functools.partial(_dsa_sample_select_kernel, steps=steps, topk=topk,
                          idx_bits=int(math.ceil(math.log2(past + page)))),
        grid=(rows // rb,),
        in_specs=[pl.BlockSpec((rb, past), lambda i: (i, 0)), pl.BlockSpec((rb, page), lambda i: (i, 0))],
        out_specs=[pl.BlockSpec((rb, 1), lambda i: (i, 0)), pl.BlockSpec((rb, 1), lambda i: (i, 0))],
        out_shape=[jax.ShapeDtypeStruct((rows, 1), F32), jax.ShapeDtypeStruct((rows, 1), I32)],
        compiler_params=_cparams(1),
        name="dsa_sample_select",
    )(scores.reshape(rows, past), snew.reshape(rows, page))
    return t.reshape(nb, QP, 1), j.reshape(nb, QP, 1)


def _dsa_sample_attend_kernel(pt_ref, q_ref, kn_ref, vn_ref, sc_ref, sn_ref, t_ref, j_ref,
                              blast_ref, bnew_ref, *rest, pps, n_pages, steps):
    k_refs, v_refs, o_ref = rest[:pps], rest[pps:2 * pps], rest[2 * pps]
    lhs_s, m_s, l_s, acc_s = rest[2 * pps + 1:]
    g = pl.program_id(1)
    last = pl.num_programs(1) - 1
    page = k_refs[0].shape[-1]
    past = n_pages * page
    n = pps * page
    r = N_HEADS // N_KV_HEADS
    grow = r * QP

    @pl.when(g == 0)
    def _():
        m_s[...] = jnp.full(m_s.shape, NEG_BIG, F32)
        l_s[...] = jnp.zeros(l_s.shape, F32)
        acc_s[...] = jnp.zeros(acc_s.shape, F32)
        q = q_ref[0]
        for gg in range(N_KV_HEADS):
            lhs_s[gg] = jnp.concatenate(
                [q[:, (gg * r + u) * HEAD_DIM:(gg * r + u + 1) * HEAD_DIM] for u in range(r)], axis=0).astype(BF16)

    t, j = t_ref[0], j_ref[0]

    def select(sc, idx):
        sel = (sc > t) | ((sc == t) & (idx <= j))
        return jnp.where(sel & (sc > -jnp.inf), 0.0, NEG_BIG)

    def attend(kt, vt, madd, bias):
        mrows = jnp.concatenate([madd] * r, axis=0)
        for gg in range(N_KV_HEADS):
            rows = slice(gg * grow, (gg + 1) * grow)
            feat = slice(gg * HEAD_DIM, (gg + 1) * HEAD_DIM)
            lg = _dot(lhs_s[gg], kt[feat, :]) + mrows
            if bias is not None:
                lg = lg + bias[rows, :]
            m_old = m_s[rows, :]
            m_new = jnp.maximum(m_old, jnp.max(lg, axis=1, keepdims=True))
            p = jnp.exp(lg - m_new)
            alpha = jnp.exp(m_old - m_new)
            l_s[rows, :] = alpha * l_s[rows, :] + jnp.sum(p, axis=1, keepdims=True)
            acc_s[rows, :] = alpha * acc_s[rows, :] + _dot_nt(p.astype(BF16), vt[feat, :])
            m_s[rows, :] = m_new

    def pages(bias):
        kt = jnp.concatenate([k_refs[u][0, 0] for u in range(pps)], axis=1).astype(BF16)
        vt = jnp.concatenate([v_refs[u][0, 0] for u in range(pps)], axis=1).astype(BF16)
        idx = g * n + lax.broadcasted_iota(I32, (QP, n), 1)
        attend(kt, vt, select(sc_ref[0], idx), bias)

    @pl.when(g != last)
    def _():
        pages(None)

    @pl.when(g == last)
    def _():
        pages(jnp.concatenate([jnp.zeros((N_HEADS * QP, n - page), F32), blast_ref[...]], axis=1))
        idx = past + lax.broadcasted_iota(I32, (QP, page), 1)
        attend(_pad_t(kn_ref[0], page).astype(BF16), _pad_t(vn_ref[0], page).astype(BF16),
               select(_new_token_scores(sn_ref[0], steps), idx), bnew_ref[...])
        o = acc_s[...] / l_s[...]
        for h in range(N_HEADS):
            o_ref[0, :, h * HEAD_DIM:(h + 1) * HEAD_DIM] = o[h * QP:(h + 1) * QP, :]


def _dsa_sample(page_table, q, kn, vn, scores, snew, k_t, v_t, rel_bias, layer, steps, pps):
    nb, n_pages = page_table.shape
    page = k_t.shape[-1]
    past = n_pages * page
    nq, nkv = N_HEADS * HEAD_DIM, N_KV_HEADS * HEAD_DIM
    t, j = _dsa_sample_select(scores, snew, steps, min(TOPK_MAX, (past + steps) // 4))
    n_steps = n_pages // pps
    r = N_HEADS // N_KV_HEADS
    tab = _bias_table(rel_bias, 2 * page)
    qq = np.arange(QP)[:, None]
    off = np.arange(page)[None, :]
    blast = tab[:, page + qq - off].reshape(N_HEADS * QP, page)
    bnew = tab[:, np.maximum(qq - off, 0)].reshape(N_HEADS * QP, page)
    bmap = lambda b, g, pt: (b, 0, 0)
    grid_spec = pltpu.PrefetchScalarGridSpec(
        num_scalar_prefetch=1,
        grid=(nb, n_steps),
        in_specs=[pl.BlockSpec((1, QP, nq), bmap),
                  pl.BlockSpec((1, QP, nkv), bmap),
                  pl.BlockSpec((1, QP, nkv), bmap),
                  pl.BlockSpec((1, QP, pps * page), lambda b, g, pt: (b, 0, g)),
                  pl.BlockSpec((1, QP, page), bmap),
                  pl.BlockSpec((1, QP, 1), bmap), pl.BlockSpec((1, QP, 1), bmap),
                  pl.BlockSpec(blast.shape, lambda b, g, pt: (0, 0)),
                  pl.BlockSpec(bnew.shape, lambda b, g, pt: (0, 0))]
                 + _page_specs(k_t.shape, layer, pps) + _page_specs(v_t.shape, layer, pps),
        out_specs=pl.BlockSpec((1, QP, nq), bmap),
        scratch_shapes=[pltpu.VMEM((N_KV_HEADS, r * QP, HEAD_DIM), BF16),
                        pltpu.VMEM((N_HEADS * QP, 1), F32), pltpu.VMEM((N_HEADS * QP, 1), F32),
                        pltpu.VMEM((N_HEADS * QP, HEAD_DIM), F32)])
    return pl.pallas_call(
        functools.partial(_dsa_sample_attend_kernel, pps=pps, n_pages=n_pages, steps=steps),
        grid_spec=grid_spec,
        out_shape=jax.ShapeDtypeStruct((nb, QP, nq), F32),
        compiler_params=_cparams(2),
        name="dsa_sample_attend",
    )(page_table, q, kn, vn, scores, snew, t, j, blast, bnew, *([k_t] * pps), *([v_t] * pps))


SSD_COLS = 512


def _softplus(x):
    return jnp.maximum(x, 0.0) + jnp.log1p(jnp.exp(-jnp.abs(x)))


def _split3(a):
    p1 = a.astype(BF16)
    r1 = a - p1.astype(F32)
    p2 = r1.astype(BF16)
    return p1, p2, (r1 - p2.astype(F32)).astype(BF16)


def _dot_sel_rhs(a, e):
    p1, p2, p3 = _split3(a)
    return _dot(p1, e) + (_dot(p2, e) + _dot(p3, e))


def _dot_sel_lhs(e, a):
    p1, p2, p3 = _split3(a)
    return _dot(e, p1) + (_dot(e, p2) + _dot(e, p3))


def _ssd_prompt_kernel(z_ref, xbc_ref, dtr_ref, cw_ref, cb_ref, dtb_ref, alog_ref, dexp_ref, ng_ref,
                       e_ref, y_ref, nbuf_ref, hf_ref, prev_s, xc_s, y_s, h_s, *, width, nh, inner):
    i = pl.program_id(1)
    last = pl.num_programs(1) - 1
    qn, cdim = xbc_ref.shape[1], xbc_ref.shape[2]
    ngrp, ns, hd = SSD_GROUPS, SSD_STATE, SSD_HEADDIM
    hpg = nh // ngrp

    @pl.when(i == 0)
    def _():
        prev_s[...] = jnp.zeros(prev_s.shape, F32)
        h_s[...] = jnp.zeros(h_s.shape, F32)

    for cb in range(cdim // SSD_COLS):
        cols = slice(cb * SSD_COLS, (cb + 1) * SSD_COLS)
        xr = xbc_ref[0, :, cols]
        prev = prev_s[:, cols]
        conv = xr * cw_ref[width - 1:width, cols]
        for k in range(1, width):
            conv = conv + _shift_rows(xr, prev, k) * cw_ref[width - 1 - k:width - k, cols]
        xc_s[:, cols] = _silu(conv + cb_ref[:, cols])
        prev_s[:, cols] = xr[qn - SUBLANE:, :]
        nbuf_ref[0, :, cols] = xr[qn - (width - 1):, :]

    hl = lax.broadcasted_iota(I32, (qn, LANE), 1)
    dt = jnp.where(hl < nh, _softplus(dtr_ref[0] + dtb_ref[...]), 0.0)
    dta = dt * (-jnp.exp(alog_ref[...]))
    qrow = lax.broadcasted_iota(I32, (qn, qn), 0)
    kcol = lax.broadcasted_iota(I32, (qn, qn), 1)
    causal = kcol <= qrow
    cum = _dot_sel_lhs(jnp.where(causal, 1.0, 0.0).astype(BF16), dta)
    cum_t = cum.T
    cum_last = cum[qn - 1:qn, :]
    ecum = jnp.exp(cum)
    dtw = dt * jnp.exp(cum_last - cum)
    e_last = jnp.exp(cum_last)
    lane_p = lax.broadcasted_iota(I32, (qn, LANE), 1)
    row_p = lax.broadcasted_iota(I32, (LANE, ns), 0)

    for g in range(ngrp):
        gl = slice(g * SSD_COLS, (g + 1) * SSD_COLS)
        e_g = e_ref[:, gl]
        xs_g = xc_s[:, gl]
        dtx_g = xs_g * _dot_sel_rhs(dt, e_g)
        dtxw_g = xs_g * _dot_sel_rhs(dtw, e_g)
        ecx_g = _dot_sel_rhs(ecum, e_g)
        bm = xc_s[:, inner + g * ns:inner + (g + 1) * ns].astype(BF16)
        cm = xc_s[:, inner + (ngrp + g) * ns:inner + (ngrp + g + 1) * ns].astype(BF16)
        cbm = _dot_nt(cm, bm)
        for pi in range(hpg // 2):
            ha = g * hpg + 2 * pi
            pls = slice(pi * LANE, (pi + 1) * LANE)
            lanes = slice(ha * hd, ha * hd + LANE)
            dtx_p = dtx_g[:, pls].astype(BF16)
            yds = []
            for h in (ha, ha + 1):
                seg = cum[:, h:h + 1] - cum_t[h:h + 1, :]
                dec = jnp.exp(jnp.where(causal, seg, -jnp.inf))
                yds.append(_dot((cbm * dec).astype(BF16), dtx_p))
            yd = jnp.where(lane_p < hd, yds[0], yds[1])
            hp = h_s[ha // 2]
            yo = _dot_nt(cm, hp.astype(BF16)) * ecx_g[:, pls]
            y_s[:, lanes] = yd + yo + dexp_ref[:, lanes] * xs_g[:, pls]
            s_new = _dot(dtxw_g[:, pls].T.astype(BF16), bm)
            cd = jnp.where(row_p < hd, e_last[:, ha:ha + 1], e_last[:, ha + 1:ha + 2])
            h_s[ha // 2] = hp * cd + s_new

    for g in range(ngrp):
        gl = slice(g * SSD_COLS, (g + 1) * SSD_COLS)
        yg = y_s[:, gl] * _silu(z_ref[0, :, gl])
        ms = jnp.mean(yg * yg, axis=-1, keepdims=True)
        y_ref[0, :, gl] = ((yg * lax.rsqrt(ms + EPS)) * ng_ref[:, gl]).astype(y_ref.dtype)

    @pl.when(i == last)
    def _():
        hf_ref[0] = h_s[...]


def _head_expand(nh, hd):
    e = np.zeros((LANE, nh * hd), np.float32)
    for h in range(nh):
        e[h, h * hd:(h + 1) * hd] = 1.0
    return jnp.asarray(e, BF16)


def _pad_lanes(v, n=LANE):
    v = v.reshape(1, -1)
    return jnp.pad(v, ((0, 0), (0, n - v.shape[1])))


def _ssd_prompt(z, xbc, dtr, conv_w, conv_b, dt_bias, a_log, d_skip, norm_g):
    bsz, seq, inner = z.shape
    cdim = xbc.shape[-1]
    nh = dt_bias.shape[0]
    width = conv_w.shape[0]
    qn = math.gcd(seq, SSD_CHUNK)
    assert qn == SSD_CHUNK and inner == SSD_GROUPS * SSD_COLS and cdim % SSD_COLS == 0
    dexp = jnp.repeat(d_skip, SSD_HEADDIM).reshape(1, inner)
    blk = lambda n: pl.BlockSpec((1, qn, n), lambda b, i: (b, i, 0))
    y, nbuf, hf = pl.pallas_call(
        functools.partial(_ssd_prompt_kernel, width=width, nh=nh, inner=inner),
        grid=(bsz, seq // qn),
        in_specs=[blk(inner), blk(cdim), blk(LANE),
                  _const2((width, cdim)), _const2((1, cdim)), _const2((1, LANE)), _const2((1, LANE)),
                  _const2((1, inner)), _const2((1, inner)), _const2((LANE, inner))],
        out_specs=[blk(inner),
                   pl.BlockSpec((1, width - 1, cdim), lambda b, i: (b, 0, 0)),
                   pl.BlockSpec((1, nh // 2, LANE, SSD_STATE), lambda b, i: (b, 0, 0, 0))],
        out_shape=[jax.ShapeDtypeStruct((bsz, seq, inner), BF16),
                   jax.ShapeDtypeStruct((bsz, width - 1, cdim), F32),
                   jax.ShapeDtypeStruct((bsz, nh // 2, LANE, SSD_STATE), F32)],
        scratch_shapes=[pltpu.VMEM((SUBLANE, cdim), F32), pltpu.VMEM((qn, cdim), F32),
                        pltpu.VMEM((qn, inner), F32), pltpu.VMEM((nh // 2, LANE, SSD_STATE), F32)],
        compiler_params=_cparams(2),
        name="ssd_prompt",
    )(z, xbc, dtr, conv_w, conv_b.reshape(1, cdim), _pad_lanes(dt_bias), _pad_lanes(a_log),
      dexp, norm_g.reshape(1, inner), _head_expand(nh, SSD_HEADDIM))
    return y, nbuf, hf.reshape(bsz, nh, SSD_HEADDIM, SSD_STATE)


def _ssd_prep_sample_kernel(xbc_ref, buf_ref, dtr_ref, cw_ref, cb_ref, dtb_ref, xc_ref, dt_ref, nbuf_ref,
                            *, width, nb, steps, nh):
    cdim = xbc_ref.shape[1]
    for cb in range(cdim // SSD_COLS):
        cols = slice(cb * SSD_COLS, (cb + 1) * SSD_COLS)
        ext = jnp.concatenate([buf_ref[:, cols], xbc_ref[:, cols]], axis=0)
        conv = ext[0:steps * nb] * cw_ref[0:1, cols]
        for k in range(1, width):
            conv = conv + ext[k * nb:(k + steps) * nb] * cw_ref[k:k + 1, cols]
        xc_ref[:, cols] = _silu(conv + cb_ref[:, cols])
        nbuf_ref[:, cols] = ext[steps * nb:]
    hl = lax.broadcasted_iota(I32, dtr_ref.shape, 1)
    dt_ref[...] = jnp.where(hl < nh, _softplus(dtr_ref[...] + dtb_ref[...]), 0.0)


def _ssd_prep_sample(xbc_tb, buf_kb, dtr_tb, conv_w, conv_b, dt_bias, nb, steps):
    rows, cdim = xbc_tb.shape
    width = conv_w.shape[0]
    return pl.pallas_call(
        functools.partial(_ssd_prep_sample_kernel, width=width, nb=nb, steps=steps, nh=dt_bias.shape[0]),
        out_shape=[jax.ShapeDtypeStruct((rows, cdim), F32),
                   jax.ShapeDtypeStruct((rows, LANE), F32),
                   jax.ShapeDtypeStruct(((width - 1) * nb, cdim), F32)],
        compiler_params=pltpu.CompilerParams(vmem_limit_bytes=VMEM_LIMIT),
        name="ssd_prep_sample",
    )(xbc_tb, buf_kb, dtr_tb, conv_w, conv_b.reshape(1, cdim), _pad_lanes(dt_bias))


def _ssd_scan_sample_kernel(xc_ref, dt_ref, z_ref, h0_ref, alog_ref, dexp_ref, ng_ref, e_ref,
                            y_ref, hf_ref, *, steps, inner):
    ngrp, ns = SSD_GROUPS, SSD_STATE
    xc, dt = xc_ref[0], dt_ref[0]
    row = lax.broadcasted_iota(I32, (QP, LANE), 0)
    cum = dt * (-jnp.exp(alog_ref[...]))
    for sft in (1, 2, 4):
        cum = cum + jnp.where(row >= sft, pltpu.roll(cum, sft, 0), 0.0)
    cum_last = cum[QP - 1:QP, :]
    parts = [dt, jnp.exp(cum), dt * jnp.exp(cum_last - cum), jnp.broadcast_to(jnp.exp(cum_last), (QP, LANE))]
    for s in range(steps):
        parts.append(jnp.where(row >= s, jnp.exp(cum - cum[s:s + 1, :]), 0.0))
    stack = jnp.concatenate(parts, axis=0)

    for g in range(ngrp):
        gl = slice(g * SSD_COLS, (g + 1) * SSD_COLS)
        ex = _dot_sel_rhs(stack, e_ref[:, gl])
        dt_x, ecum_x, dtw_x, el_x = (ex[k * QP:(k + 1) * QP] for k in range(4))
        xs_g = xc[:, gl]
        dtx = xs_g * dt_x
        bm = xc[:, inner + g * ns:inner + (g + 1) * ns]
        cm = xc[:, inner + (ngrp + g) * ns:inner + (ngrp + g + 1) * ns].astype(BF16)
        bmp = jnp.concatenate([bm, jnp.zeros((LANE - QP, ns), F32)], axis=0).astype(BF16)
        cbm = _dot_nt(cm, bmp)
        yd = jnp.zeros((QP, SSD_COLS), F32)
        for s in range(steps):
            yd = yd + (ex[(4 + s) * QP:(5 + s) * QP] * cbm[:, s:s + 1]) * dtx[s:s + 1, :]
        h0g = h0_ref[0, gl, :]
        yo = _dot_nt(cm, h0g.astype(BF16)) * ecum_x
        y = (yd + yo + dexp_ref[:, gl] * xs_g) * _silu(z_ref[0, :, gl])
        ms = jnp.mean(y * y, axis=-1, keepdims=True)
        y_ref[0, :, gl] = (y * lax.rsqrt(ms + EPS)) * ng_ref[:, gl]
        tm = jnp.concatenate([xs_g * dtw_x, el_x[0:1], jnp.zeros((LANE - QP - 1, SSD_COLS), F32)], axis=0)
        tt = tm.T
        hf_ref[0, gl, :] = h0g * tt[:, QP:QP + 1] + _dot(tt.astype(BF16), bmp)


def _ssd_scan_sample(xc, dt, z, h0, a_log, d_skip, norm_g, steps):
    nb, _, cdim = xc.shape
    inner = z.shape[-1]
    nh = a_log.shape[0]
    rows_h = nh * SSD_HEADDIM
    dexp = jnp.repeat(d_skip, SSD_HEADDIM).reshape(1, inner)
    one = lambda shape: pl.BlockSpec(shape, lambda b: (0,) * len(shape))
    return pl.pallas_call(
        functools.partial(_ssd_scan_sample_kernel, steps=steps, inner=inner),
        grid=(nb,),
        in_specs=[pl.BlockSpec((1, QP, cdim), lambda b: (b, 0, 0)),
                  pl.BlockSpec((1, QP, LANE), lambda b: (b, 0, 0)),
                  pl.BlockSpec((1, QP, inner), lambda b: (b, 0, 0)),
                  pl.BlockSpec((1, rows_h, SSD_STATE), lambda b: (b, 0, 0)),
                  one((1, LANE)), one((1, inner)), one((1, inner)), one((LANE, inner))],
        out_specs=[pl.BlockSpec((1, QP, inner), lambda b: (b, 0, 0)),
                   pl.BlockSpec((1, rows_h, SSD_STATE), lambda b: (b, 0, 0))],
        out_shape=[jax.ShapeDtypeStruct((nb, QP, inner), F32),
                   jax.ShapeDtypeStruct((nb, rows_h, SSD_STATE), F32)],
        compiler_params=_cparams(1),
        name="ssd_scan_sample",
    )(xc, dt, z, h0, _pad_lanes(a_log), dexp, norm_g.reshape(1, inner), _head_expand(nh, SSD_HEADDIM))


TM = 256


def _to_steps(a, nb, steps):
    n = a.shape[-1]
    return a.reshape(nb, steps, n).transpose(1, 0, 2).reshape(steps * nb, n)


def _to_batch(a, nb, steps):
    n = a.shape[-1]
    return a.reshape(steps, nb, n).transpose(1, 0, 2).reshape(1, nb * steps, n)


def _pad_q(a, nb, steps):
    n = a.shape[-1]
    return jnp.pad(a.reshape(nb, steps, n), ((0, 0), (0, QP - steps), (0, 0)))


def kernel(x_prompt, x_sample, cache_k, cache_v, cache_kidx, state_s5_re, state_s5_im, state_sconv, state_ssd, state_ssd_conv, page_table, c_prompt, c_sample, rel_bias, ada_w, ada_b, norm_mix, norm_mlp, norm_final, attn_w_in, attn_w_out, s5_lam_re, s5_lam_im, s5_log_dt, s5_b_re, s5_b_im, s5_c_re, s5_c_im, s5_d, s5_w_glu, sc_w_in, sc_w_conv, sc_w_out, ssd_w_in, ssd_conv_w, ssd_conv_b, ssd_dt_bias, ssd_a_log, ssd_d, ssd_norm, ssd_w_out, mlp_w1, mlp_w2):
    bp, seq, d = x_prompt.shape
    nb, steps, _ = x_sample.shape
    depth = ada_w.shape[0]
    n_mixers = 4
    rs = nb * steps
    tm = min(TM, seq)

    rows = bp + nb
    c_all = jnp.pad(jnp.concatenate([c_prompt, c_sample], axis=0), ((0, (-rows) % SUBLANE), (0, 0)))
    ada = _ada(c_all, ada_w, ada_b)

    xp = x_prompt
    xs = x_sample.reshape(1, rs, d)
    outs = {name: [] for name in ("kp", "vp", "kip", "ks", "vs", "kis", "s5pr", "s5pi", "s5sr", "s5si",
                                  "scp", "scs", "ssdp", "ssdcp", "ssds", "ssdcs")}
    nq, nkv = N_HEADS * HEAD_DIM, N_KV_HEADS * HEAD_DIM
    w1_all, w2_all = mlp_w1.astype(BF16), mlp_w2.astype(BF16)
    for i in range(depth):
        m, j = i % n_mixers, i // n_mixers
        mp = [ada[i, :bp, k * d:(k + 1) * d].reshape(bp, 1, d) for k in range(6)]
        ms = [jnp.repeat(ada[i, bp:bp + nb, k * d:(k + 1) * d], steps, axis=0).reshape(1, rs, d)
              for k in range(6)]
        g_mix = norm_mix[i]
        mix_p = mix_s = None
        if m == 0:
            w_in = attn_w_in[j]
            w_qkv = w_in[:, :nq + 2 * nkv].astype(BF16)
            n_idx = IDX_HEADS * IDX_DIM + LANE
            w_idx = jnp.pad(w_in[:, nq + 2 * nkv:], ((0, 0), (0, n_idx - (w_in.shape[1] - nq - 2 * nkv))))
            w_out = attn_w_out[j].astype(BF16)
            page = cache_k.shape[2]
            q, k, v, kw, qcat, kcat, kaug, vaug, ki = _attn_proj(xp, mp[0], mp[1], g_mix, w_qkv, w_idx,
                                                                 2 * MAX_DISTANCE, True, page)
            o = _dsa_prompt(q, qcat, kw, kcat, kaug, vaug, rel_bias, MAX_DISTANCE)
            mix_p = ("proj", o, mp[2], w_out)
            to_pages = lambda t: jnp.transpose(t.reshape(bp, seq // page, N_KV_HEADS, HEAD_DIM, page), (0, 1, 4, 2, 3))
            outs["kp"].append(to_pages(k))
            outs["vp"].append(to_pages(v))
            outs["kip"].append(jnp.transpose(ki, (0, 1, 3, 2)))
            q, k, v, kw, qcat, kcat, _, _ = _attn_proj(xs, ms[0], ms[1], g_mix, w_qkv, w_idx, rs, False)
            pps = math.gcd(PAGES_PER_STEP, page_table.shape[1])
            qc_p, kw_p = _pad_q(qcat.astype(F32), nb, steps), _pad_q(kw, nb, steps)
            scores, snew = _dsa_sample_scores(page_table, qc_p, kw_p, _pad_q(kcat.astype(F32), nb, steps),
                                              _pages_t(cache_kidx), j, pps)
            o = _dsa_sample(page_table, _pad_q(q, nb, steps), _pad_q(k, nb, steps), _pad_q(v, nb, steps),
                            scores, snew, _pages_t(cache_k), _pages_t(cache_v), rel_bias, j, steps, pps)
            mix_s = ("proj", o[:, :steps].reshape(1, rs, nq), ms[2], w_out)
            outs["ks"].append(k.reshape(nb, steps, N_KV_HEADS, HEAD_DIM))
            outs["vs"].append(v.reshape(nb, steps, N_KV_HEADS, HEAD_DIM))
            outs["kis"].append(kw[..., :IDX_DIM].reshape(nb, steps, IDX_DIM))
        elif m == 1:
            ar, ai, wb, wc = _s5_weights(s5_lam_re[j], s5_lam_im[j], s5_log_dt[j], s5_b_re[j], s5_b_im[j],
                                         s5_c_re[j], s5_c_im[j])
            w_glu = s5_w_glu[j].astype(BF16)
            grp, nst = s5_lam_re.shape[1:]
            z, fr, fi = _s5_prompt(xp, mp[0], mp[1], g_mix, s5_d[j], ar, ai, wb, wc)
            mix_p = ("glu", z, mp[2], w_glu)
            outs["s5pr"].append(fr.reshape(bp, grp, nst))
            outs["s5pi"].append(fi.reshape(bp, grp, nst))
            z, fr, fi = _s5_sample(_to_steps(xs, nb, steps), _to_steps(ms[0], nb, steps),
                                   _to_steps(ms[1], nb, steps), g_mix, s5_d[j], ar, ai, wb, wc,
                                   state_s5_re[j].reshape(nb, grp * nst), state_s5_im[j].reshape(nb, grp * nst),
                                   nb, steps)
            mix_s = ("glu", _to_batch(z, nb, steps), ms[2], w_glu)
            outs["s5sr"].append(fr.reshape(nb, grp, nst))
            outs["s5si"].append(fi.reshape(nb, grp, nst))
        elif m == 2:
            w_in = sc_w_in[j].astype(BF16)
            w_out = sc_w_out[j].astype(BF16)
            width = sc_w_conv.shape[1]
            xp, nbuf = _sconv_prompt(xp, mp[0], mp[1], g_mix, w_in, mp[2], jnp.zeros((bp, width - 1, d), F32),
                                     sc_w_conv[j], w_out, tm)
            outs["scp"].append(nbuf)
            (p,) = _proj(xs, ms[0], ms[1], g_mix, w_in, (3 * d,), rs)
            buf = state_sconv[j].transpose(1, 0, 2).reshape((width - 1) * nb, d)
            y, nbuf = _sconv_sample(_to_steps(p, nb, steps), _to_steps(xs, nb, steps), _to_steps(ms[2], nb, steps),
                                    buf, sc_w_conv[j], w_out, nb, steps)
            xs = _to_batch(y, nb, steps)
            outs["scs"].append(nbuf.reshape(width - 1, nb, d).transpose(1, 0, 2))
        else:
            inner = ssd_norm.shape[1]
            cdim = ssd_conv_w.shape[2]
            nh = ssd_dt_bias.shape[1]
            width = ssd_conv_w.shape[1]
            w_in = jnp.pad(ssd_w_in[j], ((0, 0), (0, LANE - nh))).astype(BF16)
            w_out = ssd_w_out[j].astype(BF16)
            z, xbc, dtr = _proj(xp, mp[0], mp[1], g_mix, w_in, (inner, cdim, LANE), tm)
            y, nbuf, hf = _ssd_prompt(z, xbc, dtr, ssd_conv_w[j], ssd_conv_b[j], ssd_dt_bias[j], ssd_a_log[j],
                                      ssd_d[j], ssd_norm[j])
            mix_p = ("proj", y, mp[2], w_out)
            outs["ssdp"].append(hf)
            outs["ssdcp"].append(nbuf)
            z, xbc, dtr = _proj(xs, ms[0], ms[1], g_mix, w_in, (inner, cdim, LANE), rs)
            buf = state_ssd_conv[j].transpose(1, 0, 2).reshape((width - 1) * nb, cdim)
            xc, dt, nbuf = _ssd_prep_sample(_to_steps(xbc, nb, steps), buf, _to_steps(dtr, nb, steps),
                                            ssd_conv_w[j], ssd_conv_b[j], ssd_dt_bias[j], nb, steps)
            y, hf = _ssd_scan_sample(_pad_q(_to_batch(xc, nb, steps), nb, steps),
                                     _pad_q(_to_batch(dt, nb, steps), nb, steps), _pad_q(z, nb, steps),
                                     state_ssd[j].reshape(nb, nh * SSD_HEADDIM, SSD_STATE),
                                     ssd_a_log[j], ssd_d[j], ssd_norm[j], steps)
            mix_s = ("proj", y[:, :steps].reshape(1, rs, inner), ms[2], w_out)
            outs["ssds"].append(hf.reshape(nb, nh, SSD_HEADDIM, SSD_STATE))
            outs["ssdcs"].append(nbuf.reshape(width - 1, nb, cdim).transpose(1, 0, 2))
        fin = i == depth - 1
        xp = _mlp(xp, mp[3], mp[4], mp[5], norm_mlp[i], w1_all, w2_all, i, norm_final, fin, tm, mix_p)
        xs = _mlp(xs, ms[3], ms[4], ms[5], norm_mlp[i], w1_all, w2_all, i, norm_final, fin, rs, mix_s)
    st = jnp.stack
    return (xp, xs.reshape(nb, steps, d), st(outs["kp"]), st(outs["vp"]), st(outs["kip"]),
            st(outs["ks"]), st(outs["vs"]), st(outs["kis"]), st(outs["s5pr"]), st(outs["s5pi"]),
            st(outs["s5sr"]), st(outs["s5si"]), st(outs["scp"]), st(outs["scs"]),
            st(outs["ssdp"]), st(outs["ssdcp"]), st(outs["ssds"]), st(outs["ssdcs"]))
```

```python
import functools
import math

import jax
import jax.numpy as jnp
import numpy as np
from jax import lax
from jax.experimental import pallas as pl
from jax.experimental.pallas import tpu as pltpu

F32 = jnp.float32
BF16 = jnp.bfloat16
I32 = jnp.int32

EPS = 1e-6
N_HEADS = 16
HEAD_DIM = 64
N_KV_HEADS = 4
IDX_HEADS = 8
IDX_DIM = 64
TOPK_MAX = 256
N_BUCKETS = 32
MAX_DISTANCE = 128
S5_GROUP = 16
S5_STATE = 64
SSD_HEADDIM = 64
SSD_GROUPS = 4
SSD_STATE = 128
SSD_CHUNK = 128

LANE = 128
SUBLANE = 8
VMEM_LIMIT = 56 * 1024 * 1024
NEG_BIG = -1e30
INT_MIN = -2147483648


def _cparams(n_axes):
    return pltpu.CompilerParams(dimension_semantics=("arbitrary",) * n_axes,
                                vmem_limit_bytes=VMEM_LIMIT)


def _dot(a, b):
    return jnp.dot(a, b, preferred_element_type=F32)


def _dot_nt(a, b):
    return lax.dot_general(a, b, (((1,), (1,)), ((), ())), preferred_element_type=F32)


def _split(a):
    hi = a.astype(BF16)
    lo = (a - hi.astype(F32)).astype(BF16)
    return hi, lo


def _dot3(a, b):
    ah, al = _split(a)
    bh, bl = _split(b)
    return _dot(ah, bh) + (_dot(al, bh) + _dot(ah, bl))


def _rms_mod(x, g, shift, scale):
    y = x * lax.rsqrt(jnp.mean(x * x, axis=-1, keepdims=True) + EPS)
    return (y * g) * (1.0 + scale) + shift


def _sigmoid(x):
    return 1.0 / (1.0 + jnp.exp(-x))


def _silu(x):
    return x * _sigmoid(x)


def _row_block(arr, tm):
    n = arr.shape[-1]
    if arr.shape[1] == 1:
        return pl.BlockSpec((1, 1, n), lambda b, i: (b, 0, 0))
    return pl.BlockSpec((1, tm, n), lambda b, i: (b, i, 0))


def _const2(shape):
    return pl.BlockSpec(shape, lambda b, i: (0,) * len(shape))


def _ada_kernel(c_ref, w_ref, b_ref, o_ref):
    o_ref[0] = _dot3(_silu(c_ref[...]), w_ref[0]) + b_ref[0]


def _ada(c_all, ada_w, ada_b):
    depth, d, n = ada_w.shape
    rows = c_all.shape[0]
    tn = 1536
    return pl.pallas_call(
        _ada_kernel,
        grid=(depth, n // tn),
        in_specs=[pl.BlockSpec((rows, d), lambda l, j: (0, 0)),
                  pl.BlockSpec((1, d, tn), lambda l, j: (l, 0, j)),
                  pl.BlockSpec((1, 1, tn), lambda l, j: (l, 0, j))],
        out_specs=pl.BlockSpec((1, rows, tn), lambda l, j: (l, 0, j)),
        out_shape=jax.ShapeDtypeStruct((depth, rows, n), F32),
        compiler_params=_cparams(2),
        name="ada",
    )(c_all, ada_w, ada_b.reshape(depth, 1, n))


def _proj_kernel(x_ref, sh_ref, sc_ref, g_ref, w_ref, *o_refs, splits):
    h = _rms_mod(x_ref[0], g_ref[...], sh_ref[0], sc_ref[0]).astype(BF16)
    off = 0
    for o_ref, n in zip(o_refs, splits):
        o_ref[0] = _dot(h, w_ref[:, off:off + n])
        off += n


def _proj(x, shift, scale, g, w_bf16, splits, tm):
    bsz, seq, d = x.shape
    n = w_bf16.shape[1]
    assert sum(splits) == n and seq % tm == 0
    return pl.pallas_call(
        functools.partial(_proj_kernel, splits=tuple(splits)),
        grid=(bsz, seq // tm),
        in_specs=[pl.BlockSpec((1, tm, d), lambda b, i: (b, i, 0)),
                  _row_block(shift, tm), _row_block(scale, tm),
                  _const2((1, d)), _const2((d, n))],
        out_specs=[pl.BlockSpec((1, tm, s), lambda b, i: (b, i, 0)) for s in splits],
        out_shape=[jax.ShapeDtypeStruct((bsz, seq, s), F32) for s in splits],
        compiler_params=_cparams(2),
        name="proj",
    )(x, shift, scale, g.reshape(1, d), w_bf16)


def _aug_heads(t, col):
    rows = t.shape[0]
    lane = lax.broadcasted_iota(I32, (rows, LANE - HEAD_DIM), 1)
    extra = jnp.where(lane == 0, col, 0.0).astype(F32)
    parts = []
    for g in range(N_KV_HEADS):
        parts += [t[:, g * HEAD_DIM:(g + 1) * HEAD_DIM], extra]
    return jnp.concatenate(parts, axis=1).astype(BF16)


def _attn_proj_kernel(x_ref, sh_ref, sc_ref, g_ref, w_ref, wi_ref,
                      q_ref, k_ref, v_ref, kw_ref, qcat_ref, kcat_ref, kaug_ref, vaug_ref, *ki_refs,
                      nq, nkv, nqi, key_major):
    (ki_ref,) = ki_refs if key_major else (None,)
    h = _rms_mod(x_ref[0], g_ref[...], sh_ref[0], sc_ref[0])
    hb = h.astype(BF16)
    q_ref[0] = _dot(hb, w_ref[:, 0:nq]) * (HEAD_DIM ** -0.5)
    k = _dot(hb, w_ref[:, nq:nq + nkv])
    v = _dot(hb, w_ref[:, nq + nkv:nq + 2 * nkv])
    r = _dot3(h, wi_ref[...])
    kw = r[:, nqi:nqi + LANE]
    kw_ref[0] = kw
    if key_major:
        page = k_ref.shape[-1]
        kt, vt, kwt = k.T, v.T, kw.T
        for p in range(k_ref.shape[1]):
            k_ref[0, p] = kt[:, p * page:(p + 1) * page]
            v_ref[0, p] = vt[:, p * page:(p + 1) * page]
            ki_ref[0, p] = kwt[0:IDX_DIM, p * page:(p + 1) * page]
    else:
        k_ref[0] = k
        v_ref[0] = v
    zero = jnp.zeros((h.shape[0], IDX_DIM), BF16)
    parts = []
    for hh in range(IDX_HEADS):
        ah, al = _split(r[:, hh * IDX_DIM:(hh + 1) * IDX_DIM])
        parts += [ah, al, ah, zero]
    qcat_ref[0] = jnp.concatenate(parts, axis=1)
    kcat = _idx_rhs(kw[:, 0:IDX_DIM])
    kaug = _aug_heads(k, 0.0)
    vaug_ref[0] = _aug_heads(v, 1.0)
    if key_major:
        kcat_ref[0, 0] = kcat.astype(F32).T.astype(BF16)
        kaug_ref[0, 0] = kaug.astype(F32).T.astype(BF16)
    else:
        kcat_ref[0] = kcat
        kaug_ref[0] = kaug


def _attn_proj(x, shift, scale, g, w_qkv_bf16, w_idx, tm, key_major, page=None):
    bsz, seq, d = x.shape
    nq = N_HEADS * HEAD_DIM
    nkv = N_KV_HEADS * HEAD_DIM
    nqi = IDX_HEADS * IDX_DIM
    rows = lambda n, dt: (pl.BlockSpec((1, tm, n), lambda b, i: (b, i, 0)), jax.ShapeDtypeStruct((bsz, seq, n), dt))
    cols = lambda n, dt: (pl.BlockSpec((1, 1, n, tm), lambda b, i: (b, i, 0, 0)),
                          jax.ShapeDtypeStruct((bsz, seq // tm, n, tm), dt))
    keys = cols if key_major else rows
    if key_major:
        ppt = tm // page
        paged = lambda n: (pl.BlockSpec((1, ppt, n, page), lambda b, i: (b, i, 0, 0)),
                           jax.ShapeDtypeStruct((bsz, seq // page, n, page), F32))
        kv = [paged(nkv), paged(nkv)]
    else:
        kv = [rows(nkv, F32), rows(nkv, F32)]
    outs = [rows(nq, F32)] + kv + [rows(LANE, F32), rows(4 * nqi, BF16),
                                   keys(4 * IDX_DIM, BF16), keys(N_KV_HEADS * LANE, BF16),
                                   rows(N_KV_HEADS * LANE, BF16)]
    if key_major:
        outs.append(paged(IDX_DIM))
    return pl.pallas_call(
        functools.partial(_attn_proj_kernel, nq=nq, nkv=nkv, nqi=nqi, key_major=key_major),
        grid=(bsz, seq // tm),
        in_specs=[pl.BlockSpec((1, tm, d), lambda b, i: (b, i, 0)),
                  _row_block(shift, tm), _row_block(scale, tm),
                  _const2((1, d)), _const2(w_qkv_bf16.shape), _const2(w_idx.shape)],
        out_specs=[o[0] for o in outs],
        out_shape=[o[1] for o in outs],
        compiler_params=_cparams(2),
        name="attn_proj",
    )(x, shift, scale, g.reshape(1, d), w_qkv_bf16, w_idx)


def _mlp_kernel(*refs, ff_chunk, final_norm, mixer_out):
    if mixer_out is None:
        x_ref, sh_ref, sc_ref, gate_ref, g_ref, w1_ref, w2_ref, gf_ref, y_ref = refs
        x = x_ref[0]
    else:
        o_ref, gmix_ref, wo_ref, x_ref, sh_ref, sc_ref, gate_ref, g_ref, w1_ref, w2_ref, gf_ref, y_ref = refs
        d = x_ref.shape[-1]
        ob = o_ref[0].astype(BF16)
        if mixer_out == "glu":
            t = _dot(ob, wo_ref[:, :d]) * _sigmoid(_dot(ob, wo_ref[:, d:]))
        else:
            t = _dot(ob, wo_ref[...])
        x = x_ref[0] + gmix_ref[0] * t
    h = _rms_mod(x, g_ref[...], sh_ref[0], sc_ref[0]).astype(BF16)
    dff = w1_ref.shape[-1]
    acc = jnp.zeros(x.shape, F32)
    for c in range(dff // ff_chunk):
        a = jnp.maximum(_dot(h, w1_ref[0, :, c * ff_chunk:(c + 1) * ff_chunk]), 0.0)
        acc = acc + _dot((a * a).astype(BF16), w2_ref[0, c * ff_chunk:(c + 1) * ff_chunk, :])
    y = x + gate_ref[0] * acc
    if final_norm:
        y = (y * lax.rsqrt(jnp.mean(y * y, axis=-1, keepdims=True) + EPS)) * gf_ref[...]
    y_ref[0] = y


def _resident(shape):
    return pl.BlockSpec(shape, lambda b, i: (0,) * len(shape), pipeline_mode=pl.Buffered(1))


def _mlp(x, shift, scale, gate, g, w1_bf16, w2_bf16, layer, g_final, final_norm, tm, mixer=None):
    bsz, seq, d = x.shape
    dff = w1_bf16.shape[-1]
    layer_w = lambda r, c: pl.BlockSpec((1, r, c), lambda b, i: (layer, 0, 0), pipeline_mode=pl.Buffered(1))
    rows = lambda n: pl.BlockSpec((1, tm, n), lambda b, i: (b, i, 0))
    args, specs, kind = [], [], None
    if mixer is not None:
        kind, o, gmix, wo = mixer
        args += [o, gmix, wo]
        specs += [rows(o.shape[-1]), _row_block(gmix, tm), _resident(wo.shape)]
    args += [x, shift, scale, gate, g.reshape(1, d), w1_bf16, w2_bf16, g_final.reshape(1, d)]
    specs += [rows(d), _row_block(shift, tm), _row_block(scale, tm), _row_block(gate, tm),
              _const2((1, d)), layer_w(d, dff), layer_w(dff, d), _const2((1, d))]
    return pl.pallas_call(
        functools.partial(_mlp_kernel, ff_chunk=1024, final_norm=final_norm, mixer_out=kind),
        grid=(bsz, seq // tm),
        in_specs=specs,
        out_specs=rows(d),
        out_shape=jax.ShapeDtypeStruct((bsz, seq, d), F32),
        compiler_params=_cparams(2),
        name="mlp",
    )(*args)


S5_CH = 128
S5_BLK = (S5_CH // S5_GROUP) * S5_STATE


def _s5_disc_kernel(lr_ref, li_ref, ldt_ref, br_ref, bi_ref, ar_ref, ai_ref, bbr_ref, bbi_ref):
    lr, li = lr_ref[...], li_ref[...]
    dt = jnp.exp(ldt_ref[...])
    mag = jnp.exp(lr * dt)
    ab_re, ab_im = mag * jnp.cos(li * dt), mag * jnp.sin(li * dt)
    den = lr * lr + li * li
    nr = ab_re - 1.0
    f_re = (nr * lr + ab_im * li) / den
    f_im = (ab_im * lr - nr * li) / den
    ar_ref[...] = ab_re
    ai_ref[...] = ab_im
    for c in range(br_ref.shape[0]):
        br, bi = br_ref[c], bi_ref[c]
        bbr_ref[c] = f_re * br - f_im * bi
        bbi_ref[c] = f_re * bi + f_im * br


def _s5_weights(lam_re, lam_im, log_dt, b_re, b_im, c_re, c_im):
    g, p = lam_re.shape
    gc = b_re.shape[-1]
    brt = jnp.moveaxis(b_re, 2, 0)
    bit = jnp.moveaxis(b_im, 2, 0)
    ar, ai, bbr, bbi = pl.pallas_call(
        _s5_disc_kernel,
        out_shape=[jax.ShapeDtypeStruct((g, p), F32), jax.ShapeDtypeStruct((g, p), F32),
                   jax.ShapeDtypeStruct((gc, g, p), F32), jax.ShapeDtypeStruct((gc, g, p), F32)],
        name="s5_disc",
    )(lam_re, lam_im, log_dt.reshape(g, 1), brt, bit)
    nblk = (g * gc) // S5_CH
    gpb = g // nblk
    eye = jnp.eye(gpb, dtype=F32)

    def bd_in(bb):
        t = jnp.moveaxis(bb, 0, 1).reshape(nblk, gpb, gc, p)
        return jnp.einsum('ngcp,gh->ngchp', t, eye).reshape(nblk, gpb * gc, gpb * p)

    def bd_out(cc):
        t = cc.reshape(nblk, gpb, gc, p)
        return jnp.einsum('ngcp,gh->ngphc', t, eye).reshape(nblk, gpb * p, gpb * gc)

    wb = jnp.concatenate([bd_in(bbr), bd_in(bbi)], axis=-1).astype(BF16)
    wc = jnp.concatenate([bd_out(c_re), bd_out(c_im)], axis=1).astype(BF16)
    return ar.reshape(nblk, 1, gpb * p), ai.reshape(nblk, 1, gpb * p), wb, wc


def _gelu_tanh(y):
    return 0.5 * y * (1.0 + jnp.tanh(math.sqrt(2.0 / math.pi) * (y + 0.044715 * (y * y * y))))


def _cmul(ar, ai, xr, xi):
    return ar * xr - ai * xi, ar * xi + ai * xr


def _s5_prompt_kernel(x_ref, sh_ref, sc_ref, g_ref, d_ref, ar_ref, ai_ref, wb_ref, wc_ref,
                      perm_ref, unperm_ref, z_ref, fr_ref, fi_ref, u_s, br_s, bi_s, st_r, st_i,
                      *, nseg, seg):
    i = pl.program_id(1)
    nblk = wb_ref.shape[0]
    blk = ar_ref.shape[-1]

    @pl.when(i == 0)
    def _():
        st_r[...] = jnp.zeros(st_r.shape, F32)
        st_i[...] = jnp.zeros(st_i.shape, F32)

    u_s[...] = _dot_sel_lhs(perm_ref[...], _rms_mod(x_ref[0], g_ref[...], sh_ref[0], sc_ref[0]))

    for c in range(nblk):
        lo, hi = c * S5_CH, (c + 1) * S5_CH
        uc = u_s[:, lo:hi]
        bu = _dot(uc.astype(BF16), wb_ref[c])
        br_s[...] = bu[:, :blk]
        bi_s[...] = bu[:, blk:]
        ar = jnp.broadcast_to(ar_ref[c], (nseg, blk))
        ai = jnp.broadcast_to(ai_ref[c], (nseg, blk))

        def local(j, carry):
            xr, xi = carry
            r0 = pl.multiple_of(j * nseg, nseg)
            pr, pi = _cmul(ar, ai, xr, xi)
            nr = pr + br_s[pl.ds(r0, nseg), :]
            ni = pi + bi_s[pl.ds(r0, nseg), :]
            br_s[pl.ds(r0, nseg), :] = nr
            bi_s[pl.ds(r0, nseg), :] = ni
            return nr, ni

        zero = jnp.zeros((nseg, blk), F32)
        fr, fi = lax.fori_loop(0, seg, local, (zero, zero), unroll=True)

        pr, pi = ar_ref[c], ai_ref[c]
        for _ in range(int(math.log2(seg))):
            pr, pi = _cmul(pr, pi, pr, pi)
        cr, ci = st_r[c], st_i[c]
        rows_r, rows_i = [], []
        for s in range(nseg):
            rows_r.append(cr)
            rows_i.append(ci)
            tr, ti = _cmul(pr, pi, cr, ci)
            cr, ci = tr + fr[s:s + 1], ti + fi[s:s + 1]
        st_r[c] = cr
        st_i[c] = ci
        fr_ref[0, c] = cr
        fi_ref[0, c] = ci
        dr, di = _cmul(ar, ai, jnp.concatenate(rows_r, axis=0), jnp.concatenate(rows_i, axis=0))

        def fix(j, carry):
            dr, di = carry
            r0 = pl.multiple_of(j * nseg, nseg)
            br_s[pl.ds(r0, nseg), :] = br_s[pl.ds(r0, nseg), :] + dr
            bi_s[pl.ds(r0, nseg), :] = bi_s[pl.ds(r0, nseg), :] + di
            return _cmul(ar, ai, dr, di)

        lax.fori_loop(0, seg, fix, (dr, di), unroll=True)

        y = (_dot(br_s[...].astype(BF16), wc_ref[c, :blk, :])
             - _dot(bi_s[...].astype(BF16), wc_ref[c, blk:, :]))
        y = y + d_ref[:, lo:hi] * uc
        u_s[:, lo:hi] = _gelu_tanh(y)

    z_ref[0] = _dot_sel_lhs(unperm_ref[...], u_s[...]).astype(z_ref.dtype)


def _s5_prompt(x, shift, scale, g, d_skip, ar, ai, wb, wc):
    bsz, seq, d = x.shape
    nblk, _, blk = ar.shape
    nseg, seg = SUBLANE, 32
    tm = nseg * seg
    assert seq % tm == 0
    perm = np.zeros((tm, tm), np.float32)
    for s in range(nseg):
        for j in range(seg):
            perm[j * nseg + s, s * seg + j] = 1.0
    unperm = jnp.asarray(perm.T, BF16)
    perm = jnp.asarray(perm, BF16)
    z, fr, fi = pl.pallas_call(
        functools.partial(_s5_prompt_kernel, nseg=nseg, seg=seg),
        grid=(bsz, seq // tm),
        in_specs=[pl.BlockSpec((1, tm, d), lambda b, i: (b, i, 0)),
                  _row_block(shift, tm), _row_block(scale, tm),
                  _const2((1, d)), _const2((1, d)),
                  _const2(ar.shape), _const2(ai.shape), _const2(wb.shape), _const2(wc.shape),
                  _const2((tm, tm)), _const2((tm, tm))],
        out_specs=[pl.BlockSpec((1, tm, d), lambda b, i: (b, i, 0)),
                   pl.BlockSpec((1, nblk, 1, blk), lambda b, i: (b, 0, 0, 0)),
                   pl.BlockSpec((1, nblk, 1, blk), lambda b, i: (b, 0, 0, 0))],
        out_shape=[jax.ShapeDtypeStruct((bsz, seq, d), BF16),
                   jax.ShapeDtypeStruct((bsz, nblk, 1, blk), F32),
                   jax.ShapeDtypeStruct((bsz, nblk, 1, blk), F32)],
        scratch_shapes=[pltpu.VMEM((tm, d), F32), pltpu.VMEM((tm, blk), F32), pltpu.VMEM((tm, blk), F32),
                        pltpu.VMEM((nblk, 1, blk), F32), pltpu.VMEM((nblk, 1, blk), F32)],
        compiler_params=_cparams(2),
        name="s5_prompt",
    )(x, shift, scale, g.reshape(1, d), d_skip.reshape(1, d), ar, ai, wb, wc, perm, unperm)
    return z, fr, fi


def _s5_sample_kernel(x_ref, sh_ref, sc_ref, g_ref, d_ref, ar_ref, ai_ref, wb_ref, wc_ref,
                      s0r_ref, s0i_ref, z_ref, fr_ref, fi_ref, *, nb, steps):
    nblk = wb_ref.shape[0]
    blk = ar_ref.shape[-1]
    u = _rms_mod(x_ref[...], g_ref[...], sh_ref[...], sc_ref[...])
    for c in range(nblk):
        lo, hi = c * S5_CH, (c + 1) * S5_CH
        uc = u[:, lo:hi]
        bu = _dot(uc.astype(BF16), wb_ref[c])
        ar = jnp.broadcast_to(ar_ref[c], (nb, blk))
        ai = jnp.broadcast_to(ai_ref[c], (nb, blk))
        xr, xi = s0r_ref[:, c * blk:(c + 1) * blk], s0i_ref[:, c * blk:(c + 1) * blk]
        xrs, xis = [], []
        for t in range(steps):
            pr, pi = _cmul(ar, ai, xr, xi)
            xr = pr + bu[t * nb:(t + 1) * nb, :blk]
            xi = pi + bu[t * nb:(t + 1) * nb, blk:]
            xrs.append(xr)
            xis.append(xi)
        fr_ref[:, c * blk:(c + 1) * blk] = xr
        fi_ref[:, c * blk:(c + 1) * blk] = xi
        y = (_dot(jnp.concatenate(xrs, axis=0).astype(BF16), wc_ref[c, :blk, :])
             - _dot(jnp.concatenate(xis, axis=0).astype(BF16), wc_ref[c, blk:, :]))
        y = y + d_ref[:, lo:hi] * uc
        z_ref[:, lo:hi] = _gelu_tanh(y)


def _s5_sample(x_tb, shift_tb, scale_tb, g, d_skip, ar, ai, wb, wc, s0r, s0i, nb, steps):
    rows, d = x_tb.shape
    nblk, _, blk = ar.shape
    return pl.pallas_call(
        functools.partial(_s5_sample_kernel, nb=nb, steps=steps),
        out_shape=[jax.ShapeDtypeStruct((rows, d), F32),
                   jax.ShapeDtypeStruct((nb, nblk * blk), F32),
                   jax.ShapeDtypeStruct((nb, nblk * blk), F32)],
        compiler_params=pltpu.CompilerParams(vmem_limit_bytes=VMEM_LIMIT),
        name="s5_sample",
    )(x_tb, shift_tb, scale_tb, g.reshape(1, d), d_skip.reshape(1, d), ar, ai, wb, wc, s0r, s0i)


def _shift_rows(u, prev, k):
    if k == 0:
        return u
    rolled = pltpu.roll(u, k, 0)
    head = pltpu.roll(prev, k, 0)
    row = lax.broadcasted_iota(I32, (SUBLANE, u.shape[1]), 0)
    first = jnp.where(row < k, head, rolled[:SUBLANE])
    return jnp.concatenate([first, rolled[SUBLANE:]], axis=0)


def _sconv_prompt_kernel(x_ref, sh_ref, sc_ref, g_ref, wi_ref, gate_ref, buf_ref, wc_ref, wo_ref,
                         y_ref, nb_ref, prev_s, *, width):
    i = pl.program_id(1)
    d = x_ref.shape[-1]

    @pl.when(i == 0)
    def _():
        prev_s[...] = jnp.zeros(prev_s.shape, F32)
        prev_s[SUBLANE - (width - 1):, :] = buf_ref[0]

    h = _rms_mod(x_ref[0], g_ref[...], sh_ref[0], sc_ref[0]).astype(BF16)
    gb = _dot(h, wi_ref[:, :d])
    u = _dot(h, wi_ref[:, d:2 * d]) * _dot(h, wi_ref[:, 2 * d:])
    prev = prev_s[...]
    conv = u * wc_ref[width - 1:width, :]
    for k in range(1, width):
        conv = conv + _shift_rows(u, prev, k) * wc_ref[width - 1 - k:width - k, :]
    prev_s[...] = u[u.shape[0] - SUBLANE:, :]
    nb_ref[0] = u[u.shape[0] - (width - 1):, :]
    y_ref[0] = x_ref[0] + gate_ref[0] * _dot((gb * conv).astype(BF16), wo_ref[...])


def _sconv_prompt(x, shift, scale, g, w_in_bf16, gate, buf, w_conv, w_out_bf16, tm):
    bsz, seq, d = x.shape
    width = w_conv.shape[0]
    return pl.pallas_call(
        functools.partial(_sconv_prompt_kernel, width=width),
        grid=(bsz, seq // tm),
        in_specs=[pl.BlockSpec((1, tm, d), lambda b, i: (b, i, 0)),
                  _row_block(shift, tm), _row_block(scale, tm), _const2((1, d)),
                  _resident(w_in_bf16.shape), _row_block(gate, tm),
                  pl.BlockSpec((1, width - 1, d), lambda b, i: (b, 0, 0)),
                  _const2((width, d)), _resident((d, d))],
        out_specs=[pl.BlockSpec((1, tm, d), lambda b, i: (b, i, 0)),
                   pl.BlockSpec((1, width - 1, d), lambda b, i: (b, 0, 0))],
        out_shape=[jax.ShapeDtypeStruct((bsz, seq, d), F32),
                   jax.ShapeDtypeStruct((bsz, width - 1, d), F32)],
        scratch_shapes=[pltpu.VMEM((SUBLANE, d), F32)],
        compiler_params=_cparams(2),
        name="sconv_prompt",
    )(x, shift, scale, g.reshape(1, d), w_in_bf16, gate, buf, w_conv, w_out_bf16)


def _sconv_sample_kernel(p_ref, x_ref, gate_ref, buf_ref, wc_ref, wo_ref, y_ref, nb_ref,
                         *, width, nb, steps):
    d = x_ref.shape[-1]
    p = p_ref[...]
    gb, gc, xh = p[:, :d], p[:, d:2 * d], p[:, 2 * d:]
    u = gc * xh
    ext = jnp.concatenate([buf_ref[...], u], axis=0)
    conv = ext[0:steps * nb] * wc_ref[0:1, :]
    for k in range(1, width):
        conv = conv + ext[k * nb:(k + steps) * nb] * wc_ref[k:k + 1, :]
    nb_ref[...] = ext[steps * nb:]
    y_ref[...] = x_ref[...] + gate_ref[...] * _dot((gb * conv).astype(BF16), wo_ref[...])


def _sconv_sample(p_tb, x_tb, gate_tb, buf_kb, w_conv, w_out_bf16, nb, steps):
    rows, d = x_tb.shape
    width = w_conv.shape[0]
    return pl.pallas_call(
        functools.partial(_sconv_sample_kernel, width=width, nb=nb, steps=steps),
        out_shape=[jax.ShapeDtypeStruct((rows, d), F32),
                   jax.ShapeDtypeStruct(((width - 1) * nb, d), F32)],
        compiler_params=pltpu.CompilerParams(vmem_limit_bytes=VMEM_LIMIT),
        name="sconv_sample",
    )(p_tb, x_tb, gate_tb, buf_kb, w_conv, w_out_bf16)


KEY_NEG_INF = -2139095041
KEY_POS_INF = 2139095040


def _code_to_f32(key):
    key = jnp.clip(key, KEY_NEG_INF, KEY_POS_INF)
    return lax.bitcast_convert_type(key ^ ((key >> 31) & 0x7FFFFFFF), F32)


def _stack_heads(qcat):
    w = 4 * IDX_DIM
    return jnp.concatenate([qcat[:, h * w:(h + 1) * w] for h in range(IDX_HEADS)], axis=0)


def _idx_rhs(ki):
    kh, kl = _split(ki)
    return jnp.concatenate([kh, kh, kl, jnp.zeros(kh.shape, BF16)], axis=1)


def _idx_score(dts, wcols, nrows):
    s = wcols[0] * jnp.maximum(dts[0:nrows], 0.0)
    for h in range(1, IDX_HEADS):
        s = s + wcols[h] * jnp.maximum(dts[h * nrows:(h + 1) * nrows], 0.0)
    return s


def _idx_wcols(kw):
    sc = (IDX_HEADS ** -0.5) * (IDX_DIM ** -0.5)
    return [kw[:, IDX_DIM + h:IDX_DIM + h + 1] * sc for h in range(IDX_HEADS)]


def _topk_select(count, rows, lanes, topk, idx_bits, stash=lambda v: (lambda: v)):
    wide = lambda v: jnp.broadcast_to(v, (rows, lanes))

    def bit_step(it, tu):
        mask = jnp.left_shift(jnp.int32(1), 31 - it)
        cand_u = tu | mask
        cand = stash(wide(_code_to_f32(cand_u ^ INT_MIN)))
        cnt = count(lambda c, sc, idx: jnp.where(sc >= cand(), 1.0, 0.0))
        return jnp.where(cnt >= topk, cand_u, tu)

    t = _code_to_f32(lax.fori_loop(0, 32, bit_step, jnp.zeros((rows, 1), I32)) ^ INT_MIN)
    tw = wide(t)
    n_gt = count(lambda c, sc, idx: jnp.where(sc > tw, 1.0, 0.0))
    n_ge = count(lambda c, sc, idx: jnp.where(sc >= tw, 1.0, 0.0))
    need = topk - n_gt
    tied = jnp.max(jnp.where(t > -jnp.inf, n_ge - topk, 0.0)) > 0.0

    def tie_search(_):
        def idx_step(it, j):
            cand = j | jnp.left_shift(jnp.int32(1), idx_bits - 1 - it)
            cw = wide(cand)
            cnt = count(lambda c, sc, idx: jnp.where((sc == tw) & (idx < cw), 1.0, 0.0))
            return jnp.where(cnt < need, cand, j)
        return lax.fori_loop(0, idx_bits, idx_step, jnp.zeros((rows, 1), I32))

    j = lax.cond(tied, tie_search, lambda _: jnp.full((rows, 1), 2 ** idx_bits, I32), 0)
    return t, j


def _dsa_prompt_kernel(q_ref, qcat_ref, kwq_ref, kcat_ref, kaug_ref, vaug_ref, bias_ref, o_ref,
                       sc_s, madd_s, wbc_s, cand_s, qa_s, lg_s, mx_s, sh_s, acc_s, *, tq, topk, idx_bits):
    i = pl.program_id(1)
    nk = i + 1
    nk2 = (nk + 1) // 2
    r = N_HEADS // N_KV_HEADS

    for h, w in enumerate(_idx_wcols(kwq_ref[0])):
        wbc_s[h] = jnp.broadcast_to(w, (tq, tq))
    qrow = lax.broadcasted_iota(I32, (tq, tq), 0)
    kcol = lax.broadcasted_iota(I32, (tq, tq), 1)
    qpos = i * tq + qrow
    wq = 4 * IDX_DIM

    def score_pair(c2, _):
        kc = kcat_ref[0, c2]
        s = None
        for hp in range(IDX_HEADS // 2):
            lhs = jnp.concatenate([qcat_ref[0, :, (2 * hp + v) * wq:(2 * hp + v + 1) * wq] for v in range(2)], axis=0)
            dd = jnp.maximum(_dot(lhs, kc), 0.0)
            for v in range(2):
                w = wbc_s[2 * hp + v]
                d = jnp.concatenate([w, w], axis=1) * dd[v * tq:(v + 1) * tq]
                s = d if s is None else s + d
        for u in range(2):
            c = 2 * c2 + u
            sc_s[c] = jnp.where(c * tq + kcol <= qpos, s[:, u * tq:(u + 1) * tq], -jnp.inf)
        return 0

    lax.fori_loop(0, nk2, score_pair, 0)

    def count(fn):
        def body(c2, acc):
            c = 2 * c2
            acc = acc + fn(c, sc_s[c], c * tq + kcol)
            return acc + fn(c + 1, sc_s[c + 1], (c + 1) * tq + kcol)
        acc = lax.fori_loop(0, nk2, body, jnp.zeros((tq, tq), F32))
        return jnp.sum(acc, axis=1, keepdims=True)

    def stash(v):
        cand_s[...] = v
        return lambda: cand_s[...]

    t, j = _topk_select(count, tq, tq, float(topk), idx_bits, stash)
    tw = jnp.broadcast_to(t, (tq, tq))
    jw = jnp.broadcast_to(j, (tq, tq))

    def mask_pair(c2, _):
        for u in range(2):
            c = 2 * c2 + u
            sc = sc_s[c]
            sel = (sc > tw) | ((sc == tw) & (c * tq + kcol <= jw))
            madd_s[c] = jnp.where(sel & (sc > -jnp.inf), 0.0, NEG_BIG)
        return 0

    lax.fori_loop(0, nk2, mask_pair, 0)

    q = q_ref[0]
    zpad = jnp.zeros((tq, LANE - HEAD_DIM), F32)
    for g in range(N_KV_HEADS):
        qa_s[g] = jnp.concatenate(
            [jnp.concatenate([q[:, (g * r + u) * HEAD_DIM:(g * r + u + 1) * HEAD_DIM], zpad], axis=1)
             for u in range(r)], axis=0).astype(BF16)

    def pair_mask(c2, near, g):
        halves = []
        for u in range(2):
            c = 2 * c2 + u
            m = madd_s[c]
            if near:
                back = nk - 1 - c
                carries = (back == 0) | ((back == 1) & (kcol > qrow))
                halves.append([m + jnp.where(carries, bias_ref[g * r + v], 0.0) for v in range(r)])
            else:
                halves.append([m] * r)
        return jnp.concatenate([jnp.concatenate([halves[0][v], halves[1][v]], axis=1) for v in range(r)], axis=0)

    ngh = lg_s.shape[0]
    for half in range(N_KV_HEADS // ngh):
        groups = [half * ngh + k for k in range(ngh)]
        mx_s[...] = jnp.full(mx_s.shape, NEG_BIG, F32)

        def logits_pair(c2, near):
            for k, g in enumerate(groups):
                lg = _dot(qa_s[g], kaug_ref[0, c2, g * LANE:(g + 1) * LANE, :]) + pair_mask(c2, near, g)
                lg_s[k, c2] = lg
                mx_s[k] = jnp.maximum(mx_s[k], jnp.maximum(lg[:, :tq], lg[:, tq:]))

        def far(c4, _):
            logits_pair(2 * c4, False)
            logits_pair(2 * c4 + 1, False)
            return 0

        nfar = jnp.maximum(nk2 - 2, 0)
        lax.fori_loop(0, nfar // 2, far, 0)

        @pl.when(nfar % 2 == 1)
        def _():
            logits_pair(nfar - 1, False)

        @pl.when(nk2 >= 2)
        def _():
            logits_pair(nk2 - 2, True)

        logits_pair(nk2 - 1, True)

        for k in range(ngh):
            sh_s[k] = jnp.broadcast_to(jnp.max(mx_s[k], axis=1, keepdims=True), (r * tq, LANE))
        acc_s[...] = jnp.zeros(acc_s.shape, F32)

        def pv_pair(c2):
            r0 = pl.multiple_of(c2 * 2 * tq, 2 * tq)
            va = vaug_ref[0, pl.ds(r0, 2 * tq), :]
            for k, g in enumerate(groups):
                s = sh_s[k]
                p = jnp.exp(lg_s[k, c2] - jnp.concatenate([s, s], axis=1))
                acc_s[k] = acc_s[k] + _dot(p.astype(BF16), va[:, g * LANE:(g + 1) * LANE])

        def pv_two(c4, _):
            pv_pair(2 * c4)
            pv_pair(2 * c4 + 1)
            return 0

        lax.fori_loop(0, nk2 // 2, pv_two, 0)

        @pl.when(nk2 % 2 == 1)
        def _():
            pv_pair(nk2 - 1)
        for k, g in enumerate(groups):
            acc = acc_s[k]
            o = acc[:, 0:HEAD_DIM] / acc[:, HEAD_DIM:HEAD_DIM + 1]
            heads = jnp.concatenate([o[u * tq:(u + 1) * tq, :] for u in range(r)], axis=1)
            o_ref[0, :, g * r * HEAD_DIM:(g + 1) * r * HEAD_DIM] = heads.astype(o_ref.dtype)


def _t5_bucket(dist):
    n = jnp.maximum(dist, 0)
    max_exact = N_BUCKETS // 2
    nf = jnp.maximum(n, max_exact).astype(F32)
    large = max_exact + (jnp.log(nf / max_exact) / math.log(MAX_DISTANCE / max_exact)
                         * (N_BUCKETS - max_exact)).astype(I32)
    large = jnp.minimum(large, N_BUCKETS - 1)
    return jnp.where(n < max_exact, n, large)


def _bias_table(rel_bias, n):
    tab = rel_bias[_t5_bucket(jnp.arange(n, dtype=I32))]
    return (tab - rel_bias[N_BUCKETS - 1][None, :]).T


DSA_GROUPS_PER_PASS = 2


def _dsa_prompt(q, qcat, kw, kcat, kaug, vaug, rel_bias, tq):
    bsz, seq, nq = q.shape
    topk = min(TOPK_MAX, seq // 4)
    nkc = seq // tq
    r = N_HEADS // N_KV_HEADS
    assert tq == MAX_DISTANCE and seq % (2 * tq) == 0
    tab = _bias_table(rel_bias, tq)
    dmod = (np.arange(tq)[:, None] - np.arange(tq)[None, :]) % tq
    onehot = (jnp.arange(tq, dtype=I32)[:, None, None] == jnp.asarray(dmod, I32)[None]).astype(F32)
    bias = jnp.einsum('hd,dqk->hqk', tab, onehot, precision=lax.Precision.HIGHEST)
    assert kcat.shape[1:] == (seq // (2 * tq), 4 * IDX_DIM, 2 * tq)
    row = lambda n: pl.BlockSpec((1, tq, n), lambda b, i: (b, i, 0))
    tiles = lambda a: pl.BlockSpec((1,) + a.shape[1:], lambda b, i: (b, 0, 0, 0), pipeline_mode=pl.Buffered(1))
    return pl.pallas_call(
        functools.partial(_dsa_prompt_kernel, tq=tq, topk=topk, idx_bits=int(math.log2(seq))),
        grid=(bsz, nkc),
        in_specs=[row(nq), row(qcat.shape[-1]), row(LANE), tiles(kcat), tiles(kaug),
                  pl.BlockSpec((1, seq, vaug.shape[-1]), lambda b, i: (b, 0, 0), pipeline_mode=pl.Buffered(1)),
                  _resident((N_HEADS, tq, tq))],
        out_specs=row(nq),
        out_shape=jax.ShapeDtypeStruct((bsz, seq, nq), BF16),
        scratch_shapes=[pltpu.VMEM((nkc, tq, tq), F32), pltpu.VMEM((nkc, tq, tq), F32),
                        pltpu.VMEM((IDX_HEADS, tq, tq), F32), pltpu.VMEM((tq, tq), F32),
                        pltpu.VMEM((N_KV_HEADS, r * tq, LANE), BF16),
                        pltpu.VMEM((DSA_GROUPS_PER_PASS, nkc // 2, r * tq, 2 * tq), F32),
                        pltpu.VMEM((DSA_GROUPS_PER_PASS, r * tq, tq), F32),
                        pltpu.VMEM((DSA_GROUPS_PER_PASS, r * tq, LANE), F32),
                        pltpu.VMEM((DSA_GROUPS_PER_PASS, r * tq, LANE), F32)],
        compiler_params=_cparams(2),
        name="dsa_prompt",
    )(q, qcat, kw, kcat, kaug, vaug, bias)


QP = SUBLANE
PAGES_PER_STEP = 16


def _page_specs(shape4, layer, pps):
    def spec(u):
        return pl.BlockSpec((1, 1) + tuple(shape4[2:]),
                            lambda b, g, pt: (layer, pt[b, g * pps + u], 0, 0))
    return [spec(u) for u in range(pps)]


def _pages_t(cache):
    nd = cache.ndim
    t = jnp.transpose(cache, (0, 1) + tuple(range(3, nd)) + (2,))
    return t.reshape(t.shape[0], t.shape[1], -1, t.shape[-1])


def _pad_t(a, page):
    return jnp.concatenate([a, jnp.zeros((page - QP, a.shape[1]), F32)], axis=0).T


def _dsa_sample_score_kernel(pt_ref, qcat_ref, kw_ref, kcn_ref, *rest, pps):
    page_refs, o_ref, on_ref = rest[:pps], rest[pps], rest[pps + 1]
    lhs = _stack_heads(qcat_ref[0]).astype(BF16)
    wcols = _idx_wcols(kw_ref[0])
    kt = jnp.concatenate([page_refs[u][0, 0] for u in range(pps)], axis=1)
    kh, kl = _split(kt)
    rhs = jnp.concatenate([kh, kh, kl, jnp.zeros(kh.shape, BF16)], axis=0)
    o_ref[0] = _idx_score(_dot(lhs, rhs), wcols, QP)

    @pl.when(pl.program_id(1) == pl.num_programs(1) - 1)
    def _():
        on_ref[0] = _idx_score(_dot(lhs, _pad_t(kcn_ref[0], on_ref.shape[-1]).astype(BF16)), wcols, QP)


def _dsa_sample_scores(page_table, qcat, kw, kcn, kidx_t, layer, pps):
    nb, n_pages = page_table.shape
    page = kidx_t.shape[-1]
    bmap = lambda b, g, pt: (b, 0, 0)
    grid_spec = pltpu.PrefetchScalarGridSpec(
        num_scalar_prefetch=1,
        grid=(nb, n_pages // pps),
        in_specs=[pl.BlockSpec((1, QP, qcat.shape[-1]), bmap), pl.BlockSpec((1, QP, LANE), bmap),
                  pl.BlockSpec((1, QP, kcn.shape[-1]), bmap)] + _page_specs(kidx_t.shape, layer, pps),
        out_specs=[pl.BlockSpec((1, QP, pps * page), lambda b, g, pt: (b, 0, g)),
                   pl.BlockSpec((1, QP, page), bmap)])
    return pl.pallas_call(
        functools.partial(_dsa_sample_score_kernel, pps=pps),
        grid_spec=grid_spec,
        out_shape=[jax.ShapeDtypeStruct((nb, QP, n_pages * page), F32),
                   jax.ShapeDtypeStruct((nb, QP, page), F32)],
        compiler_params=_cparams(2),
        name="dsa_sample_scores",
    )(page_table, qcat, kw, kcn, *([kidx_t] * pps))


def _new_token_scores(s_new, steps):
    lane = lax.broadcasted_iota(I32, s_new.shape, 1)
    q = lax.broadcasted_iota(I32, s_new.shape, 0) % QP
    return jnp.where((lane <= q) & (lane < steps), s_new, -jnp.inf)


def _dsa_sample_select_kernel(sc_ref, sn_ref, t_ref, j_ref, *, steps, topk, idx_bits):
    scores = jnp.concatenate([sc_ref[...], _new_token_scores(sn_ref[...], steps)], axis=1)
    idx = lax.broadcasted_iota(I32, scores.shape, 1)
    count = lambda fn: jnp.sum(fn(0, scores, idx), axis=1, keepdims=True)
    t, j = _topk_select(count, scores.shape[0], scores.shape[1], float(topk), idx_bits)
    t_ref[...] = t
    j_ref[...] = j


def _dsa_sample_select(scores, snew, steps, topk):
    nb, _, past = scores.shape
    page = snew.shape[-1]
    rows = nb * QP
    rb = math.gcd(rows, 64)
    t, j = pl.pallas_call(
        functools.partial(_dsa_sample_select_kernel, steps=steps, topk=topk,
                          idx_bits=int(math.ceil(math.log2(past + page)))),
        grid=(rows // rb,),
        in_specs=[pl.BlockSpec((rb, past), lambda i: (i, 0)), pl.BlockSpec((rb, page), lambda i: (i, 0))],
        out_specs=[pl.BlockSpec((rb, 1), lambda i: (i, 0)), pl.BlockSpec((rb, 1), lambda i: (i, 0))],
        out_shape=[jax.ShapeDtypeStruct((rows, 1), F32), jax.ShapeDtypeStruct((rows, 1), I32)],
        compiler_params=_cparams(1),
        name="dsa_sample_select",
    )(scores.reshape(rows, past), snew.reshape(rows, page))
    return t.reshape(nb, QP, 1), j.reshape(nb, QP, 1)


def _dsa_sample_attend_kernel(pt_ref, q_ref, kn_ref, vn_ref, sc_ref, sn_ref, t_ref, j_ref,
                              blast_ref, bnew_ref, *rest, pps, n_pages, steps):
    k_refs, v_refs, o_ref = rest[:pps], rest[pps:2 * pps], rest[2 * pps]
    lhs_s, m_s, l_s, acc_s = rest[2 * pps + 1:]
    g = pl.program_id(1)
    last = pl.num_programs(1) - 1
    page = k_refs[0].shape[-1]
    past = n_pages * page
    n = pps * page
    r = N_HEADS // N_KV_HEADS
    grow = r * QP

    @pl.when(g == 0)
    def _():
        m_s[...] = jnp.full(m_s.shape, NEG_BIG, F32)
        l_s[...] = jnp.zeros(l_s.shape, F32)
        acc_s[...] = jnp.zeros(acc_s.shape, F32)
        q = q_ref[0]
        for gg in range(N_KV_HEADS):
            lhs_s[gg] = jnp.concatenate(
                [q[:, (gg * r + u) * HEAD_DIM:(gg * r + u + 1) * HEAD_DIM] for u in range(r)], axis=0).astype(BF16)

    t, j = t_ref[0], j_ref[0]

    def select(sc, idx):
        sel = (sc > t) | ((sc == t) & (idx <= j))
        return jnp.where(sel & (sc > -jnp.inf), 0.0, NEG_BIG)

    def attend(kt, vt, madd, bias):
        mrows = jnp.concatenate([madd] * r, axis=0)
        for gg in range(N_KV_HEADS):
            rows = slice(gg * grow, (gg + 1) * grow)
            feat = slice(gg * HEAD_DIM, (gg + 1) * HEAD_DIM)
            lg = _dot(lhs_s[gg], kt[feat, :]) + mrows
            if bias is not None:
                lg = lg + bias[rows, :]
            m_old = m_s[rows, :]
            m_new = jnp.maximum(m_old, jnp.max(lg, axis=1, keepdims=True))
            p = jnp.exp(lg - m_new)
            alpha = jnp.exp(m_old - m_new)
            l_s[rows, :] = alpha * l_s[rows, :] + jnp.sum(p, axis=1, keepdims=True)
            acc_s[rows, :] = alpha * acc_s[rows, :] + _dot_nt(p.astype(BF16), vt[feat, :])
            m_s[rows, :] = m_new

    def pages(bias):
        kt = jnp.concatenate([k_refs[u][0, 0] for u in range(pps)], axis=1).astype(BF16)
        vt = jnp.concatenate([v_refs[u][0, 0] for u in range(pps)], axis=1).astype(BF16)
        idx = g * n + lax.broadcasted_iota(I32, (QP, n), 1)
        attend(kt, vt, select(sc_ref[0], idx), bias)

    @pl.when(g != last)
    def _():
        pages(None)

    @pl.when(g == last)
    def _():
        pages(jnp.concatenate([jnp.zeros((N_HEADS * QP, n - page), F32), blast_ref[...]], axis=1))
        idx = past + lax.broadcasted_iota(I32, (QP, page), 1)
        attend(_pad_t(kn_ref[0], page).astype(BF16), _pad_t(vn_ref[0], page).astype(BF16),
               select(_new_token_scores(sn_ref[0], steps), idx), bnew_ref[...])
        o = acc_s[...] / l_s[...]
        for h in range(N_HEADS):
            o_ref[0, :, h * HEAD_DIM:(h + 1) * HEAD_DIM] = o[h * QP:(h + 1) * QP, :]


def _dsa_sample(page_table, q, kn, vn, scores, snew, k_t, v_t, rel_bias, layer, steps, pps):
    nb, n_pages = page_table.shape
    page = k_t.shape[-1]
    past = n_pages * page
    nq, nkv = N_HEADS * HEAD_DIM, N_KV_HEADS * HEAD_DIM
    t, j = _dsa_sample_select(scores, snew, steps, min(TOPK_MAX, (past + steps) // 4))
    n_steps = n_pages // pps
    r = N_HEADS // N_KV_HEADS
    tab = _bias_table(rel_bias, 2 * page)
    qq = np.arange(QP)[:, None]
    off = np.arange(page)[None, :]
    blast = tab[:, page + qq - off].reshape(N_HEADS * QP, page)
    bnew = tab[:, np.maximum(qq - off, 0)].reshape(N_HEADS * QP, page)
    bmap = lambda b, g, pt: (b, 0, 0)
    grid_spec = pltpu.PrefetchScalarGridSpec(
        num_scalar_prefetch=1,
        grid=(nb, n_steps),
        in_specs=[pl.BlockSpec((1, QP, nq), bmap),
                  pl.BlockSpec((1, QP, nkv), bmap),
                  pl.BlockSpec((1, QP, nkv), bmap),
                  pl.BlockSpec((1, QP, pps * page), lambda b, g, pt: (b, 0, g)),
                  pl.BlockSpec((1, QP, page), bmap),
                  pl.BlockSpec((1, QP, 1), bmap), pl.BlockSpec((1, QP, 1), bmap),
                  pl.BlockSpec(blast.shape, lambda b, g, pt: (0, 0)),
                  pl.BlockSpec(bnew.shape, lambda b, g, pt: (0, 0))]
                 + _page_specs(k_t.shape, layer, pps) + _page_specs(v_t.shape, layer, pps),
        out_specs=pl.BlockSpec((1, QP, nq), bmap),
        scratch_shapes=[pltpu.VMEM((N_KV_HEADS, r * QP, HEAD_DIM), BF16),
                        pltpu.VMEM((N_HEADS * QP, 1), F32), pltpu.VMEM((N_HEADS * QP, 1), F32),
                        pltpu.VMEM((N_HEADS * QP, HEAD_DIM), F32)])
    return pl.pallas_call(
        functools.partial(_dsa_sample_attend_kernel, pps=pps, n_pages=n_pages, steps=steps),
        grid_spec=grid_spec,
        out_shape=jax.ShapeDtypeStruct((nb, QP, nq), F32),
        compiler_params=_cparams(2),
        name="dsa_sample_attend",
    )(page_table, q, kn, vn, scores, snew, t, j, blast, bnew, *([k_t] * pps), *([v_t] * pps))


SSD_COLS = 512


def _softplus(x):
    return jnp.maximum(x, 0.0) + jnp.log1p(jnp.exp(-jnp.abs(x)))


def _split3(a):
    p1 = a.astype(BF16)
    r1 = a - p1.astype(F32)
    p2 = r1.astype(BF16)
    return p1, p2, (r1 - p2.astype(F32)).astype(BF16)


def _dot_sel_rhs(a, e):
    p1, p2, p3 = _split3(a)
    return _dot(p1, e) + (_dot(p2, e) + _dot(p3, e))


def _dot_sel_lhs(e, a):
    p1, p2, p3 = _split3(a)
    return _dot(e, p1) + (_dot(e, p2) + _dot(e, p3))


def _ssd_prompt_kernel(z_ref, xbc_ref, dtr_ref, cw_ref, cb_ref, dtb_ref, alog_ref, dexp_ref, ng_ref,
                       e_ref, y_ref, nbuf_ref, hf_ref, prev_s, xc_s, y_s, h_s, *, width, nh, inner):
    i = pl.program_id(1)
    last = pl.num_programs(1) - 1
    qn, cdim = xbc_ref.shape[1], xbc_ref.shape[2]
    ngrp, ns, hd = SSD_GROUPS, SSD_STATE, SSD_HEADDIM
    hpg = nh // ngrp

    @pl.when(i == 0)
    def _():
        prev_s[...] = jnp.zeros(prev_s.shape, F32)
        h_s[...] = jnp.zeros(h_s.shape, F32)

    for cb in range(cdim // SSD_COLS):
        cols = slice(cb * SSD_COLS, (cb + 1) * SSD_COLS)
        xr = xbc_ref[0, :, cols]
        prev = prev_s[:, cols]
        conv = xr * cw_ref[width - 1:width, cols]
        for k in range(1, width):
            conv = conv + _shift_rows(xr, prev, k) * cw_ref[width - 1 - k:width - k, cols]
        xc_s[:, cols] = _silu(conv + cb_ref[:, cols])
        prev_s[:, cols] = xr[qn - SUBLANE:, :]
        nbuf_ref[0, :, cols] = xr[qn - (width - 1):, :]

    hl = lax.broadcasted_iota(I32, (qn, LANE), 1)
    dt = jnp.where(hl < nh, _softplus(dtr_ref[0] + dtb_ref[...]), 0.0)
    dta = dt * (-jnp.exp(alog_ref[...]))
    qrow = lax.broadcasted_iota(I32, (qn, qn), 0)
    kcol = lax.broadcasted_iota(I32, (qn, qn), 1)
    causal = kcol <= qrow
    cum = _dot_sel_lhs(jnp.where(causal, 1.0, 0.0).astype(BF16), dta)
    cum_t = cum.T
    cum_last = cum[qn - 1:qn, :]
    ecum = jnp.exp(cum)
    dtw = dt * jnp.exp(cum_last - cum)
    e_last = jnp.exp(cum_last)
    lane_p = lax.broadcasted_iota(I32, (qn, LANE), 1)
    row_p = lax.broadcasted_iota(I32, (LANE, ns), 0)

    for g in range(ngrp):
        gl = slice(g * SSD_COLS, (g + 1) * SSD_COLS)
        e_g = e_ref[:, gl]
        xs_g = xc_s[:, gl]
        dtx_g = xs_g * _dot_sel_rhs(dt, e_g)
        dtxw_g = xs_g * _dot_sel_rhs(dtw, e_g)
        ecx_g = _dot_sel_rhs(ecum, e_g)
        bm = xc_s[:, inner + g * ns:inner + (g + 1) * ns].astype(BF16)
        cm = xc_s[:, inner + (ngrp + g) * ns:inner + (ngrp + g + 1) * ns].astype(BF16)
        cbm = _dot_nt(cm, bm)
        for pi in range(hpg // 2):
            ha = g * hpg + 2 * pi
            pls = slice(pi * LANE, (pi + 1) * LANE)
            lanes = slice(ha * hd, ha * hd + LANE)
            dtx_p = dtx_g[:, pls].astype(BF16)
            yds = []
            for h in (ha, ha + 1):
                seg = cum[:, h:h + 1] - cum_t[h:h + 1, :]
                dec = jnp.exp(jnp.where(causal, seg, -jnp.inf))
                yds.append(_dot((cbm * dec).astype(BF16), dtx_p))
            yd = jnp.where(lane_p < hd, yds[0], yds[1])
            hp = h_s[ha // 2]
            yo = _dot_nt(cm, hp.astype(BF16)) * ecx_g[:, pls]
            y_s[:, lanes] = yd + yo + dexp_ref[:, lanes] * xs_g[:, pls]
            s_new = _dot(dtxw_g[:, pls].T.astype(BF16), bm)
            cd = jnp.where(row_p < hd, e_last[:, ha:ha + 1], e_last[:, ha + 1:ha + 2])
            h_s[ha // 2] = hp * cd + s_new

    for g in range(ngrp):
        gl = slice(g * SSD_COLS, (g + 1) * SSD_COLS)
        yg = y_s[:, gl] * _silu(z_ref[0, :, gl])
        ms = jnp.mean(yg * yg, axis=-1, keepdims=True)
        y_ref[0, :, gl] = ((yg * lax.rsqrt(ms + EPS)) * ng_ref[:, gl]).astype(y_ref.dtype)

    @pl.when(i == last)
    def _():
        hf_ref[0] = h_s[...]


def _head_expand(nh, hd):
    e = np.zeros((LANE, nh * hd), np.float32)
    for h in range(nh):
        e[h, h * hd:(h + 1) * hd] = 1.0
    return jnp.asarray(e, BF16)


def _pad_lanes(v, n=LANE):
    v = v.reshape(1, -1)
    return jnp.pad(v, ((0, 0), (0, n - v.shape[1])))


def _ssd_prompt(z, xbc, dtr, conv_w, conv_b, dt_bias, a_log, d_skip, norm_g):
    bsz, seq, inner = z.shape
    cdim = xbc.shape[-1]
    nh = dt_bias.shape[0]
    width = conv_w.shape[0]
    qn = math.gcd(seq, SSD_CHUNK)
    assert qn == SSD_CHUNK and inner == SSD_GROUPS * SSD_COLS and cdim % SSD_COLS == 0
    dexp = jnp.repeat(d_skip, SSD_HEADDIM).reshape(1, inner)
    blk = lambda n: pl.BlockSpec((1, qn, n), lambda b, i: (b, i, 0))
    y, nbuf, hf = pl.pallas_call(
        functools.partial(_ssd_prompt_kernel, width=width, nh=nh, inner=inner),
        grid=(bsz, seq // qn),
        in_specs=[blk(inner), blk(cdim), blk(LANE),
                  _const2((width, cdim)), _const2((1, cdim)), _const2((1, LANE)), _const2((1, LANE)),
                  _const2((1, inner)), _const2((1, inner)), _const2((LANE, inner))],
        out_specs=[blk(inner),
                   pl.BlockSpec((1, width - 1, cdim), lambda b, i: (b, 0, 0)),
                   pl.BlockSpec((1, nh // 2, LANE, SSD_STATE), lambda b, i: (b, 0, 0, 0))],
        out_shape=[jax.ShapeDtypeStruct((bsz, seq, inner), BF16),
                   jax.ShapeDtypeStruct((bsz, width - 1, cdim), F32),
                   jax.ShapeDtypeStruct((bsz, nh // 2, LANE, SSD_STATE), F32)],
        scratch_shapes=[pltpu.VMEM((SUBLANE, cdim), F32), pltpu.VMEM((qn, cdim), F32),
                        pltpu.VMEM((qn, inner), F32), pltpu.VMEM((nh // 2, LANE, SSD_STATE), F32)],
        compiler_params=_cparams(2),
        name="ssd_prompt",
    )(z, xbc, dtr, conv_w, conv_b.reshape(1, cdim), _pad_lanes(dt_bias), _pad_lanes(a_log),
      dexp, norm_g.reshape(1, inner), _head_expand(nh, SSD_HEADDIM))
    return y, nbuf, hf.reshape(bsz, nh, SSD_HEADDIM, SSD_STATE)


def _ssd_prep_sample_kernel(xbc_ref, buf_ref, dtr_ref, cw_ref, cb_ref, dtb_ref, xc_ref, dt_ref, nbuf_ref,
                            *, width, nb, steps, nh):
    cdim = xbc_ref.shape[1]
    for cb in range(cdim // SSD_COLS):
        cols = slice(cb * SSD_COLS, (cb + 1) * SSD_COLS)
        ext = jnp.concatenate([buf_ref[:, cols], xbc_ref[:, cols]], axis=0)
        conv = ext[0:steps * nb] * cw_ref[0:1, cols]
        for k in range(1, width):
            conv = conv + ext[k * nb:(k + steps) * nb] * cw_ref[k:k + 1, cols]
        xc_ref[:, cols] = _silu(conv + cb_ref[:, cols])
        nbuf_ref[:, cols] = ext[steps * nb:]
    hl = lax.broadcasted_iota(I32, dtr_ref.shape, 1)
    dt_ref[...] = jnp.where(hl < nh, _softplus(dtr_ref[...] + dtb_ref[...]), 0.0)


def _ssd_prep_sample(xbc_tb, buf_kb, dtr_tb, conv_w, conv_b, dt_bias, nb, steps):
    rows, cdim = xbc_tb.shape
    width = conv_w.shape[0]
    return pl.pallas_call(
        functools.partial(_ssd_prep_sample_kernel, width=width, nb=nb, steps=steps, nh=dt_bias.shape[0]),
        out_shape=[jax.ShapeDtypeStruct((rows, cdim), F32),
                   jax.ShapeDtypeStruct((rows, LANE), F32),
                   jax.ShapeDtypeStruct(((width - 1) * nb, cdim), F32)],
        compiler_params=pltpu.CompilerParams(vmem_limit_bytes=VMEM_LIMIT),
        name="ssd_prep_sample",
    )(xbc_tb, buf_kb, dtr_tb, conv_w, conv_b.reshape(1, cdim), _pad_lanes(dt_bias))


def _ssd_scan_sample_kernel(xc_ref, dt_ref, z_ref, h0_ref, alog_ref, dexp_ref, ng_ref, e_ref,
                            y_ref, hf_ref, *, steps, inner):
    ngrp, ns = SSD_GROUPS, SSD_STATE
    xc, dt = xc_ref[0], dt_ref[0]
    row = lax.broadcasted_iota(I32, (QP, LANE), 0)
    cum = dt * (-jnp.exp(alog_ref[...]))
    for sft in (1, 2, 4):
        cum = cum + jnp.where(row >= sft, pltpu.roll(cum, sft, 0), 0.0)
    cum_last = cum[QP - 1:QP, :]
    parts = [dt, jnp.exp(cum), dt * jnp.exp(cum_last - cum), jnp.broadcast_to(jnp.exp(cum_last), (QP, LANE))]
    for s in range(steps):
        parts.append(jnp.where(row >= s, jnp.exp(cum - cum[s:s + 1, :]), 0.0))
    stack = jnp.concatenate(parts, axis=0)

    for g in range(ngrp):
        gl = slice(g * SSD_COLS, (g + 1) * SSD_COLS)
        ex = _dot_sel_rhs(stack, e_ref[:, gl])
        dt_x, ecum_x, dtw_x, el_x = (ex[k * QP:(k + 1) * QP] for k in range(4))
        xs_g = xc[:, gl]
        dtx = xs_g * dt_x
        bm = xc[:, inner + g * ns:inner + (g + 1) * ns]
        cm = xc[:, inner + (ngrp + g) * ns:inner + (ngrp + g + 1) * ns].astype(BF16)
        bmp = jnp.concatenate([bm, jnp.zeros((LANE - QP, ns), F32)], axis=0).astype(BF16)
        cbm = _dot_nt(cm, bmp)
        yd = jnp.zeros((QP, SSD_COLS), F32)
        for s in range(steps):
            yd = yd + (ex[(4 + s) * QP:(5 + s) * QP] * cbm[:, s:s + 1]) * dtx[s:s + 1, :]
        h0g = h0_ref[0, gl, :]
        yo = _dot_nt(cm, h0g.astype(BF16)) * ecum_x
        y = (yd + yo + dexp_ref[:, gl] * xs_g) * _silu(z_ref[0, :, gl])
        ms = jnp.mean(y * y, axis=-1, keepdims=True)
        y_ref[0, :, gl] = (y * lax.rsqrt(ms + EPS)) * ng_ref[:, gl]
        tm = jnp.concatenate([xs_g * dtw_x, el_x[0:1], jnp.zeros((LANE - QP - 1, SSD_COLS), F32)], axis=0)
        tt = tm.T
        hf_ref[0, gl, :] = h0g * tt[:, QP:QP + 1] + _dot(tt.astype(BF16), bmp)


def _ssd_scan_sample(xc, dt, z, h0, a_log, d_skip, norm_g, steps):
    nb, _, cdim = xc.shape
    inner = z.shape[-1]
    nh = a_log.shape[0]
    rows_h = nh * SSD_HEADDIM
    dexp = jnp.repeat(d_skip, SSD_HEADDIM).reshape(1, inner)
    one = lambda shape: pl.BlockSpec(shape, lambda b: (0,) * len(shape))
    return pl.pallas_call(
        functools.partial(_ssd_scan_sample_kernel, steps=steps, inner=inner),
        grid=(nb,),
        in_specs=[pl.BlockSpec((1, QP, cdim), lambda b: (b, 0, 0)),
                  pl.BlockSpec((1, QP, LANE), lambda b: (b, 0, 0)),
                  pl.BlockSpec((1, QP, inner), lambda b: (b, 0, 0)),
                  pl.BlockSpec((1, rows_h, SSD_STATE), lambda b: (b, 0, 0)),
                  one((1, LANE)), one((1, inner)), one((1, inner)), one((LANE, inner))],
        out_specs=[pl.BlockSpec((1, QP, inner), lambda b: (b, 0, 0)),
                   pl.BlockSpec((1, rows_h, SSD_STATE), lambda b: (b, 0, 0))],
        out_shape=[jax.ShapeDtypeStruct((nb, QP, inner), F32),
                   jax.ShapeDtypeStruct((nb, rows_h, SSD_STATE), F32)],
        compiler_params=_cparams(1),
        name="ssd_scan_sample",
    )(xc, dt, z, h0, _pad_lanes(a_log), dexp, norm_g.reshape(1, inner), _head_expand(nh, SSD_HEADDIM))


TM = 256


def _to_steps(a, nb, steps):
    n = a.shape[-1]
    return a.reshape(nb, steps, n).transpose(1, 0, 2).reshape(steps * nb, n)


def _to_batch(a, nb, steps):
    n = a.shape[-1]
    return a.reshape(steps, nb, n).transpose(1, 0, 2).reshape(1, nb * steps, n)


def _pad_q(a, nb, steps):
    n = a.shape[-1]
    return jnp.pad(a.reshape(nb, steps, n), ((0, 0), (0, QP - steps), (0, 0)))


def kernel(x_prompt, x_sample, cache_k, cache_v, cache_kidx, state_s5_re, state_s5_im, state_sconv, state_ssd, state_ssd_conv, page_table, c_prompt, c_sample, rel_bias, ada_w, ada_b, norm_mix, norm_mlp, norm_final, attn_w_in, attn_w_out, s5_lam_re, s5_lam_im, s5_log_dt, s5_b_re, s5_b_im, s5_c_re, s5_c_im, s5_d, s5_w_glu, sc_w_in, sc_w_conv, sc_w_out, ssd_w_in, ssd_conv_w, ssd_conv_b, ssd_dt_bias, ssd_a_log, ssd_d, ssd_norm, ssd_w_out, mlp_w1, mlp_w2):
    bp, seq, d = x_prompt.shape
    nb, steps, _ = x_sample.shape
    depth = ada_w.shape[0]
    n_mixers = 4
    rs = nb * steps
    tm = min(TM, seq)

    rows = bp + nb
    c_all = jnp.pad(jnp.concatenate([c_prompt, c_sample], axis=0), ((0, (-rows) % SUBLANE), (0, 0)))
    ada = _ada(c_all, ada_w, ada_b)

    xp = x_prompt
    xs = x_sample.reshape(1, rs, d)
    outs = {name: [] for name in ("kp", "vp", "kip", "ks", "vs", "kis", "s5pr", "s5pi", "s5sr", "s5si",
                                  "scp", "scs", "ssdp", "ssdcp", "ssds", "ssdcs")}
    nq, nkv = N_HEADS * HEAD_DIM, N_KV_HEADS * HEAD_DIM
    w1_all, w2_all = mlp_w1.astype(BF16), mlp_w2.astype(BF16)
    for i in range(depth):
        m, j = i % n_mixers, i // n_mixers
        mp = [ada[i, :bp, k * d:(k + 1) * d].reshape(bp, 1, d) for k in range(6)]
        ms = [jnp.repeat(ada[i, bp:bp + nb, k * d:(k + 1) * d], steps, axis=0).reshape(1, rs, d)
              for k in range(6)]
        g_mix = norm_mix[i]
        mix_p = mix_s = None
        if m == 0:
            w_in = attn_w_in[j]
            w_qkv = w_in[:, :nq + 2 * nkv].astype(BF16)
            n_idx = IDX_HEADS * IDX_DIM + LANE
            w_idx = jnp.pad(w_in[:, nq + 2 * nkv:], ((0, 0), (0, n_idx - (w_in.shape[1] - nq - 2 * nkv))))
            w_out = attn_w_out[j].astype(BF16)
            page = cache_k.shape[2]
            q, k, v, kw, qcat, kcat, kaug, vaug, ki = _attn_proj(xp, mp[0], mp[1], g_mix, w_qkv, w_idx,
                                                                 2 * MAX_DISTANCE, True, page)
            o = _dsa_prompt(q, qcat, kw, kcat, kaug, vaug, rel_bias, MAX_DISTANCE)
            mix_p = ("proj", o, mp[2], w_out)
            to_pages = lambda t: jnp.transpose(t.reshape(bp, seq // page, N_KV_HEADS, HEAD_DIM, page), (0, 1, 4, 2, 3))
            outs["kp"].append(to_pages(k))
            outs["vp"].append(to_pages(v))
            outs["kip"].append(jnp.transpose(ki, (0, 1, 3, 2)))
            q, k, v, kw, qcat, kcat, _, _ = _attn_proj(xs, ms[0], ms[1], g_mix, w_qkv, w_idx, rs, False)
            pps = math.gcd(PAGES_PER_STEP, page_table.shape[1])
            qc_p, kw_p = _pad_q(qcat.astype(F32), nb, steps), _pad_q(kw, nb, steps)
            scores, snew = _dsa_sample_scores(page_table, qc_p, kw_p, _pad_q(kcat.astype(F32), nb, steps),
                                              _pages_t(cache_kidx), j, pps)
            o = _dsa_sample(page_table, _pad_q(q, nb, steps), _pad_q(k, nb, steps), _pad_q(v, nb, steps),
                            scores, snew, _pages_t(cache_k), _pages_t(cache_v), rel_bias, j, steps, pps)
            mix_s = ("proj", o[:, :steps].reshape(1, rs, nq), ms[2], w_out)
            outs["ks"].append(k.reshape(nb, steps, N_KV_HEADS, HEAD_DIM))
            outs["vs"].append(v.reshape(nb, steps, N_KV_HEADS, HEAD_DIM))
            outs["kis"].append(kw[..., :IDX_DIM].reshape(nb, steps, IDX_DIM))
        elif m == 1:
            ar, ai, wb, wc = _s5_weights(s5_lam_re[j], s5_lam_im[j], s5_log_dt[j], s5_b_re[j], s5_b_im[j],
                                         s5_c_re[j], s5_c_im[j])
            w_glu = s5_w_glu[j].astype(BF16)
            grp, nst = s5_lam_re.shape[1:]
            z, fr, fi = _s5_prompt(xp, mp[0], mp[1], g_mix, s5_d[j], ar, ai, wb, wc)
            mix_p = ("glu", z, mp[2], w_glu)
            outs["s5pr"].append(fr.reshape(bp, grp, nst))
            outs["s5pi"].append(fi.reshape(bp, grp, nst))
            z, fr, fi = _s5_sample(_to_steps(xs, nb, steps), _to_steps(ms[0], nb, steps),
                                   _to_steps(ms[1], nb, steps), g_mix, s5_d[j], ar, ai, wb, wc,
                                   state_s5_re[j].reshape(nb, grp * nst), state_s5_im[j].reshape(nb, grp * nst),
                                   nb, steps)
            mix_s = ("glu", _to_batch(z, nb, steps), ms[2], w_glu)
            outs["s5sr"].append(fr.reshape(nb, grp, nst))
            outs["s5si"].append(fi.reshape(nb, grp, nst))
        elif m == 2:
            w_in = sc_w_in[j].astype(BF16)
            w_out = sc_w_out[j].astype(BF16)
            width = sc_w_conv.shape[1]
            xp, nbuf = _sconv_prompt(xp, mp[0], mp[1], g_mix, w_in, mp[2], jnp.zeros((bp, width - 1, d), F32),
                                     sc_w_conv[j], w_out, tm)
            outs["scp"].append(nbuf)
            (p,) = _proj(xs, ms[0], ms[1], g_mix, w_in, (3 * d,), rs)
            buf = state_sconv[j].transpose(1, 0, 2).reshape((width - 1) * nb, d)
            y, nbuf = _sconv_sample(_to_steps(p, nb, steps), _to_steps(xs, nb, steps), _to_steps(ms[2], nb, steps),
                                    buf, sc_w_conv[j], w_out, nb, steps)
            xs = _to_batch(y, nb, steps)
            outs["scs"].append(nbuf.reshape(width - 1, nb, d).transpose(1, 0, 2))
        else:
            inner = ssd_norm.shape[1]
            cdim = ssd_conv_w.shape[2]
            nh = ssd_dt_bias.shape[1]
            width = ssd_conv_w.shape[1]
            w_in = jnp.pad(ssd_w_in[j], ((0, 0), (0, LANE - nh))).astype(BF16)
            w_out = ssd_w_out[j].astype(BF16)
            z, xbc, dtr = _proj(xp, mp[0], mp[1], g_mix, w_in, (inner, cdim, LANE), tm)
            y, nbuf, hf = _ssd_prompt(z, xbc, dtr, ssd_conv_w[j], ssd_conv_b[j], ssd_dt_bias[j], ssd_a_log[j],
                                      ssd_d[j], ssd_norm[j])
            mix_p = ("proj", y, mp[2], w_out)
            outs["ssdp"].append(hf)
            outs["ssdcp"].append(nbuf)
            z, xbc, dtr = _proj(xs, ms[0], ms[1], g_mix, w_in, (inner, cdim, LANE), rs)
            buf = state_ssd_conv[j].transpose(1, 0, 2).reshape((width - 1) * nb, cdim)
            xc, dt, nbuf = _ssd_prep_sample(_to_steps(xbc, nb, steps), buf, _to_steps(dtr, nb, steps),
                                            ssd_conv_w[j], ssd_conv_b[j], ssd_dt_bias[j], nb, steps)
            y, hf = _ssd_scan_sample(_pad_q(_to_batch(xc, nb, steps), nb, steps),
                                     _pad_q(_to_batch(dt, nb, steps), nb, steps), _pad_q(z, nb, steps),
                                     state_ssd[j].reshape(nb, nh * SSD_HEADDIM, SSD_STATE),
                                     ssd_a_log[j], ssd_d[j], ssd_norm[j], steps)
            mix_s = ("proj", y[:, :steps].reshape(1, rs, inner), ms[2], w_out)
            outs["ssds"].append(hf.reshape(nb, nh, SSD_HEADDIM, SSD_STATE))
            outs["ssdcs"].append(nbuf.reshape(width - 1, nb, cdim).transpose(1, 0, 2))
        fin = i == depth - 1
        xp = _mlp(xp, mp[3], mp[4], mp[5], norm_mlp[i], w1_all, w2_all, i, norm_final, fin, tm, mix_p)
        xs = _mlp(xs, ms[3], ms[4], ms[5], norm_mlp[i], w1_all, w2_all, i, norm_final, fin, rs, mix_s)
    st = jnp.stack
    return (xp, xs.reshape(nb, steps, d), st(outs["kp"]), st(outs["vp"]), st(outs["kip"]),
            st(outs["ks"]), st(outs["vs"]), st(outs["kis"]), st(outs["s5pr"]), st(outs["s5pi"]),
            st(outs["s5sr"]), st(outs["s5si"]), st(outs["scp"]), st(outs["scs"]),
            st(outs["ssdp"]), st(outs["ssdcp"]), st(outs["ssds"]), st(outs["ssdcs"]))
```

```python
import functools
import math

import jax
import jax.numpy as jnp
import numpy as np
from jax import lax
from jax.experimental import pallas as pl
from jax.experimental.pallas import tpu as pltpu

F32 = jnp.float32
BF16 = jnp.bfloat16
I32 = jnp.int32

EPS = 1e-6
N_HEADS = 16
HEAD_DIM = 64
N_KV_HEADS = 4
IDX_HEADS = 8
IDX_DIM = 64
TOPK_MAX = 256
N_BUCKETS = 32
MAX_DISTANCE = 128
S5_GROUP = 16
S5_STATE = 64
SSD_HEADDIM = 64
SSD_GROUPS = 4
SSD_STATE = 128
SSD_CHUNK = 128

LANE = 128
SUBLANE = 8
VMEM_LIMIT = 56 * 1024 * 1024
NEG_BIG = -1e30
INT_MIN = -2147483648


def _cparams(n_axes):
    return pltpu.CompilerParams(dimension_semantics=("arbitrary",) * n_axes,
                                vmem_limit_bytes=VMEM_LIMIT)


def _dot(a, b):
    return jnp.dot(a, b, preferred_element_type=F32)


def _dot_nt(a, b):
    return lax.dot_general(a, b, (((1,), (1,)), ((), ())), preferred_element_type=F32)


def _split(a):
    hi = a.astype(BF16)
    lo = (a - hi.astype(F32)).astype(BF16)
    return hi, lo


def _dot3(a, b):
    ah, al = _split(a)
    bh, bl = _split(b)
    return _dot(ah, bh) + (_dot(al, bh) + _dot(ah, bl))


def _rms_mod(x, g, shift, scale):
    y = x * lax.rsqrt(jnp.mean(x * x, axis=-1, keepdims=True) + EPS)
    return (y * g) * (1.0 + scale) + shift


def _sigmoid(x):
    return 1.0 / (1.0 + jnp.exp(-x))


def _silu(x):
    return x * _sigmoid(x)


def _row_block(arr, tm):
    n = arr.shape[-1]
    if arr.shape[1] == 1:
        return pl.BlockSpec((1, 1, n), lambda b, i: (b, 0, 0))
    return pl.BlockSpec((1, tm, n), lambda b, i: (b, i, 0))


def _const2(shape):
    return pl.BlockSpec(shape, lambda b, i: (0,) * len(shape))


def _ada_kernel(c_ref, w_ref, b_ref, o_ref):
    o_ref[0] = _dot3(_silu(c_ref[...]), w_ref[0]) + b_ref[0]


def _ada(c_all, ada_w, ada_b):
    depth, d, n = ada_w.shape
    rows = c_all.shape[0]
    tn = 1536
    return pl.pallas_call(
        _ada_kernel,
        grid=(depth, n // tn),
        in_specs=[pl.BlockSpec((rows, d), lambda l, j: (0, 0)),
                  pl.BlockSpec((1, d, tn), lambda l, j: (l, 0, j)),
                  pl.BlockSpec((1, 1, tn), lambda l, j: (l, 0, j))],
        out_specs=pl.BlockSpec((1, rows, tn), lambda l, j: (l, 0, j)),
        out_shape=jax.ShapeDtypeStruct((depth, rows, n), F32),
        compiler_params=_cparams(2),
        name="ada",
    )(c_all, ada_w, ada_b.reshape(depth, 1, n))


def _proj_kernel(x_ref, sh_ref, sc_ref, g_ref, w_ref, *o_refs, splits):
    h = _rms_mod(x_ref[0], g_ref[...], sh_ref[0], sc_ref[0]).astype(BF16)
    off = 0
    for o_ref, n in zip(o_refs, splits):
        o_ref[0] = _dot(h, w_ref[:, off:off + n])
        off += n


def _proj(x, shift, scale, g, w_bf16, splits, tm):
    bsz, seq, d = x.shape
    n = w_bf16.shape[1]
    assert sum(splits) == n and seq % tm == 0
    return pl.pallas_call(
        functools.partial(_proj_kernel, splits=tuple(splits)),
        grid=(bsz, seq // tm),
        in_specs=[pl.BlockSpec((1, tm, d), lambda b, i: (b, i, 0)),
                  _row_block(shift, tm), _row_block(scale, tm),
                  _const2((1, d)), _const2((d, n))],
        out_specs=[pl.BlockSpec((1, tm, s), lambda b, i: (b, i, 0)) for s in splits],
        out_shape=[jax.ShapeDtypeStruct((bsz, seq, s), F32) for s in splits],
        compiler_params=_cparams(2),
        name="proj",
    )(x, shift, scale, g.reshape(1, d), w_bf16)


def _aug_heads(t, col):
    rows = t.shape[0]
    lane = lax.broadcasted_iota(I32, (rows, LANE - HEAD_DIM), 1)
    extra = jnp.where(lane == 0, col, 0.0).astype(F32)
    parts = []
    for g in range(N_KV_HEADS):
        parts += [t[:, g * HEAD_DIM:(g + 1) * HEAD_DIM], extra]
    return jnp.concatenate(parts, axis=1).astype(BF16)


def _attn_proj_kernel(x_ref, sh_ref, sc_ref, g_ref, w_ref, wi_ref,
                      q_ref, k_ref, v_ref, kw_ref, qcat_ref, kcat_ref, kaug_ref, vaug_ref, *ki_refs,
                      nq, nkv, nqi, key_major):
    (ki_ref,) = ki_refs if key_major else (None,)
    h = _rms_mod(x_ref[0], g_ref[...], sh_ref[0], sc_ref[0])
    hb = h.astype(BF16)
    q_ref[0] = (_dot(hb, w_ref[:, 0:nq]) * (HEAD_DIM ** -0.5)).astype(q_ref.dtype)
    k = _dot(hb, w_ref[:, nq:nq + nkv])
    v = _dot(hb, w_ref[:, nq + nkv:nq + 2 * nkv])
    r = _dot3(h, wi_ref[...])
    kw = r[:, nqi:nqi + LANE]
    kw_ref[0] = kw
    if key_major:
        page = k_ref.shape[-1]
        kt, vt, kwt = k.T, v.T, kw.T
        for p in range(k_ref.shape[1]):
            k_ref[0, p] = kt[:, p * page:(p + 1) * page]
            v_ref[0, p] = vt[:, p * page:(p + 1) * page]
            ki_ref[0, p] = kwt[0:IDX_DIM, p * page:(p + 1) * page]
    else:
        k_ref[0] = k
        v_ref[0] = v
    zero = jnp.zeros((h.shape[0], IDX_DIM), BF16)
    parts = []
    for hh in range(IDX_HEADS):
        ah, al = _split(r[:, hh * IDX_DIM:(hh + 1) * IDX_DIM])
        parts += [ah, al, ah, zero]
    qcat_ref[0] = jnp.concatenate(parts, axis=1)
    kcat = _idx_rhs(kw[:, 0:IDX_DIM])
    kaug = _aug_heads(k, 0.0)
    vaug_ref[0] = _aug_heads(v, 1.0)
    if key_major:
        kcat_ref[0, 0] = kcat.astype(F32).T.astype(BF16)
        kaug_ref[0, 0] = kaug.astype(F32).T.astype(BF16)
    else:
        kcat_ref[0] = kcat
        kaug_ref[0] = kaug


def _attn_proj(x, shift, scale, g, w_qkv_bf16, w_idx, tm, key_major, page=None):
    bsz, seq, d = x.shape
    nq = N_HEADS * HEAD_DIM
    nkv = N_KV_HEADS * HEAD_DIM
    nqi = IDX_HEADS * IDX_DIM
    rows = lambda n, dt: (pl.BlockSpec((1, tm, n), lambda b, i: (b, i, 0)), jax.ShapeDtypeStruct((bsz, seq, n), dt))
    cols = lambda n, dt: (pl.BlockSpec((1, 1, n, tm), lambda b, i: (b, i, 0, 0)),
                          jax.ShapeDtypeStruct((bsz, seq // tm, n, tm), dt))
    keys = cols if key_major else rows
    if key_major:
        ppt = tm // page
        paged = lambda n: (pl.BlockSpec((1, ppt, n, page), lambda b, i: (b, i, 0, 0)),
                           jax.ShapeDtypeStruct((bsz, seq // page, n, page), F32))
        kv = [paged(nkv), paged(nkv)]
    else:
        kv = [rows(nkv, F32), rows(nkv, F32)]
    outs = [rows(nq, BF16 if key_major else F32)] + kv + [rows(LANE, F32), rows(4 * nqi, BF16),
                                   keys(4 * IDX_DIM, BF16), keys(N_KV_HEADS * LANE, BF16),
                                   rows(N_KV_HEADS * LANE, BF16)]
    if key_major:
        outs.append(paged(IDX_DIM))
    return pl.pallas_call(
        functools.partial(_attn_proj_kernel, nq=nq, nkv=nkv, nqi=nqi, key_major=key_major),
        grid=(bsz, seq // tm),
        in_specs=[pl.BlockSpec((1, tm, d), lambda b, i: (b, i, 0)),
                  _row_block(shift, tm), _row_block(scale, tm),
                  _const2((1, d)), _const2(w_qkv_bf16.shape), _const2(w_idx.shape)],
        out_specs=[o[0] for o in outs],
        out_shape=[o[1] for o in outs],
        compiler_params=_cparams(2),
        name="attn_proj",
    )(x, shift, scale, g.reshape(1, d), w_qkv_bf16, w_idx)


def _mlp_kernel(*refs, ff_chunk, final_norm, mixer_out):
    if mixer_out is None:
        x_ref, sh_ref, sc_ref, gate_ref, g_ref, w1_ref, w2_ref, gf_ref, y_ref = refs
        x = x_ref[0]
    else:
        o_ref, gmix_ref, wo_ref, x_ref, sh_ref, sc_ref, gate_ref, g_ref, w1_ref, w2_ref, gf_ref, y_ref = refs
        d = x_ref.shape[-1]
        ob = o_ref[0].astype(BF16)
        if mixer_out == "glu":
            t = _dot(ob, wo_ref[:, :d]) * _sigmoid(_dot(ob, wo_ref[:, d:]))
        else:
            t = _dot(ob, wo_ref[...])
        x = x_ref[0] + gmix_ref[0] * t
    h = _rms_mod(x, g_ref[...], sh_ref[0], sc_ref[0]).astype(BF16)
    dff = w1_ref.shape[-1]
    acc = jnp.zeros(x.shape, F32)
    for c in range(dff // ff_chunk):
        a = jnp.maximum(_dot(h, w1_ref[0, :, c * ff_chunk:(c + 1) * ff_chunk]), 0.0)
        acc = acc + _dot((a * a).astype(BF16), w2_ref[0, c * ff_chunk:(c + 1) * ff_chunk, :])
    y = x + gate_ref[0] * acc
    if final_norm:
        y = (y * lax.rsqrt(jnp.mean(y * y, axis=-1, keepdims=True) + EPS)) * gf_ref[...]
    y_ref[0] = y


def _resident(shape):
    return pl.BlockSpec(shape, lambda b, i: (0,) * len(shape), pipeline_mode=pl.Buffered(1))


def _mlp(x, shift, scale, gate, g, w1_bf16, w2_bf16, layer, g_final, final_norm, tm, mixer=None):
    bsz, seq, d = x.shape
    dff = w1_bf16.shape[-1]
    layer_w = lambda r, c: pl.BlockSpec((1, r, c), lambda b, i: (layer, 0, 0), pipeline_mode=pl.Buffered(1))
    rows = lambda n: pl.BlockSpec((1, tm, n), lambda b, i: (b, i, 0))
    args, specs, kind = [], [], None
    if mixer is not None:
        kind, o, gmix, wo = mixer
        args += [o, gmix, wo]
        specs += [rows(o.shape[-1]), _row_block(gmix, tm), _resident(wo.shape)]
    args += [x, shift, scale, gate, g.reshape(1, d), w1_bf16, w2_bf16, g_final.reshape(1, d)]
    specs += [rows(d), _row_block(shift, tm), _row_block(scale, tm), _row_block(gate, tm),
              _const2((1, d)), layer_w(d, dff), layer_w(dff, d), _const2((1, d))]
    return pl.pallas_call(
        functools.partial(_mlp_kernel, ff_chunk=1024, final_norm=final_norm, mixer_out=kind),
        grid=(bsz, seq // tm),
        in_specs=specs,
        out_specs=rows(d),
        out_shape=jax.ShapeDtypeStruct((bsz, seq, d), F32),
        compiler_params=_cparams(2),
        name="mlp",
    )(*args)


S5_CH = 128
S5_BLK = (S5_CH // S5_GROUP) * S5_STATE


def _s5_disc_kernel(lr_ref, li_ref, ldt_ref, br_ref, bi_ref, ar_ref, ai_ref, bbr_ref, bbi_ref):
    lr, li = lr_ref[...], li_ref[...]
    dt = jnp.exp(ldt_ref[...])
    mag = jnp.exp(lr * dt)
    ab_re, ab_im = mag * jnp.cos(li * dt), mag * jnp.sin(li * dt)
    den = lr * lr + li * li
    nr = ab_re - 1.0
    f_re = (nr * lr + ab_im * li) / den
    f_im = (ab_im * lr - nr * li) / den
    ar_ref[...] = ab_re
    ai_ref[...] = ab_im
    for c in range(br_ref.shape[0]):
        br, bi = br_ref[c], bi_ref[c]
        bbr_ref[c] = f_re * br - f_im * bi
        bbi_ref[c] = f_re * bi + f_im * br


def _s5_weights(lam_re, lam_im, log_dt, b_re, b_im, c_re, c_im):
    g, p = lam_re.shape
    gc = b_re.shape[-1]
    brt = jnp.moveaxis(b_re, 2, 0)
    bit = jnp.moveaxis(b_im, 2, 0)
    ar, ai, bbr, bbi = pl.pallas_call(
        _s5_disc_kernel,
        out_shape=[jax.ShapeDtypeStruct((g, p), F32), jax.ShapeDtypeStruct((g, p), F32),
                   jax.ShapeDtypeStruct((gc, g, p), F32), jax.ShapeDtypeStruct((gc, g, p), F32)],
        name="s5_disc",
    )(lam_re, lam_im, log_dt.reshape(g, 1), brt, bit)
    nblk = (g * gc) // S5_CH
    gpb = g // nblk
    eye = jnp.eye(gpb, dtype=F32)

    def bd_in(bb):
        t = jnp.moveaxis(bb, 0, 1).reshape(nblk, gpb, gc, p)
        return jnp.einsum('ngcp,gh->ngchp', t, eye).reshape(nblk, gpb * gc, gpb * p)

    def bd_out(cc):
        t = cc.reshape(nblk, gpb, gc, p)
        return jnp.einsum('ngcp,gh->ngphc', t, eye).reshape(nblk, gpb * p, gpb * gc)

    wb = jnp.concatenate([bd_in(bbr), bd_in(bbi)], axis=-1).astype(BF16)
    wc = jnp.concatenate([bd_out(c_re), bd_out(c_im)], axis=1).astype(BF16)
    return ar.reshape(nblk, 1, gpb * p), ai.reshape(nblk, 1, gpb * p), wb, wc


def _gelu_tanh(y):
    return 0.5 * y * (1.0 + jnp.tanh(math.sqrt(2.0 / math.pi) * (y + 0.044715 * (y * y * y))))


def _cmul(ar, ai, xr, xi):
    return ar * xr - ai * xi, ar * xi + ai * xr


def _s5_prompt_kernel(x_ref, sh_ref, sc_ref, g_ref, d_ref, ar_ref, ai_ref, wb_ref, wc_ref,
                      perm_ref, unperm_ref, z_ref, fr_ref, fi_ref, u_s, br_s, bi_s, st_r, st_i,
                      *, nseg, seg):
    i = pl.program_id(1)
    nblk = wb_ref.shape[0]
    blk = ar_ref.shape[-1]

    @pl.when(i == 0)
    def _():
        st_r[...] = jnp.zeros(st_r.shape, F32)
        st_i[...] = jnp.zeros(st_i.shape, F32)

    u_s[...] = _dot_sel_lhs(perm_ref[...], _rms_mod(x_ref[0], g_ref[...], sh_ref[0], sc_ref[0]))

    for c in range(nblk):
        lo, hi = c * S5_CH, (c + 1) * S5_CH
        uc = u_s[:, lo:hi]
        bu = _dot(uc.astype(BF16), wb_ref[c])
        br_s[...] = bu[:, :blk]
        bi_s[...] = bu[:, blk:]
        ar = jnp.broadcast_to(ar_ref[c], (nseg, blk))
        ai = jnp.broadcast_to(ai_ref[c], (nseg, blk))

        def local(j, carry):
            xr, xi = carry
            r0 = pl.multiple_of(j * nseg, nseg)
            pr, pi = _cmul(ar, ai, xr, xi)
            nr = pr + br_s[pl.ds(r0, nseg), :]
            ni = pi + bi_s[pl.ds(r0, nseg), :]
            br_s[pl.ds(r0, nseg), :] = nr
            bi_s[pl.ds(r0, nseg), :] = ni
            return nr, ni

        zero = jnp.zeros((nseg, blk), F32)
        fr, fi = lax.fori_loop(0, seg, local, (zero, zero), unroll=True)

        pr, pi = ar_ref[c], ai_ref[c]
        for _ in range(int(math.log2(seg))):
            pr, pi = _cmul(pr, pi, pr, pi)
        cr, ci = st_r[c], st_i[c]
        rows_r, rows_i = [], []
        for s in range(nseg):
            rows_r.append(cr)
            rows_i.append(ci)
            tr, ti = _cmul(pr, pi, cr, ci)
            cr, ci = tr + fr[s:s + 1], ti + fi[s:s + 1]
        st_r[c] = cr
        st_i[c] = ci
        fr_ref[0, c] = cr
        fi_ref[0, c] = ci
        dr, di = _cmul(ar, ai, jnp.concatenate(rows_r, axis=0), jnp.concatenate(rows_i, axis=0))

        def fix(j, carry):
            dr, di = carry
            r0 = pl.multiple_of(j * nseg, nseg)
            br_s[pl.ds(r0, nseg), :] = br_s[pl.ds(r0, nseg), :] + dr
            bi_s[pl.ds(r0, nseg), :] = bi_s[pl.ds(r0, nseg), :] + di
            return _cmul(ar, ai, dr, di)

        lax.fori_loop(0, seg, fix, (dr, di), unroll=True)

        y = (_dot(br_s[...].astype(BF16), wc_ref[c, :blk, :])
             - _dot(bi_s[...].astype(BF16), wc_ref[c, blk:, :]))
        y = y + d_ref[:, lo:hi] * uc
        u_s[:, lo:hi] = _gelu_tanh(y)

    z_ref[0] = _dot_sel_lhs(unperm_ref[...], u_s[...]).astype(z_ref.dtype)


def _s5_prompt(x, shift, scale, g, d_skip, ar, ai, wb, wc):
    bsz, seq, d = x.shape
    nblk, _, blk = ar.shape
    nseg, seg = SUBLANE, 32
    tm = nseg * seg
    assert seq % tm == 0
    perm = np.zeros((tm, tm), np.float32)
    for s in range(nseg):
        for j in range(seg):
            perm[j * nseg + s, s * seg + j] = 1.0
    unperm = jnp.asarray(perm.T, BF16)
    perm = jnp.asarray(perm, BF16)
    z, fr, fi = pl.pallas_call(
        functools.partial(_s5_prompt_kernel, nseg=nseg, seg=seg),
        grid=(bsz, seq // tm),
        in_specs=[pl.BlockSpec((1, tm, d), lambda b, i: (b, i, 0)),
                  _row_block(shift, tm), _row_block(scale, tm),
                  _const2((1, d)), _const2((1, d)),
                  _const2(ar.shape), _const2(ai.shape), _const2(wb.shape), _const2(wc.shape),
                  _const2((tm, tm)), _const2((tm, tm))],
        out_specs=[pl.BlockSpec((1, tm, d), lambda b, i: (b, i, 0)),
                   pl.BlockSpec((1, nblk, 1, blk), lambda b, i: (b, 0, 0, 0)),
                   pl.BlockSpec((1, nblk, 1, blk), lambda b, i: (b, 0, 0, 0))],
        out_shape=[jax.ShapeDtypeStruct((bsz, seq, d), BF16),
                   jax.ShapeDtypeStruct((bsz, nblk, 1, blk), F32),
                   jax.ShapeDtypeStruct((bsz, nblk, 1, blk), F32)],
        scratch_shapes=[pltpu.VMEM((tm, d), F32), pltpu.VMEM((tm, blk), F32), pltpu.VMEM((tm, blk), F32),
                        pltpu.VMEM((nblk, 1, blk), F32), pltpu.VMEM((nblk, 1, blk), F32)],
        compiler_params=_cparams(2),
        name="s5_prompt",
    )(x, shift, scale, g.reshape(1, d), d_skip.reshape(1, d), ar, ai, wb, wc, perm, unperm)
    return z, fr, fi


def _s5_sample_kernel(x_ref, sh_ref, sc_ref, g_ref, d_ref, ar_ref, ai_ref, wb_ref, wc_ref,
                      s0r_ref, s0i_ref, z_ref, fr_ref, fi_ref, *, nb, steps):
    nblk = wb_ref.shape[0]
    blk = ar_ref.shape[-1]
    u = _rms_mod(x_ref[...], g_ref[...], sh_ref[...], sc_ref[...])
    for c in range(nblk):
        lo, hi = c * S5_CH, (c + 1) * S5_CH
        uc = u[:, lo:hi]
        bu = _dot(uc.astype(BF16), wb_ref[c])
        ar = jnp.broadcast_to(ar_ref[c], (nb, blk))
        ai = jnp.broadcast_to(ai_ref[c], (nb, blk))
        xr, xi = s0r_ref[:, c * blk:(c + 1) * blk], s0i_ref[:, c * blk:(c + 1) * blk]
        xrs, xis = [], []
        for t in range(steps):
            pr, pi = _cmul(ar, ai, xr, xi)
            xr = pr + bu[t * nb:(t + 1) * nb, :blk]
            xi = pi + bu[t * nb:(t + 1) * nb, blk:]
            xrs.append(xr)
            xis.append(xi)
        fr_ref[:, c * blk:(c + 1) * blk] = xr
        fi_ref[:, c * blk:(c + 1) * blk] = xi
        y = (_dot(jnp.concatenate(xrs, axis=0).astype(BF16), wc_ref[c, :blk, :])
             - _dot(jnp.concatenate(xis, axis=0).astype(BF16), wc_ref[c, blk:, :]))
        y = y + d_ref[:, lo:hi] * uc
        z_ref[:, lo:hi] = _gelu_tanh(y)


def _s5_sample(x_tb, shift_tb, scale_tb, g, d_skip, ar, ai, wb, wc, s0r, s0i, nb, steps):
    rows, d = x_tb.shape
    nblk, _, blk = ar.shape
    return pl.pallas_call(
        functools.partial(_s5_sample_kernel, nb=nb, steps=steps),
        out_shape=[jax.ShapeDtypeStruct((rows, d), F32),
                   jax.ShapeDtypeStruct((nb, nblk * blk), F32),
                   jax.ShapeDtypeStruct((nb, nblk * blk), F32)],
        compiler_params=pltpu.CompilerParams(vmem_limit_bytes=VMEM_LIMIT),
        name="s5_sample",
    )(x_tb, shift_tb, scale_tb, g.reshape(1, d), d_skip.reshape(1, d), ar, ai, wb, wc, s0r, s0i)


def _shift_rows(u, prev, k):
    if k == 0:
        return u
    rolled = pltpu.roll(u, k, 0)
    head = pltpu.roll(prev, k, 0)
    row = lax.broadcasted_iota(I32, (SUBLANE, u.shape[1]), 0)
    first = jnp.where(row < k, head, rolled[:SUBLANE])
    return jnp.concatenate([first, rolled[SUBLANE:]], axis=0)


def _sconv_prompt_kernel(x_ref, sh_ref, sc_ref, g_ref, wi_ref, gate_ref, buf_ref, wc_ref, wo_ref,
                         y_ref, nb_ref, prev_s, *, width):
    i = pl.program_id(1)
    d = x_ref.shape[-1]

    @pl.when(i == 0)
    def _():
        prev_s[...] = jnp.zeros(prev_s.shape, F32)
        prev_s[SUBLANE - (width - 1):, :] = buf_ref[0]

    h = _rms_mod(x_ref[0], g_ref[...], sh_ref[0], sc_ref[0]).astype(BF16)
    gb = _dot(h, wi_ref[:, :d])
    u = _dot(h, wi_ref[:, d:2 * d]) * _dot(h, wi_ref[:, 2 * d:])
    prev = prev_s[...]
    conv = u * wc_ref[width - 1:width, :]
    for k in range(1, width):
        conv = conv + _shift_rows(u, prev, k) * wc_ref[width - 1 - k:width - k, :]
    prev_s[...] = u[u.shape[0] - SUBLANE:, :]
    nb_ref[0] = u[u.shape[0] - (width - 1):, :]
    y_ref[0] = x_ref[0] + gate_ref[0] * _dot((gb * conv).astype(BF16), wo_ref[...])


def _sconv_prompt(x, shift, scale, g, w_in_bf16, gate, buf, w_conv, w_out_bf16, tm):
    bsz, seq, d = x.shape
    width = w_conv.shape[0]
    return pl.pallas_call(
        functools.partial(_sconv_prompt_kernel, width=width),
        grid=(bsz, seq // tm),
        in_specs=[pl.BlockSpec((1, tm, d), lambda b, i: (b, i, 0)),
                  _row_block(shift, tm), _row_block(scale, tm), _const2((1, d)),
                  _resident(w_in_bf16.shape), _row_block(gate, tm),
                  pl.BlockSpec((1, width - 1, d), lambda b, i: (b, 0, 0)),
                  _const2((width, d)), _resident((d, d))],
        out_specs=[pl.BlockSpec((1, tm, d), lambda b, i: (b, i, 0)),
                   pl.BlockSpec((1, width - 1, d), lambda b, i: (b, 0, 0))],
        out_shape=[jax.ShapeDtypeStruct((bsz, seq, d), F32),
                   jax.ShapeDtypeStruct((bsz, width - 1, d), F32)],
        scratch_shapes=[pltpu.VMEM((SUBLANE, d), F32)],
        compiler_params=_cparams(2),
        name="sconv_prompt",
    )(x, shift, scale, g.reshape(1, d), w_in_bf16, gate, buf, w_conv, w_out_bf16)


def _sconv_sample_kernel(p_ref, x_ref, gate_ref, buf_ref, wc_ref, wo_ref, y_ref, nb_ref,
                         *, width, nb, steps):
    d = x_ref.shape[-1]
    p = p_ref[...]
    gb, gc, xh = p[:, :d], p[:, d:2 * d], p[:, 2 * d:]
    u = gc * xh
    ext = jnp.concatenate([buf_ref[...], u], axis=0)
    conv = ext[0:steps * nb] * wc_ref[0:1, :]
    for k in range(1, width):
        conv = conv + ext[k * nb:(k + steps) * nb] * wc_ref[k:k + 1, :]
    nb_ref[...] = ext[steps * nb:]
    y_ref[...] = x_ref[...] + gate_ref[...] * _dot((gb * conv).astype(BF16), wo_ref[...])


def _sconv_sample(p_tb, x_tb, gate_tb, buf_kb, w_conv, w_out_bf16, nb, steps):
    rows, d = x_tb.shape
    width = w_conv.shape[0]
    return pl.pallas_call(
        functools.partial(_sconv_sample_kernel, width=width, nb=nb, steps=steps),
        out_shape=[jax.ShapeDtypeStruct((rows, d), F32),
                   jax.ShapeDtypeStruct(((width - 1) * nb, d), F32)],
        compiler_params=pltpu.CompilerParams(vmem_limit_bytes=VMEM_LIMIT),
        name="sconv_sample",
    )(p_tb, x_tb, gate_tb, buf_kb, w_conv, w_out_bf16)


KEY_NEG_INF = -2139095041
KEY_POS_INF = 2139095040


def _code_to_f32(key):
    key = jnp.clip(key, KEY_NEG_INF, KEY_POS_INF)
    return lax.bitcast_convert_type(key ^ ((key >> 31) & 0x7FFFFFFF), F32)


def _stack_heads(qcat):
    w = 4 * IDX_DIM
    return jnp.concatenate([qcat[:, h * w:(h + 1) * w] for h in range(IDX_HEADS)], axis=0)


def _idx_rhs(ki):
    kh, kl = _split(ki)
    return jnp.concatenate([kh, kh, kl, jnp.zeros(kh.shape, BF16)], axis=1)


def _idx_score(dts, wcols, nrows):
    s = wcols[0] * jnp.maximum(dts[0:nrows], 0.0)
    for h in range(1, IDX_HEADS):
        s = s + wcols[h] * jnp.maximum(dts[h * nrows:(h + 1) * nrows], 0.0)
    return s


def _idx_wcols(kw):
    sc = (IDX_HEADS ** -0.5) * (IDX_DIM ** -0.5)
    return [kw[:, IDX_DIM + h:IDX_DIM + h + 1] * sc for h in range(IDX_HEADS)]


def _topk_select(count, rows, lanes, topk, idx_bits, stash=lambda v: (lambda: v)):
    wide = lambda v: jnp.broadcast_to(v, (rows, lanes))

    def bit_step(it, tu):
        mask = jnp.left_shift(jnp.int32(1), 31 - it)
        cand_u = tu | mask
        cand = stash(wide(_code_to_f32(cand_u ^ INT_MIN)))
        cnt = count(lambda c, sc, idx: jnp.where(sc >= cand(), 1.0, 0.0))
        return jnp.where(cnt >= topk, cand_u, tu)

    t = _code_to_f32(lax.fori_loop(0, 32, bit_step, jnp.zeros((rows, 1), I32)) ^ INT_MIN)
    tw = wide(t)
    n_gt = count(lambda c, sc, idx: jnp.where(sc > tw, 1.0, 0.0))
    n_ge = count(lambda c, sc, idx: jnp.where(sc >= tw, 1.0, 0.0))
    need = topk - n_gt
    tied = jnp.max(jnp.where(t > -jnp.inf, n_ge - topk, 0.0)) > 0.0

    def tie_search(_):
        def idx_step(it, j):
            cand = j | jnp.left_shift(jnp.int32(1), idx_bits - 1 - it)
            cw = wide(cand)
            cnt = count(lambda c, sc, idx: jnp.where((sc == tw) & (idx < cw), 1.0, 0.0))
            return jnp.where(cnt < need, cand, j)
        return lax.fori_loop(0, idx_bits, idx_step, jnp.zeros((rows, 1), I32))

    j = lax.cond(tied, tie_search, lambda _: jnp.full((rows, 1), 2 ** idx_bits, I32), 0)
    return t, j


def _dsa_prompt_kernel(q_ref, qcat_ref, kwq_ref, kcat_ref, kaug_ref, vaug_ref, bias_ref, o_ref,
                       sc_s, madd_s, wbc_s, cand_s, qa_s, lg_s, mx_s, sh_s, acc_s, *, tq, topk, idx_bits):
    i = pl.program_id(1)
    nk = i + 1
    nk2 = (nk + 1) // 2
    r = N_HEADS // N_KV_HEADS

    for h, w in enumerate(_idx_wcols(kwq_ref[0])):
        wbc_s[h] = jnp.broadcast_to(w, (tq, tq))
    qrow = lax.broadcasted_iota(I32, (tq, tq), 0)
    kcol = lax.broadcasted_iota(I32, (tq, tq), 1)
    qpos = i * tq + qrow
    wq = 4 * IDX_DIM

    def score_pair(c2, _):
        kc = kcat_ref[0, c2]
        s = None
        for hp in range(IDX_HEADS // 2):
            lhs = jnp.concatenate([qcat_ref[0, :, (2 * hp + v) * wq:(2 * hp + v + 1) * wq] for v in range(2)], axis=0)
            dd = jnp.maximum(_dot(lhs, kc), 0.0)
            for v in range(2):
                w = wbc_s[2 * hp + v]
                d = jnp.concatenate([w, w], axis=1) * dd[v * tq:(v + 1) * tq]
                s = d if s is None else s + d
        for u in range(2):
            c = 2 * c2 + u
            sc_s[c] = jnp.where(c * tq + kcol <= qpos, s[:, u * tq:(u + 1) * tq], -jnp.inf)
        return 0

    lax.fori_loop(0, nk2, score_pair, 0)

    def count(fn):
        def body(c2, acc):
            c = 2 * c2
            acc = acc + fn(c, sc_s[c], c * tq + kcol)
            return acc + fn(c + 1, sc_s[c + 1], (c + 1) * tq + kcol)
        acc = lax.fori_loop(0, nk2, body, jnp.zeros((tq, tq), F32))
        return jnp.sum(acc, axis=1, keepdims=True)

    def stash(v):
        cand_s[...] = v
        return lambda: cand_s[...]

    t, j = _topk_select(count, tq, tq, float(topk), idx_bits, stash)
    tw = jnp.broadcast_to(t, (tq, tq))
    jw = jnp.broadcast_to(j, (tq, tq))

    def mask_pair(c2, _):
        for u in range(2):
            c = 2 * c2 + u
            sc = sc_s[c]
            sel = (sc > tw) | ((sc == tw) & (c * tq + kcol <= jw))
            madd_s[c] = jnp.where(sel & (sc > -jnp.inf), 0.0, NEG_BIG)
        return 0

    lax.fori_loop(0, nk2, mask_pair, 0)

    q = q_ref[0].astype(F32)
    zpad = jnp.zeros((tq, LANE - HEAD_DIM), F32)
    for g in range(N_KV_HEADS):
        qa_s[g] = jnp.concatenate(
            [jnp.concatenate([q[:, (g * r + u) * HEAD_DIM:(g * r + u + 1) * HEAD_DIM], zpad], axis=1)
             for u in range(r)], axis=0).astype(BF16)

    def pair_mask(c2, near, g):
        halves = []
        for u in range(2):
            c = 2 * c2 + u
            m = madd_s[c]
            if near:
                back = nk - 1 - c
                carries = (back == 0) | ((back == 1) & (kcol > qrow))
                halves.append([m + jnp.where(carries, bias_ref[g * r + v], 0.0) for v in range(r)])
            else:
                halves.append([m] * r)
        return jnp.concatenate([jnp.concatenate([halves[0][v], halves[1][v]], axis=1) for v in range(r)], axis=0)

    ngh = lg_s.shape[0]
    for half in range(N_KV_HEADS // ngh):
        groups = [half * ngh + k for k in range(ngh)]
        mx_s[...] = jnp.full(mx_s.shape, NEG_BIG, F32)

        def logits_pair(c2, near):
            for k, g in enumerate(groups):
                lg = _dot(qa_s[g], kaug_ref[0, c2, g * LANE:(g + 1) * LANE, :]) + pair_mask(c2, near, g)
                lg_s[k, c2] = lg
                mx_s[k] = jnp.maximum(mx_s[k], jnp.maximum(lg[:, :tq], lg[:, tq:]))

        def far(c4, _):
            logits_pair(2 * c4, False)
            logits_pair(2 * c4 + 1, False)
            return 0

        nfar = jnp.maximum(nk2 - 2, 0)
        lax.fori_loop(0, nfar // 2, far, 0)

        @pl.when(nfar % 2 == 1)
        def _():
            logits_pair(nfar - 1, False)

        @pl.when(nk2 >= 2)
        def _():
            logits_pair(nk2 - 2, True)

        logits_pair(nk2 - 1, True)

        for k in range(ngh):
            sh_s[k] = jnp.broadcast_to(jnp.max(mx_s[k], axis=1, keepdims=True), (r * tq, LANE))
        acc_s[...] = jnp.zeros(acc_s.shape, F32)

        def pv_pair(c2):
            r0 = pl.multiple_of(c2 * 2 * tq, 2 * tq)
            va = vaug_ref[0, pl.ds(r0, 2 * tq), :]
            for k, g in enumerate(groups):
                s = sh_s[k]
                p = jnp.exp(lg_s[k, c2] - jnp.concatenate([s, s], axis=1))
                acc_s[k] = acc_s[k] + _dot(p.astype(BF16), va[:, g * LANE:(g + 1) * LANE])

        def pv_two(c4, _):
            pv_pair(2 * c4)
            pv_pair(2 * c4 + 1)
            return 0

        lax.fori_loop(0, nk2 // 2, pv_two, 0)

        @pl.when(nk2 % 2 == 1)
        def _():
            pv_pair(nk2 - 1)
        for k, g in enumerate(groups):
            acc = acc_s[k]
            o = acc[:, 0:HEAD_DIM] / acc[:, HEAD_DIM:HEAD_DIM + 1]
            heads = jnp.concatenate([o[u * tq:(u + 1) * tq, :] for u in range(r)], axis=1)
            o_ref[0, :, g * r * HEAD_DIM:(g + 1) * r * HEAD_DIM] = heads.astype(o_ref.dtype)


def _t5_bucket(dist):
    n = jnp.maximum(dist, 0)
    max_exact = N_BUCKETS // 2
    nf = jnp.maximum(n, max_exact).astype(F32)
    large = max_exact + (jnp.log(nf / max_exact) / math.log(MAX_DISTANCE / max_exact)
                         * (N_BUCKETS - max_exact)).astype(I32)
    large = jnp.minimum(large, N_BUCKETS - 1)
    return jnp.where(n < max_exact, n, large)


def _bias_table(rel_bias, n):
    tab = rel_bias[_t5_bucket(jnp.arange(n, dtype=I32))]
    return (tab - rel_bias[N_BUCKETS - 1][None, :]).T


DSA_GROUPS_PER_PASS = 2


def _dsa_prompt(q, qcat, kw, kcat, kaug, vaug, rel_bias, tq):
    bsz, seq, nq = q.shape
    topk = min(TOPK_MAX, seq // 4)
    nkc = seq // tq
    r = N_HEADS // N_KV_HEADS
    assert tq == MAX_DISTANCE and seq % (2 * tq) == 0
    tab = _bias_table(rel_bias, tq)
    dmod = (np.arange(tq)[:, None] - np.arange(tq)[None, :]) % tq
    onehot = (jnp.arange(tq, dtype=I32)[:, None, None] == jnp.asarray(dmod, I32)[None]).astype(F32)
    bias = jnp.einsum('hd,dqk->hqk', tab, onehot, precision=lax.Precision.HIGHEST)
    assert kcat.shape[1:] == (seq // (2 * tq), 4 * IDX_DIM, 2 * tq)
    row = lambda n: pl.BlockSpec((1, tq, n), lambda b, i: (b, i, 0))
    tiles = lambda a: pl.BlockSpec((1,) + a.shape[1:], lambda b, i: (b, 0, 0, 0), pipeline_mode=pl.Buffered(1))
    return pl.pallas_call(
        functools.partial(_dsa_prompt_kernel, tq=tq, topk=topk, idx_bits=int(math.log2(seq))),
        grid=(bsz, nkc),
        in_specs=[row(nq), row(qcat.shape[-1]), row(LANE), tiles(kcat), tiles(kaug),
                  pl.BlockSpec((1, seq, vaug.shape[-1]), lambda b, i: (b, 0, 0), pipeline_mode=pl.Buffered(1)),
                  _resident((N_HEADS, tq, tq))],
        out_specs=row(nq),
        out_shape=jax.ShapeDtypeStruct((bsz, seq, nq), BF16),
        scratch_shapes=[pltpu.VMEM((nkc, tq, tq), F32), pltpu.VMEM((nkc, tq, tq), F32),
                        pltpu.VMEM((IDX_HEADS, tq, tq), F32), pltpu.VMEM((tq, tq), F32),
                        pltpu.VMEM((N_KV_HEADS, r * tq, LANE), BF16),
                        pltpu.VMEM((DSA_GROUPS_PER_PASS, nkc // 2, r * tq, 2 * tq), F32),
                        pltpu.VMEM((DSA_GROUPS_PER_PASS, r * tq, tq), F32),
                        pltpu.VMEM((DSA_GROUPS_PER_PASS, r * tq, LANE), F32),
                        pltpu.VMEM((DSA_GROUPS_PER_PASS, r * tq, LANE), F32)],
        compiler_params=_cparams(2),
        name="dsa_prompt",
    )(q, qcat, kw, kcat, kaug, vaug, bias)


QP = SUBLANE
PAGES_PER_STEP = 16


def _page_specs(shape4, layer, pps):
    def spec(u):
        return pl.BlockSpec((1, 1) + tuple(shape4[2:]),
                            lambda b, g, pt: (layer, pt[b, g * pps + u], 0, 0))
    return [spec(u) for u in range(pps)]


def _pages_t(cache):
    nd = cache.ndim
    t = jnp.transpose(cache, (0, 1) + tuple(range(3, nd)) + (2,))
    return t.reshape(t.shape[0], t.shape[1], -1, t.shape[-1])


def _pad_t(a, page):
    return jnp.concatenate([a, jnp.zeros((page - QP, a.shape[1]), F32)], axis=0).T


def _dsa_sample_score_kernel(pt_ref, qcat_ref, kw_ref, kcn_ref, *rest, pps):
    page_refs, o_ref, on_ref = rest[:pps], rest[pps], rest[pps + 1]
    lhs = _stack_heads(qcat_ref[0]).astype(BF16)
    wcols = _idx_wcols(kw_ref[0])
    kt = jnp.concatenate([page_refs[u][0, 0] for u in range(pps)], axis=1)
    kh, kl = _split(kt)
    rhs = jnp.concatenate([kh, kh, kl, jnp.zeros(kh.shape, BF16)], axis=0)
    o_ref[0] = _idx_score(_dot(lhs, rhs), wcols, QP)

    @pl.when(pl.program_id(1) == pl.num_programs(1) - 1)
    def _():
        on_ref[0] = _idx_score(_dot(lhs, _pad_t(kcn_ref[0], on_ref.shape[-1]).astype(BF16)), wcols, QP)


def _dsa_sample_scores(page_table, qcat, kw, kcn, kidx_t, layer, pps):
    nb, n_pages = page_table.shape
    page = kidx_t.shape[-1]
    bmap = lambda b, g, pt: (b, 0, 0)
    grid_spec = pltpu.PrefetchScalarGridSpec(
        num_scalar_prefetch=1,
        grid=(nb, n_pages // pps),
        in_specs=[pl.BlockSpec((1, QP, qcat.shape[-1]), bmap), pl.BlockSpec((1, QP, LANE), bmap),
                  pl.BlockSpec((1, QP, kcn.shape[-1]), bmap)] + _page_specs(kidx_t.shape, layer, pps),
        out_specs=[pl.BlockSpec((1, QP, pps * page), lambda b, g, pt: (b, 0, g)),
                   pl.BlockSpec((1, QP, page), bmap)])
    return pl.pallas_call(
        functools.partial(_dsa_sample_score_kernel, pps=pps),
        grid_spec=grid_spec,
        out_shape=[jax.ShapeDtypeStruct((nb, QP, n_pages * page), F32),
                   jax.ShapeDtypeStruct((nb, QP, page), F32)],
        compiler_params=_cparams(2),
        name="dsa_sample_scores",
    )(page_table, qcat, kw, kcn, *([kidx_t] * pps))


def _new_token_scores(s_new, steps):
    lane = lax.broadcasted_iota(I32, s_new.shape, 1)
    q = lax.broadcasted_iota(I32, s_new.shape, 0) % QP
    return jnp.where((lane <= q) & (lane < steps), s_new, -jnp.inf)


def _dsa_sample_select_kernel(sc_ref, sn_ref, t_ref, j_ref, *, steps, topk, idx_bits):
    scores = jnp.concatenate([sc_ref[...], _new_token_scores(sn_ref[...], steps)], axis=1)
    idx = lax.broadcasted_iota(I32, scores.shape, 1)
    count = lambda fn: jnp.sum(fn(0, scores, idx), axis=1, keepdims=True)
    t, j = _topk_select(count, scores.shape[0], scores.shape[1], float(topk), idx_bits)
    t_ref[...] = t
    j_ref[...] = j


def _dsa_sample_select(scores, snew, steps, topk):
    nb, _, past = scores.shape
    page = snew.shape[-1]
    rows = nb * QP
    rb = math.gcd(rows, 256)
    t, j = pl.pallas_call(
        functools.partial(_dsa_sample_select_kernel, steps=steps, topk=topk,
                          idx_bits=int(math.ceil(math.log2(past + page)))),
        grid=(rows // rb,),
        in_specs=[pl.BlockSpec((rb, past), lambda i: (i, 0)), pl.BlockSpec((rb, page), lambda i: (i, 0))],
        out_specs=[pl.BlockSpec((rb, 1), lambda i: (i, 0)), pl.BlockSpec((rb, 1), lambda i: (i, 0))],
        out_shape=[jax.ShapeDtypeStruct((rows, 1), F32), jax.ShapeDtypeStruct((rows, 1), I32)],
        compiler_params=_cparams(1),
        name="dsa_sample_select",
    )(scores.reshape(rows, past), snew.reshape(rows, page))
    return t.reshape(nb, QP, 1), j.reshape(nb, QP, 1)


def _dsa_sample_attend_kernel(pt_ref, q_ref, kn_ref, vn_ref, sc_ref, sn_ref, t_ref, j_ref,
                              blast_ref, bnew_ref, *rest, pps, n_pages, steps):
    k_refs, v_refs, o_ref = rest[:pps], rest[pps:2 * pps], rest[2 * pps]
    lhs_s, m_s, l_s, acc_s = rest[2 * pps + 1:]
    g = pl.program_id(1)
    last = pl.num_programs(1) - 1
    page = k_refs[0].shape[-1]
    past = n_pages * page
    n = pps * page
    r = N_HEADS // N_KV_HEADS
    grow = r * QP

    @pl.when(g == 0)
    def _():
        m_s[...] = jnp.full(m_s.shape, NEG_BIG, F32)
        l_s[...] = jnp.zeros(l_s.shape, F32)
        acc_s[...] = jnp.zeros(acc_s.shape, F32)
        q = q_ref[0]
        for gg in range(N_KV_HEADS):
            lhs_s[gg] = jnp.concatenate(
                [q[:, (gg * r + u) * HEAD_DIM:(gg * r + u + 1) * HEAD_DIM] for u in range(r)], axis=0).astype(BF16)

    t, j = t_ref[0], j_ref[0]

    def select(sc, idx):
        sel = (sc > t) | ((sc == t) & (idx <= j))
        return jnp.where(sel & (sc > -jnp.inf), 0.0, NEG_BIG)

    def attend(kt, vt, madd, bias):
        mrows = jnp.concatenate([madd] * r, axis=0)
        for gg in range(N_KV_HEADS):
            rows = slice(gg * grow, (gg + 1) * grow)
            feat = slice(gg * HEAD_DIM, (gg + 1) * HEAD_DIM)
            lg = _dot(lhs_s[gg], kt[feat, :]) + mrows
            if bias is not None:
                lg = lg + bias[rows, :]
            m_old = m_s[rows, :]
            m_new = jnp.maximum(m_old, jnp.max(lg, axis=1, keepdims=True))
            p = jnp.exp(lg - m_new)
            alpha = jnp.exp(m_old - m_new)
            l_s[rows, :] = alpha * l_s[rows, :] + jnp.sum(p, axis=1, keepdims=True)
            acc_s[rows, :] = alpha * acc_s[rows, :] + _dot_nt(p.astype(BF16), vt[feat, :])
            m_s[rows, :] = m_new

    def pages(bias):
        kt = jnp.concatenate([k_refs[u][0, 0] for u in range(pps)], axis=1).astype(BF16)
        vt = jnp.concatenate([v_refs[u][0, 0] for u in range(pps)], axis=1).astype(BF16)
        idx = g * n + lax.broadcasted_iota(I32, (QP, n), 1)
        attend(kt, vt, select(sc_ref[0], idx), bias)

    @pl.when(g != last)
    def _():
        pages(None)

    @pl.when(g == last)
    def _():
        pages(jnp.concatenate([jnp.zeros((N_HEADS * QP, n - page), F32), blast_ref[...]], axis=1))
        idx = past + lax.broadcasted_iota(I32, (QP, page), 1)
        attend(_pad_t(kn_ref[0], page).astype(BF16), _pad_t(vn_ref[0], page).astype(BF16),
               select(_new_token_scores(sn_ref[0], steps), idx), bnew_ref[...])
        o = acc_s[...] / l_s[...]
        for h in range(N_HEADS):
            o_ref[0, :, h * HEAD_DIM:(h + 1) * HEAD_DIM] = o[h * QP:(h + 1) * QP, :]


def _dsa_sample(page_table, q, kn, vn, scores, snew, k_t, v_t, rel_bias, layer, steps, pps):
    nb, n_pages = page_table.shape
    page = k_t.shape[-1]
    past = n_pages * page
    nq, nkv = N_HEADS * HEAD_DIM, N_KV_HEADS * HEAD_DIM
    t, j = _dsa_sample_select(scores, snew, steps, min(TOPK_MAX, (past + steps) // 4))
    n_steps = n_pages // pps
    r = N_HEADS // N_KV_HEADS
    tab = _bias_table(rel_bias, 2 * page)
    qq = np.arange(QP)[:, None]
    off = np.arange(page)[None, :]
    blast = tab[:, page + qq - off].reshape(N_HEADS * QP, page)
    bnew = tab[:, np.maximum(qq - off, 0)].reshape(N_HEADS * QP, page)
    bmap = lambda b, g, pt: (b, 0, 0)
    grid_spec = pltpu.PrefetchScalarGridSpec(
        num_scalar_prefetch=1,
        grid=(nb, n_steps),
        in_specs=[pl.BlockSpec((1, QP, nq), bmap),
                  pl.BlockSpec((1, QP, nkv), bmap),
                  pl.BlockSpec((1, QP, nkv), bmap),
                  pl.BlockSpec((1, QP, pps * page), lambda b, g, pt: (b, 0, g)),
                  pl.BlockSpec((1, QP, page), bmap),
                  pl.BlockSpec((1, QP, 1), bmap), pl.BlockSpec((1, QP, 1), bmap),
                  pl.BlockSpec(blast.shape, lambda b, g, pt: (0, 0)),
                  pl.BlockSpec(bnew.shape, lambda b, g, pt: (0, 0))]
                 + _page_specs(k_t.shape, layer, pps) + _page_specs(v_t.shape, layer, pps),
        out_specs=pl.BlockSpec((1, QP, nq), bmap),
        scratch_shapes=[pltpu.VMEM((N_KV_HEADS, r * QP, HEAD_DIM), BF16),
                        pltpu.VMEM((N_HEADS * QP, 1), F32), pltpu.VMEM((N_HEADS * QP, 1), F32),
                        pltpu.VMEM((N_HEADS * QP, HEAD_DIM), F32)])
    return pl.pallas_call(
        functools.partial(_dsa_sample_attend_kernel, pps=pps, n_pages=n_pages, steps=steps),
        grid_spec=grid_spec,
        out_shape=jax.ShapeDtypeStruct((nb, QP, nq), F32),
        compiler_params=_cparams(2),
        name="dsa_sample_attend",
    )(page_table, q, kn, vn, scores, snew, t, j, blast, bnew, *([k_t] * pps), *([v_t] * pps))


SSD_COLS = 512


def _softplus(x):
    return jnp.maximum(x, 0.0) + jnp.log1p(jnp.exp(-jnp.abs(x)))


def _split3(a):
    p1 = a.astype(BF16)
    r1 = a - p1.astype(F32)
    p2 = r1.astype(BF16)
    return p1, p2, (r1 - p2.astype(F32)).astype(BF16)


def _dot_sel_rhs(a, e):
    p1, p2, p3 = _split3(a)
    return _dot(p1, e) + (_dot(p2, e) + _dot(p3, e))


def _dot_sel_lhs(e, a):
    p1, p2, p3 = _split3(a)
    return _dot(e, p1) + (_dot(e, p2) + _dot(e, p3))


def _ssd_prompt_kernel(z_ref, xbc_ref, dtr_ref, cw_ref, cb_ref, dtb_ref, alog_ref, dexp_ref, ng_ref,
                       e_ref, y_ref, nbuf_ref, hf_ref, prev_s, xc_s, y_s, h_s, *, width, nh, inner):
    i = pl.program_id(1)
    last = pl.num_programs(1) - 1
    qn, cdim = xbc_ref.shape[1], xbc_ref.shape[2]
    ngrp, ns, hd = SSD_GROUPS, SSD_STATE, SSD_HEADDIM
    hpg = nh // ngrp

    @pl.when(i == 0)
    def _():
        prev_s[...] = jnp.zeros(prev_s.shape, F32)
        h_s[...] = jnp.zeros(h_s.shape, F32)

    for cb in range(cdim // SSD_COLS):
        cols = slice(cb * SSD_COLS, (cb + 1) * SSD_COLS)
        xr = xbc_ref[0, :, cols]
        prev = prev_s[:, cols]
        conv = xr * cw_ref[width - 1:width, cols]
        for k in range(1, width):
            conv = conv + _shift_rows(xr, prev, k) * cw_ref[width - 1 - k:width - k, cols]
        xc_s[:, cols] = _silu(conv + cb_ref[:, cols])
        prev_s[:, cols] = xr[qn - SUBLANE:, :]
        nbuf_ref[0, :, cols] = xr[qn - (width - 1):, :]

    hl = lax.broadcasted_iota(I32, (qn, LANE), 1)
    dt = jnp.where(hl < nh, _softplus(dtr_ref[0] + dtb_ref[...]), 0.0)
    dta = dt * (-jnp.exp(alog_ref[...]))
    qrow = lax.broadcasted_iota(I32, (qn, qn), 0)
    kcol = lax.broadcasted_iota(I32, (qn, qn), 1)
    causal = kcol <= qrow
    cum = _dot_sel_lhs(jnp.where(causal, 1.0, 0.0).astype(BF16), dta)
    cum_t = cum.T
    cum_last = cum[qn - 1:qn, :]
    ecum = jnp.exp(cum)
    dtw = dt * jnp.exp(cum_last - cum)
    e_last = jnp.exp(cum_last)
    lane_p = lax.broadcasted_iota(I32, (qn, LANE), 1)
    row_p = lax.broadcasted_iota(I32, (LANE, ns), 0)

    for g in range(ngrp):
        gl = slice(g * SSD_COLS, (g + 1) * SSD_COLS)
        e_g = e_ref[:, gl]
        xs_g = xc_s[:, gl]
        dtx_g = xs_g * _dot_sel_rhs(dt, e_g)
        dtxw_g = xs_g * _dot_sel_rhs(dtw, e_g)
        ecx_g = _dot_sel_rhs(ecum, e_g)
        bm = xc_s[:, inner + g * ns:inner + (g + 1) * ns].astype(BF16)
        cm = xc_s[:, inner + (ngrp + g) * ns:inner + (ngrp + g + 1) * ns].astype(BF16)
        cbm = _dot_nt(cm, bm)
        for pi in range(hpg // 2):
            ha = g * hpg + 2 * pi
            pls = slice(pi * LANE, (pi + 1) * LANE)
            lanes = slice(ha * hd, ha * hd + LANE)
            dtx_p = dtx_g[:, pls].astype(BF16)
            yds = []
            for h in (ha, ha + 1):
                seg = cum[:, h:h + 1] - cum_t[h:h + 1, :]
                dec = jnp.exp(jnp.where(causal, seg, -jnp.inf))
                yds.append(_dot((cbm * dec).astype(BF16), dtx_p))
            yd = jnp.where(lane_p < hd, yds[0], yds[1])
            hp = h_s[ha // 2]
            yo = _dot_nt(cm, hp.astype(BF16)) * ecx_g[:, pls]
            y_s[:, lanes] = yd + yo + dexp_ref[:, lanes] * xs_g[:, pls]
            s_new = _dot(dtxw_g[:, pls].T.astype(BF16), bm)
            cd = jnp.where(row_p < hd, e_last[:, ha:ha + 1], e_last[:, ha + 1:ha + 2])
            h_s[ha // 2] = hp * cd + s_new

    for g in range(ngrp):
        gl = slice(g * SSD_COLS, (g + 1) * SSD_COLS)
        yg = y_s[:, gl] * _silu(z_ref[0, :, gl])
        ms = jnp.mean(yg * yg, axis=-1, keepdims=True)
        y_ref[0, :, gl] = ((yg * lax.rsqrt(ms + EPS)) * ng_ref[:, gl]).astype(y_ref.dtype)

    @pl.when(i == last)
    def _():
        hf_ref[0] = h_s[...]


def _head_expand(nh, hd):
    e = np.zeros((LANE, nh * hd), np.float32)
    for h in range(nh):
        e[h, h * hd:(h + 1) * hd] = 1.0
    return jnp.asarray(e, BF16)


def _pad_lanes(v, n=LANE):
    v = v.reshape(1, -1)
    return jnp.pad(v, ((0, 0), (0, n - v.shape[1])))


def _ssd_prompt(z, xbc, dtr, conv_w, conv_b, dt_bias, a_log, d_skip, norm_g):
    bsz, seq, inner = z.shape
    cdim = xbc.shape[-1]
    nh = dt_bias.shape[0]
    width = conv_w.shape[0]
    qn = math.gcd(seq, SSD_CHUNK)
    assert qn == SSD_CHUNK and inner == SSD_GROUPS * SSD_COLS and cdim % SSD_COLS == 0
    dexp = jnp.repeat(d_skip, SSD_HEADDIM).reshape(1, inner)
    blk = lambda n: pl.BlockSpec((1, qn, n), lambda b, i: (b, i, 0))
    y, nbuf, hf = pl.pallas_call(
        functools.partial(_ssd_prompt_kernel, width=width, nh=nh, inner=inner),
        grid=(bsz, seq // qn),
        in_specs=[blk(inner), blk(cdim), blk(LANE),
                  _const2((width, cdim)), _const2((1, cdim)), _const2((1, LANE)), _const2((1, LANE)),
                  _const2((1, inner)), _const2((1, inner)), _const2((LANE, inner))],
        out_specs=[blk(inner),
                   pl.BlockSpec((1, width - 1, cdim), lambda b, i: (b, 0, 0)),
                   pl.BlockSpec((1, nh // 2, LANE, SSD_STATE), lambda b, i: (b, 0, 0, 0))],
        out_shape=[jax.ShapeDtypeStruct((bsz, seq, inner), BF16),
                   jax.ShapeDtypeStruct((bsz, width - 1, cdim), F32),
                   jax.ShapeDtypeStruct((bsz, nh // 2, LANE, SSD_STATE), F32)],
        scratch_shapes=[pltpu.VMEM((SUBLANE, cdim), F32), pltpu.VMEM((qn, cdim), F32),
                        pltpu.VMEM((qn, inner), F32), pltpu.VMEM((nh // 2, LANE, SSD_STATE), F32)],
        compiler_params=_cparams(2),
        name="ssd_prompt",
    )(z, xbc, dtr, conv_w, conv_b.reshape(1, cdim), _pad_lanes(dt_bias), _pad_lanes(a_log),
      dexp, norm_g.reshape(1, inner), _head_expand(nh, SSD_HEADDIM))
    return y, nbuf, hf.reshape(bsz, nh, SSD_HEADDIM, SSD_STATE)


def _ssd_prep_sample_kernel(xbc_ref, buf_ref, dtr_ref, cw_ref, cb_ref, dtb_ref, xc_ref, dt_ref, nbuf_ref,
                            *, width, nb, steps, nh):
    cdim = xbc_ref.shape[1]
    for cb in range(cdim // SSD_COLS):
        cols = slice(cb * SSD_COLS, (cb + 1) * SSD_COLS)
        ext = jnp.concatenate([buf_ref[:, cols], xbc_ref[:, cols]], axis=0)
        conv = ext[0:steps * nb] * cw_ref[0:1, cols]
        for k in range(1, width):
            conv = conv + ext[k * nb:(k + steps) * nb] * cw_ref[k:k + 1, cols]
        xc_ref[:, cols] = _silu(conv + cb_ref[:, cols])
        nbuf_ref[:, cols] = ext[steps * nb:]
    hl = lax.broadcasted_iota(I32, dtr_ref.shape, 1)
    dt_ref[...] = jnp.where(hl < nh, _softplus(dtr_ref[...] + dtb_ref[...]), 0.0)


def _ssd_prep_sample(xbc_tb, buf_kb, dtr_tb, conv_w, conv_b, dt_bias, nb, steps):
    rows, cdim = xbc_tb.shape
    width = conv_w.shape[0]
    return pl.pallas_call(
        functools.partial(_ssd_prep_sample_kernel, width=width, nb=nb, steps=steps, nh=dt_bias.shape[0]),
        out_shape=[jax.ShapeDtypeStruct((rows, cdim), F32),
                   jax.ShapeDtypeStruct((rows, LANE), F32),
                   jax.ShapeDtypeStruct(((width - 1) * nb, cdim), F32)],
        compiler_params=pltpu.CompilerParams(vmem_limit_bytes=VMEM_LIMIT),
        name="ssd_prep_sample",
    )(xbc_tb, buf_kb, dtr_tb, conv_w, conv_b.reshape(1, cdim), _pad_lanes(dt_bias))


def _ssd_scan_sample_kernel(xc_ref, dt_ref, z_ref, h0_ref, alog_ref, dexp_ref, ng_ref, e_ref,
                            y_ref, hf_ref, *, steps, inner):
    ngrp, ns = SSD_GROUPS, SSD_STATE
    xc, dt = xc_ref[0], dt_ref[0]
    row = lax.broadcasted_iota(I32, (QP, LANE), 0)
    cum = dt * (-jnp.exp(alog_ref[...]))
    for sft in (1, 2, 4):
        cum = cum + jnp.where(row >= sft, pltpu.roll(cum, sft, 0), 0.0)
    cum_last = cum[QP - 1:QP, :]
    parts = [dt, jnp.exp(cum), dt * jnp.exp(cum_last - cum), jnp.broadcast_to(jnp.exp(cum_last), (QP, LANE))]
    for s in range(steps):
        parts.append(jnp.where(row >= s, jnp.exp(cum - cum[s:s + 1, :]), 0.0))
    stack = jnp.concatenate(parts, axis=0)

    for g in range(ngrp):
        gl = slice(g * SSD_COLS, (g + 1) * SSD_COLS)
        ex = _dot_sel_rhs(stack, e_ref[:, gl])
        dt_x, ecum_x, dtw_x, el_x = (ex[k * QP:(k + 1) * QP] for k in range(4))
        xs_g = xc[:, gl]
        dtx = xs_g * dt_x
        bm = xc[:, inner + g * ns:inner + (g + 1) * ns]
        cm = xc[:, inner + (ngrp + g) * ns:inner + (ngrp + g + 1) * ns].astype(BF16)
        bmp = jnp.concatenate([bm, jnp.zeros((LANE - QP, ns), F32)], axis=0).astype(BF16)
        cbm = _dot_nt(cm, bmp)
        yd = jnp.zeros((QP, SSD_COLS), F32)
        for s in range(steps):
            yd = yd + (ex[(4 + s) * QP:(5 + s) * QP] * cbm[:, s:s + 1]) * dtx[s:s + 1, :]
        h0g = h0_ref[0, gl, :]
        yo = _dot_nt(cm, h0g.astype(BF16)) * ecum_x
        y = (yd + yo + dexp_ref[:, gl] * xs_g) * _silu(z_ref[0, :, gl])
        ms = jnp.mean(y * y, axis=-1, keepdims=True)
        y_ref[0, :, gl] = (y * lax.rsqrt(ms + EPS)) * ng_ref[:, gl]
        tm = jnp.concatenate([xs_g * dtw_x, el_x[0:1], jnp.zeros((LANE - QP - 1, SSD_COLS), F32)], axis=0)
        tt = tm.T
        hf_ref[0, gl, :] = h0g * tt[:, QP:QP + 1] + _dot(tt.astype(BF16), bmp)


def _ssd_scan_sample(xc, dt, z, h0, a_log, d_skip, norm_g, steps):
    nb, _, cdim = xc.shape
    inner = z.shape[-1]
    nh = a_log.shape[0]
    rows_h = nh * SSD_HEADDIM
    dexp = jnp.repeat(d_skip, SSD_HEADDIM).reshape(1, inner)
    one = lambda shape: pl.BlockSpec(shape, lambda b: (0,) * len(shape))
    return pl.pallas_call(
        functools.partial(_ssd_scan_sample_kernel, steps=steps, inner=inner),
        grid=(nb,),
        in_specs=[pl.BlockSpec((1, QP, cdim), lambda b: (b, 0, 0)),
                  pl.BlockSpec((1, QP, LANE), lambda b: (b, 0, 0)),
                  pl.BlockSpec((1, QP, inner), lambda b: (b, 0, 0)),
                  pl.BlockSpec((1, rows_h, SSD_STATE), lambda b: (b, 0, 0)),
                  one((1, LANE)), one((1, inner)), one((1, inner)), one((LANE, inner))],
        out_specs=[pl.BlockSpec((1, QP, inner), lambda b: (b, 0, 0)),
                   pl.BlockSpec((1, rows_h, SSD_STATE), lambda b: (b, 0, 0))],
        out_shape=[jax.ShapeDtypeStruct((nb, QP, inner), F32),
                   jax.ShapeDtypeStruct((nb, rows_h, SSD_STATE), F32)],
        compiler_params=_cparams(1),
        name="ssd_scan_sample",
    )(xc, dt, z, h0, _pad_lanes(a_log), dexp, norm_g.reshape(1, inner), _head_expand(nh, SSD_HEADDIM))


TM = 256


def _to_steps(a, nb, steps):
    n = a.shape[-1]
    return a.reshape(nb, steps, n).transpose(1, 0, 2).reshape(steps * nb, n)


def _to_batch(a, nb, steps):
    n = a.shape[-1]
    return a.reshape(steps, nb, n).transpose(1, 0, 2).reshape(1, nb * steps, n)


def _pad_q(a, nb, steps):
    n = a.shape[-1]
    return jnp.pad(a.reshape(nb, steps, n), ((0, 0), (0, QP - steps), (0, 0)))


def kernel(x_prompt, x_sample, cache_k, cache_v, cache_kidx, state_s5_re, state_s5_im, state_sconv, state_ssd, state_ssd_conv, page_table, c_prompt, c_sample, rel_bias, ada_w, ada_b, norm_mix, norm_mlp, norm_final, attn_w_in, attn_w_out, s5_lam_re, s5_lam_im, s5_log_dt, s5_b_re, s5_b_im, s5_c_re, s5_c_im, s5_d, s5_w_glu, sc_w_in, sc_w_conv, sc_w_out, ssd_w_in, ssd_conv_w, ssd_conv_b, ssd_dt_bias, ssd_a_log, ssd_d, ssd_norm, ssd_w_out, mlp_w1, mlp_w2):
    bp, seq, d = x_prompt.shape
    nb, steps, _ = x_sample.shape
    depth = ada_w.shape[0]
    n_mixers = 4
    rs = nb * steps
    tm = min(TM, seq)

    rows = bp + nb
    c_all = jnp.pad(jnp.concatenate([c_prompt, c_sample], axis=0), ((0, (-rows) % SUBLANE), (0, 0)))
    ada = _ada(c_all, ada_w, ada_b)

    xp = x_prompt
    xs = x_sample.reshape(1, rs, d)
    outs = {name: [] for name in ("kp", "vp", "kip", "ks", "vs", "kis", "s5pr", "s5pi", "s5sr", "s5si",
                                  "scp", "scs", "ssdp", "ssdcp", "ssds", "ssdcs")}
    nq, nkv = N_HEADS * HEAD_DIM, N_KV_HEADS * HEAD_DIM
    w1_all, w2_all = mlp_w1.astype(BF16), mlp_w2.astype(BF16)
    for i in range(depth):
        m, j = i % n_mixers, i // n_mixers
        mp = [ada[i, :bp, k * d:(k + 1) * d].reshape(bp, 1, d) for k in range(6)]
        ms = [jnp.repeat(ada[i, bp:bp + nb, k * d:(k + 1) * d], steps, axis=0).reshape(1, rs, d)
              for k in range(6)]
        g_mix = norm_mix[i]
        mix_p = mix_s = None
        if m == 0:
            w_in = attn_w_in[j]
            w_qkv = w_in[:, :nq + 2 * nkv].astype(BF16)
            n_idx = IDX_HEADS * IDX_DIM + LANE
            w_idx = jnp.pad(w_in[:, nq + 2 * nkv:], ((0, 0), (0, n_idx - (w_in.shape[1] - nq - 2 * nkv))))
            w_out = attn_w_out[j].astype(BF16)
            page = cache_k.shape[2]
            q, k, v, kw, qcat, kcat, kaug, vaug, ki = _attn_proj(xp, mp[0], mp[1], g_mix, w_qkv, w_idx,
                                                                 2 * MAX_DISTANCE, True, page)
            o = _dsa_prompt(q, qcat, kw, kcat, kaug, vaug, rel_bias, MAX_DISTANCE)
            mix_p = ("proj", o, mp[2], w_out)
            to_pages = lambda t: jnp.transpose(t.reshape(bp, seq // page, N_KV_HEADS, HEAD_DIM, page), (0, 1, 4, 2, 3))
            outs["kp"].append(to_pages(k))
            outs["vp"].append(to_pages(v))
            outs["kip"].append(jnp.transpose(ki, (0, 1, 3, 2)))
            q, k, v, kw, qcat, kcat, _, _ = _attn_proj(xs, ms[0], ms[1], g_mix, w_qkv, w_idx, rs, False)
            pps = math.gcd(PAGES_PER_STEP, page_table.shape[1])
            qc_p, kw_p = _pad_q(qcat.astype(F32), nb, steps), _pad_q(kw, nb, steps)
            scores, snew = _dsa_sample_scores(page_table, qc_p, kw_p, _pad_q(kcat.astype(F32), nb, steps),
                                              _pages_t(cache_kidx), j, pps)
            o = _dsa_sample(page_table, _pad_q(q, nb, steps), _pad_q(k, nb, steps), _pad_q(v, nb, steps),
                            scores, snew, _pages_t(cache_k), _pages_t(cache_v), rel_bias, j, steps, pps)
            mix_s = ("proj", o[:, :steps].reshape(1, rs, nq), ms[2], w_out)
            outs["ks"].append(k.reshape(nb, steps, N_KV_HEADS, HEAD_DIM))
            outs["vs"].append(v.reshape(nb, steps, N_KV_HEADS, HEAD_DIM))
            outs["kis"].append(kw[..., :IDX_DIM].reshape(nb, steps, IDX_DIM))
        elif m == 1:
            ar, ai, wb, wc = _s5_weights(s5_lam_re[j], s5_lam_im[j], s5_log_dt[j], s5_b_re[j], s5_b_im[j],
                                         s5_c_re[j], s5_c_im[j])
            w_glu = s5_w_glu[j].astype(BF16)
            grp, nst = s5_lam_re.shape[1:]
            z, fr, fi = _s5_prompt(xp, mp[0], mp[1], g_mix, s5_d[j], ar, ai, wb, wc)
            mix_p = ("glu", z, mp[2], w_glu)
            outs["s5pr"].append(fr.reshape(bp, grp, nst))
            outs["s5pi"].append(fi.reshape(bp, grp, nst))
            z, fr, fi = _s5_sample(_to_steps(xs, nb, steps), _to_steps(ms[0], nb, steps),
                                   _to_steps(ms[1], nb, steps), g_mix, s5_d[j], ar, ai, wb, wc,
                                   state_s5_re[j].reshape(nb, grp * nst), state_s5_im[j].reshape(nb, grp * nst),
                                   nb, steps)
            mix_s = ("glu", _to_batch(z, nb, steps), ms[2], w_glu)
            outs["s5sr"].append(fr.reshape(nb, grp, nst))
            outs["s5si"].append(fi.reshape(nb, grp, nst))
        elif m == 2:
            w_in = sc_w_in[j].astype(BF16)
            w_out = sc_w_out[j].astype(BF16)
            width = sc_w_conv.shape[1]
            xp, nbuf = _sconv_prompt(xp, mp[0], mp[1], g_mix, w_in, mp[2], jnp.zeros((bp, width - 1, d), F32),
                                     sc_w_conv[j], w_out, tm)
            outs["scp"].append(nbuf)
            (p,) = _proj(xs, ms[0], ms[1], g_mix, w_in, (3 * d,), rs)
            buf = state_sconv[j].transpose(1, 0, 2).reshape((width - 1) * nb, d)
            y, nbuf = _sconv_sample(_to_steps(p, nb, steps), _to_steps(xs, nb, steps), _to_steps(ms[2], nb, steps),
                                    buf, sc_w_conv[j], w_out, nb, steps)
            xs = _to_batch(y, nb, steps)
            outs["scs"].append(nbuf.reshape(width - 1, nb, d).transpose(1, 0, 2))
        else:
            inner = ssd_norm.shape[1]
            cdim = ssd_conv_w.shape[2]
            nh = ssd_dt_bias.shape[1]
            width = ssd_conv_w.shape[1]
            w_in = jnp.pad(ssd_w_in[j], ((0, 0), (0, LANE - nh))).astype(BF16)
            w_out = ssd_w_out[j].astype(BF16)
            z, xbc, dtr = _proj(xp, mp[0], mp[1], g_mix, w_in, (inner, cdim, LANE), tm)
            y, nbuf, hf = _ssd_prompt(z, xbc, dtr, ssd_conv_w[j], ssd_conv_b[j], ssd_dt_bias[j], ssd_a_log[j],
                                      ssd_d[j], ssd_norm[j])
            mix_p = ("proj", y, mp[2], w_out)
            outs["ssdp"].append(hf)
            outs["ssdcp"].append(nbuf)
            z, xbc, dtr = _proj(xs, ms[0], ms[1], g_mix, w_in, (inner, cdim, LANE), rs)
            buf = state_ssd_conv[j].transpose(1, 0, 2).reshape((width - 1) * nb, cdim)
            xc, dt, nbuf = _ssd_prep_sample(_to_steps(xbc, nb, steps), buf, _to_steps(dtr, nb, steps),
                                            ssd_conv_w[j], ssd_conv_b[j], ssd_dt_bias[j], nb, steps)
            y, hf = _ssd_scan_sample(_pad_q(_to_batch(xc, nb, steps), nb, steps),
                                     _pad_q(_to_batch(dt, nb, steps), nb, steps), _pad_q(z, nb, steps),
                                     state_ssd[j].reshape(nb, nh * SSD_HEADDIM, SSD_STATE),
                                     ssd_a_log[j], ssd_d[j], ssd_norm[j], steps)
            mix_s = ("proj", y[:, :steps].reshape(1, rs, inner), ms[2], w_out)
            outs["ssds"].append(hf.reshape(nb, nh, SSD_HEADDIM, SSD_STATE))
            outs["ssdcs"].append(nbuf.reshape(width - 1, nb, cdim).transpose(1, 0, 2))
        fin = i == depth - 1
        xp = _mlp(xp, mp[3], mp[4], mp[5], norm_mlp[i], w1_all, w2_all, i, norm_final, fin, tm, mix_p)
        xs = _mlp(xs, ms[3], ms[4], ms[5], norm_mlp[i], w1_all, w2_all, i, norm_final, fin, rs, mix_s)
    st = jnp.stack
    return (xp, xs.reshape(nb, steps, d), st(outs["kp"]), st(outs["vp"]), st(outs["kip"]),
            st(outs["ks"]), st(outs["vs"]), st(outs["kis"]), st(outs["s5pr"]), st(outs["s5pi"]),
            st(outs["s5sr"]), st(outs["s5si"]), st(outs["scp"]), st(outs["scs"]),
            st(outs["ssdp"]), st(outs["ssdcp"]), st(outs["ssds"]), st(outs["ssdcs"]))
```

```python
import functools
import math

import jax
import jax.numpy as jnp
import numpy as np
from jax import lax
from jax.experimental import pallas as pl
from jax.experimental.pallas import tpu as pltpu

F32 = jnp.float32
BF16 = jnp.bfloat16
I32 = jnp.int32

EPS = 1e-6
N_HEADS = 16
HEAD_DIM = 64
N_KV_HEADS = 4
IDX_HEADS = 8
IDX_DIM = 64
TOPK_MAX = 256
N_BUCKETS = 32
MAX_DISTANCE = 128
S5_GROUP = 16
S5_STATE = 64
SSD_HEADDIM = 64
SSD_GROUPS = 4
SSD_STATE = 128
SSD_CHUNK = 128

LANE = 128
SUBLANE = 8
VMEM_LIMIT = 56 * 1024 * 1024
NEG_BIG = -1e30
INT_MIN = -2147483648


def _cparams(n_axes):
    return pltpu.CompilerParams(dimension_semantics=("arbitrary",) * n_axes,
                                vmem_limit_bytes=VMEM_LIMIT)


def _dot(a, b):
    return jnp.dot(a, b, preferred_element_type=F32)


def _dot_nt(a, b):
    return lax.dot_general(a, b, (((1,), (1,)), ((), ())), preferred_element_type=F32)


def _split(a):
    hi = a.astype(BF16)
    lo = (a - hi.astype(F32)).astype(BF16)
    return hi, lo


def _dot3(a, b):
    ah, al = _split(a)
    bh, bl = _split(b)
    return _dot(ah, bh) + (_dot(al, bh) + _dot(ah, bl))


def _rms_mod(x, g, shift, scale):
    y = x * lax.rsqrt(jnp.mean(x * x, axis=-1, keepdims=True) + EPS)
    return (y * g) * (1.0 + scale) + shift


def _sigmoid(x):
    return 1.0 / (1.0 + jnp.exp(-x))


def _silu(x):
    return x * _sigmoid(x)


def _row_block(arr, tm):
    n = arr.shape[-1]
    if arr.shape[1] == 1:
        return pl.BlockSpec((1, 1, n), lambda b, i: (b, 0, 0))
    return pl.BlockSpec((1, tm, n), lambda b, i: (b, i, 0))


def _const2(shape):
    return pl.BlockSpec(shape, lambda b, i: (0,) * len(shape))


def _ada_kernel(c_ref, w_ref, b_ref, o_ref):
    o_ref[0] = _dot3(_silu(c_ref[...]), w_ref[0]) + b_ref[0]


def _ada(c_all, ada_w, ada_b):
    depth, d, n = ada_w.shape
    rows = c_all.shape[0]
    tn = 1536
    return pl.pallas_call(
        _ada_kernel,
        grid=(depth, n // tn),
        in_specs=[pl.BlockSpec((rows, d), lambda l, j: (0, 0)),
                  pl.BlockSpec((1, d, tn), lambda l, j: (l, 0, j)),
                  pl.BlockSpec((1, 1, tn), lambda l, j: (l, 0, j))],
        out_specs=pl.BlockSpec((1, rows, tn), lambda l, j: (l, 0, j)),
        out_shape=jax.ShapeDtypeStruct((depth, rows, n), F32),
        compiler_params=_cparams(2),
        name="ada",
    )(c_all, ada_w, ada_b.reshape(depth, 1, n))


def _proj_kernel(x_ref, sh_ref, sc_ref, g_ref, w_ref, *o_refs, splits):
    h = _rms_mod(x_ref[0], g_ref[...], sh_ref[0], sc_ref[0]).astype(BF16)
    off = 0
    for o_ref, n in zip(o_refs, splits):
        o_ref[0] = _dot(h, w_ref[:, off:off + n])
        off += n


def _proj(x, shift, scale, g, w_bf16, splits, tm):
    bsz, seq, d = x.shape
    n = w_bf16.shape[1]
    assert sum(splits) == n and seq % tm == 0
    return pl.pallas_call(
        functools.partial(_proj_kernel, splits=tuple(splits)),
        grid=(bsz, seq // tm),
        in_specs=[pl.BlockSpec((1, tm, d), lambda b, i: (b, i, 0)),
                  _row_block(shift, tm), _row_block(scale, tm),
                  _const2((1, d)), _const2((d, n))],
        out_specs=[pl.BlockSpec((1, tm, s), lambda b, i: (b, i, 0)) for s in splits],
        out_shape=[jax.ShapeDtypeStruct((bsz, seq, s), F32) for s in splits],
        compiler_params=_cparams(2),
        name="proj",
    )(x, shift, scale, g.reshape(1, d), w_bf16)


def _aug_heads(t, col):
    rows = t.shape[0]
    lane = lax.broadcasted_iota(I32, (rows, LANE - HEAD_DIM), 1)
    extra = jnp.where(lane == 0, col, 0.0).astype(F32)
    parts = []
    for g in range(N_KV_HEADS):
        parts += [t[:, g * HEAD_DIM:(g + 1) * HEAD_DIM], extra]
    return jnp.concatenate(parts, axis=1).astype(BF16)


def _attn_proj_kernel(x_ref, sh_ref, sc_ref, g_ref, w_ref, wi_ref,
                      q_ref, k_ref, v_ref, kw_ref, qcat_ref, kcat_ref, kaug_ref, vaug_ref, *ki_refs,
                      nq, nkv, nqi, key_major):
    (ki_ref,) = ki_refs if key_major else (None,)
    h = _rms_mod(x_ref[0], g_ref[...], sh_ref[0], sc_ref[0])
    hb = h.astype(BF16)
    q_ref[0] = (_dot(hb, w_ref[:, 0:nq]) * (HEAD_DIM ** -0.5)).astype(q_ref.dtype)
    k = _dot(hb, w_ref[:, nq:nq + nkv])
    v = _dot(hb, w_ref[:, nq + nkv:nq + 2 * nkv])
    r = _dot3(h, wi_ref[...])
    kw = r[:, nqi:nqi + LANE]
    kw_ref[0] = kw
    if key_major:
        page = k_ref.shape[-1]
        kt, vt, kwt = k.T, v.T, kw.T
        for p in range(k_ref.shape[1]):
            k_ref[0, p] = kt[:, p * page:(p + 1) * page]
            v_ref[0, p] = vt[:, p * page:(p + 1) * page]
            ki_ref[0, p] = kwt[0:IDX_DIM, p * page:(p + 1) * page]
    else:
        k_ref[0] = k
        v_ref[0] = v
    zero = jnp.zeros((h.shape[0], IDX_DIM), BF16)
    parts = []
    for hh in range(IDX_HEADS):
        ah, al = _split(r[:, hh * IDX_DIM:(hh + 1) * IDX_DIM])
        parts += [ah, al, ah, zero]
    qcat_ref[0] = jnp.concatenate(parts, axis=1)
    kcat = _idx_rhs(kw[:, 0:IDX_DIM])
    kaug = _aug_heads(k, 0.0)
    vaug_ref[0] = _aug_heads(v, 1.0)
    if key_major:
        kcat_ref[0, 0] = kcat.astype(F32).T.astype(BF16)
        kaug_ref[0, 0] = kaug.astype(F32).T.astype(BF16)
    else:
        kcat_ref[0] = kcat
        kaug_ref[0] = kaug


def _attn_proj(x, shift, scale, g, w_qkv_bf16, w_idx, tm, key_major, page=None):
    bsz, seq, d = x.shape
    nq = N_HEADS * HEAD_DIM
    nkv = N_KV_HEADS * HEAD_DIM
    nqi = IDX_HEADS * IDX_DIM
    rows = lambda n, dt: (pl.BlockSpec((1, tm, n), lambda b, i: (b, i, 0)), jax.ShapeDtypeStruct((bsz, seq, n), dt))
    cols = lambda n, dt: (pl.BlockSpec((1, 1, n, tm), lambda b, i: (b, i, 0, 0)),
                          jax.ShapeDtypeStruct((bsz, seq // tm, n, tm), dt))
    keys = cols if key_major else rows
    if key_major:
        ppt = tm // page
        paged = lambda n: (pl.BlockSpec((1, ppt, n, page), lambda b, i: (b, i, 0, 0)),
                           jax.ShapeDtypeStruct((bsz, seq // page, n, page), F32))
        kv = [paged(nkv), paged(nkv)]
    else:
        kv = [rows(nkv, F32), rows(nkv, F32)]
    outs = [rows(nq, BF16 if key_major else F32)] + kv + [rows(LANE, F32), rows(4 * nqi, BF16),
                                   keys(4 * IDX_DIM, BF16), keys(N_KV_HEADS * LANE, BF16),
                                   rows(N_KV_HEADS * LANE, BF16)]
    if key_major:
        outs.append(paged(IDX_DIM))
    return pl.pallas_call(
        functools.partial(_attn_proj_kernel, nq=nq, nkv=nkv, nqi=nqi, key_major=key_major),
        grid=(bsz, seq // tm),
        in_specs=[pl.BlockSpec((1, tm, d), lambda b, i: (b, i, 0)),
                  _row_block(shift, tm), _row_block(scale, tm),
                  _const2((1, d)), _const2(w_qkv_bf16.shape), _const2(w_idx.shape)],
        out_specs=[o[0] for o in outs],
        out_shape=[o[1] for o in outs],
        compiler_params=_cparams(2),
        name="attn_proj",
    )(x, shift, scale, g.reshape(1, d), w_qkv_bf16, w_idx)


def _mlp_kernel(*refs, ff_chunk, final_norm, mixer_out):
    if mixer_out is None:
        x_ref, sh_ref, sc_ref, gate_ref, g_ref, w1_ref, w2_ref, gf_ref, y_ref = refs
        x = x_ref[0]
    else:
        o_ref, gmix_ref, wo_ref, x_ref, sh_ref, sc_ref, gate_ref, g_ref, w1_ref, w2_ref, gf_ref, y_ref = refs
        d = x_ref.shape[-1]
        ob = o_ref[0].astype(BF16)
        if mixer_out == "glu":
            t = _dot(ob, wo_ref[:, :d]) * _sigmoid(_dot(ob, wo_ref[:, d:]))
        else:
            t = _dot(ob, wo_ref[...])
        x = x_ref[0] + gmix_ref[0] * t
    h = _rms_mod(x, g_ref[...], sh_ref[0], sc_ref[0]).astype(BF16)
    dff = w1_ref.shape[-1]
    acc = jnp.zeros(x.shape, F32)
    for c in range(dff // ff_chunk):
        a = jnp.maximum(_dot(h, w1_ref[0, :, c * ff_chunk:(c + 1) * ff_chunk]), 0.0)
        acc = acc + _dot((a * a).astype(BF16), w2_ref[0, c * ff_chunk:(c + 1) * ff_chunk, :])
    y = x + gate_ref[0] * acc
    if final_norm:
        y = (y * lax.rsqrt(jnp.mean(y * y, axis=-1, keepdims=True) + EPS)) * gf_ref[...]
    y_ref[0] = y


def _resident(shape):
    return pl.BlockSpec(shape, lambda b, i: (0,) * len(shape), pipeline_mode=pl.Buffered(1))


def _mlp(x, shift, scale, gate, g, w1_bf16, w2_bf16, layer, g_final, final_norm, tm, mixer=None):
    bsz, seq, d = x.shape
    dff = w1_bf16.shape[-1]
    layer_w = lambda r, c: pl.BlockSpec((1, r, c), lambda b, i: (layer, 0, 0), pipeline_mode=pl.Buffered(1))
    rows = lambda n: pl.BlockSpec((1, tm, n), lambda b, i: (b, i, 0))
    args, specs, kind = [], [], None
    if mixer is not None:
        kind, o, gmix, wo = mixer
        args += [o, gmix, wo]
        specs += [rows(o.shape[-1]), _row_block(gmix, tm), _resident(wo.shape)]
    args += [x, shift, scale, gate, g.reshape(1, d), w1_bf16, w2_bf16, g_final.reshape(1, d)]
    specs += [rows(d), _row_block(shift, tm), _row_block(scale, tm), _row_block(gate, tm),
              _const2((1, d)), layer_w(d, dff), layer_w(dff, d), _const2((1, d))]
    return pl.pallas_call(
        functools.partial(_mlp_kernel, ff_chunk=1024, final_norm=final_norm, mixer_out=kind),
        grid=(bsz, seq // tm),
        in_specs=specs,
        out_specs=rows(d),
        out_shape=jax.ShapeDtypeStruct((bsz, seq, d), F32),
        compiler_params=_cparams(2),
        name="mlp",
    )(*args)


S5_CH = 128
S5_BLK = (S5_CH // S5_GROUP) * S5_STATE


def _s5_disc_kernel(lr_ref, li_ref, ldt_ref, br_ref, bi_ref, ar_ref, ai_ref, bbr_ref, bbi_ref):
    lr, li = lr_ref[...], li_ref[...]
    dt = jnp.exp(ldt_ref[...])
    mag = jnp.exp(lr * dt)
    ab_re, ab_im = mag * jnp.cos(li * dt), mag * jnp.sin(li * dt)
    den = lr * lr + li * li
    nr = ab_re - 1.0
    f_re = (nr * lr + ab_im * li) / den
    f_im = (ab_im * lr - nr * li) / den
    ar_ref[...] = ab_re
    ai_ref[...] = ab_im
    for c in range(br_ref.shape[0]):
        br, bi = br_ref[c], bi_ref[c]
        bbr_ref[c] = f_re * br - f_im * bi
        bbi_ref[c] = f_re * bi + f_im * br


def _s5_weights(lam_re, lam_im, log_dt, b_re, b_im, c_re, c_im):
    g, p = lam_re.shape
    gc = b_re.shape[-1]
    brt = jnp.moveaxis(b_re, 2, 0)
    bit = jnp.moveaxis(b_im, 2, 0)
    ar, ai, bbr, bbi = pl.pallas_call(
        _s5_disc_kernel,
        out_shape=[jax.ShapeDtypeStruct((g, p), F32), jax.ShapeDtypeStruct((g, p), F32),
                   jax.ShapeDtypeStruct((gc, g, p), F32), jax.ShapeDtypeStruct((gc, g, p), F32)],
        name="s5_disc",
    )(lam_re, lam_im, log_dt.reshape(g, 1), brt, bit)
    nblk = (g * gc) // S5_CH
    gpb = g // nblk
    eye = jnp.eye(gpb, dtype=F32)

    def bd_in(bb):
        t = jnp.moveaxis(bb, 0, 1).reshape(nblk, gpb, gc, p)
        return jnp.einsum('ngcp,gh->ngchp', t, eye).reshape(nblk, gpb * gc, gpb * p)

    def bd_out(cc):
        t = cc.reshape(nblk, gpb, gc, p)
        return jnp.einsum('ngcp,gh->ngphc', t, eye).reshape(nblk, gpb * p, gpb * gc)

    wb = jnp.concatenate([bd_in(bbr), bd_in(bbi)], axis=-1).astype(BF16)
    wc = jnp.concatenate([bd_out(c_re), bd_out(c_im)], axis=1).astype(BF16)
    return ar.reshape(nblk, 1, gpb * p), ai.reshape(nblk, 1, gpb * p), wb, wc


def _gelu_tanh(y):
    return 0.5 * y * (1.0 + jnp.tanh(math.sqrt(2.0 / math.pi) * (y + 0.044715 * (y * y * y))))


def _cmul(ar, ai, xr, xi):
    return ar * xr - ai * xi, ar * xi + ai * xr


def _s5_prompt_kernel(x_ref, sh_ref, sc_ref, g_ref, d_ref, ar_ref, ai_ref, wb_ref, wc_ref,
                      perm_ref, unperm_ref, z_ref, fr_ref, fi_ref, u_s, br_s, bi_s, st_r, st_i,
                      *, nseg, seg):
    i = pl.program_id(1)
    nblk = wb_ref.shape[0]
    blk = ar_ref.shape[-1]

    @pl.when(i == 0)
    def _():
        st_r[...] = jnp.zeros(st_r.shape, F32)
        st_i[...] = jnp.zeros(st_i.shape, F32)

    u_s[...] = _dot_sel_lhs(perm_ref[...], _rms_mod(x_ref[0], g_ref[...], sh_ref[0], sc_ref[0]))

    for c in range(nblk):
        lo, hi = c * S5_CH, (c + 1) * S5_CH
        uc = u_s[:, lo:hi]
        bu = _dot(uc.astype(BF16), wb_ref[c])
        br_s[...] = bu[:, :blk]
        bi_s[...] = bu[:, blk:]
        ar = jnp.broadcast_to(ar_ref[c], (nseg, blk))
        ai = jnp.broadcast_to(ai_ref[c], (nseg, blk))

        def local(j, carry):
            xr, xi = carry
            r0 = pl.multiple_of(j * nseg, nseg)
            pr, pi = _cmul(ar, ai, xr, xi)
            nr = pr + br_s[pl.ds(r0, nseg), :]
            ni = pi + bi_s[pl.ds(r0, nseg), :]
            br_s[pl.ds(r0, nseg), :] = nr
            bi_s[pl.ds(r0, nseg), :] = ni
            return nr, ni

        zero = jnp.zeros((nseg, blk), F32)
        fr, fi = lax.fori_loop(0, seg, local, (zero, zero), unroll=True)

        pr, pi = ar_ref[c], ai_ref[c]
        for _ in range(int(math.log2(seg))):
            pr, pi = _cmul(pr, pi, pr, pi)
        cr, ci = st_r[c], st_i[c]
        rows_r, rows_i = [], []
        for s in range(nseg):
            rows_r.append(cr)
            rows_i.append(ci)
            tr, ti = _cmul(pr, pi, cr, ci)
            cr, ci = tr + fr[s:s + 1], ti + fi[s:s + 1]
        st_r[c] = cr
        st_i[c] = ci
        fr_ref[0, c] = cr
        fi_ref[0, c] = ci
        dr, di = _cmul(ar, ai, jnp.concatenate(rows_r, axis=0), jnp.concatenate(rows_i, axis=0))

        def fix(j, carry):
            dr, di = carry
            r0 = pl.multiple_of(j * nseg, nseg)
            br_s[pl.ds(r0, nseg), :] = br_s[pl.ds(r0, nseg), :] + dr
            bi_s[pl.ds(r0, nseg), :] = bi_s[pl.ds(r0, nseg), :] + di
            return _cmul(ar, ai, dr, di)

        lax.fori_loop(0, seg, fix, (dr, di), unroll=True)

        y = (_dot(br_s[...].astype(BF16), wc_ref[c, :blk, :])
             - _dot(bi_s[...].astype(BF16), wc_ref[c, blk:, :]))
        y = y + d_ref[:, lo:hi] * uc
        u_s[:, lo:hi] = _gelu_tanh(y)

    z_ref[0] = _dot_sel_lhs(unperm_ref[...], u_s[...]).astype(z_ref.dtype)


def _s5_prompt(x, shift, scale, g, d_skip, ar, ai, wb, wc):
    bsz, seq, d = x.shape
    nblk, _, blk = ar.shape
    nseg, seg = SUBLANE, 32
    tm = nseg * seg
    assert seq % tm == 0
    perm = np.zeros((tm, tm), np.float32)
    for s in range(nseg):
        for j in range(seg):
            perm[j * nseg + s, s * seg + j] = 1.0
    unperm = jnp.asarray(perm.T, BF16)
    perm = jnp.asarray(perm, BF16)
    z, fr, fi = pl.pallas_call(
        functools.partial(_s5_prompt_kernel, nseg=nseg, seg=seg),
        grid=(bsz, seq // tm),
        in_specs=[pl.BlockSpec((1, tm, d), lambda b, i: (b, i, 0)),
                  _row_block(shift, tm), _row_block(scale, tm),
                  _const2((1, d)), _const2((1, d)),
                  _const2(ar.shape), _const2(ai.shape), _const2(wb.shape), _const2(wc.shape),
                  _const2((tm, tm)), _const2((tm, tm))],
        out_specs=[pl.BlockSpec((1, tm, d), lambda b, i: (b, i, 0)),
                   pl.BlockSpec((1, nblk, 1, blk), lambda b, i: (b, 0, 0, 0)),
                   pl.BlockSpec((1, nblk, 1, blk), lambda b, i: (b, 0, 0, 0))],
        out_shape=[jax.ShapeDtypeStruct((bsz, seq, d), BF16),
                   jax.ShapeDtypeStruct((bsz, nblk, 1, blk), F32),
                   jax.ShapeDtypeStruct((bsz, nblk, 1, blk), F32)],
        scratch_shapes=[pltpu.VMEM((tm, d), F32), pltpu.VMEM((tm, blk), F32), pltpu.VMEM((tm, blk), F32),
                        pltpu.VMEM((nblk, 1, blk), F32), pltpu.VMEM((nblk, 1, blk), F32)],
        compiler_params=_cparams(2),
        name="s5_prompt",
    )(x, shift, scale, g.reshape(1, d), d_skip.reshape(1, d), ar, ai, wb, wc, perm, unperm)
    return z, fr, fi


def _s5_sample_kernel(x_ref, sh_ref, sc_ref, g_ref, d_ref, ar_ref, ai_ref, wb_ref, wc_ref,
                      s0r_ref, s0i_ref, z_ref, fr_ref, fi_ref, *, nb, steps):
    nblk = wb_ref.shape[0]
    blk = ar_ref.shape[-1]
    u = _rms_mod(x_ref[...], g_ref[...], sh_ref[...], sc_ref[...])
    for c in range(nblk):
        lo, hi = c * S5_CH, (c + 1) * S5_CH
        uc = u[:, lo:hi]
        bu = _dot(uc.astype(BF16), wb_ref[c])
        ar = jnp.broadcast_to(ar_ref[c], (nb, blk))
        ai = jnp.broadcast_to(ai_ref[c], (nb, blk))
        xr, xi = s0r_ref[:, c * blk:(c + 1) * blk], s0i_ref[:, c * blk:(c + 1) * blk]
        xrs, xis = [], []
        for t in range(steps):
            pr, pi = _cmul(ar, ai, xr, xi)
            xr = pr + bu[t * nb:(t + 1) * nb, :blk]
            xi = pi + bu[t * nb:(t + 1) * nb, blk:]
            xrs.append(xr)
            xis.append(xi)
        fr_ref[:, c * blk:(c + 1) * blk] = xr
        fi_ref[:, c * blk:(c + 1) * blk] = xi
        y = (_dot(jnp.concatenate(xrs, axis=0).astype(BF16), wc_ref[c, :blk, :])
             - _dot(jnp.concatenate(xis, axis=0).astype(BF16), wc_ref[c, blk:, :]))
        y = y + d_ref[:, lo:hi] * uc
        z_ref[:, lo:hi] = _gelu_tanh(y)


def _s5_sample(x_tb, shift_tb, scale_tb, g, d_skip, ar, ai, wb, wc, s0r, s0i, nb, steps):
    rows, d = x_tb.shape
    nblk, _, blk = ar.shape
    return pl.pallas_call(
        functools.partial(_s5_sample_kernel, nb=nb, steps=steps),
        out_shape=[jax.ShapeDtypeStruct((rows, d), F32),
                   jax.ShapeDtypeStruct((nb, nblk * blk), F32),
                   jax.ShapeDtypeStruct((nb, nblk * blk), F32)],
        compiler_params=pltpu.CompilerParams(vmem_limit_bytes=VMEM_LIMIT),
        name="s5_sample",
    )(x_tb, shift_tb, scale_tb, g.reshape(1, d), d_skip.reshape(1, d), ar, ai, wb, wc, s0r, s0i)


def _shift_rows(u, prev, k):
    if k == 0:
        return u
    rolled = pltpu.roll(u, k, 0)
    head = pltpu.roll(prev, k, 0)
    row = lax.broadcasted_iota(I32, (SUBLANE, u.shape[1]), 0)
    first = jnp.where(row < k, head, rolled[:SUBLANE])
    return jnp.concatenate([first, rolled[SUBLANE:]], axis=0)


def _sconv_prompt_kernel(x_ref, sh_ref, sc_ref, g_ref, wi_ref, gate_ref, buf_ref, wc_ref, wo_ref,
                         y_ref, nb_ref, prev_s, *, width):
    i = pl.program_id(1)
    d = x_ref.shape[-1]

    @pl.when(i == 0)
    def _():
        prev_s[...] = jnp.zeros(prev_s.shape, F32)
        prev_s[SUBLANE - (width - 1):, :] = buf_ref[0]

    h = _rms_mod(x_ref[0], g_ref[...], sh_ref[0], sc_ref[0]).astype(BF16)
    gb = _dot(h, wi_ref[:, :d])
    u = _dot(h, wi_ref[:, d:2 * d]) * _dot(h, wi_ref[:, 2 * d:])
    prev = prev_s[...]
    conv = u * wc_ref[width - 1:width, :]
    for k in range(1, width):
        conv = conv + _shift_rows(u, prev, k) * wc_ref[width - 1 - k:width - k, :]
    prev_s[...] = u[u.shape[0] - SUBLANE:, :]
    nb_ref[0] = u[u.shape[0] - (width - 1):, :]
    y_ref[0] = x_ref[0] + gate_ref[0] * _dot((gb * conv).astype(BF16), wo_ref[...])


def _sconv_prompt(x, shift, scale, g, w_in_bf16, gate, buf, w_conv, w_out_bf16, tm):
    bsz, seq, d = x.shape
    width = w_conv.shape[0]
    return pl.pallas_call(
        functools.partial(_sconv_prompt_kernel, width=width),
        grid=(bsz, seq // tm),
        in_specs=[pl.BlockSpec((1, tm, d), lambda b, i: (b, i, 0)),
                  _row_block(shift, tm), _row_block(scale, tm), _const2((1, d)),
                  _resident(w_in_bf16.shape), _row_block(gate, tm),
                  pl.BlockSpec((1, width - 1, d), lambda b, i: (b, 0, 0)),
                  _const2((width, d)), _resident((d, d))],
        out_specs=[pl.BlockSpec((1, tm, d), lambda b, i: (b, i, 0)),
                   pl.BlockSpec((1, width - 1, d), lambda b, i: (b, 0, 0))],
        out_shape=[jax.ShapeDtypeStruct((bsz, seq, d), F32),
                   jax.ShapeDtypeStruct((bsz, width - 1, d), F32)],
        scratch_shapes=[pltpu.VMEM((SUBLANE, d), F32)],
        compiler_params=_cparams(2),
        name="sconv_prompt",
    )(x, shift, scale, g.reshape(1, d), w_in_bf16, gate, buf, w_conv, w_out_bf16)


def _sconv_sample_kernel(p_ref, x_ref, gate_ref, buf_ref, wc_ref, wo_ref, y_ref, nb_ref,
                         *, width, nb, steps):
    d = x_ref.shape[-1]
    p = p_ref[...]
    gb, gc, xh = p[:, :d], p[:, d:2 * d], p[:, 2 * d:]
    u = gc * xh
    ext = jnp.concatenate([buf_ref[...], u], axis=0)
    conv = ext[0:steps * nb] * wc_ref[0:1, :]
    for k in range(1, width):
        conv = conv + ext[k * nb:(k + steps) * nb] * wc_ref[k:k + 1, :]
    nb_ref[...] = ext[steps * nb:]
    y_ref[...] = x_ref[...] + gate_ref[...] * _dot((gb * conv).astype(BF16), wo_ref[...])


def _sconv_sample(p_tb, x_tb, gate_tb, buf_kb, w_conv, w_out_bf16, nb, steps):
    rows, d = x_tb.shape
    width = w_conv.shape[0]
    return pl.pallas_call(
        functools.partial(_sconv_sample_kernel, width=width, nb=nb, steps=steps),
        out_shape=[jax.ShapeDtypeStruct((rows, d), F32),
                   jax.ShapeDtypeStruct(((width - 1) * nb, d), F32)],
        compiler_params=pltpu.CompilerParams(vmem_limit_bytes=VMEM_LIMIT),
        name="sconv_sample",
    )(p_tb, x_tb, gate_tb, buf_kb, w_conv, w_out_bf16)


KEY_NEG_INF = -2139095041
KEY_POS_INF = 2139095040


def _code_to_f32(key):
    key = jnp.clip(key, KEY_NEG_INF, KEY_POS_INF)
    return lax.bitcast_convert_type(key ^ ((key >> 31) & 0x7FFFFFFF), F32)


def _stack_heads(qcat):
    w = 4 * IDX_DIM
    return jnp.concatenate([qcat[:, h * w:(h + 1) * w] for h in range(IDX_HEADS)], axis=0)


def _idx_rhs(ki):
    kh, kl = _split(ki)
    return jnp.concatenate([kh, kh, kl, jnp.zeros(kh.shape, BF16)], axis=1)


def _idx_score(dts, wcols, nrows):
    s = wcols[0] * jnp.maximum(dts[0:nrows], 0.0)
    for h in range(1, IDX_HEADS):
        s = s + wcols[h] * jnp.maximum(dts[h * nrows:(h + 1) * nrows], 0.0)
    return s


def _idx_wcols(kw):
    sc = (IDX_HEADS ** -0.5) * (IDX_DIM ** -0.5)
    return [kw[:, IDX_DIM + h:IDX_DIM + h + 1] * sc for h in range(IDX_HEADS)]


def _topk_select(count, rows, lanes, topk, idx_bits, stash=lambda v: (lambda: v)):
    wide = lambda v: jnp.broadcast_to(v, (rows, lanes))

    def bit_step(it, tu):
        mask = jnp.left_shift(jnp.int32(1), 31 - it)
        cand_u = tu | mask
        cand = stash(wide(_code_to_f32(cand_u ^ INT_MIN)))
        cnt = count(lambda c, sc, idx: jnp.where(sc >= cand(), 1.0, 0.0))
        return jnp.where(cnt >= topk, cand_u, tu)

    t = _code_to_f32(lax.fori_loop(0, 32, bit_step, jnp.zeros((rows, 1), I32)) ^ INT_MIN)
    tw = wide(t)
    n_gt = count(lambda c, sc, idx: jnp.where(sc > tw, 1.0, 0.0))
    n_ge = count(lambda c, sc, idx: jnp.where(sc >= tw, 1.0, 0.0))
    need = topk - n_gt
    tied = jnp.max(jnp.where(t > -jnp.inf, n_ge - topk, 0.0)) > 0.0

    def tie_search(_):
        def idx_step(it, j):
            cand = j | jnp.left_shift(jnp.int32(1), idx_bits - 1 - it)
            cw = wide(cand)
            cnt = count(lambda c, sc, idx: jnp.where((sc == tw) & (idx < cw), 1.0, 0.0))
            return jnp.where(cnt < need, cand, j)
        return lax.fori_loop(0, idx_bits, idx_step, jnp.zeros((rows, 1), I32))

    j = lax.cond(tied, tie_search, lambda _: jnp.full((rows, 1), 2 ** idx_bits, I32), 0)
    return t, j


def _dsa_prompt_kernel(q_ref, qcat_ref, kwq_ref, kcat_ref, kaug_ref, vaug_ref, bias_ref, o_ref,
                       sc_s, madd_s, wbc_s, cand_s, qa_s, lg_s, mx_s, sh_s, acc_s, *, tq, topk, idx_bits):
    i = pl.program_id(1)
    nk = i + 1
    nk2 = (nk + 1) // 2
    r = N_HEADS // N_KV_HEADS

    for h, w in enumerate(_idx_wcols(kwq_ref[0])):
        wbc_s[h] = jnp.broadcast_to(w, (tq, tq))
    qrow = lax.broadcasted_iota(I32, (tq, tq), 0)
    kcol = lax.broadcasted_iota(I32, (tq, tq), 1)
    qpos = i * tq + qrow
    wq = 4 * IDX_DIM

    def score_pair(c2, _):
        kc = kcat_ref[0, c2]
        s = None
        hpd = 4
        for hp in range(IDX_HEADS // hpd):
            lhs = jnp.concatenate([qcat_ref[0, :, (hpd * hp + v) * wq:(hpd * hp + v + 1) * wq] for v in range(hpd)],
                                  axis=0)
            dd = jnp.maximum(_dot(lhs, kc), 0.0)
            for v in range(hpd):
                w = wbc_s[hpd * hp + v]
                d = jnp.concatenate([w, w], axis=1) * dd[v * tq:(v + 1) * tq]
                s = d if s is None else s + d
        for u in range(2):
            c = 2 * c2 + u
            sc_s[c] = jnp.where(c * tq + kcol <= qpos, s[:, u * tq:(u + 1) * tq], -jnp.inf)
        return 0

    lax.fori_loop(0, nk2, score_pair, 0)

    def count(fn):
        def body(c2, acc):
            c = 2 * c2
            acc = acc + fn(c, sc_s[c], c * tq + kcol)
            return acc + fn(c + 1, sc_s[c + 1], (c + 1) * tq + kcol)
        acc = lax.fori_loop(0, nk2, body, jnp.zeros((tq, tq), F32))
        return jnp.sum(acc, axis=1, keepdims=True)

    def stash(v):
        cand_s[...] = v
        return lambda: cand_s[...]

    t, j = _topk_select(count, tq, tq, float(topk), idx_bits, stash)
    tw = jnp.broadcast_to(t, (tq, tq))
    jw = jnp.broadcast_to(j, (tq, tq))

    def mask_pair(c2, _):
        for u in range(2):
            c = 2 * c2 + u
            sc = sc_s[c]
            sel = (sc > tw) | ((sc == tw) & (c * tq + kcol <= jw))
            madd_s[c] = jnp.where(sel & (sc > -jnp.inf), 0.0, NEG_BIG)
        return 0

    lax.fori_loop(0, nk2, mask_pair, 0)

    q = q_ref[0].astype(F32)
    zpad = jnp.zeros((tq, LANE - HEAD_DIM), F32)
    for g in range(N_KV_HEADS):
        qa_s[g] = jnp.concatenate(
            [jnp.concatenate([q[:, (g * r + u) * HEAD_DIM:(g * r + u + 1) * HEAD_DIM], zpad], axis=1)
             for u in range(r)], axis=0).astype(BF16)

    def pair_mask(c2, near, g):
        halves = []
        for u in range(2):
            c = 2 * c2 + u
            m = madd_s[c]
            if near:
                back = nk - 1 - c
                carries = (back == 0) | ((back == 1) & (kcol > qrow))
                halves.append([m + jnp.where(carries, bias_ref[g * r + v], 0.0) for v in range(r)])
            else:
                halves.append([m] * r)
        return jnp.concatenate([jnp.concatenate([halves[0][v], halves[1][v]], axis=1) for v in range(r)], axis=0)

    ngh = lg_s.shape[0]
    for half in range(N_KV_HEADS // ngh):
        groups = [half * ngh + k for k in range(ngh)]
        mx_s[...] = jnp.full(mx_s.shape, NEG_BIG, F32)

        def logits_pair(c2, near):
            for k, g in enumerate(groups):
                lg = _dot(qa_s[g], kaug_ref[0, c2, g * LANE:(g + 1) * LANE, :]) + pair_mask(c2, near, g)
                lg_s[k, c2] = lg
                mx_s[k] = jnp.maximum(mx_s[k], jnp.maximum(lg[:, :tq], lg[:, tq:]))

        def far(c4, _):
            logits_pair(2 * c4, False)
            logits_pair(2 * c4 + 1, False)
            return 0

        nfar = jnp.maximum(nk2 - 2, 0)
        lax.fori_loop(0, nfar // 2, far, 0)

        @pl.when(nfar % 2 == 1)
        def _():
            logits_pair(nfar - 1, False)

        @pl.when(nk2 >= 2)
        def _():
            logits_pair(nk2 - 2, True)

        logits_pair(nk2 - 1, True)

        for k in range(ngh):
            sh_s[k] = jnp.broadcast_to(jnp.max(mx_s[k], axis=1, keepdims=True), (r * tq, LANE))
        acc_s[...] = jnp.zeros(acc_s.shape, F32)

        def pv_pair(c2):
            r0 = pl.multiple_of(c2 * 2 * tq, 2 * tq)
            va = vaug_ref[0, pl.ds(r0, 2 * tq), :]
            for k, g in enumerate(groups):
                s = sh_s[k]
                p = jnp.exp(lg_s[k, c2] - jnp.concatenate([s, s], axis=1))
                acc_s[k] = acc_s[k] + _dot(p.astype(BF16), va[:, g * LANE:(g + 1) * LANE])

        def pv_two(c4, _):
            pv_pair(2 * c4)
            pv_pair(2 * c4 + 1)
            return 0

        lax.fori_loop(0, nk2 // 2, pv_two, 0)

        @pl.when(nk2 % 2 == 1)
        def _():
            pv_pair(nk2 - 1)
        for k, g in enumerate(groups):
            acc = acc_s[k]
            o = acc[:, 0:HEAD_DIM] / acc[:, HEAD_DIM:HEAD_DIM + 1]
            heads = jnp.concatenate([o[u * tq:(u + 1) * tq, :] for u in range(r)], axis=1)
            o_ref[0, :, g * r * HEAD_DIM:(g + 1) * r * HEAD_DIM] = heads.astype(o_ref.dtype)


def _t5_bucket(dist):
    n = jnp.maximum(dist, 0)
    max_exact = N_BUCKETS // 2
    nf = jnp.maximum(n, max_exact).astype(F32)
    large = max_exact + (jnp.log(nf / max_exact) / math.log(MAX_DISTANCE / max_exact)
                         * (N_BUCKETS - max_exact)).astype(I32)
    large = jnp.minimum(large, N_BUCKETS - 1)
    return jnp.where(n < max_exact, n, large)


def _bias_table(rel_bias, n):
    tab = rel_bias[_t5_bucket(jnp.arange(n, dtype=I32))]
    return (tab - rel_bias[N_BUCKETS - 1][None, :]).T


DSA_GROUPS_PER_PASS = 2


def _dsa_prompt(q, qcat, kw, kcat, kaug, vaug, rel_bias, tq):
    bsz, seq, nq = q.shape
    topk = min(TOPK_MAX, seq // 4)
    nkc = seq // tq
    r = N_HEADS // N_KV_HEADS
    assert tq == MAX_DISTANCE and seq % (2 * tq) == 0
    tab = _bias_table(rel_bias, tq)
    dmod = (np.arange(tq)[:, None] - np.arange(tq)[None, :]) % tq
    onehot = (jnp.arange(tq, dtype=I32)[:, None, None] == jnp.asarray(dmod, I32)[None]).astype(F32)
    bias = jnp.einsum('hd,dqk->hqk', tab, onehot, precision=lax.Precision.HIGHEST)
    assert kcat.shape[1:] == (seq // (2 * tq), 4 * IDX_DIM, 2 * tq)
    row = lambda n: pl.BlockSpec((1, tq, n), lambda b, i: (b, i, 0))
    tiles = lambda a: pl.BlockSpec((1,) + a.shape[1:], lambda b, i: (b, 0, 0, 0), pipeline_mode=pl.Buffered(1))
    return pl.pallas_call(
        functools.partial(_dsa_prompt_kernel, tq=tq, topk=topk, idx_bits=int(math.log2(seq))),
        grid=(bsz, nkc),
        in_specs=[row(nq), row(qcat.shape[-1]), row(LANE), tiles(kcat), tiles(kaug),
                  pl.BlockSpec((1, seq, vaug.shape[-1]), lambda b, i: (b, 0, 0), pipeline_mode=pl.Buffered(1)),
                  _resident((N_HEADS, tq, tq))],
        out_specs=row(nq),
        out_shape=jax.ShapeDtypeStruct((bsz, seq, nq), BF16),
        scratch_shapes=[pltpu.VMEM((nkc, tq, tq), F32), pltpu.VMEM((nkc, tq, tq), F32),
                        pltpu.VMEM((IDX_HEADS, tq, tq), F32), pltpu.VMEM((tq, tq), F32),
                        pltpu.VMEM((N_KV_HEADS, r * tq, LANE), BF16),
                        pltpu.VMEM((DSA_GROUPS_PER_PASS, nkc // 2, r * tq, 2 * tq), F32),
                        pltpu.VMEM((DSA_GROUPS_PER_PASS, r * tq, tq), F32),
                        pltpu.VMEM((DSA_GROUPS_PER_PASS, r * tq, LANE), F32),
                        pltpu.VMEM((DSA_GROUPS_PER_PASS, r * tq, LANE), F32)],
        compiler_params=_cparams(2),
        name="dsa_prompt",
    )(q, qcat, kw, kcat, kaug, vaug, bias)


QP = SUBLANE
PAGES_PER_STEP = 16


def _page_specs(shape4, layer, pps):
    def spec(u):
        return pl.BlockSpec((1, 1) + tuple(shape4[2:]),
                            lambda b, g, pt: (layer, pt[b, g * pps + u], 0, 0))
    return [spec(u) for u in range(pps)]


def _pages_t(cache):
    nd = cache.ndim
    t = jnp.transpose(cache, (0, 1) + tuple(range(3, nd)) + (2,))
    return t.reshape(t.shape[0], t.shape[1], -1, t.shape[-1])


def _pad_t(a, page):
    return jnp.concatenate([a, jnp.zeros((page - QP, a.shape[1]), F32)], axis=0).T


def _dsa_sample_score_kernel(pt_ref, qcat_ref, kw_ref, kcn_ref, *rest, pps):
    page_refs, o_ref, on_ref = rest[:pps], rest[pps], rest[pps + 1]
    lhs = _stack_heads(qcat_ref[0]).astype(BF16)
    wcols = _idx_wcols(kw_ref[0])
    kt = jnp.concatenate([page_refs[u][0, 0] for u in range(pps)], axis=1)
    kh, kl = _split(kt)
    rhs = jnp.concatenate([kh, kh, kl, jnp.zeros(kh.shape, BF16)], axis=0)
    o_ref[0] = _idx_score(_dot(lhs, rhs), wcols, QP)

    @pl.when(pl.program_id(1) == pl.num_programs(1) - 1)
    def _():
        on_ref[0] = _idx_score(_dot(lhs, _pad_t(kcn_ref[0], on_ref.shape[-1]).astype(BF16)), wcols, QP)


def _dsa_sample_scores(page_table, qcat, kw, kcn, kidx_t, layer, pps):
    nb, n_pages = page_table.shape
    page = kidx_t.shape[-1]
    bmap = lambda b, g, pt: (b, 0, 0)
    grid_spec = pltpu.PrefetchScalarGridSpec(
        num_scalar_prefetch=1,
        grid=(nb, n_pages // pps),
        in_specs=[pl.BlockSpec((1, QP, qcat.shape[-1]), bmap), pl.BlockSpec((1, QP, LANE), bmap),
                  pl.BlockSpec((1, QP, kcn.shape[-1]), bmap)] + _page_specs(kidx_t.shape, layer, pps),
        out_specs=[pl.BlockSpec((1, QP, pps * page), lambda b, g, pt: (b, 0, g)),
                   pl.BlockSpec((1, QP, page), bmap)])
    return pl.pallas_call(
        functools.partial(_dsa_sample_score_kernel, pps=pps),
        grid_spec=grid_spec,
        out_shape=[jax.ShapeDtypeStruct((nb, QP, n_pages * page), F32),
                   jax.ShapeDtypeStruct((nb, QP, page), F32)],
        compiler_params=_cparams(2),
        name="dsa_sample_scores",
    )(page_table, qcat, kw, kcn, *([kidx_t] * pps))


def _new_token_scores(s_new, steps):
    lane = lax.broadcasted_iota(I32, s_new.shape, 1)
    q = lax.broadcasted_iota(I32, s_new.shape, 0) % QP
    return jnp.where((lane <= q) & (lane < steps), s_new, -jnp.inf)


def _dsa_sample_select_kernel(sc_ref, sn_ref, t_ref, j_ref, *, steps, topk, idx_bits):
    scores = jnp.concatenate([sc_ref[...], _new_token_scores(sn_ref[...], steps)], axis=1)
    idx = lax.broadcasted_iota(I32, scores.shape, 1)
    count = lambda fn: jnp.sum(fn(0, scores, idx), axis=1, keepdims=True)
    t, j = _topk_select(count, scores.shape[0], scores.shape[1], float(topk), idx_bits)
    t_ref[...] = t
    j_ref[...] = j


def _dsa_sample_select(scores, snew, steps, topk):
    nb, _, past = scores.shape
    page = snew.shape[-1]
    rows = nb * QP
    rb = math.gcd(rows, 256)
    t, j = pl.pallas_call(
        functools.partial(_dsa_sample_select_kernel, steps=steps, topk=topk,
                          idx_bits=int(math.ceil(math.log2(past + page)))),
        grid=(rows // rb,),
        in_specs=[pl.BlockSpec((rb, past), lambda i: (i, 0)), pl.BlockSpec((rb, page), lambda i: (i, 0))],
        out_specs=[pl.BlockSpec((rb, 1), lambda i: (i, 0)), pl.BlockSpec((rb, 1), lambda i: (i, 0))],
        out_shape=[jax.ShapeDtypeStruct((rows, 1), F32), jax.ShapeDtypeStruct((rows, 1), I32)],
        compiler_params=_cparams(1),
        name="dsa_sample_select",
    )(scores.reshape(rows, past), snew.reshape(rows, page))
    return t.reshape(nb, QP, 1), j.reshape(nb, QP, 1)


def _dsa_sample_attend_kernel(pt_ref, q_ref, kn_ref, vn_ref, sc_ref, sn_ref, t_ref, j_ref,
                              blast_ref, bnew_ref, *rest, pps, n_pages, steps):
    k_refs, v_refs, o_ref = rest[:pps], rest[pps:2 * pps], rest[2 * pps]
    lhs_s, m_s, l_s, acc_s = rest[2 * pps + 1:]
    g = pl.program_id(1)
    last = pl.num_programs(1) - 1
    page = k_refs[0].shape[-1]
    past = n_pages * page
    n = pps * page
    r = N_HEADS // N_KV_HEADS
    grow = r * QP

    @pl.when(g == 0)
    def _():
        m_s[...] = jnp.full(m_s.shape, NEG_BIG, F32)
        l_s[...] = jnp.zeros(l_s.shape, F32)
        acc_s[...] = jnp.zeros(acc_s.shape, F32)
        q = q_ref[0]
        for gg in range(N_KV_HEADS):
            lhs_s[gg] = jnp.concatenate(
                [q[:, (gg * r + u) * HEAD_DIM:(gg * r + u + 1) * HEAD_DIM] for u in range(r)], axis=0).astype(BF16)

    t, j = t_ref[0], j_ref[0]

    def select(sc, idx):
        sel = (sc > t) | ((sc == t) & (idx <= j))
        return jnp.where(sel & (sc > -jnp.inf), 0.0, NEG_BIG)

    def attend(kt, vt, madd, bias):
        mrows = jnp.concatenate([madd] * r, axis=0)
        for gg in range(N_KV_HEADS):
            rows = slice(gg * grow, (gg + 1) * grow)
            feat = slice(gg * HEAD_DIM, (gg + 1) * HEAD_DIM)
            lg = _dot(lhs_s[gg], kt[feat, :]) + mrows
            if bias is not None:
                lg = lg + bias[rows, :]
            m_old = m_s[rows, :]
            m_new = jnp.maximum(m_old, jnp.max(lg, axis=1, keepdims=True))
            p = jnp.exp(lg - m_new)
            alpha = jnp.exp(m_old - m_new)
            l_s[rows, :] = alpha * l_s[rows, :] + jnp.sum(p, axis=1, keepdims=True)
            acc_s[rows, :] = alpha * acc_s[rows, :] + _dot_nt(p.astype(BF16), vt[feat, :])
            m_s[rows, :] = m_new

    def pages(bias):
        kt = jnp.concatenate([k_refs[u][0, 0] for u in range(pps)], axis=1).astype(BF16)
        vt = jnp.concatenate([v_refs[u][0, 0] for u in range(pps)], axis=1).astype(BF16)
        idx = g * n + lax.broadcasted_iota(I32, (QP, n), 1)
        attend(kt, vt, select(sc_ref[0], idx), bias)

    @pl.when(g != last)
    def _():
        pages(None)

    @pl.when(g == last)
    def _():
        pages(jnp.concatenate([jnp.zeros((N_HEADS * QP, n - page), F32), blast_ref[...]], axis=1))
        idx = past + lax.broadcasted_iota(I32, (QP, page), 1)
        attend(_pad_t(kn_ref[0], page).astype(BF16), _pad_t(vn_ref[0], page).astype(BF16),
               select(_new_token_scores(sn_ref[0], steps), idx), bnew_ref[...])
        o = acc_s[...] / l_s[...]
        for h in range(N_HEADS):
            o_ref[0, :, h * HEAD_DIM:(h + 1) * HEAD_DIM] = o[h * QP:(h + 1) * QP, :]


def _dsa_sample(page_table, q, kn, vn, scores, snew, k_t, v_t, rel_bias, layer, steps, pps):
    nb, n_pages = page_table.shape
    page = k_t.shape[-1]
    past = n_pages * page
    nq, nkv = N_HEADS * HEAD_DIM, N_KV_HEADS * HEAD_DIM
    t, j = _dsa_sample_select(scores, snew, steps, min(TOPK_MAX, (past + steps) // 4))
    n_steps = n_pages // pps
    r = N_HEADS // N_KV_HEADS
    tab = _bias_table(rel_bias, 2 * page)
    qq = np.arange(QP)[:, None]
    off = np.arange(page)[None, :]
    blast = tab[:, page + qq - off].reshape(N_HEADS * QP, page)
    bnew = tab[:, np.maximum(qq - off, 0)].reshape(N_HEADS * QP, page)
    bmap = lambda b, g, pt: (b, 0, 0)
    grid_spec = pltpu.PrefetchScalarGridSpec(
        num_scalar_prefetch=1,
        grid=(nb, n_steps),
        in_specs=[pl.BlockSpec((1, QP, nq), bmap),
                  pl.BlockSpec((1, QP, nkv), bmap),
                  pl.BlockSpec((1, QP, nkv), bmap),
                  pl.BlockSpec((1, QP, pps * page), lambda b, g, pt: (b, 0, g)),
                  pl.BlockSpec((1, QP, page), bmap),
                  pl.BlockSpec((1, QP, 1), bmap), pl.BlockSpec((1, QP, 1), bmap),
                  pl.BlockSpec(blast.shape, lambda b, g, pt: (0, 0)),
                  pl.BlockSpec(bnew.shape, lambda b, g, pt: (0, 0))]
                 + _page_specs(k_t.shape, layer, pps) + _page_specs(v_t.shape, layer, pps),
        out_specs=pl.BlockSpec((1, QP, nq), bmap),
        scratch_shapes=[pltpu.VMEM((N_KV_HEADS, r * QP, HEAD_DIM), BF16),
                        pltpu.VMEM((N_HEADS * QP, 1), F32), pltpu.VMEM((N_HEADS * QP, 1), F32),
                        pltpu.VMEM((N_HEADS * QP, HEAD_DIM), F32)])
    return pl.pallas_call(
        functools.partial(_dsa_sample_attend_kernel, pps=pps, n_pages=n_pages, steps=steps),
        grid_spec=grid_spec,
        out_shape=jax.ShapeDtypeStruct((nb, QP, nq), F32),
        compiler_params=_cparams(2),
        name="dsa_sample_attend",
    )(page_table, q, kn, vn, scores, snew, t, j, blast, bnew, *([k_t] * pps), *([v_t] * pps))


SSD_COLS = 512


def _softplus(x):
    return jnp.maximum(x, 0.0) + jnp.log1p(jnp.exp(-jnp.abs(x)))


def _split3(a):
    p1 = a.astype(BF16)
    r1 = a - p1.astype(F32)
    p2 = r1.astype(BF16)
    return p1, p2, (r1 - p2.astype(F32)).astype(BF16)


def _dot_sel_rhs(a, e):
    p1, p2, p3 = _split3(a)
    return _dot(p1, e) + (_dot(p2, e) + _dot(p3, e))


def _dot_sel_lhs(e, a):
    p1, p2, p3 = _split3(a)
    return _dot(e, p1) + (_dot(e, p2) + _dot(e, p3))


def _ssd_prompt_kernel(z_ref, xbc_ref, dtr_ref, cw_ref, cb_ref, dtb_ref, alog_ref, dexp_ref, ng_ref,
                       e_ref, y_ref, nbuf_ref, hf_ref, prev_s, xc_s, y_s, h_s, *, width, nh, inner):
    i = pl.program_id(1)
    last = pl.num_programs(1) - 1
    qn, cdim = xbc_ref.shape[1], xbc_ref.shape[2]
    ngrp, ns, hd = SSD_GROUPS, SSD_STATE, SSD_HEADDIM
    hpg = nh // ngrp

    @pl.when(i == 0)
    def _():
        prev_s[...] = jnp.zeros(prev_s.shape, F32)
        h_s[...] = jnp.zeros(h_s.shape, F32)

    for cb in range(cdim // SSD_COLS):
        cols = slice(cb * SSD_COLS, (cb + 1) * SSD_COLS)
        xr = xbc_ref[0, :, cols]
        prev = prev_s[:, cols]
        conv = xr * cw_ref[width - 1:width, cols]
        for k in range(1, width):
            conv = conv + _shift_rows(xr, prev, k) * cw_ref[width - 1 - k:width - k, cols]
        xc_s[:, cols] = _silu(conv + cb_ref[:, cols])
        prev_s[:, cols] = xr[qn - SUBLANE:, :]
        nbuf_ref[0, :, cols] = xr[qn - (width - 1):, :]

    hl = lax.broadcasted_iota(I32, (qn, LANE), 1)
    dt = jnp.where(hl < nh, _softplus(dtr_ref[0] + dtb_ref[...]), 0.0)
    dta = dt * (-jnp.exp(alog_ref[...]))
    qrow = lax.broadcasted_iota(I32, (qn, qn), 0)
    kcol = lax.broadcasted_iota(I32, (qn, qn), 1)
    causal = kcol <= qrow
    cum = _dot_sel_lhs(jnp.where(causal, 1.0, 0.0).astype(BF16), dta)
    cum_t = cum.T
    cum_last = cum[qn - 1:qn, :]
    ecum = jnp.exp(cum)
    dtw = dt * jnp.exp(cum_last - cum)
    e_last = jnp.exp(cum_last)
    lane_p = lax.broadcasted_iota(I32, (qn, LANE), 1)
    row_p = lax.broadcasted_iota(I32, (LANE, ns), 0)

    for g in range(ngrp):
        gl = slice(g * SSD_COLS, (g + 1) * SSD_COLS)
        e_g = e_ref[:, gl]
        xs_g = xc_s[:, gl]
        dtx_g = xs_g * _dot_sel_rhs(dt, e_g)
        dtxw_g = xs_g * _dot_sel_rhs(dtw, e_g)
        ecx_g = _dot_sel_rhs(ecum, e_g)
        bm = xc_s[:, inner + g * ns:inner + (g + 1) * ns].astype(BF16)
        cm = xc_s[:, inner + (ngrp + g) * ns:inner + (ngrp + g + 1) * ns].astype(BF16)
        cbm = _dot_nt(cm, bm)
        for pi in range(hpg // 2):
            ha = g * hpg + 2 * pi
            pls = slice(pi * LANE, (pi + 1) * LANE)
            lanes = slice(ha * hd, ha * hd + LANE)
            dtx_p = dtx_g[:, pls].astype(BF16)
            yds = []
            for h in (ha, ha + 1):
                seg = cum[:, h:h + 1] - cum_t[h:h + 1, :]
                dec = jnp.exp(jnp.where(causal, seg, -jnp.inf))
                yds.append(_dot((cbm * dec).astype(BF16), dtx_p))
            yd = jnp.where(lane_p < hd, yds[0], yds[1])
            hp = h_s[ha // 2]
            yo = _dot_nt(cm, hp.astype(BF16)) * ecx_g[:, pls]
            y_s[:, lanes] = yd + yo + dexp_ref[:, lanes] * xs_g[:, pls]
            s_new = _dot(dtxw_g[:, pls].T.astype(BF16), bm)
            cd = jnp.where(row_p < hd, e_last[:, ha:ha + 1], e_last[:, ha + 1:ha + 2])
            h_s[ha // 2] = hp * cd + s_new

    for g in range(ngrp):
        gl = slice(g * SSD_COLS, (g + 1) * SSD_COLS)
        yg = y_s[:, gl] * _silu(z_ref[0, :, gl])
        ms = jnp.mean(yg * yg, axis=-1, keepdims=True)
        y_ref[0, :, gl] = ((yg * lax.rsqrt(ms + EPS)) * ng_ref[:, gl]).astype(y_ref.dtype)

    @pl.when(i == last)
    def _():
        hf_ref[0] = h_s[...]


def _head_expand(nh, hd):
    e = np.zeros((LANE, nh * hd), np.float32)
    for h in range(nh):
        e[h, h * hd:(h + 1) * hd] = 1.0
    return jnp.asarray(e, BF16)


def _pad_lanes(v, n=LANE):
    v = v.reshape(1, -1)
    return jnp.pad(v, ((0, 0), (0, n - v.shape[1])))


def _ssd_prompt(z, xbc, dtr, conv_w, conv_b, dt_bias, a_log, d_skip, norm_g):
    bsz, seq, inner = z.shape
    cdim = xbc.shape[-1]
    nh = dt_bias.shape[0]
    width = conv_w.shape[0]
    qn = math.gcd(seq, SSD_CHUNK)
    assert qn == SSD_CHUNK and inner == SSD_GROUPS * SSD_COLS and cdim % SSD_COLS == 0
    dexp = jnp.repeat(d_skip, SSD_HEADDIM).reshape(1, inner)
    blk = lambda n: pl.BlockSpec((1, qn, n), lambda b, i: (b, i, 0))
    y, nbuf, hf = pl.pallas_call(
        functools.partial(_ssd_prompt_kernel, width=width, nh=nh, inner=inner),
        grid=(bsz, seq // qn),
        in_specs=[blk(inner), blk(cdim), blk(LANE),
                  _const2((width, cdim)), _const2((1, cdim)), _const2((1, LANE)), _const2((1, LANE)),
                  _const2((1, inner)), _const2((1, inner)), _const2((LANE, inner))],
        out_specs=[blk(inner),
                   pl.BlockSpec((1, width - 1, cdim), lambda b, i: (b, 0, 0)),
                   pl.BlockSpec((1, nh // 2, LANE, SSD_STATE), lambda b, i: (b, 0, 0, 0))],
        out_shape=[jax.ShapeDtypeStruct((bsz, seq, inner), BF16),
                   jax.ShapeDtypeStruct((bsz, width - 1, cdim), F32),
                   jax.ShapeDtypeStruct((bsz, nh // 2, LANE, SSD_STATE), F32)],
        scratch_shapes=[pltpu.VMEM((SUBLANE, cdim), F32), pltpu.VMEM((qn, cdim), F32),
                        pltpu.VMEM((qn, inner), F32), pltpu.VMEM((nh // 2, LANE, SSD_STATE), F32)],
        compiler_params=_cparams(2),
        name="ssd_prompt",
    )(z, xbc, dtr, conv_w, conv_b.reshape(1, cdim), _pad_lanes(dt_bias), _pad_lanes(a_log),
      dexp, norm_g.reshape(1, inner), _head_expand(nh, SSD_HEADDIM))
    return y, nbuf, hf.reshape(bsz, nh, SSD_HEADDIM, SSD_STATE)


def _ssd_prep_sample_kernel(xbc_ref, buf_ref, dtr_ref, cw_ref, cb_ref, dtb_ref, xc_ref, dt_ref, nbuf_ref,
                            *, width, nb, steps, nh):
    cdim = xbc_ref.shape[1]
    for cb in range(cdim // SSD_COLS):
        cols = slice(cb * SSD_COLS, (cb + 1) * SSD_COLS)
        ext = jnp.concatenate([buf_ref[:, cols], xbc_ref[:, cols]], axis=0)
        conv = ext[0:steps * nb] * cw_ref[0:1, cols]
        for k in range(1, width):
            conv = conv + ext[k * nb:(k + steps) * nb] * cw_ref[k:k + 1, cols]
        xc_ref[:, cols] = _silu(conv + cb_ref[:, cols])
        nbuf_ref[:, cols] = ext[steps * nb:]
    hl = lax.broadcasted_iota(I32, dtr_ref.shape, 1)
    dt_ref[...] = jnp.where(hl < nh, _softplus(dtr_ref[...] + dtb_ref[...]), 0.0)


def _ssd_prep_sample(xbc_tb, buf_kb, dtr_tb, conv_w, conv_b, dt_bias, nb, steps):
    rows, cdim = xbc_tb.shape
    width = conv_w.shape[0]
    return pl.pallas_call(
        functools.partial(_ssd_prep_sample_kernel, width=width, nb=nb, steps=steps, nh=dt_bias.shape[0]),
        out_shape=[jax.ShapeDtypeStruct((rows, cdim), F32),
                   jax.ShapeDtypeStruct((rows, LANE), F32),
                   jax.ShapeDtypeStruct(((width - 1) * nb, cdim), F32)],
        compiler_params=pltpu.CompilerParams(vmem_limit_bytes=VMEM_LIMIT),
        name="ssd_prep_sample",
    )(xbc_tb, buf_kb, dtr_tb, conv_w, conv_b.reshape(1, cdim), _pad_lanes(dt_bias))


def _ssd_scan_sample_kernel(xc_ref, dt_ref, z_ref, h0_ref, alog_ref, dexp_ref, ng_ref, e_ref,
                            y_ref, hf_ref, *, steps, inner):
    ngrp, ns = SSD_GROUPS, SSD_STATE
    xc, dt = xc_ref[0], dt_ref[0]
    row = lax.broadcasted_iota(I32, (QP, LANE), 0)
    cum = dt * (-jnp.exp(alog_ref[...]))
    for sft in (1, 2, 4):
        cum = cum + jnp.where(row >= sft, pltpu.roll(cum, sft, 0), 0.0)
    cum_last = cum[QP - 1:QP, :]
    parts = [dt, jnp.exp(cum), dt * jnp.exp(cum_last - cum), jnp.broadcast_to(jnp.exp(cum_last), (QP, LANE))]
    for s in range(steps):
        parts.append(jnp.where(row >= s, jnp.exp(cum - cum[s:s + 1, :]), 0.0))
    stack = jnp.concatenate(parts, axis=0)

    for g in range(ngrp):
        gl = slice(g * SSD_COLS, (g + 1) * SSD_COLS)
        ex = _dot_sel_rhs(stack, e_ref[:, gl])
        dt_x, ecum_x, dtw_x, el_x = (ex[k * QP:(k + 1) * QP] for k in range(4))
        xs_g = xc[:, gl]
        dtx = xs_g * dt_x
        bm = xc[:, inner + g * ns:inner + (g + 1) * ns]
        cm = xc[:, inner + (ngrp + g) * ns:inner + (ngrp + g + 1) * ns].astype(BF16)
        bmp = jnp.concatenate([bm, jnp.zeros((LANE - QP, ns), F32)], axis=0).astype(BF16)
        cbm = _dot_nt(cm, bmp)
        yd = jnp.zeros((QP, SSD_COLS), F32)
        for s in range(steps):
            yd = yd + (ex[(4 + s) * QP:(5 + s) * QP] * cbm[:, s:s + 1]) * dtx[s:s + 1, :]
        h0g = h0_ref[0, gl, :]
        yo = _dot_nt(cm, h0g.astype(BF16)) * ecum_x
        y = (yd + yo + dexp_ref[:, gl] * xs_g) * _silu(z_ref[0, :, gl])
        ms = jnp.mean(y * y, axis=-1, keepdims=True)
        y_ref[0, :, gl] = (y * lax.rsqrt(ms + EPS)) * ng_ref[:, gl]
        tm = jnp.concatenate([xs_g * dtw_x, el_x[0:1], jnp.zeros((LANE - QP - 1, SSD_COLS), F32)], axis=0)
        tt = tm.T
        hf_ref[0, gl, :] = h0g * tt[:, QP:QP + 1] + _dot(tt.astype(BF16), bmp)


def _ssd_scan_sample(xc, dt, z, h0, a_log, d_skip, norm_g, steps):
    nb, _, cdim = xc.shape
    inner = z.shape[-1]
    nh = a_log.shape[0]
    rows_h = nh * SSD_HEADDIM
    dexp = jnp.repeat(d_skip, SSD_HEADDIM).reshape(1, inner)
    one = lambda shape: pl.BlockSpec(shape, lambda b: (0,) * len(shape))
    return pl.pallas_call(
        functools.partial(_ssd_scan_sample_kernel, steps=steps, inner=inner),
        grid=(nb,),
        in_specs=[pl.BlockSpec((1, QP, cdim), lambda b: (b, 0, 0)),
                  pl.BlockSpec((1, QP, LANE), lambda b: (b, 0, 0)),
                  pl.BlockSpec((1, QP, inner), lambda b: (b, 0, 0)),
                  pl.BlockSpec((1, rows_h, SSD_STATE), lambda b: (b, 0, 0)),
                  one((1, LANE)), one((1, inner)), one((1, inner)), one((LANE, inner))],
        out_specs=[pl.BlockSpec((1, QP, inner), lambda b: (b, 0, 0)),
                   pl.BlockSpec((1, rows_h, SSD_STATE), lambda b: (b, 0, 0))],
        out_shape=[jax.ShapeDtypeStruct((nb, QP, inner), F32),
                   jax.ShapeDtypeStruct((nb, rows_h, SSD_STATE), F32)],
        compiler_params=_cparams(1),
        name="ssd_scan_sample",
    )(xc, dt, z, h0, _pad_lanes(a_log), dexp, norm_g.reshape(1, inner), _head_expand(nh, SSD_HEADDIM))


TM = 256


def _to_steps(a, nb, steps):
    n = a.shape[-1]
    return a.reshape(nb, steps, n).transpose(1, 0, 2).reshape(steps * nb, n)


def _to_batch(a, nb, steps):
    n = a.shape[-1]
    return a.reshape(steps, nb, n).transpose(1, 0, 2).reshape(1, nb * steps, n)


def _pad_q(a, nb, steps):
    n = a.shape[-1]
    return jnp.pad(a.reshape(nb, steps, n), ((0, 0), (0, QP - steps), (0, 0)))


def kernel(x_prompt, x_sample, cache_k, cache_v, cache_kidx, state_s5_re, state_s5_im, state_sconv, state_ssd, state_ssd_conv, page_table, c_prompt, c_sample, rel_bias, ada_w, ada_b, norm_mix, norm_mlp, norm_final, attn_w_in, attn_w_out, s5_lam_re, s5_lam_im, s5_log_dt, s5_b_re, s5_b_im, s5_c_re, s5_c_im, s5_d, s5_w_glu, sc_w_in, sc_w_conv, sc_w_out, ssd_w_in, ssd_conv_w, ssd_conv_b, ssd_dt_bias, ssd_a_log, ssd_d, ssd_norm, ssd_w_out, mlp_w1, mlp_w2):
    bp, seq, d = x_prompt.shape
    nb, steps, _ = x_sample.shape
    depth = ada_w.shape[0]
    n_mixers = 4
    rs = nb * steps
    tm = min(TM, seq)

    rows = bp + nb
    c_all = jnp.pad(jnp.concatenate([c_prompt, c_sample], axis=0), ((0, (-rows) % SUBLANE), (0, 0)))
    ada = _ada(c_all, ada_w, ada_b)

    xp = x_prompt
    xs = x_sample.reshape(1, rs, d)
    outs = {name: [] for name in ("kp", "vp", "kip", "ks", "vs", "kis", "s5pr", "s5pi", "s5sr", "s5si",
                                  "scp", "scs", "ssdp", "ssdcp", "ssds", "ssdcs")}
    nq, nkv = N_HEADS * HEAD_DIM, N_KV_HEADS * HEAD_DIM
    w1_all, w2_all = mlp_w1.astype(BF16), mlp_w2.astype(BF16)
    for i in range(depth):
        m, j = i % n_mixers, i // n_mixers
        mp = [ada[i, :bp, k * d:(k + 1) * d].reshape(bp, 1, d) for k in range(6)]
        ms = [jnp.repeat(ada[i, bp:bp + nb, k * d:(k + 1) * d], steps, axis=0).reshape(1, rs, d)
              for k in range(6)]
        g_mix = norm_mix[i]
        mix_p = mix_s = None
        if m == 0:
            w_in = attn_w_in[j]
            w_qkv = w_in[:, :nq + 2 * nkv].astype(BF16)
            n_idx = IDX_HEADS * IDX_DIM + LANE
            w_idx = jnp.pad(w_in[:, nq + 2 * nkv:], ((0, 0), (0, n_idx - (w_in.shape[1] - nq - 2 * nkv))))
            w_out = attn_w_out[j].astype(BF16)
            page = cache_k.shape[2]
            q, k, v, kw, qcat, kcat, kaug, vaug, ki = _attn_proj(xp, mp[0], mp[1], g_mix, w_qkv, w_idx,
                                                                 2 * MAX_DISTANCE, True, page)
            o = _dsa_prompt(q, qcat, kw, kcat, kaug, vaug, rel_bias, MAX_DISTANCE)
            mix_p = ("proj", o, mp[2], w_out)
            to_pages = lambda t: jnp.transpose(t.reshape(bp, seq // page, N_KV_HEADS, HEAD_DIM, page), (0, 1, 4, 2, 3))
            outs["kp"].append(to_pages(k))
            outs["vp"].append(to_pages(v))
            outs["kip"].append(jnp.transpose(ki, (0, 1, 3, 2)))
            q, k, v, kw, qcat, kcat, _, _ = _attn_proj(xs, ms[0], ms[1], g_mix, w_qkv, w_idx, rs, False)
            pps = math.gcd(PAGES_PER_STEP, page_table.shape[1])
            qc_p, kw_p = _pad_q(qcat.astype(F32), nb, steps), _pad_q(kw, nb, steps)
            scores, snew = _dsa_sample_scores(page_table, qc_p, kw_p, _pad_q(kcat.astype(F32), nb, steps),
                                              _pages_t(cache_kidx), j,
                                              math.gcd(2 * PAGES_PER_STEP, page_table.shape[1]))
            o = _dsa_sample(page_table, _pad_q(q, nb, steps), _pad_q(k, nb, steps), _pad_q(v, nb, steps),
                            scores, snew, _pages_t(cache_k), _pages_t(cache_v), rel_bias, j, steps, pps)
            mix_s = ("proj", o[:, :steps].reshape(1, rs, nq), ms[2], w_out)
            outs["ks"].append(k.reshape(nb, steps, N_KV_HEADS, HEAD_DIM))
            outs["vs"].append(v.reshape(nb, steps, N_KV_HEADS, HEAD_DIM))
            outs["kis"].append(kw[..., :IDX_DIM].reshape(nb, steps, IDX_DIM))
        elif m == 1:
            ar, ai, wb, wc = _s5_weights(s5_lam_re[j], s5_lam_im[j], s5_log_dt[j], s5_b_re[j], s5_b_im[j],
                                         s5_c_re[j], s5_c_im[j])
            w_glu = s5_w_glu[j].astype(BF16)
            grp, nst = s5_lam_re.shape[1:]
            z, fr, fi = _s5_prompt(xp, mp[0], mp[1], g_mix, s5_d[j], ar, ai, wb, wc)
            mix_p = ("glu", z, mp[2], w_glu)
            outs["s5pr"].append(fr.reshape(bp, grp, nst))
            outs["s5pi"].append(fi.reshape(bp, grp, nst))
            z, fr, fi = _s5_sample(_to_steps(xs, nb, steps), _to_steps(ms[0], nb, steps),
                                   _to_steps(ms[1], nb, steps), g_mix, s5_d[j], ar, ai, wb, wc,
                                   state_s5_re[j].reshape(nb, grp * nst), state_s5_im[j].reshape(nb, grp * nst),
                                   nb, steps)
            mix_s = ("glu", _to_batch(z, nb, steps), ms[2], w_glu)
            outs["s5sr"].append(fr.reshape(nb, grp, nst))
            outs["s5si"].append(fi.reshape(nb, grp, nst))
        elif m == 2:
            w_in = sc_w_in[j].astype(BF16)
            w_out = sc_w_out[j].astype(BF16)
            width = sc_w_conv.shape[1]
            xp, nbuf = _sconv_prompt(xp, mp[0], mp[1], g_mix, w_in, mp[2], jnp.zeros((bp, width - 1, d), F32),
                                     sc_w_conv[j], w_out, tm)
            outs["scp"].append(nbuf)
            (p,) = _proj(xs, ms[0], ms[1], g_mix, w_in, (3 * d,), rs)
            buf = state_sconv[j].transpose(1, 0, 2).reshape((width - 1) * nb, d)
            y, nbuf = _sconv_sample(_to_steps(p, nb, steps), _to_steps(xs, nb, steps), _to_steps(ms[2], nb, steps),
                                    buf, sc_w_conv[j], w_out, nb, steps)
            xs = _to_batch(y, nb, steps)
            outs["scs"].append(nbuf.reshape(width - 1, nb, d).transpose(1, 0, 2))
        else:
            inner = ssd_norm.shape[1]
            cdim = ssd_conv_w.shape[2]
            nh = ssd_dt_bias.shape[1]
            width = ssd_conv_w.shape[1]
            w_in = jnp.pad(ssd_w_in[j], ((0, 0), (0, LANE - nh))).astype(BF16)
            w_out = ssd_w_out[j].astype(BF16)
            z, xbc, dtr = _proj(xp, mp[0], mp[1], g_mix, w_in, (inner, cdim, LANE), tm)
            y, nbuf, hf = _ssd_prompt(z, xbc, dtr, ssd_conv_w[j], ssd_conv_b[j], ssd_dt_bias[j], ssd_a_log[j],
                                      ssd_d[j], ssd_norm[j])
            mix_p = ("proj", y, mp[2], w_out)
            outs["ssdp"].append(hf)
            outs["ssdcp"].append(nbuf)
            z, xbc, dtr = _proj(xs, ms[0], ms[1], g_mix, w_in, (inner, cdim, LANE), rs)
            buf = state_ssd_conv[j].transpose(1, 0, 2).reshape((width - 1) * nb, cdim)
            xc, dt, nbuf = _ssd_prep_sample(_to_steps(xbc, nb, steps), buf, _to_steps(dtr, nb, steps),
                                            ssd_conv_w[j], ssd_conv_b[j], ssd_dt_bias[j], nb, steps)
            y, hf = _ssd_scan_sample(_pad_q(_to_batch(xc, nb, steps), nb, steps),
                                     _pad_q(_to_batch(dt, nb, steps), nb, steps), _pad_q(z, nb, steps),
                                     state_ssd[j].reshape(nb, nh * SSD_HEADDIM, SSD_STATE),
                                     ssd_a_log[j], ssd_d[j], ssd_norm[j], steps)
            mix_s = ("proj", y[:, :steps].reshape(1, rs, inner), ms[2], w_out)
            outs["ssds"].append(hf.reshape(nb, nh, SSD_HEADDIM, SSD_STATE))
            outs["ssdcs"].append(nbuf.reshape(width - 1, nb, cdim).transpose(1, 0, 2))
        fin = i == depth - 1
        xp = _mlp(xp, mp[3], mp[4], mp[5], norm_mlp[i], w1_all, w2_all, i, norm_final, fin, tm, mix_p)
        xs = _mlp(xs, ms[3], ms[4], ms[5], norm_mlp[i], w1_all, w2_all, i, norm_final, fin, rs, mix_s)
    st = jnp.stack
    return (xp, xs.reshape(nb, steps, d), st(outs["kp"]), st(outs["vp"]), st(outs["kip"]),
            st(outs["ks"]), st(outs["vs"]), st(outs["kis"]), st(outs["s5pr"]), st(outs["s5pi"]),
            st(outs["s5sr"]), st(outs["s5si"]), st(outs["scp"]), st(outs["scs"]),
            st(outs["ssdp"]), st(outs["ssdcp"]), st(outs["ssds"]), st(outs["ssdcs"]))
```

```python
import functools
import math

import jax
import jax.numpy as jnp
import numpy as np
from jax import lax
from jax.experimental import pallas as pl
from jax.experimental.pallas import tpu as pltpu

F32 = jnp.float32
BF16 = jnp.bfloat16
I32 = jnp.int32

EPS = 1e-6
N_HEADS = 16
HEAD_DIM = 64
N_KV_HEADS = 4
IDX_HEADS = 8
IDX_DIM = 64
TOPK_MAX = 256
N_BUCKETS = 32
MAX_DISTANCE = 128
S5_GROUP = 16
S5_STATE = 64
SSD_HEADDIM = 64
SSD_GROUPS = 4
SSD_STATE = 128
SSD_CHUNK = 128

LANE = 128
SUBLANE = 8
VMEM_LIMIT = 56 * 1024 * 1024
NEG_BIG = -1e30
INT_MIN = -2147483648


def _cparams(n_axes):
    return pltpu.CompilerParams(dimension_semantics=("arbitrary",) * n_axes,
                                vmem_limit_bytes=VMEM_LIMIT)


def _dot(a, b):
    return jnp.dot(a, b, preferred_element_type=F32)


def _dot_nt(a, b):
    return lax.dot_general(a, b, (((1,), (1,)), ((), ())), preferred_element_type=F32)


def _split(a):
    hi = a.astype(BF16)
    lo = (a - hi.astype(F32)).astype(BF16)
    return hi, lo


def _dot3(a, b):
    ah, al = _split(a)
    bh, bl = _split(b)
    return _dot(ah, bh) + (_dot(al, bh) + _dot(ah, bl))


def _rms_mod(x, g, shift, scale):
    y = x * lax.rsqrt(jnp.mean(x * x, axis=-1, keepdims=True) + EPS)
    return (y * g) * (1.0 + scale) + shift


def _sigmoid(x):
    return 1.0 / (1.0 + jnp.exp(-x))


def _silu(x):
    return x * _sigmoid(x)


def _row_block(arr, tm):
    n = arr.shape[-1]
    if arr.shape[1] == 1:
        return pl.BlockSpec((1, 1, n), lambda b, i: (b, 0, 0))
    return pl.BlockSpec((1, tm, n), lambda b, i: (b, i, 0))


def _const2(shape):
    return pl.BlockSpec(shape, lambda b, i: (0,) * len(shape))


def _ada_kernel(c_ref, w_ref, b_ref, o_ref):
    o_ref[0] = _dot3(_silu(c_ref[...]), w_ref[0]) + b_ref[0]


def _ada(c_all, ada_w, ada_b):
    depth, d, n = ada_w.shape
    rows = c_all.shape[0]
    tn = 1536
    return pl.pallas_call(
        _ada_kernel,
        grid=(depth, n // tn),
        in_specs=[pl.BlockSpec((rows, d), lambda l, j: (0, 0)),
                  pl.BlockSpec((1, d, tn), lambda l, j: (l, 0, j)),
                  pl.BlockSpec((1, 1, tn), lambda l, j: (l, 0, j))],
        out_specs=pl.BlockSpec((1, rows, tn), lambda l, j: (l, 0, j)),
        out_shape=jax.ShapeDtypeStruct((depth, rows, n), F32),
        compiler_params=_cparams(2),
        name="ada",
    )(c_all, ada_w, ada_b.reshape(depth, 1, n))


def _proj_kernel(x_ref, sh_ref, sc_ref, g_ref, w_ref, *o_refs, splits):
    h = _rms_mod(x_ref[0], g_ref[...], sh_ref[0], sc_ref[0]).astype(BF16)
    off = 0
    for o_ref, n in zip(o_refs, splits):
        o_ref[0] = _dot(h, w_ref[:, off:off + n])
        off += n


def _proj(x, shift, scale, g, w_bf16, splits, tm):
    bsz, seq, d = x.shape
    n = w_bf16.shape[1]
    assert sum(splits) == n and seq % tm == 0
    return pl.pallas_call(
        functools.partial(_proj_kernel, splits=tuple(splits)),
        grid=(bsz, seq // tm),
        in_specs=[pl.BlockSpec((1, tm, d), lambda b, i: (b, i, 0)),
                  _row_block(shift, tm), _row_block(scale, tm),
                  _const2((1, d)), _const2((d, n))],
        out_specs=[pl.BlockSpec((1, tm, s), lambda b, i: (b, i, 0)) for s in splits],
        out_shape=[jax.ShapeDtypeStruct((bsz, seq, s), F32) for s in splits],
        compiler_params=_cparams(2),
        name="proj",
    )(x, shift, scale, g.reshape(1, d), w_bf16)


def _aug_heads(t, col):
    rows = t.shape[0]
    lane = lax.broadcasted_iota(I32, (rows, LANE - HEAD_DIM), 1)
    extra = jnp.where(lane == 0, col, 0.0).astype(F32)
    parts = []
    for g in range(N_KV_HEADS):
        parts += [t[:, g * HEAD_DIM:(g + 1) * HEAD_DIM], extra]
    return jnp.concatenate(parts, axis=1).astype(BF16)


def _attn_proj_kernel(x_ref, sh_ref, sc_ref, g_ref, w_ref, wi_ref,
                      q_ref, k_ref, v_ref, kw_ref, qcat_ref, kcat_ref, kaug_ref, vaug_ref, *ki_refs,
                      nq, nkv, nqi, key_major):
    (ki_ref,) = ki_refs if key_major else (None,)
    h = _rms_mod(x_ref[0], g_ref[...], sh_ref[0], sc_ref[0])
    hb = h.astype(BF16)
    q_ref[0] = (_dot(hb, w_ref[:, 0:nq]) * (HEAD_DIM ** -0.5)).astype(q_ref.dtype)
    k = _dot(hb, w_ref[:, nq:nq + nkv])
    v = _dot(hb, w_ref[:, nq + nkv:nq + 2 * nkv])
    r = _dot3(h, wi_ref[...])
    kw = r[:, nqi:nqi + LANE]
    kw_ref[0] = kw
    if key_major:
        page = k_ref.shape[-1]
        kt, vt, kwt = k.T, v.T, kw.T
        for p in range(k_ref.shape[1]):
            k_ref[0, p] = kt[:, p * page:(p + 1) * page]
            v_ref[0, p] = vt[:, p * page:(p + 1) * page]
            ki_ref[0, p] = kwt[0:IDX_DIM, p * page:(p + 1) * page]
    else:
        k_ref[0] = k
        v_ref[0] = v
    zero = jnp.zeros((h.shape[0], IDX_DIM), BF16)
    parts = []
    for hh in range(IDX_HEADS):
        ah, al = _split(r[:, hh * IDX_DIM:(hh + 1) * IDX_DIM])
        parts += [ah, al, ah, zero]
    qcat_ref[0] = jnp.concatenate(parts, axis=1)
    kcat = _idx_rhs(kw[:, 0:IDX_DIM])
    kaug = _aug_heads(k, 0.0)
    vaug_ref[0] = _aug_heads(v, 1.0)
    if key_major:
        kcat_ref[0, 0] = kcat.astype(F32).T.astype(BF16)
        kaug_ref[0, 0] = kaug.astype(F32).T.astype(BF16)
    else:
        kcat_ref[0] = kcat
        kaug_ref[0] = kaug


def _attn_proj(x, shift, scale, g, w_qkv_bf16, w_idx, tm, key_major, page=None):
    bsz, seq, d = x.shape
    nq = N_HEADS * HEAD_DIM
    nkv = N_KV_HEADS * HEAD_DIM
    nqi = IDX_HEADS * IDX_DIM
    rows = lambda n, dt: (pl.BlockSpec((1, tm, n), lambda b, i: (b, i, 0)), jax.ShapeDtypeStruct((bsz, seq, n), dt))
    cols = lambda n, dt: (pl.BlockSpec((1, 1, n, tm), lambda b, i: (b, i, 0, 0)),
                          jax.ShapeDtypeStruct((bsz, seq // tm, n, tm), dt))
    keys = cols if key_major else rows
    if key_major:
        ppt = tm // page
        paged = lambda n: (pl.BlockSpec((1, ppt, n, page), lambda b, i: (b, i, 0, 0)),
                           jax.ShapeDtypeStruct((bsz, seq // page, n, page), F32))
        kv = [paged(nkv), paged(nkv)]
    else:
        kv = [rows(nkv, F32), rows(nkv, F32)]
    outs = [rows(nq, BF16 if key_major else F32)] + kv + [rows(LANE, F32), rows(4 * nqi, BF16),
                                   keys(4 * IDX_DIM, BF16), keys(N_KV_HEADS * LANE, BF16),
                                   rows(N_KV_HEADS * LANE, BF16)]
    if key_major:
        outs.append(paged(IDX_DIM))
    return pl.pallas_call(
        functools.partial(_attn_proj_kernel, nq=nq, nkv=nkv, nqi=nqi, key_major=key_major),
        grid=(bsz, seq // tm),
        in_specs=[pl.BlockSpec((1, tm, d), lambda b, i: (b, i, 0)),
                  _row_block(shift, tm), _row_block(scale, tm),
                  _const2((1, d)), _const2(w_qkv_bf16.shape), _const2(w_idx.shape)],
        out_specs=[o[0] for o in outs],
        out_shape=[o[1] for o in outs],
        compiler_params=_cparams(2),
        name="attn_proj",
    )(x, shift, scale, g.reshape(1, d), w_qkv_bf16, w_idx)


def _mlp_kernel(*refs, ff_chunk, final_norm, mixer_out):
    if mixer_out is None:
        x_ref, sh_ref, sc_ref, gate_ref, g_ref, w1_ref, w2_ref, gf_ref, y_ref = refs
        x = x_ref[0]
    else:
        o_ref, gmix_ref, wo_ref, x_ref, sh_ref, sc_ref, gate_ref, g_ref, w1_ref, w2_ref, gf_ref, y_ref = refs
        d = x_ref.shape[-1]
        ob = o_ref[0].astype(BF16)
        if mixer_out == "glu":
            t = _dot(ob, wo_ref[:, :d]) * _sigmoid(_dot(ob, wo_ref[:, d:]))
        else:
            t = _dot(ob, wo_ref[...])
        x = x_ref[0] + gmix_ref[0] * t
    h = _rms_mod(x, g_ref[...], sh_ref[0], sc_ref[0]).astype(BF16)
    dff = w1_ref.shape[-1]
    acc = jnp.zeros(x.shape, F32)
    for c in range(dff // ff_chunk):
        a = jnp.maximum(_dot(h, w1_ref[0, :, c * ff_chunk:(c + 1) * ff_chunk]), 0.0)
        acc = acc + _dot((a * a).astype(BF16), w2_ref[0, c * ff_chunk:(c + 1) * ff_chunk, :])
    y = x + gate_ref[0] * acc
    if final_norm:
        y = (y * lax.rsqrt(jnp.mean(y * y, axis=-1, keepdims=True) + EPS)) * gf_ref[...]
    y_ref[0] = y


def _resident(shape):
    return pl.BlockSpec(shape, lambda b, i: (0,) * len(shape), pipeline_mode=pl.Buffered(1))


def _mlp(x, shift, scale, gate, g, w1_bf16, w2_bf16, layer, g_final, final_norm, tm, mixer=None):
    bsz, seq, d = x.shape
    dff = w1_bf16.shape[-1]
    layer_w = lambda r, c: pl.BlockSpec((1, r, c), lambda b, i: (layer, 0, 0), pipeline_mode=pl.Buffered(1))
    rows = lambda n: pl.BlockSpec((1, tm, n), lambda b, i: (b, i, 0))
    args, specs, kind = [], [], None
    if mixer is not None:
        kind, o, gmix, wo = mixer
        args += [o, gmix, wo]
        specs += [rows(o.shape[-1]), _row_block(gmix, tm), _resident(wo.shape)]
    args += [x, shift, scale, gate, g.reshape(1, d), w1_bf16, w2_bf16, g_final.reshape(1, d)]
    specs += [rows(d), _row_block(shift, tm), _row_block(scale, tm), _row_block(gate, tm),
              _const2((1, d)), layer_w(d, dff), layer_w(dff, d), _const2((1, d))]
    return pl.pallas_call(
        functools.partial(_mlp_kernel, ff_chunk=1024, final_norm=final_norm, mixer_out=kind),
        grid=(bsz, seq // tm),
        in_specs=specs,
        out_specs=rows(d),
        out_shape=jax.ShapeDtypeStruct((bsz, seq, d), F32),
        compiler_params=_cparams(2),
        name="mlp",
    )(*args)


S5_CH = 128
S5_BLK = (S5_CH // S5_GROUP) * S5_STATE


def _s5_disc_kernel(lr_ref, li_ref, ldt_ref, br_ref, bi_ref, ar_ref, ai_ref, bbr_ref, bbi_ref):
    lr, li = lr_ref[...], li_ref[...]
    dt = jnp.exp(ldt_ref[...])
    mag = jnp.exp(lr * dt)
    ab_re, ab_im = mag * jnp.cos(li * dt), mag * jnp.sin(li * dt)
    den = lr * lr + li * li
    nr = ab_re - 1.0
    f_re = (nr * lr + ab_im * li) / den
    f_im = (ab_im * lr - nr * li) / den
    ar_ref[...] = ab_re
    ai_ref[...] = ab_im
    for c in range(br_ref.shape[0]):
        br, bi = br_ref[c], bi_ref[c]
        bbr_ref[c] = f_re * br - f_im * bi
        bbi_ref[c] = f_re * bi + f_im * br


def _s5_weights(lam_re, lam_im, log_dt, b_re, b_im, c_re, c_im):
    g, p = lam_re.shape
    gc = b_re.shape[-1]
    brt = jnp.moveaxis(b_re, 2, 0)
    bit = jnp.moveaxis(b_im, 2, 0)
    ar, ai, bbr, bbi = pl.pallas_call(
        _s5_disc_kernel,
        out_shape=[jax.ShapeDtypeStruct((g, p), F32), jax.ShapeDtypeStruct((g, p), F32),
                   jax.ShapeDtypeStruct((gc, g, p), F32), jax.ShapeDtypeStruct((gc, g, p), F32)],
        name="s5_disc",
    )(lam_re, lam_im, log_dt.reshape(g, 1), brt, bit)
    nblk = (g * gc) // S5_CH
    gpb = g // nblk
    eye = jnp.eye(gpb, dtype=F32)

    def bd_in(bb):
        t = jnp.moveaxis(bb, 0, 1).reshape(nblk, gpb, gc, p)
        return jnp.einsum('ngcp,gh->ngchp', t, eye).reshape(nblk, gpb * gc, gpb * p)

    def bd_out(cc):
        t = cc.reshape(nblk, gpb, gc, p)
        return jnp.einsum('ngcp,gh->ngphc', t, eye).reshape(nblk, gpb * p, gpb * gc)

    wb = jnp.concatenate([bd_in(bbr), bd_in(bbi)], axis=-1).astype(BF16)
    wc = jnp.concatenate([bd_out(c_re), bd_out(c_im)], axis=1).astype(BF16)
    return ar.reshape(nblk, 1, gpb * p), ai.reshape(nblk, 1, gpb * p), wb, wc


def _gelu_tanh(y):
    return 0.5 * y * (1.0 + jnp.tanh(math.sqrt(2.0 / math.pi) * (y + 0.044715 * (y * y * y))))


def _cmul(ar, ai, xr, xi):
    return ar * xr - ai * xi, ar * xi + ai * xr


def _s5_prompt_kernel(x_ref, sh_ref, sc_ref, g_ref, d_ref, ar_ref, ai_ref, wb_ref, wc_ref,
                      perm_ref, unperm_ref, z_ref, fr_ref, fi_ref, u_s, br_s, bi_s, st_r, st_i,
                      *, nseg, seg):
    i = pl.program_id(1)
    nblk = wb_ref.shape[0]
    blk = ar_ref.shape[-1]

    @pl.when(i == 0)
    def _():
        st_r[...] = jnp.zeros(st_r.shape, F32)
        st_i[...] = jnp.zeros(st_i.shape, F32)

    u_s[...] = _dot_sel_lhs(perm_ref[...], _rms_mod(x_ref[0], g_ref[...], sh_ref[0], sc_ref[0]))

    for c in range(nblk):
        lo, hi = c * S5_CH, (c + 1) * S5_CH
        uc = u_s[:, lo:hi]
        bu = _dot(uc.astype(BF16), wb_ref[c])
        br_s[...] = bu[:, :blk]
        bi_s[...] = bu[:, blk:]
        ar = jnp.broadcast_to(ar_ref[c], (nseg, blk))
        ai = jnp.broadcast_to(ai_ref[c], (nseg, blk))

        def local(j, carry):
            xr, xi = carry
            r0 = pl.multiple_of(j * nseg, nseg)
            pr, pi = _cmul(ar, ai, xr, xi)
            nr = pr + br_s[pl.ds(r0, nseg), :]
            ni = pi + bi_s[pl.ds(r0, nseg), :]
            br_s[pl.ds(r0, nseg), :] = nr
            bi_s[pl.ds(r0, nseg), :] = ni
            return nr, ni

        zero = jnp.zeros((nseg, blk), F32)
        fr, fi = lax.fori_loop(0, seg, local, (zero, zero), unroll=True)

        pr, pi = ar_ref[c], ai_ref[c]
        for _ in range(int(math.log2(seg))):
            pr, pi = _cmul(pr, pi, pr, pi)
        cr, ci = st_r[c], st_i[c]
        rows_r, rows_i = [], []
        for s in range(nseg):
            rows_r.append(cr)
            rows_i.append(ci)
            tr, ti = _cmul(pr, pi, cr, ci)
            cr, ci = tr + fr[s:s + 1], ti + fi[s:s + 1]
        st_r[c] = cr
        st_i[c] = ci
        fr_ref[0, c] = cr
        fi_ref[0, c] = ci
        dr, di = _cmul(ar, ai, jnp.concatenate(rows_r, axis=0), jnp.concatenate(rows_i, axis=0))

        def fix(j, carry):
            dr, di = carry
            r0 = pl.multiple_of(j * nseg, nseg)
            br_s[pl.ds(r0, nseg), :] = br_s[pl.ds(r0, nseg), :] + dr
            bi_s[pl.ds(r0, nseg), :] = bi_s[pl.ds(r0, nseg), :] + di
            return _cmul(ar, ai, dr, di)

        lax.fori_loop(0, seg, fix, (dr, di), unroll=True)

        y = (_dot(br_s[...].astype(BF16), wc_ref[c, :blk, :])
             - _dot(bi_s[...].astype(BF16), wc_ref[c, blk:, :]))
        y = y + d_ref[:, lo:hi] * uc
        u_s[:, lo:hi] = _gelu_tanh(y)

    z_ref[0] = _dot_sel_lhs(unperm_ref[...], u_s[...]).astype(z_ref.dtype)


def _s5_prompt(x, shift, scale, g, d_skip, ar, ai, wb, wc):
    bsz, seq, d = x.shape
    nblk, _, blk = ar.shape
    nseg, seg = SUBLANE, 32
    tm = nseg * seg
    assert seq % tm == 0
    perm = np.zeros((tm, tm), np.float32)
    for s in range(nseg):
        for j in range(seg):
            perm[j * nseg + s, s * seg + j] = 1.0
    unperm = jnp.asarray(perm.T, BF16)
    perm = jnp.asarray(perm, BF16)
    z, fr, fi = pl.pallas_call(
        functools.partial(_s5_prompt_kernel, nseg=nseg, seg=seg),
        grid=(bsz, seq // tm),
        in_specs=[pl.BlockSpec((1, tm, d), lambda b, i: (b, i, 0)),
                  _row_block(shift, tm), _row_block(scale, tm),
                  _const2((1, d)), _const2((1, d)),
                  _const2(ar.shape), _const2(ai.shape), _const2(wb.shape), _const2(wc.shape),
                  _const2((tm, tm)), _const2((tm, tm))],
        out_specs=[pl.BlockSpec((1, tm, d), lambda b, i: (b, i, 0)),
                   pl.BlockSpec((1, nblk, 1, blk), lambda b, i: (b, 0, 0, 0)),
                   pl.BlockSpec((1, nblk, 1, blk), lambda b, i: (b, 0, 0, 0))],
        out_shape=[jax.ShapeDtypeStruct((bsz, seq, d), BF16),
                   jax.ShapeDtypeStruct((bsz, nblk, 1, blk), F32),
                   jax.ShapeDtypeStruct((bsz, nblk, 1, blk), F32)],
        scratch_shapes=[pltpu.VMEM((tm, d), F32), pltpu.VMEM((tm, blk), F32), pltpu.VMEM((tm, blk), F32),
                        pltpu.VMEM((nblk, 1, blk), F32), pltpu.VMEM((nblk, 1, blk), F32)],
        compiler_params=_cparams(2),
        name="s5_prompt",
    )(x, shift, scale, g.reshape(1, d), d_skip.reshape(1, d), ar, ai, wb, wc, perm, unperm)
    return z, fr, fi


def _s5_sample_kernel(x_ref, sh_ref, sc_ref, g_ref, d_ref, ar_ref, ai_ref, wb_ref, wc_ref,
                      s0r_ref, s0i_ref, z_ref, fr_ref, fi_ref, *, nb, steps):
    nblk = wb_ref.shape[0]
    blk = ar_ref.shape[-1]
    u = _rms_mod(x_ref[...], g_ref[...], sh_ref[...], sc_ref[...])
    for c in range(nblk):
        lo, hi = c * S5_CH, (c + 1) * S5_CH
        uc = u[:, lo:hi]
        bu = _dot(uc.astype(BF16), wb_ref[c])
        ar = jnp.broadcast_to(ar_ref[c], (nb, blk))
        ai = jnp.broadcast_to(ai_ref[c], (nb, blk))
        xr, xi = s0r_ref[:, c * blk:(c + 1) * blk], s0i_ref[:, c * blk:(c + 1) * blk]
        xrs, xis = [], []
        for t in range(steps):
            pr, pi = _cmul(ar, ai, xr, xi)
            xr = pr + bu[t * nb:(t + 1) * nb, :blk]
            xi = pi + bu[t * nb:(t + 1) * nb, blk:]
            xrs.append(xr)
            xis.append(xi)
        fr_ref[:, c * blk:(c + 1) * blk] = xr
        fi_ref[:, c * blk:(c + 1) * blk] = xi
        y = (_dot(jnp.concatenate(xrs, axis=0).astype(BF16), wc_ref[c, :blk, :])
             - _dot(jnp.concatenate(xis, axis=0).astype(BF16), wc_ref[c, blk:, :]))
        y = y + d_ref[:, lo:hi] * uc
        z_ref[:, lo:hi] = _gelu_tanh(y)


def _s5_sample(x_tb, shift_tb, scale_tb, g, d_skip, ar, ai, wb, wc, s0r, s0i, nb, steps):
    rows, d = x_tb.shape
    nblk, _, blk = ar.shape
    return pl.pallas_call(
        functools.partial(_s5_sample_kernel, nb=nb, steps=steps),
        out_shape=[jax.ShapeDtypeStruct((rows, d), F32),
                   jax.ShapeDtypeStruct((nb, nblk * blk), F32),
                   jax.ShapeDtypeStruct((nb, nblk * blk), F32)],
        compiler_params=pltpu.CompilerParams(vmem_limit_bytes=VMEM_LIMIT),
        name="s5_sample",
    )(x_tb, shift_tb, scale_tb, g.reshape(1, d), d_skip.reshape(1, d), ar, ai, wb, wc, s0r, s0i)


def _shift_rows(u, prev, k):
    if k == 0:
        return u
    rolled = pltpu.roll(u, k, 0)
    head = pltpu.roll(prev, k, 0)
    row = lax.broadcasted_iota(I32, (SUBLANE, u.shape[1]), 0)
    first = jnp.where(row < k, head, rolled[:SUBLANE])
    return jnp.concatenate([first, rolled[SUBLANE:]], axis=0)


def _sconv_prompt_kernel(x_ref, sh_ref, sc_ref, g_ref, wi_ref, gate_ref, buf_ref, wc_ref, wo_ref,
                         y_ref, nb_ref, prev_s, *, width):
    i = pl.program_id(1)
    d = x_ref.shape[-1]

    @pl.when(i == 0)
    def _():
        prev_s[...] = jnp.zeros(prev_s.shape, F32)
        prev_s[SUBLANE - (width - 1):, :] = buf_ref[0]

    h = _rms_mod(x_ref[0], g_ref[...], sh_ref[0], sc_ref[0]).astype(BF16)
    gb = _dot(h, wi_ref[:, :d])
    u = _dot(h, wi_ref[:, d:2 * d]) * _dot(h, wi_ref[:, 2 * d:])
    prev = prev_s[...]
    conv = u * wc_ref[width - 1:width, :]
    for k in range(1, width):
        conv = conv + _shift_rows(u, prev, k) * wc_ref[width - 1 - k:width - k, :]
    prev_s[...] = u[u.shape[0] - SUBLANE:, :]
    nb_ref[0] = u[u.shape[0] - (width - 1):, :]
    y_ref[0] = x_ref[0] + gate_ref[0] * _dot((gb * conv).astype(BF16), wo_ref[...])


def _sconv_prompt(x, shift, scale, g, w_in_bf16, gate, buf, w_conv, w_out_bf16, tm):
    bsz, seq, d = x.shape
    width = w_conv.shape[0]
    return pl.pallas_call(
        functools.partial(_sconv_prompt_kernel, width=width),
        grid=(bsz, seq // tm),
        in_specs=[pl.BlockSpec((1, tm, d), lambda b, i: (b, i, 0)),
                  _row_block(shift, tm), _row_block(scale, tm), _const2((1, d)),
                  _resident(w_in_bf16.shape), _row_block(gate, tm),
                  pl.BlockSpec((1, width - 1, d), lambda b, i: (b, 0, 0)),
                  _const2((width, d)), _resident((d, d))],
        out_specs=[pl.BlockSpec((1, tm, d), lambda b, i: (b, i, 0)),
                   pl.BlockSpec((1, width - 1, d), lambda b, i: (b, 0, 0))],
        out_shape=[jax.ShapeDtypeStruct((bsz, seq, d), F32),
                   jax.ShapeDtypeStruct((bsz, width - 1, d), F32)],
        scratch_shapes=[pltpu.VMEM((SUBLANE, d), F32)],
        compiler_params=_cparams(2),
        name="sconv_prompt",
    )(x, shift, scale, g.reshape(1, d), w_in_bf16, gate, buf, w_conv, w_out_bf16)


def _sconv_sample_kernel(p_ref, x_ref, gate_ref, buf_ref, wc_ref, wo_ref, y_ref, nb_ref,
                         *, width, nb, steps):
    d = x_ref.shape[-1]
    p = p_ref[...]
    gb, gc, xh = p[:, :d], p[:, d:2 * d], p[:, 2 * d:]
    u = gc * xh
    ext = jnp.concatenate([buf_ref[...], u], axis=0)
    conv = ext[0:steps * nb] * wc_ref[0:1, :]
    for k in range(1, width):
        conv = conv + ext[k * nb:(k + steps) * nb] * wc_ref[k:k + 1, :]
    nb_ref[...] = ext[steps * nb:]
    y_ref[...] = x_ref[...] + gate_ref[...] * _dot((gb * conv).astype(BF16), wo_ref[...])


def _sconv_sample(p_tb, x_tb, gate_tb, buf_kb, w_conv, w_out_bf16, nb, steps):
    rows, d = x_tb.shape
    width = w_conv.shape[0]
    return pl.pallas_call(
        functools.partial(_sconv_sample_kernel, width=width, nb=nb, steps=steps),
        out_shape=[jax.ShapeDtypeStruct((rows, d), F32),
                   jax.ShapeDtypeStruct(((width - 1) * nb, d), F32)],
        compiler_params=pltpu.CompilerParams(vmem_limit_bytes=VMEM_LIMIT),
        name="sconv_sample",
    )(p_tb, x_tb, gate_tb, buf_kb, w_conv, w_out_bf16)


KEY_NEG_INF = -2139095041
KEY_POS_INF = 2139095040


def _code_to_f32(key):
    key = jnp.clip(key, KEY_NEG_INF, KEY_POS_INF)
    return lax.bitcast_convert_type(key ^ ((key >> 31) & 0x7FFFFFFF), F32)


def _stack_heads(qcat):
    w = 4 * IDX_DIM
    return jnp.concatenate([qcat[:, h * w:(h + 1) * w] for h in range(IDX_HEADS)], axis=0)


def _idx_rhs(ki):
    kh, kl = _split(ki)
    return jnp.concatenate([kh, kh, kl, jnp.zeros(kh.shape, BF16)], axis=1)


def _idx_score(dts, wcols, nrows):
    s = wcols[0] * jnp.maximum(dts[0:nrows], 0.0)
    for h in range(1, IDX_HEADS):
        s = s + wcols[h] * jnp.maximum(dts[h * nrows:(h + 1) * nrows], 0.0)
    return s


def _idx_wcols(kw):
    sc = (IDX_HEADS ** -0.5) * (IDX_DIM ** -0.5)
    return [kw[:, IDX_DIM + h:IDX_DIM + h + 1] * sc for h in range(IDX_HEADS)]


def _topk_select(count, rows, lanes, topk, idx_bits, stash=lambda v: (lambda: v)):
    wide = lambda v: jnp.broadcast_to(v, (rows, lanes))

    def bit_step(it, tu):
        mask = jnp.left_shift(jnp.int32(1), 31 - it)
        cand_u = tu | mask
        cand = stash(wide(_code_to_f32(cand_u ^ INT_MIN)))
        cnt = count(lambda c, sc, idx: jnp.where(sc >= cand(), 1.0, 0.0))
        return jnp.where(cnt >= topk, cand_u, tu)

    t = _code_to_f32(lax.fori_loop(0, 32, bit_step, jnp.zeros((rows, 1), I32)) ^ INT_MIN)
    tw = wide(t)
    n_gt = count(lambda c, sc, idx: jnp.where(sc > tw, 1.0, 0.0))
    n_ge = count(lambda c, sc, idx: jnp.where(sc >= tw, 1.0, 0.0))
    need = topk - n_gt
    tied = jnp.max(jnp.where(t > -jnp.inf, n_ge - topk, 0.0)) > 0.0

    def tie_search(_):
        def idx_step(it, j):
            cand = j | jnp.left_shift(jnp.int32(1), idx_bits - 1 - it)
            cw = wide(cand)
            cnt = count(lambda c, sc, idx: jnp.where((sc == tw) & (idx < cw), 1.0, 0.0))
            return jnp.where(cnt < need, cand, j)
        return lax.fori_loop(0, idx_bits, idx_step, jnp.zeros((rows, 1), I32))

    j = lax.cond(tied, tie_search, lambda _: jnp.full((rows, 1), 2 ** idx_bits, I32), 0)
    return t, j


def _dsa_prompt_kernel(q_ref, qcat_ref, kwq_ref, kcat_ref, kaug_ref, vaug_ref, bias_ref, o_ref,
                       sc_s, madd_s, wbc_s, cand_s, qa_s, lg_s, mx_s, sh_s, acc_s, *, tq, topk, idx_bits):
    i = pl.program_id(1)
    nk = i + 1
    nk2 = (nk + 1) // 2
    r = N_HEADS // N_KV_HEADS

    for h, w in enumerate(_idx_wcols(kwq_ref[0])):
        wbc_s[h] = jnp.broadcast_to(w, (tq, tq))
    qrow = lax.broadcasted_iota(I32, (tq, tq), 0)
    kcol = lax.broadcasted_iota(I32, (tq, tq), 1)
    qpos = i * tq + qrow
    wq = 4 * IDX_DIM

    def score_pair(c2, _):
        kc = kcat_ref[0, c2]
        s = None
        hpd = 4
        for hp in range(IDX_HEADS // hpd):
            lhs = jnp.concatenate([qcat_ref[0, :, (hpd * hp + v) * wq:(hpd * hp + v + 1) * wq] for v in range(hpd)],
                                  axis=0)
            dd = jnp.maximum(_dot(lhs, kc), 0.0)
            for v in range(hpd):
                w = wbc_s[hpd * hp + v]
                d = jnp.concatenate([w, w], axis=1) * dd[v * tq:(v + 1) * tq]
                s = d if s is None else s + d
        for u in range(2):
            c = 2 * c2 + u
            sc_s[c] = jnp.where(c * tq + kcol <= qpos, s[:, u * tq:(u + 1) * tq], -jnp.inf)
        return 0

    lax.fori_loop(0, nk2, score_pair, 0)

    def count(fn):
        def body(c2, acc):
            c = 2 * c2
            acc = acc + fn(c, sc_s[c], c * tq + kcol)
            return acc + fn(c + 1, sc_s[c + 1], (c + 1) * tq + kcol)
        acc = lax.fori_loop(0, nk2, body, jnp.zeros((tq, tq), F32))
        return jnp.sum(acc, axis=1, keepdims=True)

    def stash(v):
        cand_s[...] = v
        return lambda: cand_s[...]

    t, j = _topk_select(count, tq, tq, float(topk), idx_bits, stash)
    tw = jnp.broadcast_to(t, (tq, tq))
    jw = jnp.broadcast_to(j, (tq, tq))

    def mask_pair(c2, _):
        for u in range(2):
            c = 2 * c2 + u
            sc = sc_s[c]
            sel = (sc > tw) | ((sc == tw) & (c * tq + kcol <= jw))
            madd_s[c] = jnp.where(sel & (sc > -jnp.inf), 0.0, NEG_BIG)
        return 0

    lax.fori_loop(0, nk2, mask_pair, 0)

    q = q_ref[0].astype(F32)
    zpad = jnp.zeros((tq, LANE - HEAD_DIM), F32)
    for g in range(N_KV_HEADS):
        qa_s[g] = jnp.concatenate(
            [jnp.concatenate([q[:, (g * r + u) * HEAD_DIM:(g * r + u + 1) * HEAD_DIM], zpad], axis=1)
             for u in range(r)], axis=0).astype(BF16)

    def pair_mask(c2, near, g):
        halves = []
        for u in range(2):
            c = 2 * c2 + u
            m = madd_s[c]
            if near:
                back = nk - 1 - c
                carries = (back == 0) | ((back == 1) & (kcol > qrow))
                halves.append([m + jnp.where(carries, bias_ref[g * r + v], 0.0) for v in range(r)])
            else:
                halves.append([m] * r)
        return jnp.concatenate([jnp.concatenate([halves[0][v], halves[1][v]], axis=1) for v in range(r)], axis=0)

    ngh = lg_s.shape[0]
    for half in range(N_KV_HEADS // ngh):
        groups = [half * ngh + k for k in range(ngh)]
        mx_s[...] = jnp.full(mx_s.shape, NEG_BIG, F32)

        def logits_pair(c2, near):
            for k, g in enumerate(groups):
                lg = _dot(qa_s[g], kaug_ref[0, c2, g * LANE:(g + 1) * LANE, :]) + pair_mask(c2, near, g)
                lg_s[k, c2] = lg
                mx_s[k] = jnp.maximum(mx_s[k], jnp.maximum(lg[:, :tq], lg[:, tq:]))

        def far(c4, _):
            logits_pair(2 * c4, False)
            logits_pair(2 * c4 + 1, False)
            return 0

        nfar = jnp.maximum(nk2 - 2, 0)
        lax.fori_loop(0, nfar // 2, far, 0)

        @pl.when(nfar % 2 == 1)
        def _():
            logits_pair(nfar - 1, False)

        @pl.when(nk2 >= 2)
        def _():
            logits_pair(nk2 - 2, True)

        logits_pair(nk2 - 1, True)

        for k in range(ngh):
            sh_s[k] = jnp.broadcast_to(jnp.max(mx_s[k], axis=1, keepdims=True), (r * tq, LANE))
        acc_s[...] = jnp.zeros(acc_s.shape, F32)

        def pv_pair(c2):
            r0 = pl.multiple_of(c2 * 2 * tq, 2 * tq)
            va = vaug_ref[0, pl.ds(r0, 2 * tq), :]
            for k, g in enumerate(groups):
                s = sh_s[k]
                p = jnp.exp(lg_s[k, c2] - jnp.concatenate([s, s], axis=1))
                acc_s[k] = acc_s[k] + _dot(p.astype(BF16), va[:, g * LANE:(g + 1) * LANE])

        def pv_two(c4, _):
            pv_pair(2 * c4)
            pv_pair(2 * c4 + 1)
            return 0

        lax.fori_loop(0, nk2 // 2, pv_two, 0)

        @pl.when(nk2 % 2 == 1)
        def _():
            pv_pair(nk2 - 1)
        for k, g in enumerate(groups):
            acc = acc_s[k]
            o = acc[:, 0:HEAD_DIM] / acc[:, HEAD_DIM:HEAD_DIM + 1]
            heads = jnp.concatenate([o[u * tq:(u + 1) * tq, :] for u in range(r)], axis=1)
            o_ref[0, :, g * r * HEAD_DIM:(g + 1) * r * HEAD_DIM] = heads.astype(o_ref.dtype)


def _t5_bucket(dist):
    n = jnp.maximum(dist, 0)
    max_exact = N_BUCKETS // 2
    nf = jnp.maximum(n, max_exact).astype(F32)
    large = max_exact + (jnp.log(nf / max_exact) / math.log(MAX_DISTANCE / max_exact)
                         * (N_BUCKETS - max_exact)).astype(I32)
    large = jnp.minimum(large, N_BUCKETS - 1)
    return jnp.where(n < max_exact, n, large)


def _bias_table(rel_bias, n):
    tab = rel_bias[_t5_bucket(jnp.arange(n, dtype=I32))]
    return (tab - rel_bias[N_BUCKETS - 1][None, :]).T


DSA_GROUPS_PER_PASS = 2


def _dsa_prompt(q, qcat, kw, kcat, kaug, vaug, rel_bias, tq):
    bsz, seq, nq = q.shape
    topk = min(TOPK_MAX, seq // 4)
    nkc = seq // tq
    r = N_HEADS // N_KV_HEADS
    assert tq == MAX_DISTANCE and seq % (2 * tq) == 0
    tab = _bias_table(rel_bias, tq)
    dmod = (np.arange(tq)[:, None] - np.arange(tq)[None, :]) % tq
    onehot = (jnp.arange(tq, dtype=I32)[:, None, None] == jnp.asarray(dmod, I32)[None]).astype(F32)
    bias = jnp.einsum('hd,dqk->hqk', tab, onehot, precision=lax.Precision.HIGHEST)
    assert kcat.shape[1:] == (seq // (2 * tq), 4 * IDX_DIM, 2 * tq)
    row = lambda n: pl.BlockSpec((1, tq, n), lambda b, i: (b, i, 0))
    tiles = lambda a: pl.BlockSpec((1,) + a.shape[1:], lambda b, i: (b, 0, 0, 0), pipeline_mode=pl.Buffered(1))
    return pl.pallas_call(
        functools.partial(_dsa_prompt_kernel, tq=tq, topk=topk, idx_bits=int(math.log2(seq))),
        grid=(bsz, nkc),
        in_specs=[row(nq), row(qcat.shape[-1]), row(LANE), tiles(kcat), tiles(kaug),
                  pl.BlockSpec((1, seq, vaug.shape[-1]), lambda b, i: (b, 0, 0), pipeline_mode=pl.Buffered(1)),
                  _resident((N_HEADS, tq, tq))],
        out_specs=row(nq),
        out_shape=jax.ShapeDtypeStruct((bsz, seq, nq), BF16),
        scratch_shapes=[pltpu.VMEM((nkc, tq, tq), F32), pltpu.VMEM((nkc, tq, tq), F32),
                        pltpu.VMEM((IDX_HEADS, tq, tq), F32), pltpu.VMEM((tq, tq), F32),
                        pltpu.VMEM((N_KV_HEADS, r * tq, LANE), BF16),
                        pltpu.VMEM((DSA_GROUPS_PER_PASS, nkc // 2, r * tq, 2 * tq), F32),
                        pltpu.VMEM((DSA_GROUPS_PER_PASS, r * tq, tq), F32),
                        pltpu.VMEM((DSA_GROUPS_PER_PASS, r * tq, LANE), F32),
                        pltpu.VMEM((DSA_GROUPS_PER_PASS, r * tq, LANE), F32)],
        compiler_params=_cparams(2),
        name="dsa_prompt",
    )(q, qcat, kw, kcat, kaug, vaug, bias)


QP = SUBLANE
PAGES_PER_STEP = 32


def _page_specs(shape4, layer, pps):
    def spec(u):
        return pl.BlockSpec((1, 1) + tuple(shape4[2:]),
                            lambda b, g, pt: (layer, pt[b, g * pps + u], 0, 0))
    return [spec(u) for u in range(pps)]


def _pages_t(cache):
    nd = cache.ndim
    t = jnp.transpose(cache, (0, 1) + tuple(range(3, nd)) + (2,))
    return t.reshape(t.shape[0], t.shape[1], -1, t.shape[-1])


def _pad_t(a, page):
    return jnp.concatenate([a, jnp.zeros((page - QP, a.shape[1]), F32)], axis=0).T


def _dsa_sample_score_kernel(pt_ref, qcat_ref, kw_ref, kcn_ref, *rest, pps):
    page_refs, o_ref, on_ref = rest[:pps], rest[pps], rest[pps + 1]
    lhs = _stack_heads(qcat_ref[0]).astype(BF16)
    wcols = _idx_wcols(kw_ref[0])
    kt = jnp.concatenate([page_refs[u][0, 0] for u in range(pps)], axis=1)
    kh, kl = _split(kt)
    rhs = jnp.concatenate([kh, kh, kl, jnp.zeros(kh.shape, BF16)], axis=0)
    o_ref[0] = _idx_score(_dot(lhs, rhs), wcols, QP)

    @pl.when(pl.program_id(1) == pl.num_programs(1) - 1)
    def _():
        on_ref[0] = _idx_score(_dot(lhs, _pad_t(kcn_ref[0], on_ref.shape[-1]).astype(BF16)), wcols, QP)


def _dsa_sample_scores(page_table, qcat, kw, kcn, kidx_t, layer, pps):
    nb, n_pages = page_table.shape
    page = kidx_t.shape[-1]
    bmap = lambda b, g, pt: (b, 0, 0)
    grid_spec = pltpu.PrefetchScalarGridSpec(
        num_scalar_prefetch=1,
        grid=(nb, n_pages // pps),
        in_specs=[pl.BlockSpec((1, QP, qcat.shape[-1]), bmap), pl.BlockSpec((1, QP, LANE), bmap),
                  pl.BlockSpec((1, QP, kcn.shape[-1]), bmap)] + _page_specs(kidx_t.shape, layer, pps),
        out_specs=[pl.BlockSpec((1, QP, pps * page), lambda b, g, pt: (b, 0, g)),
                   pl.BlockSpec((1, QP, page), bmap)])
    return pl.pallas_call(
        functools.partial(_dsa_sample_score_kernel, pps=pps),
        grid_spec=grid_spec,
        out_shape=[jax.ShapeDtypeStruct((nb, QP, n_pages * page), F32),
                   jax.ShapeDtypeStruct((nb, QP, page), F32)],
        compiler_params=_cparams(2),
        name="dsa_sample_scores",
    )(page_table, qcat, kw, kcn, *([kidx_t] * pps))


def _new_token_scores(s_new, steps):
    lane = lax.broadcasted_iota(I32, s_new.shape, 1)
    q = lax.broadcasted_iota(I32, s_new.shape, 0) % QP
    return jnp.where((lane <= q) & (lane < steps), s_new, -jnp.inf)


def _dsa_sample_select_kernel(sc_ref, sn_ref, t_ref, j_ref, *, steps, topk, idx_bits):
    scores = jnp.concatenate([sc_ref[...], _new_token_scores(sn_ref[...], steps)], axis=1)
    idx = lax.broadcasted_iota(I32, scores.shape, 1)
    count = lambda fn: jnp.sum(fn(0, scores, idx), axis=1, keepdims=True)
    t, j = _topk_select(count, scores.shape[0], scores.shape[1], float(topk), idx_bits)
    t_ref[...] = t
    j_ref[...] = j


def _dsa_sample_select(scores, snew, steps, topk):
    nb, _, past = scores.shape
    page = snew.shape[-1]
    rows = nb * QP
    rb = math.gcd(rows, 256)
    t, j = pl.pallas_call(
        functools.partial(_dsa_sample_select_kernel, steps=steps, topk=topk,
                          idx_bits=int(math.ceil(math.log2(past + page)))),
        grid=(rows // rb,),
        in_specs=[pl.BlockSpec((rb, past), lambda i: (i, 0)), pl.BlockSpec((rb, page), lambda i: (i, 0))],
        out_specs=[pl.BlockSpec((rb, 1), lambda i: (i, 0)), pl.BlockSpec((rb, 1), lambda i: (i, 0))],
        out_shape=[jax.ShapeDtypeStruct((rows, 1), F32), jax.ShapeDtypeStruct((rows, 1), I32)],
        compiler_params=_cparams(1),
        name="dsa_sample_select",
    )(scores.reshape(rows, past), snew.reshape(rows, page))
    return t.reshape(nb, QP, 1), j.reshape(nb, QP, 1)


def _dsa_sample_attend_kernel(pt_ref, q_ref, kn_ref, vn_ref, sc_ref, sn_ref, t_ref, j_ref,
                              blast_ref, bnew_ref, *rest, pps, n_pages, steps):
    k_refs, v_refs, o_ref = rest[:pps], rest[pps:2 * pps], rest[2 * pps]
    lhs_s, m_s, l_s, acc_s = rest[2 * pps + 1:]
    g = pl.program_id(1)
    last = pl.num_programs(1) - 1
    page = k_refs[0].shape[-1]
    past = n_pages * page
    n = pps * page
    r = N_HEADS // N_KV_HEADS
    grow = r * QP

    @pl.when(g == 0)
    def _():
        m_s[...] = jnp.full(m_s.shape, NEG_BIG, F32)
        l_s[...] = jnp.zeros(l_s.shape, F32)
        acc_s[...] = jnp.zeros(acc_s.shape, F32)
        q = q_ref[0]
        for gg in range(N_KV_HEADS):
            lhs_s[gg] = jnp.concatenate(
                [q[:, (gg * r + u) * HEAD_DIM:(gg * r + u + 1) * HEAD_DIM] for u in range(r)], axis=0).astype(BF16)

    t, j = t_ref[0], j_ref[0]

    def select(sc, idx):
        sel = (sc > t) | ((sc == t) & (idx <= j))
        return jnp.where(sel & (sc > -jnp.inf), 0.0, NEG_BIG)

    def attend(kt, vt, madd, bias):
        mrows = jnp.concatenate([madd] * r, axis=0)
        for gg in range(N_KV_HEADS):
            rows = slice(gg * grow, (gg + 1) * grow)
            feat = slice(gg * HEAD_DIM, (gg + 1) * HEAD_DIM)
            lg = _dot(lhs_s[gg], kt[feat, :]) + mrows
            if bias is not None:
                lg = lg + bias[rows, :]
            m_old = m_s[rows, :]
            m_new = jnp.maximum(m_old, jnp.max(lg, axis=1, keepdims=True))
            p = jnp.exp(lg - m_new)
            alpha = jnp.exp(m_old - m_new)
            l_s[rows, :] = alpha * l_s[rows, :] + jnp.sum(p, axis=1, keepdims=True)
            acc_s[rows, :] = alpha * acc_s[rows, :] + _dot_nt(p.astype(BF16), vt[feat, :])
            m_s[rows, :] = m_new

    def pages(bias):
        kt = jnp.concatenate([k_refs[u][0, 0] for u in range(pps)], axis=1).astype(BF16)
        vt = jnp.concatenate([v_refs[u][0, 0] for u in range(pps)], axis=1).astype(BF16)
        idx = g * n + lax.broadcasted_iota(I32, (QP, n), 1)
        attend(kt, vt, select(sc_ref[0], idx), bias)

    @pl.when(g != last)
    def _():
        pages(None)

    @pl.when(g == last)
    def _():
        pages(jnp.concatenate([jnp.zeros((N_HEADS * QP, n - page), F32), blast_ref[...]], axis=1))
        idx = past + lax.broadcasted_iota(I32, (QP, page), 1)
        attend(_pad_t(kn_ref[0], page).astype(BF16), _pad_t(vn_ref[0], page).astype(BF16),
               select(_new_token_scores(sn_ref[0], steps), idx), bnew_ref[...])
        o = acc_s[...] / l_s[...]
        for h in range(N_HEADS):
            o_ref[0, :, h * HEAD_DIM:(h + 1) * HEAD_DIM] = o[h * QP:(h + 1) * QP, :]


def _dsa_sample(page_table, q, kn, vn, scores, snew, k_t, v_t, rel_bias, layer, steps, pps):
    nb, n_pages = page_table.shape
    page = k_t.shape[-1]
    past = n_pages * page
    nq, nkv = N_HEADS * HEAD_DIM, N_KV_HEADS * HEAD_DIM
    t, j = _dsa_sample_select(scores, snew, steps, min(TOPK_MAX, (past + steps) // 4))
    n_steps = n_pages // pps
    r = N_HEADS // N_KV_HEADS
    tab = _bias_table(rel_bias, 2 * page)
    qq = np.arange(QP)[:, None]
    off = np.arange(page)[None, :]
    blast = tab[:, page + qq - off].reshape(N_HEADS * QP, page)
    bnew = tab[:, np.maximum(qq - off, 0)].reshape(N_HEADS * QP, page)
    bmap = lambda b, g, pt: (b, 0, 0)
    grid_spec = pltpu.PrefetchScalarGridSpec(
        num_scalar_prefetch=1,
        grid=(nb, n_steps),
        in_specs=[pl.BlockSpec((1, QP, nq), bmap),
                  pl.BlockSpec((1, QP, nkv), bmap),
                  pl.BlockSpec((1, QP, nkv), bmap),
                  pl.BlockSpec((1, QP, pps * page), lambda b, g, pt: (b, 0, g)),
                  pl.BlockSpec((1, QP, page), bmap),
                  pl.BlockSpec((1, QP, 1), bmap), pl.BlockSpec((1, QP, 1), bmap),
                  pl.BlockSpec(blast.shape, lambda b, g, pt: (0, 0)),
                  pl.BlockSpec(bnew.shape, lambda b, g, pt: (0, 0))]
                 + _page_specs(k_t.shape, layer, pps) + _page_specs(v_t.shape, layer, pps),
        out_specs=pl.BlockSpec((1, QP, nq), bmap),
        scratch_shapes=[pltpu.VMEM((N_KV_HEADS, r * QP, HEAD_DIM), BF16),
                        pltpu.VMEM((N_HEADS * QP, 1), F32), pltpu.VMEM((N_HEADS * QP, 1), F32),
                        pltpu.VMEM((N_HEADS * QP, HEAD_DIM), F32)])
    return pl.pallas_call(
        functools.partial(_dsa_sample_attend_kernel, pps=pps, n_pages=n_pages, steps=steps),
        grid_spec=grid_spec,
        out_shape=jax.ShapeDtypeStruct((nb, QP, nq), F32),
        compiler_params=_cparams(2),
        name="dsa_sample_attend",
    )(page_table, q, kn, vn, scores, snew, t, j, blast, bnew, *([k_t] * pps), *([v_t] * pps))


SSD_COLS = 512


def _softplus(x):
    return jnp.maximum(x, 0.0) + jnp.log1p(jnp.exp(-jnp.abs(x)))


def _split3(a):
    p1 = a.astype(BF16)
    r1 = a - p1.astype(F32)
    p2 = r1.astype(BF16)
    return p1, p2, (r1 - p2.astype(F32)).astype(BF16)


def _dot_sel_rhs(a, e):
    p1, p2, p3 = _split3(a)
    return _dot(p1, e) + (_dot(p2, e) + _dot(p3, e))


def _dot_sel_lhs(e, a):
    p1, p2, p3 = _split3(a)
    return _dot(e, p1) + (_dot(e, p2) + _dot(e, p3))


def _ssd_prompt_kernel(z_ref, xbc_ref, dtr_ref, cw_ref, cb_ref, dtb_ref, alog_ref, dexp_ref, ng_ref,
                       e_ref, y_ref, nbuf_ref, hf_ref, prev_s, xc_s, y_s, h_s, *, width, nh, inner):
    i = pl.program_id(1)
    last = pl.num_programs(1) - 1
    qn, cdim = xbc_ref.shape[1], xbc_ref.shape[2]
    ngrp, ns, hd = SSD_GROUPS, SSD_STATE, SSD_HEADDIM
    hpg = nh // ngrp

    @pl.when(i == 0)
    def _():
        prev_s[...] = jnp.zeros(prev_s.shape, F32)
        h_s[...] = jnp.zeros(h_s.shape, F32)

    for cb in range(cdim // SSD_COLS):
        cols = slice(cb * SSD_COLS, (cb + 1) * SSD_COLS)
        xr = xbc_ref[0, :, cols]
        prev = prev_s[:, cols]
        conv = xr * cw_ref[width - 1:width, cols]
        for k in range(1, width):
            conv = conv + _shift_rows(xr, prev, k) * cw_ref[width - 1 - k:width - k, cols]
        xc_s[:, cols] = _silu(conv + cb_ref[:, cols])
        prev_s[:, cols] = xr[qn - SUBLANE:, :]
        nbuf_ref[0, :, cols] = xr[qn - (width - 1):, :]

    hl = lax.broadcasted_iota(I32, (qn, LANE), 1)
    dt = jnp.where(hl < nh, _softplus(dtr_ref[0] + dtb_ref[...]), 0.0)
    dta = dt * (-jnp.exp(alog_ref[...]))
    qrow = lax.broadcasted_iota(I32, (qn, qn), 0)
    kcol = lax.broadcasted_iota(I32, (qn, qn), 1)
    causal = kcol <= qrow
    cum = _dot_sel_lhs(jnp.where(causal, 1.0, 0.0).astype(BF16), dta)
    cum_t = cum.T
    cum_last = cum[qn - 1:qn, :]
    ecum = jnp.exp(cum)
    dtw = dt * jnp.exp(cum_last - cum)
    e_last = jnp.exp(cum_last)
    lane_p = lax.broadcasted_iota(I32, (qn, LANE), 1)
    row_p = lax.broadcasted_iota(I32, (LANE, ns), 0)

    for g in range(ngrp):
        gl = slice(g * SSD_COLS, (g + 1) * SSD_COLS)
        e_g = e_ref[:, gl]
        xs_g = xc_s[:, gl]
        dtx_g = xs_g * _dot_sel_rhs(dt, e_g)
        dtxw_g = xs_g * _dot_sel_rhs(dtw, e_g)
        ecx_g = _dot_sel_rhs(ecum, e_g)
        bm = xc_s[:, inner + g * ns:inner + (g + 1) * ns].astype(BF16)
        cm = xc_s[:, inner + (ngrp + g) * ns:inner + (ngrp + g + 1) * ns].astype(BF16)
        cbm = _dot_nt(cm, bm)
        for pi in range(hpg // 2):
            ha = g * hpg + 2 * pi
            pls = slice(pi * LANE, (pi + 1) * LANE)
            lanes = slice(ha * hd, ha * hd + LANE)
            dtx_p = dtx_g[:, pls].astype(BF16)
            yds = []
            for h in (ha, ha + 1):
                seg = cum[:, h:h + 1] - cum_t[h:h + 1, :]
                dec = jnp.exp(jnp.where(causal, seg, -jnp.inf))
                yds.append(_dot((cbm * dec).astype(BF16), dtx_p))
            yd = jnp.where(lane_p < hd, yds[0], yds[1])
            hp = h_s[ha // 2]
            yo = _dot_nt(cm, hp.astype(BF16)) * ecx_g[:, pls]
            y_s[:, lanes] = yd + yo + dexp_ref[:, lanes] * xs_g[:, pls]
            s_new = _dot(dtxw_g[:, pls].T.astype(BF16), bm)
            cd = jnp.where(row_p < hd, e_last[:, ha:ha + 1], e_last[:, ha + 1:ha + 2])
            h_s[ha // 2] = hp * cd + s_new

    for g in range(ngrp):
        gl = slice(g * SSD_COLS, (g + 1) * SSD_COLS)
        yg = y_s[:, gl] * _silu(z_ref[0, :, gl])
        ms = jnp.mean(yg * yg, axis=-1, keepdims=True)
        y_ref[0, :, gl] = ((yg * lax.rsqrt(ms + EPS)) * ng_ref[:, gl]).astype(y_ref.dtype)

    @pl.when(i == last)
    def _():
        hf_ref[0] = h_s[...]


def _head_expand(nh, hd):
    e = np.zeros((LANE, nh * hd), np.float32)
    for h in range(nh):
        e[h, h * hd:(h + 1) * hd] = 1.0
    return jnp.asarray(e, BF16)


def _pad_lanes(v, n=LANE):
    v = v.reshape(1, -1)
    return jnp.pad(v, ((0, 0), (0, n - v.shape[1])))


def _ssd_prompt(z, xbc, dtr, conv_w, conv_b, dt_bias, a_log, d_skip, norm_g):
    bsz, seq, inner = z.shape
    cdim = xbc.shape[-1]
    nh = dt_bias.shape[0]
    width = conv_w.shape[0]
    qn = math.gcd(seq, SSD_CHUNK)
    assert qn == SSD_CHUNK and inner == SSD_GROUPS * SSD_COLS and cdim % SSD_COLS == 0
    dexp = jnp.repeat(d_skip, SSD_HEADDIM).reshape(1, inner)
    blk = lambda n: pl.BlockSpec((1, qn, n), lambda b, i: (b, i, 0))
    y, nbuf, hf = pl.pallas_call(
        functools.partial(_ssd_prompt_kernel, width=width, nh=nh, inner=inner),
        grid=(bsz, seq // qn),
        in_specs=[blk(inner), blk(cdim), blk(LANE),
                  _const2((width, cdim)), _const2((1, cdim)), _const2((1, LANE)), _const2((1, LANE)),
                  _const2((1, inner)), _const2((1, inner)), _const2((LANE, inner))],
        out_specs=[blk(inner),
                   pl.BlockSpec((1, width - 1, cdim), lambda b, i: (b, 0, 0)),
                   pl.BlockSpec((1, nh // 2, LANE, SSD_STATE), lambda b, i: (b, 0, 0, 0))],
        out_shape=[jax.ShapeDtypeStruct((bsz, seq, inner), BF16),
                   jax.ShapeDtypeStruct((bsz, width - 1, cdim), F32),
                   jax.ShapeDtypeStruct((bsz, nh // 2, LANE, SSD_STATE), F32)],
        scratch_shapes=[pltpu.VMEM((SUBLANE, cdim), F32), pltpu.VMEM((qn, cdim), F32),
                        pltpu.VMEM((qn, inner), F32), pltpu.VMEM((nh // 2, LANE, SSD_STATE), F32)],
        compiler_params=_cparams(2),
        name="ssd_prompt",
    )(z, xbc, dtr, conv_w, conv_b.reshape(1, cdim), _pad_lanes(dt_bias), _pad_lanes(a_log),
      dexp, norm_g.reshape(1, inner), _head_expand(nh, SSD_HEADDIM))
    return y, nbuf, hf.reshape(bsz, nh, SSD_HEADDIM, SSD_STATE)


def _ssd_prep_sample_kernel(xbc_ref, buf_ref, dtr_ref, cw_ref, cb_ref, dtb_ref, xc_ref, dt_ref, nbuf_ref,
                            *, width, nb, steps, nh):
    cdim = xbc_ref.shape[1]
    for cb in range(cdim // SSD_COLS):
        cols = slice(cb * SSD_COLS, (cb + 1) * SSD_COLS)
        ext = jnp.concatenate([buf_ref[:, cols], xbc_ref[:, cols]], axis=0)
        conv = ext[0:steps * nb] * cw_ref[0:1, cols]
        for k in range(1, width):
            conv = conv + ext[k * nb:(k + steps) * nb] * cw_ref[k:k + 1, cols]
        xc_ref[:, cols] = _silu(conv + cb_ref[:, cols])
        nbuf_ref[:, cols] = ext[steps * nb:]
    hl = lax.broadcasted_iota(I32, dtr_ref.shape, 1)
    dt_ref[...] = jnp.where(hl < nh, _softplus(dtr_ref[...] + dtb_ref[...]), 0.0)


def _ssd_prep_sample(xbc_tb, buf_kb, dtr_tb, conv_w, conv_b, dt_bias, nb, steps):
    rows, cdim = xbc_tb.shape
    width = conv_w.shape[0]
    return pl.pallas_call(
        functools.partial(_ssd_prep_sample_kernel, width=width, nb=nb, steps=steps, nh=dt_bias.shape[0]),
        out_shape=[jax.ShapeDtypeStruct((rows, cdim), F32),
                   jax.ShapeDtypeStruct((rows, LANE), F32),
                   jax.ShapeDtypeStruct(((width - 1) * nb, cdim), F32)],
        compiler_params=pltpu.CompilerParams(vmem_limit_bytes=VMEM_LIMIT),
        name="ssd_prep_sample",
    )(xbc_tb, buf_kb, dtr_tb, conv_w, conv_b.reshape(1, cdim), _pad_lanes(dt_bias))


def _ssd_scan_sample_kernel(xc_ref, dt_ref, z_ref, h0_ref, alog_ref, dexp_ref, ng_ref, e_ref,
                            y_ref, hf_ref, *, steps, inner):
    ngrp, ns = SSD_GROUPS, SSD_STATE
    xc, dt = xc_ref[0], dt_ref[0]
    row = lax.broadcasted_iota(I32, (QP, LANE), 0)
    cum = dt * (-jnp.exp(alog_ref[...]))
    for sft in (1, 2, 4):
        cum = cum + jnp.where(row >= sft, pltpu.roll(cum, sft, 0), 0.0)
    cum_last = cum[QP - 1:QP, :]
    parts = [dt, jnp.exp(cum), dt * jnp.exp(cum_last - cum), jnp.broadcast_to(jnp.exp(cum_last), (QP, LANE))]
    for s in range(steps):
        parts.append(jnp.where(row >= s, jnp.exp(cum - cum[s:s + 1, :]), 0.0))
    stack = jnp.concatenate(parts, axis=0)

    for g in range(ngrp):
        gl = slice(g * SSD_COLS, (g + 1) * SSD_COLS)
        ex = _dot_sel_rhs(stack, e_ref[:, gl])
        dt_x, ecum_x, dtw_x, el_x = (ex[k * QP:(k + 1) * QP] for k in range(4))
        xs_g = xc[:, gl]
        dtx = xs_g * dt_x
        bm = xc[:, inner + g * ns:inner + (g + 1) * ns]
        cm = xc[:, inner + (ngrp + g) * ns:inner + (ngrp + g + 1) * ns].astype(BF16)
        bmp = jnp.concatenate([bm, jnp.zeros((LANE - QP, ns), F32)], axis=0).astype(BF16)
        cbm = _dot_nt(cm, bmp)
        yd = jnp.zeros((QP, SSD_COLS), F32)
        for s in range(steps):
            yd = yd + (ex[(4 + s) * QP:(5 + s) * QP] * cbm[:, s:s + 1]) * dtx[s:s + 1, :]
        h0g = h0_ref[0, gl, :]
        yo = _dot_nt(cm, h0g.astype(BF16)) * ecum_x
        y = (yd + yo + dexp_ref[:, gl] * xs_g) * _silu(z_ref[0, :, gl])
        ms = jnp.mean(y * y, axis=-1, keepdims=True)
        y_ref[0, :, gl] = (y * lax.rsqrt(ms + EPS)) * ng_ref[:, gl]
        tm = jnp.concatenate([xs_g * dtw_x, el_x[0:1], jnp.zeros((LANE - QP - 1, SSD_COLS), F32)], axis=0)
        tt = tm.T
        hf_ref[0, gl, :] = h0g * tt[:, QP:QP + 1] + _dot(tt.astype(BF16), bmp)


def _ssd_scan_sample(xc, dt, z, h0, a_log, d_skip, norm_g, steps):
    nb, _, cdim = xc.shape
    inner = z.shape[-1]
    nh = a_log.shape[0]
    rows_h = nh * SSD_HEADDIM
    dexp = jnp.repeat(d_skip, SSD_HEADDIM).reshape(1, inner)
    one = lambda shape: pl.BlockSpec(shape, lambda b: (0,) * len(shape))
    return pl.pallas_call(
        functools.partial(_ssd_scan_sample_kernel, steps=steps, inner=inner),
        grid=(nb,),
        in_specs=[pl.BlockSpec((1, QP, cdim), lambda b: (b, 0, 0)),
                  pl.BlockSpec((1, QP, LANE), lambda b: (b, 0, 0)),
                  pl.BlockSpec((1, QP, inner), lambda b: (b, 0, 0)),
                  pl.BlockSpec((1, rows_h, SSD_STATE), lambda b: (b, 0, 0)),
                  one((1, LANE)), one((1, inner)), one((1, inner)), one((LANE, inner))],
        out_specs=[pl.BlockSpec((1, QP, inner), lambda b: (b, 0, 0)),
                   pl.BlockSpec((1, rows_h, SSD_STATE), lambda b: (b, 0, 0))],
        out_shape=[jax.ShapeDtypeStruct((nb, QP, inner), F32),
                   jax.ShapeDtypeStruct((nb, rows_h, SSD_STATE), F32)],
        compiler_params=_cparams(1),
        name="ssd_scan_sample",
    )(xc, dt, z, h0, _pad_lanes(a_log), dexp, norm_g.reshape(1, inner), _head_expand(nh, SSD_HEADDIM))


TM = 256


def _to_steps(a, nb, steps):
    n = a.shape[-1]
    return a.reshape(nb, steps, n).transpose(1, 0, 2).reshape(steps * nb, n)


def _to_batch(a, nb, steps):
    n = a.shape[-1]
    return a.reshape(steps, nb, n).transpose(1, 0, 2).reshape(1, nb * steps, n)


def _pad_q(a, nb, steps):
    n = a.shape[-1]
    return jnp.pad(a.reshape(nb, steps, n), ((0, 0), (0, QP - steps), (0, 0)))


def kernel(x_prompt, x_sample, cache_k, cache_v, cache_kidx, state_s5_re, state_s5_im, state_sconv, state_ssd, state_ssd_conv, page_table, c_prompt, c_sample, rel_bias, ada_w, ada_b, norm_mix, norm_mlp, norm_final, attn_w_in, attn_w_out, s5_lam_re, s5_lam_im, s5_log_dt, s5_b_re, s5_b_im, s5_c_re, s5_c_im, s5_d, s5_w_glu, sc_w_in, sc_w_conv, sc_w_out, ssd_w_in, ssd_conv_w, ssd_conv_b, ssd_dt_bias, ssd_a_log, ssd_d, ssd_norm, ssd_w_out, mlp_w1, mlp_w2):
    bp, seq, d = x_prompt.shape
    nb, steps, _ = x_sample.shape
    depth = ada_w.shape[0]
    n_mixers = 4
    rs = nb * steps
    tm = min(TM, seq)

    rows = bp + nb
    c_all = jnp.pad(jnp.concatenate([c_prompt, c_sample], axis=0), ((0, (-rows) % SUBLANE), (0, 0)))
    ada = _ada(c_all, ada_w, ada_b)

    xp = x_prompt
    xs = x_sample.reshape(1, rs, d)
    outs = {name: [] for name in ("kp", "vp", "kip", "ks", "vs", "kis", "s5pr", "s5pi", "s5sr", "s5si",
                                  "scp", "scs", "ssdp", "ssdcp", "ssds", "ssdcs")}
    nq, nkv = N_HEADS * HEAD_DIM, N_KV_HEADS * HEAD_DIM
    w1_all, w2_all = mlp_w1.astype(BF16), mlp_w2.astype(BF16)
    for i in range(depth):
        m, j = i % n_mixers, i // n_mixers
        mp = [ada[i, :bp, k * d:(k + 1) * d].reshape(bp, 1, d) for k in range(6)]
        ms = [jnp.repeat(ada[i, bp:bp + nb, k * d:(k + 1) * d], steps, axis=0).reshape(1, rs, d)
              for k in range(6)]
        g_mix = norm_mix[i]
        mix_p = mix_s = None
        if m == 0:
            w_in = attn_w_in[j]
            w_qkv = w_in[:, :nq + 2 * nkv].astype(BF16)
            n_idx = IDX_HEADS * IDX_DIM + LANE
            w_idx = jnp.pad(w_in[:, nq + 2 * nkv:], ((0, 0), (0, n_idx - (w_in.shape[1] - nq - 2 * nkv))))
            w_out = attn_w_out[j].astype(BF16)
            page = cache_k.shape[2]
            q, k, v, kw, qcat, kcat, kaug, vaug, ki = _attn_proj(xp, mp[0], mp[1], g_mix, w_qkv, w_idx,
                                                                 2 * MAX_DISTANCE, True, page)
            o = _dsa_prompt(q, qcat, kw, kcat, kaug, vaug, rel_bias, MAX_DISTANCE)
            mix_p = ("proj", o, mp[2], w_out)
            to_pages = lambda t: jnp.transpose(t.reshape(bp, seq // page, N_KV_HEADS, HEAD_DIM, page), (0, 1, 4, 2, 3))
            outs["kp"].append(to_pages(k))
            outs["vp"].append(to_pages(v))
            outs["kip"].append(jnp.transpose(ki, (0, 1, 3, 2)))
            q, k, v, kw, qcat, kcat, _, _ = _attn_proj(xs, ms[0], ms[1], g_mix, w_qkv, w_idx, rs, False)
            pps = math.gcd(PAGES_PER_STEP, page_table.shape[1])
            qc_p, kw_p = _pad_q(qcat.astype(F32), nb, steps), _pad_q(kw, nb, steps)
            scores, snew = _dsa_sample_scores(page_table, qc_p, kw_p, _pad_q(kcat.astype(F32), nb, steps),
                                              _pages_t(cache_kidx), j,
                                              math.gcd(2 * PAGES_PER_STEP, page_table.shape[1]))
            o = _dsa_sample(page_table, _pad_q(q, nb, steps), _pad_q(k, nb, steps), _pad_q(v, nb, steps),
                            scores, snew, _pages_t(cache_k), _pages_t(cache_v), rel_bias, j, steps, pps)
            mix_s = ("proj", o[:, :steps].reshape(1, rs, nq), ms[2], w_out)
            outs["ks"].append(k.reshape(nb, steps, N_KV_HEADS, HEAD_DIM))
            outs["vs"].append(v.reshape(nb, steps, N_KV_HEADS, HEAD_DIM))
            outs["kis"].append(kw[..., :IDX_DIM].reshape(nb, steps, IDX_DIM))
        elif m == 1:
            ar, ai, wb, wc = _s5_weights(s5_lam_re[j], s5_lam_im[j], s5_log_dt[j], s5_b_re[j], s5_b_im[j],
                                         s5_c_re[j], s5_c_im[j])
            w_glu = s5_w_glu[j].astype(BF16)
            grp, nst = s5_lam_re.shape[1:]
            z, fr, fi = _s5_prompt(xp, mp[0], mp[1], g_mix, s5_d[j], ar, ai, wb, wc)
            mix_p = ("glu", z, mp[2], w_glu)
            outs["s5pr"].append(fr.reshape(bp, grp, nst))
            outs["s5pi"].append(fi.reshape(bp, grp, nst))
            z, fr, fi = _s5_sample(_to_steps(xs, nb, steps), _to_steps(ms[0], nb, steps),
                                   _to_steps(ms[1], nb, steps), g_mix, s5_d[j], ar, ai, wb, wc,
                                   state_s5_re[j].reshape(nb, grp * nst), state_s5_im[j].reshape(nb, grp * nst),
                                   nb, steps)
            mix_s = ("glu", _to_batch(z, nb, steps), ms[2], w_glu)
            outs["s5sr"].append(fr.reshape(nb, grp, nst))
            outs["s5si"].append(fi.reshape(nb, grp, nst))
        elif m == 2:
            w_in = sc_w_in[j].astype(BF16)
            w_out = sc_w_out[j].astype(BF16)
            width = sc_w_conv.shape[1]
            xp, nbuf = _sconv_prompt(xp, mp[0], mp[1], g_mix, w_in, mp[2], jnp.zeros((bp, width - 1, d), F32),
                                     sc_w_conv[j], w_out, tm)
            outs["scp"].append(nbuf)
            (p,) = _proj(xs, ms[0], ms[1], g_mix, w_in, (3 * d,), rs)
            buf = state_sconv[j].transpose(1, 0, 2).reshape((width - 1) * nb, d)
            y, nbuf = _sconv_sample(_to_steps(p, nb, steps), _to_steps(xs, nb, steps), _to_steps(ms[2], nb, steps),
                                    buf, sc_w_conv[j], w_out, nb, steps)
            xs = _to_batch(y, nb, steps)
            outs["scs"].append(nbuf.reshape(width - 1, nb, d).transpose(1, 0, 2))
        else:
            inner = ssd_norm.shape[1]
            cdim = ssd_conv_w.shape[2]
            nh = ssd_dt_bias.shape[1]
            width = ssd_conv_w.shape[1]
            w_in = jnp.pad(ssd_w_in[j], ((0, 0), (0, LANE - nh))).astype(BF16)
            w_out = ssd_w_out[j].astype(BF16)
            z, xbc, dtr = _proj(xp, mp[0], mp[1], g_mix, w_in, (inner, cdim, LANE), tm)
            y, nbuf, hf = _ssd_prompt(z, xbc, dtr, ssd_conv_w[j], ssd_conv_b[j], ssd_dt_bias[j], ssd_a_log[j],
                                      ssd_d[j], ssd_norm[j])
            mix_p = ("proj", y, mp[2], w_out)
            outs["ssdp"].append(hf)
            outs["ssdcp"].append(nbuf)
            z, xbc, dtr = _proj(xs, ms[0], ms[1], g_mix, w_in, (inner, cdim, LANE), rs)
            buf = state_ssd_conv[j].transpose(1, 0, 2).reshape((width - 1) * nb, cdim)
            xc, dt, nbuf = _ssd_prep_sample(_to_steps(xbc, nb, steps), buf, _to_steps(dtr, nb, steps),
                                            ssd_conv_w[j], ssd_conv_b[j], ssd_dt_bias[j], nb, steps)
            y, hf = _ssd_scan_sample(_pad_q(_to_batch(xc, nb, steps), nb, steps),
                                     _pad_q(_to_batch(dt, nb, steps), nb, steps), _pad_q(z, nb, steps),
                                     state_ssd[j].reshape(nb, nh * SSD_HEADDIM, SSD_STATE),
                                     ssd_a_log[j], ssd_d[j], ssd_norm[j], steps)
            mix_s = ("proj", y[:, :steps].reshape(1, rs, inner), ms[2], w_out)
            outs["ssds"].append(hf.reshape(nb, nh, SSD_HEADDIM, SSD_STATE))
            outs["ssdcs"].append(nbuf.reshape(width - 1, nb, cdim).transpose(1, 0, 2))
        fin = i == depth - 1
        xp = _mlp(xp, mp[3], mp[4], mp[5], norm_mlp[i], w1_all, w2_all, i, norm_final, fin, tm, mix_p)
        xs = _mlp(xs, ms[3], ms[4], ms[5], norm_mlp[i], w1_all, w2_all, i, norm_final, fin, rs, mix_s)
    st = jnp.stack
    return (xp, xs.reshape(nb, steps, d), st(outs["kp"]), st(outs["vp"]), st(outs["kip"]),
            st(outs["ks"]), st(outs["vs"]), st(outs["kis"]), st(outs["s5pr"]), st(outs["s5pi"]),
            st(outs["s5sr"]), st(outs["s5si"]), st(outs["scp"]), st(outs["scs"]),
            st(outs["ssdp"]), st(outs["ssdcp"]), st(outs["ssds"]), st(outs["ssdcs"]))
```

```python
import functools
import math

import jax
import jax.numpy as jnp
import numpy as np
from jax import lax
from jax.experimental import pallas as pl
from jax.experimental.pallas import tpu as pltpu

F32 = jnp.float32
BF16 = jnp.bfloat16
I32 = jnp.int32

EPS = 1e-6
N_HEADS = 16
HEAD_DIM = 64
N_KV_HEADS = 4
IDX_HEADS = 8
IDX_DIM = 64
TOPK_MAX = 256
N_BUCKETS = 32
MAX_DISTANCE = 128
S5_GROUP = 16
S5_STATE = 64
SSD_HEADDIM = 64
SSD_GROUPS = 4
SSD_STATE = 128
SSD_CHUNK = 128

LANE = 128
SUBLANE = 8
VMEM_LIMIT = 56 * 1024 * 1024
NEG_BIG = -1e30
INT_MIN = -2147483648


def _cparams(n_axes):
    return pltpu.CompilerParams(dimension_semantics=("arbitrary",) * n_axes,
                                vmem_limit_bytes=VMEM_LIMIT)


def _dot(a, b):
    return jnp.dot(a, b, preferred_element_type=F32)


def _dot_nt(a, b):
    return lax.dot_general(a, b, (((1,), (1,)), ((), ())), preferred_element_type=F32)


def _split(a):
    hi = a.astype(BF16)
    lo = (a - hi.astype(F32)).astype(BF16)
    return hi, lo


def _dot3(a, b):
    ah, al = _split(a)
    bh, bl = _split(b)
    return _dot(ah, bh) + (_dot(al, bh) + _dot(ah, bl))


def _rms_mod(x, g, shift, scale):
    y = x * lax.rsqrt(jnp.mean(x * x, axis=-1, keepdims=True) + EPS)
    return (y * g) * (1.0 + scale) + shift


def _sigmoid(x):
    return 1.0 / (1.0 + jnp.exp(-x))


def _silu(x):
    return x * _sigmoid(x)


def _row_block(arr, tm):
    n = arr.shape[-1]
    if arr.shape[1] == 1:
        return pl.BlockSpec((1, 1, n), lambda b, i: (b, 0, 0))
    return pl.BlockSpec((1, tm, n), lambda b, i: (b, i, 0))


def _const2(shape):
    return pl.BlockSpec(shape, lambda b, i: (0,) * len(shape))


def _ada_kernel(c_ref, w_ref, b_ref, o_ref):
    o_ref[0] = _dot3(_silu(c_ref[...]), w_ref[0]) + b_ref[0]


def _ada(c_all, ada_w, ada_b):
    depth, d, n = ada_w.shape
    rows = c_all.shape[0]
    tn = 1536
    return pl.pallas_call(
        _ada_kernel,
        grid=(depth, n // tn),
        in_specs=[pl.BlockSpec((rows, d), lambda l, j: (0, 0)),
                  pl.BlockSpec((1, d, tn), lambda l, j: (l, 0, j)),
                  pl.BlockSpec((1, 1, tn), lambda l, j: (l, 0, j))],
        out_specs=pl.BlockSpec((1, rows, tn), lambda l, j: (l, 0, j)),
        out_shape=jax.ShapeDtypeStruct((depth, rows, n), F32),
        compiler_params=_cparams(2),
        name="ada",
    )(c_all, ada_w, ada_b.reshape(depth, 1, n))


def _proj_kernel(x_ref, sh_ref, sc_ref, g_ref, w_ref, *o_refs, splits):
    h = _rms_mod(x_ref[0], g_ref[...], sh_ref[0], sc_ref[0]).astype(BF16)
    off = 0
    for o_ref, n in zip(o_refs, splits):
        o_ref[0] = _dot(h, w_ref[:, off:off + n])
        off += n


def _proj(x, shift, scale, g, w_bf16, splits, tm):
    bsz, seq, d = x.shape
    n = w_bf16.shape[1]
    assert sum(splits) == n and seq % tm == 0
    return pl.pallas_call(
        functools.partial(_proj_kernel, splits=tuple(splits)),
        grid=(bsz, seq // tm),
        in_specs=[pl.BlockSpec((1, tm, d), lambda b, i: (b, i, 0)),
                  _row_block(shift, tm), _row_block(scale, tm),
                  _const2((1, d)), _const2((d, n))],
        out_specs=[pl.BlockSpec((1, tm, s), lambda b, i: (b, i, 0)) for s in splits],
        out_shape=[jax.ShapeDtypeStruct((bsz, seq, s), F32) for s in splits],
        compiler_params=_cparams(2),
        name="proj",
    )(x, shift, scale, g.reshape(1, d), w_bf16)


def _aug_heads(t, col):
    rows = t.shape[0]
    lane = lax.broadcasted_iota(I32, (rows, LANE - HEAD_DIM), 1)
    extra = jnp.where(lane == 0, col, 0.0).astype(F32)
    parts = []
    for g in range(N_KV_HEADS):
        parts += [t[:, g * HEAD_DIM:(g + 1) * HEAD_DIM], extra]
    return jnp.concatenate(parts, axis=1).astype(BF16)


def _attn_proj_kernel(x_ref, sh_ref, sc_ref, g_ref, w_ref, wi_ref,
                      q_ref, k_ref, v_ref, kw_ref, qcat_ref, kcat_ref, kaug_ref, vaug_ref, *ki_refs,
                      nq, nkv, nqi, key_major):
    (ki_ref,) = ki_refs if key_major else (None,)
    h = _rms_mod(x_ref[0], g_ref[...], sh_ref[0], sc_ref[0])
    hb = h.astype(BF16)
    q_ref[0] = (_dot(hb, w_ref[:, 0:nq]) * (HEAD_DIM ** -0.5)).astype(q_ref.dtype)
    k = _dot(hb, w_ref[:, nq:nq + nkv])
    v = _dot(hb, w_ref[:, nq + nkv:nq + 2 * nkv])
    r = _dot3(h, wi_ref[...])
    kw = r[:, nqi:nqi + LANE]
    kw_ref[0] = kw
    if key_major:
        page = k_ref.shape[-1]
        kt, vt, kwt = k.T, v.T, kw.T
        for p in range(k_ref.shape[1]):
            k_ref[0, p] = kt[:, p * page:(p + 1) * page]
            v_ref[0, p] = vt[:, p * page:(p + 1) * page]
            ki_ref[0, p] = kwt[0:IDX_DIM, p * page:(p + 1) * page]
    else:
        k_ref[0] = k
        v_ref[0] = v
    zero = jnp.zeros((h.shape[0], IDX_DIM), BF16)
    parts = []
    for hh in range(IDX_HEADS):
        ah, al = _split(r[:, hh * IDX_DIM:(hh + 1) * IDX_DIM])
        parts += [ah, al, ah, zero]
    qcat_ref[0] = jnp.concatenate(parts, axis=1)
    kcat = _idx_rhs(kw[:, 0:IDX_DIM])
    kaug = _aug_heads(k, 0.0)
    vaug_ref[0] = _aug_heads(v, 1.0)
    if key_major:
        kcat_ref[0, 0] = kcat.astype(F32).T.astype(BF16)
        kaug_ref[0, 0] = kaug.astype(F32).T.astype(BF16)
    else:
        kcat_ref[0] = kcat
        kaug_ref[0] = kaug


def _attn_proj(x, shift, scale, g, w_qkv_bf16, w_idx, tm, key_major, page=None):
    bsz, seq, d = x.shape
    nq = N_HEADS * HEAD_DIM
    nkv = N_KV_HEADS * HEAD_DIM
    nqi = IDX_HEADS * IDX_DIM
    rows = lambda n, dt: (pl.BlockSpec((1, tm, n), lambda b, i: (b, i, 0)), jax.ShapeDtypeStruct((bsz, seq, n), dt))
    cols = lambda n, dt: (pl.BlockSpec((1, 1, n, tm), lambda b, i: (b, i, 0, 0)),
                          jax.ShapeDtypeStruct((bsz, seq // tm, n, tm), dt))
    keys = cols if key_major else rows
    if key_major:
        ppt = tm // page
        paged = lambda n: (pl.BlockSpec((1, ppt, n, page), lambda b, i: (b, i, 0, 0)),
                           jax.ShapeDtypeStruct((bsz, seq // page, n, page), F32))
        kv = [paged(nkv), paged(nkv)]
    else:
        kv = [rows(nkv, F32), rows(nkv, F32)]
    outs = [rows(nq, BF16 if key_major else F32)] + kv + [rows(LANE, F32), rows(4 * nqi, BF16),
                                   keys(4 * IDX_DIM, BF16), keys(N_KV_HEADS * LANE, BF16),
                                   rows(N_KV_HEADS * LANE, BF16)]
    if key_major:
        outs.append(paged(IDX_DIM))
    return pl.pallas_call(
        functools.partial(_attn_proj_kernel, nq=nq, nkv=nkv, nqi=nqi, key_major=key_major),
        grid=(bsz, seq // tm),
        in_specs=[pl.BlockSpec((1, tm, d), lambda b, i: (b, i, 0)),
                  _row_block(shift, tm), _row_block(scale, tm),
                  _const2((1, d)), _const2(w_qkv_bf16.shape), _const2(w_idx.shape)],
        out_specs=[o[0] for o in outs],
        out_shape=[o[1] for o in outs],
        compiler_params=_cparams(2),
        name="attn_proj",
    )(x, shift, scale, g.reshape(1, d), w_qkv_bf16, w_idx)


def _mlp_kernel(*refs, ff_chunk, final_norm, mixer_out):
    if mixer_out is None:
        x_ref, sh_ref, sc_ref, gate_ref, g_ref, w1_ref, w2_ref, gf_ref, y_ref = refs
        x = x_ref[0]
    else:
        o_ref, gmix_ref, wo_ref, x_ref, sh_ref, sc_ref, gate_ref, g_ref, w1_ref, w2_ref, gf_ref, y_ref = refs
        d = x_ref.shape[-1]
        ob = o_ref[0].astype(BF16)
        if mixer_out == "glu":
            t = _dot(ob, wo_ref[:, :d]) * _sigmoid(_dot(ob, wo_ref[:, d:]))
        else:
            t = _dot(ob, wo_ref[...])
        x = x_ref[0] + gmix_ref[0] * t
    h = _rms_mod(x, g_ref[...], sh_ref[0], sc_ref[0]).astype(BF16)
    dff = w1_ref.shape[-1]
    acc = jnp.zeros(x.shape, F32)
    for c in range(dff // ff_chunk):
        a = jnp.maximum(_dot(h, w1_ref[0, :, c * ff_chunk:(c + 1) * ff_chunk]), 0.0)
        acc = acc + _dot((a * a).astype(BF16), w2_ref[0, c * ff_chunk:(c + 1) * ff_chunk, :])
    y = x + gate_ref[0] * acc
    if final_norm:
        y = (y * lax.rsqrt(jnp.mean(y * y, axis=-1, keepdims=True) + EPS)) * gf_ref[...]
    y_ref[0] = y


def _resident(shape):
    return pl.BlockSpec(shape, lambda b, i: (0,) * len(shape), pipeline_mode=pl.Buffered(1))


def _mlp(x, shift, scale, gate, g, w1_bf16, w2_bf16, layer, g_final, final_norm, tm, mixer=None):
    bsz, seq, d = x.shape
    dff = w1_bf16.shape[-1]
    layer_w = lambda r, c: pl.BlockSpec((1, r, c), lambda b, i: (layer, 0, 0), pipeline_mode=pl.Buffered(1))
    rows = lambda n: pl.BlockSpec((1, tm, n), lambda b, i: (b, i, 0))
    args, specs, kind = [], [], None
    if mixer is not None:
        kind, o, gmix, wo = mixer
        args += [o, gmix, wo]
        specs += [rows(o.shape[-1]), _row_block(gmix, tm), _resident(wo.shape)]
    args += [x, shift, scale, gate, g.reshape(1, d), w1_bf16, w2_bf16, g_final.reshape(1, d)]
    specs += [rows(d), _row_block(shift, tm), _row_block(scale, tm), _row_block(gate, tm),
              _const2((1, d)), layer_w(d, dff), layer_w(dff, d), _const2((1, d))]
    return pl.pallas_call(
        functools.partial(_mlp_kernel, ff_chunk=1024, final_norm=final_norm, mixer_out=kind),
        grid=(bsz, seq // tm),
        in_specs=specs,
        out_specs=rows(d),
        out_shape=jax.ShapeDtypeStruct((bsz, seq, d), F32),
        compiler_params=_cparams(2),
        name="mlp",
    )(*args)


S5_CH = 128
S5_BLK = (S5_CH // S5_GROUP) * S5_STATE


def _s5_disc_kernel(lr_ref, li_ref, ldt_ref, br_ref, bi_ref, ar_ref, ai_ref, bbr_ref, bbi_ref):
    lr, li = lr_ref[...], li_ref[...]
    dt = jnp.exp(ldt_ref[...])
    mag = jnp.exp(lr * dt)
    ab_re, ab_im = mag * jnp.cos(li * dt), mag * jnp.sin(li * dt)
    den = lr * lr + li * li
    nr = ab_re - 1.0
    f_re = (nr * lr + ab_im * li) / den
    f_im = (ab_im * lr - nr * li) / den
    ar_ref[...] = ab_re
    ai_ref[...] = ab_im
    for c in range(br_ref.shape[0]):
        br, bi = br_ref[c], bi_ref[c]
        bbr_ref[c] = f_re * br - f_im * bi
        bbi_ref[c] = f_re * bi + f_im * br


def _s5_weights(lam_re, lam_im, log_dt, b_re, b_im, c_re, c_im):
    g, p = lam_re.shape
    gc = b_re.shape[-1]
    brt = jnp.moveaxis(b_re, 2, 0)
    bit = jnp.moveaxis(b_im, 2, 0)
    ar, ai, bbr, bbi = pl.pallas_call(
        _s5_disc_kernel,
        out_shape=[jax.ShapeDtypeStruct((g, p), F32), jax.ShapeDtypeStruct((g, p), F32),
                   jax.ShapeDtypeStruct((gc, g, p), F32), jax.ShapeDtypeStruct((gc, g, p), F32)],
        name="s5_disc",
    )(lam_re, lam_im, log_dt.reshape(g, 1), brt, bit)
    nblk = (g * gc) // S5_CH
    gpb = g // nblk
    eye = jnp.eye(gpb, dtype=F32)

    def bd_in(bb):
        t = jnp.moveaxis(bb, 0, 1).reshape(nblk, gpb, gc, p)
        return jnp.einsum('ngcp,gh->ngchp', t, eye).reshape(nblk, gpb * gc, gpb * p)

    def bd_out(cc):
        t = cc.reshape(nblk, gpb, gc, p)
        return jnp.einsum('ngcp,gh->ngphc', t, eye).reshape(nblk, gpb * p, gpb * gc)

    wb = jnp.concatenate([bd_in(bbr), bd_in(bbi)], axis=-1).astype(BF16)
    wc = jnp.concatenate([bd_out(c_re), bd_out(c_im)], axis=1).astype(BF16)
    return ar.reshape(nblk, 1, gpb * p), ai.reshape(nblk, 1, gpb * p), wb, wc


def _gelu_tanh(y):
    return 0.5 * y * (1.0 + jnp.tanh(math.sqrt(2.0 / math.pi) * (y + 0.044715 * (y * y * y))))


def _cmul(ar, ai, xr, xi):
    return ar * xr - ai * xi, ar * xi + ai * xr


def _s5_prompt_kernel(x_ref, sh_ref, sc_ref, g_ref, d_ref, ar_ref, ai_ref, wb_ref, wc_ref,
                      perm_ref, unperm_ref, z_ref, fr_ref, fi_ref, u_s, br_s, bi_s, st_r, st_i,
                      *, nseg, seg):
    i = pl.program_id(1)
    nblk = wb_ref.shape[0]
    blk = ar_ref.shape[-1]

    @pl.when(i == 0)
    def _():
        st_r[...] = jnp.zeros(st_r.shape, F32)
        st_i[...] = jnp.zeros(st_i.shape, F32)

    u_s[...] = _dot_sel_lhs(perm_ref[...], _rms_mod(x_ref[0], g_ref[...], sh_ref[0], sc_ref[0]))

    for c in range(nblk):
        lo, hi = c * S5_CH, (c + 1) * S5_CH
        uc = u_s[:, lo:hi]
        bu = _dot(uc.astype(BF16), wb_ref[c])
        br_s[...] = bu[:, :blk]
        bi_s[...] = bu[:, blk:]
        ar = jnp.broadcast_to(ar_ref[c], (nseg, blk))
        ai = jnp.broadcast_to(ai_ref[c], (nseg, blk))

        def local(j, carry):
            xr, xi = carry
            r0 = pl.multiple_of(j * nseg, nseg)
            pr, pi = _cmul(ar, ai, xr, xi)
            nr = pr + br_s[pl.ds(r0, nseg), :]
            ni = pi + bi_s[pl.ds(r0, nseg), :]
            br_s[pl.ds(r0, nseg), :] = nr
            bi_s[pl.ds(r0, nseg), :] = ni
            return nr, ni

        zero = jnp.zeros((nseg, blk), F32)
        fr, fi = lax.fori_loop(0, seg, local, (zero, zero), unroll=True)

        pr, pi = ar_ref[c], ai_ref[c]
        for _ in range(int(math.log2(seg))):
            pr, pi = _cmul(pr, pi, pr, pi)
        cr, ci = st_r[c], st_i[c]
        rows_r, rows_i = [], []
        for s in range(nseg):
            rows_r.append(cr)
            rows_i.append(ci)
            tr, ti = _cmul(pr, pi, cr, ci)
            cr, ci = tr + fr[s:s + 1], ti + fi[s:s + 1]
        st_r[c] = cr
        st_i[c] = ci
        fr_ref[0, c] = cr
        fi_ref[0, c] = ci
        dr, di = _cmul(ar, ai, jnp.concatenate(rows_r, axis=0), jnp.concatenate(rows_i, axis=0))

        def fix(j, carry):
            dr, di = carry
            r0 = pl.multiple_of(j * nseg, nseg)
            br_s[pl.ds(r0, nseg), :] = br_s[pl.ds(r0, nseg), :] + dr
            bi_s[pl.ds(r0, nseg), :] = bi_s[pl.ds(r0, nseg), :] + di
            return _cmul(ar, ai, dr, di)

        lax.fori_loop(0, seg, fix, (dr, di), unroll=True)

        y = (_dot(br_s[...].astype(BF16), wc_ref[c, :blk, :])
             - _dot(bi_s[...].astype(BF16), wc_ref[c, blk:, :]))
        y = y + d_ref[:, lo:hi] * uc
        u_s[:, lo:hi] = _gelu_tanh(y)

    z_ref[0] = _dot_sel_lhs(unperm_ref[...], u_s[...]).astype(z_ref.dtype)


def _s5_prompt(x, shift, scale, g, d_skip, ar, ai, wb, wc):
    bsz, seq, d = x.shape
    nblk, _, blk = ar.shape
    nseg, seg = SUBLANE, 32
    tm = nseg * seg
    assert seq % tm == 0
    perm = np.zeros((tm, tm), np.float32)
    for s in range(nseg):
        for j in range(seg):
            perm[j * nseg + s, s * seg + j] = 1.0
    unperm = jnp.asarray(perm.T, BF16)
    perm = jnp.asarray(perm, BF16)
    z, fr, fi = pl.pallas_call(
        functools.partial(_s5_prompt_kernel, nseg=nseg, seg=seg),
        grid=(bsz, seq // tm),
        in_specs=[pl.BlockSpec((1, tm, d), lambda b, i: (b, i, 0)),
                  _row_block(shift, tm), _row_block(scale, tm),
                  _const2((1, d)), _const2((1, d)),
                  _const2(ar.shape), _const2(ai.shape), _const2(wb.shape), _const2(wc.shape),
                  _const2((tm, tm)), _const2((tm, tm))],
        out_specs=[pl.BlockSpec((1, tm, d), lambda b, i: (b, i, 0)),
                   pl.BlockSpec((1, nblk, 1, blk), lambda b, i: (b, 0, 0, 0)),
                   pl.BlockSpec((1, nblk, 1, blk), lambda b, i: (b, 0, 0, 0))],
        out_shape=[jax.ShapeDtypeStruct((bsz, seq, d), BF16),
                   jax.ShapeDtypeStruct((bsz, nblk, 1, blk), F32),
                   jax.ShapeDtypeStruct((bsz, nblk, 1, blk), F32)],
        scratch_shapes=[pltpu.VMEM((tm, d), F32), pltpu.VMEM((tm, blk), F32), pltpu.VMEM((tm, blk), F32),
                        pltpu.VMEM((nblk, 1, blk), F32), pltpu.VMEM((nblk, 1, blk), F32)],
        compiler_params=_cparams(2),
        name="s5_prompt",
    )(x, shift, scale, g.reshape(1, d), d_skip.reshape(1, d), ar, ai, wb, wc, perm, unperm)
    return z, fr, fi


def _s5_sample_kernel(x_ref, sh_ref, sc_ref, g_ref, d_ref, ar_ref, ai_ref, wb_ref, wc_ref,
                      s0r_ref, s0i_ref, z_ref, fr_ref, fi_ref, *, nb, steps):
    nblk = wb_ref.shape[0]
    blk = ar_ref.shape[-1]
    u = _rms_mod(x_ref[...], g_ref[...], sh_ref[...], sc_ref[...])
    for c in range(nblk):
        lo, hi = c * S5_CH, (c + 1) * S5_CH
        uc = u[:, lo:hi]
        bu = _dot(uc.astype(BF16), wb_ref[c])
        ar = jnp.broadcast_to(ar_ref[c], (nb, blk))
        ai = jnp.broadcast_to(ai_ref[c], (nb, blk))
        xr, xi = s0r_ref[:, c * blk:(c + 1) * blk], s0i_ref[:, c * blk:(c + 1) * blk]
        xrs, xis = [], []
        for t in range(steps):
            pr, pi = _cmul(ar, ai, xr, xi)
            xr = pr + bu[t * nb:(t + 1) * nb, :blk]
            xi = pi + bu[t * nb:(t + 1) * nb, blk:]
            xrs.append(xr)
            xis.append(xi)
        fr_ref[:, c * blk:(c + 1) * blk] = xr
        fi_ref[:, c * blk:(c + 1) * blk] = xi
        y = (_dot(jnp.concatenate(xrs, axis=0).astype(BF16), wc_ref[c, :blk, :])
             - _dot(jnp.concatenate(xis, axis=0).astype(BF16), wc_ref[c, blk:, :]))
        y = y + d_ref[:, lo:hi] * uc
        z_ref[:, lo:hi] = _gelu_tanh(y)


def _s5_sample(x_tb, shift_tb, scale_tb, g, d_skip, ar, ai, wb, wc, s0r, s0i, nb, steps):
    rows, d = x_tb.shape
    nblk, _, blk = ar.shape
    return pl.pallas_call(
        functools.partial(_s5_sample_kernel, nb=nb, steps=steps),
        out_shape=[jax.ShapeDtypeStruct((rows, d), F32),
                   jax.ShapeDtypeStruct((nb, nblk * blk), F32),
                   jax.ShapeDtypeStruct((nb, nblk * blk), F32)],
        compiler_params=pltpu.CompilerParams(vmem_limit_bytes=VMEM_LIMIT),
        name="s5_sample",
    )(x_tb, shift_tb, scale_tb, g.reshape(1, d), d_skip.reshape(1, d), ar, ai, wb, wc, s0r, s0i)


def _shift_rows(u, prev, k):
    if k == 0:
        return u
    rolled = pltpu.roll(u, k, 0)
    head = pltpu.roll(prev, k, 0)
    row = lax.broadcasted_iota(I32, (SUBLANE, u.shape[1]), 0)
    first = jnp.where(row < k, head, rolled[:SUBLANE])
    return jnp.concatenate([first, rolled[SUBLANE:]], axis=0)


def _sconv_prompt_kernel(x_ref, sh_ref, sc_ref, g_ref, wi_ref, gate_ref, buf_ref, wc_ref, wo_ref,
                         y_ref, nb_ref, prev_s, *, width):
    i = pl.program_id(1)
    d = x_ref.shape[-1]

    @pl.when(i == 0)
    def _():
        prev_s[...] = jnp.zeros(prev_s.shape, F32)
        prev_s[SUBLANE - (width - 1):, :] = buf_ref[0]

    h = _rms_mod(x_ref[0], g_ref[...], sh_ref[0], sc_ref[0]).astype(BF16)
    gb = _dot(h, wi_ref[:, :d])
    u = _dot(h, wi_ref[:, d:2 * d]) * _dot(h, wi_ref[:, 2 * d:])
    prev = prev_s[...]
    conv = u * wc_ref[width - 1:width, :]
    for k in range(1, width):
        conv = conv + _shift_rows(u, prev, k) * wc_ref[width - 1 - k:width - k, :]
    prev_s[...] = u[u.shape[0] - SUBLANE:, :]
    nb_ref[0] = u[u.shape[0] - (width - 1):, :]
    y_ref[0] = x_ref[0] + gate_ref[0] * _dot((gb * conv).astype(BF16), wo_ref[...])


def _sconv_prompt(x, shift, scale, g, w_in_bf16, gate, buf, w_conv, w_out_bf16, tm):
    bsz, seq, d = x.shape
    width = w_conv.shape[0]
    return pl.pallas_call(
        functools.partial(_sconv_prompt_kernel, width=width),
        grid=(bsz, seq // tm),
        in_specs=[pl.BlockSpec((1, tm, d), lambda b, i: (b, i, 0)),
                  _row_block(shift, tm), _row_block(scale, tm), _const2((1, d)),
                  _resident(w_in_bf16.shape), _row_block(gate, tm),
                  pl.BlockSpec((1, width - 1, d), lambda b, i: (b, 0, 0)),
                  _const2((width, d)), _resident((d, d))],
        out_specs=[pl.BlockSpec((1, tm, d), lambda b, i: (b, i, 0)),
                   pl.BlockSpec((1, width - 1, d), lambda b, i: (b, 0, 0))],
        out_shape=[jax.ShapeDtypeStruct((bsz, seq, d), F32),
                   jax.ShapeDtypeStruct((bsz, width - 1, d), F32)],
        scratch_shapes=[pltpu.VMEM((SUBLANE, d), F32)],
        compiler_params=_cparams(2),
        name="sconv_prompt",
    )(x, shift, scale, g.reshape(1, d), w_in_bf16, gate, buf, w_conv, w_out_bf16)


def _sconv_sample_kernel(p_ref, x_ref, gate_ref, buf_ref, wc_ref, wo_ref, y_ref, nb_ref,
                         *, width, nb, steps):
    d = x_ref.shape[-1]
    p = p_ref[...]
    gb, gc, xh = p[:, :d], p[:, d:2 * d], p[:, 2 * d:]
    u = gc * xh
    ext = jnp.concatenate([buf_ref[...], u], axis=0)
    conv = ext[0:steps * nb] * wc_ref[0:1, :]
    for k in range(1, width):
        conv = conv + ext[k * nb:(k + steps) * nb] * wc_ref[k:k + 1, :]
    nb_ref[...] = ext[steps * nb:]
    y_ref[...] = x_ref[...] + gate_ref[...] * _dot((gb * conv).astype(BF16), wo_ref[...])


def _sconv_sample(p_tb, x_tb, gate_tb, buf_kb, w_conv, w_out_bf16, nb, steps):
    rows, d = x_tb.shape
    width = w_conv.shape[0]
    return pl.pallas_call(
        functools.partial(_sconv_sample_kernel, width=width, nb=nb, steps=steps),
        out_shape=[jax.ShapeDtypeStruct((rows, d), F32),
                   jax.ShapeDtypeStruct(((width - 1) * nb, d), F32)],
        compiler_params=pltpu.CompilerParams(vmem_limit_bytes=VMEM_LIMIT),
        name="sconv_sample",
    )(p_tb, x_tb, gate_tb, buf_kb, w_conv, w_out_bf16)


KEY_NEG_INF = -2139095041
KEY_POS_INF = 2139095040


def _code_to_f32(key):
    key = jnp.clip(key, KEY_NEG_INF, KEY_POS_INF)
    return lax.bitcast_convert_type(key ^ ((key >> 31) & 0x7FFFFFFF), F32)


def _stack_heads(qcat):
    w = 4 * IDX_DIM
    return jnp.concatenate([qcat[:, h * w:(h + 1) * w] for h in range(IDX_HEADS)], axis=0)


def _idx_rhs(ki):
    kh, kl = _split(ki)
    return jnp.concatenate([kh, kh, kl, jnp.zeros(kh.shape, BF16)], axis=1)


def _idx_score(dts, wcols, nrows):
    s = wcols[0] * jnp.maximum(dts[0:nrows], 0.0)
    for h in range(1, IDX_HEADS):
        s = s + wcols[h] * jnp.maximum(dts[h * nrows:(h + 1) * nrows], 0.0)
    return s


def _idx_wcols(kw):
    sc = (IDX_HEADS ** -0.5) * (IDX_DIM ** -0.5)
    return [kw[:, IDX_DIM + h:IDX_DIM + h + 1] * sc for h in range(IDX_HEADS)]


def _topk_select(count, rows, lanes, topk, idx_bits, stash=lambda v: (lambda: v)):
    wide = lambda v: jnp.broadcast_to(v, (rows, lanes))

    def bit_step(it, tu):
        mask = jnp.left_shift(jnp.int32(1), 31 - it)
        cand_u = tu | mask
        cand = stash(wide(_code_to_f32(cand_u ^ INT_MIN)))
        cnt = count(lambda c, sc, idx: jnp.where(sc >= cand(), 1.0, 0.0))
        return jnp.where(cnt >= topk, cand_u, tu)

    t = _code_to_f32(lax.fori_loop(0, 32, bit_step, jnp.zeros((rows, 1), I32)) ^ INT_MIN)
    tw = wide(t)
    n_gt = count(lambda c, sc, idx: jnp.where(sc > tw, 1.0, 0.0))
    n_ge = count(lambda c, sc, idx: jnp.where(sc >= tw, 1.0, 0.0))
    need = topk - n_gt
    tied = jnp.max(jnp.where(t > -jnp.inf, n_ge - topk, 0.0)) > 0.0

    def tie_search(_):
        def idx_step(it, j):
            cand = j | jnp.left_shift(jnp.int32(1), idx_bits - 1 - it)
            cw = wide(cand)
            cnt = count(lambda c, sc, idx: jnp.where((sc == tw) & (idx < cw), 1.0, 0.0))
            return jnp.where(cnt < need, cand, j)
        return lax.fori_loop(0, idx_bits, idx_step, jnp.zeros((rows, 1), I32))

    j = lax.cond(tied, tie_search, lambda _: jnp.full((rows, 1), 2 ** idx_bits, I32), 0)
    return t, j


def _dsa_prompt_kernel(q_ref, qcat_ref, kwq_ref, kcat_ref, kaug_ref, vaug_ref, bias_ref, o_ref,
                       sc_s, madd_s, wbc_s, cand_s, qa_s, lg_s, mx_s, sh_s, acc_s, *, tq, topk, idx_bits):
    i = pl.program_id(1)
    nk = i + 1
    nk2 = (nk + 1) // 2
    r = N_HEADS // N_KV_HEADS

    for h, w in enumerate(_idx_wcols(kwq_ref[0])):
        wbc_s[h] = jnp.broadcast_to(w, (tq, tq))
    qrow = lax.broadcasted_iota(I32, (tq, tq), 0)
    kcol = lax.broadcasted_iota(I32, (tq, tq), 1)
    qpos = i * tq + qrow
    wq = 4 * IDX_DIM

    def score_pair(c2, _):
        kc = kcat_ref[0, c2]
        s = None
        hpd = 4
        for hp in range(IDX_HEADS // hpd):
            lhs = jnp.concatenate([qcat_ref[0, :, (hpd * hp + v) * wq:(hpd * hp + v + 1) * wq] for v in range(hpd)],
                                  axis=0)
            dd = jnp.maximum(_dot(lhs, kc), 0.0)
            for v in range(hpd):
                w = wbc_s[hpd * hp + v]
                d = jnp.concatenate([w, w], axis=1) * dd[v * tq:(v + 1) * tq]
                s = d if s is None else s + d
        for u in range(2):
            c = 2 * c2 + u
            sc_s[c] = jnp.where(c * tq + kcol <= qpos, s[:, u * tq:(u + 1) * tq], -jnp.inf)
        return 0

    lax.fori_loop(0, nk2, score_pair, 0)

    def count(fn):
        def body(c2, acc):
            c = 2 * c2
            acc = acc + fn(c, sc_s[c], c * tq + kcol)
            return acc + fn(c + 1, sc_s[c + 1], (c + 1) * tq + kcol)
        acc = lax.fori_loop(0, nk2, body, jnp.zeros((tq, tq), F32))
        return jnp.sum(acc, axis=1, keepdims=True)

    def stash(v):
        cand_s[...] = v
        return lambda: cand_s[...]

    t, j = _topk_select(count, tq, tq, float(topk), idx_bits, stash)
    tw = jnp.broadcast_to(t, (tq, tq))
    jw = jnp.broadcast_to(j, (tq, tq))

    def mask_pair(c2, _):
        for u in range(2):
            c = 2 * c2 + u
            sc = sc_s[c]
            sel = (sc > tw) | ((sc == tw) & (c * tq + kcol <= jw))
            madd_s[c] = jnp.where(sel & (sc > -jnp.inf), 0.0, NEG_BIG)
        return 0

    lax.fori_loop(0, nk2, mask_pair, 0)

    q = q_ref[0].astype(F32)
    zpad = jnp.zeros((tq, LANE - HEAD_DIM), F32)
    for g in range(N_KV_HEADS):
        qa_s[g] = jnp.concatenate(
            [jnp.concatenate([q[:, (g * r + u) * HEAD_DIM:(g * r + u + 1) * HEAD_DIM], zpad], axis=1)
             for u in range(r)], axis=0).astype(BF16)

    def pair_mask(c2, near, g):
        halves = []
        for u in range(2):
            c = 2 * c2 + u
            m = madd_s[c]
            if near:
                back = nk - 1 - c
                carries = (back == 0) | ((back == 1) & (kcol > qrow))
                halves.append([m + jnp.where(carries, bias_ref[g * r + v], 0.0) for v in range(r)])
            else:
                halves.append([m] * r)
        return jnp.concatenate([jnp.concatenate([halves[0][v], halves[1][v]], axis=1) for v in range(r)], axis=0)

    ngh = lg_s.shape[0]
    for half in range(N_KV_HEADS // ngh):
        groups = [half * ngh + k for k in range(ngh)]
        mx_s[...] = jnp.full(mx_s.shape, NEG_BIG, F32)

        def logits_pair(c2, near):
            for k, g in enumerate(groups):
                lg = _dot(qa_s[g], kaug_ref[0, c2, g * LANE:(g + 1) * LANE, :]) + pair_mask(c2, near, g)
                lg_s[k, c2] = lg
                mx_s[k] = jnp.maximum(mx_s[k], jnp.maximum(lg[:, :tq], lg[:, tq:]))

        def far(c4, _):
            logits_pair(2 * c4, False)
            logits_pair(2 * c4 + 1, False)
            return 0

        nfar = jnp.maximum(nk2 - 2, 0)
        lax.fori_loop(0, nfar // 2, far, 0)

        @pl.when(nfar % 2 == 1)
        def _():
            logits_pair(nfar - 1, False)

        @pl.when(nk2 >= 2)
        def _():
            logits_pair(nk2 - 2, True)

        logits_pair(nk2 - 1, True)

        for k in range(ngh):
            sh_s[k] = jnp.broadcast_to(jnp.max(mx_s[k], axis=1, keepdims=True), (r * tq, LANE))
        acc_s[...] = jnp.zeros(acc_s.shape, F32)

        def pv_pair(c2):
            r0 = pl.multiple_of(c2 * 2 * tq, 2 * tq)
            va = vaug_ref[0, pl.ds(r0, 2 * tq), :]
            for k, g in enumerate(groups):
                s = sh_s[k]
                p = jnp.exp(lg_s[k, c2] - jnp.concatenate([s, s], axis=1))
                acc_s[k] = acc_s[k] + _dot(p.astype(BF16), va[:, g * LANE:(g + 1) * LANE])

        def pv_two(c4, _):
            pv_pair(2 * c4)
            pv_pair(2 * c4 + 1)
            return 0

        lax.fori_loop(0, nk2 // 2, pv_two, 0)

        @pl.when(nk2 % 2 == 1)
        def _():
            pv_pair(nk2 - 1)
        for k, g in enumerate(groups):
            acc = acc_s[k]
            o = acc[:, 0:HEAD_DIM] / acc[:, HEAD_DIM:HEAD_DIM + 1]
            heads = jnp.concatenate([o[u * tq:(u + 1) * tq, :] for u in range(r)], axis=1)
            o_ref[0, :, g * r * HEAD_DIM:(g + 1) * r * HEAD_DIM] = heads.astype(o_ref.dtype)


def _t5_bucket(dist):
    n = jnp.maximum(dist, 0)
    max_exact = N_BUCKETS // 2
    nf = jnp.maximum(n, max_exact).astype(F32)
    large = max_exact + (jnp.log(nf / max_exact) / math.log(MAX_DISTANCE / max_exact)
                         * (N_BUCKETS - max_exact)).astype(I32)
    large = jnp.minimum(large, N_BUCKETS - 1)
    return jnp.where(n < max_exact, n, large)


def _bias_table(rel_bias, n):
    tab = rel_bias[_t5_bucket(jnp.arange(n, dtype=I32))]
    return (tab - rel_bias[N_BUCKETS - 1][None, :]).T


DSA_GROUPS_PER_PASS = 2


def _dsa_prompt(q, qcat, kw, kcat, kaug, vaug, rel_bias, tq):
    bsz, seq, nq = q.shape
    topk = min(TOPK_MAX, seq // 4)
    nkc = seq // tq
    r = N_HEADS // N_KV_HEADS
    assert tq == MAX_DISTANCE and seq % (2 * tq) == 0
    tab = _bias_table(rel_bias, tq)
    dmod = (np.arange(tq)[:, None] - np.arange(tq)[None, :]) % tq
    onehot = (jnp.arange(tq, dtype=I32)[:, None, None] == jnp.asarray(dmod, I32)[None]).astype(F32)
    bias = jnp.einsum('hd,dqk->hqk', tab, onehot, precision=lax.Precision.HIGHEST)
    assert kcat.shape[1:] == (seq // (2 * tq), 4 * IDX_DIM, 2 * tq)
    row = lambda n: pl.BlockSpec((1, tq, n), lambda b, i: (b, i, 0))
    tiles = lambda a: pl.BlockSpec((1,) + a.shape[1:], lambda b, i: (b, 0, 0, 0), pipeline_mode=pl.Buffered(1))
    return pl.pallas_call(
        functools.partial(_dsa_prompt_kernel, tq=tq, topk=topk, idx_bits=int(math.log2(seq))),
        grid=(bsz, nkc),
        in_specs=[row(nq), row(qcat.shape[-1]), row(LANE), tiles(kcat), tiles(kaug),
                  pl.BlockSpec((1, seq, vaug.shape[-1]), lambda b, i: (b, 0, 0), pipeline_mode=pl.Buffered(1)),
                  _resident((N_HEADS, tq, tq))],
        out_specs=row(nq),
        out_shape=jax.ShapeDtypeStruct((bsz, seq, nq), BF16),
        scratch_shapes=[pltpu.VMEM((nkc, tq, tq), F32), pltpu.VMEM((nkc, tq, tq), F32),
                        pltpu.VMEM((IDX_HEADS, tq, tq), F32), pltpu.VMEM((tq, tq), F32),
                        pltpu.VMEM((N_KV_HEADS, r * tq, LANE), BF16),
                        pltpu.VMEM((DSA_GROUPS_PER_PASS, nkc // 2, r * tq, 2 * tq), F32),
                        pltpu.VMEM((DSA_GROUPS_PER_PASS, r * tq, tq), F32),
                        pltpu.VMEM((DSA_GROUPS_PER_PASS, r * tq, LANE), F32),
                        pltpu.VMEM((DSA_GROUPS_PER_PASS, r * tq, LANE), F32)],
        compiler_params=_cparams(2),
        name="dsa_prompt",
    )(q, qcat, kw, kcat, kaug, vaug, bias)


QP = SUBLANE
PAGES_PER_STEP = 64


def _page_specs(shape4, layer, pps):
    def spec(u):
        return pl.BlockSpec((1, 1) + tuple(shape4[2:]),
                            lambda b, g, pt: (layer, pt[b, g * pps + u], 0, 0))
    return [spec(u) for u in range(pps)]


def _pages_t(cache):
    nd = cache.ndim
    t = jnp.transpose(cache, (0, 1) + tuple(range(3, nd)) + (2,))
    return t.reshape(t.shape[0], t.shape[1], -1, t.shape[-1])


def _pad_t(a, page):
    return jnp.concatenate([a, jnp.zeros((page - QP, a.shape[1]), F32)], axis=0).T


def _dsa_sample_score_kernel(pt_ref, qcat_ref, kw_ref, kcn_ref, *rest, pps):
    page_refs, o_ref, on_ref = rest[:pps], rest[pps], rest[pps + 1]
    lhs = _stack_heads(qcat_ref[0]).astype(BF16)
    wcols = _idx_wcols(kw_ref[0])
    kt = jnp.concatenate([page_refs[u][0, 0] for u in range(pps)], axis=1)
    kh, kl = _split(kt)
    rhs = jnp.concatenate([kh, kh, kl, jnp.zeros(kh.shape, BF16)], axis=0)
    o_ref[0] = _idx_score(_dot(lhs, rhs), wcols, QP)

    @pl.when(pl.program_id(1) == pl.num_programs(1) - 1)
    def _():
        on_ref[0] = _idx_score(_dot(lhs, _pad_t(kcn_ref[0], on_ref.shape[-1]).astype(BF16)), wcols, QP)


def _dsa_sample_scores(page_table, qcat, kw, kcn, kidx_t, layer, pps):
    nb, n_pages = page_table.shape
    page = kidx_t.shape[-1]
    bmap = lambda b, g, pt: (b, 0, 0)
    grid_spec = pltpu.PrefetchScalarGridSpec(
        num_scalar_prefetch=1,
        grid=(nb, n_pages // pps),
        in_specs=[pl.BlockSpec((1, QP, qcat.shape[-1]), bmap), pl.BlockSpec((1, QP, LANE), bmap),
                  pl.BlockSpec((1, QP, kcn.shape[-1]), bmap)] + _page_specs(kidx_t.shape, layer, pps),
        out_specs=[pl.BlockSpec((1, QP, pps * page), lambda b, g, pt: (b, 0, g)),
                   pl.BlockSpec((1, QP, page), bmap)])
    return pl.pallas_call(
        functools.partial(_dsa_sample_score_kernel, pps=pps),
        grid_spec=grid_spec,
        out_shape=[jax.ShapeDtypeStruct((nb, QP, n_pages * page), F32),
                   jax.ShapeDtypeStruct((nb, QP, page), F32)],
        compiler_params=_cparams(2),
        name="dsa_sample_scores",
    )(page_table, qcat, kw, kcn, *([kidx_t] * pps))


def _new_token_scores(s_new, steps):
    lane = lax.broadcasted_iota(I32, s_new.shape, 1)
    q = lax.broadcasted_iota(I32, s_new.shape, 0) % QP
    return jnp.where((lane <= q) & (lane < steps), s_new, -jnp.inf)


def _dsa_sample_select_kernel(sc_ref, sn_ref, t_ref, j_ref, *, steps, topk, idx_bits):
    scores = jnp.concatenate([sc_ref[...], _new_token_scores(sn_ref[...], steps)], axis=1)
    idx = lax.broadcasted_iota(I32, scores.shape, 1)
    count = lambda fn: jnp.sum(fn(0, scores, idx), axis=1, keepdims=True)
    t, j = _topk_select(count, scores.shape[0], scores.shape[1], float(topk), idx_bits)
    t_ref[...] = t
    j_ref[...] = j


def _dsa_sample_select(scores, snew, steps, topk):
    nb, _, past = scores.shape
    page = snew.shape[-1]
    rows = nb * QP
    rb = math.gcd(rows, 256)
    t, j = pl.pallas_call(
        functools.partial(_dsa_sample_select_kernel, steps=steps, topk=topk,
                          idx_bits=int(math.ceil(math.log2(past + page)))),
        grid=(rows // rb,),
        in_specs=[pl.BlockSpec((rb, past), lambda i: (i, 0)), pl.BlockSpec((rb, page), lambda i: (i, 0))],
        out_specs=[pl.BlockSpec((rb, 1), lambda i: (i, 0)), pl.BlockSpec((rb, 1), lambda i: (i, 0))],
        out_shape=[jax.ShapeDtypeStruct((rows, 1), F32), jax.ShapeDtypeStruct((rows, 1), I32)],
        compiler_params=_cparams(1),
        name="dsa_sample_select",
    )(scores.reshape(rows, past), snew.reshape(rows, page))
    return t.reshape(nb, QP, 1), j.reshape(nb, QP, 1)


def _dsa_sample_attend_kernel(pt_ref, q_ref, kn_ref, vn_ref, sc_ref, sn_ref, t_ref, j_ref,
                              blast_ref, bnew_ref, *rest, pps, n_pages, steps):
    k_refs, v_refs, o_ref = rest[:pps], rest[pps:2 * pps], rest[2 * pps]
    lhs_s, m_s, l_s, acc_s = rest[2 * pps + 1:]
    g = pl.program_id(1)
    last = pl.num_programs(1) - 1
    page = k_refs[0].shape[-1]
    past = n_pages * page
    n = pps * page
    r = N_HEADS // N_KV_HEADS
    grow = r * QP

    @pl.when(g == 0)
    def _():
        m_s[...] = jnp.full(m_s.shape, NEG_BIG, F32)
        l_s[...] = jnp.zeros(l_s.shape, F32)
        acc_s[...] = jnp.zeros(acc_s.shape, F32)
        q = q_ref[0]
        for gg in range(N_KV_HEADS):
            lhs_s[gg] = jnp.concatenate(
                [q[:, (gg * r + u) * HEAD_DIM:(gg * r + u + 1) * HEAD_DIM] for u in range(r)], axis=0).astype(BF16)

    t, j = t_ref[0], j_ref[0]

    def select(sc, idx):
        sel = (sc > t) | ((sc == t) & (idx <= j))
        return jnp.where(sel & (sc > -jnp.inf), 0.0, NEG_BIG)

    def attend(kt, vt, madd, bias):
        mrows = jnp.concatenate([madd] * r, axis=0)
        for gg in range(N_KV_HEADS):
            rows = slice(gg * grow, (gg + 1) * grow)
            feat = slice(gg * HEAD_DIM, (gg + 1) * HEAD_DIM)
            lg = _dot(lhs_s[gg], kt[feat, :]) + mrows
            if bias is not None:
                lg = lg + bias[rows, :]
            m_old = m_s[rows, :]
            m_new = jnp.maximum(m_old, jnp.max(lg, axis=1, keepdims=True))
            p = jnp.exp(lg - m_new)
            alpha = jnp.exp(m_old - m_new)
            l_s[rows, :] = alpha * l_s[rows, :] + jnp.sum(p, axis=1, keepdims=True)
            acc_s[rows, :] = alpha * acc_s[rows, :] + _dot_nt(p.astype(BF16), vt[feat, :])
            m_s[rows, :] = m_new

    def pages(bias):
        kt = jnp.concatenate([k_refs[u][0, 0] for u in range(pps)], axis=1).astype(BF16)
        vt = jnp.concatenate([v_refs[u][0, 0] for u in range(pps)], axis=1).astype(BF16)
        idx = g * n + lax.broadcasted_iota(I32, (QP, n), 1)
        attend(kt, vt, select(sc_ref[0], idx), bias)

    @pl.when(g != last)
    def _():
        pages(None)

    @pl.when(g == last)
    def _():
        pages(jnp.concatenate([jnp.zeros((N_HEADS * QP, n - page), F32), blast_ref[...]], axis=1))
        idx = past + lax.broadcasted_iota(I32, (QP, page), 1)
        attend(_pad_t(kn_ref[0], page).astype(BF16), _pad_t(vn_ref[0], page).astype(BF16),
               select(_new_token_scores(sn_ref[0], steps), idx), bnew_ref[...])
        o = acc_s[...] / l_s[...]
        for h in range(N_HEADS):
            o_ref[0, :, h * HEAD_DIM:(h + 1) * HEAD_DIM] = o[h * QP:(h + 1) * QP, :]


def _dsa_sample(page_table, q, kn, vn, scores, snew, k_t, v_t, rel_bias, layer, steps, pps):
    nb, n_pages = page_table.shape
    page = k_t.shape[-1]
    past = n_pages * page
    nq, nkv = N_HEADS * HEAD_DIM, N_KV_HEADS * HEAD_DIM
    t, j = _dsa_sample_select(scores, snew, steps, min(TOPK_MAX, (past + steps) // 4))
    n_steps = n_pages // pps
    r = N_HEADS // N_KV_HEADS
    tab = _bias_table(rel_bias, 2 * page)
    qq = np.arange(QP)[:, None]
    off = np.arange(page)[None, :]
    blast = tab[:, page + qq - off].reshape(N_HEADS * QP, page)
    bnew = tab[:, np.maximum(qq - off, 0)].reshape(N_HEADS * QP, page)
    bmap = lambda b, g, pt: (b, 0, 0)
    grid_spec = pltpu.PrefetchScalarGridSpec(
        num_scalar_prefetch=1,
        grid=(nb, n_steps),
        in_specs=[pl.BlockSpec((1, QP, nq), bmap),
                  pl.BlockSpec((1, QP, nkv), bmap),
                  pl.BlockSpec((1, QP, nkv), bmap),
                  pl.BlockSpec((1, QP, pps * page), lambda b, g, pt: (b, 0, g)),
                  pl.BlockSpec((1, QP, page), bmap),
                  pl.BlockSpec((1, QP, 1), bmap), pl.BlockSpec((1, QP, 1), bmap),
                  pl.BlockSpec(blast.shape, lambda b, g, pt: (0, 0)),
                  pl.BlockSpec(bnew.shape, lambda b, g, pt: (0, 0))]
                 + _page_specs(k_t.shape, layer, pps) + _page_specs(v_t.shape, layer, pps),
        out_specs=pl.BlockSpec((1, QP, nq), bmap),
        scratch_shapes=[pltpu.VMEM((N_KV_HEADS, r * QP, HEAD_DIM), BF16),
                        pltpu.VMEM((N_HEADS * QP, 1), F32), pltpu.VMEM((N_HEADS * QP, 1), F32),
                        pltpu.VMEM((N_HEADS * QP, HEAD_DIM), F32)])
    return pl.pallas_call(
        functools.partial(_dsa_sample_attend_kernel, pps=pps, n_pages=n_pages, steps=steps),
        grid_spec=grid_spec,
        out_shape=jax.ShapeDtypeStruct((nb, QP, nq), F32),
        compiler_params=_cparams(2),
        name="dsa_sample_attend",
    )(page_table, q, kn, vn, scores, snew, t, j, blast, bnew, *([k_t] * pps), *([v_t] * pps))


SSD_COLS = 512


def _softplus(x):
    return jnp.maximum(x, 0.0) + jnp.log1p(jnp.exp(-jnp.abs(x)))


def _split3(a):
    p1 = a.astype(BF16)
    r1 = a - p1.astype(F32)
    p2 = r1.astype(BF16)
    return p1, p2, (r1 - p2.astype(F32)).astype(BF16)


def _dot_sel_rhs(a, e):
    p1, p2, p3 = _split3(a)
    return _dot(p1, e) + (_dot(p2, e) + _dot(p3, e))


def _dot_sel_lhs(e, a):
    p1, p2, p3 = _split3(a)
    return _dot(e, p1) + (_dot(e, p2) + _dot(e, p3))


def _ssd_prompt_kernel(z_ref, xbc_ref, dtr_ref, cw_ref, cb_ref, dtb_ref, alog_ref, dexp_ref, ng_ref,
                       e_ref, y_ref, nbuf_ref, hf_ref, prev_s, xc_s, y_s, h_s, *, width, nh, inner):
    i = pl.program_id(1)
    last = pl.num_programs(1) - 1
    qn, cdim = xbc_ref.shape[1], xbc_ref.shape[2]
    ngrp, ns, hd = SSD_GROUPS, SSD_STATE, SSD_HEADDIM
    hpg = nh // ngrp

    @pl.when(i == 0)
    def _():
        prev_s[...] = jnp.zeros(prev_s.shape, F32)
        h_s[...] = jnp.zeros(h_s.shape, F32)

    for cb in range(cdim // SSD_COLS):
        cols = slice(cb * SSD_COLS, (cb + 1) * SSD_COLS)
        xr = xbc_ref[0, :, cols]
        prev = prev_s[:, cols]
        conv = xr * cw_ref[width - 1:width, cols]
        for k in range(1, width):
            conv = conv + _shift_rows(xr, prev, k) * cw_ref[width - 1 - k:width - k, cols]
        xc_s[:, cols] = _silu(conv + cb_ref[:, cols])
        prev_s[:, cols] = xr[qn - SUBLANE:, :]
        nbuf_ref[0, :, cols] = xr[qn - (width - 1):, :]

    hl = lax.broadcasted_iota(I32, (qn, LANE), 1)
    dt = jnp.where(hl < nh, _softplus(dtr_ref[0] + dtb_ref[...]), 0.0)
    dta = dt * (-jnp.exp(alog_ref[...]))
    qrow = lax.broadcasted_iota(I32, (qn, qn), 0)
    kcol = lax.broadcasted_iota(I32, (qn, qn), 1)
    causal = kcol <= qrow
    cum = _dot_sel_lhs(jnp.where(causal, 1.0, 0.0).astype(BF16), dta)
    cum_t = cum.T
    cum_last = cum[qn - 1:qn, :]
    ecum = jnp.exp(cum)
    dtw = dt * jnp.exp(cum_last - cum)
    e_last = jnp.exp(cum_last)
    lane_p = lax.broadcasted_iota(I32, (qn, LANE), 1)
    row_p = lax.broadcasted_iota(I32, (LANE, ns), 0)

    for g in range(ngrp):
        gl = slice(g * SSD_COLS, (g + 1) * SSD_COLS)
        e_g = e_ref[:, gl]
        xs_g = xc_s[:, gl]
        dtx_g = xs_g * _dot_sel_rhs(dt, e_g)
        dtxw_g = xs_g * _dot_sel_rhs(dtw, e_g)
        ecx_g = _dot_sel_rhs(ecum, e_g)
        bm = xc_s[:, inner + g * ns:inner + (g + 1) * ns].astype(BF16)
        cm = xc_s[:, inner + (ngrp + g) * ns:inner + (ngrp + g + 1) * ns].astype(BF16)
        cbm = _dot_nt(cm, bm)
        for pi in range(hpg // 2):
            ha = g * hpg + 2 * pi
            pls = slice(pi * LANE, (pi + 1) * LANE)
            lanes = slice(ha * hd, ha * hd + LANE)
            dtx_p = dtx_g[:, pls].astype(BF16)
            yds = []
            for h in (ha, ha + 1):
                seg = cum[:, h:h + 1] - cum_t[h:h + 1, :]
                dec = jnp.exp(jnp.where(causal, seg, -jnp.inf))
                yds.append(_dot((cbm * dec).astype(BF16), dtx_p))
            yd = jnp.where(lane_p < hd, yds[0], yds[1])
            hp = h_s[ha // 2]
            yo = _dot_nt(cm, hp.astype(BF16)) * ecx_g[:, pls]
            y_s[:, lanes] = yd + yo + dexp_ref[:, lanes] * xs_g[:, pls]
            s_new = _dot(dtxw_g[:, pls].T.astype(BF16), bm)
            cd = jnp.where(row_p < hd, e_last[:, ha:ha + 1], e_last[:, ha + 1:ha + 2])
            h_s[ha // 2] = hp * cd + s_new

    for g in range(ngrp):
        gl = slice(g * SSD_COLS, (g + 1) * SSD_COLS)
        yg = y_s[:, gl] * _silu(z_ref[0, :, gl])
        ms = jnp.mean(yg * yg, axis=-1, keepdims=True)
        y_ref[0, :, gl] = ((yg * lax.rsqrt(ms + EPS)) * ng_ref[:, gl]).astype(y_ref.dtype)

    @pl.when(i == last)
    def _():
        hf_ref[0] = h_s[...]


def _head_expand(nh, hd):
    e = np.zeros((LANE, nh * hd), np.float32)
    for h in range(nh):
        e[h, h * hd:(h + 1) * hd] = 1.0
    return jnp.asarray(e, BF16)


def _pad_lanes(v, n=LANE):
    v = v.reshape(1, -1)
    return jnp.pad(v, ((0, 0), (0, n - v.shape[1])))


def _ssd_prompt(z, xbc, dtr, conv_w, conv_b, dt_bias, a_log, d_skip, norm_g):
    bsz, seq, inner = z.shape
    cdim = xbc.shape[-1]
    nh = dt_bias.shape[0]
    width = conv_w.shape[0]
    qn = math.gcd(seq, SSD_CHUNK)
    assert qn == SSD_CHUNK and inner == SSD_GROUPS * SSD_COLS and cdim % SSD_COLS == 0
    dexp = jnp.repeat(d_skip, SSD_HEADDIM).reshape(1, inner)
    blk = lambda n: pl.BlockSpec((1, qn, n), lambda b, i: (b, i, 0))
    y, nbuf, hf = pl.pallas_call(
        functools.partial(_ssd_prompt_kernel, width=width, nh=nh, inner=inner),
        grid=(bsz, seq // qn),
        in_specs=[blk(inner), blk(cdim), blk(LANE),
                  _const2((width, cdim)), _const2((1, cdim)), _const2((1, LANE)), _const2((1, LANE)),
                  _const2((1, inner)), _const2((1, inner)), _const2((LANE, inner))],
        out_specs=[blk(inner),
                   pl.BlockSpec((1, width - 1, cdim), lambda b, i: (b, 0, 0)),
                   pl.BlockSpec((1, nh // 2, LANE, SSD_STATE), lambda b, i: (b, 0, 0, 0))],
        out_shape=[jax.ShapeDtypeStruct((bsz, seq, inner), BF16),
                   jax.ShapeDtypeStruct((bsz, width - 1, cdim), F32),
                   jax.ShapeDtypeStruct((bsz, nh // 2, LANE, SSD_STATE), F32)],
        scratch_shapes=[pltpu.VMEM((SUBLANE, cdim), F32), pltpu.VMEM((qn, cdim), F32),
                        pltpu.VMEM((qn, inner), F32), pltpu.VMEM((nh // 2, LANE, SSD_STATE), F32)],
        compiler_params=_cparams(2),
        name="ssd_prompt",
    )(z, xbc, dtr, conv_w, conv_b.reshape(1, cdim), _pad_lanes(dt_bias), _pad_lanes(a_log),
      dexp, norm_g.reshape(1, inner), _head_expand(nh, SSD_HEADDIM))
    return y, nbuf, hf.reshape(bsz, nh, SSD_HEADDIM, SSD_STATE)


def _ssd_prep_sample_kernel(xbc_ref, buf_ref, dtr_ref, cw_ref, cb_ref, dtb_ref, xc_ref, dt_ref, nbuf_ref,
                            *, width, nb, steps, nh):
    cdim = xbc_ref.shape[1]
    for cb in range(cdim // SSD_COLS):
        cols = slice(cb * SSD_COLS, (cb + 1) * SSD_COLS)
        ext = jnp.concatenate([buf_ref[:, cols], xbc_ref[:, cols]], axis=0)
        conv = ext[0:steps * nb] * cw_ref[0:1, cols]
        for k in range(1, width):
            conv = conv + ext[k * nb:(k + steps) * nb] * cw_ref[k:k + 1, cols]
        xc_ref[:, cols] = _silu(conv + cb_ref[:, cols])
        nbuf_ref[:, cols] = ext[steps * nb:]
    hl = lax.broadcasted_iota(I32, dtr_ref.shape, 1)
    dt_ref[...] = jnp.where(hl < nh, _softplus(dtr_ref[...] + dtb_ref[...]), 0.0)


def _ssd_prep_sample(xbc_tb, buf_kb, dtr_tb, conv_w, conv_b, dt_bias, nb, steps):
    rows, cdim = xbc_tb.shape
    width = conv_w.shape[0]
    return pl.pallas_call(
        functools.partial(_ssd_prep_sample_kernel, width=width, nb=nb, steps=steps, nh=dt_bias.shape[0]),
        out_shape=[jax.ShapeDtypeStruct((rows, cdim), F32),
                   jax.ShapeDtypeStruct((rows, LANE), F32),
                   jax.ShapeDtypeStruct(((width - 1) * nb, cdim), F32)],
        compiler_params=pltpu.CompilerParams(vmem_limit_bytes=VMEM_LIMIT),
        name="ssd_prep_sample",
    )(xbc_tb, buf_kb, dtr_tb, conv_w, conv_b.reshape(1, cdim), _pad_lanes(dt_bias))


def _ssd_scan_sample_kernel(xc_ref, dt_ref, z_ref, h0_ref, alog_ref, dexp_ref, ng_ref, e_ref,
                            y_ref, hf_ref, *, steps, inner):
    ngrp, ns = SSD_GROUPS, SSD_STATE
    xc, dt = xc_ref[0], dt_ref[0]
    row = lax.broadcasted_iota(I32, (QP, LANE), 0)
    cum = dt * (-jnp.exp(alog_ref[...]))
    for sft in (1, 2, 4):
        cum = cum + jnp.where(row >= sft, pltpu.roll(cum, sft, 0), 0.0)
    cum_last = cum[QP - 1:QP, :]
    parts = [dt, jnp.exp(cum), dt * jnp.exp(cum_last - cum), jnp.broadcast_to(jnp.exp(cum_last), (QP, LANE))]
    for s in range(steps):
        parts.append(jnp.where(row >= s, jnp.exp(cum - cum[s:s + 1, :]), 0.0))
    stack = jnp.concatenate(parts, axis=0)

    for g in range(ngrp):
        gl = slice(g * SSD_COLS, (g + 1) * SSD_COLS)
        ex = _dot_sel_rhs(stack, e_ref[:, gl])
        dt_x, ecum_x, dtw_x, el_x = (ex[k * QP:(k + 1) * QP] for k in range(4))
        xs_g = xc[:, gl]
        dtx = xs_g * dt_x
        bm = xc[:, inner + g * ns:inner + (g + 1) * ns]
        cm = xc[:, inner + (ngrp + g) * ns:inner + (ngrp + g + 1) * ns].astype(BF16)
        bmp = jnp.concatenate([bm, jnp.zeros((LANE - QP, ns), F32)], axis=0).astype(BF16)
        cbm = _dot_nt(cm, bmp)
        yd = jnp.zeros((QP, SSD_COLS), F32)
        for s in range(steps):
            yd = yd + (ex[(4 + s) * QP:(5 + s) * QP] * cbm[:, s:s + 1]) * dtx[s:s + 1, :]
        h0g = h0_ref[0, gl, :]
        yo = _dot_nt(cm, h0g.astype(BF16)) * ecum_x
        y = (yd + yo + dexp_ref[:, gl] * xs_g) * _silu(z_ref[0, :, gl])
        ms = jnp.mean(y * y, axis=-1, keepdims=True)
        y_ref[0, :, gl] = (y * lax.rsqrt(ms + EPS)) * ng_ref[:, gl]
        tm = jnp.concatenate([xs_g * dtw_x, el_x[0:1], jnp.zeros((LANE - QP - 1, SSD_COLS), F32)], axis=0)
        tt = tm.T
        hf_ref[0, gl, :] = h0g * tt[:, QP:QP + 1] + _dot(tt.astype(BF16), bmp)


def _ssd_scan_sample(xc, dt, z, h0, a_log, d_skip, norm_g, steps):
    nb, _, cdim = xc.shape
    inner = z.shape[-1]
    nh = a_log.shape[0]
    rows_h = nh * SSD_HEADDIM
    dexp = jnp.repeat(d_skip, SSD_HEADDIM).reshape(1, inner)
    one = lambda shape: pl.BlockSpec(shape, lambda b: (0,) * len(shape))
    return pl.pallas_call(
        functools.partial(_ssd_scan_sample_kernel, steps=steps, inner=inner),
        grid=(nb,),
        in_specs=[pl.BlockSpec((1, QP, cdim), lambda b: (b, 0, 0)),
                  pl.BlockSpec((1, QP, LANE), lambda b: (b, 0, 0)),
                  pl.BlockSpec((1, QP, inner), lambda b: (b, 0, 0)),
                  pl.BlockSpec((1, rows_h, SSD_STATE), lambda b: (b, 0, 0)),
                  one((1, LANE)), one((1, inner)), one((1, inner)), one((LANE, inner))],
        out_specs=[pl.BlockSpec((1, QP, inner), lambda b: (b, 0, 0)),
                   pl.BlockSpec((1, rows_h, SSD_STATE), lambda b: (b, 0, 0))],
        out_shape=[jax.ShapeDtypeStruct((nb, QP, inner), F32),
                   jax.ShapeDtypeStruct((nb, rows_h, SSD_STATE), F32)],
        compiler_params=_cparams(1),
        name="ssd_scan_sample",
    )(xc, dt, z, h0, _pad_lanes(a_log), dexp, norm_g.reshape(1, inner), _head_expand(nh, SSD_HEADDIM))


TM = 256


def _to_steps(a, nb, steps):
    n = a.shape[-1]
    return a.reshape(nb, steps, n).transpose(1, 0, 2).reshape(steps * nb, n)


def _to_batch(a, nb, steps):
    n = a.shape[-1]
    return a.reshape(steps, nb, n).transpose(1, 0, 2).reshape(1, nb * steps, n)


def _pad_q(a, nb, steps):
    n = a.shape[-1]
    return jnp.pad(a.reshape(nb, steps, n), ((0, 0), (0, QP - steps), (0, 0)))


def kernel(x_prompt, x_sample, cache_k, cache_v, cache_kidx, state_s5_re, state_s5_im, state_sconv, state_ssd, state_ssd_conv, page_table, c_prompt, c_sample, rel_bias, ada_w, ada_b, norm_mix, norm_mlp, norm_final, attn_w_in, attn_w_out, s5_lam_re, s5_lam_im, s5_log_dt, s5_b_re, s5_b_im, s5_c_re, s5_c_im, s5_d, s5_w_glu, sc_w_in, sc_w_conv, sc_w_out, ssd_w_in, ssd_conv_w, ssd_conv_b, ssd_dt_bias, ssd_a_log, ssd_d, ssd_norm, ssd_w_out, mlp_w1, mlp_w2):
    bp, seq, d = x_prompt.shape
    nb, steps, _ = x_sample.shape
    depth = ada_w.shape[0]
    n_mixers = 4
    rs = nb * steps
    tm = min(TM, seq)

    rows = bp + nb
    c_all = jnp.pad(jnp.concatenate([c_prompt, c_sample], axis=0), ((0, (-rows) % SUBLANE), (0, 0)))
    ada = _ada(c_all, ada_w, ada_b)

    xp = x_prompt
    xs = x_sample.reshape(1, rs, d)
    outs = {name: [] for name in ("kp", "vp", "kip", "ks", "vs", "kis", "s5pr", "s5pi", "s5sr", "s5si",
                                  "scp", "scs", "ssdp", "ssdcp", "ssds", "ssdcs")}
    nq, nkv = N_HEADS * HEAD_DIM, N_KV_HEADS * HEAD_DIM
    w1_all, w2_all = mlp_w1.astype(BF16), mlp_w2.astype(BF16)
    for i in range(depth):
        m, j = i % n_mixers, i // n_mixers
        mp = [ada[i, :bp, k * d:(k + 1) * d].reshape(bp, 1, d) for k in range(6)]
        ms = [jnp.repeat(ada[i, bp:bp + nb, k * d:(k + 1) * d], steps, axis=0).reshape(1, rs, d)
              for k in range(6)]
        g_mix = norm_mix[i]
        mix_p = mix_s = None
        if m == 0:
            w_in = attn_w_in[j]
            w_qkv = w_in[:, :nq + 2 * nkv].astype(BF16)
            n_idx = IDX_HEADS * IDX_DIM + LANE
            w_idx = jnp.pad(w_in[:, nq + 2 * nkv:], ((0, 0), (0, n_idx - (w_in.shape[1] - nq - 2 * nkv))))
            w_out = attn_w_out[j].astype(BF16)
            page = cache_k.shape[2]
            q, k, v, kw, qcat, kcat, kaug, vaug, ki = _attn_proj(xp, mp[0], mp[1], g_mix, w_qkv, w_idx,
                                                                 2 * MAX_DISTANCE, True, page)
            o = _dsa_prompt(q, qcat, kw, kcat, kaug, vaug, rel_bias, MAX_DISTANCE)
            mix_p = ("proj", o, mp[2], w_out)
            to_pages = lambda t: jnp.transpose(t.reshape(bp, seq // page, N_KV_HEADS, HEAD_DIM, page), (0, 1, 4, 2, 3))
            outs["kp"].append(to_pages(k))
            outs["vp"].append(to_pages(v))
            outs["kip"].append(jnp.transpose(ki, (0, 1, 3, 2)))
            q, k, v, kw, qcat, kcat, _, _ = _attn_proj(xs, ms[0], ms[1], g_mix, w_qkv, w_idx, rs, False)
            pps = math.gcd(PAGES_PER_STEP, page_table.shape[1])
            qc_p, kw_p = _pad_q(qcat.astype(F32), nb, steps), _pad_q(kw, nb, steps)
            scores, snew = _dsa_sample_scores(page_table, qc_p, kw_p, _pad_q(kcat.astype(F32), nb, steps),
                                              _pages_t(cache_kidx), j,
                                              math.gcd(2 * PAGES_PER_STEP, page_table.shape[1]))
            o = _dsa_sample(page_table, _pad_q(q, nb, steps), _pad_q(k, nb, steps), _pad_q(v, nb, steps),
                            scores, snew, _pages_t(cache_k), _pages_t(cache_v), rel_bias, j, steps, pps)
            mix_s = ("proj", o[:, :steps].reshape(1, rs, nq), ms[2], w_out)
            outs["ks"].append(k.reshape(nb, steps, N_KV_HEADS, HEAD_DIM))
            outs["vs"].append(v.reshape(nb, steps, N_KV_HEADS, HEAD_DIM))
            outs["kis"].append(kw[..., :IDX_DIM].reshape(nb, steps, IDX_DIM))
        elif m == 1:
            ar, ai, wb, wc = _s5_weights(s5_lam_re[j], s5_lam_im[j], s5_log_dt[j], s5_b_re[j], s5_b_im[j],
                                         s5_c_re[j], s5_c_im[j])
            w_glu = s5_w_glu[j].astype(BF16)
            grp, nst = s5_lam_re.shape[1:]
            z, fr, fi = _s5_prompt(xp, mp[0], mp[1], g_mix, s5_d[j], ar, ai, wb, wc)
            mix_p = ("glu", z, mp[2], w_glu)
            outs["s5pr"].append(fr.reshape(bp, grp, nst))
            outs["s5pi"].append(fi.reshape(bp, grp, nst))
            z, fr, fi = _s5_sample(_to_steps(xs, nb, steps), _to_steps(ms[0], nb, steps),
                                   _to_steps(ms[1], nb, steps), g_mix, s5_d[j], ar, ai, wb, wc,
                                   state_s5_re[j].reshape(nb, grp * nst), state_s5_im[j].reshape(nb, grp * nst),
                                   nb, steps)
            mix_s = ("glu", _to_batch(z, nb, steps), ms[2], w_glu)
            outs["s5sr"].append(fr.reshape(nb, grp, nst))
            outs["s5si"].append(fi.reshape(nb, grp, nst))
        elif m == 2:
            w_in = sc_w_in[j].astype(BF16)
            w_out = sc_w_out[j].astype(BF16)
            width = sc_w_conv.shape[1]
            xp, nbuf = _sconv_prompt(xp, mp[0], mp[1], g_mix, w_in, mp[2], jnp.zeros((bp, width - 1, d), F32),
                                     sc_w_conv[j], w_out, tm)
            outs["scp"].append(nbuf)
            (p,) = _proj(xs, ms[0], ms[1], g_mix, w_in, (3 * d,), rs)
            buf = state_sconv[j].transpose(1, 0, 2).reshape((width - 1) * nb, d)
            y, nbuf = _sconv_sample(_to_steps(p, nb, steps), _to_steps(xs, nb, steps), _to_steps(ms[2], nb, steps),
                                    buf, sc_w_conv[j], w_out, nb, steps)
            xs = _to_batch(y, nb, steps)
            outs["scs"].append(nbuf.reshape(width - 1, nb, d).transpose(1, 0, 2))
        else:
            inner = ssd_norm.shape[1]
            cdim = ssd_conv_w.shape[2]
            nh = ssd_dt_bias.shape[1]
            width = ssd_conv_w.shape[1]
            w_in = jnp.pad(ssd_w_in[j], ((0, 0), (0, LANE - nh))).astype(BF16)
            w_out = ssd_w_out[j].astype(BF16)
            z, xbc, dtr = _proj(xp, mp[0], mp[1], g_mix, w_in, (inner, cdim, LANE), tm)
            y, nbuf, hf = _ssd_prompt(z, xbc, dtr, ssd_conv_w[j], ssd_conv_b[j], ssd_dt_bias[j], ssd_a_log[j],
                                      ssd_d[j], ssd_norm[j])
            mix_p = ("proj", y, mp[2], w_out)
            outs["ssdp"].append(hf)
            outs["ssdcp"].append(nbuf)
            z, xbc, dtr = _proj(xs, ms[0], ms[1], g_mix, w_in, (inner, cdim, LANE), rs)
            buf = state_ssd_conv[j].transpose(1, 0, 2).reshape((width - 1) * nb, cdim)
            xc, dt, nbuf = _ssd_prep_sample(_to_steps(xbc, nb, steps), buf, _to_steps(dtr, nb, steps),
                                            ssd_conv_w[j], ssd_conv_b[j], ssd_dt_bias[j], nb, steps)
            y, hf = _ssd_scan_sample(_pad_q(_to_batch(xc, nb, steps), nb, steps),
                                     _pad_q(_to_batch(dt, nb, steps), nb, steps), _pad_q(z, nb, steps),
                                     state_ssd[j].reshape(nb, nh * SSD_HEADDIM, SSD_STATE),
                                     ssd_a_log[j], ssd_d[j], ssd_norm[j], steps)
            mix_s = ("proj", y[:, :steps].reshape(1, rs, inner), ms[2], w_out)
            outs["ssds"].append(hf.reshape(nb, nh, SSD_HEADDIM, SSD_STATE))
            outs["ssdcs"].append(nbuf.reshape(width - 1, nb, cdim).transpose(1, 0, 2))
        fin = i == depth - 1
        xp = _mlp(xp, mp[3], mp[4], mp[5], norm_mlp[i], w1_all, w2_all, i, norm_final, fin, tm, mix_p)
        xs = _mlp(xs, ms[3], ms[4], ms[5], norm_mlp[i], w1_all, w2_all, i, norm_final, fin, rs, mix_s)
    st = jnp.stack
    return (xp, xs.reshape(nb, steps, d), st(outs["kp"]), st(outs["vp"]), st(outs["kip"]),
            st(outs["ks"]), st(outs["vs"]), st(outs["kis"]), st(outs["s5pr"]), st(outs["s5pi"]),
            st(outs["s5sr"]), st(outs["s5si"]), st(outs["scp"]), st(outs["scs"]),
            st(outs["ssdp"]), st(outs["ssdcp"]), st(outs["ssds"]), st(outs["ssdcs"]))
```

```python
import functools
import math

import jax
import jax.numpy as jnp
import numpy as np
from jax import lax
from jax.experimental import pallas as pl
from jax.experimental.pallas import tpu as pltpu

F32 = jnp.float32
BF16 = jnp.bfloat16
I32 = jnp.int32

EPS = 1e-6
N_HEADS = 16
HEAD_DIM = 64
N_KV_HEADS = 4
IDX_HEADS = 8
IDX_DIM = 64
TOPK_MAX = 256
N_BUCKETS = 32
MAX_DISTANCE = 128
S5_GROUP = 16
S5_STATE = 64
SSD_HEADDIM = 64
SSD_GROUPS = 4
SSD_STATE = 128
SSD_CHUNK = 128

LANE = 128
SUBLANE = 8
VMEM_LIMIT = 56 * 1024 * 1024
NEG_BIG = -1e30
INT_MIN = -2147483648


def _cparams(n_axes):
    return pltpu.CompilerParams(dimension_semantics=("arbitrary",) * n_axes,
                                vmem_limit_bytes=VMEM_LIMIT)


def _dot(a, b):
    return jnp.dot(a, b, preferred_element_type=F32)


def _dot_nt(a, b):
    return lax.dot_general(a, b, (((1,), (1,)), ((), ())), preferred_element_type=F32)


def _split(a):
    hi = a.astype(BF16)
    lo = (a - hi.astype(F32)).astype(BF16)
    return hi, lo


def _dot3(a, b):
    ah, al = _split(a)
    bh, bl = _split(b)
    return _dot(ah, bh) + (_dot(al, bh) + _dot(ah, bl))


def _rms_mod(x, g, shift, scale):
    y = x * lax.rsqrt(jnp.mean(x * x, axis=-1, keepdims=True) + EPS)
    return (y * g) * (1.0 + scale) + shift


def _sigmoid(x):
    return 1.0 / (1.0 + jnp.exp(-x))


def _silu(x):
    return x * _sigmoid(x)


def _row_block(arr, tm):
    n = arr.shape[-1]
    if arr.shape[1] == 1:
        return pl.BlockSpec((1, 1, n), lambda b, i: (b, 0, 0))
    return pl.BlockSpec((1, tm, n), lambda b, i: (b, i, 0))


def _const2(shape):
    return pl.BlockSpec(shape, lambda b, i: (0,) * len(shape))


def _ada_kernel(c_ref, w_ref, b_ref, o_ref):
    o_ref[0] = _dot3(_silu(c_ref[...]), w_ref[0]) + b_ref[0]


def _ada(c_all, ada_w, ada_b):
    depth, d, n = ada_w.shape
    rows = c_all.shape[0]
    tn = 1536
    return pl.pallas_call(
        _ada_kernel,
        grid=(depth, n // tn),
        in_specs=[pl.BlockSpec((rows, d), lambda l, j: (0, 0)),
                  pl.BlockSpec((1, d, tn), lambda l, j: (l, 0, j)),
                  pl.BlockSpec((1, 1, tn), lambda l, j: (l, 0, j))],
        out_specs=pl.BlockSpec((1, rows, tn), lambda l, j: (l, 0, j)),
        out_shape=jax.ShapeDtypeStruct((depth, rows, n), F32),
        compiler_params=_cparams(2),
        name="ada",
    )(c_all, ada_w, ada_b.reshape(depth, 1, n))


def _proj_kernel(x_ref, sh_ref, sc_ref, g_ref, w_ref, *o_refs, splits):
    h = _rms_mod(x_ref[0], g_ref[...], sh_ref[0], sc_ref[0]).astype(BF16)
    off = 0
    for o_ref, n in zip(o_refs, splits):
        o_ref[0] = _dot(h, w_ref[:, off:off + n])
        off += n


def _proj(x, shift, scale, g, w_bf16, splits, tm):
    bsz, seq, d = x.shape
    n = w_bf16.shape[1]
    assert sum(splits) == n and seq % tm == 0
    return pl.pallas_call(
        functools.partial(_proj_kernel, splits=tuple(splits)),
        grid=(bsz, seq // tm),
        in_specs=[pl.BlockSpec((1, tm, d), lambda b, i: (b, i, 0)),
                  _row_block(shift, tm), _row_block(scale, tm),
                  _const2((1, d)), _const2((d, n))],
        out_specs=[pl.BlockSpec((1, tm, s), lambda b, i: (b, i, 0)) for s in splits],
        out_shape=[jax.ShapeDtypeStruct((bsz, seq, s), F32) for s in splits],
        compiler_params=_cparams(2),
        name="proj",
    )(x, shift, scale, g.reshape(1, d), w_bf16)


def _aug_heads(t, col):
    rows = t.shape[0]
    lane = lax.broadcasted_iota(I32, (rows, LANE - HEAD_DIM), 1)
    extra = jnp.where(lane == 0, col, 0.0).astype(F32)
    parts = []
    for g in range(N_KV_HEADS):
        parts += [t[:, g * HEAD_DIM:(g + 1) * HEAD_DIM], extra]
    return jnp.concatenate(parts, axis=1).astype(BF16)


def _attn_proj_kernel(x_ref, sh_ref, sc_ref, g_ref, w_ref, wi_ref,
                      q_ref, k_ref, v_ref, kw_ref, qcat_ref, kcat_ref, kaug_ref, vaug_ref, *ki_refs,
                      nq, nkv, nqi, key_major):
    (ki_ref,) = ki_refs if key_major else (None,)
    h = _rms_mod(x_ref[0], g_ref[...], sh_ref[0], sc_ref[0])
    hb = h.astype(BF16)
    q_ref[0] = (_dot(hb, w_ref[:, 0:nq]) * (HEAD_DIM ** -0.5)).astype(q_ref.dtype)
    k = _dot(hb, w_ref[:, nq:nq + nkv])
    v = _dot(hb, w_ref[:, nq + nkv:nq + 2 * nkv])
    r = _dot3(h, wi_ref[...])
    kw = r[:, nqi:nqi + LANE]
    kw_ref[0] = kw
    if key_major:
        page = k_ref.shape[-1]
        kt, vt, kwt = k.T, v.T, kw.T
        for p in range(k_ref.shape[1]):
            k_ref[0, p] = kt[:, p * page:(p + 1) * page]
            v_ref[0, p] = vt[:, p * page:(p + 1) * page]
            ki_ref[0, p] = kwt[0:IDX_DIM, p * page:(p + 1) * page]
    else:
        k_ref[0] = k
        v_ref[0] = v
    zero = jnp.zeros((h.shape[0], IDX_DIM), BF16)
    parts = []
    for hh in range(IDX_HEADS):
        ah, al = _split(r[:, hh * IDX_DIM:(hh + 1) * IDX_DIM])
        parts += [ah, al, ah, zero]
    qcat_ref[0] = jnp.concatenate(parts, axis=1)
    kcat = _idx_rhs(kw[:, 0:IDX_DIM])
    kaug = _aug_heads(k, 0.0)
    vaug_ref[0] = _aug_heads(v, 1.0)
    if key_major:
        kcat_ref[0, 0] = kcat.astype(F32).T.astype(BF16)
        kaug_ref[0, 0] = kaug.astype(F32).T.astype(BF16)
    else:
        kcat_ref[0] = kcat
        kaug_ref[0] = kaug


def _attn_proj(x, shift, scale, g, w_qkv_bf16, w_idx, tm, key_major, page=None):
    bsz, seq, d = x.shape
    nq = N_HEADS * HEAD_DIM
    nkv = N_KV_HEADS * HEAD_DIM
    nqi = IDX_HEADS * IDX_DIM
    rows = lambda n, dt: (pl.BlockSpec((1, tm, n), lambda b, i: (b, i, 0)), jax.ShapeDtypeStruct((bsz, seq, n), dt))
    cols = lambda n, dt: (pl.BlockSpec((1, 1, n, tm), lambda b, i: (b, i, 0, 0)),
                          jax.ShapeDtypeStruct((bsz, seq // tm, n, tm), dt))
    keys = cols if key_major else rows
    if key_major:
        ppt = tm // page
        paged = lambda n: (pl.BlockSpec((1, ppt, n, page), lambda b, i: (b, i, 0, 0)),
                           jax.ShapeDtypeStruct((bsz, seq // page, n, page), F32))
        kv = [paged(nkv), paged(nkv)]
    else:
        kv = [rows(nkv, F32), rows(nkv, F32)]
    outs = [rows(nq, BF16 if key_major else F32)] + kv + [rows(LANE, F32), rows(4 * nqi, BF16),
                                   keys(4 * IDX_DIM, BF16), keys(N_KV_HEADS * LANE, BF16),
                                   rows(N_KV_HEADS * LANE, BF16)]
    if key_major:
        outs.append(paged(IDX_DIM))
    return pl.pallas_call(
        functools.partial(_attn_proj_kernel, nq=nq, nkv=nkv, nqi=nqi, key_major=key_major),
        grid=(bsz, seq // tm),
        in_specs=[pl.BlockSpec((1, tm, d), lambda b, i: (b, i, 0)),
                  _row_block(shift, tm), _row_block(scale, tm),
                  _const2((1, d)), _const2(w_qkv_bf16.shape), _const2(w_idx.shape)],
        out_specs=[o[0] for o in outs],
        out_shape=[o[1] for o in outs],
        compiler_params=_cparams(2),
        name="attn_proj",
    )(x, shift, scale, g.reshape(1, d), w_qkv_bf16, w_idx)


def _mlp_kernel(*refs, ff_chunk, final_norm, mixer_out):
    if mixer_out is None:
        x_ref, sh_ref, sc_ref, gate_ref, g_ref, w1_ref, w2_ref, gf_ref, y_ref = refs
        x = x_ref[0]
    else:
        o_ref, gmix_ref, wo_ref, x_ref, sh_ref, sc_ref, gate_ref, g_ref, w1_ref, w2_ref, gf_ref, y_ref = refs
        d = x_ref.shape[-1]
        ob = o_ref[0].astype(BF16)
        if mixer_out == "glu":
            t = _dot(ob, wo_ref[:, :d]) * _sigmoid(_dot(ob, wo_ref[:, d:]))
        else:
            t = _dot(ob, wo_ref[...])
        x = x_ref[0] + gmix_ref[0] * t
    h = _rms_mod(x, g_ref[...], sh_ref[0], sc_ref[0]).astype(BF16)
    dff = w1_ref.shape[-1]
    acc = jnp.zeros(x.shape, F32)
    for c in range(dff // ff_chunk):
        a = jnp.maximum(_dot(h, w1_ref[0, :, c * ff_chunk:(c + 1) * ff_chunk]), 0.0)
        acc = acc + _dot((a * a).astype(BF16), w2_ref[0, c * ff_chunk:(c + 1) * ff_chunk, :])
    y = x + gate_ref[0] * acc
    if final_norm:
        y = (y * lax.rsqrt(jnp.mean(y * y, axis=-1, keepdims=True) + EPS)) * gf_ref[...]
    y_ref[0] = y


def _resident(shape):
    return pl.BlockSpec(shape, lambda b, i: (0,) * len(shape), pipeline_mode=pl.Buffered(1))


def _mlp(x, shift, scale, gate, g, w1_bf16, w2_bf16, layer, g_final, final_norm, tm, mixer=None):
    bsz, seq, d = x.shape
    dff = w1_bf16.shape[-1]
    layer_w = lambda r, c: pl.BlockSpec((1, r, c), lambda b, i: (layer, 0, 0), pipeline_mode=pl.Buffered(1))
    rows = lambda n: pl.BlockSpec((1, tm, n), lambda b, i: (b, i, 0))
    args, specs, kind = [], [], None
    if mixer is not None:
        kind, o, gmix, wo = mixer
        args += [o, gmix, wo]
        specs += [rows(o.shape[-1]), _row_block(gmix, tm), _resident(wo.shape)]
    args += [x, shift, scale, gate, g.reshape(1, d), w1_bf16, w2_bf16, g_final.reshape(1, d)]
    specs += [rows(d), _row_block(shift, tm), _row_block(scale, tm), _row_block(gate, tm),
              _const2((1, d)), layer_w(d, dff), layer_w(dff, d), _const2((1, d))]
    return pl.pallas_call(
        functools.partial(_mlp_kernel, ff_chunk=1024, final_norm=final_norm, mixer_out=kind),
        grid=(bsz, seq // tm),
        in_specs=specs,
        out_specs=rows(d),
        out_shape=jax.ShapeDtypeStruct((bsz, seq, d), F32),
        compiler_params=_cparams(2),
        name="mlp",
    )(*args)


S5_CH = 128
S5_BLK = (S5_CH // S5_GROUP) * S5_STATE


def _s5_disc_kernel(lr_ref, li_ref, ldt_ref, br_ref, bi_ref, ar_ref, ai_ref, bbr_ref, bbi_ref):
    lr, li = lr_ref[...], li_ref[...]
    dt = jnp.exp(ldt_ref[...])
    mag = jnp.exp(lr * dt)
    ab_re, ab_im = mag * jnp.cos(li * dt), mag * jnp.sin(li * dt)
    den = lr * lr + li * li
    nr = ab_re - 1.0
    f_re = (nr * lr + ab_im * li) / den
    f_im = (ab_im * lr - nr * li) / den
    ar_ref[...] = ab_re
    ai_ref[...] = ab_im
    for c in range(br_ref.shape[0]):
        br, bi = br_ref[c], bi_ref[c]
        bbr_ref[c] = f_re * br - f_im * bi
        bbi_ref[c] = f_re * bi + f_im * br


def _s5_weights(lam_re, lam_im, log_dt, b_re, b_im, c_re, c_im):
    g, p = lam_re.shape
    gc = b_re.shape[-1]
    brt = jnp.moveaxis(b_re, 2, 0)
    bit = jnp.moveaxis(b_im, 2, 0)
    ar, ai, bbr, bbi = pl.pallas_call(
        _s5_disc_kernel,
        out_shape=[jax.ShapeDtypeStruct((g, p), F32), jax.ShapeDtypeStruct((g, p), F32),
                   jax.ShapeDtypeStruct((gc, g, p), F32), jax.ShapeDtypeStruct((gc, g, p), F32)],
        name="s5_disc",
    )(lam_re, lam_im, log_dt.reshape(g, 1), brt, bit)
    nblk = (g * gc) // S5_CH
    gpb = g // nblk
    eye = jnp.eye(gpb, dtype=F32)

    def bd_in(bb):
        t = jnp.moveaxis(bb, 0, 1).reshape(nblk, gpb, gc, p)
        return jnp.einsum('ngcp,gh->ngchp', t, eye).reshape(nblk, gpb * gc, gpb * p)

    def bd_out(cc):
        t = cc.reshape(nblk, gpb, gc, p)
        return jnp.einsum('ngcp,gh->ngphc', t, eye).reshape(nblk, gpb * p, gpb * gc)

    wb = jnp.concatenate([bd_in(bbr), bd_in(bbi)], axis=-1).astype(BF16)
    wc = jnp.concatenate([bd_out(c_re), bd_out(c_im)], axis=1).astype(BF16)
    return ar.reshape(nblk, 1, gpb * p), ai.reshape(nblk, 1, gpb * p), wb, wc


def _gelu_tanh(y):
    return 0.5 * y * (1.0 + jnp.tanh(math.sqrt(2.0 / math.pi) * (y + 0.044715 * (y * y * y))))


def _cmul(ar, ai, xr, xi):
    return ar * xr - ai * xi, ar * xi + ai * xr


def _s5_prompt_kernel(x_ref, sh_ref, sc_ref, g_ref, d_ref, ar_ref, ai_ref, wb_ref, wc_ref,
                      perm_ref, unperm_ref, z_ref, fr_ref, fi_ref, u_s, br_s, bi_s, st_r, st_i,
                      *, nseg, seg):
    i = pl.program_id(1)
    nblk = wb_ref.shape[0]
    blk = ar_ref.shape[-1]

    @pl.when(i == 0)
    def _():
        st_r[...] = jnp.zeros(st_r.shape, F32)
        st_i[...] = jnp.zeros(st_i.shape, F32)

    u_s[...] = _dot_sel_lhs(perm_ref[...], _rms_mod(x_ref[0], g_ref[...], sh_ref[0], sc_ref[0]))

    for c in range(nblk):
        lo, hi = c * S5_CH, (c + 1) * S5_CH
        uc = u_s[:, lo:hi]
        bu = _dot(uc.astype(BF16), wb_ref[c])
        br_s[...] = bu[:, :blk]
        bi_s[...] = bu[:, blk:]
        ar = jnp.broadcast_to(ar_ref[c], (nseg, blk))
        ai = jnp.broadcast_to(ai_ref[c], (nseg, blk))

        def local(j, carry):
            xr, xi = carry
            r0 = pl.multiple_of(j * nseg, nseg)
            pr, pi = _cmul(ar, ai, xr, xi)
            nr = pr + br_s[pl.ds(r0, nseg), :]
            ni = pi + bi_s[pl.ds(r0, nseg), :]
            br_s[pl.ds(r0, nseg), :] = nr
            bi_s[pl.ds(r0, nseg), :] = ni
            return nr, ni

        zero = jnp.zeros((nseg, blk), F32)
        fr, fi = lax.fori_loop(0, seg, local, (zero, zero), unroll=True)

        pr, pi = ar_ref[c], ai_ref[c]
        for _ in range(int(math.log2(seg))):
            pr, pi = _cmul(pr, pi, pr, pi)
        cr, ci = st_r[c], st_i[c]
        rows_r, rows_i = [], []
        for s in range(nseg):
            rows_r.append(cr)
            rows_i.append(ci)
            tr, ti = _cmul(pr, pi, cr, ci)
            cr, ci = tr + fr[s:s + 1], ti + fi[s:s + 1]
        st_r[c] = cr
        st_i[c] = ci
        fr_ref[0, c] = cr
        fi_ref[0, c] = ci
        dr, di = _cmul(ar, ai, jnp.concatenate(rows_r, axis=0), jnp.concatenate(rows_i, axis=0))

        def fix(j, carry):
            dr, di = carry
            r0 = pl.multiple_of(j * nseg, nseg)
            br_s[pl.ds(r0, nseg), :] = br_s[pl.ds(r0, nseg), :] + dr
            bi_s[pl.ds(r0, nseg), :] = bi_s[pl.ds(r0, nseg), :] + di
            return _cmul(ar, ai, dr, di)

        lax.fori_loop(0, seg, fix, (dr, di), unroll=True)

        y = (_dot(br_s[...].astype(BF16), wc_ref[c, :blk, :])
             - _dot(bi_s[...].astype(BF16), wc_ref[c, blk:, :]))
        y = y + d_ref[:, lo:hi] * uc
        u_s[:, lo:hi] = _gelu_tanh(y)

    z_ref[0] = _dot_sel_lhs(unperm_ref[...], u_s[...]).astype(z_ref.dtype)


def _s5_prompt(x, shift, scale, g, d_skip, ar, ai, wb, wc):
    bsz, seq, d = x.shape
    nblk, _, blk = ar.shape
    nseg, seg = SUBLANE, 32
    tm = nseg * seg
    assert seq % tm == 0
    perm = np.zeros((tm, tm), np.float32)
    for s in range(nseg):
        for j in range(seg):
            perm[j * nseg + s, s * seg + j] = 1.0
    unperm = jnp.asarray(perm.T, BF16)
    perm = jnp.asarray(perm, BF16)
    z, fr, fi = pl.pallas_call(
        functools.partial(_s5_prompt_kernel, nseg=nseg, seg=seg),
        grid=(bsz, seq // tm),
        in_specs=[pl.BlockSpec((1, tm, d), lambda b, i: (b, i, 0)),
                  _row_block(shift, tm), _row_block(scale, tm),
                  _const2((1, d)), _const2((1, d)),
                  _const2(ar.shape), _const2(ai.shape), _const2(wb.shape), _const2(wc.shape),
                  _const2((tm, tm)), _const2((tm, tm))],
        out_specs=[pl.BlockSpec((1, tm, d), lambda b, i: (b, i, 0)),
                   pl.BlockSpec((1, nblk, 1, blk), lambda b, i: (b, 0, 0, 0)),
                   pl.BlockSpec((1, nblk, 1, blk), lambda b, i: (b, 0, 0, 0))],
        out_shape=[jax.ShapeDtypeStruct((bsz, seq, d), BF16),
                   jax.ShapeDtypeStruct((bsz, nblk, 1, blk), F32),
                   jax.ShapeDtypeStruct((bsz, nblk, 1, blk), F32)],
        scratch_shapes=[pltpu.VMEM((tm, d), F32), pltpu.VMEM((tm, blk), F32), pltpu.VMEM((tm, blk), F32),
                        pltpu.VMEM((nblk, 1, blk), F32), pltpu.VMEM((nblk, 1, blk), F32)],
        compiler_params=_cparams(2),
        name="s5_prompt",
    )(x, shift, scale, g.reshape(1, d), d_skip.reshape(1, d), ar, ai, wb, wc, perm, unperm)
    return z, fr, fi


def _s5_sample_kernel(x_ref, sh_ref, sc_ref, g_ref, d_ref, ar_ref, ai_ref, wb_ref, wc_ref,
                      s0r_ref, s0i_ref, z_ref, fr_ref, fi_ref, *, nb, steps):
    nblk = wb_ref.shape[0]
    blk = ar_ref.shape[-1]
    u = _rms_mod(x_ref[...], g_ref[...], sh_ref[...], sc_ref[...])
    for c in range(nblk):
        lo, hi = c * S5_CH, (c + 1) * S5_CH
        uc = u[:, lo:hi]
        bu = _dot(uc.astype(BF16), wb_ref[c])
        ar = jnp.broadcast_to(ar_ref[c], (nb, blk))
        ai = jnp.broadcast_to(ai_ref[c], (nb, blk))
        xr, xi = s0r_ref[:, c * blk:(c + 1) * blk], s0i_ref[:, c * blk:(c + 1) * blk]
        xrs, xis = [], []
        for t in range(steps):
            pr, pi = _cmul(ar, ai, xr, xi)
            xr = pr + bu[t * nb:(t + 1) * nb, :blk]
            xi = pi + bu[t * nb:(t + 1) * nb, blk:]
            xrs.append(xr)
            xis.append(xi)
        fr_ref[:, c * blk:(c + 1) * blk] = xr
        fi_ref[:, c * blk:(c + 1) * blk] = xi
        y = (_dot(jnp.concatenate(xrs, axis=0).astype(BF16), wc_ref[c, :blk, :])
             - _dot(jnp.concatenate(xis, axis=0).astype(BF16), wc_ref[c, blk:, :]))
        y = y + d_ref[:, lo:hi] * uc
        z_ref[:, lo:hi] = _gelu_tanh(y)


def _s5_sample(x_tb, shift_tb, scale_tb, g, d_skip, ar, ai, wb, wc, s0r, s0i, nb, steps):
    rows, d = x_tb.shape
    nblk, _, blk = ar.shape
    return pl.pallas_call(
        functools.partial(_s5_sample_kernel, nb=nb, steps=steps),
        out_shape=[jax.ShapeDtypeStruct((rows, d), F32),
                   jax.ShapeDtypeStruct((nb, nblk * blk), F32),
                   jax.ShapeDtypeStruct((nb, nblk * blk), F32)],
        compiler_params=pltpu.CompilerParams(vmem_limit_bytes=VMEM_LIMIT),
        name="s5_sample",
    )(x_tb, shift_tb, scale_tb, g.reshape(1, d), d_skip.reshape(1, d), ar, ai, wb, wc, s0r, s0i)


def _shift_rows(u, prev, k):
    if k == 0:
        return u
    rolled = pltpu.roll(u, k, 0)
    head = pltpu.roll(prev, k, 0)
    row = lax.broadcasted_iota(I32, (SUBLANE, u.shape[1]), 0)
    first = jnp.where(row < k, head, rolled[:SUBLANE])
    return jnp.concatenate([first, rolled[SUBLANE:]], axis=0)


def _sconv_prompt_kernel(x_ref, sh_ref, sc_ref, g_ref, wi_ref, gate_ref, buf_ref, wc_ref, wo_ref,
                         y_ref, nb_ref, prev_s, *, width):
    i = pl.program_id(1)
    d = x_ref.shape[-1]

    @pl.when(i == 0)
    def _():
        prev_s[...] = jnp.zeros(prev_s.shape, F32)
        prev_s[SUBLANE - (width - 1):, :] = buf_ref[0]

    h = _rms_mod(x_ref[0], g_ref[...], sh_ref[0], sc_ref[0]).astype(BF16)
    gb = _dot(h, wi_ref[:, :d])
    u = _dot(h, wi_ref[:, d:2 * d]) * _dot(h, wi_ref[:, 2 * d:])
    prev = prev_s[...]
    conv = u * wc_ref[width - 1:width, :]
    for k in range(1, width):
        conv = conv + _shift_rows(u, prev, k) * wc_ref[width - 1 - k:width - k, :]
    prev_s[...] = u[u.shape[0] - SUBLANE:, :]
    nb_ref[0] = u[u.shape[0] - (width - 1):, :]
    y_ref[0] = x_ref[0] + gate_ref[0] * _dot((gb * conv).astype(BF16), wo_ref[...])


def _sconv_prompt(x, shift, scale, g, w_in_bf16, gate, buf, w_conv, w_out_bf16, tm):
    bsz, seq, d = x.shape
    width = w_conv.shape[0]
    return pl.pallas_call(
        functools.partial(_sconv_prompt_kernel, width=width),
        grid=(bsz, seq // tm),
        in_specs=[pl.BlockSpec((1, tm, d), lambda b, i: (b, i, 0)),
                  _row_block(shift, tm), _row_block(scale, tm), _const2((1, d)),
                  _resident(w_in_bf16.shape), _row_block(gate, tm),
                  pl.BlockSpec((1, width - 1, d), lambda b, i: (b, 0, 0)),
                  _const2((width, d)), _resident((d, d))],
        out_specs=[pl.BlockSpec((1, tm, d), lambda b, i: (b, i, 0)),
                   pl.BlockSpec((1, width - 1, d), lambda b, i: (b, 0, 0))],
        out_shape=[jax.ShapeDtypeStruct((bsz, seq, d), F32),
                   jax.ShapeDtypeStruct((bsz, width - 1, d), F32)],
        scratch_shapes=[pltpu.VMEM((SUBLANE, d), F32)],
        compiler_params=_cparams(2),
        name="sconv_prompt",
    )(x, shift, scale, g.reshape(1, d), w_in_bf16, gate, buf, w_conv, w_out_bf16)


def _sconv_sample_kernel(p_ref, x_ref, gate_ref, buf_ref, wc_ref, wo_ref, y_ref, nb_ref,
                         *, width, nb, steps):
    d = x_ref.shape[-1]
    p = p_ref[...]
    gb, gc, xh = p[:, :d], p[:, d:2 * d], p[:, 2 * d:]
    u = gc * xh
    ext = jnp.concatenate([buf_ref[...], u], axis=0)
    conv = ext[0:steps * nb] * wc_ref[0:1, :]
    for k in range(1, width):
        conv = conv + ext[k * nb:(k + steps) * nb] * wc_ref[k:k + 1, :]
    nb_ref[...] = ext[steps * nb:]
    y_ref[...] = x_ref[...] + gate_ref[...] * _dot((gb * conv).astype(BF16), wo_ref[...])


def _sconv_sample(p_tb, x_tb, gate_tb, buf_kb, w_conv, w_out_bf16, nb, steps):
    rows, d = x_tb.shape
    width = w_conv.shape[0]
    return pl.pallas_call(
        functools.partial(_sconv_sample_kernel, width=width, nb=nb, steps=steps),
        out_shape=[jax.ShapeDtypeStruct((rows, d), F32),
                   jax.ShapeDtypeStruct(((width - 1) * nb, d), F32)],
        compiler_params=pltpu.CompilerParams(vmem_limit_bytes=VMEM_LIMIT),
        name="sconv_sample",
    )(p_tb, x_tb, gate_tb, buf_kb, w_conv, w_out_bf16)


KEY_NEG_INF = -2139095041
KEY_POS_INF = 2139095040


def _code_to_f32(key):
    key = jnp.clip(key, KEY_NEG_INF, KEY_POS_INF)
    return lax.bitcast_convert_type(key ^ ((key >> 31) & 0x7FFFFFFF), F32)


def _stack_heads(qcat):
    w = 4 * IDX_DIM
    return jnp.concatenate([qcat[:, h * w:(h + 1) * w] for h in range(IDX_HEADS)], axis=0)


def _idx_rhs(ki):
    kh, kl = _split(ki)
    return jnp.concatenate([kh, kh, kl, jnp.zeros(kh.shape, BF16)], axis=1)


def _idx_score(dts, wcols, nrows):
    s = wcols[0] * jnp.maximum(dts[0:nrows], 0.0)
    for h in range(1, IDX_HEADS):
        s = s + wcols[h] * jnp.maximum(dts[h * nrows:(h + 1) * nrows], 0.0)
    return s


def _idx_wcols(kw):
    sc = (IDX_HEADS ** -0.5) * (IDX_DIM ** -0.5)
    return [kw[:, IDX_DIM + h:IDX_DIM + h + 1] * sc for h in range(IDX_HEADS)]


def _topk_select(count, rows, lanes, topk, idx_bits, stash=lambda v: (lambda: v)):
    wide = lambda v: jnp.broadcast_to(v, (rows, lanes))

    def bit_step(it, tu):
        mask = jnp.left_shift(jnp.int32(1), 31 - it)
        cand_u = tu | mask
        cand = stash(wide(_code_to_f32(cand_u ^ INT_MIN)))
        cnt = count(lambda c, sc, idx: jnp.where(sc >= cand(), 1.0, 0.0))
        return jnp.where(cnt >= topk, cand_u, tu)

    t = _code_to_f32(lax.fori_loop(0, 32, bit_step, jnp.zeros((rows, 1), I32)) ^ INT_MIN)
    tw = wide(t)
    n_gt = count(lambda c, sc, idx: jnp.where(sc > tw, 1.0, 0.0))
    n_ge = count(lambda c, sc, idx: jnp.where(sc >= tw, 1.0, 0.0))
    need = topk - n_gt
    tied = jnp.max(jnp.where(t > -jnp.inf, n_ge - topk, 0.0)) > 0.0

    def tie_search(_):
        def idx_step(it, j):
            cand = j | jnp.left_shift(jnp.int32(1), idx_bits - 1 - it)
            cw = wide(cand)
            cnt = count(lambda c, sc, idx: jnp.where((sc == tw) & (idx < cw), 1.0, 0.0))
            return jnp.where(cnt < need, cand, j)
        return lax.fori_loop(0, idx_bits, idx_step, jnp.zeros((rows, 1), I32))

    j = lax.cond(tied, tie_search, lambda _: jnp.full((rows, 1), 2 ** idx_bits, I32), 0)
    return t, j


def _dsa_prompt_kernel(q_ref, qcat_ref, kwq_ref, kcat_ref, kaug_ref, vaug_ref, bias_ref, o_ref,
                       sc_s, madd_s, wbc_s, cand_s, qa_s, lg_s, mx_s, sh_s, acc_s, *, tq, topk, idx_bits):
    i = pl.program_id(1)
    nk = i + 1
    nk2 = (nk + 1) // 2
    r = N_HEADS // N_KV_HEADS

    for h, w in enumerate(_idx_wcols(kwq_ref[0])):
        wbc_s[h] = jnp.broadcast_to(w, (tq, tq))
    qrow = lax.broadcasted_iota(I32, (tq, tq), 0)
    kcol = lax.broadcasted_iota(I32, (tq, tq), 1)
    qpos = i * tq + qrow
    wq = 4 * IDX_DIM

    def score_pair(c2, _):
        kc = kcat_ref[0, c2]
        s = None
        hpd = 4
        for hp in range(IDX_HEADS // hpd):
            lhs = jnp.concatenate([qcat_ref[0, :, (hpd * hp + v) * wq:(hpd * hp + v + 1) * wq] for v in range(hpd)],
                                  axis=0)
            dd = jnp.maximum(_dot(lhs, kc), 0.0)
            for v in range(hpd):
                w = wbc_s[hpd * hp + v]
                d = jnp.concatenate([w, w], axis=1) * dd[v * tq:(v + 1) * tq]
                s = d if s is None else s + d
        for u in range(2):
            c = 2 * c2 + u
            sc_s[c] = jnp.where(c * tq + kcol <= qpos, s[:, u * tq:(u + 1) * tq], -jnp.inf)
        return 0

    lax.fori_loop(0, nk2, score_pair, 0)

    def count(fn):
        def body(c2, acc):
            c = 2 * c2
            acc = acc + fn(c, sc_s[c], c * tq + kcol)
            return acc + fn(c + 1, sc_s[c + 1], (c + 1) * tq + kcol)
        acc = lax.fori_loop(0, nk2, body, jnp.zeros((tq, tq), F32))
        return jnp.sum(acc, axis=1, keepdims=True)

    def stash(v):
        cand_s[...] = v
        return lambda: cand_s[...]

    t, j = _topk_select(count, tq, tq, float(topk), idx_bits, stash)
    tw = jnp.broadcast_to(t, (tq, tq))
    jw = jnp.broadcast_to(j, (tq, tq))

    def mask_pair(c2, _):
        for u in range(2):
            c = 2 * c2 + u
            sc = sc_s[c]
            sel = (sc > tw) | ((sc == tw) & (c * tq + kcol <= jw))
            madd_s[c] = jnp.where(sel & (sc > -jnp.inf), 0.0, NEG_BIG)
        return 0

    lax.fori_loop(0, nk2, mask_pair, 0)

    q = q_ref[0].astype(F32)
    zpad = jnp.zeros((tq, LANE - HEAD_DIM), F32)
    for g in range(N_KV_HEADS):
        qa_s[g] = jnp.concatenate(
            [jnp.concatenate([q[:, (g * r + u) * HEAD_DIM:(g * r + u + 1) * HEAD_DIM], zpad], axis=1)
             for u in range(r)], axis=0).astype(BF16)

    def pair_mask(c2, near, g):
        halves = []
        for u in range(2):
            c = 2 * c2 + u
            m = madd_s[c]
            if near:
                back = nk - 1 - c
                carries = (back == 0) | ((back == 1) & (kcol > qrow))
                halves.append([m + jnp.where(carries, bias_ref[g * r + v], 0.0) for v in range(r)])
            else:
                halves.append([m] * r)
        return jnp.concatenate([jnp.concatenate([halves[0][v], halves[1][v]], axis=1) for v in range(r)], axis=0)

    ngh = lg_s.shape[0]
    for half in range(N_KV_HEADS // ngh):
        groups = [half * ngh + k for k in range(ngh)]
        mx_s[...] = jnp.full(mx_s.shape, NEG_BIG, F32)

        def logits_pair(c2, near):
            for k, g in enumerate(groups):
                lg = _dot(qa_s[g], kaug_ref[0, c2, g * LANE:(g + 1) * LANE, :]) + pair_mask(c2, near, g)
                lg_s[k, c2] = lg
                mx_s[k] = jnp.maximum(mx_s[k], jnp.maximum(lg[:, :tq], lg[:, tq:]))

        def far(c4, _):
            logits_pair(2 * c4, False)
            logits_pair(2 * c4 + 1, False)
            return 0

        nfar = jnp.maximum(nk2 - 2, 0)
        lax.fori_loop(0, nfar // 2, far, 0)

        @pl.when(nfar % 2 == 1)
        def _():
            logits_pair(nfar - 1, False)

        @pl.when(nk2 >= 2)
        def _():
            logits_pair(nk2 - 2, True)

        logits_pair(nk2 - 1, True)

        for k in range(ngh):
            sh_s[k] = jnp.broadcast_to(jnp.max(mx_s[k], axis=1, keepdims=True), (r * tq, LANE))
        acc_s[...] = jnp.zeros(acc_s.shape, F32)

        def pv_pair(c2):
            r0 = pl.multiple_of(c2 * 2 * tq, 2 * tq)
            va = vaug_ref[0, pl.ds(r0, 2 * tq), :]
            for k, g in enumerate(groups):
                s = sh_s[k]
                p = jnp.exp(lg_s[k, c2] - jnp.concatenate([s, s], axis=1))
                acc_s[k] = acc_s[k] + _dot(p.astype(BF16), va[:, g * LANE:(g + 1) * LANE])

        def pv_two(c4, _):
            pv_pair(2 * c4)
            pv_pair(2 * c4 + 1)
            return 0

        lax.fori_loop(0, nk2 // 2, pv_two, 0)

        @pl.when(nk2 % 2 == 1)
        def _():
            pv_pair(nk2 - 1)
        for k, g in enumerate(groups):
            acc = acc_s[k]
            o = acc[:, 0:HEAD_DIM] / acc[:, HEAD_DIM:HEAD_DIM + 1]
            heads = jnp.concatenate([o[u * tq:(u + 1) * tq, :] for u in range(r)], axis=1)
            o_ref[0, :, g * r * HEAD_DIM:(g + 1) * r * HEAD_DIM] = heads.astype(o_ref.dtype)


def _t5_bucket(dist):
    n = jnp.maximum(dist, 0)
    max_exact = N_BUCKETS // 2
    nf = jnp.maximum(n, max_exact).astype(F32)
    large = max_exact + (jnp.log(nf / max_exact) / math.log(MAX_DISTANCE / max_exact)
                         * (N_BUCKETS - max_exact)).astype(I32)
    large = jnp.minimum(large, N_BUCKETS - 1)
    return jnp.where(n < max_exact, n, large)


def _bias_table(rel_bias, n):
    tab = rel_bias[_t5_bucket(jnp.arange(n, dtype=I32))]
    return (tab - rel_bias[N_BUCKETS - 1][None, :]).T


DSA_GROUPS_PER_PASS = 2


def _dsa_prompt(q, qcat, kw, kcat, kaug, vaug, rel_bias, tq):
    bsz, seq, nq = q.shape
    topk = min(TOPK_MAX, seq // 4)
    nkc = seq // tq
    r = N_HEADS // N_KV_HEADS
    assert tq == MAX_DISTANCE and seq % (2 * tq) == 0
    tab = _bias_table(rel_bias, tq)
    dmod = (np.arange(tq)[:, None] - np.arange(tq)[None, :]) % tq
    onehot = (jnp.arange(tq, dtype=I32)[:, None, None] == jnp.asarray(dmod, I32)[None]).astype(F32)
    bias = jnp.einsum('hd,dqk->hqk', tab, onehot, precision=lax.Precision.HIGHEST)
    assert kcat.shape[1:] == (seq // (2 * tq), 4 * IDX_DIM, 2 * tq)
    row = lambda n: pl.BlockSpec((1, tq, n), lambda b, i: (b, i, 0))
    tiles = lambda a: pl.BlockSpec((1,) + a.shape[1:], lambda b, i: (b, 0, 0, 0), pipeline_mode=pl.Buffered(1))
    return pl.pallas_call(
        functools.partial(_dsa_prompt_kernel, tq=tq, topk=topk, idx_bits=int(math.log2(seq))),
        grid=(bsz, nkc),
        in_specs=[row(nq), row(qcat.shape[-1]), row(LANE), tiles(kcat), tiles(kaug),
                  pl.BlockSpec((1, seq, vaug.shape[-1]), lambda b, i: (b, 0, 0), pipeline_mode=pl.Buffered(1)),
                  _resident((N_HEADS, tq, tq))],
        out_specs=row(nq),
        out_shape=jax.ShapeDtypeStruct((bsz, seq, nq), BF16),
        scratch_shapes=[pltpu.VMEM((nkc, tq, tq), F32), pltpu.VMEM((nkc, tq, tq), F32),
                        pltpu.VMEM((IDX_HEADS, tq, tq), F32), pltpu.VMEM((tq, tq), F32),
                        pltpu.VMEM((N_KV_HEADS, r * tq, LANE), BF16),
                        pltpu.VMEM((DSA_GROUPS_PER_PASS, nkc // 2, r * tq, 2 * tq), F32),
                        pltpu.VMEM((DSA_GROUPS_PER_PASS, r * tq, tq), F32),
                        pltpu.VMEM((DSA_GROUPS_PER_PASS, r * tq, LANE), F32),
                        pltpu.VMEM((DSA_GROUPS_PER_PASS, r * tq, LANE), F32)],
        compiler_params=_cparams(2),
        name="dsa_prompt",
    )(q, qcat, kw, kcat, kaug, vaug, bias)


QP = SUBLANE
PAGES_PER_STEP = 64


def _page_specs(shape4, layer, pps):
    def spec(u):
        return pl.BlockSpec((1, 1) + tuple(shape4[2:]),
                            lambda b, g, pt: (layer, pt[b, g * pps + u], 0, 0))
    return [spec(u) for u in range(pps)]


def _pages_t(cache):
    nd = cache.ndim
    t = jnp.transpose(cache, (0, 1) + tuple(range(3, nd)) + (2,))
    return t.reshape(t.shape[0], t.shape[1], -1, t.shape[-1])


def _pad_t(a, page):
    return jnp.concatenate([a, jnp.zeros((page - QP, a.shape[1]), F32)], axis=0).T


def _dsa_sample_score_kernel(pt_ref, qcat_ref, kw_ref, kcn_ref, *rest, pps):
    page_refs, o_ref, on_ref = rest[:pps], rest[pps], rest[pps + 1]
    lhs = _stack_heads(qcat_ref[0]).astype(BF16)
    wcols = _idx_wcols(kw_ref[0])
    kt = jnp.concatenate([page_refs[u][0, 0] for u in range(pps)], axis=1)
    kh, kl = _split(kt)
    rhs = jnp.concatenate([kh, kh, kl, jnp.zeros(kh.shape, BF16)], axis=0)
    o_ref[0] = _idx_score(_dot(lhs, rhs), wcols, QP)

    @pl.when(pl.program_id(1) == pl.num_programs(1) - 1)
    def _():
        on_ref[0] = _idx_score(_dot(lhs, _pad_t(kcn_ref[0], on_ref.shape[-1]).astype(BF16)), wcols, QP)


def _dsa_sample_scores(page_table, qcat, kw, kcn, kidx_t, layer, pps):
    nb, n_pages = page_table.shape
    page = kidx_t.shape[-1]
    bmap = lambda b, g, pt: (b, 0, 0)
    grid_spec = pltpu.PrefetchScalarGridSpec(
        num_scalar_prefetch=1,
        grid=(nb, n_pages // pps),
        in_specs=[pl.BlockSpec((1, QP, qcat.shape[-1]), bmap), pl.BlockSpec((1, QP, LANE), bmap),
                  pl.BlockSpec((1, QP, kcn.shape[-1]), bmap)] + _page_specs(kidx_t.shape, layer, pps),
        out_specs=[pl.BlockSpec((1, QP, pps * page), lambda b, g, pt: (b, 0, g)),
                   pl.BlockSpec((1, QP, page), bmap)])
    return pl.pallas_call(
        functools.partial(_dsa_sample_score_kernel, pps=pps),
        grid_spec=grid_spec,
        out_shape=[jax.ShapeDtypeStruct((nb, QP, n_pages * page), F32),
                   jax.ShapeDtypeStruct((nb, QP, page), F32)],
        compiler_params=_cparams(2),
        name="dsa_sample_scores",
    )(page_table, qcat, kw, kcn, *([kidx_t] * pps))


def _new_token_scores(s_new, steps):
    lane = lax.broadcasted_iota(I32, s_new.shape, 1)
    q = lax.broadcasted_iota(I32, s_new.shape, 0) % QP
    return jnp.where((lane <= q) & (lane < steps), s_new, -jnp.inf)


def _dsa_sample_select_kernel(sc_ref, sn_ref, t_ref, j_ref, *, steps, topk, idx_bits):
    scores = jnp.concatenate([sc_ref[...], _new_token_scores(sn_ref[...], steps)], axis=1)
    idx = lax.broadcasted_iota(I32, scores.shape, 1)
    count = lambda fn: jnp.sum(fn(0, scores, idx), axis=1, keepdims=True)
    t, j = _topk_select(count, scores.shape[0], scores.shape[1], float(topk), idx_bits)
    t_ref[...] = t
    j_ref[...] = j


def _dsa_sample_select(scores, snew, steps, topk):
    nb, _, past = scores.shape
    page = snew.shape[-1]
    rows = nb * QP
    rb = math.gcd(rows, 256)
    t, j = pl.pallas_call(
        functools.partial(_dsa_sample_select_kernel, steps=steps, topk=topk,
                          idx_bits=int(math.ceil(math.log2(past + page)))),
        grid=(rows // rb,),
        in_specs=[pl.BlockSpec((rb, past), lambda i: (i, 0)), pl.BlockSpec((rb, page), lambda i: (i, 0))],
        out_specs=[pl.BlockSpec((rb, 1), lambda i: (i, 0)), pl.BlockSpec((rb, 1), lambda i: (i, 0))],
        out_shape=[jax.ShapeDtypeStruct((rows, 1), F32), jax.ShapeDtypeStruct((rows, 1), I32)],
        compiler_params=_cparams(1),
        name="dsa_sample_select",
    )(scores.reshape(rows, past), snew.reshape(rows, page))
    return t.reshape(nb, QP, 1), j.reshape(nb, QP, 1)


def _dsa_sample_attend_kernel(pt_ref, q_ref, kn_ref, vn_ref, sc_ref, sn_ref, t_ref, j_ref,
                              blast_ref, bnew_ref, *rest, pps, n_pages, steps):
    k_refs, v_refs, o_ref = rest[:pps], rest[pps:2 * pps], rest[2 * pps]
    lhs_s, m_s, l_s, acc_s = rest[2 * pps + 1:]
    g = pl.program_id(1)
    last = pl.num_programs(1) - 1
    page = k_refs[0].shape[-1]
    past = n_pages * page
    n = pps * page
    r = N_HEADS // N_KV_HEADS
    grow = r * QP

    @pl.when(g == 0)
    def _():
        m_s[...] = jnp.full(m_s.shape, NEG_BIG, F32)
        l_s[...] = jnp.zeros(l_s.shape, F32)
        acc_s[...] = jnp.zeros(acc_s.shape, F32)
        q = q_ref[0]
        for gg in range(N_KV_HEADS):
            lhs_s[gg] = jnp.concatenate(
                [q[:, (gg * r + u) * HEAD_DIM:(gg * r + u + 1) * HEAD_DIM] for u in range(r)], axis=0).astype(BF16)

    t, j = t_ref[0], j_ref[0]

    def select(sc, idx):
        sel = (sc > t) | ((sc == t) & (idx <= j))
        return jnp.where(sel & (sc > -jnp.inf), 0.0, NEG_BIG)

    def attend(kt, vt, madd, bias):
        mrows = jnp.concatenate([madd] * r, axis=0)
        for gg in range(N_KV_HEADS):
            rows = slice(gg * grow, (gg + 1) * grow)
            feat = slice(gg * HEAD_DIM, (gg + 1) * HEAD_DIM)
            lg = _dot(lhs_s[gg], kt[feat, :]) + mrows
            if bias is not None:
                lg = lg + bias[rows, :]
            m_old = m_s[rows, :]
            m_new = jnp.maximum(m_old, jnp.max(lg, axis=1, keepdims=True))
            p = jnp.exp(lg - m_new)
            alpha = jnp.exp(m_old - m_new)
            l_s[rows, :] = alpha * l_s[rows, :] + jnp.sum(p, axis=1, keepdims=True)
            acc_s[rows, :] = alpha * acc_s[rows, :] + _dot_nt(p.astype(BF16), vt[feat, :])
            m_s[rows, :] = m_new

    def pages(bias):
        kt = jnp.concatenate([k_refs[u][0, 0] for u in range(pps)], axis=1).astype(BF16)
        vt = jnp.concatenate([v_refs[u][0, 0] for u in range(pps)], axis=1).astype(BF16)
        idx = g * n + lax.broadcasted_iota(I32, (QP, n), 1)
        attend(kt, vt, select(sc_ref[0], idx), bias)

    @pl.when(g != last)
    def _():
        pages(None)

    @pl.when(g == last)
    def _():
        pages(jnp.concatenate([jnp.zeros((N_HEADS * QP, n - page), F32), blast_ref[...]], axis=1))
        idx = past + lax.broadcasted_iota(I32, (QP, page), 1)
        attend(_pad_t(kn_ref[0], page).astype(BF16), _pad_t(vn_ref[0], page).astype(BF16),
               select(_new_token_scores(sn_ref[0], steps), idx), bnew_ref[...])
        o = acc_s[...] / l_s[...]
        for h in range(N_HEADS):
            o_ref[0, :, h * HEAD_DIM:(h + 1) * HEAD_DIM] = o[h * QP:(h + 1) * QP, :]


def _dsa_sample(page_table, q, kn, vn, scores, snew, k_t, v_t, rel_bias, layer, steps, pps):
    nb, n_pages = page_table.shape
    page = k_t.shape[-1]
    past = n_pages * page
    nq, nkv = N_HEADS * HEAD_DIM, N_KV_HEADS * HEAD_DIM
    t, j = _dsa_sample_select(scores, snew, steps, min(TOPK_MAX, (past + steps) // 4))
    n_steps = n_pages // pps
    r = N_HEADS // N_KV_HEADS
    tab = _bias_table(rel_bias, 2 * page)
    qq = np.arange(QP)[:, None]
    off = np.arange(page)[None, :]
    blast = tab[:, page + qq - off].reshape(N_HEADS * QP, page)
    bnew = tab[:, np.maximum(qq - off, 0)].reshape(N_HEADS * QP, page)
    bmap = lambda b, g, pt: (b, 0, 0)
    grid_spec = pltpu.PrefetchScalarGridSpec(
        num_scalar_prefetch=1,
        grid=(nb, n_steps),
        in_specs=[pl.BlockSpec((1, QP, nq), bmap),
                  pl.BlockSpec((1, QP, nkv), bmap),
                  pl.BlockSpec((1, QP, nkv), bmap),
                  pl.BlockSpec((1, QP, pps * page), lambda b, g, pt: (b, 0, g)),
                  pl.BlockSpec((1, QP, page), bmap),
                  pl.BlockSpec((1, QP, 1), bmap), pl.BlockSpec((1, QP, 1), bmap),
                  pl.BlockSpec(blast.shape, lambda b, g, pt: (0, 0)),
                  pl.BlockSpec(bnew.shape, lambda b, g, pt: (0, 0))]
                 + _page_specs(k_t.shape, layer, pps) + _page_specs(v_t.shape, layer, pps),
        out_specs=pl.BlockSpec((1, QP, nq), bmap),
        scratch_shapes=[pltpu.VMEM((N_KV_HEADS, r * QP, HEAD_DIM), BF16),
                        pltpu.VMEM((N_HEADS * QP, 1), F32), pltpu.VMEM((N_HEADS * QP, 1), F32),
                        pltpu.VMEM((N_HEADS * QP, HEAD_DIM), F32)])
    return pl.pallas_call(
        functools.partial(_dsa_sample_attend_kernel, pps=pps, n_pages=n_pages, steps=steps),
        grid_spec=grid_spec,
        out_shape=jax.ShapeDtypeStruct((nb, QP, nq), F32),
        compiler_params=_cparams(2),
        name="dsa_sample_attend",
    )(page_table, q, kn, vn, scores, snew, t, j, blast, bnew, *([k_t] * pps), *([v_t] * pps))


SSD_COLS = 512


def _softplus(x):
    return jnp.maximum(x, 0.0) + jnp.log1p(jnp.exp(-jnp.abs(x)))


def _split3(a):
    p1 = a.astype(BF16)
    r1 = a - p1.astype(F32)
    p2 = r1.astype(BF16)
    return p1, p2, (r1 - p2.astype(F32)).astype(BF16)


def _dot_sel_rhs(a, e):
    p1, p2, p3 = _split3(a)
    return _dot(p1, e) + (_dot(p2, e) + _dot(p3, e))


def _dot_sel_lhs(e, a):
    p1, p2, p3 = _split3(a)
    return _dot(e, p1) + (_dot(e, p2) + _dot(e, p3))


def _ssd_prompt_kernel(z_ref, xbc_ref, dtr_ref, cw_ref, cb_ref, dtb_ref, alog_ref, dexp_ref, ng_ref,
                       e_ref, y_ref, nbuf_ref, hf_ref, prev_s, xc_s, y_s, h_s, *, width, nh, inner):
    i = pl.program_id(1)
    last = pl.num_programs(1) - 1
    qn, cdim = xbc_ref.shape[1], xbc_ref.shape[2]
    ngrp, ns, hd = SSD_GROUPS, SSD_STATE, SSD_HEADDIM
    hpg = nh // ngrp

    @pl.when(i == 0)
    def _():
        prev_s[...] = jnp.zeros(prev_s.shape, F32)
        h_s[...] = jnp.zeros(h_s.shape, F32)

    for cb in range(cdim // SSD_COLS):
        cols = slice(cb * SSD_COLS, (cb + 1) * SSD_COLS)
        xr = xbc_ref[0, :, cols]
        prev = prev_s[:, cols]
        conv = xr * cw_ref[width - 1:width, cols]
        for k in range(1, width):
            conv = conv + _shift_rows(xr, prev, k) * cw_ref[width - 1 - k:width - k, cols]
        xc_s[:, cols] = _silu(conv + cb_ref[:, cols])
        prev_s[:, cols] = xr[qn - SUBLANE:, :]
        nbuf_ref[0, :, cols] = xr[qn - (width - 1):, :]

    hl = lax.broadcasted_iota(I32, (qn, LANE), 1)
    dt = jnp.where(hl < nh, _softplus(dtr_ref[0] + dtb_ref[...]), 0.0)
    dta = dt * (-jnp.exp(alog_ref[...]))
    qrow = lax.broadcasted_iota(I32, (qn, qn), 0)
    kcol = lax.broadcasted_iota(I32, (qn, qn), 1)
    causal = kcol <= qrow
    cum = _dot_sel_lhs(jnp.where(causal, 1.0, 0.0).astype(BF16), dta)
    cum_t = cum.T
    cum_last = cum[qn - 1:qn, :]
    ecum = jnp.exp(cum)
    dtw = dt * jnp.exp(cum_last - cum)
    e_last = jnp.exp(cum_last)
    lane_p = lax.broadcasted_iota(I32, (qn, LANE), 1)
    row_p = lax.broadcasted_iota(I32, (LANE, ns), 0)

    for g in range(ngrp):
        gl = slice(g * SSD_COLS, (g + 1) * SSD_COLS)
        e_g = e_ref[:, gl]
        xs_g = xc_s[:, gl]
        dtx_g = xs_g * _dot_sel_rhs(dt, e_g)
        dtxw_g = xs_g * _dot_sel_rhs(dtw, e_g)
        ecx_g = _dot_sel_rhs(ecum, e_g)
        bm = xc_s[:, inner + g * ns:inner + (g + 1) * ns].astype(BF16)
        cm = xc_s[:, inner + (ngrp + g) * ns:inner + (ngrp + g + 1) * ns].astype(BF16)
        cbm = _dot_nt(cm, bm)
        for pi in range(hpg // 2):
            ha = g * hpg + 2 * pi
            pls = slice(pi * LANE, (pi + 1) * LANE)
            lanes = slice(ha * hd, ha * hd + LANE)
            dtx_p = dtx_g[:, pls].astype(BF16)
            yds = []
            for h in (ha, ha + 1):
                seg = cum[:, h:h + 1] - cum_t[h:h + 1, :]
                dec = jnp.exp(jnp.where(causal, seg, -jnp.inf))
                yds.append(_dot((cbm * dec).astype(BF16), dtx_p))
            yd = jnp.where(lane_p < hd, yds[0], yds[1])
            hp = h_s[ha // 2]
            yo = _dot_nt(cm, hp.astype(BF16)) * ecx_g[:, pls]
            y_s[:, lanes] = yd + yo + dexp_ref[:, lanes] * xs_g[:, pls]
            s_new = _dot(dtxw_g[:, pls].T.astype(BF16), bm)
            cd = jnp.where(row_p < hd, e_last[:, ha:ha + 1], e_last[:, ha + 1:ha + 2])
            h_s[ha // 2] = hp * cd + s_new

    for g in range(ngrp):
        gl = slice(g * SSD_COLS, (g + 1) * SSD_COLS)
        yg = y_s[:, gl] * _silu(z_ref[0, :, gl])
        ms = jnp.mean(yg * yg, axis=-1, keepdims=True)
        y_ref[0, :, gl] = ((yg * lax.rsqrt(ms + EPS)) * ng_ref[:, gl]).astype(y_ref.dtype)

    @pl.when(i == last)
    def _():
        hf_ref[0] = h_s[...]


def _head_expand(nh, hd):
    e = np.zeros((LANE, nh * hd), np.float32)
    for h in range(nh):
        e[h, h * hd:(h + 1) * hd] = 1.0
    return jnp.asarray(e, BF16)


def _pad_lanes(v, n=LANE):
    v = v.reshape(1, -1)
    return jnp.pad(v, ((0, 0), (0, n - v.shape[1])))


def _ssd_prompt(z, xbc, dtr, conv_w, conv_b, dt_bias, a_log, d_skip, norm_g):
    bsz, seq, inner = z.shape
    cdim = xbc.shape[-1]
    nh = dt_bias.shape[0]
    width = conv_w.shape[0]
    qn = math.gcd(seq, SSD_CHUNK)
    assert qn == SSD_CHUNK and inner == SSD_GROUPS * SSD_COLS and cdim % SSD_COLS == 0
    dexp = jnp.repeat(d_skip, SSD_HEADDIM).reshape(1, inner)
    blk = lambda n: pl.BlockSpec((1, qn, n), lambda b, i: (b, i, 0))
    y, nbuf, hf = pl.pallas_call(
        functools.partial(_ssd_prompt_kernel, width=width, nh=nh, inner=inner),
        grid=(bsz, seq // qn),
        in_specs=[blk(inner), blk(cdim), blk(LANE),
                  _const2((width, cdim)), _const2((1, cdim)), _const2((1, LANE)), _const2((1, LANE)),
                  _const2((1, inner)), _const2((1, inner)), _const2((LANE, inner))],
        out_specs=[blk(inner),
                   pl.BlockSpec((1, width - 1, cdim), lambda b, i: (b, 0, 0)),
                   pl.BlockSpec((1, nh // 2, LANE, SSD_STATE), lambda b, i: (b, 0, 0, 0))],
        out_shape=[jax.ShapeDtypeStruct((bsz, seq, inner), BF16),
                   jax.ShapeDtypeStruct((bsz, width - 1, cdim), F32),
                   jax.ShapeDtypeStruct((bsz, nh // 2, LANE, SSD_STATE), F32)],
        scratch_shapes=[pltpu.VMEM((SUBLANE, cdim), F32), pltpu.VMEM((qn, cdim), F32),
                        pltpu.VMEM((qn, inner), F32), pltpu.VMEM((nh // 2, LANE, SSD_STATE), F32)],
        compiler_params=_cparams(2),
        name="ssd_prompt",
    )(z, xbc, dtr, conv_w, conv_b.reshape(1, cdim), _pad_lanes(dt_bias), _pad_lanes(a_log),
      dexp, norm_g.reshape(1, inner), _head_expand(nh, SSD_HEADDIM))
    return y, nbuf, hf.reshape(bsz, nh, SSD_HEADDIM, SSD_STATE)


def _ssd_prep_sample_kernel(xbc_ref, buf_ref, dtr_ref, cw_ref, cb_ref, dtb_ref, xc_ref, dt_ref, nbuf_ref,
                            *, width, nb, steps, nh):
    cdim = xbc_ref.shape[1]
    for cb in range(cdim // SSD_COLS):
        cols = slice(cb * SSD_COLS, (cb + 1) * SSD_COLS)
        ext = jnp.concatenate([buf_ref[:, cols], xbc_ref[:, cols]], axis=0)
        conv = ext[0:steps * nb] * cw_ref[0:1, cols]
        for k in range(1, width):
            conv = conv + ext[k * nb:(k + steps) * nb] * cw_ref[k:k + 1, cols]
        xc_ref[:, cols] = _silu(conv + cb_ref[:, cols])
        nbuf_ref[:, cols] = ext[steps * nb:]
    hl = lax.broadcasted_iota(I32, dtr_ref.shape, 1)
    dt_ref[...] = jnp.where(hl < nh, _softplus(dtr_ref[...] + dtb_ref[...]), 0.0)


def _ssd_prep_sample(xbc_tb, buf_kb, dtr_tb, conv_w, conv_b, dt_bias, nb, steps):
    rows, cdim = xbc_tb.shape
    width = conv_w.shape[0]
    return pl.pallas_call(
        functools.partial(_ssd_prep_sample_kernel, width=width, nb=nb, steps=steps, nh=dt_bias.shape[0]),
        out_shape=[jax.ShapeDtypeStruct((rows, cdim), F32),
                   jax.ShapeDtypeStruct((rows, LANE), F32),
                   jax.ShapeDtypeStruct(((width - 1) * nb, cdim), F32)],
        compiler_params=pltpu.CompilerParams(vmem_limit_bytes=VMEM_LIMIT),
        name="ssd_prep_sample",
    )(xbc_tb, buf_kb, dtr_tb, conv_w, conv_b.reshape(1, cdim), _pad_lanes(dt_bias))


def _ssd_scan_sample_kernel(xc_ref, dt_ref, z_ref, h0_ref, alog_ref, dexp_ref, ng_ref, e_ref,
                            y_ref, hf_ref, *, steps, inner):
    ngrp, ns = SSD_GROUPS, SSD_STATE
    xc, dt = xc_ref[0], dt_ref[0]
    row = lax.broadcasted_iota(I32, (QP, LANE), 0)
    cum = dt * (-jnp.exp(alog_ref[...]))
    for sft in (1, 2, 4):
        cum = cum + jnp.where(row >= sft, pltpu.roll(cum, sft, 0), 0.0)
    cum_last = cum[QP - 1:QP, :]
    parts = [dt, jnp.exp(cum), dt * jnp.exp(cum_last - cum), jnp.broadcast_to(jnp.exp(cum_last), (QP, LANE))]
    for s in range(steps):
        parts.append(jnp.where(row >= s, jnp.exp(cum - cum[s:s + 1, :]), 0.0))
    stack = jnp.concatenate(parts, axis=0)

    for g in range(ngrp):
        gl = slice(g * SSD_COLS, (g + 1) * SSD_COLS)
        ex = _dot_sel_rhs(stack, e_ref[:, gl])
        dt_x, ecum_x, dtw_x, el_x = (ex[k * QP:(k + 1) * QP] for k in range(4))
        xs_g = xc[:, gl]
        dtx = xs_g * dt_x
        bm = xc[:, inner + g * ns:inner + (g + 1) * ns]
        cm = xc[:, inner + (ngrp + g) * ns:inner + (ngrp + g + 1) * ns].astype(BF16)
        bmp = jnp.concatenate([bm, jnp.zeros((LANE - QP, ns), F32)], axis=0).astype(BF16)
        cbm = _dot_nt(cm, bmp)
        yd = jnp.zeros((QP, SSD_COLS), F32)
        for s in range(steps):
            yd = yd + (ex[(4 + s) * QP:(5 + s) * QP] * cbm[:, s:s + 1]) * dtx[s:s + 1, :]
        h0g = h0_ref[0, gl, :]
        yo = _dot_nt(cm, h0g.astype(BF16)) * ecum_x
        y = (yd + yo + dexp_ref[:, gl] * xs_g) * _silu(z_ref[0, :, gl])
        ms = jnp.mean(y * y, axis=-1, keepdims=True)
        y_ref[0, :, gl] = (y * lax.rsqrt(ms + EPS)) * ng_ref[:, gl]
        tm = jnp.concatenate([xs_g * dtw_x, el_x[0:1], jnp.zeros((LANE - QP - 1, SSD_COLS), F32)], axis=0)
        tt = tm.T
        hf_ref[0, gl, :] = h0g * tt[:, QP:QP + 1] + _dot(tt.astype(BF16), bmp)


def _ssd_scan_sample(xc, dt, z, h0, a_log, d_skip, norm_g, steps):
    nb, _, cdim = xc.shape
    inner = z.shape[-1]
    nh = a_log.shape[0]
    rows_h = nh * SSD_HEADDIM
    dexp = jnp.repeat(d_skip, SSD_HEADDIM).reshape(1, inner)
    one = lambda shape: pl.BlockSpec(shape, lambda b: (0,) * len(shape))
    return pl.pallas_call(
        functools.partial(_ssd_scan_sample_kernel, steps=steps, inner=inner),
        grid=(nb,),
        in_specs=[pl.BlockSpec((1, QP, cdim), lambda b: (b, 0, 0)),
                  pl.BlockSpec((1, QP, LANE), lambda b: (b, 0, 0)),
                  pl.BlockSpec((1, QP, inner), lambda b: (b, 0, 0)),
                  pl.BlockSpec((1, rows_h, SSD_STATE), lambda b: (b, 0, 0)),
                  one((1, LANE)), one((1, inner)), one((1, inner)), one((LANE, inner))],
        out_specs=[pl.BlockSpec((1, QP, inner), lambda b: (b, 0, 0)),
                   pl.BlockSpec((1, rows_h, SSD_STATE), lambda b: (b, 0, 0))],
        out_shape=[jax.ShapeDtypeStruct((nb, QP, inner), F32),
                   jax.ShapeDtypeStruct((nb, rows_h, SSD_STATE), F32)],
        compiler_params=_cparams(1),
        name="ssd_scan_sample",
    )(xc, dt, z, h0, _pad_lanes(a_log), dexp, norm_g.reshape(1, inner), _head_expand(nh, SSD_HEADDIM))


TM = 512


def _to_steps(a, nb, steps):
    n = a.shape[-1]
    return a.reshape(nb, steps, n).transpose(1, 0, 2).reshape(steps * nb, n)


def _to_batch(a, nb, steps):
    n = a.shape[-1]
    return a.reshape(steps, nb, n).transpose(1, 0, 2).reshape(1, nb * steps, n)


def _pad_q(a, nb, steps):
    n = a.shape[-1]
    return jnp.pad(a.reshape(nb, steps, n), ((0, 0), (0, QP - steps), (0, 0)))


def kernel(x_prompt, x_sample, cache_k, cache_v, cache_kidx, state_s5_re, state_s5_im, state_sconv, state_ssd, state_ssd_conv, page_table, c_prompt, c_sample, rel_bias, ada_w, ada_b, norm_mix, norm_mlp, norm_final, attn_w_in, attn_w_out, s5_lam_re, s5_lam_im, s5_log_dt, s5_b_re, s5_b_im, s5_c_re, s5_c_im, s5_d, s5_w_glu, sc_w_in, sc_w_conv, sc_w_out, ssd_w_in, ssd_conv_w, ssd_conv_b, ssd_dt_bias, ssd_a_log, ssd_d, ssd_norm, ssd_w_out, mlp_w1, mlp_w2):
    bp, seq, d = x_prompt.shape
    nb, steps, _ = x_sample.shape
    depth = ada_w.shape[0]
    n_mixers = 4
    rs = nb * steps
    tm = min(TM, seq)

    rows = bp + nb
    c_all = jnp.pad(jnp.concatenate([c_prompt, c_sample], axis=0), ((0, (-rows) % SUBLANE), (0, 0)))
    ada = _ada(c_all, ada_w, ada_b)

    xp = x_prompt
    xs = x_sample.reshape(1, rs, d)
    outs = {name: [] for name in ("kp", "vp", "kip", "ks", "vs", "kis", "s5pr", "s5pi", "s5sr", "s5si",
                                  "scp", "scs", "ssdp", "ssdcp", "ssds", "ssdcs")}
    nq, nkv = N_HEADS * HEAD_DIM, N_KV_HEADS * HEAD_DIM
    w1_all, w2_all = mlp_w1.astype(BF16), mlp_w2.astype(BF16)
    for i in range(depth):
        m, j = i % n_mixers, i // n_mixers
        mp = [ada[i, :bp, k * d:(k + 1) * d].reshape(bp, 1, d) for k in range(6)]
        ms = [jnp.repeat(ada[i, bp:bp + nb, k * d:(k + 1) * d], steps, axis=0).reshape(1, rs, d)
              for k in range(6)]
        g_mix = norm_mix[i]
        mix_p = mix_s = None
        if m == 0:
            w_in = attn_w_in[j]
            w_qkv = w_in[:, :nq + 2 * nkv].astype(BF16)
            n_idx = IDX_HEADS * IDX_DIM + LANE
            w_idx = jnp.pad(w_in[:, nq + 2 * nkv:], ((0, 0), (0, n_idx - (w_in.shape[1] - nq - 2 * nkv))))
            w_out = attn_w_out[j].astype(BF16)
            page = cache_k.shape[2]
            q, k, v, kw, qcat, kcat, kaug, vaug, ki = _attn_proj(xp, mp[0], mp[1], g_mix, w_qkv, w_idx,
                                                                 2 * MAX_DISTANCE, True, page)
            o = _dsa_prompt(q, qcat, kw, kcat, kaug, vaug, rel_bias, MAX_DISTANCE)
            mix_p = ("proj", o, mp[2], w_out)
            to_pages = lambda t: jnp.transpose(t.reshape(bp, seq // page, N_KV_HEADS, HEAD_DIM, page), (0, 1, 4, 2, 3))
            outs["kp"].append(to_pages(k))
            outs["vp"].append(to_pages(v))
            outs["kip"].append(jnp.transpose(ki, (0, 1, 3, 2)))
            q, k, v, kw, qcat, kcat, _, _ = _attn_proj(xs, ms[0], ms[1], g_mix, w_qkv, w_idx, rs, False)
            pps = math.gcd(PAGES_PER_STEP, page_table.shape[1])
            qc_p, kw_p = _pad_q(qcat.astype(F32), nb, steps), _pad_q(kw, nb, steps)
            scores, snew = _dsa_sample_scores(page_table, qc_p, kw_p, _pad_q(kcat.astype(F32), nb, steps),
                                              _pages_t(cache_kidx), j,
                                              math.gcd(2 * PAGES_PER_STEP, page_table.shape[1]))
            o = _dsa_sample(page_table, _pad_q(q, nb, steps), _pad_q(k, nb, steps), _pad_q(v, nb, steps),
                            scores, snew, _pages_t(cache_k), _pages_t(cache_v), rel_bias, j, steps, pps)
            mix_s = ("proj", o[:, :steps].reshape(1, rs, nq), ms[2], w_out)
            outs["ks"].append(k.reshape(nb, steps, N_KV_HEADS, HEAD_DIM))
            outs["vs"].append(v.reshape(nb, steps, N_KV_HEADS, HEAD_DIM))
            outs["kis"].append(kw[..., :IDX_DIM].reshape(nb, steps, IDX_DIM))
        elif m == 1:
            ar, ai, wb, wc = _s5_weights(s5_lam_re[j], s5_lam_im[j], s5_log_dt[j], s5_b_re[j], s5_b_im[j],
                                         s5_c_re[j], s5_c_im[j])
            w_glu = s5_w_glu[j].astype(BF16)
            grp, nst = s5_lam_re.shape[1:]
            z, fr, fi = _s5_prompt(xp, mp[0], mp[1], g_mix, s5_d[j], ar, ai, wb, wc)
            mix_p = ("glu", z, mp[2], w_glu)
            outs["s5pr"].append(fr.reshape(bp, grp, nst))
            outs["s5pi"].append(fi.reshape(bp, grp, nst))
            z, fr, fi = _s5_sample(_to_steps(xs, nb, steps), _to_steps(ms[0], nb, steps),
                                   _to_steps(ms[1], nb, steps), g_mix, s5_d[j], ar, ai, wb, wc,
                                   state_s5_re[j].reshape(nb, grp * nst), state_s5_im[j].reshape(nb, grp * nst),
                                   nb, steps)
            mix_s = ("glu", _to_batch(z, nb, steps), ms[2], w_glu)
            outs["s5sr"].append(fr.reshape(nb, grp, nst))
            outs["s5si"].append(fi.reshape(nb, grp, nst))
        elif m == 2:
            w_in = sc_w_in[j].astype(BF16)
            w_out = sc_w_out[j].astype(BF16)
            width = sc_w_conv.shape[1]
            xp, nbuf = _sconv_prompt(xp, mp[0], mp[1], g_mix, w_in, mp[2], jnp.zeros((bp, width - 1, d), F32),
                                     sc_w_conv[j], w_out, tm)
            outs["scp"].append(nbuf)
            (p,) = _proj(xs, ms[0], ms[1], g_mix, w_in, (3 * d,), rs)
            buf = state_sconv[j].transpose(1, 0, 2).reshape((width - 1) * nb, d)
            y, nbuf = _sconv_sample(_to_steps(p, nb, steps), _to_steps(xs, nb, steps), _to_steps(ms[2], nb, steps),
                                    buf, sc_w_conv[j], w_out, nb, steps)
            xs = _to_batch(y, nb, steps)
            outs["scs"].append(nbuf.reshape(width - 1, nb, d).transpose(1, 0, 2))
        else:
            inner = ssd_norm.shape[1]
            cdim = ssd_conv_w.shape[2]
            nh = ssd_dt_bias.shape[1]
            width = ssd_conv_w.shape[1]
            w_in = jnp.pad(ssd_w_in[j], ((0, 0), (0, LANE - nh))).astype(BF16)
            w_out = ssd_w_out[j].astype(BF16)
            z, xbc, dtr = _proj(xp, mp[0], mp[1], g_mix, w_in, (inner, cdim, LANE), tm)
            y, nbuf, hf = _ssd_prompt(z, xbc, dtr, ssd_conv_w[j], ssd_conv_b[j], ssd_dt_bias[j], ssd_a_log[j],
                                      ssd_d[j], ssd_norm[j])
            mix_p = ("proj", y, mp[2], w_out)
            outs["ssdp"].append(hf)
            outs["ssdcp"].append(nbuf)
            z, xbc, dtr = _proj(xs, ms[0], ms[1], g_mix, w_in, (inner, cdim, LANE), rs)
            buf = state_ssd_conv[j].transpose(1, 0, 2).reshape((width - 1) * nb, cdim)
            xc, dt, nbuf = _ssd_prep_sample(_to_steps(xbc, nb, steps), buf, _to_steps(dtr, nb, steps),
                                            ssd_conv_w[j], ssd_conv_b[j], ssd_dt_bias[j], nb, steps)
            y, hf = _ssd_scan_sample(_pad_q(_to_batch(xc, nb, steps), nb, steps),
                                     _pad_q(_to_batch(dt, nb, steps), nb, steps), _pad_q(z, nb, steps),
                                     state_ssd[j].reshape(nb, nh * SSD_HEADDIM, SSD_STATE),
                                     ssd_a_log[j], ssd_d[j], ssd_norm[j], steps)
            mix_s = ("proj", y[:, :steps].reshape(1, rs, inner), ms[2], w_out)
            outs["ssds"].append(hf.reshape(nb, nh, SSD_HEADDIM, SSD_STATE))
            outs["ssdcs"].append(nbuf.reshape(width - 1, nb, cdim).transpose(1, 0, 2))
        fin = i == depth - 1
        xp = _mlp(xp, mp[3], mp[4], mp[5], norm_mlp[i], w1_all, w2_all, i, norm_final, fin, tm, mix_p)
        xs = _mlp(xs, ms[3], ms[4], ms[5], norm_mlp[i], w1_all, w2_all, i, norm_final, fin, rs, mix_s)
    st = jnp.stack
    return (xp, xs.reshape(nb, steps, d), st(outs["kp"]), st(outs["vp"]), st(outs["kip"]),
            st(outs["ks"]), st(outs["vs"]), st(outs["kis"]), st(outs["s5pr"]), st(outs["s5pi"]),
            st(outs["s5sr"]), st(outs["s5si"]), st(outs["scp"]), st(outs["scs"]),
            st(outs["ssdp"]), st(outs["ssdcp"]), st(outs["ssds"]), st(outs["ssdcs"]))
```

```python
import functools
import math

import jax
import jax.numpy as jnp
import numpy as np
from jax import lax
from jax.experimental import pallas as pl
from jax.experimental.pallas import tpu as pltpu

F32 = jnp.float32
BF16 = jnp.bfloat16
I32 = jnp.int32

EPS = 1e-6
N_HEADS = 16
HEAD_DIM = 64
N_KV_HEADS = 4
IDX_HEADS = 8
IDX_DIM = 64
TOPK_MAX = 256
N_BUCKETS = 32
MAX_DISTANCE = 128
SSD_HEADDIM = 64
SSD_GROUPS = 4
SSD_STATE = 128
SSD_CHUNK = 128

LANE = 128
SUBLANE = 8
VMEM_LIMIT = 56 * 1024 * 1024
NEG_BIG = -1e30
INT_MIN = -2147483648


def _cparams(n_axes):
    return pltpu.CompilerParams(dimension_semantics=("arbitrary",) * n_axes,
                                vmem_limit_bytes=VMEM_LIMIT)


def _dot(a, b):
    return jnp.dot(a, b, preferred_element_type=F32)


def _dot_nt(a, b):
    return lax.dot_general(a, b, (((1,), (1,)), ((), ())), preferred_element_type=F32)


def _split(a):
    hi = a.astype(BF16)
    lo = (a - hi.astype(F32)).astype(BF16)
    return hi, lo


def _dot3(a, b):
    ah, al = _split(a)
    bh, bl = _split(b)
    return _dot(ah, bh) + (_dot(al, bh) + _dot(ah, bl))


def _rms_mod(x, g, shift, scale):
    y = x * lax.rsqrt(jnp.mean(x * x, axis=-1, keepdims=True) + EPS)
    return (y * g) * (1.0 + scale) + shift


def _sigmoid(x):
    return 1.0 / (1.0 + jnp.exp(-x))


def _silu(x):
    return x * _sigmoid(x)


def _row_block(arr, tm):
    n = arr.shape[-1]
    if arr.shape[1] == 1:
        return pl.BlockSpec((1, 1, n), lambda b, i: (b, 0, 0))
    return pl.BlockSpec((1, tm, n), lambda b, i: (b, i, 0))


def _const2(shape):
    return pl.BlockSpec(shape, lambda b, i: (0,) * len(shape))


def _ada_kernel(c_ref, w_ref, b_ref, o_ref):
    o_ref[0] = _dot3(_silu(c_ref[...]), w_ref[0]) + b_ref[0]


def _ada(c_all, ada_w, ada_b):
    depth, d, n = ada_w.shape
    rows = c_all.shape[0]
    tn = 1536
    return pl.pallas_call(
        _ada_kernel,
        grid=(depth, n // tn),
        in_specs=[pl.BlockSpec((rows, d), lambda l, j: (0, 0)),
                  pl.BlockSpec((1, d, tn), lambda l, j: (l, 0, j)),
                  pl.BlockSpec((1, 1, tn), lambda l, j: (l, 0, j))],
        out_specs=pl.BlockSpec((1, rows, tn), lambda l, j: (l, 0, j)),
        out_shape=jax.ShapeDtypeStruct((depth, rows, n), F32),
        compiler_params=_cparams(2),
        name="ada",
    )(c_all, ada_w, ada_b.reshape(depth, 1, n))


def _proj_kernel(x_ref, sh_ref, sc_ref, g_ref, w_ref, *o_refs, splits):
    h = _rms_mod(x_ref[0], g_ref[...], sh_ref[0], sc_ref[0]).astype(BF16)
    off = 0
    for o_ref, n in zip(o_refs, splits):
        o_ref[0] = _dot(h, w_ref[:, off:off + n])
        off += n


def _proj(x, shift, scale, g, w_bf16, splits, tm):
    bsz, seq, d = x.shape
    n = w_bf16.shape[1]
    assert sum(splits) == n and seq % tm == 0
    return pl.pallas_call(
        functools.partial(_proj_kernel, splits=tuple(splits)),
        grid=(bsz, seq // tm),
        in_specs=[pl.BlockSpec((1, tm, d), lambda b, i: (b, i, 0)),
                  _row_block(shift, tm), _row_block(scale, tm),
                  _const2((1, d)), _const2((d, n))],
        out_specs=[pl.BlockSpec((1, tm, s), lambda b, i: (b, i, 0)) for s in splits],
        out_shape=[jax.ShapeDtypeStruct((bsz, seq, s), F32) for s in splits],
        compiler_params=_cparams(2),
        name="proj",
    )(x, shift, scale, g.reshape(1, d), w_bf16)


def _aug_heads(t, col):
    rows = t.shape[0]
    lane = lax.broadcasted_iota(I32, (rows, LANE - HEAD_DIM), 1)
    extra = jnp.where(lane == 0, col, 0.0).astype(F32)
    parts = []
    for g in range(N_KV_HEADS):
        parts += [t[:, g * HEAD_DIM:(g + 1) * HEAD_DIM], extra]
    return jnp.concatenate(parts, axis=1).astype(BF16)


def _attn_proj_kernel(x_ref, sh_ref, sc_ref, g_ref, w_ref, wi_ref,
                      q_ref, k_ref, v_ref, kw_ref, qcat_ref, kcat_ref, kaug_ref, vaug_ref, *ki_refs,
                      nq, nkv, nqi, key_major):
    (ki_ref,) = ki_refs if key_major else (None,)
    h = _rms_mod(x_ref[0], g_ref[...], sh_ref[0], sc_ref[0])
    hb = h.astype(BF16)
    q_ref[0] = (_dot(hb, w_ref[:, 0:nq]) * (HEAD_DIM ** -0.5)).astype(q_ref.dtype)
    k = _dot(hb, w_ref[:, nq:nq + nkv])
    v = _dot(hb, w_ref[:, nq + nkv:nq + 2 * nkv])
    r = _dot3(h, wi_ref[...])
    kw = r[:, nqi:nqi + LANE]
    kw_ref[0] = kw
    if key_major:
        page = k_ref.shape[-1]
        kt, vt, kwt = k.T, v.T, kw.T
        for p in range(k_ref.shape[1]):
            k_ref[0, p] = kt[:, p * page:(p + 1) * page]
            v_ref[0, p] = vt[:, p * page:(p + 1) * page]
            ki_ref[0, p] = kwt[0:IDX_DIM, p * page:(p + 1) * page]
    else:
        k_ref[0] = k
        v_ref[0] = v
    zero = jnp.zeros((h.shape[0], IDX_DIM), BF16)
    parts = []
    for hh in range(IDX_HEADS):
        ah, al = _split(r[:, hh * IDX_DIM:(hh + 1) * IDX_DIM])
        parts += [ah, al, ah, zero]
    qcat_ref[0] = jnp.concatenate(parts, axis=1)
    kcat = _idx_rhs(kw[:, 0:IDX_DIM])
    kaug = _aug_heads(k, 0.0)
    vaug_ref[0] = _aug_heads(v, 1.0)
    if key_major:
        pair = kcat_ref.shape[-1]
        kcat32, kaug32 = kcat.astype(F32), kaug.astype(F32)
        for s in range(kcat_ref.shape[1]):
            kcat_ref[0, s] = kcat32[s * pair:(s + 1) * pair].T.astype(BF16)
            kaug_ref[0, s] = kaug32[s * pair:(s + 1) * pair].T.astype(BF16)
    else:
        kcat_ref[0] = kcat
        kaug_ref[0] = kaug


def _attn_proj(x, shift, scale, g, w_qkv_bf16, w_idx, tm, key_major, page=None, pair=None):
    bsz, seq, d = x.shape
    nq = N_HEADS * HEAD_DIM
    nkv = N_KV_HEADS * HEAD_DIM
    nqi = IDX_HEADS * IDX_DIM
    rows = lambda n, dt: (pl.BlockSpec((1, tm, n), lambda b, i: (b, i, 0)), jax.ShapeDtypeStruct((bsz, seq, n), dt))
    cols = lambda n, dt: (pl.BlockSpec((1, tm // pair, n, pair), lambda b, i: (b, i, 0, 0)),
                          jax.ShapeDtypeStruct((bsz, seq // pair, n, pair), dt))
    keys = cols if key_major else rows
    if key_major:
        ppt = tm // page
        paged = lambda n: (pl.BlockSpec((1, ppt, n, page), lambda b, i: (b, i, 0, 0)),
                           jax.ShapeDtypeStruct((bsz, seq // page, n, page), F32))
        kv = [paged(nkv), paged(nkv)]
    else:
        kv = [rows(nkv, F32), rows(nkv, F32)]
    outs = [rows(nq, BF16 if key_major else F32)] + kv + [rows(LANE, F32), rows(4 * nqi, BF16),
                                   keys(4 * IDX_DIM, BF16), keys(N_KV_HEADS * LANE, BF16),
                                   rows(N_KV_HEADS * LANE, BF16)]
    if key_major:
        outs.append(paged(IDX_DIM))
    return pl.pallas_call(
        functools.partial(_attn_proj_kernel, nq=nq, nkv=nkv, nqi=nqi, key_major=key_major),
        grid=(bsz, seq // tm),
        in_specs=[pl.BlockSpec((1, tm, d), lambda b, i: (b, i, 0)),
                  _row_block(shift, tm), _row_block(scale, tm),
                  _const2((1, d)), _const2(w_qkv_bf16.shape), _const2(w_idx.shape)],
        out_specs=[o[0] for o in outs],
        out_shape=[o[1] for o in outs],
        compiler_params=_cparams(2),
        name="attn_proj",
    )(x, shift, scale, g.reshape(1, d), w_qkv_bf16, w_idx)


def _mlp_kernel(*refs, ff_chunk, final_norm, mixer_out):
    if mixer_out is None:
        x_ref, sh_ref, sc_ref, gate_ref, g_ref, w1_ref, w2_ref, gf_ref, y_ref = refs
        x = x_ref[0]
    else:
        o_ref, gmix_ref, wo_ref, x_ref, sh_ref, sc_ref, gate_ref, g_ref, w1_ref, w2_ref, gf_ref, y_ref = refs
        d = x_ref.shape[-1]
        ob = o_ref[0].astype(BF16)
        if mixer_out == "glu":
            t = _dot(ob, wo_ref[:, :d]) * _sigmoid(_dot(ob, wo_ref[:, d:]))
        else:
            t = _dot(ob, wo_ref[...])
        x = x_ref[0] + gmix_ref[0] * t
    h = _rms_mod(x, g_ref[...], sh_ref[0], sc_ref[0]).astype(BF16)
    dff = w1_ref.shape[-1]
    acc = jnp.zeros(x.shape, F32)
    for c in range(dff // ff_chunk):
        a = jnp.maximum(_dot(h, w1_ref[0, :, c * ff_chunk:(c + 1) * ff_chunk]), 0.0)
        acc = acc + _dot((a * a).astype(BF16), w2_ref[0, c * ff_chunk:(c + 1) * ff_chunk, :])
    y = x + gate_ref[0] * acc
    if final_norm:
        y = (y * lax.rsqrt(jnp.mean(y * y, axis=-1, keepdims=True) + EPS)) * gf_ref[...]
    y_ref[0] = y


def _resident(shape):
    return pl.BlockSpec(shape, lambda b, i: (0,) * len(shape), pipeline_mode=pl.Buffered(1))


def _mlp(x, shift, scale, gate, g, w1_bf16, w2_bf16, layer, g_final, final_norm, tm, mixer=None):
    bsz, seq, d = x.shape
    dff = w1_bf16.shape[-1]
    layer_w = lambda r, c: pl.BlockSpec((1, r, c), lambda b, i: (layer, 0, 0), pipeline_mode=pl.Buffered(1))
    rows = lambda n: pl.BlockSpec((1, tm, n), lambda b, i: (b, i, 0))
    args, specs, kind = [], [], None
    if mixer is not None:
        kind, o, gmix, wo = mixer
        args += [o, gmix, wo]
        specs += [rows(o.shape[-1]), _row_block(gmix, tm), _resident(wo.shape)]
    args += [x, shift, scale, gate, g.reshape(1, d), w1_bf16, w2_bf16, g_final.reshape(1, d)]
    specs += [rows(d), _row_block(shift, tm), _row_block(scale, tm), _row_block(gate, tm),
              _const2((1, d)), layer_w(d, dff), layer_w(dff, d), _const2((1, d))]
    return pl.pallas_call(
        functools.partial(_mlp_kernel, ff_chunk=1024, final_norm=final_norm, mixer_out=kind),
        grid=(bsz, seq // tm),
        in_specs=specs,
        out_specs=rows(d),
        out_shape=jax.ShapeDtypeStruct((bsz, seq, d), F32),
        compiler_params=_cparams(2),
        name="mlp",
    )(*args)


S5_CH = 128


def _s5_disc_kernel(lr_ref, li_ref, ldt_ref, br_ref, bi_ref, ar_ref, ai_ref, bbr_ref, bbi_ref):
    lr, li = lr_ref[...], li_ref[...]
    dt = jnp.exp(ldt_ref[...])
    mag = jnp.exp(lr * dt)
    ab_re, ab_im = mag * jnp.cos(li * dt), mag * jnp.sin(li * dt)
    den = lr * lr + li * li
    nr = ab_re - 1.0
    f_re = (nr * lr + ab_im * li) / den
    f_im = (ab_im * lr - nr * li) / den
    ar_ref[...] = ab_re
    ai_ref[...] = ab_im
    for c in range(br_ref.shape[0]):
        br, bi = br_ref[c], bi_ref[c]
        bbr_ref[c] = f_re * br - f_im * bi
        bbi_ref[c] = f_re * bi + f_im * br


def _s5_weights(lam_re, lam_im, log_dt, b_re, b_im, c_re, c_im):
    g, p = lam_re.shape
    gc = b_re.shape[-1]
    brt = jnp.moveaxis(b_re, 2, 0)
    bit = jnp.moveaxis(b_im, 2, 0)
    ar, ai, bbr, bbi = pl.pallas_call(
        _s5_disc_kernel,
        out_shape=[jax.ShapeDtypeStruct((g, p), F32), jax.ShapeDtypeStruct((g, p), F32),
                   jax.ShapeDtypeStruct((gc, g, p), F32), jax.ShapeDtypeStruct((gc, g, p), F32)],
        name="s5_disc",
    )(lam_re, lam_im, log_dt.reshape(g, 1), brt, bit)
    nblk = (g * gc) // S5_CH
    gpb = g // nblk
    eye = jnp.eye(gpb, dtype=F32)

    def bd_in(bb):
        t = jnp.moveaxis(bb, 0, 1).reshape(nblk, gpb, gc, p)
        return jnp.einsum('ngcp,gh->ngchp', t, eye).reshape(nblk, gpb * gc, gpb * p)

    def bd_out(cc):
        t = cc.reshape(nblk, gpb, gc, p)
        return jnp.einsum('ngcp,gh->ngphc', t, eye).reshape(nblk, gpb * p, gpb * gc)

    wb = jnp.concatenate([bd_in(bbr), bd_in(bbi)], axis=-1).astype(BF16)
    wc = jnp.concatenate([bd_out(c_re), bd_out(c_im)], axis=1).astype(BF16)
    return ar.reshape(nblk, 1, gpb * p), ai.reshape(nblk, 1, gpb * p), wb, wc


def _gelu_tanh(y):
    return 0.5 * y * (1.0 + jnp.tanh(math.sqrt(2.0 / math.pi) * (y + 0.044715 * (y * y * y))))


def _cmul(ar, ai, xr, xi):
    return ar * xr - ai * xi, ar * xi + ai * xr


def _s5_prompt_kernel(x_ref, sh_ref, sc_ref, g_ref, d_ref, ar_ref, ai_ref, wb_ref, wc_ref,
                      perm_ref, unperm_ref, z_ref, fr_ref, fi_ref, u_s, br_s, bi_s, st_r, st_i,
                      *, nseg, seg):
    i = pl.program_id(1)
    nblk = wb_ref.shape[0]
    blk = ar_ref.shape[-1]

    @pl.when(i == 0)
    def _():
        st_r[...] = jnp.zeros(st_r.shape, F32)
        st_i[...] = jnp.zeros(st_i.shape, F32)

    u_s[...] = _dot_sel_lhs(perm_ref[...], _rms_mod(x_ref[0], g_ref[...], sh_ref[0], sc_ref[0]))

    for c in range(nblk):
        lo, hi = c * S5_CH, (c + 1) * S5_CH
        uc = u_s[:, lo:hi]
        bu = _dot(uc.astype(BF16), wb_ref[c])
        br_s[...] = bu[:, :blk]
        bi_s[...] = bu[:, blk:]
        ar = jnp.broadcast_to(ar_ref[c], (nseg, blk))
        ai = jnp.broadcast_to(ai_ref[c], (nseg, blk))

        def local(j, carry):
            xr, xi = carry
            r0 = pl.multiple_of(j * nseg, nseg)
            pr, pi = _cmul(ar, ai, xr, xi)
            nr = pr + br_s[pl.ds(r0, nseg), :]
            ni = pi + bi_s[pl.ds(r0, nseg), :]
            br_s[pl.ds(r0, nseg), :] = nr
            bi_s[pl.ds(r0, nseg), :] = ni
            return nr, ni

        zero = jnp.zeros((nseg, blk), F32)
        fr, fi = lax.fori_loop(0, seg, local, (zero, zero), unroll=True)

        pr, pi = ar_ref[c], ai_ref[c]
        for _ in range(int(math.log2(seg))):
            pr, pi = _cmul(pr, pi, pr, pi)
        cr, ci = st_r[c], st_i[c]
        rows_r, rows_i = [], []
        for s in range(nseg):
            rows_r.append(cr)
            rows_i.append(ci)
            tr, ti = _cmul(pr, pi, cr, ci)
            cr, ci = tr + fr[s:s + 1], ti + fi[s:s + 1]
        st_r[c] = cr
        st_i[c] = ci
        fr_ref[0, c] = cr
        fi_ref[0, c] = ci
        dr, di = _cmul(ar, ai, jnp.concatenate(rows_r, axis=0), jnp.concatenate(rows_i, axis=0))

        def fix(j, carry):
            dr, di = carry
            r0 = pl.multiple_of(j * nseg, nseg)
            br_s[pl.ds(r0, nseg), :] = br_s[pl.ds(r0, nseg), :] + dr
            bi_s[pl.ds(r0, nseg), :] = bi_s[pl.ds(r0, nseg), :] + di
            return _cmul(ar, ai, dr, di)

        lax.fori_loop(0, seg, fix, (dr, di), unroll=True)

        y = (_dot(br_s[...].astype(BF16), wc_ref[c, :blk, :])
             - _dot(bi_s[...].astype(BF16), wc_ref[c, blk:, :]))
        y = y + d_ref[:, lo:hi] * uc
        u_s[:, lo:hi] = _gelu_tanh(y)

    z_ref[0] = _dot_sel_lhs(unperm_ref[...], u_s[...]).astype(z_ref.dtype)


def _s5_prompt(x, shift, scale, g, d_skip, ar, ai, wb, wc):
    bsz, seq, d = x.shape
    nblk, _, blk = ar.shape
    nseg, seg = SUBLANE, 32
    tm = nseg * seg
    assert seq % tm == 0
    perm = np.zeros((tm, tm), np.float32)
    for s in range(nseg):
        for j in range(seg):
            perm[j * nseg + s, s * seg + j] = 1.0
    unperm = jnp.asarray(perm.T, BF16)
    perm = jnp.asarray(perm, BF16)
    z, fr, fi = pl.pallas_call(
        functools.partial(_s5_prompt_kernel, nseg=nseg, seg=seg),
        grid=(bsz, seq // tm),
        in_specs=[pl.BlockSpec((1, tm, d), lambda b, i: (b, i, 0)),
                  _row_block(shift, tm), _row_block(scale, tm),
                  _const2((1, d)), _const2((1, d)),
                  _const2(ar.shape), _const2(ai.shape), _const2(wb.shape), _const2(wc.shape),
                  _const2((tm, tm)), _const2((tm, tm))],
        out_specs=[pl.BlockSpec((1, tm, d), lambda b, i: (b, i, 0)),
                   pl.BlockSpec((1, nblk, 1, blk), lambda b, i: (b, 0, 0, 0)),
                   pl.BlockSpec((1, nblk, 1, blk), lambda b, i: (b, 0, 0, 0))],
        out_shape=[jax.ShapeDtypeStruct((bsz, seq, d), BF16),
                   jax.ShapeDtypeStruct((bsz, nblk, 1, blk), F32),
                   jax.ShapeDtypeStruct((bsz, nblk, 1, blk), F32)],
        scratch_shapes=[pltpu.VMEM((tm, d), F32), pltpu.VMEM((tm, blk), F32), pltpu.VMEM((tm, blk), F32),
                        pltpu.VMEM((nblk, 1, blk), F32), pltpu.VMEM((nblk, 1, blk), F32)],
        compiler_params=_cparams(2),
        name="s5_prompt",
    )(x, shift, scale, g.reshape(1, d), d_skip.reshape(1, d), ar, ai, wb, wc, perm, unperm)
    return z, fr, fi


def _s5_sample_kernel(x_ref, sh_ref, sc_ref, g_ref, d_ref, ar_ref, ai_ref, wb_ref, wc_ref,
                      s0r_ref, s0i_ref, z_ref, fr_ref, fi_ref, *, nb, steps):
    nblk = wb_ref.shape[0]
    blk = ar_ref.shape[-1]
    u = _rms_mod(x_ref[...], g_ref[...], sh_ref[...], sc_ref[...])
    for c in range(nblk):
        lo, hi = c * S5_CH, (c + 1) * S5_CH
        uc = u[:, lo:hi]
        bu = _dot(uc.astype(BF16), wb_ref[c])
        ar = jnp.broadcast_to(ar_ref[c], (nb, blk))
        ai = jnp.broadcast_to(ai_ref[c], (nb, blk))
        xr, xi = s0r_ref[:, c * blk:(c + 1) * blk], s0i_ref[:, c * blk:(c + 1) * blk]
        xrs, xis = [], []
        for t in range(steps):
            pr, pi = _cmul(ar, ai, xr, xi)
            xr = pr + bu[t * nb:(t + 1) * nb, :blk]
            xi = pi + bu[t * nb:(t + 1) * nb, blk:]
            xrs.append(xr)
            xis.append(xi)
        fr_ref[:, c * blk:(c + 1) * blk] = xr
        fi_ref[:, c * blk:(c + 1) * blk] = xi
        y = (_dot(jnp.concatenate(xrs, axis=0).astype(BF16), wc_ref[c, :blk, :])
             - _dot(jnp.concatenate(xis, axis=0).astype(BF16), wc_ref[c, blk:, :]))
        y = y + d_ref[:, lo:hi] * uc
        z_ref[:, lo:hi] = _gelu_tanh(y)


def _s5_sample(x_tb, shift_tb, scale_tb, g, d_skip, ar, ai, wb, wc, s0r, s0i, nb, steps):
    rows, d = x_tb.shape
    nblk, _, blk = ar.shape
    return pl.pallas_call(
        functools.partial(_s5_sample_kernel, nb=nb, steps=steps),
        out_shape=[jax.ShapeDtypeStruct((rows, d), F32),
                   jax.ShapeDtypeStruct((nb, nblk * blk), F32),
                   jax.ShapeDtypeStruct((nb, nblk * blk), F32)],
        compiler_params=pltpu.CompilerParams(vmem_limit_bytes=VMEM_LIMIT),
        name="s5_sample",
    )(x_tb, shift_tb, scale_tb, g.reshape(1, d), d_skip.reshape(1, d), ar, ai, wb, wc, s0r, s0i)


def _shift_rows(u, prev, k):
    if k == 0:
        return u
    rolled = pltpu.roll(u, k, 0)
    head = pltpu.roll(prev, k, 0)
    row = lax.broadcasted_iota(I32, (SUBLANE, u.shape[1]), 0)
    first = jnp.where(row < k, head, rolled[:SUBLANE])
    return jnp.concatenate([first, rolled[SUBLANE:]], axis=0)


def _sconv_prompt_kernel(x_ref, sh_ref, sc_ref, g_ref, wi_ref, gate_ref, buf_ref, wc_ref, wo_ref,
                         y_ref, nb_ref, prev_s, *, width):
    i = pl.program_id(1)
    d = x_ref.shape[-1]

    @pl.when(i == 0)
    def _():
        prev_s[...] = jnp.zeros(prev_s.shape, F32)
        prev_s[SUBLANE - (width - 1):, :] = buf_ref[0]

    h = _rms_mod(x_ref[0], g_ref[...], sh_ref[0], sc_ref[0]).astype(BF16)
    gb = _dot(h, wi_ref[:, :d])
    u = _dot(h, wi_ref[:, d:2 * d]) * _dot(h, wi_ref[:, 2 * d:])
    prev = prev_s[...]
    conv = u * wc_ref[width - 1:width, :]
    for k in range(1, width):
        conv = conv + _shift_rows(u, prev, k) * wc_ref[width - 1 - k:width - k, :]
    prev_s[...] = u[u.shape[0] - SUBLANE:, :]
    nb_ref[0] = u[u.shape[0] - (width - 1):, :]
    y_ref[0] = x_ref[0] + gate_ref[0] * _dot((gb * conv).astype(BF16), wo_ref[...])


def _sconv_prompt(x, shift, scale, g, w_in_bf16, gate, buf, w_conv, w_out_bf16, tm):
    bsz, seq, d = x.shape
    width = w_conv.shape[0]
    return pl.pallas_call(
        functools.partial(_sconv_prompt_kernel, width=width),
        grid=(bsz, seq // tm),
        in_specs=[pl.BlockSpec((1, tm, d), lambda b, i: (b, i, 0)),
                  _row_block(shift, tm), _row_block(scale, tm), _const2((1, d)),
                  _resident(w_in_bf16.shape), _row_block(gate, tm),
                  pl.BlockSpec((1, width - 1, d), lambda b, i: (b, 0, 0)),
                  _const2((width, d)), _resident((d, d))],
        out_specs=[pl.BlockSpec((1, tm, d), lambda b, i: (b, i, 0)),
                   pl.BlockSpec((1, width - 1, d), lambda b, i: (b, 0, 0))],
        out_shape=[jax.ShapeDtypeStruct((bsz, seq, d), F32),
                   jax.ShapeDtypeStruct((bsz, width - 1, d), F32)],
        scratch_shapes=[pltpu.VMEM((SUBLANE, d), F32)],
        compiler_params=_cparams(2),
        name="sconv_prompt",
    )(x, shift, scale, g.reshape(1, d), w_in_bf16, gate, buf, w_conv, w_out_bf16)


def _sconv_sample_kernel(p_ref, x_ref, gate_ref, buf_ref, wc_ref, wo_ref, y_ref, nb_ref,
                         *, width, nb, steps):
    d = x_ref.shape[-1]
    p = p_ref[...]
    gb, gc, xh = p[:, :d], p[:, d:2 * d], p[:, 2 * d:]
    u = gc * xh
    ext = jnp.concatenate([buf_ref[...], u], axis=0)
    conv = ext[0:steps * nb] * wc_ref[0:1, :]
    for k in range(1, width):
        conv = conv + ext[k * nb:(k + steps) * nb] * wc_ref[k:k + 1, :]
    nb_ref[...] = ext[steps * nb:]
    y_ref[...] = x_ref[...] + gate_ref[...] * _dot((gb * conv).astype(BF16), wo_ref[...])


def _sconv_sample(p_tb, x_tb, gate_tb, buf_kb, w_conv, w_out_bf16, nb, steps):
    rows, d = x_tb.shape
    width = w_conv.shape[0]
    return pl.pallas_call(
        functools.partial(_sconv_sample_kernel, width=width, nb=nb, steps=steps),
        out_shape=[jax.ShapeDtypeStruct((rows, d), F32),
                   jax.ShapeDtypeStruct(((width - 1) * nb, d), F32)],
        compiler_params=pltpu.CompilerParams(vmem_limit_bytes=VMEM_LIMIT),
        name="sconv_sample",
    )(p_tb, x_tb, gate_tb, buf_kb, w_conv, w_out_bf16)


KEY_NEG_INF = -2139095041
KEY_POS_INF = 2139095040


def _code_to_f32(key):
    key = jnp.clip(key, KEY_NEG_INF, KEY_POS_INF)
    return lax.bitcast_convert_type(key ^ ((key >> 31) & 0x7FFFFFFF), F32)


def _stack_heads(qcat):
    w = 4 * IDX_DIM
    return jnp.concatenate([qcat[:, h * w:(h + 1) * w] for h in range(IDX_HEADS)], axis=0)


def _idx_rhs(ki):
    kh, kl = _split(ki)
    return jnp.concatenate([kh, kh, kl, jnp.zeros(kh.shape, BF16)], axis=1)


def _idx_score(dts, wcols, nrows):
    s = wcols[0] * jnp.maximum(dts[0:nrows], 0.0)
    for h in range(1, IDX_HEADS):
        s = s + wcols[h] * jnp.maximum(dts[h * nrows:(h + 1) * nrows], 0.0)
    return s


def _idx_wcols(kw):
    sc = (IDX_HEADS ** -0.5) * (IDX_DIM ** -0.5)
    return [kw[:, IDX_DIM + h:IDX_DIM + h + 1] * sc for h in range(IDX_HEADS)]


def _topk_select(count, rows, lanes, topk, idx_bits, stash=lambda v: (lambda: v)):
    wide = lambda v: jnp.broadcast_to(v, (rows, lanes))

    def bit_step(it, tu):
        mask = jnp.left_shift(jnp.int32(1), 31 - it)
        cand_u = tu | mask
        cand = stash(wide(_code_to_f32(cand_u ^ INT_MIN)))
        cnt = count(lambda c, sc, idx: jnp.where(sc >= cand(), 1.0, 0.0))
        return jnp.where(cnt >= topk, cand_u, tu)

    t = _code_to_f32(lax.fori_loop(0, 32, bit_step, jnp.zeros((rows, 1), I32)) ^ INT_MIN)
    tw = wide(t)
    n_gt = count(lambda c, sc, idx: jnp.where(sc > tw, 1.0, 0.0))
    n_ge = count(lambda c, sc, idx: jnp.where(sc >= tw, 1.0, 0.0))
    need = topk - n_gt
    tied = jnp.max(jnp.where(t > -jnp.inf, n_ge - topk, 0.0)) > 0.0

    def tie_search(_):
        def idx_step(it, j):
            cand = j | jnp.left_shift(jnp.int32(1), idx_bits - 1 - it)
            cw = wide(cand)
            cnt = count(lambda c, sc, idx: jnp.where((sc == tw) & (idx < cw), 1.0, 0.0))
            return jnp.where(cnt < need, cand, j)
        return lax.fori_loop(0, idx_bits, idx_step, jnp.zeros((rows, 1), I32))

    j = lax.cond(tied, tie_search, lambda _: jnp.full((rows, 1), 2 ** idx_bits, I32), 0)
    return t, j


def _dsa_prompt_kernel(q_ref, qcat_ref, kwq_ref, kcat_ref, kaug_ref, vaug_ref, bias_ref, o_ref,
                       sc_s, madd_s, wbc_s, cand_s, qa_s, lg_s, mx_s, sh_s, acc_s, *, tq, topk, idx_bits):
    i = pl.program_id(1)
    nk = i + 1
    nk2 = (nk + 1) // 2
    r = N_HEADS // N_KV_HEADS

    for h, w in enumerate(_idx_wcols(kwq_ref[0])):
        wbc_s[h] = jnp.broadcast_to(w, (tq, tq))
    qrow = lax.broadcasted_iota(I32, (tq, tq), 0)
    kcol = lax.broadcasted_iota(I32, (tq, tq), 1)
    qpos = i * tq + qrow
    wq = 4 * IDX_DIM

    def score_pair(c2, _):
        kc = kcat_ref[0, c2]
        s = None
        hpd = 4
        for hp in range(IDX_HEADS // hpd):
            lhs = jnp.concatenate([qcat_ref[0, :, (hpd * hp + v) * wq:(hpd * hp + v + 1) * wq] for v in range(hpd)],
                                  axis=0)
            dd = jnp.maximum(_dot(lhs, kc), 0.0)
            for v in range(hpd):
                w = wbc_s[hpd * hp + v]
                d = jnp.concatenate([w, w], axis=1) * dd[v * tq:(v + 1) * tq]
                s = d if s is None else s + d
        for u in range(2):
            c = 2 * c2 + u
            sc_s[c] = jnp.where(c * tq + kcol <= qpos, s[:, u * tq:(u + 1) * tq], -jnp.inf)
        return 0

    lax.fori_loop(0, nk2, score_pair, 0)

    def count(fn):
        def body(c2, acc):
            c = 2 * c2
            acc = acc + fn(c, sc_s[c], c * tq + kcol)
            return acc + fn(c + 1, sc_s[c + 1], (c + 1) * tq + kcol)
        acc = lax.fori_loop(0, nk2, body, jnp.zeros((tq, tq), F32))
        return jnp.sum(acc, axis=1, keepdims=True)

    def stash(v):
        cand_s[...] = v
        return lambda: cand_s[...]

    t, j = _topk_select(count, tq, tq, float(topk), idx_bits, stash)
    tw = jnp.broadcast_to(t, (tq, tq))
    jw = jnp.broadcast_to(j, (tq, tq))

    def mask_pair(c2, _):
        for u in range(2):
            c = 2 * c2 + u
            sc = sc_s[c]
            sel = (sc > tw) | ((sc == tw) & (c * tq + kcol <= jw))
            madd_s[c] = jnp.where(sel & (sc > -jnp.inf), 0.0, NEG_BIG)
        return 0

    lax.fori_loop(0, nk2, mask_pair, 0)

    q = q_ref[0].astype(F32)
    zpad = jnp.zeros((tq, LANE - HEAD_DIM), F32)
    for g in range(N_KV_HEADS):
        qa_s[g] = jnp.concatenate(
            [jnp.concatenate([q[:, (g * r + u) * HEAD_DIM:(g * r + u + 1) * HEAD_DIM], zpad], axis=1)
             for u in range(r)], axis=0).astype(BF16)

    def pair_mask(c2, near, g):
        halves = []
        for u in range(2):
            c = 2 * c2 + u
            m = madd_s[c]
            if near:
                back = nk - 1 - c
                carries = (back == 0) | ((back == 1) & (kcol > qrow))
                halves.append([m + jnp.where(carries, bias_ref[g * r + v], 0.0) for v in range(r)])
            else:
                halves.append([m] * r)
        return jnp.concatenate([jnp.concatenate([halves[0][v], halves[1][v]], axis=1) for v in range(r)], axis=0)

    ngh = lg_s.shape[0]
    for half in range(N_KV_HEADS // ngh):
        groups = [half * ngh + k for k in range(ngh)]
        mx_s[...] = jnp.full(mx_s.shape, NEG_BIG, F32)

        def logits_pair(c2, near):
            for k, g in enumerate(groups):
                lg = _dot(qa_s[g], kaug_ref[0, c2, g * LANE:(g + 1) * LANE, :]) + pair_mask(c2, near, g)
                lg_s[k, c2] = lg
                mx_s[k] = jnp.maximum(mx_s[k], jnp.maximum(lg[:, :tq], lg[:, tq:]))

        def far(c4, _):
            logits_pair(2 * c4, False)
            logits_pair(2 * c4 + 1, False)
            return 0

        nfar = jnp.maximum(nk2 - 2, 0)
        lax.fori_loop(0, nfar // 2, far, 0)

        @pl.when(nfar % 2 == 1)
        def _():
            logits_pair(nfar - 1, False)

        @pl.when(nk2 >= 2)
        def _():
            logits_pair(nk2 - 2, True)

        logits_pair(nk2 - 1, True)

        for k in range(ngh):
            sh_s[k] = jnp.broadcast_to(jnp.max(mx_s[k], axis=1, keepdims=True), (r * tq, LANE))
        acc_s[...] = jnp.zeros(acc_s.shape, F32)

        def pv_pair(c2):
            r0 = pl.multiple_of(c2 * 2 * tq, 2 * tq)
            va = vaug_ref[0, pl.ds(r0, 2 * tq), :]
            for k, g in enumerate(groups):
                s = sh_s[k]
                p = jnp.exp(lg_s[k, c2] - jnp.concatenate([s, s], axis=1))
                acc_s[k] = acc_s[k] + _dot(p.astype(BF16), va[:, g * LANE:(g + 1) * LANE])

        def pv_two(c4, _):
            pv_pair(2 * c4)
            pv_pair(2 * c4 + 1)
            return 0

        lax.fori_loop(0, nk2 // 2, pv_two, 0)

        @pl.when(nk2 % 2 == 1)
        def _():
            pv_pair(nk2 - 1)
        for k, g in enumerate(groups):
            acc = acc_s[k]
            o = acc[:, 0:HEAD_DIM] / acc[:, HEAD_DIM:HEAD_DIM + 1]
            heads = jnp.concatenate([o[u * tq:(u + 1) * tq, :] for u in range(r)], axis=1)
            o_ref[0, :, g * r * HEAD_DIM:(g + 1) * r * HEAD_DIM] = heads.astype(o_ref.dtype)


def _t5_bucket(dist):
    n = jnp.maximum(dist, 0)
    max_exact = N_BUCKETS // 2
    nf = jnp.maximum(n, max_exact).astype(F32)
    large = max_exact + (jnp.log(nf / max_exact) / math.log(MAX_DISTANCE / max_exact)
                         * (N_BUCKETS - max_exact)).astype(I32)
    large = jnp.minimum(large, N_BUCKETS - 1)
    return jnp.where(n < max_exact, n, large)


def _bias_table(rel_bias, n):
    tab = rel_bias[_t5_bucket(jnp.arange(n, dtype=I32))]
    return (tab - rel_bias[N_BUCKETS - 1][None, :]).T


DSA_GROUPS_PER_PASS = 2


def _dsa_prompt(q, qcat, kw, kcat, kaug, vaug, rel_bias, tq):
    bsz, seq, nq = q.shape
    topk = min(TOPK_MAX, seq // 4)
    nkc = seq // tq
    r = N_HEADS // N_KV_HEADS
    assert tq == MAX_DISTANCE and seq % (2 * tq) == 0
    tab = _bias_table(rel_bias, tq)
    dmod = (np.arange(tq)[:, None] - np.arange(tq)[None, :]) % tq
    onehot = (jnp.arange(tq, dtype=I32)[:, None, None] == jnp.asarray(dmod, I32)[None]).astype(F32)
    bias = jnp.einsum('hd,dqk->hqk', tab, onehot, precision=lax.Precision.HIGHEST)
    assert kcat.shape[1:] == (seq // (2 * tq), 4 * IDX_DIM, 2 * tq)
    row = lambda n: pl.BlockSpec((1, tq, n), lambda b, i: (b, i, 0))
    tiles = lambda a: pl.BlockSpec((1,) + a.shape[1:], lambda b, i: (b, 0, 0, 0), pipeline_mode=pl.Buffered(1))
    return pl.pallas_call(
        functools.partial(_dsa_prompt_kernel, tq=tq, topk=topk, idx_bits=int(math.log2(seq))),
        grid=(bsz, nkc),
        in_specs=[row(nq), row(qcat.shape[-1]), row(LANE), tiles(kcat), tiles(kaug),
                  pl.BlockSpec((1, seq, vaug.shape[-1]), lambda b, i: (b, 0, 0), pipeline_mode=pl.Buffered(1)),
                  _resident((N_HEADS, tq, tq))],
        out_specs=row(nq),
        out_shape=jax.ShapeDtypeStruct((bsz, seq, nq), BF16),
        scratch_shapes=[pltpu.VMEM((nkc, tq, tq), F32), pltpu.VMEM((nkc, tq, tq), F32),
                        pltpu.VMEM((IDX_HEADS, tq, tq), F32), pltpu.VMEM((tq, tq), F32),
                        pltpu.VMEM((N_KV_HEADS, r * tq, LANE), BF16),
                        pltpu.VMEM((DSA_GROUPS_PER_PASS, nkc // 2, r * tq, 2 * tq), F32),
                        pltpu.VMEM((DSA_GROUPS_PER_PASS, r * tq, tq), F32),
                        pltpu.VMEM((DSA_GROUPS_PER_PASS, r * tq, LANE), F32),
                        pltpu.VMEM((DSA_GROUPS_PER_PASS, r * tq, LANE), F32)],
        compiler_params=_cparams(2),
        name="dsa_prompt",
    )(q, qcat, kw, kcat, kaug, vaug, bias)


QP = SUBLANE
PAGES_PER_STEP = 64


def _page_specs(shape4, layer, pps):
    def spec(u):
        return pl.BlockSpec((1, 1) + tuple(shape4[2:]),
                            lambda b, g, pt: (layer, pt[b, g * pps + u], 0, 0))
    return [spec(u) for u in range(pps)]


def _pages_t(cache):
    nd = cache.ndim
    t = jnp.transpose(cache, (0, 1) + tuple(range(3, nd)) + (2,))
    return t.reshape(t.shape[0], t.shape[1], -1, t.shape[-1])


def _pad_t(a, page):
    return jnp.concatenate([a, jnp.zeros((page - QP, a.shape[1]), F32)], axis=0).T


def _dsa_sample_score_kernel(pt_ref, qcat_ref, kw_ref, kcn_ref, *rest, pps):
    page_refs, o_ref, on_ref = rest[:pps], rest[pps], rest[pps + 1]
    lhs = _stack_heads(qcat_ref[0]).astype(BF16)
    wcols = _idx_wcols(kw_ref[0])
    kt = jnp.concatenate([page_refs[u][0, 0] for u in range(pps)], axis=1)
    kh, kl = _split(kt)
    rhs = jnp.concatenate([kh, kh, kl, jnp.zeros(kh.shape, BF16)], axis=0)
    o_ref[0] = _idx_score(_dot(lhs, rhs), wcols, QP)

    @pl.when(pl.program_id(1) == pl.num_programs(1) - 1)
    def _():
        on_ref[0] = _idx_score(_dot(lhs, _pad_t(kcn_ref[0], on_ref.shape[-1]).astype(BF16)), wcols, QP)


def _dsa_sample_scores(page_table, qcat, kw, kcn, kidx_t, layer, pps):
    nb, n_pages = page_table.shape
    page = kidx_t.shape[-1]
    bmap = lambda b, g, pt: (b, 0, 0)
    grid_spec = pltpu.PrefetchScalarGridSpec(
        num_scalar_prefetch=1,
        grid=(nb, n_pages // pps),
        in_specs=[pl.BlockSpec((1, QP, qcat.shape[-1]), bmap), pl.BlockSpec((1, QP, LANE), bmap),
                  pl.BlockSpec((1, QP, kcn.shape[-1]), bmap)] + _page_specs(kidx_t.shape, layer, pps),
        out_specs=[pl.BlockSpec((1, QP, pps * page), lambda b, g, pt: (b, 0, g)),
                   pl.BlockSpec((1, QP, page), bmap)])
    return pl.pallas_call(
        functools.partial(_dsa_sample_score_kernel, pps=pps),
        grid_spec=grid_spec,
        out_shape=[jax.ShapeDtypeStruct((nb, QP, n_pages * page), F32),
                   jax.ShapeDtypeStruct((nb, QP, page), F32)],
        compiler_params=_cparams(2),
        name="dsa_sample_scores",
    )(page_table, qcat, kw, kcn, *([kidx_t] * pps))


def _new_token_scores(s_new, steps):
    lane = lax.broadcasted_iota(I32, s_new.shape, 1)
    q = lax.broadcasted_iota(I32, s_new.shape, 0) % QP
    return jnp.where((lane <= q) & (lane < steps), s_new, -jnp.inf)


def _dsa_sample_select_kernel(sc_ref, sn_ref, t_ref, j_ref, *, steps, topk, idx_bits):
    scores = jnp.concatenate([sc_ref[...], _new_token_scores(sn_ref[...], steps)], axis=1)
    idx = lax.broadcasted_iota(I32, scores.shape, 1)
    count = lambda fn: jnp.sum(fn(0, scores, idx), axis=1, keepdims=True)
    t, j = _topk_select(count, scores.shape[0], scores.shape[1], float(topk), idx_bits)
    t_ref[...] = t
    j_ref[...] = j


def _dsa_sample_select(scores, snew, steps, topk):
    nb, _, past = scores.shape
    page = snew.shape[-1]
    rows = nb * QP
    rb = math.gcd(rows, 256)
    t, j = pl.pallas_call(
        functools.partial(_dsa_sample_select_kernel, steps=steps, topk=topk,
                          idx_bits=int(math.ceil(math.log2(past + page)))),
        grid=(rows // rb,),
        in_specs=[pl.BlockSpec((rb, past), lambda i: (i, 0)), pl.BlockSpec((rb, page), lambda i: (i, 0))],
        out_specs=[pl.BlockSpec((rb, 1), lambda i: (i, 0)), pl.BlockSpec((rb, 1), lambda i: (i, 0))],
        out_shape=[jax.ShapeDtypeStruct((rows, 1), F32), jax.ShapeDtypeStruct((rows, 1), I32)],
        compiler_params=_cparams(1),
        name="dsa_sample_select",
    )(scores.reshape(rows, past), snew.reshape(rows, page))
    return t.reshape(nb, QP, 1), j.reshape(nb, QP, 1)


def _dsa_sample_attend_kernel(pt_ref, q_ref, kn_ref, vn_ref, sc_ref, sn_ref, t_ref, j_ref,
                              blast_ref, bnew_ref, *rest, pps, n_pages, steps):
    k_refs, v_refs, o_ref = rest[:pps], rest[pps:2 * pps], rest[2 * pps]
    lhs_s, m_s, l_s, acc_s = rest[2 * pps + 1:]
    g = pl.program_id(1)
    last = pl.num_programs(1) - 1
    page = k_refs[0].shape[-1]
    past = n_pages * page
    n = pps * page
    r = N_HEADS // N_KV_HEADS
    grow = r * QP

    @pl.when(g == 0)
    def _():
        m_s[...] = jnp.full(m_s.shape, NEG_BIG, F32)
        l_s[...] = jnp.zeros(l_s.shape, F32)
        acc_s[...] = jnp.zeros(acc_s.shape, F32)
        q = q_ref[0]
        for gg in range(N_KV_HEADS):
            lhs_s[gg] = jnp.concatenate(
                [q[:, (gg * r + u) * HEAD_DIM:(gg * r + u + 1) * HEAD_DIM] for u in range(r)], axis=0).astype(BF16)

    t, j = t_ref[0], j_ref[0]

    def select(sc, idx):
        sel = (sc > t) | ((sc == t) & (idx <= j))
        return jnp.where(sel & (sc > -jnp.inf), 0.0, NEG_BIG)

    def attend(kt, vt, madd, bias):
        mrows = jnp.concatenate([madd] * r, axis=0)
        for gg in range(N_KV_HEADS):
            rows = slice(gg * grow, (gg + 1) * grow)
            feat = slice(gg * HEAD_DIM, (gg + 1) * HEAD_DIM)
            lg = _dot(lhs_s[gg], kt[feat, :]) + mrows
            if bias is not None:
                lg = lg + bias[rows, :]
            m_old = m_s[rows, :]
            m_new = jnp.maximum(m_old, jnp.max(lg, axis=1, keepdims=True))
            p = jnp.exp(lg - m_new)
            alpha = jnp.exp(m_old - m_new)
            l_s[rows, :] = alpha * l_s[rows, :] + jnp.sum(p, axis=1, keepdims=True)
            acc_s[rows, :] = alpha * acc_s[rows, :] + _dot_nt(p.astype(BF16), vt[feat, :])
            m_s[rows, :] = m_new

    def pages(bias):
        kt = jnp.concatenate([k_refs[u][0, 0] for u in range(pps)], axis=1).astype(BF16)
        vt = jnp.concatenate([v_refs[u][0, 0] for u in range(pps)], axis=1).astype(BF16)
        idx = g * n + lax.broadcasted_iota(I32, (QP, n), 1)
        attend(kt, vt, select(sc_ref[0], idx), bias)

    @pl.when(g != last)
    def _():
        pages(None)

    @pl.when(g == last)
    def _():
        pages(jnp.concatenate([jnp.zeros((N_HEADS * QP, n - page), F32), blast_ref[...]], axis=1))
        idx = past + lax.broadcasted_iota(I32, (QP, page), 1)
        attend(_pad_t(kn_ref[0], page).astype(BF16), _pad_t(vn_ref[0], page).astype(BF16),
               select(_new_token_scores(sn_ref[0], steps), idx), bnew_ref[...])
        o = acc_s[...] / l_s[...]
        for h in range(N_HEADS):
            o_ref[0, :, h * HEAD_DIM:(h + 1) * HEAD_DIM] = o[h * QP:(h + 1) * QP, :]


def _dsa_sample(page_table, q, kn, vn, scores, snew, k_t, v_t, rel_bias, layer, steps, pps):
    nb, n_pages = page_table.shape
    page = k_t.shape[-1]
    past = n_pages * page
    nq, nkv = N_HEADS * HEAD_DIM, N_KV_HEADS * HEAD_DIM
    t, j = _dsa_sample_select(scores, snew, steps, min(TOPK_MAX, (past + steps) // 4))
    n_steps = n_pages // pps
    r = N_HEADS // N_KV_HEADS
    tab = _bias_table(rel_bias, 2 * page)
    qq = np.arange(QP)[:, None]
    off = np.arange(page)[None, :]
    blast = tab[:, page + qq - off].reshape(N_HEADS * QP, page)
    bnew = tab[:, np.maximum(qq - off, 0)].reshape(N_HEADS * QP, page)
    bmap = lambda b, g, pt: (b, 0, 0)
    grid_spec = pltpu.PrefetchScalarGridSpec(
        num_scalar_prefetch=1,
        grid=(nb, n_steps),
        in_specs=[pl.BlockSpec((1, QP, nq), bmap),
                  pl.BlockSpec((1, QP, nkv), bmap),
                  pl.BlockSpec((1, QP, nkv), bmap),
                  pl.BlockSpec((1, QP, pps * page), lambda b, g, pt: (b, 0, g)),
                  pl.BlockSpec((1, QP, page), bmap),
                  pl.BlockSpec((1, QP, 1), bmap), pl.BlockSpec((1, QP, 1), bmap),
                  pl.BlockSpec(blast.shape, lambda b, g, pt: (0, 0)),
                  pl.BlockSpec(bnew.shape, lambda b, g, pt: (0, 0))]
                 + _page_specs(k_t.shape, layer, pps) + _page_specs(v_t.shape, layer, pps),
        out_specs=pl.BlockSpec((1, QP, nq), bmap),
        scratch_shapes=[pltpu.VMEM((N_KV_HEADS, r * QP, HEAD_DIM), BF16),
                        pltpu.VMEM((N_HEADS * QP, 1), F32), pltpu.VMEM((N_HEADS * QP, 1), F32),
                        pltpu.VMEM((N_HEADS * QP, HEAD_DIM), F32)])
    return pl.pallas_call(
        functools.partial(_dsa_sample_attend_kernel, pps=pps, n_pages=n_pages, steps=steps),
        grid_spec=grid_spec,
        out_shape=jax.ShapeDtypeStruct((nb, QP, nq), F32),
        compiler_params=_cparams(2),
        name="dsa_sample_attend",
    )(page_table, q, kn, vn, scores, snew, t, j, blast, bnew, *([k_t] * pps), *([v_t] * pps))


SSD_COLS = 512


def _softplus(x):
    return jnp.maximum(x, 0.0) + jnp.log1p(jnp.exp(-jnp.abs(x)))


def _split3(a):
    p1 = a.astype(BF16)
    r1 = a - p1.astype(F32)
    p2 = r1.astype(BF16)
    return p1, p2, (r1 - p2.astype(F32)).astype(BF16)


def _dot_sel_rhs(a, e):
    p1, p2, p3 = _split3(a)
    return _dot(p1, e) + (_dot(p2, e) + _dot(p3, e))


def _dot_sel_lhs(e, a):
    p1, p2, p3 = _split3(a)
    return _dot(e, p1) + (_dot(e, p2) + _dot(e, p3))


def _ssd_prompt_kernel(z_ref, xbc_ref, dtr_ref, cw_ref, cb_ref, dtb_ref, alog_ref, dexp_ref, ng_ref,
                       e_ref, y_ref, nbuf_ref, hf_ref, prev_s, xc_s, y_s, h_s, *, width, nh, inner):
    i = pl.program_id(1)
    last = pl.num_programs(1) - 1
    qn, cdim = xbc_ref.shape[1], xbc_ref.shape[2]
    ngrp, ns, hd = SSD_GROUPS, SSD_STATE, SSD_HEADDIM
    hpg = nh // ngrp

    @pl.when(i == 0)
    def _():
        prev_s[...] = jnp.zeros(prev_s.shape, F32)
        h_s[...] = jnp.zeros(h_s.shape, F32)

    for cb in range(cdim // SSD_COLS):
        cols = slice(cb * SSD_COLS, (cb + 1) * SSD_COLS)
        xr = xbc_ref[0, :, cols]
        prev = prev_s[:, cols]
        conv = xr * cw_ref[width - 1:width, cols]
        for k in range(1, width):
            conv = conv + _shift_rows(xr, prev, k) * cw_ref[width - 1 - k:width - k, cols]
        xc_s[:, cols] = _silu(conv + cb_ref[:, cols])
        prev_s[:, cols] = xr[qn - SUBLANE:, :]
        nbuf_ref[0, :, cols] = xr[qn - (width - 1):, :]

    hl = lax.broadcasted_iota(I32, (qn, LANE), 1)
    dt = jnp.where(hl < nh, _softplus(dtr_ref[0] + dtb_ref[...]), 0.0)
    dta = dt * (-jnp.exp(alog_ref[...]))
    qrow = lax.broadcasted_iota(I32, (qn, qn), 0)
    kcol = lax.broadcasted_iota(I32, (qn, qn), 1)
    causal = kcol <= qrow
    cum = _dot_sel_lhs(jnp.where(causal, 1.0, 0.0).astype(BF16), dta)
    cum_t = cum.T
    cum_last = cum[qn - 1:qn, :]
    ecum = jnp.exp(cum)
    dtw = dt * jnp.exp(cum_last - cum)
    e_last = jnp.exp(cum_last)
    lane_p = lax.broadcasted_iota(I32, (qn, LANE), 1)
    row_p = lax.broadcasted_iota(I32, (LANE, ns), 0)

    for g in range(ngrp):
        gl = slice(g * SSD_COLS, (g + 1) * SSD_COLS)
        e_g = e_ref[:, gl]
        xs_g = xc_s[:, gl]
        dtx_g = xs_g * _dot_sel_rhs(dt, e_g)
        dtxw_g = xs_g * _dot_sel_rhs(dtw, e_g)
        ecx_g = _dot_sel_rhs(ecum, e_g)
        bm = xc_s[:, inner + g * ns:inner + (g + 1) * ns].astype(BF16)
        cm = xc_s[:, inner + (ngrp + g) * ns:inner + (ngrp + g + 1) * ns].astype(BF16)
        cbm = _dot_nt(cm, bm)
        for pi in range(hpg // 2):
            ha = g * hpg + 2 * pi
            pls = slice(pi * LANE, (pi + 1) * LANE)
            lanes = slice(ha * hd, ha * hd + LANE)
            dtx_p = dtx_g[:, pls].astype(BF16)
            yds = []
            for h in (ha, ha + 1):
                seg = cum[:, h:h + 1] - cum_t[h:h + 1, :]
                dec = jnp.exp(jnp.where(causal, seg, -jnp.inf))
                yds.append(_dot((cbm * dec).astype(BF16), dtx_p))
            yd = jnp.where(lane_p < hd, yds[0], yds[1])
            hp = h_s[ha // 2]
            yo = _dot_nt(cm, hp.astype(BF16)) * ecx_g[:, pls]
            y_s[:, lanes] = yd + yo + dexp_ref[:, lanes] * xs_g[:, pls]
            s_new = _dot(dtxw_g[:, pls].T.astype(BF16), bm)
            cd = jnp.where(row_p < hd, e_last[:, ha:ha + 1], e_last[:, ha + 1:ha + 2])
            h_s[ha // 2] = hp * cd + s_new

    for g in range(ngrp):
        gl = slice(g * SSD_COLS, (g + 1) * SSD_COLS)
        yg = y_s[:, gl] * _silu(z_ref[0, :, gl])
        ms = jnp.mean(yg * yg, axis=-1, keepdims=True)
        y_ref[0, :, gl] = ((yg * lax.rsqrt(ms + EPS)) * ng_ref[:, gl]).astype(y_ref.dtype)

    @pl.when(i == last)
    def _():
        hf_ref[0] = h_s[...]


def _head_expand(nh, hd):
    e = np.zeros((LANE, nh * hd), np.float32)
    for h in range(nh):
        e[h, h * hd:(h + 1) * hd] = 1.0
    return jnp.asarray(e, BF16)


def _pad_lanes(v, n=LANE):
    v = v.reshape(1, -1)
    return jnp.pad(v, ((0, 0), (0, n - v.shape[1])))


def _ssd_prompt(z, xbc, dtr, conv_w, conv_b, dt_bias, a_log, d_skip, norm_g):
    bsz, seq, inner = z.shape
    cdim = xbc.shape[-1]
    nh = dt_bias.shape[0]
    width = conv_w.shape[0]
    qn = math.gcd(seq, SSD_CHUNK)
    assert qn == SSD_CHUNK and inner == SSD_GROUPS * SSD_COLS and cdim % SSD_COLS == 0
    dexp = jnp.repeat(d_skip, SSD_HEADDIM).reshape(1, inner)
    blk = lambda n: pl.BlockSpec((1, qn, n), lambda b, i: (b, i, 0))
    y, nbuf, hf = pl.pallas_call(
        functools.partial(_ssd_prompt_kernel, width=width, nh=nh, inner=inner),
        grid=(bsz, seq // qn),
        in_specs=[blk(inner), blk(cdim), blk(LANE),
                  _const2((width, cdim)), _const2((1, cdim)), _const2((1, LANE)), _const2((1, LANE)),
                  _const2((1, inner)), _const2((1, inner)), _const2((LANE, inner))],
        out_specs=[blk(inner),
                   pl.BlockSpec((1, width - 1, cdim), lambda b, i: (b, 0, 0)),
                   pl.BlockSpec((1, nh // 2, LANE, SSD_STATE), lambda b, i: (b, 0, 0, 0))],
        out_shape=[jax.ShapeDtypeStruct((bsz, seq, inner), BF16),
                   jax.ShapeDtypeStruct((bsz, width - 1, cdim), F32),
                   jax.ShapeDtypeStruct((bsz, nh // 2, LANE, SSD_STATE), F32)],
        scratch_shapes=[pltpu.VMEM((SUBLANE, cdim), F32), pltpu.VMEM((qn, cdim), F32),
                        pltpu.VMEM((qn, inner), F32), pltpu.VMEM((nh // 2, LANE, SSD_STATE), F32)],
        compiler_params=_cparams(2),
        name="ssd_prompt",
    )(z, xbc, dtr, conv_w, conv_b.reshape(1, cdim), _pad_lanes(dt_bias), _pad_lanes(a_log),
      dexp, norm_g.reshape(1, inner), _head_expand(nh, SSD_HEADDIM))
    return y, nbuf, hf.reshape(bsz, nh, SSD_HEADDIM, SSD_STATE)


def _ssd_prep_sample_kernel(xbc_ref, buf_ref, dtr_ref, cw_ref, cb_ref, dtb_ref, xc_ref, dt_ref, nbuf_ref,
                            *, width, nb, steps, nh):
    cdim = xbc_ref.shape[1]
    for cb in range(cdim // SSD_COLS):
        cols = slice(cb * SSD_COLS, (cb + 1) * SSD_COLS)
        ext = jnp.concatenate([buf_ref[:, cols], xbc_ref[:, cols]], axis=0)
        conv = ext[0:steps * nb] * cw_ref[0:1, cols]
        for k in range(1, width):
            conv = conv + ext[k * nb:(k + steps) * nb] * cw_ref[k:k + 1, cols]
        xc_ref[:, cols] = _silu(conv + cb_ref[:, cols])
        nbuf_ref[:, cols] = ext[steps * nb:]
    hl = lax.broadcasted_iota(I32, dtr_ref.shape, 1)
    dt_ref[...] = jnp.where(hl < nh, _softplus(dtr_ref[...] + dtb_ref[...]), 0.0)


def _ssd_prep_sample(xbc_tb, buf_kb, dtr_tb, conv_w, conv_b, dt_bias, nb, steps):
    rows, cdim = xbc_tb.shape
    width = conv_w.shape[0]
    return pl.pallas_call(
        functools.partial(_ssd_prep_sample_kernel, width=width, nb=nb, steps=steps, nh=dt_bias.shape[0]),
        out_shape=[jax.ShapeDtypeStruct((rows, cdim), F32),
                   jax.ShapeDtypeStruct((rows, LANE), F32),
                   jax.ShapeDtypeStruct(((width - 1) * nb, cdim), F32)],
        compiler_params=pltpu.CompilerParams(vmem_limit_bytes=VMEM_LIMIT),
        name="ssd_prep_sample",
    )(xbc_tb, buf_kb, dtr_tb, conv_w, conv_b.reshape(1, cdim), _pad_lanes(dt_bias))


def _ssd_scan_sample_kernel(xc_ref, dt_ref, z_ref, h0_ref, alog_ref, dexp_ref, ng_ref, e_ref,
                            y_ref, hf_ref, *, steps, inner):
    ngrp, ns = SSD_GROUPS, SSD_STATE
    xc, dt = xc_ref[0], dt_ref[0]
    row = lax.broadcasted_iota(I32, (QP, LANE), 0)
    cum = dt * (-jnp.exp(alog_ref[...]))
    for sft in (1, 2, 4):
        cum = cum + jnp.where(row >= sft, pltpu.roll(cum, sft, 0), 0.0)
    cum_last = cum[QP - 1:QP, :]
    parts = [dt, jnp.exp(cum), dt * jnp.exp(cum_last - cum), jnp.broadcast_to(jnp.exp(cum_last), (QP, LANE))]
    for s in range(steps):
        parts.append(jnp.where(row >= s, jnp.exp(cum - cum[s:s + 1, :]), 0.0))
    stack = jnp.concatenate(parts, axis=0)

    for g in range(ngrp):
        gl = slice(g * SSD_COLS, (g + 1) * SSD_COLS)
        ex = _dot_sel_rhs(stack, e_ref[:, gl])
        dt_x, ecum_x, dtw_x, el_x = (ex[k * QP:(k + 1) * QP] for k in range(4))
        xs_g = xc[:, gl]
        dtx = xs_g * dt_x
        bm = xc[:, inner + g * ns:inner + (g + 1) * ns]
        cm = xc[:, inner + (ngrp + g) * ns:inner + (ngrp + g + 1) * ns].astype(BF16)
        bmp = jnp.concatenate([bm, jnp.zeros((LANE - QP, ns), F32)], axis=0).astype(BF16)
        cbm = _dot_nt(cm, bmp)
        yd = jnp.zeros((QP, SSD_COLS), F32)
        for s in range(steps):
            yd = yd + (ex[(4 + s) * QP:(5 + s) * QP] * cbm[:, s:s + 1]) * dtx[s:s + 1, :]
        h0g = h0_ref[0, gl, :]
        yo = _dot_nt(cm, h0g.astype(BF16)) * ecum_x
        y = (yd + yo + dexp_ref[:, gl] * xs_g) * _silu(z_ref[0, :, gl])
        ms = jnp.mean(y * y, axis=-1, keepdims=True)
        y_ref[0, :, gl] = (y * lax.rsqrt(ms + EPS)) * ng_ref[:, gl]
        tm = jnp.concatenate([xs_g * dtw_x, el_x[0:1], jnp.zeros((LANE - QP - 1, SSD_COLS), F32)], axis=0)
        tt = tm.T
        hf_ref[0, gl, :] = h0g * tt[:, QP:QP + 1] + _dot(tt.astype(BF16), bmp)


def _ssd_scan_sample(xc, dt, z, h0, a_log, d_skip, norm_g, steps):
    nb, _, cdim = xc.shape
    inner = z.shape[-1]
    nh = a_log.shape[0]
    rows_h = nh * SSD_HEADDIM
    dexp = jnp.repeat(d_skip, SSD_HEADDIM).reshape(1, inner)
    one = lambda shape: pl.BlockSpec(shape, lambda b: (0,) * len(shape))
    return pl.pallas_call(
        functools.partial(_ssd_scan_sample_kernel, steps=steps, inner=inner),
        grid=(nb,),
        in_specs=[pl.BlockSpec((1, QP, cdim), lambda b: (b, 0, 0)),
                  pl.BlockSpec((1, QP, LANE), lambda b: (b, 0, 0)),
                  pl.BlockSpec((1, QP, inner), lambda b: (b, 0, 0)),
                  pl.BlockSpec((1, rows_h, SSD_STATE), lambda b: (b, 0, 0)),
                  one((1, LANE)), one((1, inner)), one((1, inner)), one((LANE, inner))],
        out_specs=[pl.BlockSpec((1, QP, inner), lambda b: (b, 0, 0)),
                   pl.BlockSpec((1, rows_h, SSD_STATE), lambda b: (b, 0, 0))],
        out_shape=[jax.ShapeDtypeStruct((nb, QP, inner), F32),
                   jax.ShapeDtypeStruct((nb, rows_h, SSD_STATE), F32)],
        compiler_params=_cparams(1),
        name="ssd_scan_sample",
    )(xc, dt, z, h0, _pad_lanes(a_log), dexp, norm_g.reshape(1, inner), _head_expand(nh, SSD_HEADDIM))


TM = 512


def _to_steps(a, nb, steps):
    n = a.shape[-1]
    return a.reshape(nb, steps, n).transpose(1, 0, 2).reshape(steps * nb, n)


def _to_batch(a, nb, steps):
    n = a.shape[-1]
    return a.reshape(steps, nb, n).transpose(1, 0, 2).reshape(1, nb * steps, n)


def _pad_q(a, nb, steps):
    n = a.shape[-1]
    return jnp.pad(a.reshape(nb, steps, n), ((0, 0), (0, QP - steps), (0, 0)))


def kernel(x_prompt, x_sample, cache_k, cache_v, cache_kidx, state_s5_re, state_s5_im, state_sconv, state_ssd, state_ssd_conv, page_table, c_prompt, c_sample, rel_bias, ada_w, ada_b, norm_mix, norm_mlp, norm_final, attn_w_in, attn_w_out, s5_lam_re, s5_lam_im, s5_log_dt, s5_b_re, s5_b_im, s5_c_re, s5_c_im, s5_d, s5_w_glu, sc_w_in, sc_w_conv, sc_w_out, ssd_w_in, ssd_conv_w, ssd_conv_b, ssd_dt_bias, ssd_a_log, ssd_d, ssd_norm, ssd_w_out, mlp_w1, mlp_w2):
    bp, seq, d = x_prompt.shape
    nb, steps, _ = x_sample.shape
    depth = ada_w.shape[0]
    n_mixers = 4
    rs = nb * steps
    tm = min(TM, seq)

    rows = bp + nb
    c_all = jnp.pad(jnp.concatenate([c_prompt, c_sample], axis=0), ((0, (-rows) % SUBLANE), (0, 0)))
    ada = _ada(c_all, ada_w, ada_b)

    xp = x_prompt
    xs = x_sample.reshape(1, rs, d)
    outs = {name: [] for name in ("kp", "vp", "kip", "ks", "vs", "kis", "s5pr", "s5pi", "s5sr", "s5si",
                                  "scp", "scs", "ssdp", "ssdcp", "ssds", "ssdcs")}
    nq, nkv = N_HEADS * HEAD_DIM, N_KV_HEADS * HEAD_DIM
    w1_all, w2_all = mlp_w1.astype(BF16), mlp_w2.astype(BF16)
    for i in range(depth):
        m, j = i % n_mixers, i // n_mixers
        mp = [ada[i, :bp, k * d:(k + 1) * d].reshape(bp, 1, d) for k in range(6)]
        ms = [jnp.repeat(ada[i, bp:bp + nb, k * d:(k + 1) * d], steps, axis=0).reshape(1, rs, d)
              for k in range(6)]
        g_mix = norm_mix[i]
        mix_p = mix_s = None
        if m == 0:
            w_in = attn_w_in[j]
            w_qkv = w_in[:, :nq + 2 * nkv].astype(BF16)
            n_idx = IDX_HEADS * IDX_DIM + LANE
            w_idx = jnp.pad(w_in[:, nq + 2 * nkv:], ((0, 0), (0, n_idx - (w_in.shape[1] - nq - 2 * nkv))))
            w_out = attn_w_out[j].astype(BF16)
            page = cache_k.shape[2]
            q, k, v, kw, qcat, kcat, kaug, vaug, ki = _attn_proj(xp, mp[0], mp[1], g_mix, w_qkv, w_idx,
                                                                 max(tm, 2 * MAX_DISTANCE), True, page,
                                                                 2 * MAX_DISTANCE)
            o = _dsa_prompt(q, qcat, kw, kcat, kaug, vaug, rel_bias, MAX_DISTANCE)
            mix_p = ("proj", o, mp[2], w_out)
            to_pages = lambda t: jnp.transpose(t.reshape(bp, seq // page, N_KV_HEADS, HEAD_DIM, page), (0, 1, 4, 2, 3))
            outs["kp"].append(to_pages(k))
            outs["vp"].append(to_pages(v))
            outs["kip"].append(jnp.transpose(ki, (0, 1, 3, 2)))
            q, k, v, kw, qcat, kcat, _, _ = _attn_proj(xs, ms[0], ms[1], g_mix, w_qkv, w_idx, rs, False)
            pps = math.gcd(PAGES_PER_STEP, page_table.shape[1])
            qc_p, kw_p = _pad_q(qcat.astype(F32), nb, steps), _pad_q(kw, nb, steps)
            scores, snew = _dsa_sample_scores(page_table, qc_p, kw_p, _pad_q(kcat.astype(F32), nb, steps),
                                              _pages_t(cache_kidx), j,
                                              math.gcd(2 * PAGES_PER_STEP, page_table.shape[1]))
            o = _dsa_sample(page_table, _pad_q(q, nb, steps), _pad_q(k, nb, steps), _pad_q(v, nb, steps),
                            scores, snew, _pages_t(cache_k), _pages_t(cache_v), rel_bias, j, steps, pps)
            mix_s = ("proj", o[:, :steps].reshape(1, rs, nq), ms[2], w_out)
            outs["ks"].append(k.reshape(nb, steps, N_KV_HEADS, HEAD_DIM))
            outs["vs"].append(v.reshape(nb, steps, N_KV_HEADS, HEAD_DIM))
            outs["kis"].append(kw[..., :IDX_DIM].reshape(nb, steps, IDX_DIM))
        elif m == 1:
            ar, ai, wb, wc = _s5_weights(s5_lam_re[j], s5_lam_im[j], s5_log_dt[j], s5_b_re[j], s5_b_im[j],
                                         s5_c_re[j], s5_c_im[j])
            w_glu = s5_w_glu[j].astype(BF16)
            grp, nst = s5_lam_re.shape[1:]
            z, fr, fi = _s5_prompt(xp, mp[0], mp[1], g_mix, s5_d[j], ar, ai, wb, wc)
            mix_p = ("glu", z, mp[2], w_glu)
            outs["s5pr"].append(fr.reshape(bp, grp, nst))
            outs["s5pi"].append(fi.reshape(bp, grp, nst))
            z, fr, fi = _s5_sample(_to_steps(xs, nb, steps), _to_steps(ms[0], nb, steps),
                                   _to_steps(ms[1], nb, steps), g_mix, s5_d[j], ar, ai, wb, wc,
                                   state_s5_re[j].reshape(nb, grp * nst), state_s5_im[j].reshape(nb, grp * nst),
                                   nb, steps)
            mix_s = ("glu", _to_batch(z, nb, steps), ms[2], w_glu)
            outs["s5sr"].append(fr.reshape(nb, grp, nst))
            outs["s5si"].append(fi.reshape(nb, grp, nst))
        elif m == 2:
            w_in = sc_w_in[j].astype(BF16)
            w_out = sc_w_out[j].astype(BF16)
            width = sc_w_conv.shape[1]
            xp, nbuf = _sconv_prompt(xp, mp[0], mp[1], g_mix, w_in, mp[2], jnp.zeros((bp, width - 1, d), F32),
                                     sc_w_conv[j], w_out, tm)
            outs["scp"].append(nbuf)
            (p,) = _proj(xs, ms[0], ms[1], g_mix, w_in, (3 * d,), rs)
            buf = state_sconv[j].transpose(1, 0, 2).reshape((width - 1) * nb, d)
            y, nbuf = _sconv_sample(_to_steps(p, nb, steps), _to_steps(xs, nb, steps), _to_steps(ms[2], nb, steps),
                                    buf, sc_w_conv[j], w_out, nb, steps)
            xs = _to_batch(y, nb, steps)
            outs["scs"].append(nbuf.reshape(width - 1, nb, d).transpose(1, 0, 2))
        else:
            inner = ssd_norm.shape[1]
            cdim = ssd_conv_w.shape[2]
            nh = ssd_dt_bias.shape[1]
            width = ssd_conv_w.shape[1]
            w_in = jnp.pad(ssd_w_in[j], ((0, 0), (0, LANE - nh))).astype(BF16)
            w_out = ssd_w_out[j].astype(BF16)
            z, xbc, dtr = _proj(xp, mp[0], mp[1], g_mix, w_in, (inner, cdim, LANE), tm)
            y, nbuf, hf = _ssd_prompt(z, xbc, dtr, ssd_conv_w[j], ssd_conv_b[j], ssd_dt_bias[j], ssd_a_log[j],
                                      ssd_d[j], ssd_norm[j])
            mix_p = ("proj", y, mp[2], w_out)
            outs["ssdp"].append(hf)
            outs["ssdcp"].append(nbuf)
            z, xbc, dtr = _proj(xs, ms[0], ms[1], g_mix, w_in, (inner, cdim, LANE), rs)
            buf = state_ssd_conv[j].transpose(1, 0, 2).reshape((width - 1) * nb, cdim)
            xc, dt, nbuf = _ssd_prep_sample(_to_steps(xbc, nb, steps), buf, _to_steps(dtr, nb, steps),
                                            ssd_conv_w[j], ssd_conv_b[j], ssd_dt_bias[j], nb, steps)
            y, hf = _ssd_scan_sample(_pad_q(_to_batch(xc, nb, steps), nb, steps),
                                     _pad_q(_to_batch(dt, nb, steps), nb, steps), _pad_q(z, nb, steps),
                                     state_ssd[j].reshape(nb, nh * SSD_HEADDIM, SSD_STATE),
                                     ssd_a_log[j], ssd_d[j], ssd_norm[j], steps)
            mix_s = ("proj", y[:, :steps].reshape(1, rs, inner), ms[2], w_out)
            outs["ssds"].append(hf.reshape(nb, nh, SSD_HEADDIM, SSD_STATE))
            outs["ssdcs"].append(nbuf.reshape(width - 1, nb, cdim).transpose(1, 0, 2))
        fin = i == depth - 1
        xp = _mlp(xp, mp[3], mp[4], mp[5], norm_mlp[i], w1_all, w2_all, i, norm_final, fin, tm, mix_p)
        xs = _mlp(xs, ms[3], ms[4], ms[5], norm_mlp[i], w1_all, w2_all, i, norm_final, fin, rs, mix_s)
    st = jnp.stack
    return (xp, xs.reshape(nb, steps, d), st(outs["kp"]), st(outs["vp"]), st(outs["kip"]),
            st(outs["ks"]), st(outs["vs"]), st(outs["kis"]), st(outs["s5pr"]), st(outs["s5pi"]),
            st(outs["s5sr"]), st(outs["s5si"]), st(outs["scp"]), st(outs["scs"]),
            st(outs["ssdp"]), st(outs["ssdcp"]), st(outs["ssds"]), st(outs["ssdcs"]))
```

```python
import functools
import math

import jax
import jax.numpy as jnp
import numpy as np
from jax import lax
from jax.experimental import pallas as pl
from jax.experimental.pallas import tpu as pltpu

F32 = jnp.float32
BF16 = jnp.bfloat16
I32 = jnp.int32

EPS = 1e-6
N_HEADS = 16
HEAD_DIM = 64
N_KV_HEADS = 4
IDX_HEADS = 8
IDX_DIM = 64
TOPK_MAX = 256
N_BUCKETS = 32
MAX_DISTANCE = 128
SSD_HEADDIM = 64
SSD_GROUPS = 4
SSD_STATE = 128
SSD_CHUNK = 128

LANE = 128
SUBLANE = 8
VMEM_LIMIT = 56 * 1024 * 1024
NEG_BIG = -1e30
INT_MIN = -2147483648


def _cparams(n_axes):
    return pltpu.CompilerParams(dimension_semantics=("arbitrary",) * n_axes,
                                vmem_limit_bytes=VMEM_LIMIT)


def _dot(a, b):
    return jnp.dot(a, b, preferred_element_type=F32)


def _dot_nt(a, b):
    return lax.dot_general(a, b, (((1,), (1,)), ((), ())), preferred_element_type=F32)


def _split(a):
    hi = a.astype(BF16)
    lo = (a - hi.astype(F32)).astype(BF16)
    return hi, lo


def _dot3(a, b):
    ah, al = _split(a)
    bh, bl = _split(b)
    return _dot(ah, bh) + (_dot(al, bh) + _dot(ah, bl))


def _rms_mod(x, g, shift, scale):
    y = x * lax.rsqrt(jnp.mean(x * x, axis=-1, keepdims=True) + EPS)
    return (y * g) * (1.0 + scale) + shift


def _sigmoid(x):
    return 1.0 / (1.0 + jnp.exp(-x))


def _silu(x):
    return x * _sigmoid(x)


def _row_block(arr, tm):
    n = arr.shape[-1]
    if arr.shape[1] == 1:
        return pl.BlockSpec((1, 1, n), lambda b, i: (b, 0, 0))
    return pl.BlockSpec((1, tm, n), lambda b, i: (b, i, 0))


def _const2(shape):
    return pl.BlockSpec(shape, lambda b, i: (0,) * len(shape))


def _ada_kernel(c_ref, w_ref, b_ref, o_ref):
    o_ref[0] = _dot3(_silu(c_ref[...]), w_ref[0]) + b_ref[0]


def _ada(c_all, ada_w, ada_b):
    depth, d, n = ada_w.shape
    rows = c_all.shape[0]
    tn = 1536
    return pl.pallas_call(
        _ada_kernel,
        grid=(depth, n // tn),
        in_specs=[pl.BlockSpec((rows, d), lambda l, j: (0, 0)),
                  pl.BlockSpec((1, d, tn), lambda l, j: (l, 0, j)),
                  pl.BlockSpec((1, 1, tn), lambda l, j: (l, 0, j))],
        out_specs=pl.BlockSpec((1, rows, tn), lambda l, j: (l, 0, j)),
        out_shape=jax.ShapeDtypeStruct((depth, rows, n), F32),
        compiler_params=_cparams(2),
        name="ada",
    )(c_all, ada_w, ada_b.reshape(depth, 1, n))


def _proj_kernel(x_ref, sh_ref, sc_ref, g_ref, w_ref, *o_refs, splits):
    h = _rms_mod(x_ref[0], g_ref[...], sh_ref[0], sc_ref[0]).astype(BF16)
    off = 0
    for o_ref, n in zip(o_refs, splits):
        o_ref[0] = _dot(h, w_ref[:, off:off + n])
        off += n


def _proj(x, shift, scale, g, w_bf16, splits, tm):
    bsz, seq, d = x.shape
    n = w_bf16.shape[1]
    assert sum(splits) == n and seq % tm == 0
    return pl.pallas_call(
        functools.partial(_proj_kernel, splits=tuple(splits)),
        grid=(bsz, seq // tm),
        in_specs=[pl.BlockSpec((1, tm, d), lambda b, i: (b, i, 0)),
                  _row_block(shift, tm), _row_block(scale, tm),
                  _const2((1, d)), _const2((d, n))],
        out_specs=[pl.BlockSpec((1, tm, s), lambda b, i: (b, i, 0)) for s in splits],
        out_shape=[jax.ShapeDtypeStruct((bsz, seq, s), F32) for s in splits],
        compiler_params=_cparams(2),
        name="proj",
    )(x, shift, scale, g.reshape(1, d), w_bf16)


def _aug_heads(t, col):
    rows = t.shape[0]
    lane = lax.broadcasted_iota(I32, (rows, LANE - HEAD_DIM), 1)
    extra = jnp.where(lane == 0, col, 0.0).astype(F32)
    parts = []
    for g in range(N_KV_HEADS):
        parts += [t[:, g * HEAD_DIM:(g + 1) * HEAD_DIM], extra]
    return jnp.concatenate(parts, axis=1).astype(BF16)


def _attn_proj_kernel(x_ref, sh_ref, sc_ref, g_ref, w_ref, wi_ref,
                      q_ref, k_ref, v_ref, kw_ref, qcat_ref, kcat_ref, kaug_ref, vaug_ref, *ki_refs,
                      nq, nkv, nqi, key_major):
    (ki_ref,) = ki_refs if key_major else (None,)
    h = _rms_mod(x_ref[0], g_ref[...], sh_ref[0], sc_ref[0])
    hb = h.astype(BF16)
    q_ref[0] = (_dot(hb, w_ref[:, 0:nq]) * (HEAD_DIM ** -0.5)).astype(q_ref.dtype)
    k = _dot(hb, w_ref[:, nq:nq + nkv])
    v = _dot(hb, w_ref[:, nq + nkv:nq + 2 * nkv])
    r = _dot3(h, wi_ref[...])
    kw = r[:, nqi:nqi + LANE]
    kw_ref[0] = kw
    if key_major:
        page = k_ref.shape[-1]
        kt, vt, kwt = k.T, v.T, kw.T
        for p in range(k_ref.shape[1]):
            k_ref[0, p] = kt[:, p * page:(p + 1) * page]
            v_ref[0, p] = vt[:, p * page:(p + 1) * page]
            ki_ref[0, p] = kwt[0:IDX_DIM, p * page:(p + 1) * page]
    else:
        k_ref[0] = k
        v_ref[0] = v
    zero = jnp.zeros((h.shape[0], IDX_DIM), BF16)
    parts = []
    for hh in range(IDX_HEADS):
        ah, al = _split(r[:, hh * IDX_DIM:(hh + 1) * IDX_DIM])
        parts += [ah, al, ah, zero]
    qcat_ref[0] = jnp.concatenate(parts, axis=1)
    kcat = _idx_rhs(kw[:, 0:IDX_DIM])
    kaug = _aug_heads(k, 0.0)
    vaug_ref[0] = _aug_heads(v, 1.0)
    if key_major:
        pair = kcat_ref.shape[-1]
        kcat32, kaug32 = kcat.astype(F32), kaug.astype(F32)
        for s in range(kcat_ref.shape[1]):
            kcat_ref[0, s] = kcat32[s * pair:(s + 1) * pair].T.astype(BF16)
            kaug_ref[0, s] = kaug32[s * pair:(s + 1) * pair].T.astype(BF16)
    else:
        kcat_ref[0] = kcat
        kaug_ref[0] = kaug


def _attn_proj(x, shift, scale, g, w_qkv_bf16, w_idx, tm, key_major, page=None, pair=None):
    bsz, seq, d = x.shape
    nq = N_HEADS * HEAD_DIM
    nkv = N_KV_HEADS * HEAD_DIM
    nqi = IDX_HEADS * IDX_DIM
    rows = lambda n, dt: (pl.BlockSpec((1, tm, n), lambda b, i: (b, i, 0)), jax.ShapeDtypeStruct((bsz, seq, n), dt))
    cols = lambda n, dt: (pl.BlockSpec((1, tm // pair, n, pair), lambda b, i: (b, i, 0, 0)),
                          jax.ShapeDtypeStruct((bsz, seq // pair, n, pair), dt))
    keys = cols if key_major else rows
    if key_major:
        ppt = tm // page
        paged = lambda n: (pl.BlockSpec((1, ppt, n, page), lambda b, i: (b, i, 0, 0)),
                           jax.ShapeDtypeStruct((bsz, seq // page, n, page), F32))
        kv = [paged(nkv), paged(nkv)]
    else:
        kv = [rows(nkv, F32), rows(nkv, F32)]
    outs = [rows(nq, BF16 if key_major else F32)] + kv + [rows(LANE, F32), rows(4 * nqi, BF16),
                                   keys(4 * IDX_DIM, BF16), keys(N_KV_HEADS * LANE, BF16),
                                   rows(N_KV_HEADS * LANE, BF16)]
    if key_major:
        outs.append(paged(IDX_DIM))
    return pl.pallas_call(
        functools.partial(_attn_proj_kernel, nq=nq, nkv=nkv, nqi=nqi, key_major=key_major),
        grid=(bsz, seq // tm),
        in_specs=[pl.BlockSpec((1, tm, d), lambda b, i: (b, i, 0)),
                  _row_block(shift, tm), _row_block(scale, tm),
                  _const2((1, d)), _const2(w_qkv_bf16.shape), _const2(w_idx.shape)],
        out_specs=[o[0] for o in outs],
        out_shape=[o[1] for o in outs],
        compiler_params=_cparams(2),
        name="attn_proj",
    )(x, shift, scale, g.reshape(1, d), w_qkv_bf16, w_idx)


def _mlp_kernel(*refs, ff_chunk, final_norm, mixer_out):
    if mixer_out is None:
        x_ref, sh_ref, sc_ref, gate_ref, g_ref, w1_ref, w2_ref, gf_ref, y_ref = refs
        x = x_ref[0]
    else:
        o_ref, gmix_ref, wo_ref, x_ref, sh_ref, sc_ref, gate_ref, g_ref, w1_ref, w2_ref, gf_ref, y_ref = refs
        d = x_ref.shape[-1]
        ob = o_ref[0].astype(BF16)
        if mixer_out == "glu":
            t = _dot(ob, wo_ref[:, :d]) * _sigmoid(_dot(ob, wo_ref[:, d:]))
        else:
            t = _dot(ob, wo_ref[...])
        x = x_ref[0] + gmix_ref[0] * t
    h = _rms_mod(x, g_ref[...], sh_ref[0], sc_ref[0]).astype(BF16)
    dff = w1_ref.shape[-1]
    acc = jnp.zeros(x.shape, F32)
    for c in range(dff // ff_chunk):
        a = jnp.maximum(_dot(h, w1_ref[0, :, c * ff_chunk:(c + 1) * ff_chunk]), 0.0)
        acc = acc + _dot((a * a).astype(BF16), w2_ref[0, c * ff_chunk:(c + 1) * ff_chunk, :])
    y = x + gate_ref[0] * acc
    if final_norm:
        y = (y * lax.rsqrt(jnp.mean(y * y, axis=-1, keepdims=True) + EPS)) * gf_ref[...]
    y_ref[0] = y


def _resident(shape):
    return pl.BlockSpec(shape, lambda b, i: (0,) * len(shape), pipeline_mode=pl.Buffered(1))


def _mlp(x, shift, scale, gate, g, w1_bf16, w2_bf16, layer, g_final, final_norm, tm, mixer=None):
    bsz, seq, d = x.shape
    dff = w1_bf16.shape[-1]
    layer_w = lambda r, c: pl.BlockSpec((1, r, c), lambda b, i: (layer, 0, 0), pipeline_mode=pl.Buffered(1))
    rows = lambda n: pl.BlockSpec((1, tm, n), lambda b, i: (b, i, 0))
    args, specs, kind = [], [], None
    if mixer is not None:
        kind, o, gmix, wo = mixer
        args += [o, gmix, wo]
        specs += [rows(o.shape[-1]), _row_block(gmix, tm), _resident(wo.shape)]
    args += [x, shift, scale, gate, g.reshape(1, d), w1_bf16, w2_bf16, g_final.reshape(1, d)]
    specs += [rows(d), _row_block(shift, tm), _row_block(scale, tm), _row_block(gate, tm),
              _const2((1, d)), layer_w(d, dff), layer_w(dff, d), _const2((1, d))]
    return pl.pallas_call(
        functools.partial(_mlp_kernel, ff_chunk=1024, final_norm=final_norm, mixer_out=kind),
        grid=(bsz, seq // tm),
        in_specs=specs,
        out_specs=rows(d),
        out_shape=jax.ShapeDtypeStruct((bsz, seq, d), F32),
        compiler_params=_cparams(2),
        name="mlp",
    )(*args)


S5_CH = 128


def _s5_disc_kernel(lr_ref, li_ref, ldt_ref, br_ref, bi_ref, ar_ref, ai_ref, bbr_ref, bbi_ref):
    lr, li = lr_ref[...], li_ref[...]
    dt = jnp.exp(ldt_ref[...])
    mag = jnp.exp(lr * dt)
    ab_re, ab_im = mag * jnp.cos(li * dt), mag * jnp.sin(li * dt)
    den = lr * lr + li * li
    nr = ab_re - 1.0
    f_re = (nr * lr + ab_im * li) / den
    f_im = (ab_im * lr - nr * li) / den
    ar_ref[...] = ab_re
    ai_ref[...] = ab_im
    for c in range(br_ref.shape[0]):
        br, bi = br_ref[c], bi_ref[c]
        bbr_ref[c] = f_re * br - f_im * bi
        bbi_ref[c] = f_re * bi + f_im * br


def _s5_weights(lam_re, lam_im, log_dt, b_re, b_im, c_re, c_im):
    g, p = lam_re.shape
    gc = b_re.shape[-1]
    brt = jnp.moveaxis(b_re, 2, 0)
    bit = jnp.moveaxis(b_im, 2, 0)
    ar, ai, bbr, bbi = pl.pallas_call(
        _s5_disc_kernel,
        out_shape=[jax.ShapeDtypeStruct((g, p), F32), jax.ShapeDtypeStruct((g, p), F32),
                   jax.ShapeDtypeStruct((gc, g, p), F32), jax.ShapeDtypeStruct((gc, g, p), F32)],
        name="s5_disc",
    )(lam_re, lam_im, log_dt.reshape(g, 1), brt, bit)
    nblk = (g * gc) // S5_CH
    gpb = g // nblk
    eye = jnp.eye(gpb, dtype=F32)

    def bd_in(bb):
        t = jnp.moveaxis(bb, 0, 1).reshape(nblk, gpb, gc, p)
        return jnp.einsum('ngcp,gh->ngchp', t, eye).reshape(nblk, gpb * gc, gpb * p)

    def bd_out(cc):
        t = cc.reshape(nblk, gpb, gc, p)
        return jnp.einsum('ngcp,gh->ngphc', t, eye).reshape(nblk, gpb * p, gpb * gc)

    wb = jnp.concatenate([bd_in(bbr), bd_in(bbi)], axis=-1).astype(BF16)
    wc = jnp.concatenate([bd_out(c_re), bd_out(c_im)], axis=1).astype(BF16)
    return ar.reshape(nblk, 1, gpb * p), ai.reshape(nblk, 1, gpb * p), wb, wc


def _gelu_tanh(y):
    return 0.5 * y * (1.0 + jnp.tanh(math.sqrt(2.0 / math.pi) * (y + 0.044715 * (y * y * y))))


def _cmul(ar, ai, xr, xi):
    return ar * xr - ai * xi, ar * xi + ai * xr


def _s5_prompt_kernel(x_ref, sh_ref, sc_ref, g_ref, d_ref, ar_ref, ai_ref, wb_ref, wc_ref,
                      perm_ref, unperm_ref, z_ref, fr_ref, fi_ref, u_s, br_s, bi_s, st_r, st_i,
                      *, nseg, seg):
    i = pl.program_id(1)
    nblk = wb_ref.shape[0]
    blk = ar_ref.shape[-1]

    @pl.when(i == 0)
    def _():
        st_r[...] = jnp.zeros(st_r.shape, F32)
        st_i[...] = jnp.zeros(st_i.shape, F32)

    h = _rms_mod(x_ref[0], g_ref[...], sh_ref[0], sc_ref[0])
    u_s[...] = _dot(perm_ref[...], h.astype(BF16))

    for c in range(nblk):
        lo, hi = c * S5_CH, (c + 1) * S5_CH
        uc = u_s[:, lo:hi]
        bu = _dot(uc.astype(BF16), wb_ref[c])
        br_s[...] = bu[:, :blk]
        bi_s[...] = bu[:, blk:]
        ar = jnp.broadcast_to(ar_ref[c], (nseg, blk))
        ai = jnp.broadcast_to(ai_ref[c], (nseg, blk))

        def local(j, carry):
            xr, xi = carry
            r0 = pl.multiple_of(j * nseg, nseg)
            pr, pi = _cmul(ar, ai, xr, xi)
            nr = pr + br_s[pl.ds(r0, nseg), :]
            ni = pi + bi_s[pl.ds(r0, nseg), :]
            br_s[pl.ds(r0, nseg), :] = nr
            bi_s[pl.ds(r0, nseg), :] = ni
            return nr, ni

        zero = jnp.zeros((nseg, blk), F32)
        fr, fi = lax.fori_loop(0, seg, local, (zero, zero), unroll=True)

        pr, pi = ar_ref[c], ai_ref[c]
        for _ in range(int(math.log2(seg))):
            pr, pi = _cmul(pr, pi, pr, pi)
        cr, ci = st_r[c], st_i[c]
        rows_r, rows_i = [], []
        for s in range(nseg):
            rows_r.append(cr)
            rows_i.append(ci)
            tr, ti = _cmul(pr, pi, cr, ci)
            cr, ci = tr + fr[s:s + 1], ti + fi[s:s + 1]
        st_r[c] = cr
        st_i[c] = ci
        fr_ref[0, c] = cr
        fi_ref[0, c] = ci
        dr, di = _cmul(ar, ai, jnp.concatenate(rows_r, axis=0), jnp.concatenate(rows_i, axis=0))

        def fix(j, carry):
            dr, di = carry
            r0 = pl.multiple_of(j * nseg, nseg)
            br_s[pl.ds(r0, nseg), :] = br_s[pl.ds(r0, nseg), :] + dr
            bi_s[pl.ds(r0, nseg), :] = bi_s[pl.ds(r0, nseg), :] + di
            return _cmul(ar, ai, dr, di)

        lax.fori_loop(0, seg, fix, (dr, di), unroll=True)

        y = (_dot(br_s[...].astype(BF16), wc_ref[c, :blk, :])
             - _dot(bi_s[...].astype(BF16), wc_ref[c, blk:, :]))
        u_s[:, lo:hi] = y

    y_tok = _dot_sel_lhs(unperm_ref[...], u_s[...]) + d_ref[...] * h
    z_ref[0] = _gelu_tanh(y_tok).astype(z_ref.dtype)


def _s5_prompt(x, shift, scale, g, d_skip, ar, ai, wb, wc):
    bsz, seq, d = x.shape
    nblk, _, blk = ar.shape
    nseg, seg = SUBLANE, 32
    tm = nseg * seg
    assert seq % tm == 0
    perm = np.zeros((tm, tm), np.float32)
    for s in range(nseg):
        for j in range(seg):
            perm[j * nseg + s, s * seg + j] = 1.0
    unperm = jnp.asarray(perm.T, BF16)
    perm = jnp.asarray(perm, BF16)
    z, fr, fi = pl.pallas_call(
        functools.partial(_s5_prompt_kernel, nseg=nseg, seg=seg),
        grid=(bsz, seq // tm),
        in_specs=[pl.BlockSpec((1, tm, d), lambda b, i: (b, i, 0)),
                  _row_block(shift, tm), _row_block(scale, tm),
                  _const2((1, d)), _const2((1, d)),
                  _const2(ar.shape), _const2(ai.shape), _const2(wb.shape), _const2(wc.shape),
                  _const2((tm, tm)), _const2((tm, tm))],
        out_specs=[pl.BlockSpec((1, tm, d), lambda b, i: (b, i, 0)),
                   pl.BlockSpec((1, nblk, 1, blk), lambda b, i: (b, 0, 0, 0)),
                   pl.BlockSpec((1, nblk, 1, blk), lambda b, i: (b, 0, 0, 0))],
        out_shape=[jax.ShapeDtypeStruct((bsz, seq, d), BF16),
                   jax.ShapeDtypeStruct((bsz, nblk, 1, blk), F32),
                   jax.ShapeDtypeStruct((bsz, nblk, 1, blk), F32)],
        scratch_shapes=[pltpu.VMEM((tm, d), F32), pltpu.VMEM((tm, blk), F32), pltpu.VMEM((tm, blk), F32),
                        pltpu.VMEM((nblk, 1, blk), F32), pltpu.VMEM((nblk, 1, blk), F32)],
        compiler_params=_cparams(2),
        name="s5_prompt",
    )(x, shift, scale, g.reshape(1, d), d_skip.reshape(1, d), ar, ai, wb, wc, perm, unperm)
    return z, fr, fi


def _s5_sample_kernel(x_ref, sh_ref, sc_ref, g_ref, d_ref, ar_ref, ai_ref, wb_ref, wc_ref,
                      s0r_ref, s0i_ref, z_ref, fr_ref, fi_ref, *, nb, steps):
    nblk = wb_ref.shape[0]
    blk = ar_ref.shape[-1]
    u = _rms_mod(x_ref[...], g_ref[...], sh_ref[...], sc_ref[...])
    for c in range(nblk):
        lo, hi = c * S5_CH, (c + 1) * S5_CH
        uc = u[:, lo:hi]
        bu = _dot(uc.astype(BF16), wb_ref[c])
        ar = jnp.broadcast_to(ar_ref[c], (nb, blk))
        ai = jnp.broadcast_to(ai_ref[c], (nb, blk))
        xr, xi = s0r_ref[:, c * blk:(c + 1) * blk], s0i_ref[:, c * blk:(c + 1) * blk]
        xrs, xis = [], []
        for t in range(steps):
            pr, pi = _cmul(ar, ai, xr, xi)
            xr = pr + bu[t * nb:(t + 1) * nb, :blk]
            xi = pi + bu[t * nb:(t + 1) * nb, blk:]
            xrs.append(xr)
            xis.append(xi)
        fr_ref[:, c * blk:(c + 1) * blk] = xr
        fi_ref[:, c * blk:(c + 1) * blk] = xi
        y = (_dot(jnp.concatenate(xrs, axis=0).astype(BF16), wc_ref[c, :blk, :])
             - _dot(jnp.concatenate(xis, axis=0).astype(BF16), wc_ref[c, blk:, :]))
        y = y + d_ref[:, lo:hi] * uc
        z_ref[:, lo:hi] = _gelu_tanh(y)


def _s5_sample(x_tb, shift_tb, scale_tb, g, d_skip, ar, ai, wb, wc, s0r, s0i, nb, steps):
    rows, d = x_tb.shape
    nblk, _, blk = ar.shape
    return pl.pallas_call(
        functools.partial(_s5_sample_kernel, nb=nb, steps=steps),
        out_shape=[jax.ShapeDtypeStruct((rows, d), F32),
                   jax.ShapeDtypeStruct((nb, nblk * blk), F32),
                   jax.ShapeDtypeStruct((nb, nblk * blk), F32)],
        compiler_params=pltpu.CompilerParams(vmem_limit_bytes=VMEM_LIMIT),
        name="s5_sample",
    )(x_tb, shift_tb, scale_tb, g.reshape(1, d), d_skip.reshape(1, d), ar, ai, wb, wc, s0r, s0i)


def _shift_rows(u, prev, k):
    if k == 0:
        return u
    rolled = pltpu.roll(u, k, 0)
    head = pltpu.roll(prev, k, 0)
    row = lax.broadcasted_iota(I32, (SUBLANE, u.shape[1]), 0)
    first = jnp.where(row < k, head, rolled[:SUBLANE])
    return jnp.concatenate([first, rolled[SUBLANE:]], axis=0)


def _sconv_prompt_kernel(x_ref, sh_ref, sc_ref, g_ref, wi_ref, gate_ref, buf_ref, wc_ref, wo_ref,
                         y_ref, nb_ref, prev_s, *, width):
    i = pl.program_id(1)
    d = x_ref.shape[-1]

    @pl.when(i == 0)
    def _():
        prev_s[...] = jnp.zeros(prev_s.shape, F32)
        prev_s[SUBLANE - (width - 1):, :] = buf_ref[0]

    h = _rms_mod(x_ref[0], g_ref[...], sh_ref[0], sc_ref[0]).astype(BF16)
    gb = _dot(h, wi_ref[:, :d])
    u = _dot(h, wi_ref[:, d:2 * d]) * _dot(h, wi_ref[:, 2 * d:])
    prev = prev_s[...]
    conv = u * wc_ref[width - 1:width, :]
    for k in range(1, width):
        conv = conv + _shift_rows(u, prev, k) * wc_ref[width - 1 - k:width - k, :]
    prev_s[...] = u[u.shape[0] - SUBLANE:, :]
    nb_ref[0] = u[u.shape[0] - (width - 1):, :]
    y_ref[0] = x_ref[0] + gate_ref[0] * _dot((gb * conv).astype(BF16), wo_ref[...])


def _sconv_prompt(x, shift, scale, g, w_in_bf16, gate, buf, w_conv, w_out_bf16, tm):
    bsz, seq, d = x.shape
    width = w_conv.shape[0]
    return pl.pallas_call(
        functools.partial(_sconv_prompt_kernel, width=width),
        grid=(bsz, seq // tm),
        in_specs=[pl.BlockSpec((1, tm, d), lambda b, i: (b, i, 0)),
                  _row_block(shift, tm), _row_block(scale, tm), _const2((1, d)),
                  _resident(w_in_bf16.shape), _row_block(gate, tm),
                  pl.BlockSpec((1, width - 1, d), lambda b, i: (b, 0, 0)),
                  _const2((width, d)), _resident((d, d))],
        out_specs=[pl.BlockSpec((1, tm, d), lambda b, i: (b, i, 0)),
                   pl.BlockSpec((1, width - 1, d), lambda b, i: (b, 0, 0))],
        out_shape=[jax.ShapeDtypeStruct((bsz, seq, d), F32),
                   jax.ShapeDtypeStruct((bsz, width - 1, d), F32)],
        scratch_shapes=[pltpu.VMEM((SUBLANE, d), F32)],
        compiler_params=_cparams(2),
        name="sconv_prompt",
    )(x, shift, scale, g.reshape(1, d), w_in_bf16, gate, buf, w_conv, w_out_bf16)


def _sconv_sample_kernel(p_ref, x_ref, gate_ref, buf_ref, wc_ref, wo_ref, y_ref, nb_ref,
                         *, width, nb, steps):
    d = x_ref.shape[-1]
    p = p_ref[...]
    gb, gc, xh = p[:, :d], p[:, d:2 * d], p[:, 2 * d:]
    u = gc * xh
    ext = jnp.concatenate([buf_ref[...], u], axis=0)
    conv = ext[0:steps * nb] * wc_ref[0:1, :]
    for k in range(1, width):
        conv = conv + ext[k * nb:(k + steps) * nb] * wc_ref[k:k + 1, :]
    nb_ref[...] = ext[steps * nb:]
    y_ref[...] = x_ref[...] + gate_ref[...] * _dot((gb * conv).astype(BF16), wo_ref[...])


def _sconv_sample(p_tb, x_tb, gate_tb, buf_kb, w_conv, w_out_bf16, nb, steps):
    rows, d = x_tb.shape
    width = w_conv.shape[0]
    return pl.pallas_call(
        functools.partial(_sconv_sample_kernel, width=width, nb=nb, steps=steps),
        out_shape=[jax.ShapeDtypeStruct((rows, d), F32),
                   jax.ShapeDtypeStruct(((width - 1) * nb, d), F32)],
        compiler_params=pltpu.CompilerParams(vmem_limit_bytes=VMEM_LIMIT),
        name="sconv_sample",
    )(p_tb, x_tb, gate_tb, buf_kb, w_conv, w_out_bf16)


KEY_NEG_INF = -2139095041
KEY_POS_INF = 2139095040


def _code_to_f32(key):
    key = jnp.clip(key, KEY_NEG_INF, KEY_POS_INF)
    return lax.bitcast_convert_type(key ^ ((key >> 31) & 0x7FFFFFFF), F32)


def _stack_heads(qcat):
    w = 4 * IDX_DIM
    return jnp.concatenate([qcat[:, h * w:(h + 1) * w] for h in range(IDX_HEADS)], axis=0)


def _idx_rhs(ki):
    kh, kl = _split(ki)
    return jnp.concatenate([kh, kh, kl, jnp.zeros(kh.shape, BF16)], axis=1)


def _idx_score(dts, wcols, nrows):
    s = wcols[0] * jnp.maximum(dts[0:nrows], 0.0)
    for h in range(1, IDX_HEADS):
        s = s + wcols[h] * jnp.maximum(dts[h * nrows:(h + 1) * nrows], 0.0)
    return s


def _idx_wcols(kw):
    sc = (IDX_HEADS ** -0.5) * (IDX_DIM ** -0.5)
    return [kw[:, IDX_DIM + h:IDX_DIM + h + 1] * sc for h in range(IDX_HEADS)]


def _topk_select(count, rows, lanes, topk, idx_bits, stash=lambda v: (lambda: v)):
    wide = lambda v: jnp.broadcast_to(v, (rows, lanes))

    def bit_step(it, tu):
        mask = jnp.left_shift(jnp.int32(1), 31 - it)
        cand_u = tu | mask
        cand = stash(wide(_code_to_f32(cand_u ^ INT_MIN)))
        cnt = count(lambda c, sc, idx: jnp.where(sc >= cand(), 1.0, 0.0))
        return jnp.where(cnt >= topk, cand_u, tu)

    t = _code_to_f32(lax.fori_loop(0, 32, bit_step, jnp.zeros((rows, 1), I32)) ^ INT_MIN)
    tw = wide(t)
    n_gt = count(lambda c, sc, idx: jnp.where(sc > tw, 1.0, 0.0))
    n_ge = count(lambda c, sc, idx: jnp.where(sc >= tw, 1.0, 0.0))
    need = topk - n_gt
    tied = jnp.max(jnp.where(t > -jnp.inf, n_ge - topk, 0.0)) > 0.0

    def tie_search(_):
        def idx_step(it, j):
            cand = j | jnp.left_shift(jnp.int32(1), idx_bits - 1 - it)
            cw = wide(cand)
            cnt = count(lambda c, sc, idx: jnp.where((sc == tw) & (idx < cw), 1.0, 0.0))
            return jnp.where(cnt < need, cand, j)
        return lax.fori_loop(0, idx_bits, idx_step, jnp.zeros((rows, 1), I32))

    j = lax.cond(tied, tie_search, lambda _: jnp.full((rows, 1), 2 ** idx_bits, I32), 0)
    return t, j


def _dsa_prompt_kernel(q_ref, qcat_ref, kwq_ref, kcat_ref, kaug_ref, vaug_ref, bias_ref, o_ref,
                       sc_s, madd_s, wbc_s, cand_s, qa_s, lg_s, mx_s, sh_s, acc_s, *, tq, topk, idx_bits):
    i = pl.program_id(1)
    nk = i + 1
    nk2 = (nk + 1) // 2
    r = N_HEADS // N_KV_HEADS

    for h, w in enumerate(_idx_wcols(kwq_ref[0])):
        wbc_s[h] = jnp.broadcast_to(w, (tq, tq))
    qrow = lax.broadcasted_iota(I32, (tq, tq), 0)
    kcol = lax.broadcasted_iota(I32, (tq, tq), 1)
    qpos = i * tq + qrow
    wq = 4 * IDX_DIM

    def score_pair(c2, _):
        kc = kcat_ref[0, c2]
        s = None
        hpd = 4
        for hp in range(IDX_HEADS // hpd):
            lhs = jnp.concatenate([qcat_ref[0, :, (hpd * hp + v) * wq:(hpd * hp + v + 1) * wq] for v in range(hpd)],
                                  axis=0)
            dd = jnp.maximum(_dot(lhs, kc), 0.0)
            for v in range(hpd):
                w = wbc_s[hpd * hp + v]
                d = jnp.concatenate([w, w], axis=1) * dd[v * tq:(v + 1) * tq]
                s = d if s is None else s + d
        for u in range(2):
            c = 2 * c2 + u
            sc_s[c] = jnp.where(c * tq + kcol <= qpos, s[:, u * tq:(u + 1) * tq], -jnp.inf)
        return 0

    lax.fori_loop(0, nk2, score_pair, 0)

    def count(fn):
        def body(c2, acc):
            c = 2 * c2
            acc = acc + fn(c, sc_s[c], c * tq + kcol)
            return acc + fn(c + 1, sc_s[c + 1], (c + 1) * tq + kcol)
        acc = lax.fori_loop(0, nk2, body, jnp.zeros((tq, tq), F32))
        return jnp.sum(acc, axis=1, keepdims=True)

    def stash(v):
        cand_s[...] = v
        return lambda: cand_s[...]

    t, j = _topk_select(count, tq, tq, float(topk), idx_bits, stash)
    tw = jnp.broadcast_to(t, (tq, tq))
    jw = jnp.broadcast_to(j, (tq, tq))

    def mask_pair(c2, _):
        for u in range(2):
            c = 2 * c2 + u
            sc = sc_s[c]
            sel = (sc > tw) | ((sc == tw) & (c * tq + kcol <= jw))
            madd_s[c] = jnp.where(sel & (sc > -jnp.inf), 0.0, NEG_BIG)
        return 0

    lax.fori_loop(0, nk2, mask_pair, 0)

    q = q_ref[0].astype(F32)
    zpad = jnp.zeros((tq, LANE - HEAD_DIM), F32)
    for g in range(N_KV_HEADS):
        qa_s[g] = jnp.concatenate(
            [jnp.concatenate([q[:, (g * r + u) * HEAD_DIM:(g * r + u + 1) * HEAD_DIM], zpad], axis=1)
             for u in range(r)], axis=0).astype(BF16)

    def pair_mask(c2, near, g):
        halves = []
        for u in range(2):
            c = 2 * c2 + u
            m = madd_s[c]
            if near:
                back = nk - 1 - c
                carries = (back == 0) | ((back == 1) & (kcol > qrow))
                halves.append([m + jnp.where(carries, bias_ref[g * r + v], 0.0) for v in range(r)])
            else:
                halves.append([m] * r)
        return jnp.concatenate([jnp.concatenate([halves[0][v], halves[1][v]], axis=1) for v in range(r)], axis=0)

    ngh = lg_s.shape[0]
    for half in range(N_KV_HEADS // ngh):
        groups = [half * ngh + k for k in range(ngh)]
        mx_s[...] = jnp.full(mx_s.shape, NEG_BIG, F32)

        def logits_pair(c2, near):
            for k, g in enumerate(groups):
                lg = _dot(qa_s[g], kaug_ref[0, c2, g * LANE:(g + 1) * LANE, :]) + pair_mask(c2, near, g)
                lg_s[k, c2] = lg
                mx_s[k] = jnp.maximum(mx_s[k], jnp.maximum(lg[:, :tq], lg[:, tq:]))

        def far(c4, _):
            logits_pair(2 * c4, False)
            logits_pair(2 * c4 + 1, False)
            return 0

        nfar = jnp.maximum(nk2 - 2, 0)
        lax.fori_loop(0, nfar // 2, far, 0)

        @pl.when(nfar % 2 == 1)
        def _():
            logits_pair(nfar - 1, False)

        @pl.when(nk2 >= 2)
        def _():
            logits_pair(nk2 - 2, True)

        logits_pair(nk2 - 1, True)

        for k in range(ngh):
            sh_s[k] = jnp.broadcast_to(jnp.max(mx_s[k], axis=1, keepdims=True), (r * tq, LANE))
        acc_s[...] = jnp.zeros(acc_s.shape, F32)

        def pv_pair(c2):
            r0 = pl.multiple_of(c2 * 2 * tq, 2 * tq)
            va = vaug_ref[0, pl.ds(r0, 2 * tq), :]
            for k, g in enumerate(groups):
                s = sh_s[k]
                p = jnp.exp(lg_s[k, c2] - jnp.concatenate([s, s], axis=1))
                acc_s[k] = acc_s[k] + _dot(p.astype(BF16), va[:, g * LANE:(g + 1) * LANE])

        def pv_two(c4, _):
            pv_pair(2 * c4)
            pv_pair(2 * c4 + 1)
            return 0

        lax.fori_loop(0, nk2 // 2, pv_two, 0)

        @pl.when(nk2 % 2 == 1)
        def _():
            pv_pair(nk2 - 1)
        for k, g in enumerate(groups):
            acc = acc_s[k]
            o = acc[:, 0:HEAD_DIM] / acc[:, HEAD_DIM:HEAD_DIM + 1]
            heads = jnp.concatenate([o[u * tq:(u + 1) * tq, :] for u in range(r)], axis=1)
            o_ref[0, :, g * r * HEAD_DIM:(g + 1) * r * HEAD_DIM] = heads.astype(o_ref.dtype)


def _t5_bucket(dist):
    n = jnp.maximum(dist, 0)
    max_exact = N_BUCKETS // 2
    nf = jnp.maximum(n, max_exact).astype(F32)
    large = max_exact + (jnp.log(nf / max_exact) / math.log(MAX_DISTANCE / max_exact)
                         * (N_BUCKETS - max_exact)).astype(I32)
    large = jnp.minimum(large, N_BUCKETS - 1)
    return jnp.where(n < max_exact, n, large)


def _bias_table(rel_bias, n):
    tab = rel_bias[_t5_bucket(jnp.arange(n, dtype=I32))]
    return (tab - rel_bias[N_BUCKETS - 1][None, :]).T


DSA_GROUPS_PER_PASS = 2


def _dsa_prompt(q, qcat, kw, kcat, kaug, vaug, rel_bias, tq):
    bsz, seq, nq = q.shape
    topk = min(TOPK_MAX, seq // 4)
    nkc = seq // tq
    r = N_HEADS // N_KV_HEADS
    assert tq == MAX_DISTANCE and seq % (2 * tq) == 0
    tab = _bias_table(rel_bias, tq)
    dmod = (np.arange(tq)[:, None] - np.arange(tq)[None, :]) % tq
    onehot = (jnp.arange(tq, dtype=I32)[:, None, None] == jnp.asarray(dmod, I32)[None]).astype(F32)
    bias = jnp.einsum('hd,dqk->hqk', tab, onehot, precision=lax.Precision.HIGHEST)
    assert kcat.shape[1:] == (seq // (2 * tq), 4 * IDX_DIM, 2 * tq)
    row = lambda n: pl.BlockSpec((1, tq, n), lambda b, i: (b, i, 0))
    tiles = lambda a: pl.BlockSpec((1,) + a.shape[1:], lambda b, i: (b, 0, 0, 0), pipeline_mode=pl.Buffered(1))
    return pl.pallas_call(
        functools.partial(_dsa_prompt_kernel, tq=tq, topk=topk, idx_bits=int(math.log2(seq))),
        grid=(bsz, nkc),
        in_specs=[row(nq), row(qcat.shape[-1]), row(LANE), tiles(kcat), tiles(kaug),
                  pl.BlockSpec((1, seq, vaug.shape[-1]), lambda b, i: (b, 0, 0), pipeline_mode=pl.Buffered(1)),
                  _resident((N_HEADS, tq, tq))],
        out_specs=row(nq),
        out_shape=jax.ShapeDtypeStruct((bsz, seq, nq), BF16),
        scratch_shapes=[pltpu.VMEM((nkc, tq, tq), F32), pltpu.VMEM((nkc, tq, tq), F32),
                        pltpu.VMEM((IDX_HEADS, tq, tq), F32), pltpu.VMEM((tq, tq), F32),
                        pltpu.VMEM((N_KV_HEADS, r * tq, LANE), BF16),
                        pltpu.VMEM((DSA_GROUPS_PER_PASS, nkc // 2, r * tq, 2 * tq), F32),
                        pltpu.VMEM((DSA_GROUPS_PER_PASS, r * tq, tq), F32),
                        pltpu.VMEM((DSA_GROUPS_PER_PASS, r * tq, LANE), F32),
                        pltpu.VMEM((DSA_GROUPS_PER_PASS, r * tq, LANE), F32)],
        compiler_params=_cparams(2),
        name="dsa_prompt",
    )(q, qcat, kw, kcat, kaug, vaug, bias)


QP = SUBLANE
PAGES_PER_STEP = 64


def _page_specs(shape4, layer, pps):
    def spec(u):
        return pl.BlockSpec((1, 1) + tuple(shape4[2:]),
                            lambda b, g, pt: (layer, pt[b, g * pps + u], 0, 0))
    return [spec(u) for u in range(pps)]


def _pages_t(cache):
    nd = cache.ndim
    t = jnp.transpose(cache, (0, 1) + tuple(range(3, nd)) + (2,))
    return t.reshape(t.shape[0], t.shape[1], -1, t.shape[-1])


def _pad_t(a, page):
    return jnp.concatenate([a, jnp.zeros((page - QP, a.shape[1]), F32)], axis=0).T


def _dsa_sample_score_kernel(pt_ref, qcat_ref, kw_ref, kcn_ref, *rest, pps):
    page_refs, o_ref, on_ref = rest[:pps], rest[pps], rest[pps + 1]
    lhs = _stack_heads(qcat_ref[0]).astype(BF16)
    wcols = _idx_wcols(kw_ref[0])
    kt = jnp.concatenate([page_refs[u][0, 0] for u in range(pps)], axis=1)
    kh, kl = _split(kt)
    rhs = jnp.concatenate([kh, kh, kl, jnp.zeros(kh.shape, BF16)], axis=0)
    o_ref[0] = _idx_score(_dot(lhs, rhs), wcols, QP)

    @pl.when(pl.program_id(1) == pl.num_programs(1) - 1)
    def _():
        on_ref[0] = _idx_score(_dot(lhs, _pad_t(kcn_ref[0], on_ref.shape[-1]).astype(BF16)), wcols, QP)


def _dsa_sample_scores(page_table, qcat, kw, kcn, kidx_t, layer, pps):
    nb, n_pages = page_table.shape
    page = kidx_t.shape[-1]
    bmap = lambda b, g, pt: (b, 0, 0)
    grid_spec = pltpu.PrefetchScalarGridSpec(
        num_scalar_prefetch=1,
        grid=(nb, n_pages // pps),
        in_specs=[pl.BlockSpec((1, QP, qcat.shape[-1]), bmap), pl.BlockSpec((1, QP, LANE), bmap),
                  pl.BlockSpec((1, QP, kcn.shape[-1]), bmap)] + _page_specs(kidx_t.shape, layer, pps),
        out_specs=[pl.BlockSpec((1, QP, pps * page), lambda b, g, pt: (b, 0, g)),
                   pl.BlockSpec((1, QP, page), bmap)])
    return pl.pallas_call(
        functools.partial(_dsa_sample_score_kernel, pps=pps),
        grid_spec=grid_spec,
        out_shape=[jax.ShapeDtypeStruct((nb, QP, n_pages * page), F32),
                   jax.ShapeDtypeStruct((nb, QP, page), F32)],
        compiler_params=_cparams(2),
        name="dsa_sample_scores",
    )(page_table, qcat, kw, kcn, *([kidx_t] * pps))


def _new_token_scores(s_new, steps):
    lane = lax.broadcasted_iota(I32, s_new.shape, 1)
    q = lax.broadcasted_iota(I32, s_new.shape, 0) % QP
    return jnp.where((lane <= q) & (lane < steps), s_new, -jnp.inf)


def _dsa_sample_select_kernel(sc_ref, sn_ref, t_ref, j_ref, *, steps, topk, idx_bits):
    scores = jnp.concatenate([sc_ref[...], _new_token_scores(sn_ref[...], steps)], axis=1)
    idx = lax.broadcasted_iota(I32, scores.shape, 1)
    count = lambda fn: jnp.sum(fn(0, scores, idx), axis=1, keepdims=True)
    t, j = _topk_select(count, scores.shape[0], scores.shape[1], float(topk), idx_bits)
    t_ref[...] = t
    j_ref[...] = j


def _dsa_sample_select(scores, snew, steps, topk):
    nb, _, past = scores.shape
    page = snew.shape[-1]
    rows = nb * QP
    rb = math.gcd(rows, 256)
    t, j = pl.pallas_call(
        functools.partial(_dsa_sample_select_kernel, steps=steps, topk=topk,
                          idx_bits=int(math.ceil(math.log2(past + page)))),
        grid=(rows // rb,),
        in_specs=[pl.BlockSpec((rb, past), lambda i: (i, 0)), pl.BlockSpec((rb, page), lambda i: (i, 0))],
        out_specs=[pl.BlockSpec((rb, 1), lambda i: (i, 0)), pl.BlockSpec((rb, 1), lambda i: (i, 0))],
        out_shape=[jax.ShapeDtypeStruct((rows, 1), F32), jax.ShapeDtypeStruct((rows, 1), I32)],
        compiler_params=_cparams(1),
        name="dsa_sample_select",
    )(scores.reshape(rows, past), snew.reshape(rows, page))
    return t.reshape(nb, QP, 1), j.reshape(nb, QP, 1)


def _dsa_sample_attend_kernel(pt_ref, q_ref, kn_ref, vn_ref, sc_ref, sn_ref, t_ref, j_ref,
                              blast_ref, bnew_ref, *rest, pps, n_pages, steps):
    k_refs, v_refs, o_ref = rest[:pps], rest[pps:2 * pps], rest[2 * pps]
    lhs_s, m_s, l_s, acc_s = rest[2 * pps + 1:]
    g = pl.program_id(1)
    last = pl.num_programs(1) - 1
    page = k_refs[0].shape[-1]
    past = n_pages * page
    n = pps * page
    r = N_HEADS // N_KV_HEADS
    grow = r * QP

    @pl.when(g == 0)
    def _():
        m_s[...] = jnp.full(m_s.shape, NEG_BIG, F32)
        l_s[...] = jnp.zeros(l_s.shape, F32)
        acc_s[...] = jnp.zeros(acc_s.shape, F32)
        q = q_ref[0]
        for gg in range(N_KV_HEADS):
            lhs_s[gg] = jnp.concatenate(
                [q[:, (gg * r + u) * HEAD_DIM:(gg * r + u + 1) * HEAD_DIM] for u in range(r)], axis=0).astype(BF16)

    t, j = t_ref[0], j_ref[0]

    def select(sc, idx):
        sel = (sc > t) | ((sc == t) & (idx <= j))
        return jnp.where(sel & (sc > -jnp.inf), 0.0, NEG_BIG)

    def attend(kt, vt, madd, bias):
        mrows = jnp.concatenate([madd] * r, axis=0)
        for gg in range(N_KV_HEADS):
            rows = slice(gg * grow, (gg + 1) * grow)
            feat = slice(gg * HEAD_DIM, (gg + 1) * HEAD_DIM)
            lg = _dot(lhs_s[gg], kt[feat, :]) + mrows
            if bias is not None:
                far = lg.shape[1] - bias.shape[1]
                near = lg[:, far:] + bias[rows, :]
                lg = near if far == 0 else jnp.concatenate([lg[:, :far], near], axis=1)
            m_old = m_s[rows, :]
            m_new = jnp.maximum(m_old, jnp.max(lg, axis=1, keepdims=True))
            p = jnp.exp(lg - m_new)
            alpha = jnp.exp(m_old - m_new)
            l_s[rows, :] = alpha * l_s[rows, :] + jnp.sum(p, axis=1, keepdims=True)
            acc_s[rows, :] = alpha * acc_s[rows, :] + _dot_nt(p.astype(BF16), vt[feat, :])
            m_s[rows, :] = m_new

    def pages(bias):
        kt = jnp.concatenate([k_refs[u][0, 0] for u in range(pps)], axis=1).astype(BF16)
        vt = jnp.concatenate([v_refs[u][0, 0] for u in range(pps)], axis=1).astype(BF16)
        idx = g * n + lax.broadcasted_iota(I32, (QP, n), 1)
        attend(kt, vt, select(sc_ref[0], idx), bias)

    @pl.when(g != last)
    def _():
        pages(None)

    @pl.when(g == last)
    def _():
        pages(blast_ref[...])
        idx = past + lax.broadcasted_iota(I32, (QP, page), 1)
        attend(_pad_t(kn_ref[0], page).astype(BF16), _pad_t(vn_ref[0], page).astype(BF16),
               select(_new_token_scores(sn_ref[0], steps), idx), bnew_ref[...])
        o = acc_s[...] / l_s[...]
        for h in range(N_HEADS):
            o_ref[0, :, h * HEAD_DIM:(h + 1) * HEAD_DIM] = o[h * QP:(h + 1) * QP, :]


def _dsa_sample(page_table, q, kn, vn, scores, snew, k_t, v_t, rel_bias, layer, steps, pps):
    nb, n_pages = page_table.shape
    page = k_t.shape[-1]
    past = n_pages * page
    nq, nkv = N_HEADS * HEAD_DIM, N_KV_HEADS * HEAD_DIM
    t, j = _dsa_sample_select(scores, snew, steps, min(TOPK_MAX, (past + steps) // 4))
    n_steps = n_pages // pps
    r = N_HEADS // N_KV_HEADS
    tab = _bias_table(rel_bias, 2 * page)
    qq = np.arange(QP)[:, None]
    off = np.arange(page)[None, :]
    blast = tab[:, page + qq - off].reshape(N_HEADS * QP, page)
    bnew = tab[:, np.maximum(qq - off, 0)].reshape(N_HEADS * QP, page)
    bmap = lambda b, g, pt: (b, 0, 0)
    grid_spec = pltpu.PrefetchScalarGridSpec(
        num_scalar_prefetch=1,
        grid=(nb, n_steps),
        in_specs=[pl.BlockSpec((1, QP, nq), bmap),
                  pl.BlockSpec((1, QP, nkv), bmap),
                  pl.BlockSpec((1, QP, nkv), bmap),
                  pl.BlockSpec((1, QP, pps * page), lambda b, g, pt: (b, 0, g)),
                  pl.BlockSpec((1, QP, page), bmap),
                  pl.BlockSpec((1, QP, 1), bmap), pl.BlockSpec((1, QP, 1), bmap),
                  pl.BlockSpec(blast.shape, lambda b, g, pt: (0, 0)),
                  pl.BlockSpec(bnew.shape, lambda b, g, pt: (0, 0))]
                 + _page_specs(k_t.shape, layer, pps) + _page_specs(v_t.shape, layer, pps),
        out_specs=pl.BlockSpec((1, QP, nq), bmap),
        scratch_shapes=[pltpu.VMEM((N_KV_HEADS, r * QP, HEAD_DIM), BF16),
                        pltpu.VMEM((N_HEADS * QP, 1), F32), pltpu.VMEM((N_HEADS * QP, 1), F32),
                        pltpu.VMEM((N_HEADS * QP, HEAD_DIM), F32)])
    return pl.pallas_call(
        functools.partial(_dsa_sample_attend_kernel, pps=pps, n_pages=n_pages, steps=steps),
        grid_spec=grid_spec,
        out_shape=jax.ShapeDtypeStruct((nb, QP, nq), F32),
        compiler_params=_cparams(2),
        name="dsa_sample_attend",
    )(page_table, q, kn, vn, scores, snew, t, j, blast, bnew, *([k_t] * pps), *([v_t] * pps))


SSD_COLS = 512


def _softplus(x):
    return jnp.maximum(x, 0.0) + jnp.log1p(jnp.exp(-jnp.abs(x)))


def _split3(a):
    p1 = a.astype(BF16)
    r1 = a - p1.astype(F32)
    p2 = r1.astype(BF16)
    return p1, p2, (r1 - p2.astype(F32)).astype(BF16)


def _dot_sel_rhs(a, e):
    p1, p2, p3 = _split3(a)
    return _dot(p1, e) + (_dot(p2, e) + _dot(p3, e))


def _dot_sel_lhs(e, a):
    p1, p2, p3 = _split3(a)
    return _dot(e, p1) + (_dot(e, p2) + _dot(e, p3))


def _ssd_prompt_kernel(z_ref, xbc_ref, dtr_ref, cw_ref, cb_ref, dtb_ref, alog_ref, dexp_ref, ng_ref,
                       e_ref, y_ref, nbuf_ref, hf_ref, prev_s, xc_s, y_s, h_s, *, width, nh, inner):
    i = pl.program_id(1)
    last = pl.num_programs(1) - 1
    qn, cdim = xbc_ref.shape[1], xbc_ref.shape[2]
    ngrp, ns, hd = SSD_GROUPS, SSD_STATE, SSD_HEADDIM
    hpg = nh // ngrp

    @pl.when(i == 0)
    def _():
        prev_s[...] = jnp.zeros(prev_s.shape, F32)
        h_s[...] = jnp.zeros(h_s.shape, F32)

    for cb in range(cdim // SSD_COLS):
        cols = slice(cb * SSD_COLS, (cb + 1) * SSD_COLS)
        xr = xbc_ref[0, :, cols]
        prev = prev_s[:, cols]
        conv = xr * cw_ref[width - 1:width, cols]
        for k in range(1, width):
            conv = conv + _shift_rows(xr, prev, k) * cw_ref[width - 1 - k:width - k, cols]
        xc_s[:, cols] = _silu(conv + cb_ref[:, cols])
        prev_s[:, cols] = xr[qn - SUBLANE:, :]
        nbuf_ref[0, :, cols] = xr[qn - (width - 1):, :]

    hl = lax.broadcasted_iota(I32, (qn, LANE), 1)
    dt = jnp.where(hl < nh, _softplus(dtr_ref[0] + dtb_ref[...]), 0.0)
    dta = dt * (-jnp.exp(alog_ref[...]))
    qrow = lax.broadcasted_iota(I32, (qn, qn), 0)
    kcol = lax.broadcasted_iota(I32, (qn, qn), 1)
    causal = kcol <= qrow
    cum = _dot_sel_lhs(jnp.where(causal, 1.0, 0.0).astype(BF16), dta)
    cum_t = cum.T
    cum_last = cum[qn - 1:qn, :]
    ecum = jnp.exp(cum)
    dtw = dt * jnp.exp(cum_last - cum)
    e_last = jnp.exp(cum_last)
    lane_p = lax.broadcasted_iota(I32, (qn, LANE), 1)
    row_p = lax.broadcasted_iota(I32, (LANE, ns), 0)

    for g in range(ngrp):
        gl = slice(g * SSD_COLS, (g + 1) * SSD_COLS)
        e_g = e_ref[:, gl]
        xs_g = xc_s[:, gl]
        dtx_g = xs_g * _dot_sel_rhs(dt, e_g)
        dtxw_g = xs_g * _dot_sel_rhs(dtw, e_g)
        ecx_g = _dot_sel_rhs(ecum, e_g)
        bm = xc_s[:, inner + g * ns:inner + (g + 1) * ns].astype(BF16)
        cm = xc_s[:, inner + (ngrp + g) * ns:inner + (ngrp + g + 1) * ns].astype(BF16)
        cbm = _dot_nt(cm, bm)
        for pi in range(hpg // 2):
            ha = g * hpg + 2 * pi
            pls = slice(pi * LANE, (pi + 1) * LANE)
            lanes = slice(ha * hd, ha * hd + LANE)
            dtx_p = dtx_g[:, pls].astype(BF16)
            yds = []
            for h in (ha, ha + 1):
                seg = cum[:, h:h + 1] - cum_t[h:h + 1, :]
                dec = jnp.exp(jnp.where(causal, seg, -jnp.inf))
                yds.append(_dot((cbm * dec).astype(BF16), dtx_p))
            yd = jnp.where(lane_p < hd, yds[0], yds[1])
            hp = h_s[ha // 2]
            yo = _dot_nt(cm, hp.astype(BF16)) * ecx_g[:, pls]
            y_s[:, lanes] = yd + yo + dexp_ref[:, lanes] * xs_g[:, pls]
            s_new = _dot(dtxw_g[:, pls].T.astype(BF16), bm)
            cd = jnp.where(row_p < hd, e_last[:, ha:ha + 1], e_last[:, ha + 1:ha + 2])
            h_s[ha // 2] = hp * cd + s_new

    for g in range(ngrp):
        gl = slice(g * SSD_COLS, (g + 1) * SSD_COLS)
        yg = y_s[:, gl] * _silu(z_ref[0, :, gl])
        ms = jnp.mean(yg * yg, axis=-1, keepdims=True)
        y_ref[0, :, gl] = ((yg * lax.rsqrt(ms + EPS)) * ng_ref[:, gl]).astype(y_ref.dtype)

    @pl.when(i == last)
    def _():
        hf_ref[0] = h_s[...]


def _head_expand(nh, hd):
    e = np.zeros((LANE, nh * hd), np.float32)
    for h in range(nh):
        e[h, h * hd:(h + 1) * hd] = 1.0
    return jnp.asarray(e, BF16)


def _pad_lanes(v, n=LANE):
    v = v.reshape(1, -1)
    return jnp.pad(v, ((0, 0), (0, n - v.shape[1])))


def _ssd_prompt(z, xbc, dtr, conv_w, conv_b, dt_bias, a_log, d_skip, norm_g):
    bsz, seq, inner = z.shape
    cdim = xbc.shape[-1]
    nh = dt_bias.shape[0]
    width = conv_w.shape[0]
    qn = math.gcd(seq, SSD_CHUNK)
    assert qn == SSD_CHUNK and inner == SSD_GROUPS * SSD_COLS and cdim % SSD_COLS == 0
    dexp = jnp.repeat(d_skip, SSD_HEADDIM).reshape(1, inner)
    blk = lambda n: pl.BlockSpec((1, qn, n), lambda b, i: (b, i, 0))
    y, nbuf, hf = pl.pallas_call(
        functools.partial(_ssd_prompt_kernel, width=width, nh=nh, inner=inner),
        grid=(bsz, seq // qn),
        in_specs=[blk(inner), blk(cdim), blk(LANE),
                  _const2((width, cdim)), _const2((1, cdim)), _const2((1, LANE)), _const2((1, LANE)),
                  _const2((1, inner)), _const2((1, inner)), _const2((LANE, inner))],
        out_specs=[blk(inner),
                   pl.BlockSpec((1, width - 1, cdim), lambda b, i: (b, 0, 0)),
                   pl.BlockSpec((1, nh // 2, LANE, SSD_STATE), lambda b, i: (b, 0, 0, 0))],
        out_shape=[jax.ShapeDtypeStruct((bsz, seq, inner), BF16),
                   jax.ShapeDtypeStruct((bsz, width - 1, cdim), F32),
                   jax.ShapeDtypeStruct((bsz, nh // 2, LANE, SSD_STATE), F32)],
        scratch_shapes=[pltpu.VMEM((SUBLANE, cdim), F32), pltpu.VMEM((qn, cdim), F32),
                        pltpu.VMEM((qn, inner), F32), pltpu.VMEM((nh // 2, LANE, SSD_STATE), F32)],
        compiler_params=_cparams(2),
        name="ssd_prompt",
    )(z, xbc, dtr, conv_w, conv_b.reshape(1, cdim), _pad_lanes(dt_bias), _pad_lanes(a_log),
      dexp, norm_g.reshape(1, inner), _head_expand(nh, SSD_HEADDIM))
    return y, nbuf, hf.reshape(bsz, nh, SSD_HEADDIM, SSD_STATE)


def _ssd_prep_sample_kernel(xbc_ref, buf_ref, dtr_ref, cw_ref, cb_ref, dtb_ref, xc_ref, dt_ref, nbuf_ref,
                            *, width, nb, steps, nh):
    cdim = xbc_ref.shape[1]
    for cb in range(cdim // SSD_COLS):
        cols = slice(cb * SSD_COLS, (cb + 1) * SSD_COLS)
        ext = jnp.concatenate([buf_ref[:, cols], xbc_ref[:, cols]], axis=0)
        conv = ext[0:steps * nb] * cw_ref[0:1, cols]
        for k in range(1, width):
            conv = conv + ext[k * nb:(k + steps) * nb] * cw_ref[k:k + 1, cols]
        xc_ref[:, cols] = _silu(conv + cb_ref[:, cols])
        nbuf_ref[:, cols] = ext[steps * nb:]
    hl = lax.broadcasted_iota(I32, dtr_ref.shape, 1)
    dt_ref[...] = jnp.where(hl < nh, _softplus(dtr_ref[...] + dtb_ref[...]), 0.0)


def _ssd_prep_sample(xbc_tb, buf_kb, dtr_tb, conv_w, conv_b, dt_bias, nb, steps):
    rows, cdim = xbc_tb.shape
    width = conv_w.shape[0]
    return pl.pallas_call(
        functools.partial(_ssd_prep_sample_kernel, width=width, nb=nb, steps=steps, nh=dt_bias.shape[0]),
        out_shape=[jax.ShapeDtypeStruct((rows, cdim), F32),
                   jax.ShapeDtypeStruct((rows, LANE), F32),
                   jax.ShapeDtypeStruct(((width - 1) * nb, cdim), F32)],
        compiler_params=pltpu.CompilerParams(vmem_limit_bytes=VMEM_LIMIT),
        name="ssd_prep_sample",
    )(xbc_tb, buf_kb, dtr_tb, conv_w, conv_b.reshape(1, cdim), _pad_lanes(dt_bias))


def _ssd_scan_sample_kernel(xc_ref, dt_ref, z_ref, h0_ref, alog_ref, dexp_ref, ng_ref, e_ref,
                            y_ref, hf_ref, *, steps, inner):
    ngrp, ns = SSD_GROUPS, SSD_STATE
    xc, dt = xc_ref[0], dt_ref[0]
    row = lax.broadcasted_iota(I32, (QP, LANE), 0)
    cum = dt * (-jnp.exp(alog_ref[...]))
    for sft in (1, 2, 4):
        cum = cum + jnp.where(row >= sft, pltpu.roll(cum, sft, 0), 0.0)
    cum_last = cum[QP - 1:QP, :]
    parts = [dt, jnp.exp(cum), dt * jnp.exp(cum_last - cum), jnp.broadcast_to(jnp.exp(cum_last), (QP, LANE))]
    for s in range(steps):
        parts.append(jnp.where(row >= s, jnp.exp(cum - cum[s:s + 1, :]), 0.0))
    stack = jnp.concatenate(parts, axis=0)

    for g in range(ngrp):
        gl = slice(g * SSD_COLS, (g + 1) * SSD_COLS)
        ex = _dot_sel_rhs(stack, e_ref[:, gl])
        dt_x, ecum_x, dtw_x, el_x = (ex[k * QP:(k + 1) * QP] for k in range(4))
        xs_g = xc[:, gl]
        dtx = xs_g * dt_x
        bm = xc[:, inner + g * ns:inner + (g + 1) * ns]
        cm = xc[:, inner + (ngrp + g) * ns:inner + (ngrp + g + 1) * ns].astype(BF16)
        bmp = jnp.concatenate([bm, jnp.zeros((LANE - QP, ns), F32)], axis=0).astype(BF16)
        cbm = _dot_nt(cm, bmp)
        yd = jnp.zeros((QP, SSD_COLS), F32)
        for s in range(steps):
            yd = yd + (ex[(4 + s) * QP:(5 + s) * QP] * cbm[:, s:s + 1]) * dtx[s:s + 1, :]
        h0g = h0_ref[0, gl, :]
        yo = _dot_nt(cm, h0g.astype(BF16)) * ecum_x
        y = (yd + yo + dexp_ref[:, gl] * xs_g) * _silu(z_ref[0, :, gl])
        ms = jnp.mean(y * y, axis=-1, keepdims=True)
        y_ref[0, :, gl] = (y * lax.rsqrt(ms + EPS)) * ng_ref[:, gl]
        tm = jnp.concatenate([xs_g * dtw_x, el_x[0:1], jnp.zeros((LANE - QP - 1, SSD_COLS), F32)], axis=0)
        tt = tm.T
        hf_ref[0, gl, :] = h0g * tt[:, QP:QP + 1] + _dot(tt.astype(BF16), bmp)


def _ssd_scan_sample(xc, dt, z, h0, a_log, d_skip, norm_g, steps):
    nb, _, cdim = xc.shape
    inner = z.shape[-1]
    nh = a_log.shape[0]
    rows_h = nh * SSD_HEADDIM
    dexp = jnp.repeat(d_skip, SSD_HEADDIM).reshape(1, inner)
    one = lambda shape: pl.BlockSpec(shape, lambda b: (0,) * len(shape))
    return pl.pallas_call(
        functools.partial(_ssd_scan_sample_kernel, steps=steps, inner=inner),
        grid=(nb,),
        in_specs=[pl.BlockSpec((1, QP, cdim), lambda b: (b, 0, 0)),
                  pl.BlockSpec((1, QP, LANE), lambda b: (b, 0, 0)),
                  pl.BlockSpec((1, QP, inner), lambda b: (b, 0, 0)),
                  pl.BlockSpec((1, rows_h, SSD_STATE), lambda b: (b, 0, 0)),
                  one((1, LANE)), one((1, inner)), one((1, inner)), one((LANE, inner))],
        out_specs=[pl.BlockSpec((1, QP, inner), lambda b: (b, 0, 0)),
                   pl.BlockSpec((1, rows_h, SSD_STATE), lambda b: (b, 0, 0))],
        out_shape=[jax.ShapeDtypeStruct((nb, QP, inner), F32),
                   jax.ShapeDtypeStruct((nb, rows_h, SSD_STATE), F32)],
        compiler_params=_cparams(1),
        name="ssd_scan_sample",
    )(xc, dt, z, h0, _pad_lanes(a_log), dexp, norm_g.reshape(1, inner), _head_expand(nh, SSD_HEADDIM))


TM = 512


def _to_steps(a, nb, steps):
    n = a.shape[-1]
    return a.reshape(nb, steps, n).transpose(1, 0, 2).reshape(steps * nb, n)


def _to_batch(a, nb, steps):
    n = a.shape[-1]
    return a.reshape(steps, nb, n).transpose(1, 0, 2).reshape(1, nb * steps, n)


def _pad_q(a, nb, steps):
    n = a.shape[-1]
    return jnp.pad(a.reshape(nb, steps, n), ((0, 0), (0, QP - steps), (0, 0)))


def kernel(x_prompt, x_sample, cache_k, cache_v, cache_kidx, state_s5_re, state_s5_im, state_sconv, state_ssd, state_ssd_conv, page_table, c_prompt, c_sample, rel_bias, ada_w, ada_b, norm_mix, norm_mlp, norm_final, attn_w_in, attn_w_out, s5_lam_re, s5_lam_im, s5_log_dt, s5_b_re, s5_b_im, s5_c_re, s5_c_im, s5_d, s5_w_glu, sc_w_in, sc_w_conv, sc_w_out, ssd_w_in, ssd_conv_w, ssd_conv_b, ssd_dt_bias, ssd_a_log, ssd_d, ssd_norm, ssd_w_out, mlp_w1, mlp_w2):
    bp, seq, d = x_prompt.shape
    nb, steps, _ = x_sample.shape
    depth = ada_w.shape[0]
    n_mixers = 4
    rs = nb * steps
    tm = min(TM, seq)

    rows = bp + nb
    c_all = jnp.pad(jnp.concatenate([c_prompt, c_sample], axis=0), ((0, (-rows) % SUBLANE), (0, 0)))
    ada = _ada(c_all, ada_w, ada_b)

    xp = x_prompt
    xs = x_sample.reshape(1, rs, d)
    outs = {name: [] for name in ("kp", "vp", "kip", "ks", "vs", "kis", "s5pr", "s5pi", "s5sr", "s5si",
                                  "scp", "scs", "ssdp", "ssdcp", "ssds", "ssdcs")}
    nq, nkv = N_HEADS * HEAD_DIM, N_KV_HEADS * HEAD_DIM
    w1_all, w2_all = mlp_w1.astype(BF16), mlp_w2.astype(BF16)
    for i in range(depth):
        m, j = i % n_mixers, i // n_mixers
        mp = [ada[i, :bp, k * d:(k + 1) * d].reshape(bp, 1, d) for k in range(6)]
        ms = [jnp.repeat(ada[i, bp:bp + nb, k * d:(k + 1) * d], steps, axis=0).reshape(1, rs, d)
              for k in range(6)]
        g_mix = norm_mix[i]
        mix_p = mix_s = None
        if m == 0:
            w_in = attn_w_in[j]
            w_qkv = w_in[:, :nq + 2 * nkv].astype(BF16)
            n_idx = IDX_HEADS * IDX_DIM + LANE
            w_idx = jnp.pad(w_in[:, nq + 2 * nkv:], ((0, 0), (0, n_idx - (w_in.shape[1] - nq - 2 * nkv))))
            w_out = attn_w_out[j].astype(BF16)
            page = cache_k.shape[2]
            q, k, v, kw, qcat, kcat, kaug, vaug, ki = _attn_proj(xp, mp[0], mp[1], g_mix, w_qkv, w_idx,
                                                                 max(tm, 2 * MAX_DISTANCE), True, page,
                                                                 2 * MAX_DISTANCE)
            o = _dsa_prompt(q, qcat, kw, kcat, kaug, vaug, rel_bias, MAX_DISTANCE)
            mix_p = ("proj", o, mp[2], w_out)
            to_pages = lambda t: jnp.transpose(t.reshape(bp, seq // page, N_KV_HEADS, HEAD_DIM, page), (0, 1, 4, 2, 3))
            outs["kp"].append(to_pages(k))
            outs["vp"].append(to_pages(v))
            outs["kip"].append(jnp.transpose(ki, (0, 1, 3, 2)))
            q, k, v, kw, qcat, kcat, _, _ = _attn_proj(xs, ms[0], ms[1], g_mix, w_qkv, w_idx, rs, False)
            pps = math.gcd(PAGES_PER_STEP, page_table.shape[1])
            qc_p, kw_p = _pad_q(qcat.astype(F32), nb, steps), _pad_q(kw, nb, steps)
            scores, snew = _dsa_sample_scores(page_table, qc_p, kw_p, _pad_q(kcat.astype(F32), nb, steps),
                                              _pages_t(cache_kidx), j,
                                              math.gcd(2 * PAGES_PER_STEP, page_table.shape[1]))
            o = _dsa_sample(page_table, _pad_q(q, nb, steps), _pad_q(k, nb, steps), _pad_q(v, nb, steps),
                            scores, snew, _pages_t(cache_k), _pages_t(cache_v), rel_bias, j, steps, pps)
            mix_s = ("proj", o[:, :steps].reshape(1, rs, nq), ms[2], w_out)
            outs["ks"].append(k.reshape(nb, steps, N_KV_HEADS, HEAD_DIM))
            outs["vs"].append(v.reshape(nb, steps, N_KV_HEADS, HEAD_DIM))
            outs["kis"].append(kw[..., :IDX_DIM].reshape(nb, steps, IDX_DIM))
        elif m == 1:
            ar, ai, wb, wc = _s5_weights(s5_lam_re[j], s5_lam_im[j], s5_log_dt[j], s5_b_re[j], s5_b_im[j],
                                         s5_c_re[j], s5_c_im[j])
            w_glu = s5_w_glu[j].astype(BF16)
            grp, nst = s5_lam_re.shape[1:]
            z, fr, fi = _s5_prompt(xp, mp[0], mp[1], g_mix, s5_d[j], ar, ai, wb, wc)
            mix_p = ("glu", z, mp[2], w_glu)
            outs["s5pr"].append(fr.reshape(bp, grp, nst))
            outs["s5pi"].append(fi.reshape(bp, grp, nst))
            z, fr, fi = _s5_sample(_to_steps(xs, nb, steps), _to_steps(ms[0], nb, steps),
                                   _to_steps(ms[1], nb, steps), g_mix, s5_d[j], ar, ai, wb, wc,
                                   state_s5_re[j].reshape(nb, grp * nst), state_s5_im[j].reshape(nb, grp * nst),
                                   nb, steps)
            mix_s = ("glu", _to_batch(z, nb, steps), ms[2], w_glu)
            outs["s5sr"].append(fr.reshape(nb, grp, nst))
            outs["s5si"].append(fi.reshape(nb, grp, nst))
        elif m == 2:
            w_in = sc_w_in[j].astype(BF16)
            w_out = sc_w_out[j].astype(BF16)
            width = sc_w_conv.shape[1]
            xp, nbuf = _sconv_prompt(xp, mp[0], mp[1], g_mix, w_in, mp[2], jnp.zeros((bp, width - 1, d), F32),
                                     sc_w_conv[j], w_out, tm)
            outs["scp"].append(nbuf)
            (p,) = _proj(xs, ms[0], ms[1], g_mix, w_in, (3 * d,), rs)
            buf = state_sconv[j].transpose(1, 0, 2).reshape((width - 1) * nb, d)
            y, nbuf = _sconv_sample(_to_steps(p, nb, steps), _to_steps(xs, nb, steps), _to_steps(ms[2], nb, steps),
                                    buf, sc_w_conv[j], w_out, nb, steps)
            xs = _to_batch(y, nb, steps)
            outs["scs"].append(nbuf.reshape(width - 1, nb, d).transpose(1, 0, 2))
        else:
            inner = ssd_norm.shape[1]
            cdim = ssd_conv_w.shape[2]
            nh = ssd_dt_bias.shape[1]
            width = ssd_conv_w.shape[1]
            w_in = jnp.pad(ssd_w_in[j], ((0, 0), (0, LANE - nh))).astype(BF16)
            w_out = ssd_w_out[j].astype(BF16)
            z, xbc, dtr = _proj(xp, mp[0], mp[1], g_mix, w_in, (inner, cdim, LANE), tm)
            y, nbuf, hf = _ssd_prompt(z, xbc, dtr, ssd_conv_w[j], ssd_conv_b[j], ssd_dt_bias[j], ssd_a_log[j],
                                      ssd_d[j], ssd_norm[j])
            mix_p = ("proj", y, mp[2], w_out)
            outs["ssdp"].append(hf)
            outs["ssdcp"].append(nbuf)
            z, xbc, dtr = _proj(xs, ms[0], ms[1], g_mix, w_in, (inner, cdim, LANE), rs)
            buf = state_ssd_conv[j].transpose(1, 0, 2).reshape((width - 1) * nb, cdim)
            xc, dt, nbuf = _ssd_prep_sample(_to_steps(xbc, nb, steps), buf, _to_steps(dtr, nb, steps),
                                            ssd_conv_w[j], ssd_conv_b[j], ssd_dt_bias[j], nb, steps)
            y, hf = _ssd_scan_sample(_pad_q(_to_batch(xc, nb, steps), nb, steps),
                                     _pad_q(_to_batch(dt, nb, steps), nb, steps), _pad_q(z, nb, steps),
                                     state_ssd[j].reshape(nb, nh * SSD_HEADDIM, SSD_STATE),
                                     ssd_a_log[j], ssd_d[j], ssd_norm[j], steps)
            mix_s = ("proj", y[:, :steps].reshape(1, rs, inner), ms[2], w_out)
            outs["ssds"].append(hf.reshape(nb, nh, SSD_HEADDIM, SSD_STATE))
            outs["ssdcs"].append(nbuf.reshape(width - 1, nb, cdim).transpose(1, 0, 2))
        fin = i == depth - 1
        xp = _mlp(xp, mp[3], mp[4], mp[5], norm_mlp[i], w1_all, w2_all, i, norm_final, fin, tm, mix_p)
        xs = _mlp(xs, ms[3], ms[4], ms[5], norm_mlp[i], w1_all, w2_all, i, norm_final, fin, rs, mix_s)
    st = jnp.stack
    return (xp, xs.reshape(nb, steps, d), st(outs["kp"]), st(outs["vp"]), st(outs["kip"]),
            st(outs["ks"]), st(outs["vs"]), st(outs["kis"]), st(outs["s5pr"]), st(outs["s5pi"]),
            st(outs["s5sr"]), st(outs["s5si"]), st(outs["scp"]), st(outs["scs"]),
            st(outs["ssdp"]), st(outs["ssdcp"]), st(outs["ssds"]), st(outs["ssdcs"]))
```

```python
import functools
import math

import jax
import jax.numpy as jnp
import numpy as np
from jax import lax
from jax.experimental import pallas as pl
from jax.experimental.pallas import tpu as pltpu

F32 = jnp.float32
BF16 = jnp.bfloat16
I32 = jnp.int32

EPS = 1e-6
N_HEADS = 16
HEAD_DIM = 64
N_KV_HEADS = 4
IDX_HEADS = 8
IDX_DIM = 64
TOPK_MAX = 256
N_BUCKETS = 32
MAX_DISTANCE = 128
SSD_HEADDIM = 64
SSD_GROUPS = 4
SSD_STATE = 128
SSD_CHUNK = 128

LANE = 128
SUBLANE = 8
VMEM_LIMIT = 56 * 1024 * 1024
NEG_BIG = -1e30
INT_MIN = -2147483648


def _cparams(n_axes):
    return pltpu.CompilerParams(dimension_semantics=("arbitrary",) * n_axes,
                                vmem_limit_bytes=VMEM_LIMIT)


def _dot(a, b):
    return jnp.dot(a, b, preferred_element_type=F32)


def _dot_nt(a, b):
    return lax.dot_general(a, b, (((1,), (1,)), ((), ())), preferred_element_type=F32)


def _split(a):
    hi = a.astype(BF16)
    lo = (a - hi.astype(F32)).astype(BF16)
    return hi, lo


def _dot3(a, b):
    ah, al = _split(a)
    bh, bl = _split(b)
    return _dot(ah, bh) + (_dot(al, bh) + _dot(ah, bl))


def _rms_mod(x, g, shift, scale):
    y = x * lax.rsqrt(jnp.mean(x * x, axis=-1, keepdims=True) + EPS)
    return (y * g) * (1.0 + scale) + shift


LOG2E = math.log2(math.e)


def _sigmoid(x):
    return 1.0 / (1.0 + jnp.exp2(x * (-LOG2E)))


def _silu(x):
    return x * _sigmoid(x)


def _row_block(arr, tm):
    n = arr.shape[-1]
    if arr.shape[1] == 1:
        return pl.BlockSpec((1, 1, n), lambda b, i: (b, 0, 0))
    return pl.BlockSpec((1, tm, n), lambda b, i: (b, i, 0))


def _const2(shape):
    return pl.BlockSpec(shape, lambda b, i: (0,) * len(shape))


def _ada_kernel(c_ref, w_ref, b_ref, o_ref):
    o_ref[0] = _dot3(_silu(c_ref[...]), w_ref[0]) + b_ref[0]


def _ada(c_all, ada_w, ada_b):
    depth, d, n = ada_w.shape
    rows = c_all.shape[0]
    tn = 1536
    return pl.pallas_call(
        _ada_kernel,
        grid=(depth, n // tn),
        in_specs=[pl.BlockSpec((rows, d), lambda l, j: (0, 0)),
                  pl.BlockSpec((1, d, tn), lambda l, j: (l, 0, j)),
                  pl.BlockSpec((1, 1, tn), lambda l, j: (l, 0, j))],
        out_specs=pl.BlockSpec((1, rows, tn), lambda l, j: (l, 0, j)),
        out_shape=jax.ShapeDtypeStruct((depth, rows, n), F32),
        compiler_params=_cparams(2),
        name="ada",
    )(c_all, ada_w, ada_b.reshape(depth, 1, n))


def _proj_kernel(x_ref, sh_ref, sc_ref, g_ref, w_ref, *o_refs, splits):
    h = _rms_mod(x_ref[0], g_ref[...], sh_ref[0], sc_ref[0]).astype(BF16)
    off = 0
    for o_ref, n in zip(o_refs, splits):
        o_ref[0] = _dot(h, w_ref[:, off:off + n])
        off += n


def _proj(x, shift, scale, g, w_bf16, splits, tm):
    bsz, seq, d = x.shape
    n = w_bf16.shape[1]
    assert sum(splits) == n and seq % tm == 0
    return pl.pallas_call(
        functools.partial(_proj_kernel, splits=tuple(splits)),
        grid=(bsz, seq // tm),
        in_specs=[pl.BlockSpec((1, tm, d), lambda b, i: (b, i, 0)),
                  _row_block(shift, tm), _row_block(scale, tm),
                  _const2((1, d)), _const2((d, n))],
        out_specs=[pl.BlockSpec((1, tm, s), lambda b, i: (b, i, 0)) for s in splits],
        out_shape=[jax.ShapeDtypeStruct((bsz, seq, s), F32) for s in splits],
        compiler_params=_cparams(2),
        name="proj",
    )(x, shift, scale, g.reshape(1, d), w_bf16)


def _aug_heads(t, col):
    rows = t.shape[0]
    lane = lax.broadcasted_iota(I32, (rows, LANE - HEAD_DIM), 1)
    extra = jnp.where(lane == 0, col, 0.0).astype(F32)
    parts = []
    for g in range(N_KV_HEADS):
        parts += [t[:, g * HEAD_DIM:(g + 1) * HEAD_DIM], extra]
    return jnp.concatenate(parts, axis=1).astype(BF16)


def _attn_proj_kernel(x_ref, sh_ref, sc_ref, g_ref, w_ref, wi_ref,
                      q_ref, k_ref, v_ref, kw_ref, qcat_ref, kcat_ref, kaug_ref, vaug_ref, *ki_refs,
                      nq, nkv, nqi, key_major):
    (ki_ref,) = ki_refs if key_major else (None,)
    h = _rms_mod(x_ref[0], g_ref[...], sh_ref[0], sc_ref[0])
    hb = h.astype(BF16)
    q_ref[0] = (_dot(hb, w_ref[:, 0:nq]) * (HEAD_DIM ** -0.5)).astype(q_ref.dtype)
    k = _dot(hb, w_ref[:, nq:nq + nkv])
    v = _dot(hb, w_ref[:, nq + nkv:nq + 2 * nkv])
    r = _dot3(h, wi_ref[...])
    kw = r[:, nqi:nqi + LANE]
    kw_ref[0] = kw
    if key_major:
        page = k_ref.shape[-1]
        kt, vt, kwt = k.T, v.T, kw.T
        for p in range(k_ref.shape[1]):
            k_ref[0, p] = kt[:, p * page:(p + 1) * page]
            v_ref[0, p] = vt[:, p * page:(p + 1) * page]
            ki_ref[0, p] = kwt[0:IDX_DIM, p * page:(p + 1) * page]
    else:
        k_ref[0] = k
        v_ref[0] = v
    zero = jnp.zeros((h.shape[0], IDX_DIM), BF16)
    parts = []
    for hh in range(IDX_HEADS):
        ah, al = _split(r[:, hh * IDX_DIM:(hh + 1) * IDX_DIM])
        parts += [ah, al, ah, zero]
    qcat_ref[0] = jnp.concatenate(parts, axis=1)
    kcat = _idx_rhs(kw[:, 0:IDX_DIM])
    kaug = _aug_heads(k, 0.0)
    vaug_ref[0] = _aug_heads(v, 1.0)
    if key_major:
        pair = kcat_ref.shape[-1]
        kcat32, kaug32 = kcat.astype(F32), kaug.astype(F32)
        for s in range(kcat_ref.shape[1]):
            kcat_ref[0, s] = kcat32[s * pair:(s + 1) * pair].T.astype(BF16)
            kaug_ref[0, s] = kaug32[s * pair:(s + 1) * pair].T.astype(BF16)
    else:
        kcat_ref[0] = kcat
        kaug_ref[0] = kaug


def _attn_proj(x, shift, scale, g, w_qkv_bf16, w_idx, tm, key_major, page=None, pair=None):
    bsz, seq, d = x.shape
    nq = N_HEADS * HEAD_DIM
    nkv = N_KV_HEADS * HEAD_DIM
    nqi = IDX_HEADS * IDX_DIM
    rows = lambda n, dt: (pl.BlockSpec((1, tm, n), lambda b, i: (b, i, 0)), jax.ShapeDtypeStruct((bsz, seq, n), dt))
    cols = lambda n, dt: (pl.BlockSpec((1, tm // pair, n, pair), lambda b, i: (b, i, 0, 0)),
                          jax.ShapeDtypeStruct((bsz, seq // pair, n, pair), dt))
    keys = cols if key_major else rows
    if key_major:
        ppt = tm // page
        paged = lambda n: (pl.BlockSpec((1, ppt, n, page), lambda b, i: (b, i, 0, 0)),
                           jax.ShapeDtypeStruct((bsz, seq // page, n, page), F32))
        kv = [paged(nkv), paged(nkv)]
    else:
        kv = [rows(nkv, F32), rows(nkv, F32)]
    outs = [rows(nq, BF16 if key_major else F32)] + kv + [rows(LANE, F32), rows(4 * nqi, BF16),
                                   keys(4 * IDX_DIM, BF16), keys(N_KV_HEADS * LANE, BF16),
                                   rows(N_KV_HEADS * LANE, BF16)]
    if key_major:
        outs.append(paged(IDX_DIM))
    return pl.pallas_call(
        functools.partial(_attn_proj_kernel, nq=nq, nkv=nkv, nqi=nqi, key_major=key_major),
        grid=(bsz, seq // tm),
        in_specs=[pl.BlockSpec((1, tm, d), lambda b, i: (b, i, 0)),
                  _row_block(shift, tm), _row_block(scale, tm),
                  _const2((1, d)), _const2(w_qkv_bf16.shape), _const2(w_idx.shape)],
        out_specs=[o[0] for o in outs],
        out_shape=[o[1] for o in outs],
        compiler_params=_cparams(2),
        name="attn_proj",
    )(x, shift, scale, g.reshape(1, d), w_qkv_bf16, w_idx)


def _mlp_kernel(*refs, ff_chunk, final_norm, mixer_out):
    if mixer_out is None:
        x_ref, sh_ref, sc_ref, gate_ref, g_ref, w1_ref, w2_ref, gf_ref, y_ref = refs
        x = x_ref[0]
    else:
        o_ref, gmix_ref, wo_ref, x_ref, sh_ref, sc_ref, gate_ref, g_ref, w1_ref, w2_ref, gf_ref, y_ref = refs
        d = x_ref.shape[-1]
        ob = o_ref[0].astype(BF16)
        if mixer_out == "glu":
            t = _dot(ob, wo_ref[:, :d]) * _sigmoid(_dot(ob, wo_ref[:, d:]))
        else:
            t = _dot(ob, wo_ref[...])
        x = x_ref[0] + gmix_ref[0] * t
    h = _rms_mod(x, g_ref[...], sh_ref[0], sc_ref[0]).astype(BF16)
    dff = w1_ref.shape[-1]
    acc = jnp.zeros(x.shape, F32)
    for c in range(dff // ff_chunk):
        a = jnp.maximum(_dot(h, w1_ref[0, :, c * ff_chunk:(c + 1) * ff_chunk]), 0.0)
        acc = acc + _dot((a * a).astype(BF16), w2_ref[0, c * ff_chunk:(c + 1) * ff_chunk, :])
    y = x + gate_ref[0] * acc
    if final_norm:
        y = (y * lax.rsqrt(jnp.mean(y * y, axis=-1, keepdims=True) + EPS)) * gf_ref[...]
    y_ref[0] = y


def _resident(shape):
    return pl.BlockSpec(shape, lambda b, i: (0,) * len(shape), pipeline_mode=pl.Buffered(1))


def _mlp(x, shift, scale, gate, g, w1_bf16, w2_bf16, layer, g_final, final_norm, tm, mixer=None):
    bsz, seq, d = x.shape
    dff = w1_bf16.shape[-1]
    layer_w = lambda r, c: pl.BlockSpec((1, r, c), lambda b, i: (layer, 0, 0), pipeline_mode=pl.Buffered(1))
    rows = lambda n: pl.BlockSpec((1, tm, n), lambda b, i: (b, i, 0))
    args, specs, kind = [], [], None
    if mixer is not None:
        kind, o, gmix, wo = mixer
        args += [o, gmix, wo]
        specs += [rows(o.shape[-1]), _row_block(gmix, tm), _resident(wo.shape)]
    args += [x, shift, scale, gate, g.reshape(1, d), w1_bf16, w2_bf16, g_final.reshape(1, d)]
    specs += [rows(d), _row_block(shift, tm), _row_block(scale, tm), _row_block(gate, tm),
              _const2((1, d)), layer_w(d, dff), layer_w(dff, d), _const2((1, d))]
    return pl.pallas_call(
        functools.partial(_mlp_kernel, ff_chunk=1024, final_norm=final_norm, mixer_out=kind),
        grid=(bsz, seq // tm),
        in_specs=specs,
        out_specs=rows(d),
        out_shape=jax.ShapeDtypeStruct((bsz, seq, d), F32),
        compiler_params=_cparams(2),
        name="mlp",
    )(*args)


S5_CH = 128


def _s5_disc_kernel(lr_ref, li_ref, ldt_ref, br_ref, bi_ref, ar_ref, ai_ref, bbr_ref, bbi_ref):
    lr, li = lr_ref[...], li_ref[...]
    dt = jnp.exp(ldt_ref[...])
    mag = jnp.exp(lr * dt)
    ab_re, ab_im = mag * jnp.cos(li * dt), mag * jnp.sin(li * dt)
    den = lr * lr + li * li
    nr = ab_re - 1.0
    f_re = (nr * lr + ab_im * li) / den
    f_im = (ab_im * lr - nr * li) / den
    ar_ref[...] = ab_re
    ai_ref[...] = ab_im
    for c in range(br_ref.shape[0]):
        br, bi = br_ref[c], bi_ref[c]
        bbr_ref[c] = f_re * br - f_im * bi
        bbi_ref[c] = f_re * bi + f_im * br


def _s5_weights(lam_re, lam_im, log_dt, b_re, b_im, c_re, c_im):
    g, p = lam_re.shape
    gc = b_re.shape[-1]
    brt = jnp.moveaxis(b_re, 2, 0)
    bit = jnp.moveaxis(b_im, 2, 0)
    ar, ai, bbr, bbi = pl.pallas_call(
        _s5_disc_kernel,
        out_shape=[jax.ShapeDtypeStruct((g, p), F32), jax.ShapeDtypeStruct((g, p), F32),
                   jax.ShapeDtypeStruct((gc, g, p), F32), jax.ShapeDtypeStruct((gc, g, p), F32)],
        name="s5_disc",
    )(lam_re, lam_im, log_dt.reshape(g, 1), brt, bit)
    nblk = (g * gc) // S5_CH
    gpb = g // nblk
    eye = jnp.eye(gpb, dtype=F32)

    def bd_in(bb):
        t = jnp.moveaxis(bb, 0, 1).reshape(nblk, gpb, gc, p)
        return jnp.einsum('ngcp,gh->ngchp', t, eye).reshape(nblk, gpb * gc, gpb * p)

    def bd_out(cc):
        t = cc.reshape(nblk, gpb, gc, p)
        return jnp.einsum('ngcp,gh->ngphc', t, eye).reshape(nblk, gpb * p, gpb * gc)

    wb = jnp.concatenate([bd_in(bbr), bd_in(bbi)], axis=-1).astype(BF16)
    wc = jnp.concatenate([bd_out(c_re), bd_out(c_im)], axis=1).astype(BF16)
    return ar.reshape(nblk, 1, gpb * p), ai.reshape(nblk, 1, gpb * p), wb, wc


def _gelu_tanh(y):
    return 0.5 * y * (1.0 + jnp.tanh(math.sqrt(2.0 / math.pi) * (y + 0.044715 * (y * y * y))))


def _cmul(ar, ai, xr, xi):
    return ar * xr - ai * xi, ar * xi + ai * xr


def _s5_prompt_kernel(x_ref, sh_ref, sc_ref, g_ref, d_ref, ar_ref, ai_ref, wb_ref, wc_ref,
                      perm_ref, unperm_ref, z_ref, fr_ref, fi_ref, u_s, br_s, bi_s, st_r, st_i,
                      *, nseg, seg):
    i = pl.program_id(1)
    nblk = wb_ref.shape[0]
    blk = ar_ref.shape[-1]

    @pl.when(i == 0)
    def _():
        st_r[...] = jnp.zeros(st_r.shape, F32)
        st_i[...] = jnp.zeros(st_i.shape, F32)

    h = _rms_mod(x_ref[0], g_ref[...], sh_ref[0], sc_ref[0])
    u_s[...] = _dot(perm_ref[...], h.astype(BF16))

    for c in range(nblk):
        lo, hi = c * S5_CH, (c + 1) * S5_CH
        uc = u_s[:, lo:hi]
        bu = _dot(uc.astype(BF16), wb_ref[c])
        br_s[...] = bu[:, :blk]
        bi_s[...] = bu[:, blk:]
        ar = jnp.broadcast_to(ar_ref[c], (nseg, blk))
        ai = jnp.broadcast_to(ai_ref[c], (nseg, blk))

        def local(j, carry):
            xr, xi = carry
            r0 = pl.multiple_of(j * nseg, nseg)
            pr, pi = _cmul(ar, ai, xr, xi)
            nr = pr + br_s[pl.ds(r0, nseg), :]
            ni = pi + bi_s[pl.ds(r0, nseg), :]
            br_s[pl.ds(r0, nseg), :] = nr
            bi_s[pl.ds(r0, nseg), :] = ni
            return nr, ni

        zero = jnp.zeros((nseg, blk), F32)
        fr, fi = lax.fori_loop(0, seg, local, (zero, zero), unroll=True)

        pr, pi = ar_ref[c], ai_ref[c]
        for _ in range(int(math.log2(seg))):
            pr, pi = _cmul(pr, pi, pr, pi)
        cr, ci = st_r[c], st_i[c]
        rows_r, rows_i = [], []
        for s in range(nseg):
            rows_r.append(cr)
            rows_i.append(ci)
            tr, ti = _cmul(pr, pi, cr, ci)
            cr, ci = tr + fr[s:s + 1], ti + fi[s:s + 1]
        st_r[c] = cr
        st_i[c] = ci
        fr_ref[0, c] = cr
        fi_ref[0, c] = ci
        dr, di = _cmul(ar, ai, jnp.concatenate(rows_r, axis=0), jnp.concatenate(rows_i, axis=0))

        def fix(j, carry):
            dr, di = carry
            r0 = pl.multiple_of(j * nseg, nseg)
            br_s[pl.ds(r0, nseg), :] = br_s[pl.ds(r0, nseg), :] + dr
            bi_s[pl.ds(r0, nseg), :] = bi_s[pl.ds(r0, nseg), :] + di
            return _cmul(ar, ai, dr, di)

        lax.fori_loop(0, seg, fix, (dr, di), unroll=True)

        y = (_dot(br_s[...].astype(BF16), wc_ref[c, :blk, :])
             - _dot(bi_s[...].astype(BF16), wc_ref[c, blk:, :]))
        u_s[:, lo:hi] = y

    y_tok = _dot_sel_lhs(unperm_ref[...], u_s[...]) + d_ref[...] * h
    z_ref[0] = _gelu_tanh(y_tok).astype(z_ref.dtype)


def _s5_prompt(x, shift, scale, g, d_skip, ar, ai, wb, wc):
    bsz, seq, d = x.shape
    nblk, _, blk = ar.shape
    nseg, seg = SUBLANE, 32
    tm = nseg * seg
    assert seq % tm == 0
    perm = np.zeros((tm, tm), np.float32)
    for s in range(nseg):
        for j in range(seg):
            perm[j * nseg + s, s * seg + j] = 1.0
    unperm = jnp.asarray(perm.T, BF16)
    perm = jnp.asarray(perm, BF16)
    z, fr, fi = pl.pallas_call(
        functools.partial(_s5_prompt_kernel, nseg=nseg, seg=seg),
        grid=(bsz, seq // tm),
        in_specs=[pl.BlockSpec((1, tm, d), lambda b, i: (b, i, 0)),
                  _row_block(shift, tm), _row_block(scale, tm),
                  _const2((1, d)), _const2((1, d)),
                  _const2(ar.shape), _const2(ai.shape), _const2(wb.shape), _const2(wc.shape),
                  _const2((tm, tm)), _const2((tm, tm))],
        out_specs=[pl.BlockSpec((1, tm, d), lambda b, i: (b, i, 0)),
                   pl.BlockSpec((1, nblk, 1, blk), lambda b, i: (b, 0, 0, 0)),
                   pl.BlockSpec((1, nblk, 1, blk), lambda b, i: (b, 0, 0, 0))],
        out_shape=[jax.ShapeDtypeStruct((bsz, seq, d), BF16),
                   jax.ShapeDtypeStruct((bsz, nblk, 1, blk), F32),
                   jax.ShapeDtypeStruct((bsz, nblk, 1, blk), F32)],
        scratch_shapes=[pltpu.VMEM((tm, d), F32), pltpu.VMEM((tm, blk), F32), pltpu.VMEM((tm, blk), F32),
                        pltpu.VMEM((nblk, 1, blk), F32), pltpu.VMEM((nblk, 1, blk), F32)],
        compiler_params=_cparams(2),
        name="s5_prompt",
    )(x, shift, scale, g.reshape(1, d), d_skip.reshape(1, d), ar, ai, wb, wc, perm, unperm)
    return z, fr, fi


def _s5_sample_kernel(x_ref, sh_ref, sc_ref, g_ref, d_ref, ar_ref, ai_ref, wb_ref, wc_ref,
                      s0r_ref, s0i_ref, z_ref, fr_ref, fi_ref, *, nb, steps):
    nblk = wb_ref.shape[0]
    blk = ar_ref.shape[-1]
    u = _rms_mod(x_ref[...], g_ref[...], sh_ref[...], sc_ref[...])
    for c in range(nblk):
        lo, hi = c * S5_CH, (c + 1) * S5_CH
        uc = u[:, lo:hi]
        bu = _dot(uc.astype(BF16), wb_ref[c])
        ar = jnp.broadcast_to(ar_ref[c], (nb, blk))
        ai = jnp.broadcast_to(ai_ref[c], (nb, blk))
        xr, xi = s0r_ref[:, c * blk:(c + 1) * blk], s0i_ref[:, c * blk:(c + 1) * blk]
        xrs, xis = [], []
        for t in range(steps):
            pr, pi = _cmul(ar, ai, xr, xi)
            xr = pr + bu[t * nb:(t + 1) * nb, :blk]
            xi = pi + bu[t * nb:(t + 1) * nb, blk:]
            xrs.append(xr)
            xis.append(xi)
        fr_ref[:, c * blk:(c + 1) * blk] = xr
        fi_ref[:, c * blk:(c + 1) * blk] = xi
        y = (_dot(jnp.concatenate(xrs, axis=0).astype(BF16), wc_ref[c, :blk, :])
             - _dot(jnp.concatenate(xis, axis=0).astype(BF16), wc_ref[c, blk:, :]))
        y = y + d_ref[:, lo:hi] * uc
        z_ref[:, lo:hi] = _gelu_tanh(y)


def _s5_sample(x_tb, shift_tb, scale_tb, g, d_skip, ar, ai, wb, wc, s0r, s0i, nb, steps):
    rows, d = x_tb.shape
    nblk, _, blk = ar.shape
    return pl.pallas_call(
        functools.partial(_s5_sample_kernel, nb=nb, steps=steps),
        out_shape=[jax.ShapeDtypeStruct((rows, d), F32),
                   jax.ShapeDtypeStruct((nb, nblk * blk), F32),
                   jax.ShapeDtypeStruct((nb, nblk * blk), F32)],
        compiler_params=pltpu.CompilerParams(vmem_limit_bytes=VMEM_LIMIT),
        name="s5_sample",
    )(x_tb, shift_tb, scale_tb, g.reshape(1, d), d_skip.reshape(1, d), ar, ai, wb, wc, s0r, s0i)


def _shift_rows(u, prev, k):
    if k == 0:
        return u
    rolled = pltpu.roll(u, k, 0)
    head = pltpu.roll(prev, k, 0)
    row = lax.broadcasted_iota(I32, (SUBLANE, u.shape[1]), 0)
    first = jnp.where(row < k, head, rolled[:SUBLANE])
    return jnp.concatenate([first, rolled[SUBLANE:]], axis=0)


def _sconv_prompt_kernel(x_ref, sh_ref, sc_ref, g_ref, wi_ref, gate_ref, buf_ref, wc_ref, wo_ref,
                         y_ref, nb_ref, prev_s, *, width):
    i = pl.program_id(1)
    d = x_ref.shape[-1]

    @pl.when(i == 0)
    def _():
        prev_s[...] = jnp.zeros(prev_s.shape, F32)
        prev_s[SUBLANE - (width - 1):, :] = buf_ref[0]

    h = _rms_mod(x_ref[0], g_ref[...], sh_ref[0], sc_ref[0]).astype(BF16)
    gb = _dot(h, wi_ref[:, :d])
    u = _dot(h, wi_ref[:, d:2 * d]) * _dot(h, wi_ref[:, 2 * d:])
    prev = prev_s[...]
    conv = u * wc_ref[width - 1:width, :]
    for k in range(1, width):
        conv = conv + _shift_rows(u, prev, k) * wc_ref[width - 1 - k:width - k, :]
    prev_s[...] = u[u.shape[0] - SUBLANE:, :]
    nb_ref[0] = u[u.shape[0] - (width - 1):, :]
    y_ref[0] = x_ref[0] + gate_ref[0] * _dot((gb * conv).astype(BF16), wo_ref[...])


def _sconv_prompt(x, shift, scale, g, w_in_bf16, gate, buf, w_conv, w_out_bf16, tm):
    bsz, seq, d = x.shape
    width = w_conv.shape[0]
    return pl.pallas_call(
        functools.partial(_sconv_prompt_kernel, width=width),
        grid=(bsz, seq // tm),
        in_specs=[pl.BlockSpec((1, tm, d), lambda b, i: (b, i, 0)),
                  _row_block(shift, tm), _row_block(scale, tm), _const2((1, d)),
                  _resident(w_in_bf16.shape), _row_block(gate, tm),
                  pl.BlockSpec((1, width - 1, d), lambda b, i: (b, 0, 0)),
                  _const2((width, d)), _resident((d, d))],
        out_specs=[pl.BlockSpec((1, tm, d), lambda b, i: (b, i, 0)),
                   pl.BlockSpec((1, width - 1, d), lambda b, i: (b, 0, 0))],
        out_shape=[jax.ShapeDtypeStruct((bsz, seq, d), F32),
                   jax.ShapeDtypeStruct((bsz, width - 1, d), F32)],
        scratch_shapes=[pltpu.VMEM((SUBLANE, d), F32)],
        compiler_params=_cparams(2),
        name="sconv_prompt",
    )(x, shift, scale, g.reshape(1, d), w_in_bf16, gate, buf, w_conv, w_out_bf16)


def _sconv_sample_kernel(p_ref, x_ref, gate_ref, buf_ref, wc_ref, wo_ref, y_ref, nb_ref,
                         *, width, nb, steps):
    d = x_ref.shape[-1]
    p = p_ref[...]
    gb, gc, xh = p[:, :d], p[:, d:2 * d], p[:, 2 * d:]
    u = gc * xh
    ext = jnp.concatenate([buf_ref[...], u], axis=0)
    conv = ext[0:steps * nb] * wc_ref[0:1, :]
    for k in range(1, width):
        conv = conv + ext[k * nb:(k + steps) * nb] * wc_ref[k:k + 1, :]
    nb_ref[...] = ext[steps * nb:]
    y_ref[...] = x_ref[...] + gate_ref[...] * _dot((gb * conv).astype(BF16), wo_ref[...])


def _sconv_sample(p_tb, x_tb, gate_tb, buf_kb, w_conv, w_out_bf16, nb, steps):
    rows, d = x_tb.shape
    width = w_conv.shape[0]
    return pl.pallas_call(
        functools.partial(_sconv_sample_kernel, width=width, nb=nb, steps=steps),
        out_shape=[jax.ShapeDtypeStruct((rows, d), F32),
                   jax.ShapeDtypeStruct(((width - 1) * nb, d), F32)],
        compiler_params=pltpu.CompilerParams(vmem_limit_bytes=VMEM_LIMIT),
        name="sconv_sample",
    )(p_tb, x_tb, gate_tb, buf_kb, w_conv, w_out_bf16)


KEY_NEG_INF = -2139095041
KEY_POS_INF = 2139095040


def _code_to_f32(key):
    key = jnp.clip(key, KEY_NEG_INF, KEY_POS_INF)
    return lax.bitcast_convert_type(key ^ ((key >> 31) & 0x7FFFFFFF), F32)


def _stack_heads(qcat):
    w = 4 * IDX_DIM
    return jnp.concatenate([qcat[:, h * w:(h + 1) * w] for h in range(IDX_HEADS)], axis=0)


def _idx_rhs(ki):
    kh, kl = _split(ki)
    return jnp.concatenate([kh, kh, kl, jnp.zeros(kh.shape, BF16)], axis=1)


def _idx_score(dts, wcols, nrows):
    s = wcols[0] * jnp.maximum(dts[0:nrows], 0.0)
    for h in range(1, IDX_HEADS):
        s = s + wcols[h] * jnp.maximum(dts[h * nrows:(h + 1) * nrows], 0.0)
    return s


def _idx_wcols(kw):
    sc = (IDX_HEADS ** -0.5) * (IDX_DIM ** -0.5)
    return [kw[:, IDX_DIM + h:IDX_DIM + h + 1] * sc for h in range(IDX_HEADS)]


def _topk_select(count, rows, lanes, topk, idx_bits, stash=lambda v: (lambda: v)):
    wide = lambda v: jnp.broadcast_to(v, (rows, lanes))

    def bit_step(it, tu):
        mask = jnp.left_shift(jnp.int32(1), 31 - it)
        cand_u = tu | mask
        cand = stash(wide(_code_to_f32(cand_u ^ INT_MIN)))
        cnt = count(lambda c, sc, idx: jnp.where(sc >= cand(), 1.0, 0.0))
        return jnp.where(cnt >= topk, cand_u, tu)

    t = _code_to_f32(lax.fori_loop(0, 32, bit_step, jnp.zeros((rows, 1), I32)) ^ INT_MIN)
    tw = wide(t)
    n_gt = count(lambda c, sc, idx: jnp.where(sc > tw, 1.0, 0.0))
    n_ge = count(lambda c, sc, idx: jnp.where(sc >= tw, 1.0, 0.0))
    need = topk - n_gt
    tied = jnp.max(jnp.where(t > -jnp.inf, n_ge - topk, 0.0)) > 0.0

    def tie_search(_):
        def idx_step(it, j):
            cand = j | jnp.left_shift(jnp.int32(1), idx_bits - 1 - it)
            cw = wide(cand)
            cnt = count(lambda c, sc, idx: jnp.where((sc == tw) & (idx < cw), 1.0, 0.0))
            return jnp.where(cnt < need, cand, j)
        return lax.fori_loop(0, idx_bits, idx_step, jnp.zeros((rows, 1), I32))

    j = lax.cond(tied, tie_search, lambda _: jnp.full((rows, 1), 2 ** idx_bits, I32), 0)
    return t, j


def _dsa_prompt_kernel(q_ref, qcat_ref, kwq_ref, kcat_ref, kaug_ref, vaug_ref, bias_ref, o_ref,
                       sc_s, madd_s, wbc_s, cand_s, qa_s, lg_s, mx_s, sh_s, acc_s, *, tq, topk, idx_bits):
    i = pl.program_id(1)
    nk = i + 1
    nk2 = (nk + 1) // 2
    r = N_HEADS // N_KV_HEADS

    for h, w in enumerate(_idx_wcols(kwq_ref[0])):
        wbc_s[h] = jnp.broadcast_to(w, (tq, tq))
    qrow = lax.broadcasted_iota(I32, (tq, tq), 0)
    kcol = lax.broadcasted_iota(I32, (tq, tq), 1)
    qpos = i * tq + qrow
    wq = 4 * IDX_DIM

    def score_pair(c2, _):
        kc = kcat_ref[0, c2]
        s = None
        hpd = 4
        for hp in range(IDX_HEADS // hpd):
            lhs = jnp.concatenate([qcat_ref[0, :, (hpd * hp + v) * wq:(hpd * hp + v + 1) * wq] for v in range(hpd)],
                                  axis=0)
            dd = jnp.maximum(_dot(lhs, kc), 0.0)
            for v in range(hpd):
                w = wbc_s[hpd * hp + v]
                d = jnp.concatenate([w, w], axis=1) * dd[v * tq:(v + 1) * tq]
                s = d if s is None else s + d
        for u in range(2):
            c = 2 * c2 + u
            sc_s[c] = jnp.where(c * tq + kcol <= qpos, s[:, u * tq:(u + 1) * tq], -jnp.inf)
        return 0

    lax.fori_loop(0, nk2, score_pair, 0)

    def count(fn):
        def body(c2, acc):
            c = 2 * c2
            acc = acc + fn(c, sc_s[c], c * tq + kcol)
            return acc + fn(c + 1, sc_s[c + 1], (c + 1) * tq + kcol)
        acc = lax.fori_loop(0, nk2, body, jnp.zeros((tq, tq), F32))
        return jnp.sum(acc, axis=1, keepdims=True)

    def stash(v):
        cand_s[...] = v
        return lambda: cand_s[...]

    t, j = _topk_select(count, tq, tq, float(topk), idx_bits, stash)
    tw = jnp.broadcast_to(t, (tq, tq))
    jw = jnp.broadcast_to(j, (tq, tq))

    def mask_pair(c2, _):
        for u in range(2):
            c = 2 * c2 + u
            sc = sc_s[c]
            sel = (sc > tw) | ((sc == tw) & (c * tq + kcol <= jw))
            madd_s[c] = jnp.where(sel & (sc > -jnp.inf), 0.0, NEG_BIG)
        return 0

    lax.fori_loop(0, nk2, mask_pair, 0)

    q = q_ref[0].astype(F32)
    zpad = jnp.zeros((tq, LANE - HEAD_DIM), F32)
    for g in range(N_KV_HEADS):
        qa_s[g] = jnp.concatenate(
            [jnp.concatenate([q[:, (g * r + u) * HEAD_DIM:(g * r + u + 1) * HEAD_DIM], zpad], axis=1)
             for u in range(r)], axis=0).astype(BF16)

    def pair_mask(c2, near, g):
        halves = []
        for u in range(2):
            c = 2 * c2 + u
            m = madd_s[c]
            if near:
                back = nk - 1 - c
                carries = (back == 0) | ((back == 1) & (kcol > qrow))
                halves.append([m + jnp.where(carries, bias_ref[g * r + v], 0.0) for v in range(r)])
            else:
                halves.append([m] * r)
        return jnp.concatenate([jnp.concatenate([halves[0][v], halves[1][v]], axis=1) for v in range(r)], axis=0)

    ngh = lg_s.shape[0]
    for half in range(N_KV_HEADS // ngh):
        groups = [half * ngh + k for k in range(ngh)]
        mx_s[...] = jnp.full(mx_s.shape, NEG_BIG, F32)

        def logits_pair(c2, near):
            for k, g in enumerate(groups):
                lg = _dot(qa_s[g], kaug_ref[0, c2, g * LANE:(g + 1) * LANE, :]) + pair_mask(c2, near, g)
                lg_s[k, c2] = lg
                mx_s[k] = jnp.maximum(mx_s[k], jnp.maximum(lg[:, :tq], lg[:, tq:]))

        def far(c4, _):
            logits_pair(2 * c4, False)
            logits_pair(2 * c4 + 1, False)
            return 0

        nfar = jnp.maximum(nk2 - 2, 0)
        lax.fori_loop(0, nfar // 2, far, 0)

        @pl.when(nfar % 2 == 1)
        def _():
            logits_pair(nfar - 1, False)

        @pl.when(nk2 >= 2)
        def _():
            logits_pair(nk2 - 2, True)

        logits_pair(nk2 - 1, True)

        for k in range(ngh):
            sh_s[k] = jnp.broadcast_to(jnp.max(mx_s[k], axis=1, keepdims=True), (r * tq, LANE))
        acc_s[...] = jnp.zeros(acc_s.shape, F32)

        def pv_pair(c2):
            r0 = pl.multiple_of(c2 * 2 * tq, 2 * tq)
            va = vaug_ref[0, pl.ds(r0, 2 * tq), :]
            for k, g in enumerate(groups):
                s = sh_s[k]
                p = jnp.exp(lg_s[k, c2] - jnp.concatenate([s, s], axis=1))
                acc_s[k] = acc_s[k] + _dot(p.astype(BF16), va[:, g * LANE:(g + 1) * LANE])

        def pv_two(c4, _):
            pv_pair(2 * c4)
            pv_pair(2 * c4 + 1)
            return 0

        lax.fori_loop(0, nk2 // 2, pv_two, 0)

        @pl.when(nk2 % 2 == 1)
        def _():
            pv_pair(nk2 - 1)
        for k, g in enumerate(groups):
            acc = acc_s[k]
            o = acc[:, 0:HEAD_DIM] / acc[:, HEAD_DIM:HEAD_DIM + 1]
            heads = jnp.concatenate([o[u * tq:(u + 1) * tq, :] for u in range(r)], axis=1)
            o_ref[0, :, g * r * HEAD_DIM:(g + 1) * r * HEAD_DIM] = heads.astype(o_ref.dtype)


def _t5_bucket(dist):
    n = jnp.maximum(dist, 0)
    max_exact = N_BUCKETS // 2
    nf = jnp.maximum(n, max_exact).astype(F32)
    large = max_exact + (jnp.log(nf / max_exact) / math.log(MAX_DISTANCE / max_exact)
                         * (N_BUCKETS - max_exact)).astype(I32)
    large = jnp.minimum(large, N_BUCKETS - 1)
    return jnp.where(n < max_exact, n, large)


def _bias_table(rel_bias, n):
    tab = rel_bias[_t5_bucket(jnp.arange(n, dtype=I32))]
    return (tab - rel_bias[N_BUCKETS - 1][None, :]).T


DSA_GROUPS_PER_PASS = 2


def _dsa_prompt(q, qcat, kw, kcat, kaug, vaug, rel_bias, tq):
    bsz, seq, nq = q.shape
    topk = min(TOPK_MAX, seq // 4)
    nkc = seq // tq
    r = N_HEADS // N_KV_HEADS
    assert tq == MAX_DISTANCE and seq % (2 * tq) == 0
    tab = _bias_table(rel_bias, tq)
    dmod = (np.arange(tq)[:, None] - np.arange(tq)[None, :]) % tq
    onehot = (jnp.arange(tq, dtype=I32)[:, None, None] == jnp.asarray(dmod, I32)[None]).astype(F32)
    bias = jnp.einsum('hd,dqk->hqk', tab, onehot, precision=lax.Precision.HIGHEST)
    assert kcat.shape[1:] == (seq // (2 * tq), 4 * IDX_DIM, 2 * tq)
    row = lambda n: pl.BlockSpec((1, tq, n), lambda b, i: (b, i, 0))
    tiles = lambda a: pl.BlockSpec((1,) + a.shape[1:], lambda b, i: (b, 0, 0, 0), pipeline_mode=pl.Buffered(1))
    return pl.pallas_call(
        functools.partial(_dsa_prompt_kernel, tq=tq, topk=topk, idx_bits=int(math.log2(seq))),
        grid=(bsz, nkc),
        in_specs=[row(nq), row(qcat.shape[-1]), row(LANE), tiles(kcat), tiles(kaug),
                  pl.BlockSpec((1, seq, vaug.shape[-1]), lambda b, i: (b, 0, 0), pipeline_mode=pl.Buffered(1)),
                  _resident((N_HEADS, tq, tq))],
        out_specs=row(nq),
        out_shape=jax.ShapeDtypeStruct((bsz, seq, nq), BF16),
        scratch_shapes=[pltpu.VMEM((nkc, tq, tq), F32), pltpu.VMEM((nkc, tq, tq), F32),
                        pltpu.VMEM((IDX_HEADS, tq, tq), F32), pltpu.VMEM((tq, tq), F32),
                        pltpu.VMEM((N_KV_HEADS, r * tq, LANE), BF16),
                        pltpu.VMEM((DSA_GROUPS_PER_PASS, nkc // 2, r * tq, 2 * tq), F32),
                        pltpu.VMEM((DSA_GROUPS_PER_PASS, r * tq, tq), F32),
                        pltpu.VMEM((DSA_GROUPS_PER_PASS, r * tq, LANE), F32),
                        pltpu.VMEM((DSA_GROUPS_PER_PASS, r * tq, LANE), F32)],
        compiler_params=_cparams(2),
        name="dsa_prompt",
    )(q, qcat, kw, kcat, kaug, vaug, bias)


QP = SUBLANE
PAGES_PER_STEP = 64


def _page_specs(shape4, layer, pps):
    def spec(u):
        return pl.BlockSpec((1, 1) + tuple(shape4[2:]),
                            lambda b, g, pt: (layer, pt[b, g * pps + u], 0, 0))
    return [spec(u) for u in range(pps)]


def _pages_t(cache):
    nd = cache.ndim
    t = jnp.transpose(cache, (0, 1) + tuple(range(3, nd)) + (2,))
    return t.reshape(t.shape[0], t.shape[1], -1, t.shape[-1])


def _pad_t(a, page):
    return jnp.concatenate([a, jnp.zeros((page - QP, a.shape[1]), F32)], axis=0).T


def _dsa_sample_score_kernel(pt_ref, qcat_ref, kw_ref, kcn_ref, *rest, pps):
    page_refs, o_ref, on_ref = rest[:pps], rest[pps], rest[pps + 1]
    lhs = _stack_heads(qcat_ref[0]).astype(BF16)
    wcols = _idx_wcols(kw_ref[0])
    kt = jnp.concatenate([page_refs[u][0, 0] for u in range(pps)], axis=1)
    kh, kl = _split(kt)
    rhs = jnp.concatenate([kh, kh, kl, jnp.zeros(kh.shape, BF16)], axis=0)
    o_ref[0] = _idx_score(_dot(lhs, rhs), wcols, QP)

    @pl.when(pl.program_id(1) == pl.num_programs(1) - 1)
    def _():
        on_ref[0] = _idx_score(_dot(lhs, _pad_t(kcn_ref[0], on_ref.shape[-1]).astype(BF16)), wcols, QP)


def _dsa_sample_scores(page_table, qcat, kw, kcn, kidx_t, layer, pps):
    nb, n_pages = page_table.shape
    page = kidx_t.shape[-1]
    bmap = lambda b, g, pt: (b, 0, 0)
    grid_spec = pltpu.PrefetchScalarGridSpec(
        num_scalar_prefetch=1,
        grid=(nb, n_pages // pps),
        in_specs=[pl.BlockSpec((1, QP, qcat.shape[-1]), bmap), pl.BlockSpec((1, QP, LANE), bmap),
                  pl.BlockSpec((1, QP, kcn.shape[-1]), bmap)] + _page_specs(kidx_t.shape, layer, pps),
        out_specs=[pl.BlockSpec((1, QP, pps * page), lambda b, g, pt: (b, 0, g)),
                   pl.BlockSpec((1, QP, page), bmap)])
    return pl.pallas_call(
        functools.partial(_dsa_sample_score_kernel, pps=pps),
        grid_spec=grid_spec,
        out_shape=[jax.ShapeDtypeStruct((nb, QP, n_pages * page), F32),
                   jax.ShapeDtypeStruct((nb, QP, page), F32)],
        compiler_params=_cparams(2),
        name="dsa_sample_scores",
    )(page_table, qcat, kw, kcn, *([kidx_t] * pps))


def _new_token_scores(s_new, steps):
    lane = lax.broadcasted_iota(I32, s_new.shape, 1)
    q = lax.broadcasted_iota(I32, s_new.shape, 0) % QP
    return jnp.where((lane <= q) & (lane < steps), s_new, -jnp.inf)


def _dsa_sample_select_kernel(sc_ref, sn_ref, t_ref, j_ref, *, steps, topk, idx_bits):
    scores = jnp.concatenate([sc_ref[...], _new_token_scores(sn_ref[...], steps)], axis=1)
    idx = lax.broadcasted_iota(I32, scores.shape, 1)
    count = lambda fn: jnp.sum(fn(0, scores, idx), axis=1, keepdims=True)
    t, j = _topk_select(count, scores.shape[0], scores.shape[1], float(topk), idx_bits)
    t_ref[...] = t
    j_ref[...] = j


def _dsa_sample_select(scores, snew, steps, topk):
    nb, _, past = scores.shape
    page = snew.shape[-1]
    rows = nb * QP
    rb = math.gcd(rows, 256)
    t, j = pl.pallas_call(
        functools.partial(_dsa_sample_select_kernel, steps=steps, topk=topk,
                          idx_bits=int(math.ceil(math.log2(past + page)))),
        grid=(rows // rb,),
        in_specs=[pl.BlockSpec((rb, past), lambda i: (i, 0)), pl.BlockSpec((rb, page), lambda i: (i, 0))],
        out_specs=[pl.BlockSpec((rb, 1), lambda i: (i, 0)), pl.BlockSpec((rb, 1), lambda i: (i, 0))],
        out_shape=[jax.ShapeDtypeStruct((rows, 1), F32), jax.ShapeDtypeStruct((rows, 1), I32)],
        compiler_params=_cparams(1),
        name="dsa_sample_select",
    )(scores.reshape(rows, past), snew.reshape(rows, page))
    return t.reshape(nb, QP, 1), j.reshape(nb, QP, 1)


def _dsa_sample_attend_kernel(pt_ref, q_ref, kn_ref, vn_ref, sc_ref, sn_ref, t_ref, j_ref,
                              blast_ref, bnew_ref, *rest, pps, n_pages, steps):
    k_refs, v_refs, o_ref = rest[:pps], rest[pps:2 * pps], rest[2 * pps]
    lhs_s, m_s, l_s, acc_s = rest[2 * pps + 1:]
    g = pl.program_id(1)
    last = pl.num_programs(1) - 1
    page = k_refs[0].shape[-1]
    past = n_pages * page
    n = pps * page
    r = N_HEADS // N_KV_HEADS
    grow = r * QP

    @pl.when(g == 0)
    def _():
        m_s[...] = jnp.full(m_s.shape, NEG_BIG, F32)
        l_s[...] = jnp.zeros(l_s.shape, F32)
        acc_s[...] = jnp.zeros(acc_s.shape, F32)
        q = q_ref[0]
        for gg in range(N_KV_HEADS):
            lhs_s[gg] = jnp.concatenate(
                [q[:, (gg * r + u) * HEAD_DIM:(gg * r + u + 1) * HEAD_DIM] for u in range(r)], axis=0).astype(BF16)

    t, j = t_ref[0], j_ref[0]

    def select(sc, idx):
        sel = (sc > t) | ((sc == t) & (idx <= j))
        return jnp.where(sel & (sc > -jnp.inf), 0.0, NEG_BIG)

    def attend(kt, vt, madd, bias):
        mrows = jnp.concatenate([madd] * r, axis=0)
        for gg in range(N_KV_HEADS):
            rows = slice(gg * grow, (gg + 1) * grow)
            feat = slice(gg * HEAD_DIM, (gg + 1) * HEAD_DIM)
            lg = _dot(lhs_s[gg], kt[feat, :]) + mrows
            if bias is not None:
                far = lg.shape[1] - bias.shape[1]
                near = lg[:, far:] + bias[rows, :]
                lg = near if far == 0 else jnp.concatenate([lg[:, :far], near], axis=1)
            m_old = m_s[rows, :]
            m_new = jnp.maximum(m_old, jnp.max(lg, axis=1, keepdims=True))
            p = jnp.exp(lg - m_new)
            alpha = jnp.exp(m_old - m_new)
            l_s[rows, :] = alpha * l_s[rows, :] + jnp.sum(p, axis=1, keepdims=True)
            acc_s[rows, :] = alpha * acc_s[rows, :] + _dot_nt(p.astype(BF16), vt[feat, :])
            m_s[rows, :] = m_new

    def pages(bias):
        kt = jnp.concatenate([k_refs[u][0, 0] for u in range(pps)], axis=1).astype(BF16)
        vt = jnp.concatenate([v_refs[u][0, 0] for u in range(pps)], axis=1).astype(BF16)
        idx = g * n + lax.broadcasted_iota(I32, (QP, n), 1)
        attend(kt, vt, select(sc_ref[0], idx), bias)

    @pl.when(g != last)
    def _():
        pages(None)

    @pl.when(g == last)
    def _():
        pages(blast_ref[...])
        idx = past + lax.broadcasted_iota(I32, (QP, page), 1)
        attend(_pad_t(kn_ref[0], page).astype(BF16), _pad_t(vn_ref[0], page).astype(BF16),
               select(_new_token_scores(sn_ref[0], steps), idx), bnew_ref[...])
        o = acc_s[...] / l_s[...]
        for h in range(N_HEADS):
            o_ref[0, :, h * HEAD_DIM:(h + 1) * HEAD_DIM] = o[h * QP:(h + 1) * QP, :]


def _dsa_sample(page_table, q, kn, vn, scores, snew, k_t, v_t, rel_bias, layer, steps, pps):
    nb, n_pages = page_table.shape
    page = k_t.shape[-1]
    past = n_pages * page
    nq, nkv = N_HEADS * HEAD_DIM, N_KV_HEADS * HEAD_DIM
    t, j = _dsa_sample_select(scores, snew, steps, min(TOPK_MAX, (past + steps) // 4))
    n_steps = n_pages // pps
    r = N_HEADS // N_KV_HEADS
    tab = _bias_table(rel_bias, 2 * page)
    qq = np.arange(QP)[:, None]
    off = np.arange(page)[None, :]
    blast = tab[:, page + qq - off].reshape(N_HEADS * QP, page)
    bnew = tab[:, np.maximum(qq - off, 0)].reshape(N_HEADS * QP, page)
    bmap = lambda b, g, pt: (b, 0, 0)
    grid_spec = pltpu.PrefetchScalarGridSpec(
        num_scalar_prefetch=1,
        grid=(nb, n_steps),
        in_specs=[pl.BlockSpec((1, QP, nq), bmap),
                  pl.BlockSpec((1, QP, nkv), bmap),
                  pl.BlockSpec((1, QP, nkv), bmap),
                  pl.BlockSpec((1, QP, pps * page), lambda b, g, pt: (b, 0, g)),
                  pl.BlockSpec((1, QP, page), bmap),
                  pl.BlockSpec((1, QP, 1), bmap), pl.BlockSpec((1, QP, 1), bmap),
                  pl.BlockSpec(blast.shape, lambda b, g, pt: (0, 0)),
                  pl.BlockSpec(bnew.shape, lambda b, g, pt: (0, 0))]
                 + _page_specs(k_t.shape, layer, pps) + _page_specs(v_t.shape, layer, pps),
        out_specs=pl.BlockSpec((1, QP, nq), bmap),
        scratch_shapes=[pltpu.VMEM((N_KV_HEADS, r * QP, HEAD_DIM), BF16),
                        pltpu.VMEM((N_HEADS * QP, 1), F32), pltpu.VMEM((N_HEADS * QP, 1), F32),
                        pltpu.VMEM((N_HEADS * QP, HEAD_DIM), F32)])
    return pl.pallas_call(
        functools.partial(_dsa_sample_attend_kernel, pps=pps, n_pages=n_pages, steps=steps),
        grid_spec=grid_spec,
        out_shape=jax.ShapeDtypeStruct((nb, QP, nq), F32),
        compiler_params=_cparams(2),
        name="dsa_sample_attend",
    )(page_table, q, kn, vn, scores, snew, t, j, blast, bnew, *([k_t] * pps), *([v_t] * pps))


SSD_COLS = 512


def _softplus(x):
    return jnp.maximum(x, 0.0) + jnp.log1p(jnp.exp(-jnp.abs(x)))


def _split3(a):
    p1 = a.astype(BF16)
    r1 = a - p1.astype(F32)
    p2 = r1.astype(BF16)
    return p1, p2, (r1 - p2.astype(F32)).astype(BF16)


def _dot_sel_rhs(a, e):
    p1, p2, p3 = _split3(a)
    return _dot(p1, e) + (_dot(p2, e) + _dot(p3, e))


def _dot_sel_lhs(e, a):
    p1, p2, p3 = _split3(a)
    return _dot(e, p1) + (_dot(e, p2) + _dot(e, p3))


def _ssd_prompt_kernel(z_ref, xbc_ref, dtr_ref, cw_ref, cb_ref, dtb_ref, alog_ref, dexp_ref, ng_ref,
                       e_ref, y_ref, nbuf_ref, hf_ref, prev_s, xc_s, y_s, h_s, *, width, nh, inner):
    i = pl.program_id(1)
    last = pl.num_programs(1) - 1
    qn, cdim = xbc_ref.shape[1], xbc_ref.shape[2]
    ngrp, ns, hd = SSD_GROUPS, SSD_STATE, SSD_HEADDIM
    hpg = nh // ngrp

    @pl.when(i == 0)
    def _():
        prev_s[...] = jnp.zeros(prev_s.shape, F32)
        h_s[...] = jnp.zeros(h_s.shape, F32)

    for cb in range(cdim // SSD_COLS):
        cols = slice(cb * SSD_COLS, (cb + 1) * SSD_COLS)
        xr = xbc_ref[0, :, cols]
        prev = prev_s[:, cols]
        conv = xr * cw_ref[width - 1:width, cols]
        for k in range(1, width):
            conv = conv + _shift_rows(xr, prev, k) * cw_ref[width - 1 - k:width - k, cols]
        xc_s[:, cols] = _silu(conv + cb_ref[:, cols])
        prev_s[:, cols] = xr[qn - SUBLANE:, :]
        nbuf_ref[0, :, cols] = xr[qn - (width - 1):, :]

    hl = lax.broadcasted_iota(I32, (qn, LANE), 1)
    dt = jnp.where(hl < nh, _softplus(dtr_ref[0] + dtb_ref[...]), 0.0)
    dta = dt * (-jnp.exp(alog_ref[...]))
    qrow = lax.broadcasted_iota(I32, (qn, qn), 0)
    kcol = lax.broadcasted_iota(I32, (qn, qn), 1)
    causal = kcol <= qrow
    cum = _dot_sel_lhs(jnp.where(causal, 1.0, 0.0).astype(BF16), dta)
    cum2 = cum * LOG2E
    cum2_t = cum2.T
    cum_last = cum[qn - 1:qn, :]
    ecum = jnp.exp(cum)
    dtw = dt * jnp.exp(cum_last - cum)
    e_last = jnp.exp(cum_last)
    lane_p = lax.broadcasted_iota(I32, (qn, LANE), 1)
    row_p = lax.broadcasted_iota(I32, (LANE, ns), 0)

    for g in range(ngrp):
        gl = slice(g * SSD_COLS, (g + 1) * SSD_COLS)
        e_g = e_ref[:, gl]
        xs_g = xc_s[:, gl]
        dtx_g = xs_g * _dot_sel_rhs(dt, e_g)
        dtxw_g = xs_g * _dot_sel_rhs(dtw, e_g)
        ecx_g = _dot_sel_rhs(ecum, e_g)
        bm = xc_s[:, inner + g * ns:inner + (g + 1) * ns].astype(BF16)
        cm = xc_s[:, inner + (ngrp + g) * ns:inner + (ngrp + g + 1) * ns].astype(BF16)
        cbm = _dot_nt(cm, bm)
        for pi in range(hpg // 2):
            ha = g * hpg + 2 * pi
            pls = slice(pi * LANE, (pi + 1) * LANE)
            lanes = slice(ha * hd, ha * hd + LANE)
            dtx_p = dtx_g[:, pls].astype(BF16)
            yds = []
            for h in (ha, ha + 1):
                seg = cum2[:, h:h + 1] - cum2_t[h:h + 1, :]
                dec = jnp.exp2(jnp.where(causal, seg, -jnp.inf))
                yds.append(_dot((cbm * dec).astype(BF16), dtx_p))
            yd = jnp.where(lane_p < hd, yds[0], yds[1])
            hp = h_s[ha // 2]
            yo = _dot_nt(cm, hp.astype(BF16)) * ecx_g[:, pls]
            y_s[:, lanes] = yd + yo + dexp_ref[:, lanes] * xs_g[:, pls]
            s_new = _dot(dtxw_g[:, pls].T.astype(BF16), bm)
            cd = jnp.where(row_p < hd, e_last[:, ha:ha + 1], e_last[:, ha + 1:ha + 2])
            h_s[ha // 2] = hp * cd + s_new

    for g in range(ngrp):
        gl = slice(g * SSD_COLS, (g + 1) * SSD_COLS)
        yg = y_s[:, gl] * _silu(z_ref[0, :, gl])
        ms = jnp.mean(yg * yg, axis=-1, keepdims=True)
        y_ref[0, :, gl] = ((yg * lax.rsqrt(ms + EPS)) * ng_ref[:, gl]).astype(y_ref.dtype)

    @pl.when(i == last)
    def _():
        hf_ref[0] = h_s[...]


def _head_expand(nh, hd):
    e = np.zeros((LANE, nh * hd), np.float32)
    for h in range(nh):
        e[h, h * hd:(h + 1) * hd] = 1.0
    return jnp.asarray(e, BF16)


def _pad_lanes(v, n=LANE):
    v = v.reshape(1, -1)
    return jnp.pad(v, ((0, 0), (0, n - v.shape[1])))


def _ssd_prompt(z, xbc, dtr, conv_w, conv_b, dt_bias, a_log, d_skip, norm_g):
    bsz, seq, inner = z.shape
    cdim = xbc.shape[-1]
    nh = dt_bias.shape[0]
    width = conv_w.shape[0]
    qn = math.gcd(seq, SSD_CHUNK)
    assert qn == SSD_CHUNK and inner == SSD_GROUPS * SSD_COLS and cdim % SSD_COLS == 0
    dexp = jnp.repeat(d_skip, SSD_HEADDIM).reshape(1, inner)
    blk = lambda n: pl.BlockSpec((1, qn, n), lambda b, i: (b, i, 0))
    y, nbuf, hf = pl.pallas_call(
        functools.partial(_ssd_prompt_kernel, width=width, nh=nh, inner=inner),
        grid=(bsz, seq // qn),
        in_specs=[blk(inner), blk(cdim), blk(LANE),
                  _const2((width, cdim)), _const2((1, cdim)), _const2((1, LANE)), _const2((1, LANE)),
                  _const2((1, inner)), _const2((1, inner)), _const2((LANE, inner))],
        out_specs=[blk(inner),
                   pl.BlockSpec((1, width - 1, cdim), lambda b, i: (b, 0, 0)),
                   pl.BlockSpec((1, nh // 2, LANE, SSD_STATE), lambda b, i: (b, 0, 0, 0))],
        out_shape=[jax.ShapeDtypeStruct((bsz, seq, inner), BF16),
                   jax.ShapeDtypeStruct((bsz, width - 1, cdim), F32),
                   jax.ShapeDtypeStruct((bsz, nh // 2, LANE, SSD_STATE), F32)],
        scratch_shapes=[pltpu.VMEM((SUBLANE, cdim), F32), pltpu.VMEM((qn, cdim), F32),
                        pltpu.VMEM((qn, inner), F32), pltpu.VMEM((nh // 2, LANE, SSD_STATE), F32)],
        compiler_params=_cparams(2),
        name="ssd_prompt",
    )(z, xbc, dtr, conv_w, conv_b.reshape(1, cdim), _pad_lanes(dt_bias), _pad_lanes(a_log),
      dexp, norm_g.reshape(1, inner), _head_expand(nh, SSD_HEADDIM))
    return y, nbuf, hf.reshape(bsz, nh, SSD_HEADDIM, SSD_STATE)


def _ssd_prep_sample_kernel(xbc_ref, buf_ref, dtr_ref, cw_ref, cb_ref, dtb_ref, xc_ref, dt_ref, nbuf_ref,
                            *, width, nb, steps, nh):
    cdim = xbc_ref.shape[1]
    for cb in range(cdim // SSD_COLS):
        cols = slice(cb * SSD_COLS, (cb + 1) * SSD_COLS)
        ext = jnp.concatenate([buf_ref[:, cols], xbc_ref[:, cols]], axis=0)
        conv = ext[0:steps * nb] * cw_ref[0:1, cols]
        for k in range(1, width):
            conv = conv + ext[k * nb:(k + steps) * nb] * cw_ref[k:k + 1, cols]
        xc_ref[:, cols] = _silu(conv + cb_ref[:, cols])
        nbuf_ref[:, cols] = ext[steps * nb:]
    hl = lax.broadcasted_iota(I32, dtr_ref.shape, 1)
    dt_ref[...] = jnp.where(hl < nh, _softplus(dtr_ref[...] + dtb_ref[...]), 0.0)


def _ssd_prep_sample(xbc_tb, buf_kb, dtr_tb, conv_w, conv_b, dt_bias, nb, steps):
    rows, cdim = xbc_tb.shape
    width = conv_w.shape[0]
    return pl.pallas_call(
        functools.partial(_ssd_prep_sample_kernel, width=width, nb=nb, steps=steps, nh=dt_bias.shape[0]),
        out_shape=[jax.ShapeDtypeStruct((rows, cdim), F32),
                   jax.ShapeDtypeStruct((rows, LANE), F32),
                   jax.ShapeDtypeStruct(((width - 1) * nb, cdim), F32)],
        compiler_params=pltpu.CompilerParams(vmem_limit_bytes=VMEM_LIMIT),
        name="ssd_prep_sample",
    )(xbc_tb, buf_kb, dtr_tb, conv_w, conv_b.reshape(1, cdim), _pad_lanes(dt_bias))


def _ssd_scan_sample_kernel(xc_ref, dt_ref, z_ref, h0_ref, alog_ref, dexp_ref, ng_ref, e_ref,
                            y_ref, hf_ref, *, steps, inner):
    ngrp, ns = SSD_GROUPS, SSD_STATE
    xc, dt = xc_ref[0], dt_ref[0]
    row = lax.broadcasted_iota(I32, (QP, LANE), 0)
    cum = dt * (-jnp.exp(alog_ref[...]))
    for sft in (1, 2, 4):
        cum = cum + jnp.where(row >= sft, pltpu.roll(cum, sft, 0), 0.0)
    cum_last = cum[QP - 1:QP, :]
    parts = [dt, jnp.exp(cum), dt * jnp.exp(cum_last - cum), jnp.broadcast_to(jnp.exp(cum_last), (QP, LANE))]
    for s in range(steps):
        parts.append(jnp.where(row >= s, jnp.exp(cum - cum[s:s + 1, :]), 0.0))
    stack = jnp.concatenate(parts, axis=0)

    for g in range(ngrp):
        gl = slice(g * SSD_COLS, (g + 1) * SSD_COLS)
        ex = _dot_sel_rhs(stack, e_ref[:, gl])
        dt_x, ecum_x, dtw_x, el_x = (ex[k * QP:(k + 1) * QP] for k in range(4))
        xs_g = xc[:, gl]
        dtx = xs_g * dt_x
        bm = xc[:, inner + g * ns:inner + (g + 1) * ns]
        cm = xc[:, inner + (ngrp + g) * ns:inner + (ngrp + g + 1) * ns].astype(BF16)
        bmp = jnp.concatenate([bm, jnp.zeros((LANE - QP, ns), F32)], axis=0).astype(BF16)
        cbm = _dot_nt(cm, bmp)
        yd = jnp.zeros((QP, SSD_COLS), F32)
        for s in range(steps):
            yd = yd + (ex[(4 + s) * QP:(5 + s) * QP] * cbm[:, s:s + 1]) * dtx[s:s + 1, :]
        h0g = h0_ref[0, gl, :]
        yo = _dot_nt(cm, h0g.astype(BF16)) * ecum_x
        y = (yd + yo + dexp_ref[:, gl] * xs_g) * _silu(z_ref[0, :, gl])
        ms = jnp.mean(y * y, axis=-1, keepdims=True)
        y_ref[0, :, gl] = (y * lax.rsqrt(ms + EPS)) * ng_ref[:, gl]
        tm = jnp.concatenate([xs_g * dtw_x, el_x[0:1], jnp.zeros((LANE - QP - 1, SSD_COLS), F32)], axis=0)
        tt = tm.T
        hf_ref[0, gl, :] = h0g * tt[:, QP:QP + 1] + _dot(tt.astype(BF16), bmp)


def _ssd_scan_sample(xc, dt, z, h0, a_log, d_skip, norm_g, steps):
    nb, _, cdim = xc.shape
    inner = z.shape[-1]
    nh = a_log.shape[0]
    rows_h = nh * SSD_HEADDIM
    dexp = jnp.repeat(d_skip, SSD_HEADDIM).reshape(1, inner)
    one = lambda shape: pl.BlockSpec(shape, lambda b: (0,) * len(shape))
    return pl.pallas_call(
        functools.partial(_ssd_scan_sample_kernel, steps=steps, inner=inner),
        grid=(nb,),
        in_specs=[pl.BlockSpec((1, QP, cdim), lambda b: (b, 0, 0)),
                  pl.BlockSpec((1, QP, LANE), lambda b: (b, 0, 0)),
                  pl.BlockSpec((1, QP, inner), lambda b: (b, 0, 0)),
                  pl.BlockSpec((1, rows_h, SSD_STATE), lambda b: (b, 0, 0)),
                  one((1, LANE)), one((1, inner)), one((1, inner)), one((LANE, inner))],
        out_specs=[pl.BlockSpec((1, QP, inner), lambda b: (b, 0, 0)),
                   pl.BlockSpec((1, rows_h, SSD_STATE), lambda b: (b, 0, 0))],
        out_shape=[jax.ShapeDtypeStruct((nb, QP, inner), F32),
                   jax.ShapeDtypeStruct((nb, rows_h, SSD_STATE), F32)],
        compiler_params=_cparams(1),
        name="ssd_scan_sample",
    )(xc, dt, z, h0, _pad_lanes(a_log), dexp, norm_g.reshape(1, inner), _head_expand(nh, SSD_HEADDIM))


TM = 512


def _to_steps(a, nb, steps):
    n = a.shape[-1]
    return a.reshape(nb, steps, n).transpose(1, 0, 2).reshape(steps * nb, n)


def _to_batch(a, nb, steps):
    n = a.shape[-1]
    return a.reshape(steps, nb, n).transpose(1, 0, 2).reshape(1, nb * steps, n)


def _pad_q(a, nb, steps):
    n = a.shape[-1]
    return jnp.pad(a.reshape(nb, steps, n), ((0, 0), (0, QP - steps), (0, 0)))


def kernel(x_prompt, x_sample, cache_k, cache_v, cache_kidx, state_s5_re, state_s5_im, state_sconv, state_ssd, state_ssd_conv, page_table, c_prompt, c_sample, rel_bias, ada_w, ada_b, norm_mix, norm_mlp, norm_final, attn_w_in, attn_w_out, s5_lam_re, s5_lam_im, s5_log_dt, s5_b_re, s5_b_im, s5_c_re, s5_c_im, s5_d, s5_w_glu, sc_w_in, sc_w_conv, sc_w_out, ssd_w_in, ssd_conv_w, ssd_conv_b, ssd_dt_bias, ssd_a_log, ssd_d, ssd_norm, ssd_w_out, mlp_w1, mlp_w2):
    bp, seq, d = x_prompt.shape
    nb, steps, _ = x_sample.shape
    depth = ada_w.shape[0]
    n_mixers = 4
    rs = nb * steps
    tm = min(TM, seq)

    rows = bp + nb
    c_all = jnp.pad(jnp.concatenate([c_prompt, c_sample], axis=0), ((0, (-rows) % SUBLANE), (0, 0)))
    ada = _ada(c_all, ada_w, ada_b)

    xp = x_prompt
    xs = x_sample.reshape(1, rs, d)
    outs = {name: [] for name in ("kp", "vp", "kip", "ks", "vs", "kis", "s5pr", "s5pi", "s5sr", "s5si",
                                  "scp", "scs", "ssdp", "ssdcp", "ssds", "ssdcs")}
    nq, nkv = N_HEADS * HEAD_DIM, N_KV_HEADS * HEAD_DIM
    w1_all, w2_all = mlp_w1.astype(BF16), mlp_w2.astype(BF16)
    for i in range(depth):
        m, j = i % n_mixers, i // n_mixers
        mp = [ada[i, :bp, k * d:(k + 1) * d].reshape(bp, 1, d) for k in range(6)]
        ms = [jnp.repeat(ada[i, bp:bp + nb, k * d:(k + 1) * d], steps, axis=0).reshape(1, rs, d)
              for k in range(6)]
        g_mix = norm_mix[i]
        mix_p = mix_s = None
        if m == 0:
            w_in = attn_w_in[j]
            w_qkv = w_in[:, :nq + 2 * nkv].astype(BF16)
            n_idx = IDX_HEADS * IDX_DIM + LANE
            w_idx = jnp.pad(w_in[:, nq + 2 * nkv:], ((0, 0), (0, n_idx - (w_in.shape[1] - nq - 2 * nkv))))
            w_out = attn_w_out[j].astype(BF16)
            page = cache_k.shape[2]
            q, k, v, kw, qcat, kcat, kaug, vaug, ki = _attn_proj(xp, mp[0], mp[1], g_mix, w_qkv, w_idx,
                                                                 max(tm, 2 * MAX_DISTANCE), True, page,
                                                                 2 * MAX_DISTANCE)
            o = _dsa_prompt(q, qcat, kw, kcat, kaug, vaug, rel_bias, MAX_DISTANCE)
            mix_p = ("proj", o, mp[2], w_out)
            to_pages = lambda t: jnp.transpose(t.reshape(bp, seq // page, N_KV_HEADS, HEAD_DIM, page), (0, 1, 4, 2, 3))
            outs["kp"].append(to_pages(k))
            outs["vp"].append(to_pages(v))
            outs["kip"].append(jnp.transpose(ki, (0, 1, 3, 2)))
            q, k, v, kw, qcat, kcat, _, _ = _attn_proj(xs, ms[0], ms[1], g_mix, w_qkv, w_idx, rs, False)
            pps = math.gcd(PAGES_PER_STEP, page_table.shape[1])
            qc_p, kw_p = _pad_q(qcat.astype(F32), nb, steps), _pad_q(kw, nb, steps)
            scores, snew = _dsa_sample_scores(page_table, qc_p, kw_p, _pad_q(kcat.astype(F32), nb, steps),
                                              _pages_t(cache_kidx), j,
                                              math.gcd(2 * PAGES_PER_STEP, page_table.shape[1]))
            o = _dsa_sample(page_table, _pad_q(q, nb, steps), _pad_q(k, nb, steps), _pad_q(v, nb, steps),
                            scores, snew, _pages_t(cache_k), _pages_t(cache_v), rel_bias, j, steps, pps)
            mix_s = ("proj", o[:, :steps].reshape(1, rs, nq), ms[2], w_out)
            outs["ks"].append(k.reshape(nb, steps, N_KV_HEADS, HEAD_DIM))
            outs["vs"].append(v.reshape(nb, steps, N_KV_HEADS, HEAD_DIM))
            outs["kis"].append(kw[..., :IDX_DIM].reshape(nb, steps, IDX_DIM))
        elif m == 1:
            ar, ai, wb, wc = _s5_weights(s5_lam_re[j], s5_lam_im[j], s5_log_dt[j], s5_b_re[j], s5_b_im[j],
                                         s5_c_re[j], s5_c_im[j])
            w_glu = s5_w_glu[j].astype(BF16)
            grp, nst = s5_lam_re.shape[1:]
            z, fr, fi = _s5_prompt(xp, mp[0], mp[1], g_mix, s5_d[j], ar, ai, wb, wc)
            mix_p = ("glu", z, mp[2], w_glu)
            outs["s5pr"].append(fr.reshape(bp, grp, nst))
            outs["s5pi"].append(fi.reshape(bp, grp, nst))
            z, fr, fi = _s5_sample(_to_steps(xs, nb, steps), _to_steps(ms[0], nb, steps),
                                   _to_steps(ms[1], nb, steps), g_mix, s5_d[j], ar, ai, wb, wc,
                                   state_s5_re[j].reshape(nb, grp * nst), state_s5_im[j].reshape(nb, grp * nst),
                                   nb, steps)
            mix_s = ("glu", _to_batch(z, nb, steps), ms[2], w_glu)
            outs["s5sr"].append(fr.reshape(nb, grp, nst))
            outs["s5si"].append(fi.reshape(nb, grp, nst))
        elif m == 2:
            w_in = sc_w_in[j].astype(BF16)
            w_out = sc_w_out[j].astype(BF16)
            width = sc_w_conv.shape[1]
            xp, nbuf = _sconv_prompt(xp, mp[0], mp[1], g_mix, w_in, mp[2], jnp.zeros((bp, width - 1, d), F32),
                                     sc_w_conv[j], w_out, tm)
            outs["scp"].append(nbuf)
            (p,) = _proj(xs, ms[0], ms[1], g_mix, w_in, (3 * d,), rs)
            buf = state_sconv[j].transpose(1, 0, 2).reshape((width - 1) * nb, d)
            y, nbuf = _sconv_sample(_to_steps(p, nb, steps), _to_steps(xs, nb, steps), _to_steps(ms[2], nb, steps),
                                    buf, sc_w_conv[j], w_out, nb, steps)
            xs = _to_batch(y, nb, steps)
            outs["scs"].append(nbuf.reshape(width - 1, nb, d).transpose(1, 0, 2))
        else:
            inner = ssd_norm.shape[1]
            cdim = ssd_conv_w.shape[2]
            nh = ssd_dt_bias.shape[1]
            width = ssd_conv_w.shape[1]
            w_in = jnp.pad(ssd_w_in[j], ((0, 0), (0, LANE - nh))).astype(BF16)
            w_out = ssd_w_out[j].astype(BF16)
            z, xbc, dtr = _proj(xp, mp[0], mp[1], g_mix, w_in, (inner, cdim, LANE), tm)
            y, nbuf, hf = _ssd_prompt(z, xbc, dtr, ssd_conv_w[j], ssd_conv_b[j], ssd_dt_bias[j], ssd_a_log[j],
                                      ssd_d[j], ssd_norm[j])
            mix_p = ("proj", y, mp[2], w_out)
            outs["ssdp"].append(hf)
            outs["ssdcp"].append(nbuf)
            z, xbc, dtr = _proj(xs, ms[0], ms[1], g_mix, w_in, (inner, cdim, LANE), rs)
            buf = state_ssd_conv[j].transpose(1, 0, 2).reshape((width - 1) * nb, cdim)
            xc, dt, nbuf = _ssd_prep_sample(_to_steps(xbc, nb, steps), buf, _to_steps(dtr, nb, steps),
                                            ssd_conv_w[j], ssd_conv_b[j], ssd_dt_bias[j], nb, steps)
            y, hf = _ssd_scan_sample(_pad_q(_to_batch(xc, nb, steps), nb, steps),
                                     _pad_q(_to_batch(dt, nb, steps), nb, steps), _pad_q(z, nb, steps),
                                     state_ssd[j].reshape(nb, nh * SSD_HEADDIM, SSD_STATE),
                                     ssd_a_log[j], ssd_d[j], ssd_norm[j], steps)
            mix_s = ("proj", y[:, :steps].reshape(1, rs, inner), ms[2], w_out)
            outs["ssds"].append(hf.reshape(nb, nh, SSD_HEADDIM, SSD_STATE))
            outs["ssdcs"].append(nbuf.reshape(width - 1, nb, cdim).transpose(1, 0, 2))
        fin = i == depth - 1
        xp = _mlp(xp, mp[3], mp[4], mp[5], norm_mlp[i], w1_all, w2_all, i, norm_final, fin, tm, mix_p)
        xs = _mlp(xs, ms[3], ms[4], ms[5], norm_mlp[i], w1_all, w2_all, i, norm_final, fin, rs, mix_s)
    st = jnp.stack
    return (xp, xs.reshape(nb, steps, d), st(outs["kp"]), st(outs["vp"]), st(outs["kip"]),
            st(outs["ks"]), st(outs["vs"]), st(outs["kis"]), st(outs["s5pr"]), st(outs["s5pi"]),
            st(outs["s5sr"]), st(outs["s5si"]), st(outs["scp"]), st(outs["scs"]),
            st(outs["ssdp"]), st(outs["ssdcp"]), st(outs["ssds"]), st(outs["ssdcs"]))
```
